```python
import math
import jax
import jax.numpy as jnp
from jax import lax
import numpy as np

D_MODEL = 1024
BATCH = 2
SEQ = 8192
DEPTH = 2

GRID_W = 64
CTX_LEN = 256
N_EVEN = (DEPTH + 1) // 2
N_ODD = DEPTH // 2
NORM_EPS = 1e-6

SSD_HEADS = 16
SSD_HEAD_DIM = 64
SSD_INNER = SSD_HEADS * SSD_HEAD_DIM
SSD_GROUPS = 2
SSD_STATE = 128
SSD_CONV = 3
SSD_CHUNK = 128
SSD_CONV_CH = SSD_INNER + 2 * SSD_GROUPS * SSD_STATE

HY_WIDTH = 1024
HY_ORDER = 2
HY_SHORT = 3
HY_EMB = 33
HY_HIDDEN = 64
HY_FAST_DECAY = 0.3
HY_SLOW_DECAY = 1.5
HY_TARGET = 1e-2

EV_IN = SSD_INNER + SSD_CONV_CH + 2 * SSD_HEADS + (HY_ORDER + 1) * HY_WIDTH
EV_MIX = SSD_INNER + HY_WIDTH

RW_HEADS = 16
RW_HEAD = 64
RW_WIDTH = RW_HEADS * RW_HEAD
RW_DECAY_LORA = 64
RW_AAA_LORA = 64
RW_GATE_LORA = 160
RW_CODE = 3 * RW_WIDTH + RW_DECAY_LORA + RW_AAA_LORA + RW_GATE_LORA
RW_GN_EPS = 64e-5

AT_Q_HEADS = 16
AT_KV_HEADS = 4
AT_HEAD = 64
AT_GQA = AT_Q_HEADS // AT_KV_HEADS
AT_BLOCK = 128
ROPE_AXIS = AT_HEAD // 2
ROPE_THETA = 10000.0

OD_IN = RW_CODE + (AT_Q_HEADS + 2 * AT_KV_HEADS) * AT_HEAD
OD_MIX = RW_WIDTH + AT_Q_HEADS * AT_HEAD

N_EXPERTS = 64
TOP_K = 8
N_GROUPS = 8
TOPK_GROUPS = 4
EXPERT_FF = 256
SHARED_FF = 256
ROUTED_SCALE = 2.5
MOE_BLOCK = 128

kernel_name = 'hybrid_ssd_hyena_rwkv7_gqa_moe_prefix_dit'


def rms_norm(x, g, eps=NORM_EPS):
    xf = x.astype(jnp.float32)
    y = xf * lax.rsqrt(jnp.mean(xf * xf, axis=-1, keepdims=True) + eps)
    return (y * g.astype(jnp.float32)).astype(x.dtype)


def dwconv_centred(x, w, b):
    ch, width = w.shape
    rhs = jnp.transpose(w).reshape(width, 1, ch).astype(x.dtype)
    y = lax.conv_general_dilated(x, rhs, window_strides=(1,), padding=[(width // 2, width // 2)],
                                 dimension_numbers=('NWC', 'WIO', 'NWC'), feature_group_count=ch)
    return y + b.astype(x.dtype)


def centred_shift_mix(x, mu):
    xp = jnp.pad(x, ((0, 0), (1, 1), (0, 0)))
    return x + mu * (0.5 * (xp[:, :-2] + xp[:, 2:]) - x)


def swiglu(h, w1, w3, w2):
    return (jax.nn.silu(h @ w1) * (h @ w3)) @ w2


def segsum(a):
    T = a.shape[-1]
    ar = jnp.broadcast_to(a[..., :, None], a.shape + (T,))
    strict = jnp.tril(jnp.ones((T, T), bool), -1)
    cs = jnp.cumsum(jnp.where(strict, ar, 0.0), axis=-2)
    return jnp.where(jnp.tril(jnp.ones((T, T), bool)), cs, -jnp.inf)


def ssd_chunked(X, A, Bm, Cm, init):
    n, L, G, R, P = X.shape
    S = Bm.shape[-1]
    Q = SSD_CHUNK
    nc = L // Q
    X = X.reshape(n, nc, Q, G, R, P)
    Bm = Bm.reshape(n, nc, Q, G, S)
    Cm = Cm.reshape(n, nc, Q, G, S)
    A = A.reshape(n, nc, Q, G, R).transpose(0, 3, 4, 1, 2)
    A_cs = jnp.cumsum(A, axis=-1)
    Lmat = jnp.exp(segsum(A))
    CB = jnp.einsum('nclgs,ncmgs->ncglm', Cm, Bm)
    y_diag = jnp.einsum('ncglm,ngrclm,ncmgrp->nclgrp', CB, Lmat, X)
    decay_states = jnp.exp(A_cs[..., -1:] - A_cs)
    states = jnp.einsum('nclgs,ngrcl,nclgrp->ncgrps', Bm, decay_states, X)
    states = jnp.concatenate([init[:, None], states], axis=1)
    chunk_cs = jnp.pad(A_cs[..., -1], ((0, 0), (0, 0), (0, 0), (1, 0)))
    decay_chunk = jnp.exp(segsum(chunk_cs))
    new_states = jnp.einsum('ngrzc,ncgrps->nzgrps', decay_chunk, states)
    prev_states, final = new_states[:, :-1], new_states[:, -1]
    y_off = jnp.einsum('nclgs,ncgrps,ngrcl->nclgrp', Cm, prev_states, jnp.exp(A_cs))
    return (y_diag + y_off).reshape(n, L, G, R, P), final


def ssd_branch(z, xbc, dt_raw, p, init):
    Bsz, L, _ = z.shape
    G, R, P, S = SSD_GROUPS, SSD_HEADS // SSD_GROUPS, SSD_HEAD_DIM, SSD_STATE
    xbc = xbc.astype(jnp.float32)
    xs = xbc[..., :SSD_INNER].reshape(Bsz, L, G, R, P)
    bm = xbc[..., SSD_INNER:SSD_INNER + G * S].reshape(Bsz, L, G, S)
    cm = xbc[..., SSD_INNER + G * S:].reshape(Bsz, L, G, S)
    dt = jax.nn.softplus(dt_raw.astype(jnp.float32).reshape(Bsz, L, 2, SSD_HEADS)
                         + p['ssd_dt_bias'].astype(jnp.float32)).reshape(Bsz, L, 2, G, R)
    a = -jnp.exp(p['ssd_a_log'].astype(jnp.float32)).reshape(2, G, R)

    def both(fwd, bwd):
        return jnp.concatenate([fwd, jnp.flip(bwd, axis=1)], axis=0)

    X = both(xs * dt[:, :, 0][..., None], xs * dt[:, :, 1][..., None])
    Adt = both(dt[:, :, 0] * a[0], dt[:, :, 1] * a[1])
    y, final = ssd_chunked(X, Adt, both(bm, bm), both(cm, cm), init)
    y = y[:Bsz] + jnp.flip(y[Bsz:], axis=1) + xs * p['ssd_d'].astype(jnp.float32).reshape(G, R, 1)
    gs = SSD_INNER // G
    y = y.reshape(Bsz, L, G, gs) * jax.nn.silu(z.astype(jnp.float32)).reshape(Bsz, L, G, gs)
    y = y * lax.rsqrt(jnp.mean(y * y, axis=-1, keepdims=True) + NORM_EPS)
    y = y.reshape(Bsz, L, SSD_INNER) * p['ssd_norm_w'].astype(jnp.float32)
    return y.astype(z.dtype), final


def hyena_filters(L, p):
    f32 = jnp.float32
    pos = jnp.arange(L, dtype=f32)
    t = pos / (L - 1)
    bands = (HY_EMB - 1) // 2
    freqs = jnp.linspace(1e-4, bands - 1, bands, dtype=f32)
    ang = (2.0 * math.pi / L) * pos[:, None] * freqs[None, :]
    feats = jnp.concatenate([t[:, None], jnp.cos(ang), -jnp.sin(ang)], axis=-1)
    h = jnp.sin(p['hy_freq0'] * (feats @ p['hy_mlp_w0'] + p['hy_mlp_b0']))
    h = jnp.sin(p['hy_freq1'] * (h @ p['hy_mlp_w1'] + p['hy_mlp_b1']))
    h = (h @ p['hy_mlp_w2']).astype(f32).reshape(L, HY_ORDER, 2, HY_WIDTH)
    min_decay = math.log(HY_TARGET) / HY_SLOW_DECAY
    max_decay = math.log(HY_TARGET) / HY_FAST_DECAY
    deltas = jnp.abs(jnp.linspace(min_decay, max_decay, HY_WIDTH, dtype=f32))
    h = h * jnp.exp(-t[:, None, None, None] * deltas)
    return h * lax.rsqrt(jnp.sum(h * h, axis=(0, 2), keepdims=True) + 1e-6)


def long_conv_bidir(u, h_fwd, h_bwd, bias):
    L = u.shape[1]
    k = jnp.concatenate([h_fwd, jnp.zeros_like(h_fwd[:1]), jnp.flip(h_bwd[:L - 1], axis=0)], axis=0)
    y = jnp.fft.irfft(jnp.fft.rfft(u, n=2 * L, axis=1) * jnp.fft.rfft(k, n=2 * L, axis=0)[None],
                      n=2 * L, axis=1)[:, :L]
    return y + u * bias.astype(jnp.float32)


def hyena_branch(hy, p):
    L = hy.shape[1]
    u = dwconv_centred(hy, p['hy_conv_w'], p['hy_conv_b']).astype(jnp.float32)
    x1, x2, v = u[..., :HY_WIDTH], u[..., HY_WIDTH:2 * HY_WIDTH], u[..., 2 * HY_WIDTH:]
    filt = hyena_filters(L, p)
    z = x1 * long_conv_bidir(v, filt[:, 0, 0], filt[:, 0, 1], p['hy_bias'][0])
    z = x2 * long_conv_bidir(z, filt[:, 1, 0], filt[:, 1, 1], p['hy_bias'][1])
    return z.astype(hy.dtype)


def even_mixer(hc, hl, p, need_ctx):
    Bsz = hl.shape[0]
    o1 = SSD_INNER
    o2 = o1 + SSD_CONV_CH
    o3 = o2 + 2 * SSD_HEADS
    pc = hc @ p['w_in']
    pl = hl @ p['w_in']
    xbc_c = jax.nn.silu(dwconv_centred(pc[..., o1:o2], p['ssd_conv_w'], p['ssd_conv_b']))
    xbc_l = jax.nn.silu(dwconv_centred(pl[..., o1:o2], p['ssd_conv_w'], p['ssd_conv_b']))
    init = jnp.zeros((2 * Bsz, SSD_GROUPS, SSD_HEADS // SSD_GROUPS, SSD_HEAD_DIM, SSD_STATE), jnp.float32)
    s_c, ctx_state = ssd_branch(pc[..., :o1], xbc_c, pc[..., o2:o3], p, init)
    s_l, _ = ssd_branch(pl[..., :o1], xbc_l, pl[..., o2:o3], p, ctx_state)
    yl = jnp.concatenate([s_l, hyena_branch(pl[..., o3:], p)], axis=-1) @ p['w_out']
    yc = None
    if need_ctx:
        yc = jnp.concatenate([s_c, hyena_branch(pc[..., o3:], p)], axis=-1) @ p['w_out']
    return yc, yl


def rwkv_prepare(code, p):
    Bsz, L, _ = code.shape
    code = centred_shift_mix(code.astype(jnp.float32), p['mu'].astype(jnp.float32))
    r = code[..., :RW_WIDTH]
    k = code[..., RW_WIDTH:2 * RW_WIDTH]
    v = code[..., 2 * RW_WIDTH:3 * RW_WIDTH]
    o = 3 * RW_WIDTH
    wc = code[..., o:o + RW_DECAY_LORA]
    ac = code[..., o + RW_DECAY_LORA:o + RW_DECAY_LORA + RW_AAA_LORA]
    gc = code[..., o + RW_DECAY_LORA + RW_AAA_LORA:]
    w_log = -jax.nn.softplus(-(p['w0'] + jnp.einsum('blr,drc->bldc', jnp.tanh(wc), p['w_up']))) - 0.5
    decay = jnp.exp(-jnp.exp(w_log))
    a = jax.nn.sigmoid(p['a0'] + jnp.einsum('blr,drc->bldc', ac, p['a_up']))
    g = jax.nn.sigmoid(gc) @ p['g_up']
    kk = (k * p['k_k']).reshape(Bsz, L, RW_HEADS, RW_HEAD)
    kk = (kk * lax.rsqrt(jnp.sum(kk * kk, axis=-1, keepdims=True) + 1e-12)).reshape(Bsz, L, RW_WIDTH)
    kd = k[:, :, None, :] * (1.0 + (a - 1.0) * p['k_a'])
    return r, v, kk, decay, a, kd, g


def rwkv_scan_dir(prep, d, init, reverse):
    r, v, kk, decay, a, kd, _ = prep

    def tm(t):
        return t.reshape(t.shape[0], t.shape[1], RW_HEADS, RW_HEAD).transpose(1, 0, 2, 3)

    def step(S, inp):
        r_t, w_t, k_t, v_t, kk_t, a_t = inp
        sa = jnp.einsum('bhvk,bhk->bhv', S, -kk_t)
        S = (S * w_t[:, :, None, :] + sa[..., None] * (kk_t * a_t)[:, :, None, :]
             + v_t[..., None] * k_t[:, :, None, :])
        return S, jnp.einsum('bhvk,bhk->bhv', S, r_t)

    xs = (tm(r), tm(decay[:, :, d]), tm(kd[:, :, d]), tm(v), tm(kk), tm(a[:, :, d]))
    final, ys = lax.scan(step, init, xs, reverse=reverse, unroll=8)
    return ys.transpose(1, 0, 2, 3), final


def rwkv_output(y_fwd, y_bwd, prep, p):
    r, v, _, _, _, kd, g = prep
    Bsz, L = r.shape[0], r.shape[1]
    y = y_fwd + y_bwd
    mean = jnp.mean(y, axis=-1, keepdims=True)
    var = jnp.mean(jnp.square(y - mean), axis=-1, keepdims=True)
    y = ((y - mean) * lax.rsqrt(var + RW_GN_EPS)).reshape(Bsz, L, RW_WIDTH) * p['ln_w'] + p['ln_b']
    bonus = jnp.sum(r.reshape(Bsz, L, 1, RW_HEADS, RW_HEAD) * kd.reshape(Bsz, L, 2, RW_HEADS, RW_HEAD)
                    * p['r_k'], axis=(2, 4))[..., None] * v.reshape(Bsz, L, RW_HEADS, RW_HEAD)
    return (y + bonus.reshape(Bsz, L, RW_WIDTH)) * g


def axial_rope_angles(L):
    rows = L // GRID_W
    row = jnp.repeat(jnp.arange(rows, dtype=jnp.float32), GRID_W)
    col = jnp.tile(jnp.arange(GRID_W, dtype=jnp.float32), rows)
    inv = ROPE_THETA ** (-jnp.arange(0, ROPE_AXIS, 2, dtype=jnp.float32) / ROPE_AXIS)
    return row[:, None] * inv, col[:, None] * inv


def _rotate(x, ang):
    half = x.shape[-1] // 2
    cos = jnp.cos(ang)[None, :, None, :]
    sin = jnp.sin(ang)[None, :, None, :]
    x1, x2 = x[..., :half], x[..., half:]
    return jnp.concatenate([x1 * cos - x2 * sin, x2 * cos + x1 * sin], axis=-1)


def apply_axial_rope(x, ang_row, ang_col):
    xf = x.astype(jnp.float32)
    return jnp.concatenate([_rotate(xf[..., :ROPE_AXIS], ang_row),
                            _rotate(xf[..., ROPE_AXIS:], ang_col)], axis=-1).astype(x.dtype)


def attn_qkv(pa, p, ang):
    Bsz, L, _ = pa.shape
    qw = AT_Q_HEADS * AT_HEAD
    kw = AT_KV_HEADS * AT_HEAD
    q = rms_norm(pa[..., :qw].reshape(Bsz, L, AT_Q_HEADS, AT_HEAD), p['q_norm'])
    k = rms_norm(pa[..., qw:qw + kw].reshape(Bsz, L, AT_KV_HEADS, AT_HEAD), p['k_norm'])
    v = pa[..., qw + kw:].reshape(Bsz, L, AT_KV_HEADS, AT_HEAD)
    if ang is not None:
        q = apply_axial_rope(q, *ang)
        k = apply_axial_rope(k, *ang)
    return q, k, v


def block_attention(q, k, v):
    Bsz, Lq = q.shape[0], q.shape[1]
    nb = Lq // AT_BLOCK
    qb = q.reshape(Bsz, nb, AT_BLOCK, AT_KV_HEADS, AT_GQA, AT_HEAD).transpose(1, 0, 2, 3, 4, 5)
    scale = AT_HEAD ** -0.5

    def one_block(qi):
        s = jnp.einsum('bqkgd,bskd->bkgqs', qi, k).astype(jnp.float32) * scale
        pr = jax.nn.softmax(s, axis=-1).astype(v.dtype)
        return jnp.einsum('bkgqs,bskd->bqkgd', pr, v)

    o = lax.map(one_block, qb)
    return o.transpose(1, 0, 2, 3, 4, 5).reshape(Bsz, Lq, AT_Q_HEADS * AT_HEAD)


def odd_mixer(hc, hl, p, need_ctx):
    Bsz, L, _ = hl.shape
    pc = hc @ p['w_in']
    pl = hl @ p['w_in']
    rc = rwkv_prepare(pc[..., :RW_CODE], p)
    rl = rwkv_prepare(pl[..., :RW_CODE], p)
    zero = jnp.zeros((Bsz, RW_HEADS, RW_HEAD, RW_HEAD), jnp.float32)
    yc_f, sc_f = rwkv_scan_dir(rc, 0, zero, False)
    yc_b, sc_b = rwkv_scan_dir(rc, 1, zero, True)
    yl_f, _ = rwkv_scan_dir(rl, 0, sc_f, False)
    yl_b, _ = rwkv_scan_dir(rl, 1, sc_b, True)
    o_l = rwkv_output(yl_f, yl_b, rl, p).astype(hl.dtype)
    ql, kl, vl = attn_qkv(pl[..., RW_CODE:], p, axial_rope_angles(L))
    qc, kc, vc = attn_qkv(pc[..., RW_CODE:], p, None)
    a_l = block_attention(ql, jnp.concatenate([kc, kl], axis=1), jnp.concatenate([vc, vl], axis=1))
    yl = jnp.concatenate([o_l, a_l], axis=-1) @ p['w_out']
    yc = None
    if need_ctx:
        o_c = rwkv_output(yc_f, yc_b, rc, p).astype(hc.dtype)
        yc = jnp.concatenate([o_c, block_attention(qc, kc, vc)], axis=-1) @ p['w_out']
    return yc, yl


def moe_ffn(t, p):
    M = t.shape[0]
    scores = jax.nn.sigmoid(jnp.dot(t.astype(jnp.float32), p['router_w'].astype(jnp.float32)))
    sel = scores + p['router_bias'].astype(jnp.float32)
    grp_score = jnp.sum(lax.top_k(sel.reshape(M, N_GROUPS, N_EXPERTS // N_GROUPS), 2)[0], axis=-1)
    _, grp_idx = lax.top_k(grp_score, TOPK_GROUPS)
    grp_mask = jnp.any(grp_idx[:, :, None] == jnp.arange(N_GROUPS)[None, None, :], axis=1)
    exp_mask = jnp.repeat(grp_mask, N_EXPERTS // N_GROUPS, axis=1)
    _, idx = lax.top_k(jnp.where(exp_mask, sel, -jnp.inf), TOP_K)
    wts = jnp.take_along_axis(scores, idx, axis=1)
    wts = wts / jnp.sum(wts, axis=-1, keepdims=True) * ROUTED_SCALE

    mk = M * TOP_K
    n_blocks = -(-(mk + N_EXPERTS * (MOE_BLOCK - 1)) // MOE_BLOCK)
    flat_e = idx.reshape(-1).astype(jnp.int32)
    flat_tok = jnp.arange(mk, dtype=jnp.int32) // TOP_K
    order = jnp.argsort(flat_e)
    se, stok, sw = flat_e[order], flat_tok[order], wts.reshape(-1)[order]
    counts = jnp.bincount(flat_e, length=N_EXPERTS)
    starts = jnp.cumsum(counts) - counts
    padded = (counts + MOE_BLOCK - 1) // MOE_BLOCK * MOE_BLOCK
    pend = jnp.cumsum(padded)
    pstart = pend - padded
    dest = pstart[se] + jnp.arange(mk, dtype=jnp.int32) - starts[se]
    rows = n_blocks * MOE_BLOCK
    buf_tok = jnp.zeros((rows,), jnp.int32).at[dest].set(stok)
    buf_w = jnp.zeros((rows,), jnp.float32).at[dest].set(sw)
    block_e = jnp.minimum(jnp.searchsorted(pend, jnp.arange(n_blocks, dtype=jnp.int32) * MOE_BLOCK,
                                           side='right'), N_EXPERTS - 1)

    def run_block(args):
        e, tok = args
        return swiglu(t[tok], p['w1'][e], p['w3'][e], p['w2'][e])

    ys = lax.map(run_block, (block_e, buf_tok.reshape(n_blocks, MOE_BLOCK)))
    routed = jnp.zeros_like(t).at[buf_tok].add(ys.reshape(rows, -1) * buf_w[:, None].astype(t.dtype))
    return routed + swiglu(t, p['s1'], p['s3'], p['s2'])


def setup_inputs(seed: int = 0) -> dict:
    key = jax.random.key(seed)
    ks = iter(jax.random.split(key, 80))
    f32 = jnp.float32
    D = D_MODEL
    NE, NO = N_EVEN, N_ODD

    def nrm(shape, scale):
        return jax.random.normal(next(ks), shape, f32) * scale

    def gain(shape):
        return 1.0 + nrm(shape, 0.05)

    def unif(shape, lo, hi):
        return jax.random.uniform(next(ks), shape, f32, lo, hi)

    dt0 = jnp.exp(unif((NE, 2, SSD_HEADS), math.log(1e-3), math.log(1e-1)))
    return {
        'x': nrm((BATCH, SEQ, D), 1.0),
        'c': nrm((BATCH, D), 1.0),
        'ctx': nrm((BATCH, CTX_LEN, D), 1.0),
        'c_ctx': nrm((D,), 1.0),
        'mod_w': nrm((DEPTH, D, 6 * D), 0.5 * D ** -0.5),
        'mod_b': nrm((DEPTH, 6 * D), 0.01),
        'norm_mix_pre': gain((DEPTH, D)),
        'norm_mix_post': gain((DEPTH, D)),
        'norm_ffn_pre': gain((DEPTH, D)),
        'norm_ffn_post': gain((DEPTH, D)),
        'router_w': nrm((DEPTH, D, N_EXPERTS), D ** -0.5),
        'router_bias': nrm((DEPTH, N_EXPERTS), 0.01),
        'expert_w1': nrm((DEPTH, N_EXPERTS, D, EXPERT_FF), D ** -0.5),
        'expert_w3': nrm((DEPTH, N_EXPERTS, D, EXPERT_FF), D ** -0.5),
        'expert_w2': nrm((DEPTH, N_EXPERTS, EXPERT_FF, D), EXPERT_FF ** -0.5),
        'shared_w1': nrm((DEPTH, D, SHARED_FF), D ** -0.5),
        'shared_w3': nrm((DEPTH, D, SHARED_FF), D ** -0.5),
        'shared_w2': nrm((DEPTH, SHARED_FF, D), SHARED_FF ** -0.5),
        'ev_w_in': nrm((NE, D, EV_IN), D ** -0.5),
        'ev_w_out': nrm((NE, EV_MIX, D), EV_MIX ** -0.5),
        'ssd_conv_w': nrm((NE, SSD_CONV_CH, SSD_CONV), SSD_CONV ** -0.5),
        'ssd_conv_b': nrm((NE, SSD_CONV_CH), 0.01),
        'ssd_dt_bias': dt0 + jnp.log(-jnp.expm1(-dt0)),
        'ssd_a_log': jnp.log(unif((NE, 2, SSD_HEADS), 1.0, 16.0)),
        'ssd_d': 1.0 + nrm((NE, SSD_HEADS), 0.1),
        'ssd_norm_w': gain((NE, SSD_INNER)),
        'hy_conv_w': nrm((NE, 3 * HY_WIDTH, HY_SHORT), HY_SHORT ** -0.5),
        'hy_conv_b': nrm((NE, 3 * HY_WIDTH), 0.01),
        'hy_mlp_w0': nrm((NE, HY_EMB, HY_HIDDEN), HY_EMB ** -0.5),
        'hy_mlp_b0': nrm((NE, HY_HIDDEN), 0.1),
        'hy_freq0': gain((NE, HY_HIDDEN)),
        'hy_mlp_w1': nrm((NE, HY_HIDDEN, HY_HIDDEN), HY_HIDDEN ** -0.5),
        'hy_mlp_b1': nrm((NE, HY_HIDDEN), 0.1),
        'hy_freq1': gain((NE, HY_HIDDEN)),
        'hy_mlp_w2': nrm((NE, HY_HIDDEN, HY_ORDER * 2 * HY_WIDTH), HY_HIDDEN ** -0.5),
        'hy_bias': nrm((NE, HY_ORDER, HY_WIDTH), 0.1),
        'od_w_in': nrm((NO, D, OD_IN), D ** -0.5),
        'od_w_out': nrm((NO, OD_MIX, D), OD_MIX ** -0.5),
        'rw_mu': unif((NO, RW_CODE), 0.2, 0.8),
        'rw_w0': unif((NO, 2, RW_WIDTH), -6.5, -1.5),
        'rw_w_up': nrm((NO, 2, RW_DECAY_LORA, RW_WIDTH), 0.5 * RW_DECAY_LORA ** -0.5),
        'rw_a0': nrm((NO, 2, RW_WIDTH), 0.1),
        'rw_a_up': nrm((NO, 2, RW_AAA_LORA, RW_WIDTH), 0.5 * RW_AAA_LORA ** -0.5),
        'rw_g_up': nrm((NO, RW_GATE_LORA, RW_WIDTH), RW_GATE_LORA ** -0.5),
        'rw_k_k': 0.85 + nrm((NO, RW_WIDTH), 0.05),
        'rw_k_a': gain((NO, RW_WIDTH)),
        'rw_r_k': nrm((NO, RW_HEADS, RW_HEAD), 0.1),
        'rw_ln_w': gain((NO, RW_WIDTH)),
        'rw_ln_b': nrm((NO, RW_WIDTH), 0.01),
        'at_q_norm': gain((NO, AT_HEAD)),
        'at_k_norm': gain((NO, AT_HEAD)),
    }


def reference(x, c, ctx, c_ctx, mod_w, mod_b, norm_mix_pre, norm_mix_post, norm_ffn_pre, norm_ffn_post,
              router_w, router_bias, expert_w1, expert_w3, expert_w2, shared_w1, shared_w3, shared_w2,
              ev_w_in, ev_w_out, ssd_conv_w, ssd_conv_b, ssd_dt_bias, ssd_a_log, ssd_d, ssd_norm_w,
              hy_conv_w, hy_conv_b, hy_mlp_w0, hy_mlp_b0, hy_freq0, hy_mlp_w1, hy_mlp_b1, hy_freq1,
              hy_mlp_w2, hy_bias, od_w_in, od_w_out, rw_mu, rw_w0, rw_w_up, rw_a0, rw_a_up, rw_g_up,
              rw_k_k, rw_k_a, rw_r_k, rw_ln_w, rw_ln_b, at_q_norm, at_k_norm):
    Bsz = x.shape[0]
    D = D_MODEL
    xc = ctx
    for i in range(DEPTH):
        last = i == DEPTH - 1
        m_l = (jax.nn.silu(c) @ mod_w[i] + mod_b[i]).reshape(Bsz, 6, 1, D)
        m_c = (jax.nn.silu(c_ctx) @ mod_w[i] + mod_b[i]).reshape(6, D)
        hl = rms_norm(x, norm_mix_pre[i]) * (1.0 + m_l[:, 1]) + m_l[:, 0]
        hc = rms_norm(xc, norm_mix_pre[i]) * (1.0 + m_c[1]) + m_c[0]
        if i % 2 == 0:
            j = i // 2
            pe = dict(w_in=ev_w_in[j], w_out=ev_w_out[j], ssd_conv_w=ssd_conv_w[j], ssd_conv_b=ssd_conv_b[j],
                      ssd_dt_bias=ssd_dt_bias[j], ssd_a_log=ssd_a_log[j], ssd_d=ssd_d[j],
                      ssd_norm_w=ssd_norm_w[j], hy_conv_w=hy_conv_w[j], hy_conv_b=hy_conv_b[j],
                      hy_mlp_w0=hy_mlp_w0[j], hy_mlp_b0=hy_mlp_b0[j], hy_freq0=hy_freq0[j],
                      hy_mlp_w1=hy_mlp_w1[j], hy_mlp_b1=hy_mlp_b1[j], hy_freq1=hy_freq1[j],
                      hy_mlp_w2=hy_mlp_w2[j], hy_bias=hy_bias[j])
            yc, yl = even_mixer(hc, hl, pe, not last)
        else:
            j = i // 2
            po = dict(w_in=od_w_in[j], w_out=od_w_out[j], mu=rw_mu[j], w0=rw_w0[j], w_up=rw_w_up[j],
                      a0=rw_a0[j], a_up=rw_a_up[j], g_up=rw_g_up[j], k_k=rw_k_k[j], k_a=rw_k_a[j],
                      r_k=rw_r_k[j], ln_w=rw_ln_w[j], ln_b=rw_ln_b[j], q_norm=at_q_norm[j],
                      k_norm=at_k_norm[j])
            yc, yl = odd_mixer(hc, hl, po, not last)
        x = x + m_l[:, 2] * rms_norm(yl, norm_mix_post[i])
        pm = dict(router_w=router_w[i], router_bias=router_bias[i], w1=expert_w1[i], w3=expert_w3[i],
                  w2=expert_w2[i], s1=shared_w1[i], s3=shared_w3[i], s2=shared_w2[i])
        hl = rms_norm(x, norm_ffn_pre[i]) * (1.0 + m_l[:, 4]) + m_l[:, 3]
        if last:
            f_l = moe_ffn(hl.reshape(-1, D), pm).reshape(x.shape)
        else:
            xc = xc + m_c[2] * rms_norm(yc, norm_mix_post[i])
            hc = rms_norm(xc, norm_ffn_pre[i]) * (1.0 + m_c[4]) + m_c[3]
            n_lat = hl.shape[0] * hl.shape[1]
            f_all = moe_ffn(jnp.concatenate([hl.reshape(-1, D), hc.reshape(-1, D)], axis=0), pm)
            f_l = f_all[:n_lat].reshape(x.shape)
            xc = xc + m_c[5] * rms_norm(f_all[n_lat:].reshape(xc.shape), norm_ffn_post[i])
        x = x + m_l[:, 5] * rms_norm(f_l, norm_ffn_post[i])
    return x
```

```python
import functools
import math

import numpy as np
import jax
import jax.numpy as jnp
from jax import lax
from jax.experimental import pallas as pl
from jax.experimental.pallas import tpu as pltpu

F32 = jnp.float32
BF16 = jnp.bfloat16

D = 1024
BATCH = 2
SEQ = 8192
CTX = 256
N_LAT = BATCH * SEQ
N_ROWS = N_LAT + BATCH * CTX
EPS = 1e-6
GRID_W = 64

SSD_HEADS = 16
SSD_P = 64
SSD_G = 2
SSD_S = 128
SSD_Q = 128
HY_W = 1024
HY_EMB = 33
HY_HID = 64

RW_H = 16
RW_N = 64
RW_CHUNK = 64
RW_GN_EPS = 64e-5

AT_KV = 4
AT_HD = 64

N_EXP = 64
TOP_K = 8
N_GRP = 8
TOPK_GRP = 4
EXP_FF = 256
ROUTED_SCALE = 2.5
MOE_BLK = 128

VMEM_LIMIT = 56 * 1024 * 1024


def _cp(sem, vmem=None):
    return pltpu.CompilerParams(dimension_semantics=sem, vmem_limit_bytes=vmem or VMEM_LIMIT)


def _dot(a, b):
    return jnp.dot(a, b, preferred_element_type=F32)


def _dot_nt(a, b):
    return lax.dot_general(a, b, (((1,), (1,)), ((), ())), preferred_element_type=F32)


def _split(x):
    hi = x.astype(BF16)
    lo = (x - hi.astype(F32)).astype(BF16)
    return hi, lo


def _dot3(a, b):
    ah, al = _split(a)
    bh, bl = _split(b)
    return _dot(ah, bh) + (_dot(ah, bl) + _dot(al, bh))


def _dot2l(a, b):
    ah, al = _split(a)
    return _dot(ah, b) + _dot(al, b)


def _dot2r(a, b):
    bh, bl = _split(b)
    return _dot(a, bh) + _dot(a, bl)


def _silu(x):
    return x * (1.0 / (1.0 + jnp.exp(-x)))


def _sigmoid(x):
    return 1.0 / (1.0 + jnp.exp(-x))


def _softplus(x):
    return jnp.maximum(x, 0.0) + jnp.log(1.0 + jnp.exp(-jnp.abs(x)))


def _seq_of_rowblock(i, tm):
    return jnp.minimum((i * tm) // SEQ, 2)


def _mm_kernel(a_ref, b_ref, o_ref, *, passes):
    a = a_ref[...]
    b = b_ref[...]
    if passes == 3:
        o_ref[...] = _dot3(a.astype(F32), b.astype(F32))
    else:
        o_ref[...] = _dot(a.astype(BF16), b.astype(BF16))


def matmul(a, b, tm, tn, passes=1, name="mm"):
    M, K = a.shape
    N = b.shape[1]
    return pl.pallas_call(
        functools.partial(_mm_kernel, passes=passes),
        out_shape=jax.ShapeDtypeStruct((M, N), F32),
        grid=(M // tm, N // tn),
        in_specs=[pl.BlockSpec((tm, K), lambda i, j: (i, 0)),
                  pl.BlockSpec((K, tn), lambda i, j: (0, j))],
        out_specs=pl.BlockSpec((tm, tn), lambda i, j: (i, j)),
        compiler_params=_cp(("parallel", "parallel")), name=name)(a, b)


def _nmm_kernel(x_ref, g_ref, mod_ref, w_ref, o_ref, a_sc, *, shift_i, scale_i):
    @pl.when(pl.program_id(1) == 0)
    def _():
        x = x_ref[...]
        ms = jnp.mean(x * x, axis=-1, keepdims=True)
        y = x * lax.rsqrt(ms + EPS) * g_ref[...]
        h = y * (1.0 + mod_ref[scale_i:scale_i + 1, :]) + mod_ref[shift_i:shift_i + 1, :]
        a_sc[...] = h.astype(BF16)

    o_ref[...] = _dot(a_sc[...], w_ref[...])


def norm_mod_matmul(x, g, mods, w, shift_i, scale_i, tm=512, tn=None, name="nmm"):
    M = x.shape[0]
    N = w.shape[1]
    tn = tn or N
    return pl.pallas_call(
        functools.partial(_nmm_kernel, shift_i=shift_i, scale_i=scale_i),
        out_shape=jax.ShapeDtypeStruct((M, N), F32),
        grid=(M // tm, N // tn),
        in_specs=[pl.BlockSpec((tm, D), lambda i, j: (i, 0)),
                  pl.BlockSpec((1, D), lambda i, j: (0, 0)),
                  pl.BlockSpec((None, 6, D), lambda i, j: (_seq_of_rowblock(i, tm), 0, 0)),
                  pl.BlockSpec((D, tn), lambda i, j: (0, j))],
        out_specs=pl.BlockSpec((tm, tn), lambda i, j: (i, j)),
        scratch_shapes=[pltpu.VMEM((tm, D), BF16)],
        compiler_params=_cp(("parallel", "arbitrary")), name=name)(x, g.reshape(1, D), mods, w)


def _outproj_kernel(a1_ref, a2_ref, w_ref, x_ref, g_ref, mod_ref, o_ref, *, gate_i):
    y = _dot(a1_ref[...].astype(BF16), w_ref[0:D, :]) + _dot(a2_ref[...].astype(BF16), w_ref[D:2 * D, :])
    ms = jnp.mean(y * y, axis=-1, keepdims=True)
    o_ref[...] = x_ref[...] + mod_ref[gate_i:gate_i + 1, :] * (y * lax.rsqrt(ms + EPS) * g_ref[...])


def outproj_residual(a1, a2, w, x, g, mods, gate_i, tm=256, name="outproj"):
    M = a1.shape[0]
    return pl.pallas_call(
        functools.partial(_outproj_kernel, gate_i=gate_i),
        out_shape=jax.ShapeDtypeStruct((M, D), F32),
        grid=(M // tm,),
        in_specs=[pl.BlockSpec((tm, D), lambda i: (i, 0)),
                  pl.BlockSpec((tm, D), lambda i: (i, 0)),
                  pl.BlockSpec((2 * D, D), lambda i: (0, 0)),
                  pl.BlockSpec((tm, D), lambda i: (i, 0)),
                  pl.BlockSpec((1, D), lambda i: (0, 0)),
                  pl.BlockSpec((None, 6, D), lambda i: (_seq_of_rowblock(i, tm), 0, 0))],
        out_specs=pl.BlockSpec((tm, D), lambda i: (i, 0)),
        compiler_params=_cp(("parallel",)), name=name)(a1, a2, w, x, g.reshape(1, D), mods)


def _mod_kernel(c_ref, w_ref, b_ref, o_ref):
    o_ref[...] = _dot3(_silu(c_ref[...]), w_ref[...]) + b_ref[...]


def modulation(cvecs, w, b):
    N = w.shape[1]
    tn = 1024
    return pl.pallas_call(
        _mod_kernel, out_shape=jax.ShapeDtypeStruct((8, N), F32), grid=(N // tn,),
        in_specs=[pl.BlockSpec((8, D), lambda j: (0, 0)),
                  pl.BlockSpec((D, tn), lambda j: (0, j)),
                  pl.BlockSpec((1, tn), lambda j: (0, j))],
        out_specs=pl.BlockSpec((8, tn), lambda j: (0, j)),
        compiler_params=_cp(("parallel",)), name="modulation")(cvecs, w, b.reshape(1, N))


def _neighbours(x_ref, base, ch, length):
    cur = x_ref[pl.ds(base, ch), :]
    rows = lax.broadcasted_iota(jnp.int32, cur.shape, 0)
    pbase = pl.multiple_of(jnp.maximum(base - 8, 0), 8)
    nbase = pl.multiple_of(jnp.minimum(base + ch, length - 8), 8)
    prev_row = x_ref[pl.ds(pbase, 8), :][7:8, :] * (base > 0).astype(F32)
    next_row = x_ref[pl.ds(nbase, 8), :][0:1, :] * (base + ch < length).astype(F32)
    xm1 = jnp.where(rows == 0, prev_row, pltpu.roll(cur, 1, 0))
    xp1 = jnp.where(rows == ch - 1, next_row, pltpu.roll(cur, ch - 1, 0))
    return xm1, cur, xp1


def _conv3_kernel(x_ref, w_ref, b_ref, o_ref, *, length, ch, act):
    def body(c, carry):
        base = pl.multiple_of(c * ch, ch)
        xm1, cur, xp1 = _neighbours(x_ref, base, ch, length)
        y = xm1 * w_ref[0:1, :] + cur * w_ref[1:2, :] + xp1 * w_ref[2:3, :] + b_ref[...]
        if act:
            y = _silu(y)
        o_ref[pl.ds(base, ch), :] = y
        return carry

    lax.fori_loop(0, length // ch, body, 0)


def dwconv3(p, col0, ncols, w, b, act, name, cb=256):
    wt = jnp.transpose(w)
    b2 = b.reshape(1, ncols)
    outs = []
    for (length, row0, ch) in ((SEQ, 0, 512), (CTX, N_LAT, 256)):
        rb0 = row0 // length
        outs.append(pl.pallas_call(
            functools.partial(_conv3_kernel, length=length, ch=ch, act=act),
            out_shape=jax.ShapeDtypeStruct((BATCH * length, ncols), F32),
            grid=(BATCH, ncols // cb),
            in_specs=[pl.BlockSpec((length, cb), lambda s, j: (rb0 + s, col0 // cb + j)),
                      pl.BlockSpec((3, cb), lambda s, j: (0, j)),
                      pl.BlockSpec((1, cb), lambda s, j: (0, j))],
            out_specs=pl.BlockSpec((length, cb), lambda s, j: (s, j)),
            compiler_params=_cp(("parallel", "parallel")), name=name)(p, wt, b2))
    return jnp.concatenate(outs, axis=0)


def _ssd_kernel(xs_ref, bm_ref, cm_ref, dt_ref, dtT_ref, bias_ref, biasT_ref, alog_ref, alogT_ref,
                y_ref, st_ref):
    d = pl.program_id(0)
    c = pl.program_id(3)
    Q = SSD_Q
    HG = SSD_HEADS // SSD_G

    @pl.when(c == 0)
    def _():
        st_ref[...] = jnp.zeros_like(st_ref)

    isb = d == 1
    sgn = 1 - 2 * d
    dt = _softplus(dt_ref[...] + bias_ref[...])
    dtT = _softplus(dtT_ref[...] + biasT_ref[...])
    a = dt * (-jnp.exp(alog_ref[...]))
    aT = dtT * (-jnp.exp(alogT_ref[...]))
    ii = lax.broadcasted_iota(jnp.int32, (Q, Q), 0)
    jj = lax.broadcasted_iota(jnp.int32, (Q, Q), 1)
    tri = (jj <= ii).astype(BF16)
    triT = (ii <= jj).astype(BF16)
    cs = _dot2r(tri, a)
    csT = _dot2l(aT, triT)
    tot = cs[Q - 1:Q, :]
    p = jnp.where(isb, a - cs, cs)
    pT = jnp.where(isb, aT - csT, csT)
    dec_out = jnp.exp(jnp.where(isb, tot, 0.0) + p)
    dec_state = jnp.exp(jnp.where(isb, 0.0, tot) - p)
    chunk_dec = jnp.exp(tot)
    mask = sgn * (ii - jj) >= 0
    bm = bm_ref[...].astype(BF16)
    cm = cm_ref[...].astype(BF16)
    cb = _dot_nt(cm, bm)
    xs = xs_ref[...]
    ys = []
    for h in range(HG):
        lmat = jnp.exp(jnp.where(mask, p[:, h:h + 1] - pT[h:h + 1, :], -1e30))
        g = (cb * lmat).astype(BF16)
        xh = xs[:, h * SSD_P:(h + 1) * SSD_P] * dt[:, h:h + 1]
        s_old = st_ref[h]
        y = _dot(g, xh.astype(BF16)) + dec_out[:, h:h + 1] * _dot(cm, s_old.astype(BF16))
        xdec = (xh * dec_state[:, h:h + 1]).astype(BF16)
        upd = lax.dot_general(bm, xdec, (((0,), (0,)), ((), ())), preferred_element_type=F32)
        st_ref[h] = chunk_dec[:, h:h + 1] * s_old + upd
        ys.append(y)
    y_ref[...] = jnp.concatenate(ys, axis=1)


def _ssd_rowblock(d, b, c):
    n_ctx = CTX // SSD_Q
    n_lat = SEQ // SSD_Q
    cc = jnp.where(d == 0, c, n_ctx - 1 - c)
    lc = jnp.where(d == 0, c - n_ctx, n_ctx + n_lat - 1 - c)
    return jnp.where(c < n_ctx, N_LAT // SSD_Q + b * n_ctx + cc, b * n_lat + lc)


def ssd_scan(xbc, dt_raw, dt_bias, a_log):
    HG = SSD_HEADS // SSD_G
    GW = HG * SSD_P
    dsel = dt_raw[:, :2 * SSD_HEADS].reshape(N_ROWS, 2, SSD_G, HG).transpose(1, 2, 0, 3)
    dselT = dsel.transpose(0, 1, 3, 2)
    bias = dt_bias.reshape(2, SSD_G, 1, HG)
    biasT = dt_bias.reshape(2, SSD_G, HG, 1)
    alog = a_log.reshape(2, SSD_G, 1, HG)
    alogT = a_log.reshape(2, SSD_G, HG, 1)
    nch = (CTX + SEQ) // SSD_Q
    rb = lambda d, b, g, c: _ssd_rowblock(d, b, c)
    bcol = SSD_HEADS * SSD_P // SSD_S
    return pl.pallas_call(
        _ssd_kernel,
        out_shape=jax.ShapeDtypeStruct((2, N_ROWS, SSD_HEADS * SSD_P), F32),
        grid=(2, BATCH, SSD_G, nch),
        in_specs=[pl.BlockSpec((SSD_Q, GW), lambda d, b, g, c: (rb(d, b, g, c), g)),
                  pl.BlockSpec((SSD_Q, SSD_S), lambda d, b, g, c: (rb(d, b, g, c), bcol + g)),
                  pl.BlockSpec((SSD_Q, SSD_S), lambda d, b, g, c: (rb(d, b, g, c), bcol + SSD_G + g)),
                  pl.BlockSpec((None, None, SSD_Q, HG), lambda d, b, g, c: (d, g, rb(d, b, g, c), 0)),
                  pl.BlockSpec((None, None, HG, SSD_Q), lambda d, b, g, c: (d, g, 0, rb(d, b, g, c))),
                  pl.BlockSpec((None, None, 1, HG), lambda d, b, g, c: (d, g, 0, 0)),
                  pl.BlockSpec((None, None, HG, 1), lambda d, b, g, c: (d, g, 0, 0)),
                  pl.BlockSpec((None, None, 1, HG), lambda d, b, g, c: (d, g, 0, 0)),
                  pl.BlockSpec((None, None, HG, 1), lambda d, b, g, c: (d, g, 0, 0))],
        out_specs=pl.BlockSpec((None, SSD_Q, GW), lambda d, b, g, c: (d, rb(d, b, g, c), g)),
        scratch_shapes=[pltpu.VMEM((HG, SSD_S, SSD_P), F32)],
        compiler_params=_cp(("parallel", "parallel", "parallel", "arbitrary")), name="ssd_scan",
    )(xbc, xbc, xbc, dsel, dselT, bias, biasT, alog, alogT)


def _ssd_out_kernel(yf_ref, yb_ref, xs_ref, z_ref, dskip_ref, nw_ref, o_ref):
    y = yf_ref[...] + yb_ref[...] + xs_ref[...] * dskip_ref[...]
    y = y * _silu(z_ref[...])
    gs = SSD_HEADS * SSD_P // SSD_G
    parts = []
    for g in range(SSD_G):
        yg = y[:, g * gs:(g + 1) * gs]
        parts.append(yg * lax.rsqrt(jnp.mean(yg * yg, axis=-1, keepdims=True) + EPS))
    o_ref[...] = jnp.concatenate(parts, axis=1) * nw_ref[...]


def ssd_output(y2, xbc, p, d_skip, norm_w, tm=256):
    W = SSD_HEADS * SSD_P
    dexp = jnp.repeat(d_skip, SSD_P).reshape(1, W)
    return pl.pallas_call(
        _ssd_out_kernel, out_shape=jax.ShapeDtypeStruct((N_ROWS, W), F32), grid=(N_ROWS // tm,),
        in_specs=[pl.BlockSpec((None, tm, W), lambda i: (0, i, 0)),
                  pl.BlockSpec((None, tm, W), lambda i: (1, i, 0)),
                  pl.BlockSpec((tm, W), lambda i: (i, 0)),
                  pl.BlockSpec((tm, W), lambda i: (i, 0)),
                  pl.BlockSpec((1, W), lambda i: (0, 0)),
                  pl.BlockSpec((1, W), lambda i: (0, 0))],
        out_specs=pl.BlockSpec((tm, W), lambda i: (i, 0)),
        compiler_params=_cp(("parallel",)), name="ssd_output")(y2, y2, xbc, p, dexp, norm_w.reshape(1, W))


def _hyfilt_kernel(f_ref, w0_ref, b0_ref, fr0_ref, w1_ref, b1_ref, fr1_ref, w2_ref, dl_ref, h_ref, ss_ref):
    f = f_ref[...]
    h = jnp.sin(fr0_ref[...] * (_dot3(f, w0_ref[...]) + b0_ref[...]))
    h = jnp.sin(fr1_ref[...] * (_dot3(h, w1_ref[...]) + b1_ref[...]))
    h = _dot3(h, w2_ref[...])
    h = h * jnp.exp(-f[:, 0:1] * dl_ref[...])
    h_ref[...] = h

    @pl.when(pl.program_id(0) == 0)
    def _():
        ss_ref[...] = jnp.zeros_like(ss_ref)

    ss_ref[...] += jnp.sum(h * h, axis=0, keepdims=True)


def hyena_filter_raw(L, hp):
    pos = jnp.arange(L, dtype=F32)
    t = pos / (L - 1)
    bands = (HY_EMB - 1) // 2
    freqs = jnp.linspace(1e-4, bands - 1, bands, dtype=F32)
    ang = (2.0 * math.pi / L) * pos[:, None] * freqs[None, :]
    feats = jnp.concatenate([t[:, None], jnp.cos(ang), -jnp.sin(ang)], axis=-1)
    feats = jnp.pad(feats, ((0, 0), (0, 128 - HY_EMB)))
    w0 = jnp.pad(hp['hy_mlp_w0'], ((0, 128 - HY_EMB), (0, 0)))
    min_decay = math.log(1e-2) / 1.5
    max_decay = math.log(1e-2) / 0.3
    deltas = jnp.abs(jnp.linspace(min_decay, max_decay, HY_W, dtype=F32))
    dl = jnp.tile(deltas, 4).reshape(1, 4 * HY_W)
    tl = min(L, 512)
    NF = 4 * HY_W
    full = lambda shape: pl.BlockSpec(shape, lambda i: (0, 0))
    return pl.pallas_call(
        _hyfilt_kernel,
        out_shape=(jax.ShapeDtypeStruct((L, NF), F32), jax.ShapeDtypeStruct((1, NF), F32)),
        grid=(L // tl,),
        in_specs=[pl.BlockSpec((tl, 128), lambda i: (i, 0)), full((128, HY_HID)), full((1, HY_HID)),
                  full((1, HY_HID)), full((HY_HID, HY_HID)), full((1, HY_HID)), full((1, HY_HID)),
                  full((HY_HID, NF)), full((1, NF))],
        out_specs=(pl.BlockSpec((tl, NF), lambda i: (i, 0)), pl.BlockSpec((1, NF), lambda i: (0, 0))),
        compiler_params=_cp(("arbitrary",)), name="hyena_filter",
    )(feats, w0, hp['hy_mlp_b0'].reshape(1, -1), hp['hy_freq0'].reshape(1, -1), hp['hy_mlp_w1'],
      hp['hy_mlp_b1'].reshape(1, -1), hp['hy_freq1'].reshape(1, -1), hp['hy_mlp_w2'], dl)


def _cis(num, den):
    ang = (2.0 * math.pi / den) * (num % den).astype(F32)
    return jnp.cos(ang), -jnp.sin(ang)


def _fft_consts(NB, BS):
    N = NB * BS
    h = NB // 2
    k1 = jnp.arange(h, dtype=jnp.int32)
    j = jnp.arange(NB, dtype=jnp.int32)
    re, im = _cis(j[None, :] * (2 * k1[:, None] + 1), 2 * NB)
    f1 = jnp.concatenate([re, im], axis=0)
    neg = jnp.where(j >= h, -1.0, 1.0)[None, :]
    f1_data = f1[:, :h]
    f1_filt = f1 * neg
    f1_inv = (2.0 / N) * jnp.concatenate([re[:, :h].T, im[:, :h].T], axis=1)
    r = jnp.arange(BS, dtype=jnp.int32)
    k2 = jnp.arange(BS, dtype=jnp.int32)
    kk = 2 * k1[:, None, None] + 2 * NB * k2[None, :, None] + 1
    gre, gim = _cis(kk * r[None, None, :], 2 * N)
    gf = jnp.concatenate([jnp.concatenate([gre, -gim], axis=2), jnp.concatenate([gim, gre], axis=2)], axis=1)
    gret, gimt = gre.transpose(0, 2, 1), gim.transpose(0, 2, 1)
    gi = jnp.concatenate([jnp.concatenate([gret, gimt], axis=2), jnp.concatenate([-gimt, gret], axis=2)], axis=1)
    return (f1_data.astype(BF16), f1_filt.astype(BF16), f1_inv.astype(BF16), gf.astype(BF16), gi.astype(BF16))


def _fft_fwd_kernel(u_ref, f1_ref, g_ref, o_ref, t_sc, *, NB, BS, nj, kg):
    @pl.when(pl.program_id(2) == 0)
    def _():
        f1 = f1_ref[...]

        def body(r, carry):
            xr = u_ref[pl.ds(r, nj, stride=BS), :].astype(BF16)
            t_sc[pl.ds(pl.multiple_of(r * NB, NB), NB), :] = _dot(f1, xr)
            return carry

        lax.fori_loop(0, BS, body, 0)

    k0 = pl.program_id(2) * kg
    for i in range(kg):
        are = t_sc[pl.ds(k0 + i, BS, stride=NB), :]
        aim = t_sc[pl.ds(k0 + i + NB // 2, BS, stride=NB), :]
        a = jnp.concatenate([are, aim], axis=0).astype(BF16)
        o_ref[i] = _dot(g_ref[i], a)


def fft_fwd(u, col0, nbatch, nj, f1, gf, NB, BS, ct=128, kg=8):
    h = NB // 2
    kg = min(kg, h)
    return pl.pallas_call(
        functools.partial(_fft_fwd_kernel, NB=NB, BS=BS, nj=nj, kg=kg),
        out_shape=jax.ShapeDtypeStruct((nbatch, h, 2 * BS, HY_W), F32),
        grid=(nbatch, HY_W // ct, h // kg),
        in_specs=[pl.BlockSpec((nj * BS, ct), lambda b, c, k: (b, col0 // ct + c)),
                  pl.BlockSpec((NB, nj), lambda b, c, k: (0, 0)),
                  pl.BlockSpec((kg, 2 * BS, 2 * BS), lambda b, c, k: (k, 0, 0))],
        out_specs=pl.BlockSpec((None, kg, 2 * BS, ct), lambda b, c, k: (b, k, 0, c)),
        scratch_shapes=[pltpu.VMEM((BS * NB, ct), F32)],
        compiler_params=_cp(("parallel", "parallel", "arbitrary")), name="hyena_fft_fwd")(u, f1, gf)


def _cmul(u, h, half):
    ure, uim = u[:half], u[half:]
    hre, him = h[:half], h[half:]
    return jnp.concatenate([ure * hre - uim * him, ure * him + uim * hre], axis=0)


def _fft_inv_kernel(us_ref, hs_ref, gi_ref, f1i_ref, o_ref, t_sc, *, NB, BS, kg):
    ks = pl.program_id(2)
    for i in range(kg):
        y = _cmul(us_ref[i], hs_ref[i], BS).astype(BF16)
        row = pl.multiple_of((ks * kg + i) * 2 * BS, 2 * BS)
        t_sc[pl.ds(row, 2 * BS), :] = _dot(gi_ref[i], y)

    @pl.when(ks == pl.num_programs(2) - 1)
    def _():
        f1i = f1i_ref[...]

        def body(r, carry):
            bre = t_sc[pl.ds(r, NB // 2, stride=2 * BS), :]
            bim = t_sc[pl.ds(r + BS, NB // 2, stride=2 * BS), :]
            b = jnp.concatenate([bre, bim], axis=0).astype(BF16)
            o_ref[pl.ds(r, NB // 2, stride=BS), :] = _dot(f1i, b)
            return carry

        lax.fori_loop(0, BS, body, 0)


def fft_inv(us, hs, gi, f1i, NB, BS, ct=128, kg=8):
    nbatch, h = us.shape[0], NB // 2
    kg = min(kg, h)
    L = h * BS
    return pl.pallas_call(
        functools.partial(_fft_inv_kernel, NB=NB, BS=BS, kg=kg),
        out_shape=jax.ShapeDtypeStruct((nbatch * L, HY_W), F32),
        grid=(nbatch, HY_W // ct, h // kg),
        in_specs=[pl.BlockSpec((None, kg, 2 * BS, ct), lambda b, c, k: (b, k, 0, c)),
                  pl.BlockSpec((None, kg, 2 * BS, ct), lambda b, c, k: (0, k, 0, c)),
                  pl.BlockSpec((kg, 2 * BS, 2 * BS), lambda b, c, k: (k, 0, 0)),
                  pl.BlockSpec((h, NB), lambda b, c, k: (0, 0))],
        out_specs=pl.BlockSpec((L, ct), lambda b, c, k: (b, c)),
        scratch_shapes=[pltpu.VMEM((h * 2 * BS, ct), F32)],
        compiler_params=_cp(("parallel", "parallel", "arbitrary")), name="hyena_fft_inv")(us, hs, gi, f1i)


def _dft_consts(L):
    N = 2 * L
    k = jnp.arange(L, dtype=jnp.int32)
    n = jnp.arange(N, dtype=jnp.int32)
    re, im = _cis(n[None, :] * (2 * k[:, None] + 1), 2 * N)
    f = jnp.concatenate([re, im], axis=0)
    neg = jnp.where(n >= L, -1.0, 1.0)[None, :]
    fi = (2.0 / N) * jnp.concatenate([re[:, :L].T, im[:, :L].T], axis=1)
    return f[:, :L].astype(BF16), (f * neg).astype(BF16), fi.astype(BF16)


def _cdft_kernel(f_ref, x_ref, o_ref):
    o_ref[...] = _dot(f_ref[...], x_ref[...].astype(BF16))


def dft_fwd(x, f, row0, col0, nbatch, ct=256):
    M, K = f.shape
    return pl.pallas_call(
        _cdft_kernel, out_shape=jax.ShapeDtypeStruct((nbatch, M, HY_W), F32),
        grid=(nbatch, HY_W // ct),
        in_specs=[pl.BlockSpec((M, K), lambda b, c: (0, 0)),
                  pl.BlockSpec((K, ct), lambda b, c: (row0 // K + b, col0 // ct + c))],
        out_specs=pl.BlockSpec((None, M, ct), lambda b, c: (b, 0, c)),
        compiler_params=_cp(("parallel", "parallel")), name="hyena_dft_fwd")(f, x)


def _cdft_inv_kernel(us_ref, hs_ref, fi_ref, o_ref):
    half = us_ref.shape[0] // 2
    o_ref[...] = _dot(fi_ref[...], _cmul(us_ref[...], hs_ref[...], half).astype(BF16))


def dft_inv(us, hs, fi, ct=256):
    nbatch, M2, _ = us.shape
    L = fi.shape[0]
    return pl.pallas_call(
        _cdft_inv_kernel, out_shape=jax.ShapeDtypeStruct((nbatch * L, HY_W), F32),
        grid=(nbatch, HY_W // ct),
        in_specs=[pl.BlockSpec((None, M2, ct), lambda b, c: (b, 0, c)),
                  pl.BlockSpec((None, M2, ct), lambda b, c: (0, 0, c)),
                  pl.BlockSpec((L, M2), lambda b, c: (0, 0))],
        out_specs=pl.BlockSpec((L, ct), lambda b, c: (b, c)),
        compiler_params=_cp(("parallel", "parallel")), name="hyena_dft_inv")(us, hs, fi)


def _hy_gate_kernel(g_ref, y_ref, u_ref, ss_ref, b_ref, o_ref):
    scale = lax.rsqrt(ss_ref[0:1, :] + ss_ref[1:2, :] + 1e-6)
    o_ref[...] = g_ref[...] * (y_ref[...] * scale + u_ref[...] * b_ref[...])


def hy_gate(gate, gcol, grow, y, uin, ucol, urow, ss, order, bias, tm=256):
    M = y.shape[0]
    return pl.pallas_call(
        _hy_gate_kernel, out_shape=jax.ShapeDtypeStruct((M, HY_W), F32), grid=(M // tm,),
        in_specs=[pl.BlockSpec((tm, HY_W), lambda i: (grow // tm + i, gcol // HY_W)),
                  pl.BlockSpec((tm, HY_W), lambda i: (i, 0)),
                  pl.BlockSpec((tm, HY_W), lambda i: (urow // tm + i, ucol // HY_W)),
                  pl.BlockSpec((None, 2, HY_W), lambda i: (order, 0, 0)),
                  pl.BlockSpec((None, 1, HY_W), lambda i: (order, 0, 0))],
        out_specs=pl.BlockSpec((tm, HY_W), lambda i: (i, 0)),
        compiler_params=_cp(("parallel",)), name="hyena_gate")(gate, y, uin, ss, bias)


def _filter_taps(hraw, L, order):
    h4 = hraw.reshape(L, 2, 2, HY_W)
    fwd, bwd = h4[:, order, 0], h4[:, order, 1]
    return jnp.concatenate([fwd, jnp.zeros((1, HY_W), F32), jnp.flip(bwd[:L - 1], axis=0)], axis=0)


def hyena(u, hp):
    C = HY_W
    bias = hp['hy_bias'].reshape(2, 1, C)
    NB = BS = int(round(math.sqrt(2 * SEQ)))
    f1d, f1f, f1i, gf, gi = _fft_consts(NB, BS)
    hraw, ss = hyena_filter_raw(SEQ, hp)
    ss = ss.reshape(2, 2, C)
    zin, zcol = u, 2 * C
    for order in range(2):
        taps = _filter_taps(hraw, SEQ, order)
        hs = fft_fwd(taps, 0, 1, NB, f1f, gf, NB, BS)
        us = fft_fwd(zin, zcol, BATCH, NB // 2, f1d, gf, NB, BS)
        y = fft_inv(us, hs, gi, f1i, NB, BS)
        zin = hy_gate(u, order * C, 0, y, zin, zcol, 0, ss, order, bias)
        zcol = 0
    z_lat = zin
    fd, ff, fi = _dft_consts(CTX)
    hraw, ss = hyena_filter_raw(CTX, hp)
    ss = ss.reshape(2, 2, C)
    zin, zcol, zrow = u, 2 * C, N_LAT
    for order in range(2):
        taps = _filter_taps(hraw, CTX, order)
        hs = dft_fwd(taps, ff, 0, 0, 1)
        us = dft_fwd(zin, fd, zrow, zcol, BATCH)
        y = dft_inv(us, hs, fi)
        zin = hy_gate(u, order * C, N_LAT, y, zin, zcol, zrow, ss, order, bias)
        zcol, zrow = 0, 0
    return jnp.concatenate([z_lat, zin], axis=0)


EV_SSD_IN = SSD_HEADS * SSD_P
EV_XBC = EV_SSD_IN + 2 * SSD_G * SSD_S
EV_PAD_N = 5760


def even_mixer(x, mods, g_pre, g_post, ep):
    o1 = EV_SSD_IN
    o2 = o1 + EV_XBC
    o3 = o2 + 2 * SSD_HEADS
    w = ep['w_in']
    n_in = w.shape[1]
    w_perm = jnp.concatenate([w[:, :o2], w[:, o3:], w[:, o2:o3],
                              jnp.zeros((D, EV_PAD_N - n_in), F32)], axis=1).astype(BF16)
    p = norm_mod_matmul(x, g_pre, mods, w_perm, 0, 1, tn=640, name="even_in_proj")
    xbc = dwconv3(p, o1, EV_XBC, ep['ssd_conv_w'], ep['ssd_conv_b'], True, "ssd_conv")
    u = dwconv3(p, o2, 3 * HY_W, ep['hy_conv_w'], ep['hy_conv_b'], False, "hyena_conv")
    dt_raw = p[:, o2 + 3 * HY_W:o2 + 3 * HY_W + 2 * SSD_HEADS]
    y2 = ssd_scan(xbc, dt_raw, ep['ssd_dt_bias'], ep['ssd_a_log'])
    s = ssd_output(y2, xbc, p, ep['ssd_d'], ep['ssd_norm_w'])
    zh = hyena(u, ep)
    return outproj_residual(s, zh, ep['w_out'].astype(BF16), x, g_post, mods, 2, name="even_out_proj")


def _head_sum(x, e, et):
    return _dot2l(_dot2l(x, e), et)


def _rw_prep_kernel(r_ref, k_ref, v_ref, lo_ref, w0_ref, wup_ref, a0_ref, aup_ref, gup_ref, kk_ref, ka_ref,
                    rk_ref, e_ref, et_ref, lw_ref, kd_ref, be_ref, kap_ref, g_ref, bonus_ref):
    r, k, v = r_ref[...], k_ref[...], v_ref[...]
    lo = lo_ref[...]
    wc, ac, gc = lo[:, 0:64], lo[:, 64:128], lo[:, 128:384]
    e, et = e_ref[...], et_ref[...]
    kk = k * kk_ref[...]
    kap = kk * lax.rsqrt(_head_sum(kk * kk, e, et) + 1e-12)
    kap_ref[...] = kap
    g_ref[...] = _dot3(_sigmoid(gc), gup_ref[...])
    kd_sum = jnp.zeros_like(k)
    for d in range(2):
        wlog = -_softplus(-(w0_ref[d:d + 1, :] + _dot3(jnp.tanh(wc), wup_ref[d]))) - 0.5
        lw_ref[d] = -jnp.exp(wlog)
        a = _sigmoid(a0_ref[d:d + 1, :] + _dot3(ac, aup_ref[d]))
        kd = k * (1.0 + (a - 1.0) * ka_ref[...])
        kd_ref[d] = kd
        be_ref[d] = kap * a
        kd_sum = kd_sum + kd
    bonus_ref[...] = _head_sum(r * kd_sum * rk_ref[...], e, et) * v


def rwkv_prepare(code, lora, op, tm=256):
    W = RW_H * RW_N
    heads = jnp.arange(W, dtype=jnp.int32) // RW_N
    e = (heads[:, None] == jnp.arange(128, dtype=jnp.int32)[None, :]).astype(BF16)
    et = jnp.transpose(e)
    gup = jnp.pad(op['g_up'], ((0, 256 - op['g_up'].shape[0]), (0, 0)))
    row = lambda a: a.reshape(1, W)
    full2 = lambda shape: pl.BlockSpec(shape, lambda i: (0,) * len(shape))
    outs = pl.pallas_call(
        _rw_prep_kernel,
        out_shape=(jax.ShapeDtypeStruct((2, N_ROWS, W), F32), jax.ShapeDtypeStruct((2, N_ROWS, W), F32),
                   jax.ShapeDtypeStruct((2, N_ROWS, W), F32), jax.ShapeDtypeStruct((N_ROWS, W), F32),
                   jax.ShapeDtypeStruct((N_ROWS, W), F32), jax.ShapeDtypeStruct((N_ROWS, W), F32)),
        grid=(N_ROWS // tm,),
        in_specs=[pl.BlockSpec((tm, W), lambda i: (i, 0)), pl.BlockSpec((tm, W), lambda i: (i, 1)),
                  pl.BlockSpec((tm, W), lambda i: (i, 2)), pl.BlockSpec((tm, 384), lambda i: (i, 0)),
                  full2((2, W)), full2((2, 64, W)), full2((2, W)), full2((2, 64, W)), full2((256, W)),
                  full2((1, W)), full2((1, W)), full2((1, W)), full2((W, 128)), full2((128, W))],
        out_specs=(pl.BlockSpec((2, tm, W), lambda i: (0, i, 0)), pl.BlockSpec((2, tm, W), lambda i: (0, i, 0)),
                   pl.BlockSpec((2, tm, W), lambda i: (0, i, 0)), pl.BlockSpec((tm, W), lambda i: (i, 0)),
                   pl.BlockSpec((tm, W), lambda i: (i, 0)), pl.BlockSpec((tm, W), lambda i: (i, 0))),
        compiler_params=_cp(("parallel",)), name="rwkv_prepare",
    )(code, code, code, lora, op['w0'], op['w_up'], op['a0'], op['a_up'], gup, row(op['k_k']), row(op['k_a']),
      row(op['r_k']), e, et)
    return outs


def _tri_inv(n, eye, masks):
    bd = lambda a, b: _dot(a.astype(BF16), b.astype(BF16))
    d0 = jnp.where(masks[0], n, 0.0)
    d2 = bd(d0, d0)
    d4 = bd(d2, d2)
    t = bd(bd(eye + d0, eye + d2), eye + d4)
    for m in masks[1:]:
        e = jnp.where(m, n, 0.0)
        t = t + bd(t, bd(e, t))
    return t


def _rw_scan_kernel(r_ref, v_ref, lw_ref, kd_ref, be_ref, kap_ref, y_ref, st_ref):
    d = pl.program_id(0)
    c = pl.program_id(2)
    C = RW_CHUNK
    N = RW_N

    @pl.when(c == 0)
    def _():
        st_ref[...] = jnp.zeros_like(st_ref)

    isb = d == 1
    sgn = 1 - 2 * d
    ii = lax.broadcasted_iota(jnp.int32, (C, C), 0)
    jj = lax.broadcasted_iota(jnp.int32, (C, C), 1)
    dif = sgn * (ii - jj)
    incl = dif >= 0
    strict = dif > 0
    tri = incl.astype(BF16)
    eye = (ii == jj).astype(F32)
    blk = [(ii >> s) == (jj >> s) for s in (3, 4, 5)]
    masks = [blk[0], blk[1] & ~blk[0], blk[2] & ~blk[1], ~blk[2]]
    lw = lw_ref[...]
    cum = _dot2r(tri, lw)
    ec = jnp.exp(cum)
    en = jnp.exp(-cum)
    ea = jnp.exp(cum - lw)
    last = jnp.where(isb, cum[0:1, :], cum[C - 1:C, :])
    el = jnp.exp(last - cum)
    kap = kap_ref[...]
    r = r_ref[...]
    v = v_ref[...]
    a_t = -kap * ea
    r_t = r * ec
    b_t = be_ref[...] * en
    k_t = kd_ref[...] * en
    b_l = be_ref[...] * el
    k_l = kd_ref[...] * el
    pc = jnp.exp(last)
    ys = []
    for h in range(RW_H):
        sl = slice(h * N, (h + 1) * N)
        lhs = jnp.concatenate([a_t[:, sl], r_t[:, sl]], axis=0).astype(BF16)
        rhs = jnp.concatenate([b_t[:, sl], k_t[:, sl]], axis=0).astype(BF16)
        sc = _dot_nt(lhs, rhs)
        n_ab = jnp.where(strict, sc[0:C, 0:C], 0.0)
        a_ak = jnp.where(strict, sc[0:C, C:2 * C], 0.0)
        m_rb = jnp.where(incl, sc[C:2 * C, 0:C], 0.0).astype(BF16)
        m_rk = jnp.where(incl, sc[C:2 * C, C:2 * C], 0.0).astype(BF16)
        tinv = _tri_inv(n_ab, eye, masks).astype(BF16)
        vh = v[:, sl].astype(BF16)
        av = _dot(a_ak.astype(BF16), vh)
        wu = _dot(tinv, jnp.concatenate([a_t[:, sl], av], axis=1).astype(BF16))
        wub = wu.astype(BF16)
        qy = _dot(m_rb, wub) + jnp.concatenate([r_t[:, sl], _dot(m_rk, vh)], axis=1)
        tn = lambda a, b: lax.dot_general(a, b, (((0,), (0,)), ((), ())), preferred_element_type=F32)
        pp = tn(b_l[:, sl].astype(BF16), wub) + jnp.concatenate(
            [eye * pc[:, sl], tn(k_l[:, sl].astype(BF16), vh)], axis=1)
        h_old = st_ref[h]
        ys.append(_dot3(qy[:, 0:N], h_old) + qy[:, N:2 * N])
        st_ref[h] = _dot3(pp[:, 0:N], h_old) + pp[:, N:2 * N]
    y_ref[...] = jnp.concatenate(ys, axis=1)


def _rw_rowblock(d, b, c):
    n_ctx = CTX // RW_CHUNK
    n_lat = SEQ // RW_CHUNK
    cc = jnp.where(d == 0, c, n_ctx - 1 - c)
    lc = jnp.where(d == 0, c - n_ctx, n_ctx + n_lat - 1 - c)
    return jnp.where(c < n_ctx, N_LAT // RW_CHUNK + b * n_ctx + cc, b * n_lat + lc)


def rwkv_scan(code, lw, kd, be, kap):
    W = RW_H * RW_N
    nch = (CTX + SEQ) // RW_CHUNK
    rb = lambda d, b, c: _rw_rowblock(d, b, c)
    return pl.pallas_call(
        _rw_scan_kernel,
        out_shape=jax.ShapeDtypeStruct((2, N_ROWS, W), F32),
        grid=(2, BATCH, nch),
        in_specs=[pl.BlockSpec((RW_CHUNK, W), lambda d, b, c: (rb(d, b, c), 0)),
                  pl.BlockSpec((RW_CHUNK, W), lambda d, b, c: (rb(d, b, c), 2)),
                  pl.BlockSpec((None, RW_CHUNK, W), lambda d, b, c: (d, rb(d, b, c), 0)),
                  pl.BlockSpec((None, RW_CHUNK, W), lambda d, b, c: (d, rb(d, b, c), 0)),
                  pl.BlockSpec((None, RW_CHUNK, W), lambda d, b, c: (d, rb(d, b, c), 0)),
                  pl.BlockSpec((RW_CHUNK, W), lambda d, b, c: (rb(d, b, c), 0))],
        out_specs=pl.BlockSpec((None, RW_CHUNK, W), lambda d, b, c: (d, rb(d, b, c), 0)),
        scratch_shapes=[pltpu.VMEM((RW_H, RW_N, RW_N), F32)],
        compiler_params=_cp(("parallel", "parallel", "arbitrary")), name="rwkv_scan",
    )(code, code, lw, kd, be, kap)


def _rw_out_kernel(yf_ref, yb_ref, bonus_ref, g_ref, lnw_ref, lnb_ref, e_ref, et_ref, o_ref):
    e, et = e_ref[...], et_ref[...]
    y = yf_ref[...] + yb_ref[...]
    mean = _head_sum(y, e, et) * (1.0 / RW_N)
    yc = y - mean
    var = _head_sum(yc * yc, e, et) * (1.0 / RW_N)
    yn = yc * lax.rsqrt(var + RW_GN_EPS) * lnw_ref[...] + lnb_ref[...]
    o_ref[...] = (yn + bonus_ref[...]) * g_ref[...]


def rwkv_output(y2, bonus, g, op, tm=256):
    W = RW_H * RW_N
    heads = jnp.arange(W, dtype=jnp.int32) // RW_N
    e = (heads[:, None] == jnp.arange(128, dtype=jnp.int32)[None, :]).astype(BF16)
    et = jnp.transpose(e)
    M = N_LAT
    return pl.pallas_call(
        _rw_out_kernel, out_shape=jax.ShapeDtypeStruct((M, W), F32), grid=(M // tm,),
        in_specs=[pl.BlockSpec((None, tm, W), lambda i: (0, i, 0)), pl.BlockSpec((None, tm, W), lambda i: (1, i, 0)),
                  pl.BlockSpec((tm, W), lambda i: (i, 0)), pl.BlockSpec((tm, W), lambda i: (i, 0)),
                  pl.BlockSpec((1, W), lambda i: (0, 0)), pl.BlockSpec((1, W), lambda i: (0, 0)),
                  pl.BlockSpec((W, 128), lambda i: (0, 0)), pl.BlockSpec((128, W), lambda i: (0, 0))],
        out_specs=pl.BlockSpec((tm, W), lambda i: (i, 0)),
        compiler_params=_cp(("parallel",)), name="rwkv_output",
    )(y2, y2, bonus, g, op['ln_w'].reshape(1, W), op['ln_b'].reshape(1, W), e, et)


AT_Q = RW_H * AT_HD
AT_KW = AT_KV * AT_HD
AT_TQ = 512
AT_TK = 256


def _rope_tables(tm):
    half = AT_HD // 2
    inv = 10000.0 ** (-jnp.arange(0, half, 2, dtype=F32) / half)
    pos = jnp.arange(SEQ, dtype=jnp.int32)
    row = (pos // GRID_W).astype(F32)[:, None] * inv
    col = (pos % GRID_W).astype(F32)[:, None] * inv
    cos_h = jnp.concatenate([jnp.cos(row), jnp.cos(row), jnp.cos(col), jnp.cos(col)], axis=1)
    sin_h = jnp.concatenate([-jnp.sin(row), jnp.sin(row), -jnp.sin(col), jnp.sin(col)], axis=1)
    cos_t = jnp.concatenate([jnp.tile(cos_h, (1, 2)), jnp.ones((tm, 128), F32)], axis=0)
    sin_t = jnp.concatenate([jnp.tile(sin_h, (1, 2)), jnp.zeros((tm, 128), F32)], axis=0)
    return cos_t, sin_t


def _rot_partner(x):
    q = AT_HD // 4
    w = x.shape[1]
    lane = lax.broadcasted_iota(jnp.int32, x.shape, 1)
    return jnp.where((lane % (2 * q)) < q, pltpu.roll(x, w - q, 1), pltpu.roll(x, q, 1))


def _at_prep_kernel(q_ref, k_ref, v_ref, cos_ref, sin_ref, qn_ref, kn_ref, e_ref, et_ref, qo_ref, ko_ref, vo_ref):
    e, et = e_ref[...], et_ref[...]
    cos2, sin2 = cos_ref[...], sin_ref[...]

    def norm_rope(x, gain, nrep):
        ms = _head_sum(x * x, e[:x.shape[1]], et[:, :x.shape[1]]) * (1.0 / AT_HD)
        xn = x * lax.rsqrt(ms + EPS) * gain
        cos = jnp.tile(cos2, (1, nrep))
        sin = jnp.tile(sin2, (1, nrep))
        return xn * cos + _rot_partner(xn) * sin

    qo_ref[...] = (norm_rope(q_ref[...], qn_ref[...], AT_Q // 128) * (AT_HD ** -0.5)).astype(BF16)
    ko_ref[...] = norm_rope(k_ref[...], kn_ref[...], AT_KW // 128).astype(BF16)
    vo_ref[...] = v_ref[...].astype(BF16)


def attention_prepare(p, q_norm, k_norm, tm=256):
    cos_t, sin_t = _rope_tables(tm)
    heads = jnp.arange(AT_Q, dtype=jnp.int32) // AT_HD
    e = (heads[:, None] == jnp.arange(128, dtype=jnp.int32)[None, :]).astype(BF16)
    et = jnp.transpose(e)
    tab = lambda i: jnp.where(i * tm < N_LAT, ((i * tm) % SEQ) // tm, SEQ // tm)
    qcol = (3 * RW_H * RW_N) // AT_Q
    kcol = (3 * RW_H * RW_N + AT_Q) // AT_KW
    return pl.pallas_call(
        _at_prep_kernel,
        out_shape=(jax.ShapeDtypeStruct((N_ROWS, AT_Q), BF16), jax.ShapeDtypeStruct((N_ROWS, AT_KW), BF16),
                   jax.ShapeDtypeStruct((N_ROWS, AT_KW), BF16)),
        grid=(N_ROWS // tm,),
        in_specs=[pl.BlockSpec((tm, AT_Q), lambda i: (i, qcol)),
                  pl.BlockSpec((tm, AT_KW), lambda i: (i, kcol)),
                  pl.BlockSpec((tm, AT_KW), lambda i: (i, kcol + 1)),
                  pl.BlockSpec((tm, 128), lambda i: (tab(i), 0)),
                  pl.BlockSpec((tm, 128), lambda i: (tab(i), 0)),
                  pl.BlockSpec((1, AT_Q), lambda i: (0, 0)),
                  pl.BlockSpec((1, AT_KW), lambda i: (0, 0)),
                  pl.BlockSpec((AT_Q, 128), lambda i: (0, 0)),
                  pl.BlockSpec((128, AT_Q), lambda i: (0, 0))],
        out_specs=(pl.BlockSpec((tm, AT_Q), lambda i: (i, 0)), pl.BlockSpec((tm, AT_KW), lambda i: (i, 0)),
                   pl.BlockSpec((tm, AT_KW), lambda i: (i, 0))),
        compiler_params=_cp(("parallel",)), name="attn_prepare",
    )(p, p, p, cos_t, sin_t, jnp.tile(q_norm, AT_Q // AT_HD).reshape(1, AT_Q),
      jnp.tile(k_norm, AT_KV).reshape(1, AT_KW), e, et)


def _flash_kernel(q_ref, k_ref, v_ref, o_ref, m_sc, l_sc, acc_sc):
    ki = pl.program_id(2)
    nq = AT_Q // AT_HD
    gq = nq // AT_KV

    @pl.when(ki == 0)
    def _():
        m_sc[...] = jnp.full_like(m_sc, -1e30)
        l_sc[...] = jnp.zeros_like(l_sc)
        acc_sc[...] = jnp.zeros_like(acc_sc)

    for h in range(nq):
        g = h // gq
        q = q_ref[:, h * AT_HD:(h + 1) * AT_HD]
        k = k_ref[:, g * AT_HD:(g + 1) * AT_HD]
        v = v_ref[:, g * AT_HD:(g + 1) * AT_HD]
        s = _dot_nt(q, k)
        m_old = m_sc[h]
        m_new = jnp.maximum(m_old, jnp.max(s, axis=-1, keepdims=True))
        alpha = jnp.exp(m_old - m_new)
        p = jnp.exp(s - m_new[:, 0:1])
        l_sc[h] = alpha * l_sc[h] + jnp.sum(p, axis=-1, keepdims=True)
        m_sc[h] = m_new
        acc_sc[h] = alpha[:, 0:1] * acc_sc[h] + _dot(p.astype(BF16), v)

    @pl.when(ki == pl.num_programs(2) - 1)
    def _():
        outs = [acc_sc[h] / l_sc[h][:, 0:1] for h in range(nq)]
        o_ref[...] = jnp.concatenate(outs, axis=1)


def flash_attention(q, k, v):
    nq = AT_Q // AT_HD
    nk = (CTX + SEQ) // AT_TK
    kv_rb = lambda b, ki: jnp.where(ki < CTX // AT_TK, N_LAT // AT_TK + b * (CTX // AT_TK) + ki,
                                    b * (SEQ // AT_TK) + ki - CTX // AT_TK)
    return pl.pallas_call(
        _flash_kernel,
        out_shape=jax.ShapeDtypeStruct((N_LAT, AT_Q), F32),
        grid=(BATCH, SEQ // AT_TQ, nk),
        in_specs=[pl.BlockSpec((AT_TQ, AT_Q), lambda b, qi, ki: (b * (SEQ // AT_TQ) + qi, 0)),
                  pl.BlockSpec((AT_TK, AT_KW), lambda b, qi, ki: (kv_rb(b, ki), 0)),
                  pl.BlockSpec((AT_TK, AT_KW), lambda b, qi, ki: (kv_rb(b, ki), 0))],
        out_specs=pl.BlockSpec((AT_TQ, AT_Q), lambda b, qi, ki: (b * (SEQ // AT_TQ) + qi, 0)),
        scratch_shapes=[pltpu.VMEM((nq, AT_TQ, 128), F32), pltpu.VMEM((nq, AT_TQ, 128), F32),
                        pltpu.VMEM((nq, AT_TQ, AT_HD), F32)],
        compiler_params=_cp(("parallel", "parallel", "arbitrary")), name="flash_attention")(q, k, v)


OD_PAD_N = 4992


def odd_mixer(x, mods, g_pre, g_post, op):
    W = RW_H * RW_N
    w = op['w_in']
    c3 = 3 * W
    code_n = c3 + 64 + 64 + 160
    w_perm = jnp.concatenate([w[:, :c3], w[:, code_n:], w[:, c3:code_n],
                              jnp.zeros((D, OD_PAD_N - w.shape[1]), F32)], axis=1).astype(BF16)
    p = norm_mod_matmul(x, g_pre, mods, w_perm, 0, 1, tn=384, name="odd_in_proj")
    mu = op['mu']
    taps = lambda m: jnp.stack([0.5 * m, 1.0 - m, 0.5 * m], axis=1)
    code = dwconv3(p, 0, c3, taps(mu[:c3]), jnp.zeros((c3,), F32), False, "rwkv_shift")
    lo_col = c3 + AT_Q + 2 * AT_KW
    mu_lo = jnp.pad(mu[c3:], (0, 384 - (code_n - c3)))
    lora = dwconv3(p, lo_col, 384, taps(mu_lo), jnp.zeros((384,), F32), False, "rwkv_shift_lora", cb=128)
    lw, kd, be, kap, g, bonus = rwkv_prepare(code, lora, op)
    y2 = rwkv_scan(code, lw, kd, be, kap)
    o_l = rwkv_output(y2, bonus, g, op)
    q, k, v = attention_prepare(p, op['q_norm'], op['k_norm'])
    a_l = flash_attention(q, k, v)
    return outproj_residual(o_l, a_l, op['w_out'].astype(BF16), x[:N_LAT], g_post, mods, 2, name="odd_out_proj")


def _router_kernel(x_ref, g_ref, mod_ref, rw_ref, rb_ref, s1_ref, s3_ref, s2_ref, t_ref, idx_ref, wt_ref, sh_ref):
    x = x_ref[...]
    ms = jnp.mean(x * x, axis=-1, keepdims=True)
    t = x * lax.rsqrt(ms + EPS) * g_ref[...] * (1.0 + mod_ref[4:5, :]) + mod_ref[3:4, :]
    t_ref[...] = t
    tb = t.astype(BF16)
    sh_ref[...] = _dot((_silu(_dot(tb, s1_ref[...])) * _dot(tb, s3_ref[...])).astype(BF16), s2_ref[...])
    th, tl = _split(t)
    wh, wl = _split(rw_ref[...])
    lg = _dot_nt(wh, th) + (_dot_nt(wh, tl) + _dot_nt(wl, th))
    sc = _sigmoid(lg)
    sel = sc + rb_ref[...]
    tm = sel.shape[1]
    gsz = N_EXP // N_GRP
    ninf = -jnp.inf
    sel3 = sel.reshape(N_GRP, gsz, tm)
    i3 = lax.broadcasted_iota(jnp.int32, sel3.shape, 1)
    m1 = jnp.max(sel3, axis=1, keepdims=True)
    first = jnp.min(jnp.where(sel3 == m1, i3, gsz), axis=1, keepdims=True)
    m2 = jnp.max(jnp.where(i3 == first, ninf, sel3), axis=1, keepdims=True)
    grp = (m1 + m2).reshape(N_GRP, tm)
    gi = lax.broadcasted_iota(jnp.int32, grp.shape, 0)
    gmask = jnp.zeros(grp.shape, F32)
    for _ in range(TOPK_GRP):
        m = jnp.max(grp, axis=0, keepdims=True)
        pick = jnp.min(jnp.where(grp == m, gi, N_GRP), axis=0, keepdims=True)
        hit = gi == pick
        gmask = jnp.where(hit, 1.0, gmask)
        grp = jnp.where(hit, ninf, grp)
    emask = jnp.broadcast_to(gmask.reshape(N_GRP, 1, tm), (N_GRP, gsz, tm)).reshape(N_EXP, tm)
    msel = jnp.where(emask > 0.5, sel, ninf)
    ei = lax.broadcasted_iota(jnp.int32, msel.shape, 0)
    idxs, ws = [], []
    for _ in range(TOP_K):
        m = jnp.max(msel, axis=0, keepdims=True)
        pick = jnp.min(jnp.where(msel == m, ei, N_EXP), axis=0, keepdims=True)
        hit = ei == pick
        idxs.append(pick)
        ws.append(jnp.sum(jnp.where(hit, sc, 0.0), axis=0, keepdims=True))
        msel = jnp.where(hit, ninf, msel)
    w = jnp.concatenate(ws, axis=0)
    idx_ref[...] = jnp.concatenate(idxs, axis=0)
    wt_ref[...] = w / jnp.sum(w, axis=0, keepdims=True) * ROUTED_SCALE


def moe_router(x, M, g, mods, mp, tm=256):
    full = lambda shape: pl.BlockSpec(shape, lambda i: (0,) * len(shape))
    return pl.pallas_call(
        _router_kernel,
        out_shape=(jax.ShapeDtypeStruct((M, D), F32), jax.ShapeDtypeStruct((TOP_K, M), jnp.int32),
                   jax.ShapeDtypeStruct((TOP_K, M), F32), jax.ShapeDtypeStruct((M, D), F32)),
        grid=(M // tm,),
        in_specs=[pl.BlockSpec((tm, D), lambda i: (i, 0)), full((1, D)),
                  pl.BlockSpec((None, 6, D), lambda i: (_seq_of_rowblock(i, tm), 0, 0)),
                  full((N_EXP, D)), full((N_EXP, 1)), full((D, EXP_FF)), full((D, EXP_FF)), full((EXP_FF, D))],
        out_specs=(pl.BlockSpec((tm, D), lambda i: (i, 0)), pl.BlockSpec((TOP_K, tm), lambda i: (0, i)),
                   pl.BlockSpec((TOP_K, tm), lambda i: (0, i)), pl.BlockSpec((tm, D), lambda i: (i, 0))),
        compiler_params=_cp(("parallel",)), name="moe_router",
    )(x, g.reshape(1, D), mods, jnp.transpose(mp['router_w']), mp['router_bias'].reshape(N_EXP, 1),
      mp['s1'].astype(BF16), mp['s3'].astype(BF16), mp['s2'].astype(BF16))


def _gather_rows(idx_ref, n, src_hbm, dst, sem, slot):
    def body(r, carry):
        pltpu.make_async_copy(src_hbm.at[pl.ds(idx_ref[0, r], 1)], dst.at[slot, pl.ds(r, 1)], sem.at[slot]).start()
        return carry

    lax.fori_loop(0, n, body, 0)


def _wait_rows(n, src_hbm, dst, sem, slot):
    pltpu.make_async_copy(src_hbm.at[pl.ds(0, n)], dst.at[slot], sem.at[slot]).wait()


def _expert_kernel(be_ref, tok_ref, tokn_ref, w_ref, t_hbm, w1_ref, w3_ref, w2_ref, o_ref, xbuf, sem):
    i = pl.program_id(0)
    n = pl.num_programs(0)
    slot = i % 2

    @pl.when(i == 0)
    def _():
        _gather_rows(tok_ref, MOE_BLK, t_hbm, xbuf, sem, 0)

    @pl.when(i + 1 < n)
    def _():
        _gather_rows(tokn_ref, MOE_BLK, t_hbm, xbuf, sem, 1 - slot)

    _wait_rows(MOE_BLK, t_hbm, xbuf, sem, slot)
    xb = xbuf[slot].astype(BF16)
    h = _silu(_dot(xb, w1_ref[...])) * _dot(xb, w3_ref[...])
    o_ref[...] = _dot(h.astype(BF16), w2_ref[...]) * w_ref[...]


def moe_experts(t, buf_tok, buf_w, block_e, w1, w3, w2):
    nb = block_e.shape[0]
    tok3 = buf_tok.reshape(nb, 1, MOE_BLK)
    nxt = lambda i, be: (jnp.minimum(i + 1, nb - 1), 0, 0)
    grid_spec = pltpu.PrefetchScalarGridSpec(
        num_scalar_prefetch=1, grid=(nb,),
        in_specs=[pl.BlockSpec((None, 1, MOE_BLK), lambda i, be: (i, 0, 0), memory_space=pltpu.SMEM),
                  pl.BlockSpec((None, 1, MOE_BLK), nxt, memory_space=pltpu.SMEM),
                  pl.BlockSpec((MOE_BLK, 1), lambda i, be: (i, 0)),
                  pl.BlockSpec(memory_space=pl.ANY),
                  pl.BlockSpec((None, D, EXP_FF), lambda i, be: (be[i], 0, 0)),
                  pl.BlockSpec((None, D, EXP_FF), lambda i, be: (be[i], 0, 0)),
                  pl.BlockSpec((None, EXP_FF, D), lambda i, be: (be[i], 0, 0))],
        out_specs=pl.BlockSpec((MOE_BLK, D), lambda i, be: (i, 0)),
        scratch_shapes=[pltpu.VMEM((2, MOE_BLK, D), F32), pltpu.SemaphoreType.DMA((2,))])
    return pl.pallas_call(
        _expert_kernel, out_shape=jax.ShapeDtypeStruct((nb * MOE_BLK, D), F32), grid_spec=grid_spec,
        compiler_params=_cp(("arbitrary",)), name="moe_experts",
    )(block_e, tok3, tok3, buf_w.reshape(nb * MOE_BLK, 1), t, w1, w3, w2)


MOE_TT = 32


def _combine_kernel(dst_ref, dstn_ref, ys_hbm, sh_ref, x_ref, g_ref, mod_ref, o_ref, buf, sem):
    i = pl.program_id(0)
    n = pl.num_programs(0)
    slot = i % 2
    nrow = MOE_TT * TOP_K

    @pl.when(i == 0)
    def _():
        _gather_rows(dst_ref, nrow, ys_hbm, buf, sem, 0)

    @pl.when(i + 1 < n)
    def _():
        _gather_rows(dstn_ref, nrow, ys_hbm, buf, sem, 1 - slot)

    _wait_rows(nrow, ys_hbm, buf, sem, slot)
    f = sh_ref[...]
    for k in range(TOP_K):
        f = f + buf[slot, k * MOE_TT:(k + 1) * MOE_TT, :]
    ms = jnp.mean(f * f, axis=-1, keepdims=True)
    o_ref[...] = x_ref[...] + mod_ref[5:6, :] * (f * lax.rsqrt(ms + EPS) * g_ref[...])


def moe_combine(ys, dest, sh, x, M, g, mods):
    nt = M // MOE_TT
    nrow = MOE_TT * TOP_K
    d3 = dest.reshape(nt, MOE_TT, TOP_K).transpose(0, 2, 1).reshape(nt, 1, nrow)
    nxt = lambda i: (jnp.minimum(i + 1, nt - 1), 0, 0)
    return pl.pallas_call(
        _combine_kernel, out_shape=jax.ShapeDtypeStruct((M, D), F32), grid=(nt,),
        in_specs=[pl.BlockSpec((None, 1, nrow), lambda i: (i, 0, 0), memory_space=pltpu.SMEM),
                  pl.BlockSpec((None, 1, nrow), nxt, memory_space=pltpu.SMEM),
                  pl.BlockSpec(memory_space=pl.ANY),
                  pl.BlockSpec((MOE_TT, D), lambda i: (i, 0)),
                  pl.BlockSpec((MOE_TT, D), lambda i: (i, 0)),
                  pl.BlockSpec((1, D), lambda i: (0, 0)),
                  pl.BlockSpec((None, 6, D), lambda i: (_seq_of_rowblock(i, MOE_TT), 0, 0))],
        out_specs=pl.BlockSpec((MOE_TT, D), lambda i: (i, 0)),
        scratch_shapes=[pltpu.VMEM((2, nrow, D), F32), pltpu.SemaphoreType.DMA((2,))],
        compiler_params=_cp(("arbitrary",)), name="moe_combine",
    )(d3, d3, ys, sh, x, g.reshape(1, D), mods)


def moe_layer(x, M, g_pre, g_post, mods, mp):
    t, idx_t, wts_t, sh = moe_router(x, M, g_pre, mods, mp)
    mk = M * TOP_K
    nb = -(-(mk + N_EXP * (MOE_BLK - 1)) // MOE_BLK)
    flat_e = jnp.transpose(idx_t).reshape(mk)
    flat_w = jnp.transpose(wts_t).reshape(mk)
    onehot = (flat_e[:, None] == jnp.arange(N_EXP, dtype=jnp.int32)[None, :]).astype(jnp.int32)
    csum = jnp.cumsum(onehot, axis=0)
    rank = jnp.take_along_axis(csum, flat_e[:, None], axis=1)[:, 0] - 1
    counts = csum[-1]
    padded = (counts + MOE_BLK - 1) // MOE_BLK * MOE_BLK
    pend = jnp.cumsum(padded)
    dest = (pend - padded)[flat_e] + rank
    flat_tok = jnp.arange(mk, dtype=jnp.int32) // TOP_K
    buf_tok = jnp.zeros((nb * MOE_BLK,), jnp.int32).at[dest].set(flat_tok)
    buf_w = jnp.zeros((nb * MOE_BLK,), F32).at[dest].set(flat_w)
    block_e = jnp.minimum(jnp.searchsorted(pend, jnp.arange(nb, dtype=jnp.int32) * MOE_BLK, side='right'),
                          N_EXP - 1).astype(jnp.int32)
    ys = moe_experts(t, buf_tok, buf_w, block_e, mp['w1'].astype(BF16), mp['w3'].astype(BF16),
                     mp['w2'].astype(BF16))
    return moe_combine(ys, dest.astype(jnp.int32), sh, x, M, g_post, mods)


def kernel(x, c, ctx, c_ctx, mod_w, mod_b, norm_mix_pre, norm_mix_post, norm_ffn_pre, norm_ffn_post, router_w, router_bias, expert_w1, expert_w3, expert_w2, shared_w1, shared_w3, shared_w2, ev_w_in, ev_w_out, ssd_conv_w, ssd_conv_b, ssd_dt_bias, ssd_a_log, ssd_d, ssd_norm_w, hy_conv_w, hy_conv_b, hy_mlp_w0, hy_mlp_b0, hy_freq0, hy_mlp_w1, hy_mlp_b1, hy_freq1, hy_mlp_w2, hy_bias, od_w_in, od_w_out, rw_mu, rw_w0, rw_w_up, rw_a0, rw_a_up, rw_g_up, rw_k_k, rw_k_a, rw_r_k, rw_ln_w, rw_ln_b, at_q_norm, at_k_norm):
    xs = jnp.concatenate([x.reshape(N_LAT, D), ctx.reshape(BATCH * CTX, D)], axis=0)
    cvecs = jnp.zeros((8, D), F32).at[0:BATCH].set(c).at[BATCH].set(c_ctx)
    assert mod_w.shape[0] == 2, "one even (SSD | Hyena) layer followed by one odd (RWKV | attention) layer"

    def moe_params(i):
        return dict(router_w=router_w[i], router_bias=router_bias[i], w1=expert_w1[i], w3=expert_w3[i],
                    w2=expert_w2[i], s1=shared_w1[i], s3=shared_w3[i], s2=shared_w2[i])

    mods = modulation(cvecs, mod_w[0], mod_b[0])[:BATCH + 1].reshape(BATCH + 1, 6, D)
    ep = dict(w_in=ev_w_in[0], w_out=ev_w_out[0], ssd_conv_w=ssd_conv_w[0], ssd_conv_b=ssd_conv_b[0],
              ssd_dt_bias=ssd_dt_bias[0], ssd_a_log=ssd_a_log[0], ssd_d=ssd_d[0], ssd_norm_w=ssd_norm_w[0],
              hy_conv_w=hy_conv_w[0], hy_conv_b=hy_conv_b[0], hy_mlp_w0=hy_mlp_w0[0], hy_mlp_b0=hy_mlp_b0[0],
              hy_freq0=hy_freq0[0], hy_mlp_w1=hy_mlp_w1[0], hy_mlp_b1=hy_mlp_b1[0], hy_freq1=hy_freq1[0],
              hy_mlp_w2=hy_mlp_w2[0], hy_bias=hy_bias[0])
    xs = even_mixer(xs, mods, norm_mix_pre[0], norm_mix_post[0], ep)
    xs = moe_layer(xs, N_ROWS, norm_ffn_pre[0], norm_ffn_post[0], mods, moe_params(0))
    mods = modulation(cvecs, mod_w[1], mod_b[1])[:BATCH + 1].reshape(BATCH + 1, 6, D)
    op = dict(w_in=od_w_in[0], w_out=od_w_out[0], mu=rw_mu[0], w0=rw_w0[0], w_up=rw_w_up[0], a0=rw_a0[0],
              a_up=rw_a_up[0], g_up=rw_g_up[0], k_k=rw_k_k[0], k_a=rw_k_a[0], r_k=rw_r_k[0], ln_w=rw_ln_w[0],
              ln_b=rw_ln_b[0], q_norm=at_q_norm[0], k_norm=at_k_norm[0])
    xl = odd_mixer(xs, mods, norm_mix_pre[1], norm_mix_post[1], op)
    xl = moe_layer(xl, N_LAT, norm_ffn_pre[1], norm_ffn_post[1], mods, moe_params(1))
    return xl.reshape(BATCH, SEQ, D)
```

```python
import functools
import math

import numpy as np
import jax
import jax.numpy as jnp
from jax import lax
from jax.experimental import pallas as pl
from jax.experimental.pallas import tpu as pltpu

F32 = jnp.float32
BF16 = jnp.bfloat16

D = 1024
BATCH = 2
SEQ = 8192
CTX = 256
N_LAT = BATCH * SEQ
N_ROWS = N_LAT + BATCH * CTX
EPS = 1e-6
GRID_W = 64

SSD_HEADS = 16
SSD_P = 64
SSD_G = 2
SSD_S = 128
SSD_Q = 128
HY_W = 1024
HY_EMB = 33
HY_HID = 64

RW_H = 16
RW_N = 64
RW_CHUNK = 64
RW_GN_EPS = 64e-5

AT_KV = 4
AT_HD = 64

N_EXP = 64
TOP_K = 8
N_GRP = 8
TOPK_GRP = 4
EXP_FF = 256
ROUTED_SCALE = 2.5
MOE_BLK = 128

VMEM_LIMIT = 56 * 1024 * 1024


def _cp(sem, vmem=None):
    return pltpu.CompilerParams(dimension_semantics=sem, vmem_limit_bytes=vmem or VMEM_LIMIT)


def _dot(a, b):
    return jnp.dot(a, b, preferred_element_type=F32)


def _dot_nt(a, b):
    return lax.dot_general(a, b, (((1,), (1,)), ((), ())), preferred_element_type=F32)


def _split(x):
    hi = x.astype(BF16)
    lo = (x - hi.astype(F32)).astype(BF16)
    return hi, lo


def _dot3(a, b):
    ah, al = _split(a)
    bh, bl = _split(b)
    return _dot(ah, bh) + (_dot(ah, bl) + _dot(al, bh))


def _dot2l(a, b):
    ah, al = _split(a)
    return _dot(ah, b) + _dot(al, b)


def _dot2r(a, b):
    bh, bl = _split(b)
    return _dot(a, bh) + _dot(a, bl)


def _silu(x):
    return x * (1.0 / (1.0 + jnp.exp(-x)))


def _sigmoid(x):
    return 1.0 / (1.0 + jnp.exp(-x))


def _softplus(x):
    return jnp.maximum(x, 0.0) + jnp.log(1.0 + jnp.exp(-jnp.abs(x)))


def _seq_of_rowblock(i, tm):
    return jnp.minimum((i * tm) // SEQ, 2)


def _mm_kernel(a_ref, b_ref, o_ref, *, passes):
    a = a_ref[...]
    b = b_ref[...]
    if passes == 3:
        o_ref[...] = _dot3(a.astype(F32), b.astype(F32))
    else:
        o_ref[...] = _dot(a.astype(BF16), b.astype(BF16))


def matmul(a, b, tm, tn, passes=1, name="mm"):
    M, K = a.shape
    N = b.shape[1]
    return pl.pallas_call(
        functools.partial(_mm_kernel, passes=passes),
        out_shape=jax.ShapeDtypeStruct((M, N), F32),
        grid=(M // tm, N // tn),
        in_specs=[pl.BlockSpec((tm, K), lambda i, j: (i, 0)),
                  pl.BlockSpec((K, tn), lambda i, j: (0, j))],
        out_specs=pl.BlockSpec((tm, tn), lambda i, j: (i, j)),
        compiler_params=_cp(("parallel", "parallel")), name=name)(a, b)


def _nmm_kernel(x_ref, g_ref, mod_ref, w_ref, o_ref, a_sc, *, shift_i, scale_i):
    @pl.when(pl.program_id(1) == 0)
    def _():
        x = x_ref[...]
        ms = jnp.mean(x * x, axis=-1, keepdims=True)
        y = x * lax.rsqrt(ms + EPS) * g_ref[...]
        h = y * (1.0 + mod_ref[scale_i:scale_i + 1, :]) + mod_ref[shift_i:shift_i + 1, :]
        a_sc[...] = h.astype(BF16)

    o_ref[...] = _dot(a_sc[...], w_ref[...])


def norm_mod_matmul(x, g, mods, w, shift_i, scale_i, tm=512, tn=None, name="nmm"):
    M = x.shape[0]
    N = w.shape[1]
    tn = tn or N
    return pl.pallas_call(
        functools.partial(_nmm_kernel, shift_i=shift_i, scale_i=scale_i),
        out_shape=jax.ShapeDtypeStruct((M, N), F32),
        grid=(M // tm, N // tn),
        in_specs=[pl.BlockSpec((tm, D), lambda i, j: (i, 0)),
                  pl.BlockSpec((1, D), lambda i, j: (0, 0)),
                  pl.BlockSpec((None, 6, D), lambda i, j: (_seq_of_rowblock(i, tm), 0, 0)),
                  pl.BlockSpec((D, tn), lambda i, j: (0, j))],
        out_specs=pl.BlockSpec((tm, tn), lambda i, j: (i, j)),
        scratch_shapes=[pltpu.VMEM((tm, D), BF16)],
        compiler_params=_cp(("parallel", "arbitrary")), name=name)(x, g.reshape(1, D), mods, w)


def _outproj_kernel(a1_ref, a2_ref, w_ref, x_ref, g_ref, mod_ref, o_ref, *, gate_i):
    y = _dot(a1_ref[...].astype(BF16), w_ref[0:D, :]) + _dot(a2_ref[...].astype(BF16), w_ref[D:2 * D, :])
    ms = jnp.mean(y * y, axis=-1, keepdims=True)
    o_ref[...] = x_ref[...] + mod_ref[gate_i:gate_i + 1, :] * (y * lax.rsqrt(ms + EPS) * g_ref[...])


def outproj_residual(a1, a2, w, x, g, mods, gate_i, tm=256, name="outproj"):
    M = a1.shape[0]
    return pl.pallas_call(
        functools.partial(_outproj_kernel, gate_i=gate_i),
        out_shape=jax.ShapeDtypeStruct((M, D), F32),
        grid=(M // tm,),
        in_specs=[pl.BlockSpec((tm, D), lambda i: (i, 0)),
                  pl.BlockSpec((tm, D), lambda i: (i, 0)),
                  pl.BlockSpec((2 * D, D), lambda i: (0, 0)),
                  pl.BlockSpec((tm, D), lambda i: (i, 0)),
                  pl.BlockSpec((1, D), lambda i: (0, 0)),
                  pl.BlockSpec((None, 6, D), lambda i: (_seq_of_rowblock(i, tm), 0, 0))],
        out_specs=pl.BlockSpec((tm, D), lambda i: (i, 0)),
        compiler_params=_cp(("parallel",)), name=name)(a1, a2, w, x, g.reshape(1, D), mods)


def _mod_kernel(c_ref, w_ref, b_ref, o_ref):
    o_ref[...] = _dot3(_silu(c_ref[...]), w_ref[...]) + b_ref[...]


def modulation(cvecs, w, b):
    N = w.shape[1]
    tn = 1024
    return pl.pallas_call(
        _mod_kernel, out_shape=jax.ShapeDtypeStruct((8, N), F32), grid=(N // tn,),
        in_specs=[pl.BlockSpec((8, D), lambda j: (0, 0)),
                  pl.BlockSpec((D, tn), lambda j: (0, j)),
                  pl.BlockSpec((1, tn), lambda j: (0, j))],
        out_specs=pl.BlockSpec((8, tn), lambda j: (0, j)),
        compiler_params=_cp(("parallel",)), name="modulation")(cvecs, w, b.reshape(1, N))


def _neighbours(x_ref, base, ch, length):
    cur = x_ref[pl.ds(base, ch), :]
    rows = lax.broadcasted_iota(jnp.int32, cur.shape, 0)
    pbase = pl.multiple_of(jnp.maximum(base - 8, 0), 8)
    nbase = pl.multiple_of(jnp.minimum(base + ch, length - 8), 8)
    prev_row = x_ref[pl.ds(pbase, 8), :][7:8, :] * (base > 0).astype(F32)
    next_row = x_ref[pl.ds(nbase, 8), :][0:1, :] * (base + ch < length).astype(F32)
    xm1 = jnp.where(rows == 0, prev_row, pltpu.roll(cur, 1, 0))
    xp1 = jnp.where(rows == ch - 1, next_row, pltpu.roll(cur, ch - 1, 0))
    return xm1, cur, xp1


def _conv3_kernel(x_ref, w_ref, b_ref, o_ref, *, length, ch, act):
    def body(c, carry):
        base = pl.multiple_of(c * ch, ch)
        xm1, cur, xp1 = _neighbours(x_ref, base, ch, length)
        y = xm1 * w_ref[0:1, :] + cur * w_ref[1:2, :] + xp1 * w_ref[2:3, :] + b_ref[...]
        if act:
            y = _silu(y)
        o_ref[pl.ds(base, ch), :] = y
        return carry

    lax.fori_loop(0, length // ch, body, 0)


def dwconv3(p, col0, ncols, w, b, act, name, cb=256):
    wt = jnp.transpose(w)
    b2 = b.reshape(1, ncols)
    outs = []
    for (length, row0, ch) in ((SEQ, 0, 512), (CTX, N_LAT, 256)):
        rb0 = row0 // length
        outs.append(pl.pallas_call(
            functools.partial(_conv3_kernel, length=length, ch=ch, act=act),
            out_shape=jax.ShapeDtypeStruct((BATCH * length, ncols), F32),
            grid=(BATCH, ncols // cb),
            in_specs=[pl.BlockSpec((length, cb), lambda s, j: (rb0 + s, col0 // cb + j)),
                      pl.BlockSpec((3, cb), lambda s, j: (0, j)),
                      pl.BlockSpec((1, cb), lambda s, j: (0, j))],
            out_specs=pl.BlockSpec((length, cb), lambda s, j: (s, j)),
            compiler_params=_cp(("parallel", "parallel")), name=name)(p, wt, b2))
    return jnp.concatenate(outs, axis=0)


def _ssd_kernel(xs_ref, bm_ref, cm_ref, dt_ref, dtT_ref, bias_ref, biasT_ref, alog_ref, alogT_ref,
                y_ref, st_ref):
    d = pl.program_id(0)
    c = pl.program_id(3)
    Q = SSD_Q
    HG = SSD_HEADS // SSD_G

    @pl.when(c == 0)
    def _():
        st_ref[...] = jnp.zeros_like(st_ref)

    isb = d == 1
    sgn = 1 - 2 * d
    dt = _softplus(dt_ref[...] + bias_ref[...])
    dtT = _softplus(dtT_ref[...] + biasT_ref[...])
    a = dt * (-jnp.exp(alog_ref[...]))
    aT = dtT * (-jnp.exp(alogT_ref[...]))
    ii = lax.broadcasted_iota(jnp.int32, (Q, Q), 0)
    jj = lax.broadcasted_iota(jnp.int32, (Q, Q), 1)
    tri = (jj <= ii).astype(BF16)
    triT = (ii <= jj).astype(BF16)
    cs = _dot2r(tri, a)
    csT = _dot2l(aT, triT)
    tot = cs[Q - 1:Q, :]
    p = jnp.where(isb, a - cs, cs)
    pT = jnp.where(isb, aT - csT, csT)
    dec_out = jnp.exp(jnp.where(isb, tot, 0.0) + p)
    dec_state = jnp.exp(jnp.where(isb, 0.0, tot) - p)
    chunk_dec = jnp.exp(tot)
    mask = sgn * (ii - jj) >= 0
    bm = bm_ref[...].astype(BF16)
    cm = cm_ref[...].astype(BF16)
    cb = _dot_nt(cm, bm)
    xs = xs_ref[...]
    H = range(HG)
    g = [(cb * jnp.exp(jnp.where(mask, p[:, h:h + 1] - pT[h:h + 1, :], -1e30))).astype(BF16) for h in H]
    xh = [xs[:, h * SSD_P:(h + 1) * SSD_P] * dt[:, h:h + 1] for h in H]
    s_old = [st_ref[h] for h in H]
    y_in = [_dot(g[h], xh[h].astype(BF16)) for h in H]
    y_st = [_dot(cm, s_old[h].astype(BF16)) for h in H]
    upd = [lax.dot_general(bm, (xh[h] * dec_state[:, h:h + 1]).astype(BF16), (((0,), (0,)), ((), ())),
                           preferred_element_type=F32) for h in H]
    for h in H:
        st_ref[h] = chunk_dec[:, h:h + 1] * s_old[h] + upd[h]
    y_ref[...] = jnp.concatenate([y_in[h] + dec_out[:, h:h + 1] * y_st[h] for h in H], axis=1)


def _ssd_rowblock(d, b, c):
    n_ctx = CTX // SSD_Q
    n_lat = SEQ // SSD_Q
    cc = jnp.where(d == 0, c, n_ctx - 1 - c)
    lc = jnp.where(d == 0, c - n_ctx, n_ctx + n_lat - 1 - c)
    return jnp.where(c < n_ctx, N_LAT // SSD_Q + b * n_ctx + cc, b * n_lat + lc)


def ssd_scan(xbc, dt_raw, dt_bias, a_log):
    HG = SSD_HEADS // SSD_G
    GW = HG * SSD_P
    dsel = dt_raw[:, :2 * SSD_HEADS].reshape(N_ROWS, 2, SSD_G, HG).transpose(1, 2, 0, 3)
    dselT = dsel.transpose(0, 1, 3, 2)
    bias = dt_bias.reshape(2, SSD_G, 1, HG)
    biasT = dt_bias.reshape(2, SSD_G, HG, 1)
    alog = a_log.reshape(2, SSD_G, 1, HG)
    alogT = a_log.reshape(2, SSD_G, HG, 1)
    nch = (CTX + SEQ) // SSD_Q
    rb = lambda d, b, g, c: _ssd_rowblock(d, b, c)
    bcol = SSD_HEADS * SSD_P // SSD_S
    return pl.pallas_call(
        _ssd_kernel,
        out_shape=jax.ShapeDtypeStruct((2, N_ROWS, SSD_HEADS * SSD_P), F32),
        grid=(2, BATCH, SSD_G, nch),
        in_specs=[pl.BlockSpec((SSD_Q, GW), lambda d, b, g, c: (rb(d, b, g, c), g)),
                  pl.BlockSpec((SSD_Q, SSD_S), lambda d, b, g, c: (rb(d, b, g, c), bcol + g)),
                  pl.BlockSpec((SSD_Q, SSD_S), lambda d, b, g, c: (rb(d, b, g, c), bcol + SSD_G + g)),
                  pl.BlockSpec((None, None, SSD_Q, HG), lambda d, b, g, c: (d, g, rb(d, b, g, c), 0)),
                  pl.BlockSpec((None, None, HG, SSD_Q), lambda d, b, g, c: (d, g, 0, rb(d, b, g, c))),
                  pl.BlockSpec((None, None, 1, HG), lambda d, b, g, c: (d, g, 0, 0)),
                  pl.BlockSpec((None, None, HG, 1), lambda d, b, g, c: (d, g, 0, 0)),
                  pl.BlockSpec((None, None, 1, HG), lambda d, b, g, c: (d, g, 0, 0)),
                  pl.BlockSpec((None, None, HG, 1), lambda d, b, g, c: (d, g, 0, 0))],
        out_specs=pl.BlockSpec((None, SSD_Q, GW), lambda d, b, g, c: (d, rb(d, b, g, c), g)),
        scratch_shapes=[pltpu.VMEM((HG, SSD_S, SSD_P), F32)],
        compiler_params=_cp(("parallel", "parallel", "parallel", "arbitrary")), name="ssd_scan",
    )(xbc, xbc, xbc, dsel, dselT, bias, biasT, alog, alogT)


def _ssd_out_kernel(yf_ref, yb_ref, xs_ref, z_ref, dskip_ref, nw_ref, o_ref):
    y = yf_ref[...] + yb_ref[...] + xs_ref[...] * dskip_ref[...]
    y = y * _silu(z_ref[...])
    gs = SSD_HEADS * SSD_P // SSD_G
    parts = []
    for g in range(SSD_G):
        yg = y[:, g * gs:(g + 1) * gs]
        parts.append(yg * lax.rsqrt(jnp.mean(yg * yg, axis=-1, keepdims=True) + EPS))
    o_ref[...] = jnp.concatenate(parts, axis=1) * nw_ref[...]


def ssd_output(y2, xbc, p, d_skip, norm_w, tm=256):
    W = SSD_HEADS * SSD_P
    dexp = jnp.repeat(d_skip, SSD_P).reshape(1, W)
    return pl.pallas_call(
        _ssd_out_kernel, out_shape=jax.ShapeDtypeStruct((N_ROWS, W), F32), grid=(N_ROWS // tm,),
        in_specs=[pl.BlockSpec((None, tm, W), lambda i: (0, i, 0)),
                  pl.BlockSpec((None, tm, W), lambda i: (1, i, 0)),
                  pl.BlockSpec((tm, W), lambda i: (i, 0)),
                  pl.BlockSpec((tm, W), lambda i: (i, 0)),
                  pl.BlockSpec((1, W), lambda i: (0, 0)),
                  pl.BlockSpec((1, W), lambda i: (0, 0))],
        out_specs=pl.BlockSpec((tm, W), lambda i: (i, 0)),
        compiler_params=_cp(("parallel",)), name="ssd_output")(y2, y2, xbc, p, dexp, norm_w.reshape(1, W))


def _hyfilt_kernel(f_ref, w0_ref, b0_ref, fr0_ref, w1_ref, b1_ref, fr1_ref, w2_ref, dl_ref, h_ref, ss_ref):
    f = f_ref[...]
    h = jnp.sin(fr0_ref[...] * (_dot3(f, w0_ref[...]) + b0_ref[...]))
    h = jnp.sin(fr1_ref[...] * (_dot3(h, w1_ref[...]) + b1_ref[...]))
    h = _dot3(h, w2_ref[...])
    h = h * jnp.exp(-f[:, 0:1] * dl_ref[...])
    h_ref[...] = h

    @pl.when(pl.program_id(0) == 0)
    def _():
        ss_ref[...] = jnp.zeros_like(ss_ref)

    ss_ref[...] += jnp.sum(h * h, axis=0, keepdims=True)


def hyena_filter_raw(L, hp):
    pos = jnp.arange(L, dtype=F32)
    t = pos / (L - 1)
    bands = (HY_EMB - 1) // 2
    freqs = jnp.linspace(1e-4, bands - 1, bands, dtype=F32)
    ang = (2.0 * math.pi / L) * pos[:, None] * freqs[None, :]
    feats = jnp.concatenate([t[:, None], jnp.cos(ang), -jnp.sin(ang)], axis=-1)
    feats = jnp.pad(feats, ((0, 0), (0, 128 - HY_EMB)))
    w0 = jnp.pad(hp['hy_mlp_w0'], ((0, 128 - HY_EMB), (0, 0)))
    min_decay = math.log(1e-2) / 1.5
    max_decay = math.log(1e-2) / 0.3
    deltas = jnp.abs(jnp.linspace(min_decay, max_decay, HY_W, dtype=F32))
    dl = jnp.tile(deltas, 4).reshape(1, 4 * HY_W)
    tl = min(L, 512)
    NF = 4 * HY_W
    full = lambda shape: pl.BlockSpec(shape, lambda i: (0, 0))
    return pl.pallas_call(
        _hyfilt_kernel,
        out_shape=(jax.ShapeDtypeStruct((L, NF), F32), jax.ShapeDtypeStruct((1, NF), F32)),
        grid=(L // tl,),
        in_specs=[pl.BlockSpec((tl, 128), lambda i: (i, 0)), full((128, HY_HID)), full((1, HY_HID)),
                  full((1, HY_HID)), full((HY_HID, HY_HID)), full((1, HY_HID)), full((1, HY_HID)),
                  full((HY_HID, NF)), full((1, NF))],
        out_specs=(pl.BlockSpec((tl, NF), lambda i: (i, 0)), pl.BlockSpec((1, NF), lambda i: (0, 0))),
        compiler_params=_cp(("arbitrary",)), name="hyena_filter",
    )(feats, w0, hp['hy_mlp_b0'].reshape(1, -1), hp['hy_freq0'].reshape(1, -1), hp['hy_mlp_w1'],
      hp['hy_mlp_b1'].reshape(1, -1), hp['hy_freq1'].reshape(1, -1), hp['hy_mlp_w2'], dl)


def _cis(num, den):
    ang = (2.0 * math.pi / den) * (num % den).astype(F32)
    return jnp.cos(ang), -jnp.sin(ang)


def _fft_consts(NB, BS):
    N = NB * BS
    h = NB // 2
    k1 = jnp.arange(h, dtype=jnp.int32)
    j = jnp.arange(NB, dtype=jnp.int32)
    re, im = _cis(j[None, :] * (2 * k1[:, None] + 1), 2 * NB)
    f1 = jnp.concatenate([re, im], axis=0)
    neg = jnp.where(j >= h, -1.0, 1.0)[None, :]
    f1_data = f1[:, :h]
    f1_filt = f1 * neg
    f1_inv = (2.0 / N) * jnp.concatenate([re[:, :h].T, im[:, :h].T], axis=1)
    r = jnp.arange(BS, dtype=jnp.int32)
    k2 = jnp.arange(BS, dtype=jnp.int32)
    kk = 2 * k1[:, None, None] + 2 * NB * k2[None, :, None] + 1
    gre, gim = _cis(kk * r[None, None, :], 2 * N)
    gf = jnp.concatenate([jnp.concatenate([gre, -gim], axis=2), jnp.concatenate([gim, gre], axis=2)], axis=1)
    gret, gimt = gre.transpose(0, 2, 1), gim.transpose(0, 2, 1)
    gi = jnp.concatenate([jnp.concatenate([gret, gimt], axis=2), jnp.concatenate([-gimt, gret], axis=2)], axis=1)
    return (f1_data.astype(BF16), f1_filt.astype(BF16), f1_inv.astype(BF16), gf.astype(BF16), gi.astype(BF16))


def _fft_fwd_kernel(u_ref, f1_ref, g_ref, o_ref, t_sc, *, NB, BS, nj, kg):
    @pl.when(pl.program_id(2) == 0)
    def _():
        f1 = f1_ref[...]

        def body(r, carry):
            xr = u_ref[pl.ds(r, nj, stride=BS), :].astype(BF16)
            t_sc[pl.ds(pl.multiple_of(r * NB, NB), NB), :] = _dot(f1, xr)
            return carry

        lax.fori_loop(0, BS, body, 0)

    k0 = pl.program_id(2) * kg
    for i in range(kg):
        are = t_sc[pl.ds(k0 + i, BS, stride=NB), :]
        aim = t_sc[pl.ds(k0 + i + NB // 2, BS, stride=NB), :]
        a = jnp.concatenate([are, aim], axis=0).astype(BF16)
        o_ref[i] = _dot(g_ref[i], a)


def fft_fwd(u, col0, nbatch, nj, f1, gf, NB, BS, ct=128, kg=8):
    h = NB // 2
    kg = min(kg, h)
    return pl.pallas_call(
        functools.partial(_fft_fwd_kernel, NB=NB, BS=BS, nj=nj, kg=kg),
        out_shape=jax.ShapeDtypeStruct((nbatch, h, 2 * BS, HY_W), F32),
        grid=(nbatch, HY_W // ct, h // kg),
        in_specs=[pl.BlockSpec((nj * BS, ct), lambda b, c, k: (b, col0 // ct + c)),
                  pl.BlockSpec((NB, nj), lambda b, c, k: (0, 0)),
                  pl.BlockSpec((kg, 2 * BS, 2 * BS), lambda b, c, k: (k, 0, 0))],
        out_specs=pl.BlockSpec((None, kg, 2 * BS, ct), lambda b, c, k: (b, k, 0, c)),
        scratch_shapes=[pltpu.VMEM((BS * NB, ct), F32)],
        compiler_params=_cp(("parallel", "parallel", "arbitrary")), name="hyena_fft_fwd")(u, f1, gf)


def _cmul(u, h, half):
    ure, uim = u[:half], u[half:]
    hre, him = h[:half], h[half:]
    return jnp.concatenate([ure * hre - uim * him, ure * him + uim * hre], axis=0)


def _fft_inv_kernel(us_ref, hs_ref, gi_ref, f1i_ref, o_ref, t_sc, *, NB, BS, kg):
    ks = pl.program_id(2)
    for i in range(kg):
        y = _cmul(us_ref[i], hs_ref[i], BS).astype(BF16)
        row = pl.multiple_of((ks * kg + i) * 2 * BS, 2 * BS)
        t_sc[pl.ds(row, 2 * BS), :] = _dot(gi_ref[i], y)

    @pl.when(ks == pl.num_programs(2) - 1)
    def _():
        f1i = f1i_ref[...]

        def body(r, carry):
            bre = t_sc[pl.ds(r, NB // 2, stride=2 * BS), :]
            bim = t_sc[pl.ds(r + BS, NB // 2, stride=2 * BS), :]
            b = jnp.concatenate([bre, bim], axis=0).astype(BF16)
            o_ref[pl.ds(r, NB // 2, stride=BS), :] = _dot(f1i, b)
            return carry

        lax.fori_loop(0, BS, body, 0)


def fft_inv(us, hs, gi, f1i, NB, BS, ct=128, kg=8):
    nbatch, h = us.shape[0], NB // 2
    kg = min(kg, h)
    L = h * BS
    return pl.pallas_call(
        functools.partial(_fft_inv_kernel, NB=NB, BS=BS, kg=kg),
        out_shape=jax.ShapeDtypeStruct((nbatch * L, HY_W), F32),
        grid=(nbatch, HY_W // ct, h // kg),
        in_specs=[pl.BlockSpec((None, kg, 2 * BS, ct), lambda b, c, k: (b, k, 0, c)),
                  pl.BlockSpec((None, kg, 2 * BS, ct), lambda b, c, k: (0, k, 0, c)),
                  pl.BlockSpec((kg, 2 * BS, 2 * BS), lambda b, c, k: (k, 0, 0)),
                  pl.BlockSpec((h, NB), lambda b, c, k: (0, 0))],
        out_specs=pl.BlockSpec((L, ct), lambda b, c, k: (b, c)),
        scratch_shapes=[pltpu.VMEM((h * 2 * BS, ct), F32)],
        compiler_params=_cp(("parallel", "parallel", "arbitrary")), name="hyena_fft_inv")(us, hs, gi, f1i)


def _dft_consts(L):
    N = 2 * L
    k = jnp.arange(L, dtype=jnp.int32)
    n = jnp.arange(N, dtype=jnp.int32)
    re, im = _cis(n[None, :] * (2 * k[:, None] + 1), 2 * N)
    f = jnp.concatenate([re, im], axis=0)
    neg = jnp.where(n >= L, -1.0, 1.0)[None, :]
    fi = (2.0 / N) * jnp.concatenate([re[:, :L].T, im[:, :L].T], axis=1)
    return f[:, :L].astype(BF16), (f * neg).astype(BF16), fi.astype(BF16)


def _cdft_kernel(f_ref, x_ref, o_ref):
    o_ref[...] = _dot(f_ref[...], x_ref[...].astype(BF16))


def dft_fwd(x, f, row0, col0, nbatch, ct=256):
    M, K = f.shape
    return pl.pallas_call(
        _cdft_kernel, out_shape=jax.ShapeDtypeStruct((nbatch, M, HY_W), F32),
        grid=(nbatch, HY_W // ct),
        in_specs=[pl.BlockSpec((M, K), lambda b, c: (0, 0)),
                  pl.BlockSpec((K, ct), lambda b, c: (row0 // K + b, col0 // ct + c))],
        out_specs=pl.BlockSpec((None, M, ct), lambda b, c: (b, 0, c)),
        compiler_params=_cp(("parallel", "parallel")), name="hyena_dft_fwd")(f, x)


def _cdft_inv_kernel(us_ref, hs_ref, fi_ref, o_ref):
    half = us_ref.shape[0] // 2
    o_ref[...] = _dot(fi_ref[...], _cmul(us_ref[...], hs_ref[...], half).astype(BF16))


def dft_inv(us, hs, fi, ct=256):
    nbatch, M2, _ = us.shape
    L = fi.shape[0]
    return pl.pallas_call(
        _cdft_inv_kernel, out_shape=jax.ShapeDtypeStruct((nbatch * L, HY_W), F32),
        grid=(nbatch, HY_W // ct),
        in_specs=[pl.BlockSpec((None, M2, ct), lambda b, c: (b, 0, c)),
                  pl.BlockSpec((None, M2, ct), lambda b, c: (0, 0, c)),
                  pl.BlockSpec((L, M2), lambda b, c: (0, 0))],
        out_specs=pl.BlockSpec((L, ct), lambda b, c: (b, c)),
        compiler_params=_cp(("parallel", "parallel")), name="hyena_dft_inv")(us, hs, fi)


def _hy_gate_kernel(g_ref, y_ref, u_ref, ss_ref, b_ref, o_ref):
    scale = lax.rsqrt(ss_ref[0:1, :] + ss_ref[1:2, :] + 1e-6)
    o_ref[...] = g_ref[...] * (y_ref[...] * scale + u_ref[...] * b_ref[...])


def hy_gate(gate, gcol, grow, y, uin, ucol, urow, ss, order, bias, tm=256):
    M = y.shape[0]
    return pl.pallas_call(
        _hy_gate_kernel, out_shape=jax.ShapeDtypeStruct((M, HY_W), F32), grid=(M // tm,),
        in_specs=[pl.BlockSpec((tm, HY_W), lambda i: (grow // tm + i, gcol // HY_W)),
                  pl.BlockSpec((tm, HY_W), lambda i: (i, 0)),
                  pl.BlockSpec((tm, HY_W), lambda i: (urow // tm + i, ucol // HY_W)),
                  pl.BlockSpec((None, 2, HY_W), lambda i: (order, 0, 0)),
                  pl.BlockSpec((None, 1, HY_W), lambda i: (order, 0, 0))],
        out_specs=pl.BlockSpec((tm, HY_W), lambda i: (i, 0)),
        compiler_params=_cp(("parallel",)), name="hyena_gate")(gate, y, uin, ss, bias)


def _filter_taps(hraw, L, order):
    h4 = hraw.reshape(L, 2, 2, HY_W)
    fwd, bwd = h4[:, order, 0], h4[:, order, 1]
    return jnp.concatenate([fwd, jnp.zeros((1, HY_W), F32), jnp.flip(bwd[:L - 1], axis=0)], axis=0)


def hyena(u, hp):
    C = HY_W
    bias = hp['hy_bias'].reshape(2, 1, C)
    NB = BS = int(round(math.sqrt(2 * SEQ)))
    f1d, f1f, f1i, gf, gi = _fft_consts(NB, BS)
    hraw, ss = hyena_filter_raw(SEQ, hp)
    ss = ss.reshape(2, 2, C)
    zin, zcol = u, 2 * C
    for order in range(2):
        taps = _filter_taps(hraw, SEQ, order)
        hs = fft_fwd(taps, 0, 1, NB, f1f, gf, NB, BS)
        us = fft_fwd(zin, zcol, BATCH, NB // 2, f1d, gf, NB, BS)
        y = fft_inv(us, hs, gi, f1i, NB, BS)
        zin = hy_gate(u, order * C, 0, y, zin, zcol, 0, ss, order, bias)
        zcol = 0
    z_lat = zin
    fd, ff, fi = _dft_consts(CTX)
    hraw, ss = hyena_filter_raw(CTX, hp)
    ss = ss.reshape(2, 2, C)
    zin, zcol, zrow = u, 2 * C, N_LAT
    for order in range(2):
        taps = _filter_taps(hraw, CTX, order)
        hs = dft_fwd(taps, ff, 0, 0, 1)
        us = dft_fwd(zin, fd, zrow, zcol, BATCH)
        y = dft_inv(us, hs, fi)
        zin = hy_gate(u, order * C, N_LAT, y, zin, zcol, zrow, ss, order, bias)
        zcol, zrow = 0, 0
    return jnp.concatenate([z_lat, zin], axis=0)


EV_SSD_IN = SSD_HEADS * SSD_P
EV_XBC = EV_SSD_IN + 2 * SSD_G * SSD_S
EV_PAD_N = 5760


def even_mixer(x, mods, g_pre, g_post, ep):
    o1 = EV_SSD_IN
    o2 = o1 + EV_XBC
    o3 = o2 + 2 * SSD_HEADS
    w = ep['w_in']
    n_in = w.shape[1]
    w_perm = jnp.concatenate([w[:, :o2], w[:, o3:], w[:, o2:o3],
                              jnp.zeros((D, EV_PAD_N - n_in), F32)], axis=1).astype(BF16)
    p = norm_mod_matmul(x, g_pre, mods, w_perm, 0, 1, tn=640, name="even_in_proj")
    xbc = dwconv3(p, o1, EV_XBC, ep['ssd_conv_w'], ep['ssd_conv_b'], True, "ssd_conv")
    u = dwconv3(p, o2, 3 * HY_W, ep['hy_conv_w'], ep['hy_conv_b'], False, "hyena_conv")
    dt_raw = p[:, o2 + 3 * HY_W:o2 + 3 * HY_W + 2 * SSD_HEADS]
    y2 = ssd_scan(xbc, dt_raw, ep['ssd_dt_bias'], ep['ssd_a_log'])
    s = ssd_output(y2, xbc, p, ep['ssd_d'], ep['ssd_norm_w'])
    zh = hyena(u, ep)
    return outproj_residual(s, zh, ep['w_out'].astype(BF16), x, g_post, mods, 2, name="even_out_proj")


def _head_sum(x, e, et):
    return _dot2l(_dot2l(x, e), et)


def _rw_prep_kernel(r_ref, k_ref, v_ref, lo_ref, w0_ref, wup_ref, a0_ref, aup_ref, gup_ref, kk_ref, ka_ref,
                    rk_ref, e_ref, et_ref, lw_ref, kd_ref, be_ref, kap_ref, g_ref, bonus_ref):
    r, k, v = r_ref[...], k_ref[...], v_ref[...]
    lo = lo_ref[...]
    wc, ac, gc = lo[:, 0:64], lo[:, 64:128], lo[:, 128:384]
    e, et = e_ref[...], et_ref[...]
    kk = k * kk_ref[...]
    kap = kk * lax.rsqrt(_head_sum(kk * kk, e, et) + 1e-12)
    kap_ref[...] = kap
    g_ref[...] = _dot3(_sigmoid(gc), gup_ref[...])
    kd_sum = jnp.zeros_like(k)
    for d in range(2):
        wlog = -_softplus(-(w0_ref[d:d + 1, :] + _dot3(jnp.tanh(wc), wup_ref[d]))) - 0.5
        lw_ref[d] = -jnp.exp(wlog)
        a = _sigmoid(a0_ref[d:d + 1, :] + _dot3(ac, aup_ref[d]))
        kd = k * (1.0 + (a - 1.0) * ka_ref[...])
        kd_ref[d] = kd
        be_ref[d] = kap * a
        kd_sum = kd_sum + kd
    bonus_ref[...] = _head_sum(r * kd_sum * rk_ref[...], e, et) * v


def rwkv_prepare(code, lora, op, tm=256):
    W = RW_H * RW_N
    heads = jnp.arange(W, dtype=jnp.int32) // RW_N
    e = (heads[:, None] == jnp.arange(128, dtype=jnp.int32)[None, :]).astype(BF16)
    et = jnp.transpose(e)
    gup = jnp.pad(op['g_up'], ((0, 256 - op['g_up'].shape[0]), (0, 0)))
    row = lambda a: a.reshape(1, W)
    full2 = lambda shape: pl.BlockSpec(shape, lambda i: (0,) * len(shape))
    outs = pl.pallas_call(
        _rw_prep_kernel,
        out_shape=(jax.ShapeDtypeStruct((2, N_ROWS, W), F32), jax.ShapeDtypeStruct((2, N_ROWS, W), F32),
                   jax.ShapeDtypeStruct((2, N_ROWS, W), F32), jax.ShapeDtypeStruct((N_ROWS, W), F32),
                   jax.ShapeDtypeStruct((N_ROWS, W), F32), jax.ShapeDtypeStruct((N_ROWS, W), F32)),
        grid=(N_ROWS // tm,),
        in_specs=[pl.BlockSpec((tm, W), lambda i: (i, 0)), pl.BlockSpec((tm, W), lambda i: (i, 1)),
                  pl.BlockSpec((tm, W), lambda i: (i, 2)), pl.BlockSpec((tm, 384), lambda i: (i, 0)),
                  full2((2, W)), full2((2, 64, W)), full2((2, W)), full2((2, 64, W)), full2((256, W)),
                  full2((1, W)), full2((1, W)), full2((1, W)), full2((W, 128)), full2((128, W))],
        out_specs=(pl.BlockSpec((2, tm, W), lambda i: (0, i, 0)), pl.BlockSpec((2, tm, W), lambda i: (0, i, 0)),
                   pl.BlockSpec((2, tm, W), lambda i: (0, i, 0)), pl.BlockSpec((tm, W), lambda i: (i, 0)),
                   pl.BlockSpec((tm, W), lambda i: (i, 0)), pl.BlockSpec((tm, W), lambda i: (i, 0))),
        compiler_params=_cp(("parallel",)), name="rwkv_prepare",
    )(code, code, code, lora, op['w0'], op['w_up'], op['a0'], op['a_up'], gup, row(op['k_k']), row(op['k_a']),
      row(op['r_k']), e, et)
    return outs


def _tri_inv(n, eye, masks):
    bd = lambda a, b: _dot(a.astype(BF16), b.astype(BF16))
    d0 = jnp.where(masks[0], n, 0.0)
    d2 = bd(d0, d0)
    d4 = bd(d2, d2)
    t = bd(bd(eye + d0, eye + d2), eye + d4)
    for m in masks[1:]:
        e = jnp.where(m, n, 0.0)
        t = t + bd(t, bd(e, t))
    return t


def _rw_scan_kernel(r_ref, v_ref, lw_ref, kd_ref, be_ref, kap_ref, y_ref, st_ref):
    d = pl.program_id(0)
    c = pl.program_id(2)
    C = RW_CHUNK
    N = RW_N

    @pl.when(c == 0)
    def _():
        st_ref[...] = jnp.zeros_like(st_ref)

    isb = d == 1
    sgn = 1 - 2 * d
    ii = lax.broadcasted_iota(jnp.int32, (C, C), 0)
    jj = lax.broadcasted_iota(jnp.int32, (C, C), 1)
    dif = sgn * (ii - jj)
    incl = dif >= 0
    strict = dif > 0
    tri = incl.astype(BF16)
    eye = (ii == jj).astype(F32)
    blk = [(ii >> s) == (jj >> s) for s in (3, 4, 5)]
    masks = [blk[0], blk[1] & ~blk[0], blk[2] & ~blk[1], ~blk[2]]
    lw = lw_ref[...]
    cum = _dot2r(tri, lw)
    ec = jnp.exp(cum)
    en = jnp.exp(-cum)
    ea = jnp.exp(cum - lw)
    last = jnp.where(isb, cum[0:1, :], cum[C - 1:C, :])
    el = jnp.exp(last - cum)
    kap = kap_ref[...]
    r = r_ref[...]
    v = v_ref[...]
    a_t = -kap * ea
    r_t = r * ec
    b_t = be_ref[...] * en
    k_t = kd_ref[...] * en
    b_l = be_ref[...] * el
    k_l = kd_ref[...] * el
    pc = jnp.exp(last)
    H = range(RW_H)
    sl = [slice(h * N, (h + 1) * N) for h in H]
    bd = lambda a, b: _dot(a.astype(BF16), b.astype(BF16))
    tn = lambda a, b: lax.dot_general(a, b, (((0,), (0,)), ((), ())), preferred_element_type=F32)
    sc = [_dot_nt(jnp.concatenate([a_t[:, sl[h]], r_t[:, sl[h]]], axis=0).astype(BF16),
                  jnp.concatenate([b_t[:, sl[h]], k_t[:, sl[h]]], axis=0).astype(BF16)) for h in H]
    n_ab = [jnp.where(strict, sc[h][0:C, 0:C], 0.0) for h in H]
    a_ak = [jnp.where(strict, sc[h][0:C, C:2 * C], 0.0).astype(BF16) for h in H]
    m_rb = [jnp.where(incl, sc[h][C:2 * C, 0:C], 0.0).astype(BF16) for h in H]
    m_rk = [jnp.where(incl, sc[h][C:2 * C, C:2 * C], 0.0).astype(BF16) for h in H]
    vh = [v[:, sl[h]].astype(BF16) for h in H]
    d0 = [jnp.where(masks[0], n_ab[h], 0.0) for h in H]
    d2 = [bd(d0[h], d0[h]) for h in H]
    d4 = [bd(d2[h], d2[h]) for h in H]
    t = [bd(eye + d0[h], eye + d2[h]) for h in H]
    t = [bd(t[h], eye + d4[h]) for h in H]
    for m in masks[1:]:
        et = [bd(jnp.where(m, n_ab[h], 0.0), t[h]) for h in H]
        t = [t[h] + bd(t[h], et[h]) for h in H]
    av = [_dot(a_ak[h], vh[h]) for h in H]
    wub = [bd(t[h], jnp.concatenate([a_t[:, sl[h]], av[h]], axis=1)).astype(BF16) for h in H]
    mv = [_dot(m_rk[h], vh[h]) for h in H]
    kv = [tn(k_l[:, sl[h]].astype(BF16), vh[h]) for h in H]
    qy = [_dot(m_rb[h], wub[h]) + jnp.concatenate([r_t[:, sl[h]], mv[h]], axis=1) for h in H]
    pp = [tn(b_l[:, sl[h]].astype(BF16), wub[h]) + jnp.concatenate([eye * pc[:, sl[h]], kv[h]], axis=1)
          for h in H]
    h_old = [st_ref[h] for h in H]
    ys = [_dot3(qy[h][:, 0:N], h_old[h]) + qy[h][:, N:2 * N] for h in H]
    for h in H:
        st_ref[h] = _dot3(pp[h][:, 0:N], h_old[h]) + pp[h][:, N:2 * N]
    y_ref[...] = jnp.concatenate(ys, axis=1)


def _rw_rowblock(d, b, c):
    n_ctx = CTX // RW_CHUNK
    n_lat = SEQ // RW_CHUNK
    cc = jnp.where(d == 0, c, n_ctx - 1 - c)
    lc = jnp.where(d == 0, c - n_ctx, n_ctx + n_lat - 1 - c)
    return jnp.where(c < n_ctx, N_LAT // RW_CHUNK + b * n_ctx + cc, b * n_lat + lc)


def rwkv_scan(code, lw, kd, be, kap):
    W = RW_H * RW_N
    nch = (CTX + SEQ) // RW_CHUNK
    rb = lambda d, b, c: _rw_rowblock(d, b, c)
    return pl.pallas_call(
        _rw_scan_kernel,
        out_shape=jax.ShapeDtypeStruct((2, N_ROWS, W), F32),
        grid=(2, BATCH, nch),
        in_specs=[pl.BlockSpec((RW_CHUNK, W), lambda d, b, c: (rb(d, b, c), 0)),
                  pl.BlockSpec((RW_CHUNK, W), lambda d, b, c: (rb(d, b, c), 2)),
                  pl.BlockSpec((None, RW_CHUNK, W), lambda d, b, c: (d, rb(d, b, c), 0)),
                  pl.BlockSpec((None, RW_CHUNK, W), lambda d, b, c: (d, rb(d, b, c), 0)),
                  pl.BlockSpec((None, RW_CHUNK, W), lambda d, b, c: (d, rb(d, b, c), 0)),
                  pl.BlockSpec((RW_CHUNK, W), lambda d, b, c: (rb(d, b, c), 0))],
        out_specs=pl.BlockSpec((None, RW_CHUNK, W), lambda d, b, c: (d, rb(d, b, c), 0)),
        scratch_shapes=[pltpu.VMEM((RW_H, RW_N, RW_N), F32)],
        compiler_params=_cp(("parallel", "parallel", "arbitrary")), name="rwkv_scan",
    )(code, code, lw, kd, be, kap)


def _rw_out_kernel(yf_ref, yb_ref, bonus_ref, g_ref, lnw_ref, lnb_ref, e_ref, et_ref, o_ref):
    e, et = e_ref[...], et_ref[...]
    y = yf_ref[...] + yb_ref[...]
    mean = _head_sum(y, e, et) * (1.0 / RW_N)
    yc = y - mean
    var = _head_sum(yc * yc, e, et) * (1.0 / RW_N)
    yn = yc * lax.rsqrt(var + RW_GN_EPS) * lnw_ref[...] + lnb_ref[...]
    o_ref[...] = (yn + bonus_ref[...]) * g_ref[...]


def rwkv_output(y2, bonus, g, op, tm=256):
    W = RW_H * RW_N
    heads = jnp.arange(W, dtype=jnp.int32) // RW_N
    e = (heads[:, None] == jnp.arange(128, dtype=jnp.int32)[None, :]).astype(BF16)
    et = jnp.transpose(e)
    M = N_LAT
    return pl.pallas_call(
        _rw_out_kernel, out_shape=jax.ShapeDtypeStruct((M, W), F32), grid=(M // tm,),
        in_specs=[pl.BlockSpec((None, tm, W), lambda i: (0, i, 0)), pl.BlockSpec((None, tm, W), lambda i: (1, i, 0)),
                  pl.BlockSpec((tm, W), lambda i: (i, 0)), pl.BlockSpec((tm, W), lambda i: (i, 0)),
                  pl.BlockSpec((1, W), lambda i: (0, 0)), pl.BlockSpec((1, W), lambda i: (0, 0)),
                  pl.BlockSpec((W, 128), lambda i: (0, 0)), pl.BlockSpec((128, W), lambda i: (0, 0))],
        out_specs=pl.BlockSpec((tm, W), lambda i: (i, 0)),
        compiler_params=_cp(("parallel",)), name="rwkv_output",
    )(y2, y2, bonus, g, op['ln_w'].reshape(1, W), op['ln_b'].reshape(1, W), e, et)


AT_Q = RW_H * AT_HD
AT_KW = AT_KV * AT_HD
AT_TQ = 512
AT_TK = 256


def _rope_tables(tm):
    half = AT_HD // 2
    inv = 10000.0 ** (-jnp.arange(0, half, 2, dtype=F32) / half)
    pos = jnp.arange(SEQ, dtype=jnp.int32)
    row = (pos // GRID_W).astype(F32)[:, None] * inv
    col = (pos % GRID_W).astype(F32)[:, None] * inv
    cos_h = jnp.concatenate([jnp.cos(row), jnp.cos(row), jnp.cos(col), jnp.cos(col)], axis=1)
    sin_h = jnp.concatenate([-jnp.sin(row), jnp.sin(row), -jnp.sin(col), jnp.sin(col)], axis=1)
    cos_t = jnp.concatenate([jnp.tile(cos_h, (1, 2)), jnp.ones((tm, 128), F32)], axis=0)
    sin_t = jnp.concatenate([jnp.tile(sin_h, (1, 2)), jnp.zeros((tm, 128), F32)], axis=0)
    return cos_t, sin_t


def _rot_partner(x):
    q = AT_HD // 4
    w = x.shape[1]
    lane = lax.broadcasted_iota(jnp.int32, x.shape, 1)
    return jnp.where((lane % (2 * q)) < q, pltpu.roll(x, w - q, 1), pltpu.roll(x, q, 1))


def _at_prep_kernel(q_ref, k_ref, v_ref, cos_ref, sin_ref, qn_ref, kn_ref, e_ref, et_ref, qo_ref, ko_ref, vo_ref):
    e, et = e_ref[...], et_ref[...]
    cos2, sin2 = cos_ref[...], sin_ref[...]

    def norm_rope(x, gain, nrep):
        ms = _head_sum(x * x, e[:x.shape[1]], et[:, :x.shape[1]]) * (1.0 / AT_HD)
        xn = x * lax.rsqrt(ms + EPS) * gain
        cos = jnp.tile(cos2, (1, nrep))
        sin = jnp.tile(sin2, (1, nrep))
        return xn * cos + _rot_partner(xn) * sin

    qn = norm_rope(q_ref[...], qn_ref[...], AT_Q // 128) * (AT_HD ** -0.5)
    qo_ref[...] = jnp.transpose(qn).astype(BF16)
    ko_ref[...] = norm_rope(k_ref[...], kn_ref[...], AT_KW // 128).astype(BF16)
    vo_ref[...] = jnp.transpose(v_ref[...]).astype(BF16)


def attention_prepare(p, q_norm, k_norm, tm=256):
    cos_t, sin_t = _rope_tables(tm)
    heads = jnp.arange(AT_Q, dtype=jnp.int32) // AT_HD
    e = (heads[:, None] == jnp.arange(128, dtype=jnp.int32)[None, :]).astype(BF16)
    et = jnp.transpose(e)
    tab = lambda i: jnp.where(i * tm < N_LAT, ((i * tm) % SEQ) // tm, SEQ // tm)
    qcol = (3 * RW_H * RW_N) // AT_Q
    kcol = (3 * RW_H * RW_N + AT_Q) // AT_KW
    return pl.pallas_call(
        _at_prep_kernel,
        out_shape=(jax.ShapeDtypeStruct((AT_Q, N_ROWS), BF16), jax.ShapeDtypeStruct((N_ROWS, AT_KW), BF16),
                   jax.ShapeDtypeStruct((AT_KW, N_ROWS), BF16)),
        grid=(N_ROWS // tm,),
        in_specs=[pl.BlockSpec((tm, AT_Q), lambda i: (i, qcol)),
                  pl.BlockSpec((tm, AT_KW), lambda i: (i, kcol)),
                  pl.BlockSpec((tm, AT_KW), lambda i: (i, kcol + 1)),
                  pl.BlockSpec((tm, 128), lambda i: (tab(i), 0)),
                  pl.BlockSpec((tm, 128), lambda i: (tab(i), 0)),
                  pl.BlockSpec((1, AT_Q), lambda i: (0, 0)),
                  pl.BlockSpec((1, AT_KW), lambda i: (0, 0)),
                  pl.BlockSpec((AT_Q, 128), lambda i: (0, 0)),
                  pl.BlockSpec((128, AT_Q), lambda i: (0, 0))],
        out_specs=(pl.BlockSpec((AT_Q, tm), lambda i: (0, i)), pl.BlockSpec((tm, AT_KW), lambda i: (i, 0)),
                   pl.BlockSpec((AT_KW, tm), lambda i: (0, i))),
        compiler_params=_cp(("parallel",)), name="attn_prepare",
    )(p, p, p, cos_t, sin_t, jnp.tile(q_norm, AT_Q // AT_HD).reshape(1, AT_Q),
      jnp.tile(k_norm, AT_KV).reshape(1, AT_KW), e, et)


def _flash_t_kernel(qt_ref, k_ref, vt_ref, o_ref, m_sc, l_sc, acc_sc):
    ki = pl.program_id(2)
    nq = AT_Q // AT_HD
    gq = nq // AT_KV

    @pl.when(ki == 0)
    def _():
        m_sc[...] = jnp.full_like(m_sc, -1e30)
        l_sc[...] = jnp.zeros_like(l_sc)
        acc_sc[...] = jnp.zeros_like(acc_sc)

    for g in range(AT_KV):
        kg = k_ref[:, g * AT_HD:(g + 1) * AT_HD]
        vtg = vt_ref[g * AT_HD:(g + 1) * AT_HD, :]
        hs = range(g * gq, (g + 1) * gq)
        st = [_dot(kg, qt_ref[h * AT_HD:(h + 1) * AT_HD, :]) for h in hs]
        m_old = [m_sc[h] for h in hs]
        m_new = [jnp.maximum(m_old[i], jnp.max(st[i], axis=0, keepdims=True)) for i in range(gq)]
        alpha = [jnp.exp(m_old[i] - m_new[i]) for i in range(gq)]
        pt = [jnp.exp(st[i] - m_new[i]) for i in range(gq)]
        for i, h in enumerate(hs):
            l_sc[h] = alpha[i] * l_sc[h] + jnp.sum(pt[i], axis=0, keepdims=True)
            m_sc[h] = m_new[i]
        pv = [_dot(vtg, pt[i].astype(BF16)) for i in range(gq)]
        for i, h in enumerate(hs):
            rows = pl.ds(h * AT_HD, AT_HD)
            acc_sc[rows, :] = alpha[i] * acc_sc[rows, :] + pv[i]

    @pl.when(ki == pl.num_programs(2) - 1)
    def _():
        inv = jnp.concatenate([jnp.broadcast_to(1.0 / l_sc[h], (AT_HD, l_sc.shape[2])) for h in range(nq)], axis=0)
        o_ref[...] = jnp.transpose(acc_sc[...] * inv)


def flash_attention_t(qt, k, vt):
    nq = AT_Q // AT_HD
    nk = (CTX + SEQ) // AT_TK
    kv_rb = lambda b, ki: jnp.where(ki < CTX // AT_TK, N_LAT // AT_TK + b * (CTX // AT_TK) + ki,
                                    b * (SEQ // AT_TK) + ki - CTX // AT_TK)
    return pl.pallas_call(
        _flash_t_kernel,
        out_shape=jax.ShapeDtypeStruct((N_LAT, AT_Q), F32),
        grid=(BATCH, SEQ // AT_TQ, nk),
        in_specs=[pl.BlockSpec((AT_Q, AT_TQ), lambda b, qi, ki: (0, b * (SEQ // AT_TQ) + qi)),
                  pl.BlockSpec((AT_TK, AT_KW), lambda b, qi, ki: (kv_rb(b, ki), 0)),
                  pl.BlockSpec((AT_KW, AT_TK), lambda b, qi, ki: (0, kv_rb(b, ki)))],
        out_specs=pl.BlockSpec((AT_TQ, AT_Q), lambda b, qi, ki: (b * (SEQ // AT_TQ) + qi, 0)),
        scratch_shapes=[pltpu.VMEM((nq, 1, AT_TQ), F32), pltpu.VMEM((nq, 1, AT_TQ), F32),
                        pltpu.VMEM((AT_Q, AT_TQ), F32)],
        compiler_params=_cp(("parallel", "parallel", "arbitrary")), name="flash_attention")(qt, k, vt)


OD_PAD_N = 4992


def odd_mixer(x, mods, g_pre, g_post, op):
    W = RW_H * RW_N
    w = op['w_in']
    c3 = 3 * W
    code_n = c3 + 64 + 64 + 160
    w_perm = jnp.concatenate([w[:, :c3], w[:, code_n:], w[:, c3:code_n],
                              jnp.zeros((D, OD_PAD_N - w.shape[1]), F32)], axis=1).astype(BF16)
    p = norm_mod_matmul(x, g_pre, mods, w_perm, 0, 1, tn=384, name="odd_in_proj")
    mu = op['mu']
    taps = lambda m: jnp.stack([0.5 * m, 1.0 - m, 0.5 * m], axis=1)
    code = dwconv3(p, 0, c3, taps(mu[:c3]), jnp.zeros((c3,), F32), False, "rwkv_shift")
    lo_col = c3 + AT_Q + 2 * AT_KW
    mu_lo = jnp.pad(mu[c3:], (0, 384 - (code_n - c3)))
    lora = dwconv3(p, lo_col, 384, taps(mu_lo), jnp.zeros((384,), F32), False, "rwkv_shift_lora", cb=128)
    lw, kd, be, kap, g, bonus = rwkv_prepare(code, lora, op)
    y2 = rwkv_scan(code, lw, kd, be, kap)
    o_l = rwkv_output(y2, bonus, g, op)
    q, k, v = attention_prepare(p, op['q_norm'], op['k_norm'])
    a_l = flash_attention_t(q, k, v)
    return outproj_residual(o_l, a_l, op['w_out'].astype(BF16), x[:N_LAT], g_post, mods, 2, name="odd_out_proj")


def _router_kernel(x_ref, g_ref, mod_ref, rw_ref, rb_ref, s1_ref, s3_ref, s2_ref, t_ref, idx_ref, wt_ref, sh_ref):
    x = x_ref[...]
    ms = jnp.mean(x * x, axis=-1, keepdims=True)
    t = x * lax.rsqrt(ms + EPS) * g_ref[...] * (1.0 + mod_ref[4:5, :]) + mod_ref[3:4, :]
    t_ref[...] = t
    tb = t.astype(BF16)
    sh_ref[...] = _dot((_silu(_dot(tb, s1_ref[...])) * _dot(tb, s3_ref[...])).astype(BF16), s2_ref[...])
    th, tl = _split(t)
    wh, wl = _split(rw_ref[...])
    lg = _dot_nt(wh, th) + (_dot_nt(wh, tl) + _dot_nt(wl, th))
    sc = _sigmoid(lg)
    sel = sc + rb_ref[...]
    tm = sel.shape[1]
    gsz = N_EXP // N_GRP
    ninf = -jnp.inf
    sel3 = sel.reshape(N_GRP, gsz, tm)
    i3 = lax.broadcasted_iota(jnp.int32, sel3.shape, 1)
    m1 = jnp.max(sel3, axis=1, keepdims=True)
    first = jnp.min(jnp.where(sel3 == m1, i3, gsz), axis=1, keepdims=True)
    m2 = jnp.max(jnp.where(i3 == first, ninf, sel3), axis=1, keepdims=True)
    grp = (m1 + m2).reshape(N_GRP, tm)
    gi = lax.broadcasted_iota(jnp.int32, grp.shape, 0)
    gmask = jnp.zeros(grp.shape, F32)
    for _ in range(TOPK_GRP):
        m = jnp.max(grp, axis=0, keepdims=True)
        pick = jnp.min(jnp.where(grp == m, gi, N_GRP), axis=0, keepdims=True)
        hit = gi == pick
        gmask = jnp.where(hit, 1.0, gmask)
        grp = jnp.where(hit, ninf, grp)
    emask = jnp.broadcast_to(gmask.reshape(N_GRP, 1, tm), (N_GRP, gsz, tm)).reshape(N_EXP, tm)
    msel = jnp.where(emask > 0.5, sel, ninf)
    ei = lax.broadcasted_iota(jnp.int32, msel.shape, 0)
    idxs, ws = [], []
    for _ in range(TOP_K):
        m = jnp.max(msel, axis=0, keepdims=True)
        pick = jnp.min(jnp.where(msel == m, ei, N_EXP), axis=0, keepdims=True)
        hit = ei == pick
        idxs.append(pick)
        ws.append(jnp.sum(jnp.where(hit, sc, 0.0), axis=0, keepdims=True))
        msel = jnp.where(hit, ninf, msel)
    w = jnp.concatenate(ws, axis=0)
    idx_ref[...] = jnp.concatenate(idxs, axis=0)
    wt_ref[...] = w / jnp.sum(w, axis=0, keepdims=True) * ROUTED_SCALE


def moe_router(x, M, g, mods, mp, tm=256):
    full = lambda shape: pl.BlockSpec(shape, lambda i: (0,) * len(shape))
    return pl.pallas_call(
        _router_kernel,
        out_shape=(jax.ShapeDtypeStruct((M, D), F32), jax.ShapeDtypeStruct((TOP_K, M), jnp.int32),
                   jax.ShapeDtypeStruct((TOP_K, M), F32), jax.ShapeDtypeStruct((M, D), F32)),
        grid=(M // tm,),
        in_specs=[pl.BlockSpec((tm, D), lambda i: (i, 0)), full((1, D)),
                  pl.BlockSpec((None, 6, D), lambda i: (_seq_of_rowblock(i, tm), 0, 0)),
                  full((N_EXP, D)), full((N_EXP, 1)), full((D, EXP_FF)), full((D, EXP_FF)), full((EXP_FF, D))],
        out_specs=(pl.BlockSpec((tm, D), lambda i: (i, 0)), pl.BlockSpec((TOP_K, tm), lambda i: (0, i)),
                   pl.BlockSpec((TOP_K, tm), lambda i: (0, i)), pl.BlockSpec((tm, D), lambda i: (i, 0))),
        compiler_params=_cp(("parallel",)), name="moe_router",
    )(x, g.reshape(1, D), mods, jnp.transpose(mp['router_w']), mp['router_bias'].reshape(N_EXP, 1),
      mp['s1'].astype(BF16), mp['s3'].astype(BF16), mp['s2'].astype(BF16))


def _gather_rows(idx_ref, n, src_hbm, dst, sem, slot):
    def body(r, carry):
        pltpu.make_async_copy(src_hbm.at[pl.ds(idx_ref[0, r], 1)], dst.at[slot, pl.ds(r, 1)], sem.at[slot]).start()
        return carry

    lax.fori_loop(0, n, body, 0)


def _wait_rows(n, src_hbm, dst, sem, slot):
    pltpu.make_async_copy(src_hbm.at[pl.ds(0, n)], dst.at[slot], sem.at[slot]).wait()


def _expert_kernel(be_ref, tok_ref, tokn_ref, w_ref, t_hbm, w1_ref, w3_ref, w2_ref, o_ref, xbuf, sem):
    i = pl.program_id(0)
    n = pl.num_programs(0)
    slot = i % 2

    @pl.when(i == 0)
    def _():
        _gather_rows(tok_ref, MOE_BLK, t_hbm, xbuf, sem, 0)

    @pl.when(i + 1 < n)
    def _():
        _gather_rows(tokn_ref, MOE_BLK, t_hbm, xbuf, sem, 1 - slot)

    _wait_rows(MOE_BLK, t_hbm, xbuf, sem, slot)
    xb = xbuf[slot].astype(BF16)
    h = _silu(_dot(xb, w1_ref[...])) * _dot(xb, w3_ref[...])
    o_ref[...] = _dot(h.astype(BF16), w2_ref[...]) * w_ref[...]


def moe_experts(t, buf_tok, buf_w, block_e, w1, w3, w2):
    nb = block_e.shape[0]
    tok3 = buf_tok.reshape(nb, 1, MOE_BLK)
    nxt = lambda i, be: (jnp.minimum(i + 1, nb - 1), 0, 0)
    grid_spec = pltpu.PrefetchScalarGridSpec(
        num_scalar_prefetch=1, grid=(nb,),
        in_specs=[pl.BlockSpec((None, 1, MOE_BLK), lambda i, be: (i, 0, 0), memory_space=pltpu.SMEM),
                  pl.BlockSpec((None, 1, MOE_BLK), nxt, memory_space=pltpu.SMEM),
                  pl.BlockSpec((MOE_BLK, 1), lambda i, be: (i, 0)),
                  pl.BlockSpec(memory_space=pl.ANY),
                  pl.BlockSpec((None, D, EXP_FF), lambda i, be: (be[i], 0, 0)),
                  pl.BlockSpec((None, D, EXP_FF), lambda i, be: (be[i], 0, 0)),
                  pl.BlockSpec((None, EXP_FF, D), lambda i, be: (be[i], 0, 0))],
        out_specs=pl.BlockSpec((MOE_BLK, D), lambda i, be: (i, 0)),
        scratch_shapes=[pltpu.VMEM((2, MOE_BLK, D), F32), pltpu.SemaphoreType.DMA((2,))])
    return pl.pallas_call(
        _expert_kernel, out_shape=jax.ShapeDtypeStruct((nb * MOE_BLK, D), F32), grid_spec=grid_spec,
        compiler_params=_cp(("arbitrary",)), name="moe_experts",
    )(block_e, tok3, tok3, buf_w.reshape(nb * MOE_BLK, 1), t, w1, w3, w2)


MOE_TT = 32


def _combine_kernel(dst_ref, dstn_ref, ys_hbm, sh_ref, x_ref, g_ref, mod_ref, o_ref, buf, sem):
    i = pl.program_id(0)
    n = pl.num_programs(0)
    slot = i % 2
    nrow = MOE_TT * TOP_K

    @pl.when(i == 0)
    def _():
        _gather_rows(dst_ref, nrow, ys_hbm, buf, sem, 0)

    @pl.when(i + 1 < n)
    def _():
        _gather_rows(dstn_ref, nrow, ys_hbm, buf, sem, 1 - slot)

    _wait_rows(nrow, ys_hbm, buf, sem, slot)
    f = sh_ref[...]
    for k in range(TOP_K):
        f = f + buf[slot, k * MOE_TT:(k + 1) * MOE_TT, :]
    ms = jnp.mean(f * f, axis=-1, keepdims=True)
    o_ref[...] = x_ref[...] + mod_ref[5:6, :] * (f * lax.rsqrt(ms + EPS) * g_ref[...])


def moe_combine(ys, dest, sh, x, M, g, mods):
    nt = M // MOE_TT
    nrow = MOE_TT * TOP_K
    d3 = dest.reshape(nt, MOE_TT, TOP_K).transpose(0, 2, 1).reshape(nt, 1, nrow)
    nxt = lambda i: (jnp.minimum(i + 1, nt - 1), 0, 0)
    return pl.pallas_call(
        _combine_kernel, out_shape=jax.ShapeDtypeStruct((M, D), F32), grid=(nt,),
        in_specs=[pl.BlockSpec((None, 1, nrow), lambda i: (i, 0, 0), memory_space=pltpu.SMEM),
                  pl.BlockSpec((None, 1, nrow), nxt, memory_space=pltpu.SMEM),
                  pl.BlockSpec(memory_space=pl.ANY),
                  pl.BlockSpec((MOE_TT, D), lambda i: (i, 0)),
                  pl.BlockSpec((MOE_TT, D), lambda i: (i, 0)),
                  pl.BlockSpec((1, D), lambda i: (0, 0)),
                  pl.BlockSpec((None, 6, D), lambda i: (_seq_of_rowblock(i, MOE_TT), 0, 0))],
        out_specs=pl.BlockSpec((MOE_TT, D), lambda i: (i, 0)),
        scratch_shapes=[pltpu.VMEM((2, nrow, D), F32), pltpu.SemaphoreType.DMA((2,))],
        compiler_params=_cp(("arbitrary",)), name="moe_combine",
    )(d3, d3, ys, sh, x, g.reshape(1, D), mods)


def moe_layer(x, M, g_pre, g_post, mods, mp):
    t, idx_t, wts_t, sh = moe_router(x, M, g_pre, mods, mp)
    mk = M * TOP_K
    nb = -(-(mk + N_EXP * (MOE_BLK - 1)) // MOE_BLK)
    flat_e = jnp.transpose(idx_t).reshape(mk)
    flat_w = jnp.transpose(wts_t).reshape(mk)
    onehot = (flat_e[:, None] == jnp.arange(N_EXP, dtype=jnp.int32)[None, :]).astype(jnp.int32)
    csum = jnp.cumsum(onehot, axis=0)
    rank = jnp.take_along_axis(csum, flat_e[:, None], axis=1)[:, 0] - 1
    counts = csum[-1]
    padded = (counts + MOE_BLK - 1) // MOE_BLK * MOE_BLK
    pend = jnp.cumsum(padded)
    dest = (pend - padded)[flat_e] + rank
    flat_tok = jnp.arange(mk, dtype=jnp.int32) // TOP_K
    buf_tok = jnp.zeros((nb * MOE_BLK,), jnp.int32).at[dest].set(flat_tok)
    buf_w = jnp.zeros((nb * MOE_BLK,), F32).at[dest].set(flat_w)
    block_e = jnp.minimum(jnp.searchsorted(pend, jnp.arange(nb, dtype=jnp.int32) * MOE_BLK, side='right'),
                          N_EXP - 1).astype(jnp.int32)
    ys = moe_experts(t, buf_tok, buf_w, block_e, mp['w1'].astype(BF16), mp['w3'].astype(BF16),
                     mp['w2'].astype(BF16))
    return moe_combine(ys, dest.astype(jnp.int32), sh, x, M, g_post, mods)


def kernel(x, c, ctx, c_ctx, mod_w, mod_b, norm_mix_pre, norm_mix_post, norm_ffn_pre, norm_ffn_post, router_w, router_bias, expert_w1, expert_w3, expert_w2, shared_w1, shared_w3, shared_w2, ev_w_in, ev_w_out, ssd_conv_w, ssd_conv_b, ssd_dt_bias, ssd_a_log, ssd_d, ssd_norm_w, hy_conv_w, hy_conv_b, hy_mlp_w0, hy_mlp_b0, hy_freq0, hy_mlp_w1, hy_mlp_b1, hy_freq1, hy_mlp_w2, hy_bias, od_w_in, od_w_out, rw_mu, rw_w0, rw_w_up, rw_a0, rw_a_up, rw_g_up, rw_k_k, rw_k_a, rw_r_k, rw_ln_w, rw_ln_b, at_q_norm, at_k_norm):
    xs = jnp.concatenate([x.reshape(N_LAT, D), ctx.reshape(BATCH * CTX, D)], axis=0)
    cvecs = jnp.zeros((8, D), F32).at[0:BATCH].set(c).at[BATCH].set(c_ctx)
    assert mod_w.shape[0] == 2, "one even (SSD | Hyena) layer followed by one odd (RWKV | attention) layer"

    def moe_params(i):
        return dict(router_w=router_w[i], router_bias=router_bias[i], w1=expert_w1[i], w3=expert_w3[i],
                    w2=expert_w2[i], s1=shared_w1[i], s3=shared_w3[i], s2=shared_w2[i])

    mods = modulation(cvecs, mod_w[0], mod_b[0])[:BATCH + 1].reshape(BATCH + 1, 6, D)
    ep = dict(w_in=ev_w_in[0], w_out=ev_w_out[0], ssd_conv_w=ssd_conv_w[0], ssd_conv_b=ssd_conv_b[0],
              ssd_dt_bias=ssd_dt_bias[0], ssd_a_log=ssd_a_log[0], ssd_d=ssd_d[0], ssd_norm_w=ssd_norm_w[0],
              hy_conv_w=hy_conv_w[0], hy_conv_b=hy_conv_b[0], hy_mlp_w0=hy_mlp_w0[0], hy_mlp_b0=hy_mlp_b0[0],
              hy_freq0=hy_freq0[0], hy_mlp_w1=hy_mlp_w1[0], hy_mlp_b1=hy_mlp_b1[0], hy_freq1=hy_freq1[0],
              hy_mlp_w2=hy_mlp_w2[0], hy_bias=hy_bias[0])
    xs = even_mixer(xs, mods, norm_mix_pre[0], norm_mix_post[0], ep)
    xs = moe_layer(xs, N_ROWS, norm_ffn_pre[0], norm_ffn_post[0], mods, moe_params(0))
    mods = modulation(cvecs, mod_w[1], mod_b[1])[:BATCH + 1].reshape(BATCH + 1, 6, D)
    op = dict(w_in=od_w_in[0], w_out=od_w_out[0], mu=rw_mu[0], w0=rw_w0[0], w_up=rw_w_up[0], a0=rw_a0[0],
              a_up=rw_a_up[0], g_up=rw_g_up[0], k_k=rw_k_k[0], k_a=rw_k_a[0], r_k=rw_r_k[0], ln_w=rw_ln_w[0],
              ln_b=rw_ln_b[0], q_norm=at_q_norm[0], k_norm=at_k_norm[0])
    xl = odd_mixer(xs, mods, norm_mix_pre[1], norm_mix_post[1], op)
    xl = moe_layer(xl, N_LAT, norm_ffn_pre[1], norm_ffn_post[1], mods, moe_params(1))
    return xl.reshape(BATCH, SEQ, D)
```

```python
import functools
import math

import numpy as np
import jax
import jax.numpy as jnp
from jax import lax
from jax.experimental import pallas as pl
from jax.experimental.pallas import tpu as pltpu

F32 = jnp.float32
BF16 = jnp.bfloat16

D = 1024
BATCH = 2
SEQ = 8192
CTX = 256
N_LAT = BATCH * SEQ
N_ROWS = N_LAT + BATCH * CTX
EPS = 1e-6
GRID_W = 64

SSD_HEADS = 16
SSD_P = 64
SSD_G = 2
SSD_S = 128
SSD_Q = 128
HY_W = 1024
HY_EMB = 33
HY_HID = 64

RW_H = 16
RW_N = 64
RW_CHUNK = 64
RW_GN_EPS = 64e-5

AT_KV = 4
AT_HD = 64

N_EXP = 64
TOP_K = 8
N_GRP = 8
TOPK_GRP = 4
EXP_FF = 256
ROUTED_SCALE = 2.5
MOE_BLK = 128

VMEM_LIMIT = 56 * 1024 * 1024


def _cp(sem, vmem=None):
    return pltpu.CompilerParams(dimension_semantics=sem, vmem_limit_bytes=vmem or VMEM_LIMIT)


def _dot(a, b):
    return jnp.dot(a, b, preferred_element_type=F32)


def _dot_nt(a, b):
    return lax.dot_general(a, b, (((1,), (1,)), ((), ())), preferred_element_type=F32)


def _split(x):
    hi = x.astype(BF16)
    lo = (x - hi.astype(F32)).astype(BF16)
    return hi, lo


def _dot3(a, b):
    ah, al = _split(a)
    bh, bl = _split(b)
    return _dot(ah, bh) + (_dot(ah, bl) + _dot(al, bh))


def _dot2l(a, b):
    ah, al = _split(a)
    return _dot(ah, b) + _dot(al, b)


def _dot2r(a, b):
    bh, bl = _split(b)
    return _dot(a, bh) + _dot(a, bl)


def _silu(x):
    return x * (1.0 / (1.0 + jnp.exp(-x)))


def _sigmoid(x):
    return 1.0 / (1.0 + jnp.exp(-x))


def _softplus(x):
    return jnp.maximum(x, 0.0) + jnp.log(1.0 + jnp.exp(-jnp.abs(x)))


def _seq_of_rowblock(i, tm):
    return jnp.minimum((i * tm) // SEQ, 2)


def _mm_kernel(a_ref, b_ref, o_ref, *, passes):
    a = a_ref[...]
    b = b_ref[...]
    if passes == 3:
        o_ref[...] = _dot3(a.astype(F32), b.astype(F32))
    else:
        o_ref[...] = _dot(a.astype(BF16), b.astype(BF16))


def matmul(a, b, tm, tn, passes=1, name="mm"):
    M, K = a.shape
    N = b.shape[1]
    return pl.pallas_call(
        functools.partial(_mm_kernel, passes=passes),
        out_shape=jax.ShapeDtypeStruct((M, N), F32),
        grid=(M // tm, N // tn),
        in_specs=[pl.BlockSpec((tm, K), lambda i, j: (i, 0)),
                  pl.BlockSpec((K, tn), lambda i, j: (0, j))],
        out_specs=pl.BlockSpec((tm, tn), lambda i, j: (i, j)),
        compiler_params=_cp(("parallel", "parallel")), name=name)(a, b)


def _nmm_kernel(x_ref, g_ref, mod_ref, w_ref, o_ref, a_sc, *, shift_i, scale_i):
    @pl.when(pl.program_id(1) == 0)
    def _():
        x = x_ref[...]
        ms = jnp.mean(x * x, axis=-1, keepdims=True)
        y = x * lax.rsqrt(ms + EPS) * g_ref[...]
        h = y * (1.0 + mod_ref[scale_i:scale_i + 1, :]) + mod_ref[shift_i:shift_i + 1, :]
        a_sc[...] = h.astype(BF16)

    o_ref[...] = _dot(a_sc[...], w_ref[...])


def norm_mod_matmul(x, g, mods, w, shift_i, scale_i, tm=512, tn=None, name="nmm"):
    M = x.shape[0]
    N = w.shape[1]
    tn = tn or N
    return pl.pallas_call(
        functools.partial(_nmm_kernel, shift_i=shift_i, scale_i=scale_i),
        out_shape=jax.ShapeDtypeStruct((M, N), F32),
        grid=(M // tm, N // tn),
        in_specs=[pl.BlockSpec((tm, D), lambda i, j: (i, 0)),
                  pl.BlockSpec((1, D), lambda i, j: (0, 0)),
                  pl.BlockSpec((None, 6, D), lambda i, j: (_seq_of_rowblock(i, tm), 0, 0)),
                  pl.BlockSpec((D, tn), lambda i, j: (0, j))],
        out_specs=pl.BlockSpec((tm, tn), lambda i, j: (i, j)),
        scratch_shapes=[pltpu.VMEM((tm, D), BF16)],
        compiler_params=_cp(("parallel", "arbitrary")), name=name)(x, g.reshape(1, D), mods, w)


def _outproj_kernel(a1_ref, a2_ref, w_ref, x_ref, g_ref, mod_ref, o_ref, *, gate_i):
    y = _dot(a1_ref[...].astype(BF16), w_ref[0:D, :]) + _dot(a2_ref[...].astype(BF16), w_ref[D:2 * D, :])
    ms = jnp.mean(y * y, axis=-1, keepdims=True)
    o_ref[...] = x_ref[...] + mod_ref[gate_i:gate_i + 1, :] * (y * lax.rsqrt(ms + EPS) * g_ref[...])


def outproj_residual(a1, a2, w, x, g, mods, gate_i, tm=256, name="outproj"):
    M = a1.shape[0]
    return pl.pallas_call(
        functools.partial(_outproj_kernel, gate_i=gate_i),
        out_shape=jax.ShapeDtypeStruct((M, D), F32),
        grid=(M // tm,),
        in_specs=[pl.BlockSpec((tm, D), lambda i: (i, 0)),
                  pl.BlockSpec((tm, D), lambda i: (i, 0)),
                  pl.BlockSpec((2 * D, D), lambda i: (0, 0)),
                  pl.BlockSpec((tm, D), lambda i: (i, 0)),
                  pl.BlockSpec((1, D), lambda i: (0, 0)),
                  pl.BlockSpec((None, 6, D), lambda i: (_seq_of_rowblock(i, tm), 0, 0))],
        out_specs=pl.BlockSpec((tm, D), lambda i: (i, 0)),
        compiler_params=_cp(("parallel",)), name=name)(a1, a2, w, x, g.reshape(1, D), mods)


def _mod_kernel(c_ref, w_ref, b_ref, o_ref):
    o_ref[...] = _dot3(_silu(c_ref[...]), w_ref[...]) + b_ref[...]


def modulation(cvecs, w, b):
    N = w.shape[1]
    tn = 1024
    return pl.pallas_call(
        _mod_kernel, out_shape=jax.ShapeDtypeStruct((8, N), F32), grid=(N // tn,),
        in_specs=[pl.BlockSpec((8, D), lambda j: (0, 0)),
                  pl.BlockSpec((D, tn), lambda j: (0, j)),
                  pl.BlockSpec((1, tn), lambda j: (0, j))],
        out_specs=pl.BlockSpec((8, tn), lambda j: (0, j)),
        compiler_params=_cp(("parallel",)), name="modulation")(cvecs, w, b.reshape(1, N))


def _neighbours(x_ref, base, ch, length):
    cur = x_ref[pl.ds(base, ch), :]
    rows = lax.broadcasted_iota(jnp.int32, cur.shape, 0)
    pbase = pl.multiple_of(jnp.maximum(base - 8, 0), 8)
    nbase = pl.multiple_of(jnp.minimum(base + ch, length - 8), 8)
    prev_row = x_ref[pl.ds(pbase, 8), :][7:8, :] * (base > 0).astype(F32)
    next_row = x_ref[pl.ds(nbase, 8), :][0:1, :] * (base + ch < length).astype(F32)
    xm1 = jnp.where(rows == 0, prev_row, pltpu.roll(cur, 1, 0))
    xp1 = jnp.where(rows == ch - 1, next_row, pltpu.roll(cur, ch - 1, 0))
    return xm1, cur, xp1


def _conv3_kernel(x_ref, w_ref, b_ref, o_ref, *, length, ch, act):
    def body(c, carry):
        base = pl.multiple_of(c * ch, ch)
        xm1, cur, xp1 = _neighbours(x_ref, base, ch, length)
        y = xm1 * w_ref[0:1, :] + cur * w_ref[1:2, :] + xp1 * w_ref[2:3, :] + b_ref[...]
        if act:
            y = _silu(y)
        o_ref[pl.ds(base, ch), :] = y
        return carry

    lax.fori_loop(0, length // ch, body, 0)


def dwconv3(p, col0, ncols, w, b, act, name, cb=256):
    wt = jnp.transpose(w)
    b2 = b.reshape(1, ncols)
    outs = []
    for (length, row0, ch) in ((SEQ, 0, 512), (CTX, N_LAT, 256)):
        rb0 = row0 // length
        outs.append(pl.pallas_call(
            functools.partial(_conv3_kernel, length=length, ch=ch, act=act),
            out_shape=jax.ShapeDtypeStruct((BATCH * length, ncols), F32),
            grid=(BATCH, ncols // cb),
            in_specs=[pl.BlockSpec((length, cb), lambda s, j: (rb0 + s, col0 // cb + j)),
                      pl.BlockSpec((3, cb), lambda s, j: (0, j)),
                      pl.BlockSpec((1, cb), lambda s, j: (0, j))],
            out_specs=pl.BlockSpec((length, cb), lambda s, j: (s, j)),
            compiler_params=_cp(("parallel", "parallel")), name=name)(p, wt, b2))
    return jnp.concatenate(outs, axis=0)


def _ssd_kernel(xs_ref, bm_ref, cm_ref, dt_ref, dtT_ref, bias_ref, biasT_ref, alog_ref, alogT_ref,
                y_ref, st_ref):
    d = pl.program_id(0)
    c = pl.program_id(3)
    Q = SSD_Q
    HG = SSD_HEADS // SSD_G

    @pl.when(c == 0)
    def _():
        st_ref[...] = jnp.zeros_like(st_ref)

    isb = d == 1
    sgn = 1 - 2 * d
    dt = _softplus(dt_ref[...] + bias_ref[...])
    dtT = _softplus(dtT_ref[...] + biasT_ref[...])
    a = dt * (-jnp.exp(alog_ref[...]))
    aT = dtT * (-jnp.exp(alogT_ref[...]))
    ii = lax.broadcasted_iota(jnp.int32, (Q, Q), 0)
    jj = lax.broadcasted_iota(jnp.int32, (Q, Q), 1)
    tri = (jj <= ii).astype(BF16)
    triT = (ii <= jj).astype(BF16)
    cs = _dot2r(tri, a)
    csT = _dot2l(aT, triT)
    tot = cs[Q - 1:Q, :]
    p = jnp.where(isb, a - cs, cs)
    pT = jnp.where(isb, aT - csT, csT)
    dec_out = jnp.exp(jnp.where(isb, tot, 0.0) + p)
    dec_state = jnp.exp(jnp.where(isb, 0.0, tot) - p)
    chunk_dec = jnp.exp(tot)
    mask = sgn * (ii - jj) >= 0
    bm = bm_ref[...].astype(BF16)
    cm = cm_ref[...].astype(BF16)
    cb = _dot_nt(cm, bm)
    xs = xs_ref[...]
    H = range(HG)
    g = [(cb * jnp.exp(jnp.where(mask, p[:, h:h + 1] - pT[h:h + 1, :], -1e30))).astype(BF16) for h in H]
    xh = [xs[:, h * SSD_P:(h + 1) * SSD_P] * dt[:, h:h + 1] for h in H]
    s_old = [st_ref[h] for h in H]
    y_in = [_dot(g[h], xh[h].astype(BF16)) for h in H]
    y_st = [_dot(cm, s_old[h].astype(BF16)) for h in H]
    upd = [lax.dot_general(bm, (xh[h] * dec_state[:, h:h + 1]).astype(BF16), (((0,), (0,)), ((), ())),
                           preferred_element_type=F32) for h in H]
    for h in H:
        st_ref[h] = chunk_dec[:, h:h + 1] * s_old[h] + upd[h]
    y_ref[...] = jnp.concatenate([y_in[h] + dec_out[:, h:h + 1] * y_st[h] for h in H], axis=1)


def _ssd_rowblock(d, b, c):
    n_ctx = CTX // SSD_Q
    n_lat = SEQ // SSD_Q
    cc = jnp.where(d == 0, c, n_ctx - 1 - c)
    lc = jnp.where(d == 0, c - n_ctx, n_ctx + n_lat - 1 - c)
    return jnp.where(c < n_ctx, N_LAT // SSD_Q + b * n_ctx + cc, b * n_lat + lc)


def ssd_scan(xbc, dt_raw, dt_bias, a_log):
    HG = SSD_HEADS // SSD_G
    GW = HG * SSD_P
    dsel = dt_raw[:, :2 * SSD_HEADS].reshape(N_ROWS, 2, SSD_G, HG).transpose(1, 2, 0, 3)
    dselT = dsel.transpose(0, 1, 3, 2)
    bias = dt_bias.reshape(2, SSD_G, 1, HG)
    biasT = dt_bias.reshape(2, SSD_G, HG, 1)
    alog = a_log.reshape(2, SSD_G, 1, HG)
    alogT = a_log.reshape(2, SSD_G, HG, 1)
    nch = (CTX + SEQ) // SSD_Q
    rb = lambda d, b, g, c: _ssd_rowblock(d, b, c)
    bcol = SSD_HEADS * SSD_P // SSD_S
    return pl.pallas_call(
        _ssd_kernel,
        out_shape=jax.ShapeDtypeStruct((2, N_ROWS, SSD_HEADS * SSD_P), F32),
        grid=(2, BATCH, SSD_G, nch),
        in_specs=[pl.BlockSpec((SSD_Q, GW), lambda d, b, g, c: (rb(d, b, g, c), g)),
                  pl.BlockSpec((SSD_Q, SSD_S), lambda d, b, g, c: (rb(d, b, g, c), bcol + g)),
                  pl.BlockSpec((SSD_Q, SSD_S), lambda d, b, g, c: (rb(d, b, g, c), bcol + SSD_G + g)),
                  pl.BlockSpec((None, None, SSD_Q, HG), lambda d, b, g, c: (d, g, rb(d, b, g, c), 0)),
                  pl.BlockSpec((None, None, HG, SSD_Q), lambda d, b, g, c: (d, g, 0, rb(d, b, g, c))),
                  pl.BlockSpec((None, None, 1, HG), lambda d, b, g, c: (d, g, 0, 0)),
                  pl.BlockSpec((None, None, HG, 1), lambda d, b, g, c: (d, g, 0, 0)),
                  pl.BlockSpec((None, None, 1, HG), lambda d, b, g, c: (d, g, 0, 0)),
                  pl.BlockSpec((None, None, HG, 1), lambda d, b, g, c: (d, g, 0, 0))],
        out_specs=pl.BlockSpec((None, SSD_Q, GW), lambda d, b, g, c: (d, rb(d, b, g, c), g)),
        scratch_shapes=[pltpu.VMEM((HG, SSD_S, SSD_P), F32)],
        compiler_params=_cp(("parallel", "parallel", "parallel", "arbitrary")), name="ssd_scan",
    )(xbc, xbc, xbc, dsel, dselT, bias, biasT, alog, alogT)


def _ssd_out_kernel(yf_ref, yb_ref, xs_ref, z_ref, dskip_ref, nw_ref, o_ref):
    y = yf_ref[...] + yb_ref[...] + xs_ref[...] * dskip_ref[...]
    y = y * _silu(z_ref[...])
    gs = SSD_HEADS * SSD_P // SSD_G
    parts = []
    for g in range(SSD_G):
        yg = y[:, g * gs:(g + 1) * gs]
        parts.append(yg * lax.rsqrt(jnp.mean(yg * yg, axis=-1, keepdims=True) + EPS))
    o_ref[...] = jnp.concatenate(parts, axis=1) * nw_ref[...]


def ssd_output(y2, xbc, p, d_skip, norm_w, tm=256):
    W = SSD_HEADS * SSD_P
    dexp = jnp.repeat(d_skip, SSD_P).reshape(1, W)
    return pl.pallas_call(
        _ssd_out_kernel, out_shape=jax.ShapeDtypeStruct((N_ROWS, W), F32), grid=(N_ROWS // tm,),
        in_specs=[pl.BlockSpec((None, tm, W), lambda i: (0, i, 0)),
                  pl.BlockSpec((None, tm, W), lambda i: (1, i, 0)),
                  pl.BlockSpec((tm, W), lambda i: (i, 0)),
                  pl.BlockSpec((tm, W), lambda i: (i, 0)),
                  pl.BlockSpec((1, W), lambda i: (0, 0)),
                  pl.BlockSpec((1, W), lambda i: (0, 0))],
        out_specs=pl.BlockSpec((tm, W), lambda i: (i, 0)),
        compiler_params=_cp(("parallel",)), name="ssd_output")(y2, y2, xbc, p, dexp, norm_w.reshape(1, W))


def _hyfilt_kernel(f_ref, w0_ref, b0_ref, fr0_ref, w1_ref, b1_ref, fr1_ref, w2_ref, dl_ref, h_ref, ss_ref):
    f = f_ref[...]
    h = jnp.sin(fr0_ref[...] * (_dot3(f, w0_ref[...]) + b0_ref[...]))
    h = jnp.sin(fr1_ref[...] * (_dot3(h, w1_ref[...]) + b1_ref[...]))
    h = _dot3(h, w2_ref[...])
    h = h * jnp.exp(-f[:, 0:1] * dl_ref[...])
    h_ref[...] = h

    @pl.when(pl.program_id(0) == 0)
    def _():
        ss_ref[...] = jnp.zeros_like(ss_ref)

    ss_ref[...] += jnp.sum(h * h, axis=0, keepdims=True)


def hyena_filter_raw(L, hp):
    pos = jnp.arange(L, dtype=F32)
    t = pos / (L - 1)
    bands = (HY_EMB - 1) // 2
    freqs = jnp.linspace(1e-4, bands - 1, bands, dtype=F32)
    ang = (2.0 * math.pi / L) * pos[:, None] * freqs[None, :]
    feats = jnp.concatenate([t[:, None], jnp.cos(ang), -jnp.sin(ang)], axis=-1)
    feats = jnp.pad(feats, ((0, 0), (0, 128 - HY_EMB)))
    w0 = jnp.pad(hp['hy_mlp_w0'], ((0, 128 - HY_EMB), (0, 0)))
    min_decay = math.log(1e-2) / 1.5
    max_decay = math.log(1e-2) / 0.3
    deltas = jnp.abs(jnp.linspace(min_decay, max_decay, HY_W, dtype=F32))
    dl = jnp.tile(deltas, 4).reshape(1, 4 * HY_W)
    tl = min(L, 512)
    NF = 4 * HY_W
    full = lambda shape: pl.BlockSpec(shape, lambda i: (0, 0))
    return pl.pallas_call(
        _hyfilt_kernel,
        out_shape=(jax.ShapeDtypeStruct((L, NF), F32), jax.ShapeDtypeStruct((1, NF), F32)),
        grid=(L // tl,),
        in_specs=[pl.BlockSpec((tl, 128), lambda i: (i, 0)), full((128, HY_HID)), full((1, HY_HID)),
                  full((1, HY_HID)), full((HY_HID, HY_HID)), full((1, HY_HID)), full((1, HY_HID)),
                  full((HY_HID, NF)), full((1, NF))],
        out_specs=(pl.BlockSpec((tl, NF), lambda i: (i, 0)), pl.BlockSpec((1, NF), lambda i: (0, 0))),
        compiler_params=_cp(("arbitrary",)), name="hyena_filter",
    )(feats, w0, hp['hy_mlp_b0'].reshape(1, -1), hp['hy_freq0'].reshape(1, -1), hp['hy_mlp_w1'],
      hp['hy_mlp_b1'].reshape(1, -1), hp['hy_freq1'].reshape(1, -1), hp['hy_mlp_w2'], dl)


def _cis(num, den):
    ang = (2.0 * math.pi / den) * (num % den).astype(F32)
    return jnp.cos(ang), -jnp.sin(ang)


def _fft_consts(NB, BS):
    N = NB * BS
    h = NB // 2
    k1 = jnp.arange(h, dtype=jnp.int32)
    j = jnp.arange(NB, dtype=jnp.int32)
    re, im = _cis(j[None, :] * (2 * k1[:, None] + 1), 2 * NB)
    f1 = jnp.concatenate([re, im], axis=0)
    neg = jnp.where(j >= h, -1.0, 1.0)[None, :]
    f1_data = f1[:, :h]
    f1_filt = f1 * neg
    f1_inv = (2.0 / N) * jnp.concatenate([re[:, :h].T, im[:, :h].T], axis=1)
    r = jnp.arange(BS, dtype=jnp.int32)
    k2 = jnp.arange(BS, dtype=jnp.int32)
    kk = 2 * k1[:, None, None] + 2 * NB * k2[None, :, None] + 1
    gre, gim = _cis(kk * r[None, None, :], 2 * N)
    gf = jnp.concatenate([jnp.concatenate([gre, -gim], axis=2), jnp.concatenate([gim, gre], axis=2)], axis=1)
    gret, gimt = gre.transpose(0, 2, 1), gim.transpose(0, 2, 1)
    gi = jnp.concatenate([jnp.concatenate([gret, gimt], axis=2), jnp.concatenate([-gimt, gret], axis=2)], axis=1)
    return (f1_data.astype(BF16), f1_filt.astype(BF16), f1_inv.astype(BF16), gf.astype(BF16), gi.astype(BF16))


def _fft_fwd_kernel(u_ref, f1_ref, g_ref, o_ref, t_sc, *, NB, BS, nj, kg):
    @pl.when(pl.program_id(2) == 0)
    def _():
        f1 = f1_ref[...]

        def body(r, carry):
            xr = u_ref[pl.ds(r, nj, stride=BS), :].astype(BF16)
            t_sc[pl.ds(pl.multiple_of(r * NB, NB), NB), :] = _dot(f1, xr)
            return carry

        lax.fori_loop(0, BS, body, 0)

    k0 = pl.program_id(2) * kg
    for i in range(kg):
        are = t_sc[pl.ds(k0 + i, BS, stride=NB), :]
        aim = t_sc[pl.ds(k0 + i + NB // 2, BS, stride=NB), :]
        a = jnp.concatenate([are, aim], axis=0).astype(BF16)
        o_ref[i] = _dot(g_ref[i], a)


def fft_fwd(u, col0, nbatch, nj, f1, gf, NB, BS, ct=128, kg=8):
    h = NB // 2
    kg = min(kg, h)
    return pl.pallas_call(
        functools.partial(_fft_fwd_kernel, NB=NB, BS=BS, nj=nj, kg=kg),
        out_shape=jax.ShapeDtypeStruct((nbatch, h, 2 * BS, HY_W), F32),
        grid=(nbatch, HY_W // ct, h // kg),
        in_specs=[pl.BlockSpec((nj * BS, ct), lambda b, c, k: (b, col0 // ct + c)),
                  pl.BlockSpec((NB, nj), lambda b, c, k: (0, 0)),
                  pl.BlockSpec((kg, 2 * BS, 2 * BS), lambda b, c, k: (k, 0, 0))],
        out_specs=pl.BlockSpec((None, kg, 2 * BS, ct), lambda b, c, k: (b, k, 0, c)),
        scratch_shapes=[pltpu.VMEM((BS * NB, ct), F32)],
        compiler_params=_cp(("parallel", "parallel", "arbitrary")), name="hyena_fft_fwd")(u, f1, gf)


def _cmul(u, h, half):
    ure, uim = u[:half], u[half:]
    hre, him = h[:half], h[half:]
    return jnp.concatenate([ure * hre - uim * him, ure * him + uim * hre], axis=0)


def _fft_inv_kernel(us_ref, hs_ref, gi_ref, f1i_ref, o_ref, t_sc, *, NB, BS, kg):
    ks = pl.program_id(2)
    for i in range(kg):
        y = _cmul(us_ref[i], hs_ref[i], BS).astype(BF16)
        row = pl.multiple_of((ks * kg + i) * 2 * BS, 2 * BS)
        t_sc[pl.ds(row, 2 * BS), :] = _dot(gi_ref[i], y)

    @pl.when(ks == pl.num_programs(2) - 1)
    def _():
        f1i = f1i_ref[...]

        def body(r, carry):
            bre = t_sc[pl.ds(r, NB // 2, stride=2 * BS), :]
            bim = t_sc[pl.ds(r + BS, NB // 2, stride=2 * BS), :]
            b = jnp.concatenate([bre, bim], axis=0).astype(BF16)
            o_ref[pl.ds(r, NB // 2, stride=BS), :] = _dot(f1i, b)
            return carry

        lax.fori_loop(0, BS, body, 0)


def fft_inv(us, hs, gi, f1i, NB, BS, ct=128, kg=8):
    nbatch, h = us.shape[0], NB // 2
    kg = min(kg, h)
    L = h * BS
    return pl.pallas_call(
        functools.partial(_fft_inv_kernel, NB=NB, BS=BS, kg=kg),
        out_shape=jax.ShapeDtypeStruct((nbatch * L, HY_W), F32),
        grid=(nbatch, HY_W // ct, h // kg),
        in_specs=[pl.BlockSpec((None, kg, 2 * BS, ct), lambda b, c, k: (b, k, 0, c)),
                  pl.BlockSpec((None, kg, 2 * BS, ct), lambda b, c, k: (0, k, 0, c)),
                  pl.BlockSpec((kg, 2 * BS, 2 * BS), lambda b, c, k: (k, 0, 0)),
                  pl.BlockSpec((h, NB), lambda b, c, k: (0, 0))],
        out_specs=pl.BlockSpec((L, ct), lambda b, c, k: (b, c)),
        scratch_shapes=[pltpu.VMEM((h * 2 * BS, ct), F32)],
        compiler_params=_cp(("parallel", "parallel", "arbitrary")), name="hyena_fft_inv")(us, hs, gi, f1i)


def _dft_consts(L):
    N = 2 * L
    k = jnp.arange(L, dtype=jnp.int32)
    n = jnp.arange(N, dtype=jnp.int32)
    re, im = _cis(n[None, :] * (2 * k[:, None] + 1), 2 * N)
    f = jnp.concatenate([re, im], axis=0)
    neg = jnp.where(n >= L, -1.0, 1.0)[None, :]
    fi = (2.0 / N) * jnp.concatenate([re[:, :L].T, im[:, :L].T], axis=1)
    return f[:, :L].astype(BF16), (f * neg).astype(BF16), fi.astype(BF16)


def _cdft_kernel(f_ref, x_ref, o_ref):
    o_ref[...] = _dot(f_ref[...], x_ref[...].astype(BF16))


def dft_fwd(x, f, row0, col0, nbatch, ct=256):
    M, K = f.shape
    return pl.pallas_call(
        _cdft_kernel, out_shape=jax.ShapeDtypeStruct((nbatch, M, HY_W), F32),
        grid=(nbatch, HY_W // ct),
        in_specs=[pl.BlockSpec((M, K), lambda b, c: (0, 0)),
                  pl.BlockSpec((K, ct), lambda b, c: (row0 // K + b, col0 // ct + c))],
        out_specs=pl.BlockSpec((None, M, ct), lambda b, c: (b, 0, c)),
        compiler_params=_cp(("parallel", "parallel")), name="hyena_dft_fwd")(f, x)


def _cdft_inv_kernel(us_ref, hs_ref, fi_ref, o_ref):
    half = us_ref.shape[0] // 2
    o_ref[...] = _dot(fi_ref[...], _cmul(us_ref[...], hs_ref[...], half).astype(BF16))


def dft_inv(us, hs, fi, ct=256):
    nbatch, M2, _ = us.shape
    L = fi.shape[0]
    return pl.pallas_call(
        _cdft_inv_kernel, out_shape=jax.ShapeDtypeStruct((nbatch * L, HY_W), F32),
        grid=(nbatch, HY_W // ct),
        in_specs=[pl.BlockSpec((None, M2, ct), lambda b, c: (b, 0, c)),
                  pl.BlockSpec((None, M2, ct), lambda b, c: (0, 0, c)),
                  pl.BlockSpec((L, M2), lambda b, c: (0, 0))],
        out_specs=pl.BlockSpec((L, ct), lambda b, c: (b, c)),
        compiler_params=_cp(("parallel", "parallel")), name="hyena_dft_inv")(us, hs, fi)


def _hy_gate_kernel(g_ref, y_ref, u_ref, ss_ref, b_ref, o_ref):
    scale = lax.rsqrt(ss_ref[0:1, :] + ss_ref[1:2, :] + 1e-6)
    o_ref[...] = g_ref[...] * (y_ref[...] * scale + u_ref[...] * b_ref[...])


def hy_gate(gate, gcol, grow, y, uin, ucol, urow, ss, order, bias, tm=256):
    M = y.shape[0]
    return pl.pallas_call(
        _hy_gate_kernel, out_shape=jax.ShapeDtypeStruct((M, HY_W), F32), grid=(M // tm,),
        in_specs=[pl.BlockSpec((tm, HY_W), lambda i: (grow // tm + i, gcol // HY_W)),
                  pl.BlockSpec((tm, HY_W), lambda i: (i, 0)),
                  pl.BlockSpec((tm, HY_W), lambda i: (urow // tm + i, ucol // HY_W)),
                  pl.BlockSpec((None, 2, HY_W), lambda i: (order, 0, 0)),
                  pl.BlockSpec((None, 1, HY_W), lambda i: (order, 0, 0))],
        out_specs=pl.BlockSpec((tm, HY_W), lambda i: (i, 0)),
        compiler_params=_cp(("parallel",)), name="hyena_gate")(gate, y, uin, ss, bias)


def _filter_taps(hraw, L, order):
    h4 = hraw.reshape(L, 2, 2, HY_W)
    fwd, bwd = h4[:, order, 0], h4[:, order, 1]
    return jnp.concatenate([fwd, jnp.zeros((1, HY_W), F32), jnp.flip(bwd[:L - 1], axis=0)], axis=0)


def hyena(u, hp):
    C = HY_W
    bias = hp['hy_bias'].reshape(2, 1, C)
    NB = BS = int(round(math.sqrt(2 * SEQ)))
    f1d, f1f, f1i, gf, gi = _fft_consts(NB, BS)
    hraw, ss = hyena_filter_raw(SEQ, hp)
    ss = ss.reshape(2, 2, C)
    zin, zcol = u, 2 * C
    for order in range(2):
        taps = _filter_taps(hraw, SEQ, order)
        hs = fft_fwd(taps, 0, 1, NB, f1f, gf, NB, BS)
        us = fft_fwd(zin, zcol, BATCH, NB // 2, f1d, gf, NB, BS)
        y = fft_inv(us, hs, gi, f1i, NB, BS)
        zin = hy_gate(u, order * C, 0, y, zin, zcol, 0, ss, order, bias)
        zcol = 0
    z_lat = zin
    fd, ff, fi = _dft_consts(CTX)
    hraw, ss = hyena_filter_raw(CTX, hp)
    ss = ss.reshape(2, 2, C)
    zin, zcol, zrow = u, 2 * C, N_LAT
    for order in range(2):
        taps = _filter_taps(hraw, CTX, order)
        hs = dft_fwd(taps, ff, 0, 0, 1)
        us = dft_fwd(zin, fd, zrow, zcol, BATCH)
        y = dft_inv(us, hs, fi)
        zin = hy_gate(u, order * C, N_LAT, y, zin, zcol, zrow, ss, order, bias)
        zcol, zrow = 0, 0
    return jnp.concatenate([z_lat, zin], axis=0)


EV_SSD_IN = SSD_HEADS * SSD_P
EV_XBC = EV_SSD_IN + 2 * SSD_G * SSD_S
EV_PAD_N = 5760


def even_mixer(x, mods, g_pre, g_post, ep):
    o1 = EV_SSD_IN
    o2 = o1 + EV_XBC
    o3 = o2 + 2 * SSD_HEADS
    w = ep['w_in']
    n_in = w.shape[1]
    w_perm = jnp.concatenate([w[:, :o2], w[:, o3:], w[:, o2:o3],
                              jnp.zeros((D, EV_PAD_N - n_in), F32)], axis=1).astype(BF16)
    p = norm_mod_matmul(x, g_pre, mods, w_perm, 0, 1, tn=640, name="even_in_proj")
    xbc = dwconv3(p, o1, EV_XBC, ep['ssd_conv_w'], ep['ssd_conv_b'], True, "ssd_conv")
    u = dwconv3(p, o2, 3 * HY_W, ep['hy_conv_w'], ep['hy_conv_b'], False, "hyena_conv")
    dt_raw = p[:, o2 + 3 * HY_W:o2 + 3 * HY_W + 2 * SSD_HEADS]
    y2 = ssd_scan(xbc, dt_raw, ep['ssd_dt_bias'], ep['ssd_a_log'])
    s = ssd_output(y2, xbc, p, ep['ssd_d'], ep['ssd_norm_w'])
    zh = hyena(u, ep)
    return outproj_residual(s, zh, ep['w_out'].astype(BF16), x, g_post, mods, 2, name="even_out_proj")


def _head_sum(x, e, et):
    return _dot2l(_dot2l(x, e), et)


def _rw_prep_kernel(r_ref, k_ref, v_ref, lo_ref, w0_ref, wup_ref, a0_ref, aup_ref, gup_ref, kk_ref, ka_ref,
                    rk_ref, e_ref, et_ref, lw_ref, kd_ref, be_ref, kap_ref, g_ref, bonus_ref):
    r, k, v = r_ref[...], k_ref[...], v_ref[...]
    lo = lo_ref[...]
    wc, ac, gc = lo[:, 0:64], lo[:, 64:128], lo[:, 128:384]
    e, et = e_ref[...], et_ref[...]
    kk = k * kk_ref[...]
    kap = kk * lax.rsqrt(_head_sum(kk * kk, e, et) + 1e-12)
    kap_ref[...] = kap
    g_ref[...] = _dot3(_sigmoid(gc), gup_ref[...])
    kd_sum = jnp.zeros_like(k)
    for d in range(2):
        wlog = -_softplus(-(w0_ref[d:d + 1, :] + _dot3(jnp.tanh(wc), wup_ref[d]))) - 0.5
        lw_ref[d] = -jnp.exp(wlog)
        a = _sigmoid(a0_ref[d:d + 1, :] + _dot3(ac, aup_ref[d]))
        kd = k * (1.0 + (a - 1.0) * ka_ref[...])
        kd_ref[d] = kd
        be_ref[d] = kap * a
        kd_sum = kd_sum + kd
    bonus_ref[...] = _head_sum(r * kd_sum * rk_ref[...], e, et) * v


def rwkv_prepare(code, lora, op, tm=256):
    W = RW_H * RW_N
    heads = jnp.arange(W, dtype=jnp.int32) // RW_N
    e = (heads[:, None] == jnp.arange(128, dtype=jnp.int32)[None, :]).astype(BF16)
    et = jnp.transpose(e)
    gup = jnp.pad(op['g_up'], ((0, 256 - op['g_up'].shape[0]), (0, 0)))
    row = lambda a: a.reshape(1, W)
    full2 = lambda shape: pl.BlockSpec(shape, lambda i: (0,) * len(shape))
    outs = pl.pallas_call(
        _rw_prep_kernel,
        out_shape=(jax.ShapeDtypeStruct((2, N_ROWS, W), F32), jax.ShapeDtypeStruct((2, N_ROWS, W), F32),
                   jax.ShapeDtypeStruct((2, N_ROWS, W), F32), jax.ShapeDtypeStruct((N_ROWS, W), F32),
                   jax.ShapeDtypeStruct((N_ROWS, W), F32), jax.ShapeDtypeStruct((N_ROWS, W), F32)),
        grid=(N_ROWS // tm,),
        in_specs=[pl.BlockSpec((tm, W), lambda i: (i, 0)), pl.BlockSpec((tm, W), lambda i: (i, 1)),
                  pl.BlockSpec((tm, W), lambda i: (i, 2)), pl.BlockSpec((tm, 384), lambda i: (i, 0)),
                  full2((2, W)), full2((2, 64, W)), full2((2, W)), full2((2, 64, W)), full2((256, W)),
                  full2((1, W)), full2((1, W)), full2((1, W)), full2((W, 128)), full2((128, W))],
        out_specs=(pl.BlockSpec((2, tm, W), lambda i: (0, i, 0)), pl.BlockSpec((2, tm, W), lambda i: (0, i, 0)),
                   pl.BlockSpec((2, tm, W), lambda i: (0, i, 0)), pl.BlockSpec((tm, W), lambda i: (i, 0)),
                   pl.BlockSpec((tm, W), lambda i: (i, 0)), pl.BlockSpec((tm, W), lambda i: (i, 0))),
        compiler_params=_cp(("parallel",)), name="rwkv_prepare",
    )(code, code, code, lora, op['w0'], op['w_up'], op['a0'], op['a_up'], gup, row(op['k_k']), row(op['k_a']),
      row(op['r_k']), e, et)
    return outs


def _tri_inv(n, eye, masks):
    bd = lambda a, b: _dot(a.astype(BF16), b.astype(BF16))
    d0 = jnp.where(masks[0], n, 0.0)
    d2 = bd(d0, d0)
    d4 = bd(d2, d2)
    t = bd(bd(eye + d0, eye + d2), eye + d4)
    for m in masks[1:]:
        e = jnp.where(m, n, 0.0)
        t = t + bd(t, bd(e, t))
    return t


def _rw_scan_kernel(r_ref, v_ref, lw_ref, kd_ref, be_ref, kap_ref, y_ref, st_ref):
    d = pl.program_id(0)
    c = pl.program_id(2)
    C = RW_CHUNK
    N = RW_N

    @pl.when(c == 0)
    def _():
        st_ref[...] = jnp.zeros_like(st_ref)

    isb = d == 1
    sgn = 1 - 2 * d
    ii = lax.broadcasted_iota(jnp.int32, (C, C), 0)
    jj = lax.broadcasted_iota(jnp.int32, (C, C), 1)
    dif = sgn * (ii - jj)
    incl = dif >= 0
    strict = dif > 0
    tri = incl.astype(BF16)
    eye = (ii == jj).astype(F32)
    blk = [(ii >> s) == (jj >> s) for s in (3, 4, 5)]
    masks = [blk[0], blk[1] & ~blk[0], blk[2] & ~blk[1], ~blk[2]]
    lw = lw_ref[...]
    cum = _dot2r(tri, lw)
    ec = jnp.exp(cum)
    en = jnp.exp(-cum)
    ea = jnp.exp(cum - lw)
    last = jnp.where(isb, cum[0:1, :], cum[C - 1:C, :])
    el = jnp.exp(last - cum)
    kap = kap_ref[...]
    r = r_ref[...]
    v = v_ref[...]
    a_t = -kap * ea
    r_t = r * ec
    b_t = be_ref[...] * en
    k_t = kd_ref[...] * en
    b_l = be_ref[...] * el
    k_l = kd_ref[...] * el
    pc = jnp.exp(last)
    H = range(RW_H)
    sl = [slice(h * N, (h + 1) * N) for h in H]
    bd = lambda a, b: _dot(a.astype(BF16), b.astype(BF16))
    tn = lambda a, b: lax.dot_general(a, b, (((0,), (0,)), ((), ())), preferred_element_type=F32)
    sc = [_dot_nt(jnp.concatenate([a_t[:, sl[h]], r_t[:, sl[h]]], axis=0).astype(BF16),
                  jnp.concatenate([b_t[:, sl[h]], k_t[:, sl[h]]], axis=0).astype(BF16)) for h in H]
    n_ab = [jnp.where(strict, sc[h][0:C, 0:C], 0.0) for h in H]
    a_ak = [jnp.where(strict, sc[h][0:C, C:2 * C], 0.0).astype(BF16) for h in H]
    m_rb = [jnp.where(incl, sc[h][C:2 * C, 0:C], 0.0).astype(BF16) for h in H]
    m_rk = [jnp.where(incl, sc[h][C:2 * C, C:2 * C], 0.0).astype(BF16) for h in H]
    vh = [v[:, sl[h]].astype(BF16) for h in H]
    d0 = [jnp.where(masks[0], n_ab[h], 0.0) for h in H]
    d2 = [bd(d0[h], d0[h]) for h in H]
    d4 = [bd(d2[h], d2[h]) for h in H]
    t = [bd(eye + d0[h], eye + d2[h]) for h in H]
    t = [bd(t[h], eye + d4[h]) for h in H]
    for m in masks[1:]:
        et = [bd(jnp.where(m, n_ab[h], 0.0), t[h]) for h in H]
        t = [t[h] + bd(t[h], et[h]) for h in H]
    av = [_dot(a_ak[h], vh[h]) for h in H]
    wub = [bd(t[h], jnp.concatenate([a_t[:, sl[h]], av[h]], axis=1)).astype(BF16) for h in H]
    mv = [_dot(m_rk[h], vh[h]) for h in H]
    kv = [tn(k_l[:, sl[h]].astype(BF16), vh[h]) for h in H]
    qy = [_dot(m_rb[h], wub[h]) + jnp.concatenate([r_t[:, sl[h]], mv[h]], axis=1) for h in H]
    pp = [tn(b_l[:, sl[h]].astype(BF16), wub[h]) + jnp.concatenate([eye * pc[:, sl[h]], kv[h]], axis=1)
          for h in H]
    h_old = [st_ref[h] for h in H]
    ys = [_dot3(qy[h][:, 0:N], h_old[h]) + qy[h][:, N:2 * N] for h in H]
    for h in H:
        st_ref[h] = _dot3(pp[h][:, 0:N], h_old[h]) + pp[h][:, N:2 * N]
    y_ref[...] = jnp.concatenate(ys, axis=1)


def _rw_rowblock(d, b, c):
    n_ctx = CTX // RW_CHUNK
    n_lat = SEQ // RW_CHUNK
    cc = jnp.where(d == 0, c, n_ctx - 1 - c)
    lc = jnp.where(d == 0, c - n_ctx, n_ctx + n_lat - 1 - c)
    return jnp.where(c < n_ctx, N_LAT // RW_CHUNK + b * n_ctx + cc, b * n_lat + lc)


def rwkv_scan(code, lw, kd, be, kap):
    W = RW_H * RW_N
    nch = (CTX + SEQ) // RW_CHUNK
    rb = lambda d, b, c: _rw_rowblock(d, b, c)
    return pl.pallas_call(
        _rw_scan_kernel,
        out_shape=jax.ShapeDtypeStruct((2, N_ROWS, W), F32),
        grid=(2, BATCH, nch),
        in_specs=[pl.BlockSpec((RW_CHUNK, W), lambda d, b, c: (rb(d, b, c), 0)),
                  pl.BlockSpec((RW_CHUNK, W), lambda d, b, c: (rb(d, b, c), 2)),
                  pl.BlockSpec((None, RW_CHUNK, W), lambda d, b, c: (d, rb(d, b, c), 0)),
                  pl.BlockSpec((None, RW_CHUNK, W), lambda d, b, c: (d, rb(d, b, c), 0)),
                  pl.BlockSpec((None, RW_CHUNK, W), lambda d, b, c: (d, rb(d, b, c), 0)),
                  pl.BlockSpec((RW_CHUNK, W), lambda d, b, c: (rb(d, b, c), 0))],
        out_specs=pl.BlockSpec((None, RW_CHUNK, W), lambda d, b, c: (d, rb(d, b, c), 0)),
        scratch_shapes=[pltpu.VMEM((RW_H, RW_N, RW_N), F32)],
        compiler_params=_cp(("parallel", "parallel", "arbitrary")), name="rwkv_scan",
    )(code, code, lw, kd, be, kap)


def _rw_out_kernel(yf_ref, yb_ref, bonus_ref, g_ref, lnw_ref, lnb_ref, e_ref, et_ref, o_ref):
    e, et = e_ref[...], et_ref[...]
    y = yf_ref[...] + yb_ref[...]
    mean = _head_sum(y, e, et) * (1.0 / RW_N)
    yc = y - mean
    var = _head_sum(yc * yc, e, et) * (1.0 / RW_N)
    yn = yc * lax.rsqrt(var + RW_GN_EPS) * lnw_ref[...] + lnb_ref[...]
    o_ref[...] = (yn + bonus_ref[...]) * g_ref[...]


def rwkv_output(y2, bonus, g, op, tm=256):
    W = RW_H * RW_N
    heads = jnp.arange(W, dtype=jnp.int32) // RW_N
    e = (heads[:, None] == jnp.arange(128, dtype=jnp.int32)[None, :]).astype(BF16)
    et = jnp.transpose(e)
    M = N_LAT
    return pl.pallas_call(
        _rw_out_kernel, out_shape=jax.ShapeDtypeStruct((M, W), F32), grid=(M // tm,),
        in_specs=[pl.BlockSpec((None, tm, W), lambda i: (0, i, 0)), pl.BlockSpec((None, tm, W), lambda i: (1, i, 0)),
                  pl.BlockSpec((tm, W), lambda i: (i, 0)), pl.BlockSpec((tm, W), lambda i: (i, 0)),
                  pl.BlockSpec((1, W), lambda i: (0, 0)), pl.BlockSpec((1, W), lambda i: (0, 0)),
                  pl.BlockSpec((W, 128), lambda i: (0, 0)), pl.BlockSpec((128, W), lambda i: (0, 0))],
        out_specs=pl.BlockSpec((tm, W), lambda i: (i, 0)),
        compiler_params=_cp(("parallel",)), name="rwkv_output",
    )(y2, y2, bonus, g, op['ln_w'].reshape(1, W), op['ln_b'].reshape(1, W), e, et)


AT_Q = RW_H * AT_HD
AT_KW = AT_KV * AT_HD
AT_TQ = 512
AT_TK = 256


def _rope_tables(tm):
    half = AT_HD // 2
    inv = 10000.0 ** (-jnp.arange(0, half, 2, dtype=F32) / half)
    pos = jnp.arange(SEQ, dtype=jnp.int32)
    row = (pos // GRID_W).astype(F32)[:, None] * inv
    col = (pos % GRID_W).astype(F32)[:, None] * inv
    cos_h = jnp.concatenate([jnp.cos(row), jnp.cos(row), jnp.cos(col), jnp.cos(col)], axis=1)
    sin_h = jnp.concatenate([-jnp.sin(row), jnp.sin(row), -jnp.sin(col), jnp.sin(col)], axis=1)
    cos_t = jnp.concatenate([jnp.tile(cos_h, (1, 2)), jnp.ones((tm, 128), F32)], axis=0)
    sin_t = jnp.concatenate([jnp.tile(sin_h, (1, 2)), jnp.zeros((tm, 128), F32)], axis=0)
    return cos_t, sin_t


def _rot_partner(x):
    q = AT_HD // 4
    w = x.shape[1]
    lane = lax.broadcasted_iota(jnp.int32, x.shape, 1)
    return jnp.where((lane % (2 * q)) < q, pltpu.roll(x, w - q, 1), pltpu.roll(x, q, 1))


def _at_prep_kernel(q_ref, k_ref, v_ref, cos_ref, sin_ref, qn_ref, kn_ref, e_ref, et_ref, qo_ref, ko_ref, vo_ref):
    e, et = e_ref[...], et_ref[...]
    cos2, sin2 = cos_ref[...], sin_ref[...]

    def norm_rope(x, gain, nrep):
        ms = _head_sum(x * x, e[:x.shape[1]], et[:, :x.shape[1]]) * (1.0 / AT_HD)
        xn = x * lax.rsqrt(ms + EPS) * gain
        cos = jnp.tile(cos2, (1, nrep))
        sin = jnp.tile(sin2, (1, nrep))
        return xn * cos + _rot_partner(xn) * sin

    qn = norm_rope(q_ref[...], qn_ref[...], AT_Q // 128) * (AT_HD ** -0.5)
    qo_ref[...] = jnp.transpose(qn).astype(BF16)
    ko_ref[...] = norm_rope(k_ref[...], kn_ref[...], AT_KW // 128).astype(BF16)
    vo_ref[...] = jnp.transpose(v_ref[...]).astype(BF16)


def attention_prepare(p, q_norm, k_norm, tm=256):
    cos_t, sin_t = _rope_tables(tm)
    heads = jnp.arange(AT_Q, dtype=jnp.int32) // AT_HD
    e = (heads[:, None] == jnp.arange(128, dtype=jnp.int32)[None, :]).astype(BF16)
    et = jnp.transpose(e)
    tab = lambda i: jnp.where(i * tm < N_LAT, ((i * tm) % SEQ) // tm, SEQ // tm)
    qcol = (3 * RW_H * RW_N) // AT_Q
    kcol = (3 * RW_H * RW_N + AT_Q) // AT_KW
    return pl.pallas_call(
        _at_prep_kernel,
        out_shape=(jax.ShapeDtypeStruct((AT_Q, N_ROWS), BF16), jax.ShapeDtypeStruct((N_ROWS, AT_KW), BF16),
                   jax.ShapeDtypeStruct((AT_KW, N_ROWS), BF16)),
        grid=(N_ROWS // tm,),
        in_specs=[pl.BlockSpec((tm, AT_Q), lambda i: (i, qcol)),
                  pl.BlockSpec((tm, AT_KW), lambda i: (i, kcol)),
                  pl.BlockSpec((tm, AT_KW), lambda i: (i, kcol + 1)),
                  pl.BlockSpec((tm, 128), lambda i: (tab(i), 0)),
                  pl.BlockSpec((tm, 128), lambda i: (tab(i), 0)),
                  pl.BlockSpec((1, AT_Q), lambda i: (0, 0)),
                  pl.BlockSpec((1, AT_KW), lambda i: (0, 0)),
                  pl.BlockSpec((AT_Q, 128), lambda i: (0, 0)),
                  pl.BlockSpec((128, AT_Q), lambda i: (0, 0))],
        out_specs=(pl.BlockSpec((AT_Q, tm), lambda i: (0, i)), pl.BlockSpec((tm, AT_KW), lambda i: (i, 0)),
                   pl.BlockSpec((AT_KW, tm), lambda i: (0, i))),
        compiler_params=_cp(("parallel",)), name="attn_prepare",
    )(p, p, p, cos_t, sin_t, jnp.tile(q_norm, AT_Q // AT_HD).reshape(1, AT_Q),
      jnp.tile(k_norm, AT_KV).reshape(1, AT_KW), e, et)


def _flash_t_kernel(qt_ref, k_ref, vt_ref, o_ref, m_sc, l_sc, acc_sc):
    ki = pl.program_id(2)
    nq = AT_Q // AT_HD
    gq = nq // AT_KV

    @pl.when(ki == 0)
    def _():
        m_sc[...] = jnp.full_like(m_sc, -1e30)
        l_sc[...] = jnp.zeros_like(l_sc)
        acc_sc[...] = jnp.zeros_like(acc_sc)

    for g in range(AT_KV):
        kg = k_ref[:, g * AT_HD:(g + 1) * AT_HD]
        vtg = vt_ref[g * AT_HD:(g + 1) * AT_HD, :]
        hs = range(g * gq, (g + 1) * gq)
        st = [_dot(kg, qt_ref[h * AT_HD:(h + 1) * AT_HD, :]) for h in hs]
        m_old = [m_sc[h] for h in hs]
        m_new = [jnp.maximum(m_old[i], jnp.max(st[i], axis=0, keepdims=True)) for i in range(gq)]
        alpha = [jnp.exp(m_old[i] - m_new[i]) for i in range(gq)]
        pt = [jnp.exp(st[i] - m_new[i]) for i in range(gq)]
        for i, h in enumerate(hs):
            l_sc[h] = alpha[i] * l_sc[h] + jnp.sum(pt[i], axis=0, keepdims=True)
            m_sc[h] = m_new[i]
        pv = [_dot(vtg, pt[i].astype(BF16)) for i in range(gq)]
        for i, h in enumerate(hs):
            rows = pl.ds(h * AT_HD, AT_HD)
            acc_sc[rows, :] = alpha[i] * acc_sc[rows, :] + pv[i]

    @pl.when(ki == pl.num_programs(2) - 1)
    def _():
        inv = jnp.concatenate([jnp.broadcast_to(1.0 / l_sc[h], (AT_HD, l_sc.shape[2])) for h in range(nq)], axis=0)
        o_ref[...] = jnp.transpose(acc_sc[...] * inv)


def flash_attention_t(qt, k, vt):
    nq = AT_Q // AT_HD
    nk = (CTX + SEQ) // AT_TK
    kv_rb = lambda b, ki: jnp.where(ki < CTX // AT_TK, N_LAT // AT_TK + b * (CTX // AT_TK) + ki,
                                    b * (SEQ // AT_TK) + ki - CTX // AT_TK)
    return pl.pallas_call(
        _flash_t_kernel,
        out_shape=jax.ShapeDtypeStruct((N_LAT, AT_Q), F32),
        grid=(BATCH, SEQ // AT_TQ, nk),
        in_specs=[pl.BlockSpec((AT_Q, AT_TQ), lambda b, qi, ki: (0, b * (SEQ // AT_TQ) + qi)),
                  pl.BlockSpec((AT_TK, AT_KW), lambda b, qi, ki: (kv_rb(b, ki), 0)),
                  pl.BlockSpec((AT_KW, AT_TK), lambda b, qi, ki: (0, kv_rb(b, ki)))],
        out_specs=pl.BlockSpec((AT_TQ, AT_Q), lambda b, qi, ki: (b * (SEQ // AT_TQ) + qi, 0)),
        scratch_shapes=[pltpu.VMEM((nq, 1, AT_TQ), F32), pltpu.VMEM((nq, 1, AT_TQ), F32),
                        pltpu.VMEM((AT_Q, AT_TQ), F32)],
        compiler_params=_cp(("parallel", "parallel", "arbitrary")), name="flash_attention")(qt, k, vt)


OD_PAD_N = 4992


def odd_mixer(x, mods, g_pre, g_post, op):
    W = RW_H * RW_N
    w = op['w_in']
    c3 = 3 * W
    code_n = c3 + 64 + 64 + 160
    w_perm = jnp.concatenate([w[:, :c3], w[:, code_n:], w[:, c3:code_n],
                              jnp.zeros((D, OD_PAD_N - w.shape[1]), F32)], axis=1).astype(BF16)
    p = norm_mod_matmul(x, g_pre, mods, w_perm, 0, 1, tn=384, name="odd_in_proj")
    mu = op['mu']
    taps = lambda m: jnp.stack([0.5 * m, 1.0 - m, 0.5 * m], axis=1)
    code = dwconv3(p, 0, c3, taps(mu[:c3]), jnp.zeros((c3,), F32), False, "rwkv_shift")
    lo_col = c3 + AT_Q + 2 * AT_KW
    mu_lo = jnp.pad(mu[c3:], (0, 384 - (code_n - c3)))
    lora = dwconv3(p, lo_col, 384, taps(mu_lo), jnp.zeros((384,), F32), False, "rwkv_shift_lora", cb=128)
    lw, kd, be, kap, g, bonus = rwkv_prepare(code, lora, op)
    y2 = rwkv_scan(code, lw, kd, be, kap)
    o_l = rwkv_output(y2, bonus, g, op)
    q, k, v = attention_prepare(p, op['q_norm'], op['k_norm'])
    a_l = flash_attention_t(q, k, v)
    return outproj_residual(o_l, a_l, op['w_out'].astype(BF16), x[:N_LAT], g_post, mods, 2, name="odd_out_proj")


def _router_kernel(x_ref, g_ref, mod_ref, rw_ref, rb_ref, s1_ref, s3_ref, s2_ref, t_ref, idx_ref, wt_ref, sh_ref):
    x = x_ref[...]
    ms = jnp.mean(x * x, axis=-1, keepdims=True)
    t = x * lax.rsqrt(ms + EPS) * g_ref[...] * (1.0 + mod_ref[4:5, :]) + mod_ref[3:4, :]
    t_ref[...] = t
    tb = t.astype(BF16)
    sh_ref[...] = _dot((_silu(_dot(tb, s1_ref[...])) * _dot(tb, s3_ref[...])).astype(BF16), s2_ref[...])
    th, tl = _split(t)
    wh, wl = _split(rw_ref[...])
    lg = _dot_nt(wh, th) + (_dot_nt(wh, tl) + _dot_nt(wl, th))
    sc = _sigmoid(lg)
    sel = sc + rb_ref[...]
    tm = sel.shape[1]
    gsz = N_EXP // N_GRP
    ninf = -jnp.inf
    sel3 = sel.reshape(N_GRP, gsz, tm)
    i3 = lax.broadcasted_iota(jnp.int32, sel3.shape, 1)
    m1 = jnp.max(sel3, axis=1, keepdims=True)
    first = jnp.min(jnp.where(sel3 == m1, i3, gsz), axis=1, keepdims=True)
    m2 = jnp.max(jnp.where(i3 == first, ninf, sel3), axis=1, keepdims=True)
    grp = (m1 + m2).reshape(N_GRP, tm)
    gi = lax.broadcasted_iota(jnp.int32, grp.shape, 0)
    gmask = jnp.zeros(grp.shape, F32)
    for _ in range(TOPK_GRP):
        m = jnp.max(grp, axis=0, keepdims=True)
        pick = jnp.min(jnp.where(grp == m, gi, N_GRP), axis=0, keepdims=True)
        hit = gi == pick
        gmask = jnp.where(hit, 1.0, gmask)
        grp = jnp.where(hit, ninf, grp)
    emask = jnp.broadcast_to(gmask.reshape(N_GRP, 1, tm), (N_GRP, gsz, tm)).reshape(N_EXP, tm)
    msel = jnp.where(emask > 0.5, sel, ninf)
    ei = lax.broadcasted_iota(jnp.int32, msel.shape, 0)
    idxs, ws = [], []
    for _ in range(TOP_K):
        m = jnp.max(msel, axis=0, keepdims=True)
        pick = jnp.min(jnp.where(msel == m, ei, N_EXP), axis=0, keepdims=True)
        hit = ei == pick
        idxs.append(pick)
        ws.append(jnp.sum(jnp.where(hit, sc, 0.0), axis=0, keepdims=True))
        msel = jnp.where(hit, ninf, msel)
    w = jnp.concatenate(ws, axis=0)
    idx_ref[...] = jnp.concatenate(idxs, axis=0)
    wt_ref[...] = w / jnp.sum(w, axis=0, keepdims=True) * ROUTED_SCALE


def moe_router(x, M, g, mods, mp, tm=256):
    full = lambda shape: pl.BlockSpec(shape, lambda i: (0,) * len(shape))
    return pl.pallas_call(
        _router_kernel,
        out_shape=(jax.ShapeDtypeStruct((M, D), F32), jax.ShapeDtypeStruct((TOP_K, M), jnp.int32),
                   jax.ShapeDtypeStruct((TOP_K, M), F32), jax.ShapeDtypeStruct((M, D), F32)),
        grid=(M // tm,),
        in_specs=[pl.BlockSpec((tm, D), lambda i: (i, 0)), full((1, D)),
                  pl.BlockSpec((None, 6, D), lambda i: (_seq_of_rowblock(i, tm), 0, 0)),
                  full((N_EXP, D)), full((N_EXP, 1)), full((D, EXP_FF)), full((D, EXP_FF)), full((EXP_FF, D))],
        out_specs=(pl.BlockSpec((tm, D), lambda i: (i, 0)), pl.BlockSpec((TOP_K, tm), lambda i: (0, i)),
                   pl.BlockSpec((TOP_K, tm), lambda i: (0, i)), pl.BlockSpec((tm, D), lambda i: (i, 0))),
        compiler_params=_cp(("parallel",)), name="moe_router",
    )(x, g.reshape(1, D), mods, jnp.transpose(mp['router_w']), mp['router_bias'].reshape(N_EXP, 1),
      mp['s1'].astype(BF16), mp['s3'].astype(BF16), mp['s2'].astype(BF16))


def _gather_rows(idx_ref, n, src_hbm, dst, sem, slot):
    def body(r, carry):
        pltpu.make_async_copy(src_hbm.at[pl.ds(idx_ref[0, r], 1)], dst.at[slot, pl.ds(r, 1)], sem.at[slot]).start()
        return carry

    lax.fori_loop(0, n, body, 0)


def _wait_rows(n, src_hbm, dst, sem, slot):
    pltpu.make_async_copy(src_hbm.at[pl.ds(0, n)], dst.at[slot], sem.at[slot]).wait()


def _expert_kernel(be_ref, tok_ref, tokn_ref, w_ref, t_hbm, w1_ref, w3_ref, w2_ref, o_ref, xbuf, sem):
    i = pl.program_id(0)
    n = pl.num_programs(0)
    slot = i % 2

    @pl.when(i == 0)
    def _():
        _gather_rows(tok_ref, MOE_BLK, t_hbm, xbuf, sem, 0)

    @pl.when(i + 1 < n)
    def _():
        _gather_rows(tokn_ref, MOE_BLK, t_hbm, xbuf, sem, 1 - slot)

    _wait_rows(MOE_BLK, t_hbm, xbuf, sem, slot)
    xb = xbuf[slot].astype(BF16)
    h = _silu(_dot(xb, w1_ref[...])) * _dot(xb, w3_ref[...])
    o_ref[...] = _dot(h.astype(BF16), w2_ref[...]) * w_ref[...]


def moe_experts(t, buf_tok, buf_w, block_e, w1, w3, w2):
    nb = block_e.shape[0]
    tok3 = buf_tok.reshape(nb, 1, MOE_BLK)
    nxt = lambda i, be: (jnp.minimum(i + 1, nb - 1), 0, 0)
    grid_spec = pltpu.PrefetchScalarGridSpec(
        num_scalar_prefetch=1, grid=(nb,),
        in_specs=[pl.BlockSpec((None, 1, MOE_BLK), lambda i, be: (i, 0, 0), memory_space=pltpu.SMEM),
                  pl.BlockSpec((None, 1, MOE_BLK), nxt, memory_space=pltpu.SMEM),
                  pl.BlockSpec((MOE_BLK, 1), lambda i, be: (i, 0)),
                  pl.BlockSpec(memory_space=pl.ANY),
                  pl.BlockSpec((None, D, EXP_FF), lambda i, be: (be[i], 0, 0)),
                  pl.BlockSpec((None, D, EXP_FF), lambda i, be: (be[i], 0, 0)),
                  pl.BlockSpec((None, EXP_FF, D), lambda i, be: (be[i], 0, 0))],
        out_specs=pl.BlockSpec((MOE_BLK, D), lambda i, be: (i, 0)),
        scratch_shapes=[pltpu.VMEM((2, MOE_BLK, D), F32), pltpu.SemaphoreType.DMA((2,))])
    return pl.pallas_call(
        _expert_kernel, out_shape=jax.ShapeDtypeStruct((nb * MOE_BLK, D), F32), grid_spec=grid_spec,
        compiler_params=_cp(("arbitrary",)), name="moe_experts",
    )(block_e, tok3, tok3, buf_w.reshape(nb * MOE_BLK, 1), t, w1, w3, w2)


MOE_TT = 32


def _combine_kernel(dst_ref, dstn_ref, ys_hbm, sh_ref, x_ref, g_ref, mod_ref, o_ref, buf, sem):
    i = pl.program_id(0)
    n = pl.num_programs(0)
    slot = i % 2
    nrow = MOE_TT * TOP_K

    @pl.when(i == 0)
    def _():
        _gather_rows(dst_ref, nrow, ys_hbm, buf, sem, 0)

    @pl.when(i + 1 < n)
    def _():
        _gather_rows(dstn_ref, nrow, ys_hbm, buf, sem, 1 - slot)

    _wait_rows(nrow, ys_hbm, buf, sem, slot)
    f = sh_ref[...]
    for k in range(TOP_K):
        f = f + buf[slot, k * MOE_TT:(k + 1) * MOE_TT, :]
    ms = jnp.mean(f * f, axis=-1, keepdims=True)
    o_ref[...] = x_ref[...] + mod_ref[5:6, :] * (f * lax.rsqrt(ms + EPS) * g_ref[...])


def moe_combine(ys, dest, sh, x, M, g, mods):
    nt = M // MOE_TT
    nrow = MOE_TT * TOP_K
    d3 = dest.reshape(nt, MOE_TT, TOP_K).transpose(0, 2, 1).reshape(nt, 1, nrow)
    nxt = lambda i: (jnp.minimum(i + 1, nt - 1), 0, 0)
    return pl.pallas_call(
        _combine_kernel, out_shape=jax.ShapeDtypeStruct((M, D), F32), grid=(nt,),
        in_specs=[pl.BlockSpec((None, 1, nrow), lambda i: (i, 0, 0), memory_space=pltpu.SMEM),
                  pl.BlockSpec((None, 1, nrow), nxt, memory_space=pltpu.SMEM),
                  pl.BlockSpec(memory_space=pl.ANY),
                  pl.BlockSpec((MOE_TT, D), lambda i: (i, 0)),
                  pl.BlockSpec((MOE_TT, D), lambda i: (i, 0)),
                  pl.BlockSpec((1, D), lambda i: (0, 0)),
                  pl.BlockSpec((None, 6, D), lambda i: (_seq_of_rowblock(i, MOE_TT), 0, 0))],
        out_specs=pl.BlockSpec((MOE_TT, D), lambda i: (i, 0)),
        scratch_shapes=[pltpu.VMEM((2, nrow, D), F32), pltpu.SemaphoreType.DMA((2,))],
        compiler_params=_cp(("arbitrary",)), name="moe_combine",
    )(d3, d3, ys, sh, x, g.reshape(1, D), mods)


def moe_layer(x, M, g_pre, g_post, mods, mp):
    t, idx_t, wts_t, sh = moe_router(x, M, g_pre, mods, mp)
    mk = M * TOP_K
    nb = -(-(mk + N_EXP * (MOE_BLK - 1)) // MOE_BLK)
    flat_e = jnp.transpose(idx_t).reshape(mk)
    flat_w = jnp.transpose(wts_t).reshape(mk)
    onehot = (flat_e[:, None] == jnp.arange(N_EXP, dtype=jnp.int32)[None, :]).astype(jnp.int32)
    csum = jnp.cumsum(onehot, axis=0)
    rank = jnp.take_along_axis(csum, flat_e[:, None], axis=1)[:, 0] - 1
    counts = csum[-1]
    padded = (counts + MOE_BLK - 1) // MOE_BLK * MOE_BLK
    pend = jnp.cumsum(padded)
    dest = (pend - padded)[flat_e] + rank
    flat_tok = jnp.arange(mk, dtype=jnp.int32) // TOP_K
    buf_tok = jnp.zeros((nb * MOE_BLK,), jnp.int32).at[dest].set(flat_tok)
    buf_w = jnp.zeros((nb * MOE_BLK,), F32).at[dest].set(flat_w)
    block_e = jnp.minimum(jnp.searchsorted(pend, jnp.arange(nb, dtype=jnp.int32) * MOE_BLK, side='right'),
                          N_EXP - 1).astype(jnp.int32)
    ys = moe_experts(t, buf_tok, buf_w, block_e, mp['w1'].astype(BF16), mp['w3'].astype(BF16),
                     mp['w2'].astype(BF16))
    return moe_combine(ys, dest.astype(jnp.int32), sh, x, M, g_post, mods)


MOE_T = 512
MOE_CAP = 128


def _router2_kernel(x_ref, g_ref, mod_ref, rw_ref, rb_ref, s1_ref, s3_ref, s2_ref, t_ref, wt_ref, cnt_ref, sh_ref):
    x = x_ref[...]
    ms = jnp.mean(x * x, axis=-1, keepdims=True)
    t = x * lax.rsqrt(ms + EPS) * g_ref[...] * (1.0 + mod_ref[4:5, :]) + mod_ref[3:4, :]
    tb = t.astype(BF16)
    t_ref[...] = tb
    sh_ref[...] = _dot((_silu(_dot(tb, s1_ref[...])) * _dot(tb, s3_ref[...])).astype(BF16), s2_ref[...])
    th, tl = _split(t)
    wh, wl = _split(rw_ref[...])
    lg = _dot_nt(wh, th) + (_dot_nt(wh, tl) + _dot_nt(wl, th))
    sc = _sigmoid(lg)
    sel = sc + rb_ref[...]
    tm = sel.shape[1]
    gsz = N_EXP // N_GRP
    ninf = -jnp.inf
    sel3 = sel.reshape(N_GRP, gsz, tm)
    i3 = lax.broadcasted_iota(jnp.int32, sel3.shape, 1)
    m1 = jnp.max(sel3, axis=1, keepdims=True)
    first = jnp.min(jnp.where(sel3 == m1, i3, gsz), axis=1, keepdims=True)
    m2 = jnp.max(jnp.where(i3 == first, ninf, sel3), axis=1, keepdims=True)
    grp = (m1 + m2).reshape(N_GRP, tm)
    gi = lax.broadcasted_iota(jnp.int32, grp.shape, 0)
    gmask = jnp.zeros(grp.shape, F32)
    for _ in range(TOPK_GRP):
        m = jnp.max(grp, axis=0, keepdims=True)
        pick = jnp.min(jnp.where(grp == m, gi, N_GRP), axis=0, keepdims=True)
        hit = gi == pick
        gmask = jnp.where(hit, 1.0, gmask)
        grp = jnp.where(hit, ninf, grp)
    emask = jnp.broadcast_to(gmask.reshape(N_GRP, 1, tm), (N_GRP, gsz, tm)).reshape(N_EXP, tm)
    msel = jnp.where(emask > 0.5, sel, ninf)
    ei = lax.broadcasted_iota(jnp.int32, msel.shape, 0)
    chosen = jnp.zeros(msel.shape, F32)
    for _ in range(TOP_K):
        m = jnp.max(msel, axis=0, keepdims=True)
        pick = jnp.min(jnp.where(msel == m, ei, N_EXP), axis=0, keepdims=True)
        hit = ei == pick
        chosen = jnp.where(hit, 1.0, chosen)
        msel = jnp.where(hit, ninf, msel)
    w = chosen * sc
    wt = w / jnp.sum(w, axis=0, keepdims=True) * ROUTED_SCALE
    wt_ref[...] = wt
    cnt_ref[...] = jnp.sum((wt > 0.0).astype(F32), axis=1, keepdims=True).astype(jnp.int32)


def moe_router2(x, M, g, mods, mp):
    tm = MOE_T
    full = lambda shape: pl.BlockSpec(shape, lambda i: (0,) * len(shape))
    return pl.pallas_call(
        _router2_kernel,
        out_shape=(jax.ShapeDtypeStruct((M, D), BF16), jax.ShapeDtypeStruct((N_EXP, M), F32),
                   jax.ShapeDtypeStruct((M // tm, N_EXP, 1), jnp.int32), jax.ShapeDtypeStruct((M, D), F32)),
        grid=(M // tm,),
        in_specs=[pl.BlockSpec((tm, D), lambda i: (i, 0)), full((1, D)),
                  pl.BlockSpec((None, 6, D), lambda i: (_seq_of_rowblock(i, tm), 0, 0)),
                  full((N_EXP, D)), full((N_EXP, 1)), full((D, EXP_FF)), full((D, EXP_FF)), full((EXP_FF, D))],
        out_specs=(pl.BlockSpec((tm, D), lambda i: (i, 0)), pl.BlockSpec((N_EXP, tm), lambda i: (0, i)),
                   pl.BlockSpec((None, N_EXP, 1), lambda i: (i, 0, 0)), pl.BlockSpec((tm, D), lambda i: (i, 0))),
        compiler_params=_cp(("parallel",)), name="moe_router",
    )(x, g.reshape(1, D), mods, jnp.transpose(mp['router_w']), mp['router_bias'].reshape(N_EXP, 1),
      mp['s1'].astype(BF16), mp['s3'].astype(BF16), mp['s2'].astype(BF16))


def _moe2_kernel(cnt_ref, t_ref, wt_ref, sh_ref, x_ref, g_ref, mod_ref, w1_ref, w3_ref, w2_ref, o_ref,
                 rank_sc, acc_sc):
    i = pl.program_id(0)
    e = pl.program_id(1)
    T = MOE_T

    @pl.when(e == 0)
    def _():
        picked = (wt_ref[...] > 0.0).astype(BF16)
        before = (lax.broadcasted_iota(jnp.int32, (T, T), 0) < lax.broadcasted_iota(jnp.int32, (T, T), 1))
        rank_sc[...] = _dot(picked, before.astype(BF16))
        acc_sc[...] = jnp.zeros_like(acc_sc)

    n_tok = cnt_ref[i * N_EXP + e]
    w_row = wt_ref[pl.ds(e, 1), :]
    r_row = rank_sc[pl.ds(e, 1), :]
    slot = lax.broadcasted_iota(jnp.int32, (MOE_CAP, T), 0).astype(F32)

    def chunk(ci, carry):
        hit = ((r_row - (ci * MOE_CAP).astype(F32)) == slot) & (w_row > 0.0)
        pb = hit.astype(F32).astype(BF16)
        xg = _dot(pb, t_ref[...]).astype(BF16)
        h = _silu(_dot(xg, w1_ref[...])) * _dot(xg, w3_ref[...])
        y = _dot(h.astype(BF16), w2_ref[...])
        w_slot = jnp.sum(jnp.where(hit, w_row, 0.0), axis=1, keepdims=True)
        yw = (y * w_slot).astype(BF16)
        acc_sc[...] += lax.dot_general(pb, yw, (((0,), (0,)), ((), ())), preferred_element_type=F32)
        return carry

    lax.fori_loop(0, (n_tok + MOE_CAP - 1) // MOE_CAP, chunk, 0)

    @pl.when(e == pl.num_programs(1) - 1)
    def _():
        f = acc_sc[...] + sh_ref[...]
        ms = jnp.mean(f * f, axis=-1, keepdims=True)
        o_ref[...] = x_ref[...] + mod_ref[5:6, :] * (f * lax.rsqrt(ms + EPS) * g_ref[...])


def moe_layer2(x, M, g_pre, g_post, mods, mp):
    t, wt, cnt, sh = moe_router2(x, M, g_pre, mods, mp)
    T = MOE_T
    grid_spec = pltpu.PrefetchScalarGridSpec(
        num_scalar_prefetch=1, grid=(M // T, N_EXP),
        in_specs=[pl.BlockSpec((T, D), lambda i, e, c: (i, 0)),
                  pl.BlockSpec((N_EXP, T), lambda i, e, c: (0, i)),
                  pl.BlockSpec((T, D), lambda i, e, c: (i, 0)),
                  pl.BlockSpec((T, D), lambda i, e, c: (i, 0)),
                  pl.BlockSpec((1, D), lambda i, e, c: (0, 0)),
                  pl.BlockSpec((None, 6, D), lambda i, e, c: (_seq_of_rowblock(i, T), 0, 0)),
                  pl.BlockSpec((None, D, EXP_FF), lambda i, e, c: (e, 0, 0)),
                  pl.BlockSpec((None, D, EXP_FF), lambda i, e, c: (e, 0, 0)),
                  pl.BlockSpec((None, EXP_FF, D), lambda i, e, c: (e, 0, 0))],
        out_specs=pl.BlockSpec((T, D), lambda i, e, c: (i, 0)),
        scratch_shapes=[pltpu.VMEM((N_EXP, T), F32), pltpu.VMEM((T, D), F32)])
    return pl.pallas_call(
        _moe2_kernel, out_shape=jax.ShapeDtypeStruct((M, D), F32), grid_spec=grid_spec,
        compiler_params=_cp(("parallel", "arbitrary")), name="moe_experts",
    )(cnt.reshape(-1), t, wt, sh, x, g_post.reshape(1, D), mods, mp['w1'].astype(BF16), mp['w3'].astype(BF16),
      mp['w2'].astype(BF16))


def kernel(x, c, ctx, c_ctx, mod_w, mod_b, norm_mix_pre, norm_mix_post, norm_ffn_pre, norm_ffn_post, router_w, router_bias, expert_w1, expert_w3, expert_w2, shared_w1, shared_w3, shared_w2, ev_w_in, ev_w_out, ssd_conv_w, ssd_conv_b, ssd_dt_bias, ssd_a_log, ssd_d, ssd_norm_w, hy_conv_w, hy_conv_b, hy_mlp_w0, hy_mlp_b0, hy_freq0, hy_mlp_w1, hy_mlp_b1, hy_freq1, hy_mlp_w2, hy_bias, od_w_in, od_w_out, rw_mu, rw_w0, rw_w_up, rw_a0, rw_a_up, rw_g_up, rw_k_k, rw_k_a, rw_r_k, rw_ln_w, rw_ln_b, at_q_norm, at_k_norm):
    xs = jnp.concatenate([x.reshape(N_LAT, D), ctx.reshape(BATCH * CTX, D)], axis=0)
    cvecs = jnp.zeros((8, D), F32).at[0:BATCH].set(c).at[BATCH].set(c_ctx)
    assert mod_w.shape[0] == 2, "one even (SSD | Hyena) layer followed by one odd (RWKV | attention) layer"

    def moe_params(i):
        return dict(router_w=router_w[i], router_bias=router_bias[i], w1=expert_w1[i], w3=expert_w3[i],
                    w2=expert_w2[i], s1=shared_w1[i], s3=shared_w3[i], s2=shared_w2[i])

    mods = modulation(cvecs, mod_w[0], mod_b[0])[:BATCH + 1].reshape(BATCH + 1, 6, D)
    ep = dict(w_in=ev_w_in[0], w_out=ev_w_out[0], ssd_conv_w=ssd_conv_w[0], ssd_conv_b=ssd_conv_b[0],
              ssd_dt_bias=ssd_dt_bias[0], ssd_a_log=ssd_a_log[0], ssd_d=ssd_d[0], ssd_norm_w=ssd_norm_w[0],
              hy_conv_w=hy_conv_w[0], hy_conv_b=hy_conv_b[0], hy_mlp_w0=hy_mlp_w0[0], hy_mlp_b0=hy_mlp_b0[0],
              hy_freq0=hy_freq0[0], hy_mlp_w1=hy_mlp_w1[0], hy_mlp_b1=hy_mlp_b1[0], hy_freq1=hy_freq1[0],
              hy_mlp_w2=hy_mlp_w2[0], hy_bias=hy_bias[0])
    xs = even_mixer(xs, mods, norm_mix_pre[0], norm_mix_post[0], ep)
    xs = moe_layer2(xs, N_ROWS, norm_ffn_pre[0], norm_ffn_post[0], mods, moe_params(0))
    mods = modulation(cvecs, mod_w[1], mod_b[1])[:BATCH + 1].reshape(BATCH + 1, 6, D)
    op = dict(w_in=od_w_in[0], w_out=od_w_out[0], mu=rw_mu[0], w0=rw_w0[0], w_up=rw_w_up[0], a0=rw_a0[0],
              a_up=rw_a_up[0], g_up=rw_g_up[0], k_k=rw_k_k[0], k_a=rw_k_a[0], r_k=rw_r_k[0], ln_w=rw_ln_w[0],
              ln_b=rw_ln_b[0], q_norm=at_q_norm[0], k_norm=at_k_norm[0])
    xl = odd_mixer(xs, mods, norm_mix_pre[1], norm_mix_post[1], op)
    xl = moe_layer2(xl, N_LAT, norm_ffn_pre[1], norm_ffn_post[1], mods, moe_params(1))
    return xl.reshape(BATCH, SEQ, D)
```

```python
import functools
import math

import numpy as np
import jax
import jax.numpy as jnp
from jax import lax
from jax.experimental import pallas as pl
from jax.experimental.pallas import tpu as pltpu

F32 = jnp.float32
BF16 = jnp.bfloat16

D = 1024
BATCH = 2
SEQ = 8192
CTX = 256
N_LAT = BATCH * SEQ
N_ROWS = N_LAT + BATCH * CTX
EPS = 1e-6
GRID_W = 64

SSD_HEADS = 16
SSD_P = 64
SSD_G = 2
SSD_S = 128
SSD_Q = 128
HY_W = 1024
HY_EMB = 33
HY_HID = 64

RW_H = 16
RW_N = 64
RW_CHUNK = 64
RW_GN_EPS = 64e-5

AT_KV = 4
AT_HD = 64

N_EXP = 64
TOP_K = 8
N_GRP = 8
TOPK_GRP = 4
EXP_FF = 256
ROUTED_SCALE = 2.5
MOE_BLK = 128

VMEM_LIMIT = 56 * 1024 * 1024


def _cp(sem, vmem=None):
    return pltpu.CompilerParams(dimension_semantics=sem, vmem_limit_bytes=vmem or VMEM_LIMIT)


def _dot(a, b):
    return jnp.dot(a, b, preferred_element_type=F32)


def _dot_nt(a, b):
    return lax.dot_general(a, b, (((1,), (1,)), ((), ())), preferred_element_type=F32)


def _split(x):
    hi = x.astype(BF16)
    lo = (x - hi.astype(F32)).astype(BF16)
    return hi, lo


def _dot3(a, b):
    ah, al = _split(a)
    bh, bl = _split(b)
    return _dot(ah, bh) + (_dot(ah, bl) + _dot(al, bh))


def _dot2l(a, b):
    ah, al = _split(a)
    return _dot(ah, b) + _dot(al, b)


def _dot2r(a, b):
    bh, bl = _split(b)
    return _dot(a, bh) + _dot(a, bl)


def _silu(x):
    return x * (1.0 / (1.0 + jnp.exp(-x)))


def _sigmoid(x):
    return 1.0 / (1.0 + jnp.exp(-x))


def _softplus(x):
    return jnp.maximum(x, 0.0) + jnp.log(1.0 + jnp.exp(-jnp.abs(x)))


def _seq_of_rowblock(i, tm):
    return jnp.minimum((i * tm) // SEQ, 2)


def _mm_kernel(a_ref, b_ref, o_ref, *, passes):
    a = a_ref[...]
    b = b_ref[...]
    if passes == 3:
        o_ref[...] = _dot3(a.astype(F32), b.astype(F32))
    else:
        o_ref[...] = _dot(a.astype(BF16), b.astype(BF16))


def matmul(a, b, tm, tn, passes=1, name="mm"):
    M, K = a.shape
    N = b.shape[1]
    return pl.pallas_call(
        functools.partial(_mm_kernel, passes=passes),
        out_shape=jax.ShapeDtypeStruct((M, N), F32),
        grid=(M // tm, N // tn),
        in_specs=[pl.BlockSpec((tm, K), lambda i, j: (i, 0)),
                  pl.BlockSpec((K, tn), lambda i, j: (0, j))],
        out_specs=pl.BlockSpec((tm, tn), lambda i, j: (i, j)),
        compiler_params=_cp(("parallel", "parallel")), name=name)(a, b)


def _nmm_kernel(x_ref, g_ref, mod_ref, w_ref, o_ref, a_sc, *, shift_i, scale_i):
    @pl.when(pl.program_id(1) == 0)
    def _():
        x = x_ref[...]
        ms = jnp.mean(x * x, axis=-1, keepdims=True)
        y = x * lax.rsqrt(ms + EPS) * g_ref[...]
        h = y * (1.0 + mod_ref[scale_i:scale_i + 1, :]) + mod_ref[shift_i:shift_i + 1, :]
        a_sc[...] = h.astype(BF16)

    o_ref[...] = _dot(a_sc[...], w_ref[...])


def norm_mod_matmul(x, g, mods, w, shift_i, scale_i, tm=512, tn=None, name="nmm"):
    M = x.shape[0]
    N = w.shape[1]
    tn = tn or N
    return pl.pallas_call(
        functools.partial(_nmm_kernel, shift_i=shift_i, scale_i=scale_i),
        out_shape=jax.ShapeDtypeStruct((M, N), F32),
        grid=(M // tm, N // tn),
        in_specs=[pl.BlockSpec((tm, D), lambda i, j: (i, 0)),
                  pl.BlockSpec((1, D), lambda i, j: (0, 0)),
                  pl.BlockSpec((None, 6, D), lambda i, j: (_seq_of_rowblock(i, tm), 0, 0)),
                  pl.BlockSpec((D, tn), lambda i, j: (0, j))],
        out_specs=pl.BlockSpec((tm, tn), lambda i, j: (i, j)),
        scratch_shapes=[pltpu.VMEM((tm, D), BF16)],
        compiler_params=_cp(("parallel", "arbitrary")), name=name)(x, g.reshape(1, D), mods, w)


def _outproj_kernel(a1_ref, a2_ref, w_ref, x_ref, g_ref, mod_ref, o_ref, *, gate_i):
    y = _dot(a1_ref[...].astype(BF16), w_ref[0:D, :]) + _dot(a2_ref[...].astype(BF16), w_ref[D:2 * D, :])
    ms = jnp.mean(y * y, axis=-1, keepdims=True)
    o_ref[...] = x_ref[...] + mod_ref[gate_i:gate_i + 1, :] * (y * lax.rsqrt(ms + EPS) * g_ref[...])


def outproj_residual(a1, a2, w, x, g, mods, gate_i, tm=256, name="outproj"):
    M = a1.shape[0]
    return pl.pallas_call(
        functools.partial(_outproj_kernel, gate_i=gate_i),
        out_shape=jax.ShapeDtypeStruct((M, D), F32),
        grid=(M // tm,),
        in_specs=[pl.BlockSpec((tm, D), lambda i: (i, 0)),
                  pl.BlockSpec((tm, D), lambda i: (i, 0)),
                  pl.BlockSpec((2 * D, D), lambda i: (0, 0)),
                  pl.BlockSpec((tm, D), lambda i: (i, 0)),
                  pl.BlockSpec((1, D), lambda i: (0, 0)),
                  pl.BlockSpec((None, 6, D), lambda i: (_seq_of_rowblock(i, tm), 0, 0))],
        out_specs=pl.BlockSpec((tm, D), lambda i: (i, 0)),
        compiler_params=_cp(("parallel",)), name=name)(a1, a2, w, x, g.reshape(1, D), mods)


def _mod_kernel(c_ref, w_ref, b_ref, o_ref):
    o_ref[...] = _dot3(_silu(c_ref[...]), w_ref[...]) + b_ref[...]


def modulation(cvecs, w, b):
    N = w.shape[1]
    tn = 1024
    return pl.pallas_call(
        _mod_kernel, out_shape=jax.ShapeDtypeStruct((8, N), F32), grid=(N // tn,),
        in_specs=[pl.BlockSpec((8, D), lambda j: (0, 0)),
                  pl.BlockSpec((D, tn), lambda j: (0, j)),
                  pl.BlockSpec((1, tn), lambda j: (0, j))],
        out_specs=pl.BlockSpec((8, tn), lambda j: (0, j)),
        compiler_params=_cp(("parallel",)), name="modulation")(cvecs, w, b.reshape(1, N))


def _neighbours(x_ref, base, ch, length):
    cur = x_ref[pl.ds(base, ch), :]
    rows = lax.broadcasted_iota(jnp.int32, cur.shape, 0)
    pbase = pl.multiple_of(jnp.maximum(base - 8, 0), 8)
    nbase = pl.multiple_of(jnp.minimum(base + ch, length - 8), 8)
    prev_row = x_ref[pl.ds(pbase, 8), :][7:8, :] * (base > 0).astype(F32)
    next_row = x_ref[pl.ds(nbase, 8), :][0:1, :] * (base + ch < length).astype(F32)
    xm1 = jnp.where(rows == 0, prev_row, pltpu.roll(cur, 1, 0))
    xp1 = jnp.where(rows == ch - 1, next_row, pltpu.roll(cur, ch - 1, 0))
    return xm1, cur, xp1


def _conv3_kernel(x_ref, w_ref, b_ref, o_ref, *, length, ch, act):
    def body(c, carry):
        base = pl.multiple_of(c * ch, ch)
        xm1, cur, xp1 = _neighbours(x_ref, base, ch, length)
        y = xm1 * w_ref[0:1, :] + cur * w_ref[1:2, :] + xp1 * w_ref[2:3, :] + b_ref[...]
        if act:
            y = _silu(y)
        o_ref[pl.ds(base, ch), :] = y
        return carry

    lax.fori_loop(0, length // ch, body, 0)


def dwconv3(p, col0, ncols, w, b, act, name, cb=256):
    wt = jnp.transpose(w)
    b2 = b.reshape(1, ncols)
    outs = []
    for (length, row0, ch) in ((SEQ, 0, 512), (CTX, N_LAT, 256)):
        rb0 = row0 // length
        outs.append(pl.pallas_call(
            functools.partial(_conv3_kernel, length=length, ch=ch, act=act),
            out_shape=jax.ShapeDtypeStruct((BATCH * length, ncols), F32),
            grid=(BATCH, ncols // cb),
            in_specs=[pl.BlockSpec((length, cb), lambda s, j: (rb0 + s, col0 // cb + j)),
                      pl.BlockSpec((3, cb), lambda s, j: (0, j)),
                      pl.BlockSpec((1, cb), lambda s, j: (0, j))],
            out_specs=pl.BlockSpec((length, cb), lambda s, j: (s, j)),
            compiler_params=_cp(("parallel", "parallel")), name=name)(p, wt, b2))
    return jnp.concatenate(outs, axis=0)


def _ssd_kernel(xs_ref, bm_ref, cm_ref, dt_ref, dtT_ref, bias_ref, biasT_ref, alog_ref, alogT_ref,
                y_ref, st_ref):
    d = pl.program_id(0)
    c = pl.program_id(3)
    Q = SSD_Q
    HG = SSD_HEADS // SSD_G

    @pl.when(c == 0)
    def _():
        st_ref[...] = jnp.zeros_like(st_ref)

    isb = d == 1
    sgn = 1 - 2 * d
    dt = _softplus(dt_ref[...] + bias_ref[...])
    dtT = _softplus(dtT_ref[...] + biasT_ref[...])
    a = dt * (-jnp.exp(alog_ref[...]))
    aT = dtT * (-jnp.exp(alogT_ref[...]))
    ii = lax.broadcasted_iota(jnp.int32, (Q, Q), 0)
    jj = lax.broadcasted_iota(jnp.int32, (Q, Q), 1)
    tri = (jj <= ii).astype(BF16)
    triT = (ii <= jj).astype(BF16)
    cs = _dot2r(tri, a)
    csT = _dot2l(aT, triT)
    tot = cs[Q - 1:Q, :]
    p = jnp.where(isb, a - cs, cs)
    pT = jnp.where(isb, aT - csT, csT)
    dec_out = jnp.exp(jnp.where(isb, tot, 0.0) + p)
    dec_state = jnp.exp(jnp.where(isb, 0.0, tot) - p)
    chunk_dec = jnp.exp(tot)
    mask = sgn * (ii - jj) >= 0
    bm = bm_ref[...].astype(BF16)
    cm = cm_ref[...].astype(BF16)
    cb = _dot_nt(cm, bm)
    xs = xs_ref[...]
    H = range(HG)
    g = [(cb * jnp.exp(jnp.where(mask, p[:, h:h + 1] - pT[h:h + 1, :], -1e30))).astype(BF16) for h in H]
    xh = [xs[:, h * SSD_P:(h + 1) * SSD_P] * dt[:, h:h + 1] for h in H]
    s_old = [st_ref[h] for h in H]
    y_in = [_dot(g[h], xh[h].astype(BF16)) for h in H]
    y_st = [_dot(cm, s_old[h].astype(BF16)) for h in H]
    upd = [lax.dot_general(bm, (xh[h] * dec_state[:, h:h + 1]).astype(BF16), (((0,), (0,)), ((), ())),
                           preferred_element_type=F32) for h in H]
    for h in H:
        st_ref[h] = chunk_dec[:, h:h + 1] * s_old[h] + upd[h]
    y_ref[...] = jnp.concatenate([y_in[h] + dec_out[:, h:h + 1] * y_st[h] for h in H], axis=1)


def _ssd_rowblock(d, b, c):
    n_ctx = CTX // SSD_Q
    n_lat = SEQ // SSD_Q
    cc = jnp.where(d == 0, c, n_ctx - 1 - c)
    lc = jnp.where(d == 0, c - n_ctx, n_ctx + n_lat - 1 - c)
    return jnp.where(c < n_ctx, N_LAT // SSD_Q + b * n_ctx + cc, b * n_lat + lc)


def ssd_scan(xbc, dt_raw, dt_bias, a_log):
    HG = SSD_HEADS // SSD_G
    GW = HG * SSD_P
    dsel = dt_raw[:, :2 * SSD_HEADS].reshape(N_ROWS, 2, SSD_G, HG).transpose(1, 2, 0, 3)
    dselT = dsel.transpose(0, 1, 3, 2)
    bias = dt_bias.reshape(2, SSD_G, 1, HG)
    biasT = dt_bias.reshape(2, SSD_G, HG, 1)
    alog = a_log.reshape(2, SSD_G, 1, HG)
    alogT = a_log.reshape(2, SSD_G, HG, 1)
    nch = (CTX + SEQ) // SSD_Q
    rb = lambda d, b, g, c: _ssd_rowblock(d, b, c)
    bcol = SSD_HEADS * SSD_P // SSD_S
    return pl.pallas_call(
        _ssd_kernel,
        out_shape=jax.ShapeDtypeStruct((2, N_ROWS, SSD_HEADS * SSD_P), F32),
        grid=(2, BATCH, SSD_G, nch),
        in_specs=[pl.BlockSpec((SSD_Q, GW), lambda d, b, g, c: (rb(d, b, g, c), g)),
                  pl.BlockSpec((SSD_Q, SSD_S), lambda d, b, g, c: (rb(d, b, g, c), bcol + g)),
                  pl.BlockSpec((SSD_Q, SSD_S), lambda d, b, g, c: (rb(d, b, g, c), bcol + SSD_G + g)),
                  pl.BlockSpec((None, None, SSD_Q, HG), lambda d, b, g, c: (d, g, rb(d, b, g, c), 0)),
                  pl.BlockSpec((None, None, HG, SSD_Q), lambda d, b, g, c: (d, g, 0, rb(d, b, g, c))),
                  pl.BlockSpec((None, None, 1, HG), lambda d, b, g, c: (d, g, 0, 0)),
                  pl.BlockSpec((None, None, HG, 1), lambda d, b, g, c: (d, g, 0, 0)),
                  pl.BlockSpec((None, None, 1, HG), lambda d, b, g, c: (d, g, 0, 0)),
                  pl.BlockSpec((None, None, HG, 1), lambda d, b, g, c: (d, g, 0, 0))],
        out_specs=pl.BlockSpec((None, SSD_Q, GW), lambda d, b, g, c: (d, rb(d, b, g, c), g)),
        scratch_shapes=[pltpu.VMEM((HG, SSD_S, SSD_P), F32)],
        compiler_params=_cp(("parallel", "parallel", "parallel", "arbitrary")), name="ssd_scan",
    )(xbc, xbc, xbc, dsel, dselT, bias, biasT, alog, alogT)


def _ssd_out_kernel(yf_ref, yb_ref, xs_ref, z_ref, dskip_ref, nw_ref, o_ref):
    y = yf_ref[...] + yb_ref[...] + xs_ref[...] * dskip_ref[...]
    y = y * _silu(z_ref[...])
    gs = SSD_HEADS * SSD_P // SSD_G
    parts = []
    for g in range(SSD_G):
        yg = y[:, g * gs:(g + 1) * gs]
        parts.append(yg * lax.rsqrt(jnp.mean(yg * yg, axis=-1, keepdims=True) + EPS))
    o_ref[...] = jnp.concatenate(parts, axis=1) * nw_ref[...]


def ssd_output(y2, xbc, p, d_skip, norm_w, tm=256):
    W = SSD_HEADS * SSD_P
    dexp = jnp.repeat(d_skip, SSD_P).reshape(1, W)
    return pl.pallas_call(
        _ssd_out_kernel, out_shape=jax.ShapeDtypeStruct((N_ROWS, W), F32), grid=(N_ROWS // tm,),
        in_specs=[pl.BlockSpec((None, tm, W), lambda i: (0, i, 0)),
                  pl.BlockSpec((None, tm, W), lambda i: (1, i, 0)),
                  pl.BlockSpec((tm, W), lambda i: (i, 0)),
                  pl.BlockSpec((tm, W), lambda i: (i, 0)),
                  pl.BlockSpec((1, W), lambda i: (0, 0)),
                  pl.BlockSpec((1, W), lambda i: (0, 0))],
        out_specs=pl.BlockSpec((tm, W), lambda i: (i, 0)),
        compiler_params=_cp(("parallel",)), name="ssd_output")(y2, y2, xbc, p, dexp, norm_w.reshape(1, W))


def _hyfilt_kernel(f_ref, w0_ref, b0_ref, fr0_ref, w1_ref, b1_ref, fr1_ref, w2_ref, dl_ref, h_ref, ss_ref, *,
                   n_tiles):
    f = f_ref[...]
    h = jnp.sin(fr0_ref[...] * (_dot3(f, w0_ref[...]) + b0_ref[...]))
    h = jnp.sin(fr1_ref[...] * (_dot3(h, w1_ref[...]) + b1_ref[...]))
    h = _dot3(h, w2_ref[...])
    h = h * jnp.exp(-f[:, 0:1] * dl_ref[...])
    side = pl.program_id(0) // n_tiles
    j = pl.program_id(0) % n_tiles

    @pl.when(j == 0)
    def _():
        ss_ref[...] = jnp.zeros_like(ss_ref)

    ss_ref[...] += jnp.sum(h * h, axis=0, keepdims=True)
    row = lax.broadcasted_iota(jnp.int32, (h.shape[0], 1), 0) + j * h.shape[0]
    h_ref[...] = jnp.where((side == 1) & (row == 0), 0.0, h)


def hyena_filter_taps(L, hp):
    pos = jnp.arange(L, dtype=F32)
    t = pos / (L - 1)
    bands = (HY_EMB - 1) // 2
    freqs = jnp.linspace(1e-4, bands - 1, bands, dtype=F32)
    ang = (2.0 * math.pi / L) * pos[:, None] * freqs[None, :]
    feats = jnp.concatenate([t[:, None], jnp.cos(ang), -jnp.sin(ang)], axis=-1)
    feats = jnp.pad(feats, ((0, 0), (0, 128 - HY_EMB)))
    feats = jnp.concatenate([feats, jnp.flip(feats, axis=0)], axis=0)
    w0 = jnp.pad(hp['hy_mlp_w0'], ((0, 128 - HY_EMB), (0, 0)))
    min_decay = math.log(1e-2) / 1.5
    max_decay = math.log(1e-2) / 0.3
    deltas = jnp.abs(jnp.linspace(min_decay, max_decay, HY_W, dtype=F32))
    dl = jnp.tile(deltas, 2).reshape(1, 2 * HY_W)
    w2 = hp['hy_mlp_w2'].reshape(HY_HID, 2, 2, HY_W).transpose(0, 2, 1, 3).reshape(HY_HID, 4 * HY_W)
    tl = min(L, 512)
    n_tiles = L // tl
    NS = 2 * HY_W
    full = lambda shape: pl.BlockSpec(shape, lambda i: (0, 0))
    return pl.pallas_call(
        functools.partial(_hyfilt_kernel, n_tiles=n_tiles),
        out_shape=(jax.ShapeDtypeStruct((2 * L, NS), F32), jax.ShapeDtypeStruct((1, 2 * NS), F32)),
        grid=(2 * n_tiles,),
        in_specs=[pl.BlockSpec((tl, 128), lambda i: (i, 0)), full((128, HY_HID)), full((1, HY_HID)),
                  full((1, HY_HID)), full((HY_HID, HY_HID)), full((1, HY_HID)), full((1, HY_HID)),
                  pl.BlockSpec((HY_HID, NS), lambda i: (0, i // n_tiles)), full((1, NS))],
        out_specs=(pl.BlockSpec((tl, NS), lambda i: (i, 0)), pl.BlockSpec((1, NS), lambda i: (0, i // n_tiles))),
        compiler_params=_cp(("arbitrary",)), name="hyena_filter",
    )(feats, w0, hp['hy_mlp_b0'].reshape(1, -1), hp['hy_freq0'].reshape(1, -1), hp['hy_mlp_w1'],
      hp['hy_mlp_b1'].reshape(1, -1), hp['hy_freq1'].reshape(1, -1), w2, dl)


def _cis(num, den):
    ang = (2.0 * math.pi / den) * (num % den).astype(F32)
    return jnp.cos(ang), -jnp.sin(ang)


def _fft_consts(NB, BS):
    N = NB * BS
    h = NB // 2
    k1 = jnp.arange(h, dtype=jnp.int32)
    j = jnp.arange(NB, dtype=jnp.int32)
    re, im = _cis(j[None, :] * (2 * k1[:, None] + 1), 2 * NB)
    f1 = jnp.concatenate([re, im], axis=0)
    neg = jnp.where(j >= h, -1.0, 1.0)[None, :]
    f1_data = f1[:, :h]
    f1_filt = f1 * neg
    f1_inv = (2.0 / N) * jnp.concatenate([re[:, :h].T, im[:, :h].T], axis=1)
    r = jnp.arange(BS, dtype=jnp.int32)
    k2 = jnp.arange(BS, dtype=jnp.int32)
    kk = 2 * k1[:, None, None] + 2 * NB * k2[None, :, None] + 1
    gre, gim = _cis(kk * r[None, None, :], 2 * N)
    gf = jnp.concatenate([jnp.concatenate([gre, -gim], axis=2), jnp.concatenate([gim, gre], axis=2)], axis=1)
    gret, gimt = gre.transpose(0, 2, 1), gim.transpose(0, 2, 1)
    gi = jnp.concatenate([jnp.concatenate([gret, gimt], axis=2), jnp.concatenate([-gimt, gret], axis=2)], axis=1)
    return (f1_data.astype(BF16), f1_filt.astype(BF16), f1_inv.astype(BF16), gf.astype(BF16), gi.astype(BF16))


FFT_PAD = 8


def _fft_fwd_kernel(u_ref, f1_ref, g_ref, o_ref, t_sc, *, NB, BS, nj, kg):
    pitch = NB + FFT_PAD
    @pl.when(pl.program_id(2) == 0)
    def _():
        f1 = f1_ref[...]

        def body(r, carry):
            xr = u_ref[pl.ds(r, nj, stride=BS), :].astype(BF16)
            t_sc[pl.ds(pl.multiple_of(r * pitch, 8), NB), :] = _dot(f1, xr)
            return carry

        lax.fori_loop(0, BS, body, 0)

    k0 = pl.program_id(2) * kg
    for i in range(kg):
        are = t_sc[pl.ds(k0 + i, BS, stride=pitch), :]
        aim = t_sc[pl.ds(k0 + i + NB // 2, BS, stride=pitch), :]
        a = jnp.concatenate([are, aim], axis=0).astype(BF16)
        o_ref[i] = _dot(g_ref[i], a)


def fft_fwd(u, col0, nbatch, nj, f1, gf, NB, BS, ct=128, kg=8):
    h = NB // 2
    kg = min(kg, h)
    return pl.pallas_call(
        functools.partial(_fft_fwd_kernel, NB=NB, BS=BS, nj=nj, kg=kg),
        out_shape=jax.ShapeDtypeStruct((nbatch, h, 2 * BS, HY_W), F32),
        grid=(nbatch, HY_W // ct, h // kg),
        in_specs=[pl.BlockSpec((nj * BS, ct), lambda b, c, k: (b, col0 // ct + c)),
                  pl.BlockSpec((NB, nj), lambda b, c, k: (0, 0)),
                  pl.BlockSpec((kg, 2 * BS, 2 * BS), lambda b, c, k: (k, 0, 0))],
        out_specs=pl.BlockSpec((None, kg, 2 * BS, ct), lambda b, c, k: (b, k, 0, c)),
        scratch_shapes=[pltpu.VMEM((BS * (NB + FFT_PAD), ct), F32)],
        compiler_params=_cp(("parallel", "parallel", "arbitrary")), name="hyena_fft_fwd")(u, f1, gf)


def _cmul(u, h, half):
    ure, uim = u[:half], u[half:]
    hre, him = h[:half], h[half:]
    return jnp.concatenate([ure * hre - uim * him, ure * him + uim * hre], axis=0)


def _fft_inv_kernel(us_ref, hs_ref, gi_ref, f1i_ref, o_ref, t_sc, *, NB, BS, kg):
    ks = pl.program_id(2)
    pitch = 2 * BS + FFT_PAD
    for i in range(kg):
        y = _cmul(us_ref[i], hs_ref[i], BS).astype(BF16)
        row = pl.multiple_of((ks * kg + i) * pitch, 8)
        t_sc[pl.ds(row, 2 * BS), :] = _dot(gi_ref[i], y)

    @pl.when(ks == pl.num_programs(2) - 1)
    def _():
        f1i = f1i_ref[...]

        def body(r, carry):
            bre = t_sc[pl.ds(r, NB // 2, stride=pitch), :]
            bim = t_sc[pl.ds(r + BS, NB // 2, stride=pitch), :]
            b = jnp.concatenate([bre, bim], axis=0).astype(BF16)
            o_ref[pl.ds(r, NB // 2, stride=BS), :] = _dot(f1i, b)
            return carry

        lax.fori_loop(0, BS, body, 0)


def fft_inv(us, hs, gi, f1i, NB, BS, ct=128, kg=8):
    nbatch, h = us.shape[0], NB // 2
    kg = min(kg, h)
    L = h * BS
    return pl.pallas_call(
        functools.partial(_fft_inv_kernel, NB=NB, BS=BS, kg=kg),
        out_shape=jax.ShapeDtypeStruct((nbatch * L, HY_W), F32),
        grid=(nbatch, HY_W // ct, h // kg),
        in_specs=[pl.BlockSpec((None, kg, 2 * BS, ct), lambda b, c, k: (b, k, 0, c)),
                  pl.BlockSpec((None, kg, 2 * BS, ct), lambda b, c, k: (0, k, 0, c)),
                  pl.BlockSpec((kg, 2 * BS, 2 * BS), lambda b, c, k: (k, 0, 0)),
                  pl.BlockSpec((h, NB), lambda b, c, k: (0, 0))],
        out_specs=pl.BlockSpec((L, ct), lambda b, c, k: (b, c)),
        scratch_shapes=[pltpu.VMEM((h * (2 * BS + FFT_PAD), ct), F32)],
        compiler_params=_cp(("parallel", "parallel", "arbitrary")), name="hyena_fft_inv")(us, hs, gi, f1i)


def _dft_consts(L):
    N = 2 * L
    k = jnp.arange(L, dtype=jnp.int32)
    n = jnp.arange(N, dtype=jnp.int32)
    re, im = _cis(n[None, :] * (2 * k[:, None] + 1), 2 * N)
    f = jnp.concatenate([re, im], axis=0)
    neg = jnp.where(n >= L, -1.0, 1.0)[None, :]
    fi = (2.0 / N) * jnp.concatenate([re[:, :L].T, im[:, :L].T], axis=1)
    return f[:, :L].astype(BF16), (f * neg).astype(BF16), fi.astype(BF16)


def _cdft_kernel(f_ref, x_ref, o_ref):
    o_ref[...] = _dot(f_ref[...], x_ref[...].astype(BF16))


def dft_fwd(x, f, row0, col0, nbatch, ct=256):
    M, K = f.shape
    return pl.pallas_call(
        _cdft_kernel, out_shape=jax.ShapeDtypeStruct((nbatch, M, HY_W), F32),
        grid=(nbatch, HY_W // ct),
        in_specs=[pl.BlockSpec((M, K), lambda b, c: (0, 0)),
                  pl.BlockSpec((K, ct), lambda b, c: (row0 // K + b, col0 // ct + c))],
        out_specs=pl.BlockSpec((None, M, ct), lambda b, c: (b, 0, c)),
        compiler_params=_cp(("parallel", "parallel")), name="hyena_dft_fwd")(f, x)


def _cdft_inv_kernel(us_ref, hs_ref, fi_ref, o_ref):
    half = us_ref.shape[0] // 2
    o_ref[...] = _dot(fi_ref[...], _cmul(us_ref[...], hs_ref[...], half).astype(BF16))


def dft_inv(us, hs, fi, ct=256):
    nbatch, M2, _ = us.shape
    L = fi.shape[0]
    return pl.pallas_call(
        _cdft_inv_kernel, out_shape=jax.ShapeDtypeStruct((nbatch * L, HY_W), F32),
        grid=(nbatch, HY_W // ct),
        in_specs=[pl.BlockSpec((None, M2, ct), lambda b, c: (b, 0, c)),
                  pl.BlockSpec((None, M2, ct), lambda b, c: (0, 0, c)),
                  pl.BlockSpec((L, M2), lambda b, c: (0, 0))],
        out_specs=pl.BlockSpec((L, ct), lambda b, c: (b, c)),
        compiler_params=_cp(("parallel", "parallel")), name="hyena_dft_inv")(us, hs, fi)


def _hy_gate_kernel(g_ref, y_ref, u_ref, ss_ref, b_ref, o_ref):
    scale = lax.rsqrt(ss_ref[0:1, :] + ss_ref[1:2, :] + 1e-6)
    o_ref[...] = g_ref[...] * (y_ref[...] * scale + u_ref[...] * b_ref[...])


def hy_gate(gate, gcol, grow, y, uin, ucol, urow, ss, order, bias, tm=256):
    M = y.shape[0]
    return pl.pallas_call(
        _hy_gate_kernel, out_shape=jax.ShapeDtypeStruct((M, HY_W), F32), grid=(M // tm,),
        in_specs=[pl.BlockSpec((tm, HY_W), lambda i: (grow // tm + i, gcol // HY_W)),
                  pl.BlockSpec((tm, HY_W), lambda i: (i, 0)),
                  pl.BlockSpec((tm, HY_W), lambda i: (urow // tm + i, ucol // HY_W)),
                  pl.BlockSpec((None, 2, HY_W), lambda i: (order, 0, 0)),
                  pl.BlockSpec((None, 1, HY_W), lambda i: (order, 0, 0))],
        out_specs=pl.BlockSpec((tm, HY_W), lambda i: (i, 0)),
        compiler_params=_cp(("parallel",)), name="hyena_gate")(gate, y, uin, ss, bias)


def hyena(u, hp):
    C = HY_W
    bias = hp['hy_bias'].reshape(2, 1, C)
    NB = BS = int(round(math.sqrt(2 * SEQ)))
    f1d, f1f, f1i, gf, gi = _fft_consts(NB, BS)
    taps, ss = hyena_filter_taps(SEQ, hp)
    ss = ss.reshape(2, 2, C).transpose(1, 0, 2)
    zin, zcol = u, 2 * C
    for order in range(2):
        hs = fft_fwd(taps, order * C, 1, NB, f1f, gf, NB, BS)
        us = fft_fwd(zin, zcol, BATCH, NB // 2, f1d, gf, NB, BS)
        y = fft_inv(us, hs, gi, f1i, NB, BS)
        zin = hy_gate(u, order * C, 0, y, zin, zcol, 0, ss, order, bias)
        zcol = 0
    z_lat = zin
    fd, ff, fi = _dft_consts(CTX)
    taps, ss = hyena_filter_taps(CTX, hp)
    ss = ss.reshape(2, 2, C).transpose(1, 0, 2)
    zin, zcol, zrow = u, 2 * C, N_LAT
    for order in range(2):
        hs = dft_fwd(taps, ff, 0, order * C, 1)
        us = dft_fwd(zin, fd, zrow, zcol, BATCH)
        y = dft_inv(us, hs, fi)
        zin = hy_gate(u, order * C, N_LAT, y, zin, zcol, zrow, ss, order, bias)
        zcol, zrow = 0, 0
    return jnp.concatenate([z_lat, zin], axis=0)


EV_SSD_IN = SSD_HEADS * SSD_P
EV_XBC = EV_SSD_IN + 2 * SSD_G * SSD_S
EV_PAD_N = 5760


def even_mixer(x, mods, g_pre, g_post, ep):
    o1 = EV_SSD_IN
    o2 = o1 + EV_XBC
    o3 = o2 + 2 * SSD_HEADS
    w = ep['w_in']
    n_in = w.shape[1]
    w_perm = jnp.concatenate([w[:, :o2], w[:, o3:], w[:, o2:o3],
                              jnp.zeros((D, EV_PAD_N - n_in), F32)], axis=1).astype(BF16)
    p = norm_mod_matmul(x, g_pre, mods, w_perm, 0, 1, tn=640, name="even_in_proj")
    xbc = dwconv3(p, o1, EV_XBC, ep['ssd_conv_w'], ep['ssd_conv_b'], True, "ssd_conv")
    u = dwconv3(p, o2, 3 * HY_W, ep['hy_conv_w'], ep['hy_conv_b'], False, "hyena_conv")
    dt_raw = p[:, o2 + 3 * HY_W:o2 + 3 * HY_W + 2 * SSD_HEADS]
    y2 = ssd_scan(xbc, dt_raw, ep['ssd_dt_bias'], ep['ssd_a_log'])
    s = ssd_output(y2, xbc, p, ep['ssd_d'], ep['ssd_norm_w'])
    zh = hyena(u, ep)
    return outproj_residual(s, zh, ep['w_out'].astype(BF16), x, g_post, mods, 2, name="even_out_proj")


def _head_sum(x, e, et):
    return _dot2l(_dot2l(x, e), et)


def _rw_prep_kernel(r_ref, k_ref, v_ref, lo_ref, w0_ref, wup_ref, a0_ref, aup_ref, gup_ref, kk_ref, ka_ref,
                    rk_ref, e_ref, et_ref, lw_ref, kd_ref, be_ref, kap_ref, g_ref, bonus_ref):
    r, k, v = r_ref[...], k_ref[...], v_ref[...]
    lo = lo_ref[...]
    wc, ac, gc = lo[:, 0:64], lo[:, 64:128], lo[:, 128:384]
    e, et = e_ref[...], et_ref[...]
    kk = k * kk_ref[...]
    kap = kk * lax.rsqrt(_head_sum(kk * kk, e, et) + 1e-12)
    kap_ref[...] = kap
    g_ref[...] = _dot3(_sigmoid(gc), gup_ref[...])
    kd_sum = jnp.zeros_like(k)
    for d in range(2):
        wlog = -_softplus(-(w0_ref[d:d + 1, :] + _dot3(jnp.tanh(wc), wup_ref[d]))) - 0.5
        lw_ref[d] = -jnp.exp(wlog)
        a = _sigmoid(a0_ref[d:d + 1, :] + _dot3(ac, aup_ref[d]))
        kd = k * (1.0 + (a - 1.0) * ka_ref[...])
        kd_ref[d] = kd
        be_ref[d] = kap * a
        kd_sum = kd_sum + kd
    bonus_ref[...] = _head_sum(r * kd_sum * rk_ref[...], e, et) * v


def rwkv_prepare(code, lora, op, tm=256):
    W = RW_H * RW_N
    heads = jnp.arange(W, dtype=jnp.int32) // RW_N
    e = (heads[:, None] == jnp.arange(128, dtype=jnp.int32)[None, :]).astype(BF16)
    et = jnp.transpose(e)
    gup = jnp.pad(op['g_up'], ((0, 256 - op['g_up'].shape[0]), (0, 0)))
    row = lambda a: a.reshape(1, W)
    full2 = lambda shape: pl.BlockSpec(shape, lambda i: (0,) * len(shape))
    outs = pl.pallas_call(
        _rw_prep_kernel,
        out_shape=(jax.ShapeDtypeStruct((2, N_ROWS, W), F32), jax.ShapeDtypeStruct((2, N_ROWS, W), F32),
                   jax.ShapeDtypeStruct((2, N_ROWS, W), F32), jax.ShapeDtypeStruct((N_ROWS, W), F32),
                   jax.ShapeDtypeStruct((N_ROWS, W), F32), jax.ShapeDtypeStruct((N_ROWS, W), F32)),
        grid=(N_ROWS // tm,),
        in_specs=[pl.BlockSpec((tm, W), lambda i: (i, 0)), pl.BlockSpec((tm, W), lambda i: (i, 1)),
                  pl.BlockSpec((tm, W), lambda i: (i, 2)), pl.BlockSpec((tm, 384), lambda i: (i, 0)),
                  full2((2, W)), full2((2, 64, W)), full2((2, W)), full2((2, 64, W)), full2((256, W)),
                  full2((1, W)), full2((1, W)), full2((1, W)), full2((W, 128)), full2((128, W))],
        out_specs=(pl.BlockSpec((2, tm, W), lambda i: (0, i, 0)), pl.BlockSpec((2, tm, W), lambda i: (0, i, 0)),
                   pl.BlockSpec((2, tm, W), lambda i: (0, i, 0)), pl.BlockSpec((tm, W), lambda i: (i, 0)),
                   pl.BlockSpec((tm, W), lambda i: (i, 0)), pl.BlockSpec((tm, W), lambda i: (i, 0))),
        compiler_params=_cp(("parallel",)), name="rwkv_prepare",
    )(code, code, code, lora, op['w0'], op['w_up'], op['a0'], op['a_up'], gup, row(op['k_k']), row(op['k_a']),
      row(op['r_k']), e, et)
    return outs


def _tri_inv(n, eye, masks):
    bd = lambda a, b: _dot(a.astype(BF16), b.astype(BF16))
    d0 = jnp.where(masks[0], n, 0.0)
    d2 = bd(d0, d0)
    d4 = bd(d2, d2)
    t = bd(bd(eye + d0, eye + d2), eye + d4)
    for m in masks[1:]:
        e = jnp.where(m, n, 0.0)
        t = t + bd(t, bd(e, t))
    return t


def _rw_scan_kernel(r_ref, v_ref, lw_ref, kd_ref, be_ref, kap_ref, y_ref, st_ref):
    d = pl.program_id(0)
    c = pl.program_id(2)
    C = RW_CHUNK
    N = RW_N

    @pl.when(c == 0)
    def _():
        st_ref[...] = jnp.zeros_like(st_ref)

    isb = d == 1
    sgn = 1 - 2 * d
    ii = lax.broadcasted_iota(jnp.int32, (C, C), 0)
    jj = lax.broadcasted_iota(jnp.int32, (C, C), 1)
    dif = sgn * (ii - jj)
    incl = dif >= 0
    strict = dif > 0
    tri = incl.astype(BF16)
    eye = (ii == jj).astype(F32)
    blk = [(ii >> s) == (jj >> s) for s in (3, 4, 5)]
    masks = [blk[0], blk[1] & ~blk[0], blk[2] & ~blk[1], ~blk[2]]
    lw = lw_ref[...]
    cum = _dot2r(tri, lw)
    ec = jnp.exp(cum)
    en = jnp.exp(-cum)
    ea = jnp.exp(cum - lw)
    last = jnp.where(isb, cum[0:1, :], cum[C - 1:C, :])
    el = jnp.exp(last - cum)
    kap = kap_ref[...]
    r = r_ref[...]
    v = v_ref[...]
    a_t = -kap * ea
    r_t = r * ec
    b_t = be_ref[...] * en
    k_t = kd_ref[...] * en
    b_l = be_ref[...] * el
    k_l = kd_ref[...] * el
    pc = jnp.exp(last)
    H = range(RW_H)
    sl = [slice(h * N, (h + 1) * N) for h in H]
    bd = lambda a, b: _dot(a.astype(BF16), b.astype(BF16))
    tn = lambda a, b: lax.dot_general(a, b, (((0,), (0,)), ((), ())), preferred_element_type=F32)
    sc = [_dot_nt(jnp.concatenate([a_t[:, sl[h]], r_t[:, sl[h]]], axis=0).astype(BF16),
                  jnp.concatenate([b_t[:, sl[h]], k_t[:, sl[h]]], axis=0).astype(BF16)) for h in H]
    n_ab = [jnp.where(strict, sc[h][0:C, 0:C], 0.0) for h in H]
    a_ak = [jnp.where(strict, sc[h][0:C, C:2 * C], 0.0).astype(BF16) for h in H]
    m_rb = [jnp.where(incl, sc[h][C:2 * C, 0:C], 0.0).astype(BF16) for h in H]
    m_rk = [jnp.where(incl, sc[h][C:2 * C, C:2 * C], 0.0).astype(BF16) for h in H]
    vh = [v[:, sl[h]].astype(BF16) for h in H]
    d0 = [jnp.where(masks[0], n_ab[h], 0.0) for h in H]
    d2 = [bd(d0[h], d0[h]) for h in H]
    d4 = [bd(d2[h], d2[h]) for h in H]
    t = [bd(eye + d0[h], eye + d2[h]) for h in H]
    t = [bd(t[h], eye + d4[h]) for h in H]
    for m in masks[1:]:
        et = [bd(jnp.where(m, n_ab[h], 0.0), t[h]) for h in H]
        t = [t[h] + bd(t[h], et[h]) for h in H]
    av = [_dot(a_ak[h], vh[h]) for h in H]
    wub = [bd(t[h], jnp.concatenate([a_t[:, sl[h]], av[h]], axis=1)).astype(BF16) for h in H]
    mv = [_dot(m_rk[h], vh[h]) for h in H]
    kv = [tn(k_l[:, sl[h]].astype(BF16), vh[h]) for h in H]
    qy = [_dot(m_rb[h], wub[h]) + jnp.concatenate([r_t[:, sl[h]], mv[h]], axis=1) for h in H]
    pp = [tn(b_l[:, sl[h]].astype(BF16), wub[h]) + jnp.concatenate([eye * pc[:, sl[h]], kv[h]], axis=1)
          for h in H]
    h_old = [st_ref[h] for h in H]
    ys = [_dot3(qy[h][:, 0:N], h_old[h]) + qy[h][:, N:2 * N] for h in H]
    for h in H:
        st_ref[h] = _dot3(pp[h][:, 0:N], h_old[h]) + pp[h][:, N:2 * N]
    y_ref[...] = jnp.concatenate(ys, axis=1)


def _rw_rowblock(d, b, c):
    n_ctx = CTX // RW_CHUNK
    n_lat = SEQ // RW_CHUNK
    cc = jnp.where(d == 0, c, n_ctx - 1 - c)
    lc = jnp.where(d == 0, c - n_ctx, n_ctx + n_lat - 1 - c)
    return jnp.where(c < n_ctx, N_LAT // RW_CHUNK + b * n_ctx + cc, b * n_lat + lc)


def rwkv_scan(code, lw, kd, be, kap):
    W = RW_H * RW_N
    nch = (CTX + SEQ) // RW_CHUNK
    rb = lambda d, b, c: _rw_rowblock(d, b, c)
    return pl.pallas_call(
        _rw_scan_kernel,
        out_shape=jax.ShapeDtypeStruct((2, N_ROWS, W), F32),
        grid=(2, BATCH, nch),
        in_specs=[pl.BlockSpec((RW_CHUNK, W), lambda d, b, c: (rb(d, b, c), 0)),
                  pl.BlockSpec((RW_CHUNK, W), lambda d, b, c: (rb(d, b, c), 2)),
                  pl.BlockSpec((None, RW_CHUNK, W), lambda d, b, c: (d, rb(d, b, c), 0)),
                  pl.BlockSpec((None, RW_CHUNK, W), lambda d, b, c: (d, rb(d, b, c), 0)),
                  pl.BlockSpec((None, RW_CHUNK, W), lambda d, b, c: (d, rb(d, b, c), 0)),
                  pl.BlockSpec((RW_CHUNK, W), lambda d, b, c: (rb(d, b, c), 0))],
        out_specs=pl.BlockSpec((None, RW_CHUNK, W), lambda d, b, c: (d, rb(d, b, c), 0)),
        scratch_shapes=[pltpu.VMEM((RW_H, RW_N, RW_N), F32)],
        compiler_params=_cp(("parallel", "parallel", "arbitrary")), name="rwkv_scan",
    )(code, code, lw, kd, be, kap)


def _rw_out_kernel(yf_ref, yb_ref, bonus_ref, g_ref, lnw_ref, lnb_ref, e_ref, et_ref, o_ref):
    e, et = e_ref[...], et_ref[...]
    y = yf_ref[...] + yb_ref[...]
    mean = _head_sum(y, e, et) * (1.0 / RW_N)
    yc = y - mean
    var = _head_sum(yc * yc, e, et) * (1.0 / RW_N)
    yn = yc * lax.rsqrt(var + RW_GN_EPS) * lnw_ref[...] + lnb_ref[...]
    o_ref[...] = (yn + bonus_ref[...]) * g_ref[...]


def rwkv_output(y2, bonus, g, op, tm=256):
    W = RW_H * RW_N
    heads = jnp.arange(W, dtype=jnp.int32) // RW_N
    e = (heads[:, None] == jnp.arange(128, dtype=jnp.int32)[None, :]).astype(BF16)
    et = jnp.transpose(e)
    M = N_LAT
    return pl.pallas_call(
        _rw_out_kernel, out_shape=jax.ShapeDtypeStruct((M, W), F32), grid=(M // tm,),
        in_specs=[pl.BlockSpec((None, tm, W), lambda i: (0, i, 0)), pl.BlockSpec((None, tm, W), lambda i: (1, i, 0)),
                  pl.BlockSpec((tm, W), lambda i: (i, 0)), pl.BlockSpec((tm, W), lambda i: (i, 0)),
                  pl.BlockSpec((1, W), lambda i: (0, 0)), pl.BlockSpec((1, W), lambda i: (0, 0)),
                  pl.BlockSpec((W, 128), lambda i: (0, 0)), pl.BlockSpec((128, W), lambda i: (0, 0))],
        out_specs=pl.BlockSpec((tm, W), lambda i: (i, 0)),
        compiler_params=_cp(("parallel",)), name="rwkv_output",
    )(y2, y2, bonus, g, op['ln_w'].reshape(1, W), op['ln_b'].reshape(1, W), e, et)


AT_Q = RW_H * AT_HD
AT_KW = AT_KV * AT_HD
AT_TQ = 512
AT_TK = 768


def _rope_tables(tm):
    half = AT_HD // 2
    inv = 10000.0 ** (-jnp.arange(0, half, 2, dtype=F32) / half)
    pos = jnp.arange(SEQ, dtype=jnp.int32)
    row = (pos // GRID_W).astype(F32)[:, None] * inv
    col = (pos % GRID_W).astype(F32)[:, None] * inv
    cos_h = jnp.concatenate([jnp.cos(row), jnp.cos(row), jnp.cos(col), jnp.cos(col)], axis=1)
    sin_h = jnp.concatenate([-jnp.sin(row), jnp.sin(row), -jnp.sin(col), jnp.sin(col)], axis=1)
    cos_t = jnp.concatenate([jnp.tile(cos_h, (1, 2)), jnp.ones((tm, 128), F32)], axis=0)
    sin_t = jnp.concatenate([jnp.tile(sin_h, (1, 2)), jnp.zeros((tm, 128), F32)], axis=0)
    return cos_t, sin_t


def _rot_partner(x):
    q = AT_HD // 4
    w = x.shape[1]
    lane = lax.broadcasted_iota(jnp.int32, x.shape, 1)
    return jnp.where((lane % (2 * q)) < q, pltpu.roll(x, w - q, 1), pltpu.roll(x, q, 1))


def _at_prep_kernel(q_ref, k_ref, v_ref, cos_ref, sin_ref, qn_ref, kn_ref, e_ref, et_ref, qo_ref, ko_ref, vo_ref):
    e, et = e_ref[...], et_ref[...]
    cos2, sin2 = cos_ref[...], sin_ref[...]

    def norm_rope(x, gain, nrep):
        ms = _head_sum(x * x, e[:x.shape[1]], et[:, :x.shape[1]]) * (1.0 / AT_HD)
        xn = x * lax.rsqrt(ms + EPS) * gain
        cos = jnp.tile(cos2, (1, nrep))
        sin = jnp.tile(sin2, (1, nrep))
        return xn * cos + _rot_partner(xn) * sin

    qn = norm_rope(q_ref[...], qn_ref[...], AT_Q // 128) * (AT_HD ** -0.5 * math.log2(math.e))
    qo_ref[...] = jnp.transpose(qn).astype(BF16)
    ko_ref[...] = norm_rope(k_ref[...], kn_ref[...], AT_KW // 128).astype(BF16)
    vo_ref[...] = jnp.transpose(v_ref[...]).astype(BF16)


def attention_prepare(p, q_norm, k_norm, tm=256):
    cos_t, sin_t = _rope_tables(tm)
    heads = jnp.arange(AT_Q, dtype=jnp.int32) // AT_HD
    e = (heads[:, None] == jnp.arange(128, dtype=jnp.int32)[None, :]).astype(BF16)
    et = jnp.transpose(e)
    tab = lambda i: jnp.where(i * tm < N_LAT, ((i * tm) % SEQ) // tm, SEQ // tm)
    n_lat_t, n_seq_t, n_ctx_t = N_LAT // tm, SEQ // tm, CTX // tm
    kvb = lambda i: jnp.where(i < n_lat_t, (i // n_seq_t) * (n_seq_t + n_ctx_t) + n_ctx_t + i % n_seq_t,
                              ((i - n_lat_t) // n_ctx_t) * (n_seq_t + n_ctx_t) + (i - n_lat_t) % n_ctx_t)
    qcol = (3 * RW_H * RW_N) // AT_Q
    kcol = (3 * RW_H * RW_N + AT_Q) // AT_KW
    return pl.pallas_call(
        _at_prep_kernel,
        out_shape=(jax.ShapeDtypeStruct((AT_Q, N_ROWS), BF16), jax.ShapeDtypeStruct((N_ROWS, AT_KW), BF16),
                   jax.ShapeDtypeStruct((AT_KW, N_ROWS), BF16)),
        grid=(N_ROWS // tm,),
        in_specs=[pl.BlockSpec((tm, AT_Q), lambda i: (i, qcol)),
                  pl.BlockSpec((tm, AT_KW), lambda i: (i, kcol)),
                  pl.BlockSpec((tm, AT_KW), lambda i: (i, kcol + 1)),
                  pl.BlockSpec((tm, 128), lambda i: (tab(i), 0)),
                  pl.BlockSpec((tm, 128), lambda i: (tab(i), 0)),
                  pl.BlockSpec((1, AT_Q), lambda i: (0, 0)),
                  pl.BlockSpec((1, AT_KW), lambda i: (0, 0)),
                  pl.BlockSpec((AT_Q, 128), lambda i: (0, 0)),
                  pl.BlockSpec((128, AT_Q), lambda i: (0, 0))],
        out_specs=(pl.BlockSpec((AT_Q, tm), lambda i: (0, i)), pl.BlockSpec((tm, AT_KW), lambda i: (kvb(i), 0)),
                   pl.BlockSpec((AT_KW, tm), lambda i: (0, kvb(i)))),
        compiler_params=_cp(("parallel",)), name="attn_prepare",
    )(p, p, p, cos_t, sin_t, jnp.tile(q_norm, AT_Q // AT_HD).reshape(1, AT_Q),
      jnp.tile(k_norm, AT_KV).reshape(1, AT_KW), e, et)


def _flash_t_kernel(qt_ref, k_ref, vt_ref, o_ref, m_sc, l_sc, acc_sc):
    ki = pl.program_id(2)
    nq = AT_Q // AT_HD
    gq = nq // AT_KV

    @pl.when(ki == 0)
    def _():
        m_sc[...] = jnp.full_like(m_sc, -1e30)
        l_sc[...] = jnp.zeros_like(l_sc)
        acc_sc[...] = jnp.zeros_like(acc_sc)

    for g in range(AT_KV):
        kg = k_ref[:, g * AT_HD:(g + 1) * AT_HD]
        vtg = vt_ref[g * AT_HD:(g + 1) * AT_HD, :]
        hs = range(g * gq, (g + 1) * gq)
        st = [_dot(kg, qt_ref[h * AT_HD:(h + 1) * AT_HD, :]) for h in hs]
        m_old = [m_sc[h] for h in hs]
        m_new = [jnp.maximum(m_old[i], jnp.max(st[i], axis=0, keepdims=True)) for i in range(gq)]
        alpha = [jnp.exp2(m_old[i] - m_new[i]) for i in range(gq)]
        pt = [jnp.exp2(st[i] - m_new[i]) for i in range(gq)]
        for i, h in enumerate(hs):
            l_sc[h] = alpha[i] * l_sc[h] + jnp.sum(pt[i], axis=0, keepdims=True)
            m_sc[h] = m_new[i]
        pv = [_dot(vtg, pt[i].astype(BF16)) for i in range(gq)]
        for i, h in enumerate(hs):
            rows = pl.ds(h * AT_HD, AT_HD)
            acc_sc[rows, :] = alpha[i] * acc_sc[rows, :] + pv[i]

    @pl.when(ki == pl.num_programs(2) - 1)
    def _():
        inv = jnp.concatenate([jnp.broadcast_to(1.0 / l_sc[h], (AT_HD, l_sc.shape[2])) for h in range(nq)], axis=0)
        o_ref[...] = jnp.transpose(acc_sc[...] * inv)


def flash_attention_t(qt, k, vt):
    nq = AT_Q // AT_HD
    nk = (CTX + SEQ) // AT_TK
    kv_rb = lambda b, ki: b * nk + ki
    return pl.pallas_call(
        _flash_t_kernel,
        out_shape=jax.ShapeDtypeStruct((N_LAT, AT_Q), F32),
        grid=(BATCH, SEQ // AT_TQ, nk),
        in_specs=[pl.BlockSpec((AT_Q, AT_TQ), lambda b, qi, ki: (0, b * (SEQ // AT_TQ) + qi)),
                  pl.BlockSpec((AT_TK, AT_KW), lambda b, qi, ki: (kv_rb(b, ki), 0)),
                  pl.BlockSpec((AT_KW, AT_TK), lambda b, qi, ki: (0, kv_rb(b, ki)))],
        out_specs=pl.BlockSpec((AT_TQ, AT_Q), lambda b, qi, ki: (b * (SEQ // AT_TQ) + qi, 0)),
        scratch_shapes=[pltpu.VMEM((nq, 1, AT_TQ), F32), pltpu.VMEM((nq, 1, AT_TQ), F32),
                        pltpu.VMEM((AT_Q, AT_TQ), F32)],
        compiler_params=_cp(("parallel", "parallel", "arbitrary")), name="flash_attention")(qt, k, vt)


OD_PAD_N = 4992


def odd_mixer(x, mods, g_pre, g_post, op):
    W = RW_H * RW_N
    w = op['w_in']
    c3 = 3 * W
    code_n = c3 + 64 + 64 + 160
    w_perm = jnp.concatenate([w[:, :c3], w[:, code_n:], w[:, c3:code_n],
                              jnp.zeros((D, OD_PAD_N - w.shape[1]), F32)], axis=1).astype(BF16)
    p = norm_mod_matmul(x, g_pre, mods, w_perm, 0, 1, tn=384, name="odd_in_proj")
    mu = op['mu']
    taps = lambda m: jnp.stack([0.5 * m, 1.0 - m, 0.5 * m], axis=1)
    code = dwconv3(p, 0, c3, taps(mu[:c3]), jnp.zeros((c3,), F32), False, "rwkv_shift")
    lo_col = c3 + AT_Q + 2 * AT_KW
    mu_lo = jnp.pad(mu[c3:], (0, 384 - (code_n - c3)))
    lora = dwconv3(p, lo_col, 384, taps(mu_lo), jnp.zeros((384,), F32), False, "rwkv_shift_lora", cb=128)
    lw, kd, be, kap, g, bonus = rwkv_prepare(code, lora, op)
    y2 = rwkv_scan(code, lw, kd, be, kap)
    o_l = rwkv_output(y2, bonus, g, op)
    q, k, v = attention_prepare(p, op['q_norm'], op['k_norm'])
    a_l = flash_attention_t(q, k, v)
    return outproj_residual(o_l, a_l, op['w_out'].astype(BF16), x[:N_LAT], g_post, mods, 2, name="odd_out_proj")


def _router_kernel(x_ref, g_ref, mod_ref, rw_ref, rb_ref, s1_ref, s3_ref, s2_ref, t_ref, idx_ref, wt_ref, sh_ref):
    x = x_ref[...]
    ms = jnp.mean(x * x, axis=-1, keepdims=True)
    t = x * lax.rsqrt(ms + EPS) * g_ref[...] * (1.0 + mod_ref[4:5, :]) + mod_ref[3:4, :]
    t_ref[...] = t
    tb = t.astype(BF16)
    sh_ref[...] = _dot((_silu(_dot(tb, s1_ref[...])) * _dot(tb, s3_ref[...])).astype(BF16), s2_ref[...])
    th, tl = _split(t)
    wh, wl = _split(rw_ref[...])
    lg = _dot_nt(wh, th) + (_dot_nt(wh, tl) + _dot_nt(wl, th))
    sc = _sigmoid(lg)
    sel = sc + rb_ref[...]
    tm = sel.shape[1]
    gsz = N_EXP // N_GRP
    ninf = -jnp.inf
    sel3 = sel.reshape(N_GRP, gsz, tm)
    i3 = lax.broadcasted_iota(jnp.int32, sel3.shape, 1)
    m1 = jnp.max(sel3, axis=1, keepdims=True)
    first = jnp.min(jnp.where(sel3 == m1, i3, gsz), axis=1, keepdims=True)
    m2 = jnp.max(jnp.where(i3 == first, ninf, sel3), axis=1, keepdims=True)
    grp = (m1 + m2).reshape(N_GRP, tm)
    gi = lax.broadcasted_iota(jnp.int32, grp.shape, 0)
    gmask = jnp.zeros(grp.shape, F32)
    for _ in range(TOPK_GRP):
        m = jnp.max(grp, axis=0, keepdims=True)
        pick = jnp.min(jnp.where(grp == m, gi, N_GRP), axis=0, keepdims=True)
        hit = gi == pick
        gmask = jnp.where(hit, 1.0, gmask)
        grp = jnp.where(hit, ninf, grp)
    emask = jnp.broadcast_to(gmask.reshape(N_GRP, 1, tm), (N_GRP, gsz, tm)).reshape(N_EXP, tm)
    msel = jnp.where(emask > 0.5, sel, ninf)
    ei = lax.broadcasted_iota(jnp.int32, msel.shape, 0)
    idxs, ws = [], []
    for _ in range(TOP_K):
        m = jnp.max(msel, axis=0, keepdims=True)
        pick = jnp.min(jnp.where(msel == m, ei, N_EXP), axis=0, keepdims=True)
        hit = ei == pick
        idxs.append(pick)
        ws.append(jnp.sum(jnp.where(hit, sc, 0.0), axis=0, keepdims=True))
        msel = jnp.where(hit, ninf, msel)
    w = jnp.concatenate(ws, axis=0)
    idx_ref[...] = jnp.concatenate(idxs, axis=0)
    wt_ref[...] = w / jnp.sum(w, axis=0, keepdims=True) * ROUTED_SCALE


def moe_router(x, M, g, mods, mp, tm=256):
    full = lambda shape: pl.BlockSpec(shape, lambda i: (0,) * len(shape))
    return pl.pallas_call(
        _router_kernel,
        out_shape=(jax.ShapeDtypeStruct((M, D), F32), jax.ShapeDtypeStruct((TOP_K, M), jnp.int32),
                   jax.ShapeDtypeStruct((TOP_K, M), F32), jax.ShapeDtypeStruct((M, D), F32)),
        grid=(M // tm,),
        in_specs=[pl.BlockSpec((tm, D), lambda i: (i, 0)), full((1, D)),
                  pl.BlockSpec((None, 6, D), lambda i: (_seq_of_rowblock(i, tm), 0, 0)),
                  full((N_EXP, D)), full((N_EXP, 1)), full((D, EXP_FF)), full((D, EXP_FF)), full((EXP_FF, D))],
        out_specs=(pl.BlockSpec((tm, D), lambda i: (i, 0)), pl.BlockSpec((TOP_K, tm), lambda i: (0, i)),
                   pl.BlockSpec((TOP_K, tm), lambda i: (0, i)), pl.BlockSpec((tm, D), lambda i: (i, 0))),
        compiler_params=_cp(("parallel",)), name="moe_router",
    )(x, g.reshape(1, D), mods, jnp.transpose(mp['router_w']), mp['router_bias'].reshape(N_EXP, 1),
      mp['s1'].astype(BF16), mp['s3'].astype(BF16), mp['s2'].astype(BF16))


def _gather_rows(idx_ref, n, src_hbm, dst, sem, slot):
    def body(r, carry):
        pltpu.make_async_copy(src_hbm.at[pl.ds(idx_ref[0, r], 1)], dst.at[slot, pl.ds(r, 1)], sem.at[slot]).start()
        return carry

    lax.fori_loop(0, n, body, 0)


def _wait_rows(n, src_hbm, dst, sem, slot):
    pltpu.make_async_copy(src_hbm.at[pl.ds(0, n)], dst.at[slot], sem.at[slot]).wait()


def _expert_kernel(be_ref, tok_ref, tokn_ref, w_ref, t_hbm, w1_ref, w3_ref, w2_ref, o_ref, xbuf, sem):
    i = pl.program_id(0)
    n = pl.num_programs(0)
    slot = i % 2

    @pl.when(i == 0)
    def _():
        _gather_rows(tok_ref, MOE_BLK, t_hbm, xbuf, sem, 0)

    @pl.when(i + 1 < n)
    def _():
        _gather_rows(tokn_ref, MOE_BLK, t_hbm, xbuf, sem, 1 - slot)

    _wait_rows(MOE_BLK, t_hbm, xbuf, sem, slot)
    xb = xbuf[slot].astype(BF16)
    h = _silu(_dot(xb, w1_ref[...])) * _dot(xb, w3_ref[...])
    o_ref[...] = _dot(h.astype(BF16), w2_ref[...]) * w_ref[...]


def moe_experts(t, buf_tok, buf_w, block_e, w1, w3, w2):
    nb = block_e.shape[0]
    tok3 = buf_tok.reshape(nb, 1, MOE_BLK)
    nxt = lambda i, be: (jnp.minimum(i + 1, nb - 1), 0, 0)
    grid_spec = pltpu.PrefetchScalarGridSpec(
        num_scalar_prefetch=1, grid=(nb,),
        in_specs=[pl.BlockSpec((None, 1, MOE_BLK), lambda i, be: (i, 0, 0), memory_space=pltpu.SMEM),
                  pl.BlockSpec((None, 1, MOE_BLK), nxt, memory_space=pltpu.SMEM),
                  pl.BlockSpec((MOE_BLK, 1), lambda i, be: (i, 0)),
                  pl.BlockSpec(memory_space=pl.ANY),
                  pl.BlockSpec((None, D, EXP_FF), lambda i, be: (be[i], 0, 0)),
                  pl.BlockSpec((None, D, EXP_FF), lambda i, be: (be[i], 0, 0)),
                  pl.BlockSpec((None, EXP_FF, D), lambda i, be: (be[i], 0, 0))],
        out_specs=pl.BlockSpec((MOE_BLK, D), lambda i, be: (i, 0)),
        scratch_shapes=[pltpu.VMEM((2, MOE_BLK, D), F32), pltpu.SemaphoreType.DMA((2,))])
    return pl.pallas_call(
        _expert_kernel, out_shape=jax.ShapeDtypeStruct((nb * MOE_BLK, D), F32), grid_spec=grid_spec,
        compiler_params=_cp(("arbitrary",)), name="moe_experts",
    )(block_e, tok3, tok3, buf_w.reshape(nb * MOE_BLK, 1), t, w1, w3, w2)


MOE_TT = 32


def _combine_kernel(dst_ref, dstn_ref, ys_hbm, sh_ref, x_ref, g_ref, mod_ref, o_ref, buf, sem):
    i = pl.program_id(0)
    n = pl.num_programs(0)
    slot = i % 2
    nrow = MOE_TT * TOP_K

    @pl.when(i == 0)
    def _():
        _gather_rows(dst_ref, nrow, ys_hbm, buf, sem, 0)

    @pl.when(i + 1 < n)
    def _():
        _gather_rows(dstn_ref, nrow, ys_hbm, buf, sem, 1 - slot)

    _wait_rows(nrow, ys_hbm, buf, sem, slot)
    f = sh_ref[...]
    for k in range(TOP_K):
        f = f + buf[slot, k * MOE_TT:(k + 1) * MOE_TT, :]
    ms = jnp.mean(f * f, axis=-1, keepdims=True)
    o_ref[...] = x_ref[...] + mod_ref[5:6, :] * (f * lax.rsqrt(ms + EPS) * g_ref[...])


def moe_combine(ys, dest, sh, x, M, g, mods):
    nt = M // MOE_TT
    nrow = MOE_TT * TOP_K
    d3 = dest.reshape(nt, MOE_TT, TOP_K).transpose(0, 2, 1).reshape(nt, 1, nrow)
    nxt = lambda i: (jnp.minimum(i + 1, nt - 1), 0, 0)
    return pl.pallas_call(
        _combine_kernel, out_shape=jax.ShapeDtypeStruct((M, D), F32), grid=(nt,),
        in_specs=[pl.BlockSpec((None, 1, nrow), lambda i: (i, 0, 0), memory_space=pltpu.SMEM),
                  pl.BlockSpec((None, 1, nrow), nxt, memory_space=pltpu.SMEM),
                  pl.BlockSpec(memory_space=pl.ANY),
                  pl.BlockSpec((MOE_TT, D), lambda i: (i, 0)),
                  pl.BlockSpec((MOE_TT, D), lambda i: (i, 0)),
                  pl.BlockSpec((1, D), lambda i: (0, 0)),
                  pl.BlockSpec((None, 6, D), lambda i: (_seq_of_rowblock(i, MOE_TT), 0, 0))],
        out_specs=pl.BlockSpec((MOE_TT, D), lambda i: (i, 0)),
        scratch_shapes=[pltpu.VMEM((2, nrow, D), F32), pltpu.SemaphoreType.DMA((2,))],
        compiler_params=_cp(("arbitrary",)), name="moe_combine",
    )(d3, d3, ys, sh, x, g.reshape(1, D), mods)


def moe_layer(x, M, g_pre, g_post, mods, mp):
    t, idx_t, wts_t, sh = moe_router(x, M, g_pre, mods, mp)
    mk = M * TOP_K
    nb = -(-(mk + N_EXP * (MOE_BLK - 1)) // MOE_BLK)
    flat_e = jnp.transpose(idx_t).reshape(mk)
    flat_w = jnp.transpose(wts_t).reshape(mk)
    onehot = (flat_e[:, None] == jnp.arange(N_EXP, dtype=jnp.int32)[None, :]).astype(jnp.int32)
    csum = jnp.cumsum(onehot, axis=0)
    rank = jnp.take_along_axis(csum, flat_e[:, None], axis=1)[:, 0] - 1
    counts = csum[-1]
    padded = (counts + MOE_BLK - 1) // MOE_BLK * MOE_BLK
    pend = jnp.cumsum(padded)
    dest = (pend - padded)[flat_e] + rank
    flat_tok = jnp.arange(mk, dtype=jnp.int32) // TOP_K
    buf_tok = jnp.zeros((nb * MOE_BLK,), jnp.int32).at[dest].set(flat_tok)
    buf_w = jnp.zeros((nb * MOE_BLK,), F32).at[dest].set(flat_w)
    block_e = jnp.minimum(jnp.searchsorted(pend, jnp.arange(nb, dtype=jnp.int32) * MOE_BLK, side='right'),
                          N_EXP - 1).astype(jnp.int32)
    ys = moe_experts(t, buf_tok, buf_w, block_e, mp['w1'].astype(BF16), mp['w3'].astype(BF16),
                     mp['w2'].astype(BF16))
    return moe_combine(ys, dest.astype(jnp.int32), sh, x, M, g_post, mods)


MOE_T = 512
MOE_CAP = 128
MOE_EPS = 4


def _router2_kernel(x_ref, g_ref, mod_ref, rw_ref, rb_ref, s1_ref, s3_ref, s2_ref, t_ref, wt_ref, cnt_ref, sh_ref):
    x = x_ref[...]
    ms = jnp.mean(x * x, axis=-1, keepdims=True)
    t = x * lax.rsqrt(ms + EPS) * g_ref[...] * (1.0 + mod_ref[4:5, :]) + mod_ref[3:4, :]
    tb = t.astype(BF16)
    t_ref[...] = tb
    sh_ref[...] = _dot((_silu(_dot(tb, s1_ref[...])) * _dot(tb, s3_ref[...])).astype(BF16), s2_ref[...])
    th, tl = _split(t)
    wh, wl = _split(rw_ref[...])
    lg = _dot_nt(wh, th) + (_dot_nt(wh, tl) + _dot_nt(wl, th))
    sc = _sigmoid(lg)
    sel = sc + rb_ref[...]
    tm = sel.shape[1]
    gsz = N_EXP // N_GRP
    ninf = -jnp.inf
    sel3 = sel.reshape(N_GRP, gsz, tm)
    i3 = lax.broadcasted_iota(jnp.int32, sel3.shape, 1)
    m1 = jnp.max(sel3, axis=1, keepdims=True)
    first = jnp.min(jnp.where(sel3 == m1, i3, gsz), axis=1, keepdims=True)
    m2 = jnp.max(jnp.where(i3 == first, ninf, sel3), axis=1, keepdims=True)
    grp = (m1 + m2).reshape(N_GRP, tm)
    gi = lax.broadcasted_iota(jnp.int32, grp.shape, 0)
    gmask = jnp.zeros(grp.shape, F32)
    for _ in range(TOPK_GRP):
        m = jnp.max(grp, axis=0, keepdims=True)
        pick = jnp.min(jnp.where(grp == m, gi, N_GRP), axis=0, keepdims=True)
        hit = gi == pick
        gmask = jnp.where(hit, 1.0, gmask)
        grp = jnp.where(hit, ninf, grp)
    emask = jnp.broadcast_to(gmask.reshape(N_GRP, 1, tm), (N_GRP, gsz, tm)).reshape(N_EXP, tm)
    msel = jnp.where(emask > 0.5, sel, ninf)
    ei = lax.broadcasted_iota(jnp.int32, msel.shape, 0)
    chosen = jnp.zeros(msel.shape, F32)
    for _ in range(TOP_K):
        m = jnp.max(msel, axis=0, keepdims=True)
        pick = jnp.min(jnp.where(msel == m, ei, N_EXP), axis=0, keepdims=True)
        hit = ei == pick
        chosen = jnp.where(hit, 1.0, chosen)
        msel = jnp.where(hit, ninf, msel)
    w = chosen * sc
    wt = w / jnp.sum(w, axis=0, keepdims=True) * ROUTED_SCALE
    wt_ref[...] = wt
    cnt_ref[...] = jnp.sum((wt > 0.0).astype(F32), axis=1, keepdims=True).astype(jnp.int32)


def moe_router2(x, M, g, mods, mp):
    tm = MOE_T
    full = lambda shape: pl.BlockSpec(shape, lambda i: (0,) * len(shape))
    return pl.pallas_call(
        _router2_kernel,
        out_shape=(jax.ShapeDtypeStruct((M, D), BF16), jax.ShapeDtypeStruct((N_EXP, M), F32),
                   jax.ShapeDtypeStruct((M // tm, N_EXP, 1), jnp.int32), jax.ShapeDtypeStruct((M, D), F32)),
        grid=(M // tm,),
        in_specs=[pl.BlockSpec((tm, D), lambda i: (i, 0)), full((1, D)),
                  pl.BlockSpec((None, 6, D), lambda i: (_seq_of_rowblock(i, tm), 0, 0)),
                  full((N_EXP, D)), full((N_EXP, 1)), full((D, EXP_FF)), full((D, EXP_FF)), full((EXP_FF, D))],
        out_specs=(pl.BlockSpec((tm, D), lambda i: (i, 0)), pl.BlockSpec((N_EXP, tm), lambda i: (0, i)),
                   pl.BlockSpec((None, N_EXP, 1), lambda i: (i, 0, 0)), pl.BlockSpec((tm, D), lambda i: (i, 0))),
        compiler_params=_cp(("parallel",)), name="moe_router",
    )(x, g.reshape(1, D), mods, jnp.transpose(mp['router_w']), mp['router_bias'].reshape(N_EXP, 1),
      mp['s1'].astype(BF16), mp['s3'].astype(BF16), mp['s2'].astype(BF16))


def _moe2_kernel(cnt_ref, t_ref, wt_ref, sh_ref, x_ref, g_ref, mod_ref, w1_ref, w3_ref, w2_ref, o_ref,
                 rank_sc, acc_sc):
    i = pl.program_id(0)
    eb = pl.program_id(1)
    T = MOE_T

    @pl.when(eb == 0)
    def _():
        picked = (wt_ref[...] > 0.0).astype(BF16)
        before = (lax.broadcasted_iota(jnp.int32, (T, T), 0) < lax.broadcasted_iota(jnp.int32, (T, T), 1))
        rank_sc[...] = _dot(picked, before.astype(BF16))
        acc_sc[...] = jnp.zeros_like(acc_sc)

    slot = lax.broadcasted_iota(jnp.int32, (MOE_CAP, T), 0).astype(F32)
    for j in range(MOE_EPS):
        e = eb * MOE_EPS + j
        n_tok = cnt_ref[i * N_EXP + e]
        w_row = wt_ref[pl.ds(e, 1), :]
        r_row = rank_sc[pl.ds(e, 1), :]

        def chunk(ci, carry, j=j, w_row=w_row, r_row=r_row):
            hit = ((r_row - (ci * MOE_CAP).astype(F32)) == slot) & (w_row > 0.0)
            pb = hit.astype(F32).astype(BF16)
            xg = _dot(pb, t_ref[...]).astype(BF16)
            h = _silu(_dot(xg, w1_ref[j])) * _dot(xg, w3_ref[j])
            y = _dot(h.astype(BF16), w2_ref[j])
            w_slot = jnp.sum(jnp.where(hit, w_row, 0.0), axis=1, keepdims=True)
            yw = (y * w_slot).astype(BF16)
            acc_sc[...] += lax.dot_general(pb, yw, (((0,), (0,)), ((), ())), preferred_element_type=F32)
            return carry

        lax.fori_loop(0, (n_tok + MOE_CAP - 1) // MOE_CAP, chunk, 0)

    @pl.when(eb == pl.num_programs(1) - 1)
    def _():
        f = acc_sc[...] + sh_ref[...]
        ms = jnp.mean(f * f, axis=-1, keepdims=True)
        o_ref[...] = x_ref[...] + mod_ref[5:6, :] * (f * lax.rsqrt(ms + EPS) * g_ref[...])


def moe_layer2(x, M, g_pre, g_post, mods, mp):
    t, wt, cnt, sh = moe_router2(x, M, g_pre, mods, mp)
    T = MOE_T
    grid_spec = pltpu.PrefetchScalarGridSpec(
        num_scalar_prefetch=1, grid=(M // T, N_EXP // MOE_EPS),
        in_specs=[pl.BlockSpec((T, D), lambda i, e, c: (i, 0)),
                  pl.BlockSpec((N_EXP, T), lambda i, e, c: (0, i)),
                  pl.BlockSpec((T, D), lambda i, e, c: (i, 0)),
                  pl.BlockSpec((T, D), lambda i, e, c: (i, 0)),
                  pl.BlockSpec((1, D), lambda i, e, c: (0, 0)),
                  pl.BlockSpec((None, 6, D), lambda i, e, c: (_seq_of_rowblock(i, T), 0, 0)),
                  pl.BlockSpec((MOE_EPS, D, EXP_FF), lambda i, e, c: (e, 0, 0)),
                  pl.BlockSpec((MOE_EPS, D, EXP_FF), lambda i, e, c: (e, 0, 0)),
                  pl.BlockSpec((MOE_EPS, EXP_FF, D), lambda i, e, c: (e, 0, 0))],
        out_specs=pl.BlockSpec((T, D), lambda i, e, c: (i, 0)),
        scratch_shapes=[pltpu.VMEM((N_EXP, T), F32), pltpu.VMEM((T, D), F32)])
    return pl.pallas_call(
        _moe2_kernel, out_shape=jax.ShapeDtypeStruct((M, D), F32), grid_spec=grid_spec,
        compiler_params=_cp(("parallel", "arbitrary")), name="moe_experts",
    )(cnt.reshape(-1), t, wt, sh, x, g_post.reshape(1, D), mods, mp['w1'].astype(BF16), mp['w3'].astype(BF16),
      mp['w2'].astype(BF16))


def kernel(x, c, ctx, c_ctx, mod_w, mod_b, norm_mix_pre, norm_mix_post, norm_ffn_pre, norm_ffn_post, router_w, router_bias, expert_w1, expert_w3, expert_w2, shared_w1, shared_w3, shared_w2, ev_w_in, ev_w_out, ssd_conv_w, ssd_conv_b, ssd_dt_bias, ssd_a_log, ssd_d, ssd_norm_w, hy_conv_w, hy_conv_b, hy_mlp_w0, hy_mlp_b0, hy_freq0, hy_mlp_w1, hy_mlp_b1, hy_freq1, hy_mlp_w2, hy_bias, od_w_in, od_w_out, rw_mu, rw_w0, rw_w_up, rw_a0, rw_a_up, rw_g_up, rw_k_k, rw_k_a, rw_r_k, rw_ln_w, rw_ln_b, at_q_norm, at_k_norm):
    xs = jnp.concatenate([x.reshape(N_LAT, D), ctx.reshape(BATCH * CTX, D)], axis=0)
    cvecs = jnp.zeros((8, D), F32).at[0:BATCH].set(c).at[BATCH].set(c_ctx)
    assert mod_w.shape[0] == 2, "one even (SSD | Hyena) layer followed by one odd (RWKV | attention) layer"

    def moe_params(i):
        return dict(router_w=router_w[i], router_bias=router_bias[i], w1=expert_w1[i], w3=expert_w3[i],
                    w2=expert_w2[i], s1=shared_w1[i], s3=shared_w3[i], s2=shared_w2[i])

    mods = modulation(cvecs, mod_w[0], mod_b[0])[:BATCH + 1].reshape(BATCH + 1, 6, D)
    ep = dict(w_in=ev_w_in[0], w_out=ev_w_out[0], ssd_conv_w=ssd_conv_w[0], ssd_conv_b=ssd_conv_b[0],
              ssd_dt_bias=ssd_dt_bias[0], ssd_a_log=ssd_a_log[0], ssd_d=ssd_d[0], ssd_norm_w=ssd_norm_w[0],
              hy_conv_w=hy_conv_w[0], hy_conv_b=hy_conv_b[0], hy_mlp_w0=hy_mlp_w0[0], hy_mlp_b0=hy_mlp_b0[0],
              hy_freq0=hy_freq0[0], hy_mlp_w1=hy_mlp_w1[0], hy_mlp_b1=hy_mlp_b1[0], hy_freq1=hy_freq1[0],
              hy_mlp_w2=hy_mlp_w2[0], hy_bias=hy_bias[0])
    xs = even_mixer(xs, mods, norm_mix_pre[0], norm_mix_post[0], ep)
    xs = moe_layer2(xs, N_ROWS, norm_ffn_pre[0], norm_ffn_post[0], mods, moe_params(0))
    mods = modulation(cvecs, mod_w[1], mod_b[1])[:BATCH + 1].reshape(BATCH + 1, 6, D)
    op = dict(w_in=od_w_in[0], w_out=od_w_out[0], mu=rw_mu[0], w0=rw_w0[0], w_up=rw_w_up[0], a0=rw_a0[0],
              a_up=rw_a_up[0], g_up=rw_g_up[0], k_k=rw_k_k[0], k_a=rw_k_a[0], r_k=rw_r_k[0], ln_w=rw_ln_w[0],
              ln_b=rw_ln_b[0], q_norm=at_q_norm[0], k_norm=at_k_norm[0])
    xl = odd_mixer(xs, mods, norm_mix_pre[1], norm_mix_post[1], op)
    xl = moe_layer2(xl, N_LAT, norm_ffn_pre[1], norm_ffn_post[1], mods, moe_params(1))
    return xl.reshape(BATCH, SEQ, D)
```

```python
import functools
import math

import numpy as np
import jax
import jax.numpy as jnp
from jax import lax
from jax.experimental import pallas as pl
from jax.experimental.pallas import tpu as pltpu

F32 = jnp.float32
BF16 = jnp.bfloat16

D = 1024
BATCH = 2
SEQ = 8192
CTX = 256
N_LAT = BATCH * SEQ
N_ROWS = N_LAT + BATCH * CTX
EPS = 1e-6
GRID_W = 64

SSD_HEADS = 16
SSD_P = 64
SSD_G = 2
SSD_S = 128
SSD_Q = 128
HY_W = 1024
HY_EMB = 33
HY_HID = 64

RW_H = 16
RW_N = 64
RW_CHUNK = 64
RW_GN_EPS = 64e-5

AT_KV = 4
AT_HD = 64

N_EXP = 64
TOP_K = 8
N_GRP = 8
TOPK_GRP = 4
EXP_FF = 256
ROUTED_SCALE = 2.5
MOE_BLK = 128

VMEM_LIMIT = 56 * 1024 * 1024


def _cp(sem, vmem=None):
    return pltpu.CompilerParams(dimension_semantics=sem, vmem_limit_bytes=vmem or VMEM_LIMIT)


def _dot(a, b):
    return jnp.dot(a, b, preferred_element_type=F32)


def _dot_nt(a, b):
    return lax.dot_general(a, b, (((1,), (1,)), ((), ())), preferred_element_type=F32)


def _split(x):
    hi = x.astype(BF16)
    lo = (x - hi.astype(F32)).astype(BF16)
    return hi, lo


def _dot3(a, b):
    ah, al = _split(a)
    bh, bl = _split(b)
    return _dot(ah, bh) + (_dot(ah, bl) + _dot(al, bh))


def _dot2l(a, b):
    ah, al = _split(a)
    return _dot(ah, b) + _dot(al, b)


def _dot2r(a, b):
    bh, bl = _split(b)
    return _dot(a, bh) + _dot(a, bl)


def _silu(x):
    return x * (1.0 / (1.0 + jnp.exp(-x)))


def _sigmoid(x):
    return 1.0 / (1.0 + jnp.exp(-x))


def _softplus(x):
    return jnp.maximum(x, 0.0) + jnp.log(1.0 + jnp.exp(-jnp.abs(x)))


def _seq_of_rowblock(i, tm):
    return jnp.minimum((i * tm) // SEQ, 2)


def _mm_kernel(a_ref, b_ref, o_ref, *, passes):
    a = a_ref[...]
    b = b_ref[...]
    if passes == 3:
        o_ref[...] = _dot3(a.astype(F32), b.astype(F32))
    else:
        o_ref[...] = _dot(a.astype(BF16), b.astype(BF16))


def matmul(a, b, tm, tn, passes=1, name="mm"):
    M, K = a.shape
    N = b.shape[1]
    return pl.pallas_call(
        functools.partial(_mm_kernel, passes=passes),
        out_shape=jax.ShapeDtypeStruct((M, N), F32),
        grid=(M // tm, N // tn),
        in_specs=[pl.BlockSpec((tm, K), lambda i, j: (i, 0)),
                  pl.BlockSpec((K, tn), lambda i, j: (0, j))],
        out_specs=pl.BlockSpec((tm, tn), lambda i, j: (i, j)),
        compiler_params=_cp(("parallel", "parallel")), name=name)(a, b)


def _nmm_kernel(x_ref, g_ref, mod_ref, w_ref, o_ref, a_sc, *, shift_i, scale_i):
    @pl.when(pl.program_id(1) == 0)
    def _():
        x = x_ref[...]
        ms = jnp.mean(x * x, axis=-1, keepdims=True)
        y = x * lax.rsqrt(ms + EPS) * g_ref[...]
        h = y * (1.0 + mod_ref[scale_i:scale_i + 1, :]) + mod_ref[shift_i:shift_i + 1, :]
        a_sc[...] = h.astype(BF16)

    o_ref[...] = _dot(a_sc[...], w_ref[...])


def norm_mod_matmul(x, g, mods, w, shift_i, scale_i, tm=512, tn=None, name="nmm"):
    M = x.shape[0]
    N = w.shape[1]
    tn = tn or N
    return pl.pallas_call(
        functools.partial(_nmm_kernel, shift_i=shift_i, scale_i=scale_i),
        out_shape=jax.ShapeDtypeStruct((M, N), F32),
        grid=(M // tm, N // tn),
        in_specs=[pl.BlockSpec((tm, D), lambda i, j: (i, 0)),
                  pl.BlockSpec((1, D), lambda i, j: (0, 0)),
                  pl.BlockSpec((None, 6, D), lambda i, j: (_seq_of_rowblock(i, tm), 0, 0)),
                  pl.BlockSpec((D, tn), lambda i, j: (0, j))],
        out_specs=pl.BlockSpec((tm, tn), lambda i, j: (i, j)),
        scratch_shapes=[pltpu.VMEM((tm, D), BF16)],
        compiler_params=_cp(("parallel", "arbitrary")), name=name)(x, g.reshape(1, D), mods, w)


def _outproj_kernel(a1_ref, a2_ref, w_ref, x_ref, g_ref, mod_ref, o_ref, *, gate_i):
    y = _dot(a1_ref[...].astype(BF16), w_ref[0:D, :]) + _dot(a2_ref[...].astype(BF16), w_ref[D:2 * D, :])
    ms = jnp.mean(y * y, axis=-1, keepdims=True)
    o_ref[...] = x_ref[...] + mod_ref[gate_i:gate_i + 1, :] * (y * lax.rsqrt(ms + EPS) * g_ref[...])


def outproj_residual(a1, a2, w, x, g, mods, gate_i, tm=256, name="outproj"):
    M = a1.shape[0]
    return pl.pallas_call(
        functools.partial(_outproj_kernel, gate_i=gate_i),
        out_shape=jax.ShapeDtypeStruct((M, D), F32),
        grid=(M // tm,),
        in_specs=[pl.BlockSpec((tm, D), lambda i: (i, 0)),
                  pl.BlockSpec((tm, D), lambda i: (i, 0)),
                  pl.BlockSpec((2 * D, D), lambda i: (0, 0)),
                  pl.BlockSpec((tm, D), lambda i: (i, 0)),
                  pl.BlockSpec((1, D), lambda i: (0, 0)),
                  pl.BlockSpec((None, 6, D), lambda i: (_seq_of_rowblock(i, tm), 0, 0))],
        out_specs=pl.BlockSpec((tm, D), lambda i: (i, 0)),
        compiler_params=_cp(("parallel",)), name=name)(a1, a2, w, x, g.reshape(1, D), mods)


def _mod_kernel(c_ref, w_ref, b_ref, o_ref):
    o_ref[...] = _dot3(_silu(c_ref[...]), w_ref[...]) + b_ref[...]


def modulation(cvecs, w, b):
    N = w.shape[1]
    tn = 1024
    return pl.pallas_call(
        _mod_kernel, out_shape=jax.ShapeDtypeStruct((8, N), F32), grid=(N // tn,),
        in_specs=[pl.BlockSpec((8, D), lambda j: (0, 0)),
                  pl.BlockSpec((D, tn), lambda j: (0, j)),
                  pl.BlockSpec((1, tn), lambda j: (0, j))],
        out_specs=pl.BlockSpec((8, tn), lambda j: (0, j)),
        compiler_params=_cp(("parallel",)), name="modulation")(cvecs, w, b.reshape(1, N))


def _neighbours(x_ref, base, ch, length):
    cur = x_ref[pl.ds(base, ch), :]
    rows = lax.broadcasted_iota(jnp.int32, cur.shape, 0)
    pbase = pl.multiple_of(jnp.maximum(base - 8, 0), 8)
    nbase = pl.multiple_of(jnp.minimum(base + ch, length - 8), 8)
    prev_row = x_ref[pl.ds(pbase, 8), :][7:8, :] * (base > 0).astype(F32)
    next_row = x_ref[pl.ds(nbase, 8), :][0:1, :] * (base + ch < length).astype(F32)
    xm1 = jnp.where(rows == 0, prev_row, pltpu.roll(cur, 1, 0))
    xp1 = jnp.where(rows == ch - 1, next_row, pltpu.roll(cur, ch - 1, 0))
    return xm1, cur, xp1


def _conv3_kernel(x_ref, w_ref, b_ref, o_ref, *, length, ch, act):
    def body(c, carry):
        base = pl.multiple_of(c * ch, ch)
        xm1, cur, xp1 = _neighbours(x_ref, base, ch, length)
        y = xm1 * w_ref[0:1, :] + cur * w_ref[1:2, :] + xp1 * w_ref[2:3, :] + b_ref[...]
        if act:
            y = _silu(y)
        o_ref[pl.ds(base, ch), :] = y
        return carry

    lax.fori_loop(0, length // ch, body, 0)


def dwconv3(p, col0, ncols, w, b, act, name, cb=256):
    wt = jnp.transpose(w)
    b2 = b.reshape(1, ncols)
    outs = []
    for (length, row0, ch) in ((SEQ, 0, 512), (CTX, N_LAT, 256)):
        rb0 = row0 // length
        outs.append(pl.pallas_call(
            functools.partial(_conv3_kernel, length=length, ch=ch, act=act),
            out_shape=jax.ShapeDtypeStruct((BATCH * length, ncols), F32),
            grid=(BATCH, ncols // cb),
            in_specs=[pl.BlockSpec((length, cb), lambda s, j: (rb0 + s, col0 // cb + j)),
                      pl.BlockSpec((3, cb), lambda s, j: (0, j)),
                      pl.BlockSpec((1, cb), lambda s, j: (0, j))],
            out_specs=pl.BlockSpec((length, cb), lambda s, j: (s, j)),
            compiler_params=_cp(("parallel", "parallel")), name=name)(p, wt, b2))
    return jnp.concatenate(outs, axis=0)


def _ssd_kernel(xs_ref, bm_ref, cm_ref, dt_ref, dtT_ref, bias_ref, biasT_ref, alog_ref, alogT_ref,
                y_ref, st_ref):
    d = pl.program_id(0)
    c = pl.program_id(3)
    Q = SSD_Q
    HG = SSD_HEADS // SSD_G

    @pl.when(c == 0)
    def _():
        st_ref[...] = jnp.zeros_like(st_ref)

    isb = d == 1
    sgn = 1 - 2 * d
    dt = _softplus(dt_ref[...] + bias_ref[...])
    dtT = _softplus(dtT_ref[...] + biasT_ref[...])
    a = dt * (-jnp.exp(alog_ref[...]))
    aT = dtT * (-jnp.exp(alogT_ref[...]))
    ii = lax.broadcasted_iota(jnp.int32, (Q, Q), 0)
    jj = lax.broadcasted_iota(jnp.int32, (Q, Q), 1)
    tri = (jj <= ii).astype(BF16)
    triT = (ii <= jj).astype(BF16)
    cs = _dot2r(tri, a)
    csT = _dot2l(aT, triT)
    tot = cs[Q - 1:Q, :]
    p = jnp.where(isb, a - cs, cs)
    pT = jnp.where(isb, aT - csT, csT)
    dec_out = jnp.exp(jnp.where(isb, tot, 0.0) + p)
    dec_state = jnp.exp(jnp.where(isb, 0.0, tot) - p)
    chunk_dec = jnp.exp(tot)
    mask = sgn * (ii - jj) >= 0
    bm = bm_ref[...].astype(BF16)
    cm = cm_ref[...].astype(BF16)
    cb = _dot_nt(cm, bm)
    xs = xs_ref[...]
    H = range(HG)
    g = [(cb * jnp.exp(jnp.where(mask, p[:, h:h + 1] - pT[h:h + 1, :], -1e30))).astype(BF16) for h in H]
    xh = [xs[:, h * SSD_P:(h + 1) * SSD_P] * dt[:, h:h + 1] for h in H]
    s_old = [st_ref[h] for h in H]
    y_in = [_dot(g[h], xh[h].astype(BF16)) for h in H]
    y_st = [_dot(cm, s_old[h].astype(BF16)) for h in H]
    upd = [lax.dot_general(bm, (xh[h] * dec_state[:, h:h + 1]).astype(BF16), (((0,), (0,)), ((), ())),
                           preferred_element_type=F32) for h in H]
    for h in H:
        st_ref[h] = chunk_dec[:, h:h + 1] * s_old[h] + upd[h]
    y_ref[...] = jnp.concatenate([y_in[h] + dec_out[:, h:h + 1] * y_st[h] for h in H], axis=1)


def _ssd_rowblock(d, b, c):
    n_ctx = CTX // SSD_Q
    n_lat = SEQ // SSD_Q
    cc = jnp.where(d == 0, c, n_ctx - 1 - c)
    lc = jnp.where(d == 0, c - n_ctx, n_ctx + n_lat - 1 - c)
    return jnp.where(c < n_ctx, N_LAT // SSD_Q + b * n_ctx + cc, b * n_lat + lc)


def ssd_scan(xbc, dt_raw, dt_bias, a_log):
    HG = SSD_HEADS // SSD_G
    GW = HG * SSD_P
    dsel = dt_raw[:, :2 * SSD_HEADS].reshape(N_ROWS, 2, SSD_G, HG).transpose(1, 2, 0, 3)
    dselT = dsel.transpose(0, 1, 3, 2)
    bias = dt_bias.reshape(2, SSD_G, 1, HG)
    biasT = dt_bias.reshape(2, SSD_G, HG, 1)
    alog = a_log.reshape(2, SSD_G, 1, HG)
    alogT = a_log.reshape(2, SSD_G, HG, 1)
    nch = (CTX + SEQ) // SSD_Q
    rb = lambda d, b, g, c: _ssd_rowblock(d, b, c)
    bcol = SSD_HEADS * SSD_P // SSD_S
    return pl.pallas_call(
        _ssd_kernel,
        out_shape=jax.ShapeDtypeStruct((2, N_ROWS, SSD_HEADS * SSD_P), F32),
        grid=(2, BATCH, SSD_G, nch),
        in_specs=[pl.BlockSpec((SSD_Q, GW), lambda d, b, g, c: (rb(d, b, g, c), g)),
                  pl.BlockSpec((SSD_Q, SSD_S), lambda d, b, g, c: (rb(d, b, g, c), bcol + g)),
                  pl.BlockSpec((SSD_Q, SSD_S), lambda d, b, g, c: (rb(d, b, g, c), bcol + SSD_G + g)),
                  pl.BlockSpec((None, None, SSD_Q, HG), lambda d, b, g, c: (d, g, rb(d, b, g, c), 0)),
                  pl.BlockSpec((None, None, HG, SSD_Q), lambda d, b, g, c: (d, g, 0, rb(d, b, g, c))),
                  pl.BlockSpec((None, None, 1, HG), lambda d, b, g, c: (d, g, 0, 0)),
                  pl.BlockSpec((None, None, HG, 1), lambda d, b, g, c: (d, g, 0, 0)),
                  pl.BlockSpec((None, None, 1, HG), lambda d, b, g, c: (d, g, 0, 0)),
                  pl.BlockSpec((None, None, HG, 1), lambda d, b, g, c: (d, g, 0, 0))],
        out_specs=pl.BlockSpec((None, SSD_Q, GW), lambda d, b, g, c: (d, rb(d, b, g, c), g)),
        scratch_shapes=[pltpu.VMEM((HG, SSD_S, SSD_P), F32)],
        compiler_params=_cp(("parallel", "parallel", "parallel", "arbitrary")), name="ssd_scan",
    )(xbc, xbc, xbc, dsel, dselT, bias, biasT, alog, alogT)


def _ssd_out_kernel(yf_ref, yb_ref, xs_ref, z_ref, dskip_ref, nw_ref, o_ref):
    y = yf_ref[...] + yb_ref[...] + xs_ref[...] * dskip_ref[...]
    y = y * _silu(z_ref[...])
    gs = SSD_HEADS * SSD_P // SSD_G
    parts = []
    for g in range(SSD_G):
        yg = y[:, g * gs:(g + 1) * gs]
        parts.append(yg * lax.rsqrt(jnp.mean(yg * yg, axis=-1, keepdims=True) + EPS))
    o_ref[...] = jnp.concatenate(parts, axis=1) * nw_ref[...]


def ssd_output(y2, xbc, p, d_skip, norm_w, tm=256):
    W = SSD_HEADS * SSD_P
    dexp = jnp.repeat(d_skip, SSD_P).reshape(1, W)
    return pl.pallas_call(
        _ssd_out_kernel, out_shape=jax.ShapeDtypeStruct((N_ROWS, W), F32), grid=(N_ROWS // tm,),
        in_specs=[pl.BlockSpec((None, tm, W), lambda i: (0, i, 0)),
                  pl.BlockSpec((None, tm, W), lambda i: (1, i, 0)),
                  pl.BlockSpec((tm, W), lambda i: (i, 0)),
                  pl.BlockSpec((tm, W), lambda i: (i, 0)),
                  pl.BlockSpec((1, W), lambda i: (0, 0)),
                  pl.BlockSpec((1, W), lambda i: (0, 0))],
        out_specs=pl.BlockSpec((tm, W), lambda i: (i, 0)),
        compiler_params=_cp(("parallel",)), name="ssd_output")(y2, y2, xbc, p, dexp, norm_w.reshape(1, W))


def _hyfilt_kernel(f_ref, w0_ref, b0_ref, fr0_ref, w1_ref, b1_ref, fr1_ref, w2_ref, dl_ref, h_ref, ss_ref, *,
                   n_tiles):
    f = f_ref[...]
    h = jnp.sin(fr0_ref[...] * (_dot3(f, w0_ref[...]) + b0_ref[...]))
    h = jnp.sin(fr1_ref[...] * (_dot3(h, w1_ref[...]) + b1_ref[...]))
    h = _dot3(h, w2_ref[...])
    h = h * jnp.exp(-f[:, 0:1] * dl_ref[...])
    side = pl.program_id(0) // n_tiles
    j = pl.program_id(0) % n_tiles

    @pl.when(j == 0)
    def _():
        ss_ref[...] = jnp.zeros_like(ss_ref)

    ss_ref[...] += jnp.sum(h * h, axis=0, keepdims=True)
    row = lax.broadcasted_iota(jnp.int32, (h.shape[0], 1), 0) + j * h.shape[0]
    h_ref[...] = jnp.where((side == 1) & (row == 0), 0.0, h)


def hyena_filter_taps(L, hp):
    pos = jnp.arange(L, dtype=F32)
    t = pos / (L - 1)
    bands = (HY_EMB - 1) // 2
    freqs = jnp.linspace(1e-4, bands - 1, bands, dtype=F32)
    ang = (2.0 * math.pi / L) * pos[:, None] * freqs[None, :]
    feats = jnp.concatenate([t[:, None], jnp.cos(ang), -jnp.sin(ang)], axis=-1)
    feats = jnp.pad(feats, ((0, 0), (0, 128 - HY_EMB)))
    feats = jnp.concatenate([feats, jnp.flip(feats, axis=0)], axis=0)
    w0 = jnp.pad(hp['hy_mlp_w0'], ((0, 128 - HY_EMB), (0, 0)))
    min_decay = math.log(1e-2) / 1.5
    max_decay = math.log(1e-2) / 0.3
    deltas = jnp.abs(jnp.linspace(min_decay, max_decay, HY_W, dtype=F32))
    dl = jnp.tile(deltas, 2).reshape(1, 2 * HY_W)
    w2 = hp['hy_mlp_w2'].reshape(HY_HID, 2, 2, HY_W).transpose(0, 2, 1, 3).reshape(HY_HID, 4 * HY_W)
    tl = min(L, 512)
    n_tiles = L // tl
    NS = 2 * HY_W
    full = lambda shape: pl.BlockSpec(shape, lambda i: (0, 0))
    return pl.pallas_call(
        functools.partial(_hyfilt_kernel, n_tiles=n_tiles),
        out_shape=(jax.ShapeDtypeStruct((2 * L, NS), F32), jax.ShapeDtypeStruct((1, 2 * NS), F32)),
        grid=(2 * n_tiles,),
        in_specs=[pl.BlockSpec((tl, 128), lambda i: (i, 0)), full((128, HY_HID)), full((1, HY_HID)),
                  full((1, HY_HID)), full((HY_HID, HY_HID)), full((1, HY_HID)), full((1, HY_HID)),
                  pl.BlockSpec((HY_HID, NS), lambda i: (0, i // n_tiles)), full((1, NS))],
        out_specs=(pl.BlockSpec((tl, NS), lambda i: (i, 0)), pl.BlockSpec((1, NS), lambda i: (0, i // n_tiles))),
        compiler_params=_cp(("arbitrary",)), name="hyena_filter",
    )(feats, w0, hp['hy_mlp_b0'].reshape(1, -1), hp['hy_freq0'].reshape(1, -1), hp['hy_mlp_w1'],
      hp['hy_mlp_b1'].reshape(1, -1), hp['hy_freq1'].reshape(1, -1), w2, dl)


def _cis(num, den):
    ang = (2.0 * math.pi / den) * (num % den).astype(F32)
    return jnp.cos(ang), -jnp.sin(ang)


def _fft_consts(NB, BS):
    N = NB * BS
    h = NB // 2
    k1 = jnp.arange(h, dtype=jnp.int32)
    j = jnp.arange(NB, dtype=jnp.int32)
    re, im = _cis(j[None, :] * (2 * k1[:, None] + 1), 2 * NB)
    f1 = jnp.concatenate([re, im], axis=0)
    neg = jnp.where(j >= h, -1.0, 1.0)[None, :]
    f1_data = f1[:, :h]
    f1_filt = f1 * neg
    f1_inv = (2.0 / N) * jnp.concatenate([re[:, :h].T, im[:, :h].T], axis=1)
    r = jnp.arange(BS, dtype=jnp.int32)
    k2 = jnp.arange(BS, dtype=jnp.int32)
    kk = 2 * k1[:, None, None] + 2 * NB * k2[None, :, None] + 1
    gre, gim = _cis(kk * r[None, None, :], 2 * N)
    gf = jnp.concatenate([jnp.concatenate([gre, -gim], axis=2), jnp.concatenate([gim, gre], axis=2)], axis=1)
    gret, gimt = gre.transpose(0, 2, 1), gim.transpose(0, 2, 1)
    gi = jnp.concatenate([jnp.concatenate([gret, gimt], axis=2), jnp.concatenate([-gimt, gret], axis=2)], axis=1)
    return (f1_data.astype(BF16), f1_filt.astype(BF16), f1_inv.astype(BF16), gf.astype(BF16), gi.astype(BF16))


FFT_PAD = 8


def _fft_fwd_kernel(u_ref, f1_ref, g_ref, o_ref, t_sc, *, NB, BS, nj, kg):
    pitch = NB + FFT_PAD
    @pl.when(pl.program_id(2) == 0)
    def _():
        f1 = f1_ref[...]

        def body(r, carry):
            xr = u_ref[pl.ds(r, nj, stride=BS), :].astype(BF16)
            t_sc[pl.ds(pl.multiple_of(r * pitch, 8), NB), :] = _dot(f1, xr)
            return carry

        lax.fori_loop(0, BS, body, 0)

    k0 = pl.program_id(2) * kg
    for i in range(kg):
        are = t_sc[pl.ds(k0 + i, BS, stride=pitch), :]
        aim = t_sc[pl.ds(k0 + i + NB // 2, BS, stride=pitch), :]
        a = jnp.concatenate([are, aim], axis=0).astype(BF16)
        o_ref[i] = _dot(g_ref[i], a)


def fft_fwd(u, col0, nbatch, nj, f1, gf, NB, BS, ct=128, kg=8):
    h = NB // 2
    kg = min(kg, h)
    return pl.pallas_call(
        functools.partial(_fft_fwd_kernel, NB=NB, BS=BS, nj=nj, kg=kg),
        out_shape=jax.ShapeDtypeStruct((nbatch, h, 2 * BS, HY_W), F32),
        grid=(nbatch, HY_W // ct, h // kg),
        in_specs=[pl.BlockSpec((nj * BS, ct), lambda b, c, k: (b, col0 // ct + c)),
                  pl.BlockSpec((NB, nj), lambda b, c, k: (0, 0)),
                  pl.BlockSpec((kg, 2 * BS, 2 * BS), lambda b, c, k: (k, 0, 0))],
        out_specs=pl.BlockSpec((None, kg, 2 * BS, ct), lambda b, c, k: (b, k, 0, c)),
        scratch_shapes=[pltpu.VMEM((BS * (NB + FFT_PAD), ct), F32)],
        compiler_params=_cp(("parallel", "parallel", "arbitrary")), name="hyena_fft_fwd")(u, f1, gf)


def _cmul(u, h, half):
    ure, uim = u[:half], u[half:]
    hre, him = h[:half], h[half:]
    return jnp.concatenate([ure * hre - uim * him, ure * him + uim * hre], axis=0)


def _fft_inv_kernel(us_ref, hs_ref, gi_ref, f1i_ref, o_ref, t_sc, *, NB, BS, kg):
    ks = pl.program_id(2)
    pitch = 2 * BS + FFT_PAD
    for i in range(kg):
        y = _cmul(us_ref[i], hs_ref[i], BS).astype(BF16)
        row = pl.multiple_of((ks * kg + i) * pitch, 8)
        t_sc[pl.ds(row, 2 * BS), :] = _dot(gi_ref[i], y)

    @pl.when(ks == pl.num_programs(2) - 1)
    def _():
        f1i = f1i_ref[...]

        def body(r, carry):
            bre = t_sc[pl.ds(r, NB // 2, stride=pitch), :]
            bim = t_sc[pl.ds(r + BS, NB // 2, stride=pitch), :]
            b = jnp.concatenate([bre, bim], axis=0).astype(BF16)
            o_ref[pl.ds(r, NB // 2, stride=BS), :] = _dot(f1i, b)
            return carry

        lax.fori_loop(0, BS, body, 0)


def fft_inv(us, hs, gi, f1i, NB, BS, ct=128, kg=8):
    nbatch, h = us.shape[0], NB // 2
    kg = min(kg, h)
    L = h * BS
    return pl.pallas_call(
        functools.partial(_fft_inv_kernel, NB=NB, BS=BS, kg=kg),
        out_shape=jax.ShapeDtypeStruct((nbatch * L, HY_W), F32),
        grid=(nbatch, HY_W // ct, h // kg),
        in_specs=[pl.BlockSpec((None, kg, 2 * BS, ct), lambda b, c, k: (b, k, 0, c)),
                  pl.BlockSpec((None, kg, 2 * BS, ct), lambda b, c, k: (0, k, 0, c)),
                  pl.BlockSpec((kg, 2 * BS, 2 * BS), lambda b, c, k: (k, 0, 0)),
                  pl.BlockSpec((h, NB), lambda b, c, k: (0, 0))],
        out_specs=pl.BlockSpec((L, ct), lambda b, c, k: (b, c)),
        scratch_shapes=[pltpu.VMEM((h * (2 * BS + FFT_PAD), ct), F32)],
        compiler_params=_cp(("parallel", "parallel", "arbitrary")), name="hyena_fft_inv")(us, hs, gi, f1i)


def _dft_consts(L):
    N = 2 * L
    k = jnp.arange(L, dtype=jnp.int32)
    n = jnp.arange(N, dtype=jnp.int32)
    re, im = _cis(n[None, :] * (2 * k[:, None] + 1), 2 * N)
    f = jnp.concatenate([re, im], axis=0)
    neg = jnp.where(n >= L, -1.0, 1.0)[None, :]
    fi = (2.0 / N) * jnp.concatenate([re[:, :L].T, im[:, :L].T], axis=1)
    return f[:, :L].astype(BF16), (f * neg).astype(BF16), fi.astype(BF16)


def _cdft_kernel(f_ref, x_ref, o_ref):
    o_ref[...] = _dot(f_ref[...], x_ref[...].astype(BF16))


def dft_fwd(x, f, row0, col0, nbatch, ct=256):
    M, K = f.shape
    return pl.pallas_call(
        _cdft_kernel, out_shape=jax.ShapeDtypeStruct((nbatch, M, HY_W), F32),
        grid=(nbatch, HY_W // ct),
        in_specs=[pl.BlockSpec((M, K), lambda b, c: (0, 0)),
                  pl.BlockSpec((K, ct), lambda b, c: (row0 // K + b, col0 // ct + c))],
        out_specs=pl.BlockSpec((None, M, ct), lambda b, c: (b, 0, c)),
        compiler_params=_cp(("parallel", "parallel")), name="hyena_dft_fwd")(f, x)


def _cdft_inv_kernel(us_ref, hs_ref, fi_ref, o_ref):
    half = us_ref.shape[0] // 2
    o_ref[...] = _dot(fi_ref[...], _cmul(us_ref[...], hs_ref[...], half).astype(BF16))


def dft_inv(us, hs, fi, ct=256):
    nbatch, M2, _ = us.shape
    L = fi.shape[0]
    return pl.pallas_call(
        _cdft_inv_kernel, out_shape=jax.ShapeDtypeStruct((nbatch * L, HY_W), F32),
        grid=(nbatch, HY_W // ct),
        in_specs=[pl.BlockSpec((None, M2, ct), lambda b, c: (b, 0, c)),
                  pl.BlockSpec((None, M2, ct), lambda b, c: (0, 0, c)),
                  pl.BlockSpec((L, M2), lambda b, c: (0, 0))],
        out_specs=pl.BlockSpec((L, ct), lambda b, c: (b, c)),
        compiler_params=_cp(("parallel", "parallel")), name="hyena_dft_inv")(us, hs, fi)


def _hy_gate_kernel(g_ref, y_ref, u_ref, ss_ref, b_ref, o_ref):
    scale = lax.rsqrt(ss_ref[0:1, :] + ss_ref[1:2, :] + 1e-6)
    o_ref[...] = g_ref[...] * (y_ref[...] * scale + u_ref[...] * b_ref[...])


def hy_gate(gate, gcol, grow, y, uin, ucol, urow, ss, order, bias, tm=256):
    M = y.shape[0]
    return pl.pallas_call(
        _hy_gate_kernel, out_shape=jax.ShapeDtypeStruct((M, HY_W), F32), grid=(M // tm,),
        in_specs=[pl.BlockSpec((tm, HY_W), lambda i: (grow // tm + i, gcol // HY_W)),
                  pl.BlockSpec((tm, HY_W), lambda i: (i, 0)),
                  pl.BlockSpec((tm, HY_W), lambda i: (urow // tm + i, ucol // HY_W)),
                  pl.BlockSpec((None, 2, HY_W), lambda i: (order, 0, 0)),
                  pl.BlockSpec((None, 1, HY_W), lambda i: (order, 0, 0))],
        out_specs=pl.BlockSpec((tm, HY_W), lambda i: (i, 0)),
        compiler_params=_cp(("parallel",)), name="hyena_gate")(gate, y, uin, ss, bias)


def hyena(u, hp):
    C = HY_W
    bias = hp['hy_bias'].reshape(2, 1, C)
    NB = BS = int(round(math.sqrt(2 * SEQ)))
    f1d, f1f, f1i, gf, gi = _fft_consts(NB, BS)
    taps, ss = hyena_filter_taps(SEQ, hp)
    ss = ss.reshape(2, 2, C).transpose(1, 0, 2)
    zin, zcol = u, 2 * C
    for order in range(2):
        hs = fft_fwd(taps, order * C, 1, NB, f1f, gf, NB, BS)
        us = fft_fwd(zin, zcol, BATCH, NB // 2, f1d, gf, NB, BS)
        y = fft_inv(us, hs, gi, f1i, NB, BS)
        zin = hy_gate(u, order * C, 0, y, zin, zcol, 0, ss, order, bias)
        zcol = 0
    z_lat = zin
    fd, ff, fi = _dft_consts(CTX)
    taps, ss = hyena_filter_taps(CTX, hp)
    ss = ss.reshape(2, 2, C).transpose(1, 0, 2)
    zin, zcol, zrow = u, 2 * C, N_LAT
    for order in range(2):
        hs = dft_fwd(taps, ff, 0, order * C, 1)
        us = dft_fwd(zin, fd, zrow, zcol, BATCH)
        y = dft_inv(us, hs, fi)
        zin = hy_gate(u, order * C, N_LAT, y, zin, zcol, zrow, ss, order, bias)
        zcol, zrow = 0, 0
    return jnp.concatenate([z_lat, zin], axis=0)


EV_SSD_IN = SSD_HEADS * SSD_P
EV_XBC = EV_SSD_IN + 2 * SSD_G * SSD_S
EV_PAD_N = 5760


def even_mixer(x, mods, g_pre, g_post, ep):
    o1 = EV_SSD_IN
    o2 = o1 + EV_XBC
    o3 = o2 + 2 * SSD_HEADS
    w = ep['w_in']
    n_in = w.shape[1]
    w_perm = jnp.concatenate([w[:, :o2], w[:, o3:], w[:, o2:o3],
                              jnp.zeros((D, EV_PAD_N - n_in), F32)], axis=1).astype(BF16)
    p = norm_mod_matmul(x, g_pre, mods, w_perm, 0, 1, tn=640, name="even_in_proj")
    xbc = dwconv3(p, o1, EV_XBC, ep['ssd_conv_w'], ep['ssd_conv_b'], True, "ssd_conv")
    u = dwconv3(p, o2, 3 * HY_W, ep['hy_conv_w'], ep['hy_conv_b'], False, "hyena_conv")
    dt_raw = p[:, o2 + 3 * HY_W:o2 + 3 * HY_W + 2 * SSD_HEADS]
    y2 = ssd_scan(xbc, dt_raw, ep['ssd_dt_bias'], ep['ssd_a_log'])
    s = ssd_output(y2, xbc, p, ep['ssd_d'], ep['ssd_norm_w'])
    zh = hyena(u, ep)
    return outproj_residual(s, zh, ep['w_out'].astype(BF16), x, g_post, mods, 2, name="even_out_proj")


def _head_sum(x, e, et):
    return _dot2l(_dot2l(x, e), et)


def _rw_prep_kernel(r_ref, k_ref, v_ref, lo_ref, w0_ref, wup_ref, a0_ref, aup_ref, gup_ref, kk_ref, ka_ref,
                    rk_ref, e_ref, et_ref, lw_ref, kd_ref, be_ref, kap_ref, g_ref, bonus_ref):
    r, k, v = r_ref[...], k_ref[...], v_ref[...]
    lo = lo_ref[...]
    wc, ac, gc = lo[:, 0:64], lo[:, 64:128], lo[:, 128:384]
    e, et = e_ref[...], et_ref[...]
    kk = k * kk_ref[...]
    kap = kk * lax.rsqrt(_head_sum(kk * kk, e, et) + 1e-12)
    kap_ref[...] = kap
    g_ref[...] = _dot3(_sigmoid(gc), gup_ref[...])
    kd_sum = jnp.zeros_like(k)
    for d in range(2):
        wlog = -_softplus(-(w0_ref[d:d + 1, :] + _dot3(jnp.tanh(wc), wup_ref[d]))) - 0.5
        lw_ref[d] = -jnp.exp(wlog)
        a = _sigmoid(a0_ref[d:d + 1, :] + _dot3(ac, aup_ref[d]))
        kd = k * (1.0 + (a - 1.0) * ka_ref[...])
        kd_ref[d] = kd
        be_ref[d] = kap * a
        kd_sum = kd_sum + kd
    bonus_ref[...] = _head_sum(r * kd_sum * rk_ref[...], e, et) * v


def rwkv_prepare(code, lora, op, tm=256):
    W = RW_H * RW_N
    heads = jnp.arange(W, dtype=jnp.int32) // RW_N
    e = (heads[:, None] == jnp.arange(128, dtype=jnp.int32)[None, :]).astype(BF16)
    et = jnp.transpose(e)
    gup = jnp.pad(op['g_up'], ((0, 256 - op['g_up'].shape[0]), (0, 0)))
    row = lambda a: a.reshape(1, W)
    full2 = lambda shape: pl.BlockSpec(shape, lambda i: (0,) * len(shape))
    outs = pl.pallas_call(
        _rw_prep_kernel,
        out_shape=(jax.ShapeDtypeStruct((2, N_ROWS, W), F32), jax.ShapeDtypeStruct((2, N_ROWS, W), F32),
                   jax.ShapeDtypeStruct((2, N_ROWS, W), F32), jax.ShapeDtypeStruct((N_ROWS, W), F32),
                   jax.ShapeDtypeStruct((N_ROWS, W), F32), jax.ShapeDtypeStruct((N_ROWS, W), F32)),
        grid=(N_ROWS // tm,),
        in_specs=[pl.BlockSpec((tm, W), lambda i: (i, 0)), pl.BlockSpec((tm, W), lambda i: (i, 1)),
                  pl.BlockSpec((tm, W), lambda i: (i, 2)), pl.BlockSpec((tm, 384), lambda i: (i, 0)),
                  full2((2, W)), full2((2, 64, W)), full2((2, W)), full2((2, 64, W)), full2((256, W)),
                  full2((1, W)), full2((1, W)), full2((1, W)), full2((W, 128)), full2((128, W))],
        out_specs=(pl.BlockSpec((2, tm, W), lambda i: (0, i, 0)), pl.BlockSpec((2, tm, W), lambda i: (0, i, 0)),
                   pl.BlockSpec((2, tm, W), lambda i: (0, i, 0)), pl.BlockSpec((tm, W), lambda i: (i, 0)),
                   pl.BlockSpec((tm, W), lambda i: (i, 0)), pl.BlockSpec((tm, W), lambda i: (i, 0))),
        compiler_params=_cp(("parallel",)), name="rwkv_prepare",
    )(code, code, code, lora, op['w0'], op['w_up'], op['a0'], op['a_up'], gup, row(op['k_k']), row(op['k_a']),
      row(op['r_k']), e, et)
    return outs


def _tri_inv(n, eye, masks):
    bd = lambda a, b: _dot(a.astype(BF16), b.astype(BF16))
    d0 = jnp.where(masks[0], n, 0.0)
    d2 = bd(d0, d0)
    d4 = bd(d2, d2)
    t = bd(bd(eye + d0, eye + d2), eye + d4)
    for m in masks[1:]:
        e = jnp.where(m, n, 0.0)
        t = t + bd(t, bd(e, t))
    return t


def _rw_scan_kernel(r_ref, v_ref, lw_ref, kd_ref, be_ref, kap_ref, y_ref, st_ref):
    d = pl.program_id(0)
    c = pl.program_id(2)
    C = RW_CHUNK
    N = RW_N

    @pl.when(c == 0)
    def _():
        st_ref[...] = jnp.zeros_like(st_ref)

    isb = d == 1
    sgn = 1 - 2 * d
    ii = lax.broadcasted_iota(jnp.int32, (C, C), 0)
    jj = lax.broadcasted_iota(jnp.int32, (C, C), 1)
    dif = sgn * (ii - jj)
    incl = dif >= 0
    strict = dif > 0
    tri = incl.astype(BF16)
    eye = (ii == jj).astype(F32)
    blk = [(ii >> s) == (jj >> s) for s in (3, 4, 5)]
    masks = [blk[0], blk[1] & ~blk[0], blk[2] & ~blk[1], ~blk[2]]
    lw = lw_ref[...]
    cum = _dot2r(tri, lw)
    ec = jnp.exp(cum)
    en = jnp.exp(-cum)
    ea = jnp.exp(cum - lw)
    last = jnp.where(isb, cum[0:1, :], cum[C - 1:C, :])
    el = jnp.exp(last - cum)
    kap = kap_ref[...]
    r = r_ref[...]
    v = v_ref[...]
    a_t = -kap * ea
    r_t = r * ec
    b_t = be_ref[...] * en
    k_t = kd_ref[...] * en
    b_l = be_ref[...] * el
    k_l = kd_ref[...] * el
    pc = jnp.exp(last)
    H = range(RW_H)
    sl = [slice(h * N, (h + 1) * N) for h in H]
    bd = lambda a, b: _dot(a.astype(BF16), b.astype(BF16))
    tn = lambda a, b: lax.dot_general(a, b, (((0,), (0,)), ((), ())), preferred_element_type=F32)
    sc = [_dot_nt(jnp.concatenate([a_t[:, sl[h]], r_t[:, sl[h]]], axis=0).astype(BF16),
                  jnp.concatenate([b_t[:, sl[h]], k_t[:, sl[h]]], axis=0).astype(BF16)) for h in H]
    n_ab = [jnp.where(strict, sc[h][0:C, 0:C], 0.0) for h in H]
    a_ak = [jnp.where(strict, sc[h][0:C, C:2 * C], 0.0).astype(BF16) for h in H]
    m_rb = [jnp.where(incl, sc[h][C:2 * C, 0:C], 0.0).astype(BF16) for h in H]
    m_rk = [jnp.where(incl, sc[h][C:2 * C, C:2 * C], 0.0).astype(BF16) for h in H]
    vh = [v[:, sl[h]].astype(BF16) for h in H]
    d0 = [jnp.where(masks[0], n_ab[h], 0.0) for h in H]
    d2 = [bd(d0[h], d0[h]) for h in H]
    d4 = [bd(d2[h], d2[h]) for h in H]
    t = [bd(eye + d0[h], eye + d2[h]) for h in H]
    t = [bd(t[h], eye + d4[h]) for h in H]
    for m in masks[1:]:
        et = [bd(jnp.where(m, n_ab[h], 0.0), t[h]) for h in H]
        t = [t[h] + bd(t[h], et[h]) for h in H]
    av = [_dot(a_ak[h], vh[h]) for h in H]
    wub = [bd(t[h], jnp.concatenate([a_t[:, sl[h]], av[h]], axis=1)).astype(BF16) for h in H]
    mv = [_dot(m_rk[h], vh[h]) for h in H]
    kv = [tn(k_l[:, sl[h]].astype(BF16), vh[h]) for h in H]
    qy = [_dot(m_rb[h], wub[h]) + jnp.concatenate([r_t[:, sl[h]], mv[h]], axis=1) for h in H]
    pp = [tn(b_l[:, sl[h]].astype(BF16), wub[h]) + jnp.concatenate([eye * pc[:, sl[h]], kv[h]], axis=1)
          for h in H]
    h_old = [st_ref[h] for h in H]
    ys = [_dot3(qy[h][:, 0:N], h_old[h]) + qy[h][:, N:2 * N] for h in H]
    for h in H:
        st_ref[h] = _dot3(pp[h][:, 0:N], h_old[h]) + pp[h][:, N:2 * N]
    y_ref[...] = jnp.concatenate(ys, axis=1)


def _rw_rowblock(d, b, c):
    n_ctx = CTX // RW_CHUNK
    n_lat = SEQ // RW_CHUNK
    cc = jnp.where(d == 0, c, n_ctx - 1 - c)
    lc = jnp.where(d == 0, c - n_ctx, n_ctx + n_lat - 1 - c)
    return jnp.where(c < n_ctx, N_LAT // RW_CHUNK + b * n_ctx + cc, b * n_lat + lc)


def rwkv_scan(code, lw, kd, be, kap):
    W = RW_H * RW_N
    nch = (CTX + SEQ) // RW_CHUNK
    rb = lambda d, b, c: _rw_rowblock(d, b, c)
    return pl.pallas_call(
        _rw_scan_kernel,
        out_shape=jax.ShapeDtypeStruct((2, N_ROWS, W), F32),
        grid=(2, BATCH, nch),
        in_specs=[pl.BlockSpec((RW_CHUNK, W), lambda d, b, c: (rb(d, b, c), 0)),
                  pl.BlockSpec((RW_CHUNK, W), lambda d, b, c: (rb(d, b, c), 2)),
                  pl.BlockSpec((None, RW_CHUNK, W), lambda d, b, c: (d, rb(d, b, c), 0)),
                  pl.BlockSpec((None, RW_CHUNK, W), lambda d, b, c: (d, rb(d, b, c), 0)),
                  pl.BlockSpec((None, RW_CHUNK, W), lambda d, b, c: (d, rb(d, b, c), 0)),
                  pl.BlockSpec((RW_CHUNK, W), lambda d, b, c: (rb(d, b, c), 0))],
        out_specs=pl.BlockSpec((None, RW_CHUNK, W), lambda d, b, c: (d, rb(d, b, c), 0)),
        scratch_shapes=[pltpu.VMEM((RW_H, RW_N, RW_N), F32)],
        compiler_params=_cp(("parallel", "parallel", "arbitrary")), name="rwkv_scan",
    )(code, code, lw, kd, be, kap)


def _rw_out_kernel(yf_ref, yb_ref, bonus_ref, g_ref, lnw_ref, lnb_ref, e_ref, et_ref, o_ref):
    e, et = e_ref[...], et_ref[...]
    y = yf_ref[...] + yb_ref[...]
    mean = _head_sum(y, e, et) * (1.0 / RW_N)
    yc = y - mean
    var = _head_sum(yc * yc, e, et) * (1.0 / RW_N)
    yn = yc * lax.rsqrt(var + RW_GN_EPS) * lnw_ref[...] + lnb_ref[...]
    o_ref[...] = (yn + bonus_ref[...]) * g_ref[...]


def rwkv_output(y2, bonus, g, op, tm=256):
    W = RW_H * RW_N
    heads = jnp.arange(W, dtype=jnp.int32) // RW_N
    e = (heads[:, None] == jnp.arange(128, dtype=jnp.int32)[None, :]).astype(BF16)
    et = jnp.transpose(e)
    M = N_LAT
    return pl.pallas_call(
        _rw_out_kernel, out_shape=jax.ShapeDtypeStruct((M, W), F32), grid=(M // tm,),
        in_specs=[pl.BlockSpec((None, tm, W), lambda i: (0, i, 0)), pl.BlockSpec((None, tm, W), lambda i: (1, i, 0)),
                  pl.BlockSpec((tm, W), lambda i: (i, 0)), pl.BlockSpec((tm, W), lambda i: (i, 0)),
                  pl.BlockSpec((1, W), lambda i: (0, 0)), pl.BlockSpec((1, W), lambda i: (0, 0)),
                  pl.BlockSpec((W, 128), lambda i: (0, 0)), pl.BlockSpec((128, W), lambda i: (0, 0))],
        out_specs=pl.BlockSpec((tm, W), lambda i: (i, 0)),
        compiler_params=_cp(("parallel",)), name="rwkv_output",
    )(y2, y2, bonus, g, op['ln_w'].reshape(1, W), op['ln_b'].reshape(1, W), e, et)


AT_Q = RW_H * AT_HD
AT_KW = AT_KV * AT_HD
AT_TQ = 512
AT_TK = 768


def _rope_tables(tm):
    half = AT_HD // 2
    inv = 10000.0 ** (-jnp.arange(0, half, 2, dtype=F32) / half)
    pos = jnp.arange(SEQ, dtype=jnp.int32)
    row = (pos // GRID_W).astype(F32)[:, None] * inv
    col = (pos % GRID_W).astype(F32)[:, None] * inv
    cos_h = jnp.concatenate([jnp.cos(row), jnp.cos(row), jnp.cos(col), jnp.cos(col)], axis=1)
    sin_h = jnp.concatenate([-jnp.sin(row), jnp.sin(row), -jnp.sin(col), jnp.sin(col)], axis=1)
    cos_t = jnp.concatenate([jnp.tile(cos_h, (1, 2)), jnp.ones((tm, 128), F32)], axis=0)
    sin_t = jnp.concatenate([jnp.tile(sin_h, (1, 2)), jnp.zeros((tm, 128), F32)], axis=0)
    return cos_t, sin_t


def _rot_partner(x):
    q = AT_HD // 4
    w = x.shape[1]
    lane = lax.broadcasted_iota(jnp.int32, x.shape, 1)
    return jnp.where((lane % (2 * q)) < q, pltpu.roll(x, w - q, 1), pltpu.roll(x, q, 1))


def _at_prep_kernel(q_ref, k_ref, v_ref, cos_ref, sin_ref, qn_ref, kn_ref, e_ref, et_ref, qo_ref, ko_ref, vo_ref):
    e, et = e_ref[...], et_ref[...]
    cos2, sin2 = cos_ref[...], sin_ref[...]

    def norm_rope(x, gain, nrep):
        ms = _head_sum(x * x, e[:x.shape[1]], et[:, :x.shape[1]]) * (1.0 / AT_HD)
        xn = x * lax.rsqrt(ms + EPS) * gain
        cos = jnp.tile(cos2, (1, nrep))
        sin = jnp.tile(sin2, (1, nrep))
        return xn * cos + _rot_partner(xn) * sin

    qn = norm_rope(q_ref[...], qn_ref[...], AT_Q // 128) * (AT_HD ** -0.5 * math.log2(math.e))
    qo_ref[...] = jnp.transpose(qn).astype(BF16)
    ko_ref[...] = norm_rope(k_ref[...], kn_ref[...], AT_KW // 128).astype(BF16)
    vo_ref[...] = jnp.transpose(v_ref[...]).astype(BF16)


def attention_prepare(p, q_norm, k_norm, tm=256):
    cos_t, sin_t = _rope_tables(tm)
    heads = jnp.arange(AT_Q, dtype=jnp.int32) // AT_HD
    e = (heads[:, None] == jnp.arange(128, dtype=jnp.int32)[None, :]).astype(BF16)
    et = jnp.transpose(e)
    tab = lambda i: jnp.where(i * tm < N_LAT, ((i * tm) % SEQ) // tm, SEQ // tm)
    n_lat_t, n_seq_t, n_ctx_t = N_LAT // tm, SEQ // tm, CTX // tm
    kvb = lambda i: jnp.where(i < n_lat_t, (i // n_seq_t) * (n_seq_t + n_ctx_t) + n_ctx_t + i % n_seq_t,
                              ((i - n_lat_t) // n_ctx_t) * (n_seq_t + n_ctx_t) + (i - n_lat_t) % n_ctx_t)
    qcol = (3 * RW_H * RW_N) // AT_Q
    kcol = (3 * RW_H * RW_N + AT_Q) // AT_KW
    return pl.pallas_call(
        _at_prep_kernel,
        out_shape=(jax.ShapeDtypeStruct((AT_Q, N_ROWS), BF16), jax.ShapeDtypeStruct((N_ROWS, AT_KW), BF16),
                   jax.ShapeDtypeStruct((AT_KW, N_ROWS), BF16)),
        grid=(N_ROWS // tm,),
        in_specs=[pl.BlockSpec((tm, AT_Q), lambda i: (i, qcol)),
                  pl.BlockSpec((tm, AT_KW), lambda i: (i, kcol)),
                  pl.BlockSpec((tm, AT_KW), lambda i: (i, kcol + 1)),
                  pl.BlockSpec((tm, 128), lambda i: (tab(i), 0)),
                  pl.BlockSpec((tm, 128), lambda i: (tab(i), 0)),
                  pl.BlockSpec((1, AT_Q), lambda i: (0, 0)),
                  pl.BlockSpec((1, AT_KW), lambda i: (0, 0)),
                  pl.BlockSpec((AT_Q, 128), lambda i: (0, 0)),
                  pl.BlockSpec((128, AT_Q), lambda i: (0, 0))],
        out_specs=(pl.BlockSpec((AT_Q, tm), lambda i: (0, i)), pl.BlockSpec((tm, AT_KW), lambda i: (kvb(i), 0)),
                   pl.BlockSpec((AT_KW, tm), lambda i: (0, kvb(i)))),
        compiler_params=_cp(("parallel",)), name="attn_prepare",
    )(p, p, p, cos_t, sin_t, jnp.tile(q_norm, AT_Q // AT_HD).reshape(1, AT_Q),
      jnp.tile(k_norm, AT_KV).reshape(1, AT_KW), e, et)


def _flash_t_kernel(qt_ref, k_ref, vt_ref, o_ref, m_sc, l_sc, acc_sc):
    ki = pl.program_id(2)
    nq = AT_Q // AT_HD
    gq = nq // AT_KV

    @pl.when(ki == 0)
    def _():
        m_sc[...] = jnp.full_like(m_sc, -1e30)
        l_sc[...] = jnp.zeros_like(l_sc)
        acc_sc[...] = jnp.zeros_like(acc_sc)

    for g in range(AT_KV):
        kg = k_ref[:, g * AT_HD:(g + 1) * AT_HD]
        vtg1 = jnp.concatenate([vt_ref[g * AT_HD:(g + 1) * AT_HD, :],
                                jnp.ones((16, vt_ref.shape[1]), BF16)], axis=0)
        hs = range(g * gq, (g + 1) * gq)
        st = [_dot(kg, qt_ref[h * AT_HD:(h + 1) * AT_HD, :]) for h in hs]
        m_old = [m_sc[h] for h in hs]
        m_new = [jnp.maximum(m_old[i], jnp.max(st[i], axis=0, keepdims=True)) for i in range(gq)]
        alpha = [jnp.exp2(m_old[i] - m_new[i]) for i in range(gq)]
        pt = [jnp.exp2(st[i] - m_new[i]) for i in range(gq)]
        pv = [_dot(vtg1, pt[i].astype(BF16)) for i in range(gq)]
        for i, h in enumerate(hs):
            l_sc[h] = alpha[i] * l_sc[h] + pv[i][AT_HD:AT_HD + 1, :]
            m_sc[h] = m_new[i]
            rows = pl.ds(h * AT_HD, AT_HD)
            acc_sc[rows, :] = alpha[i] * acc_sc[rows, :] + pv[i][0:AT_HD, :]

    @pl.when(ki == pl.num_programs(2) - 1)
    def _():
        inv = jnp.concatenate([jnp.broadcast_to(1.0 / l_sc[h], (AT_HD, l_sc.shape[2])) for h in range(nq)], axis=0)
        o_ref[...] = jnp.transpose(acc_sc[...] * inv)


def flash_attention_t(qt, k, vt):
    nq = AT_Q // AT_HD
    nk = (CTX + SEQ) // AT_TK
    kv_rb = lambda b, ki: b * nk + ki
    return pl.pallas_call(
        _flash_t_kernel,
        out_shape=jax.ShapeDtypeStruct((N_LAT, AT_Q), F32),
        grid=(BATCH, SEQ // AT_TQ, nk),
        in_specs=[pl.BlockSpec((AT_Q, AT_TQ), lambda b, qi, ki: (0, b * (SEQ // AT_TQ) + qi)),
                  pl.BlockSpec((AT_TK, AT_KW), lambda b, qi, ki: (kv_rb(b, ki), 0)),
                  pl.BlockSpec((AT_KW, AT_TK), lambda b, qi, ki: (0, kv_rb(b, ki)))],
        out_specs=pl.BlockSpec((AT_TQ, AT_Q), lambda b, qi, ki: (b * (SEQ // AT_TQ) + qi, 0)),
        scratch_shapes=[pltpu.VMEM((nq, 1, AT_TQ), F32), pltpu.VMEM((nq, 1, AT_TQ), F32),
                        pltpu.VMEM((AT_Q, AT_TQ), F32)],
        compiler_params=_cp(("parallel", "parallel", "arbitrary")), name="flash_attention")(qt, k, vt)


OD_PAD_N = 4992


def odd_mixer(x, mods, g_pre, g_post, op):
    W = RW_H * RW_N
    w = op['w_in']
    c3 = 3 * W
    code_n = c3 + 64 + 64 + 160
    w_perm = jnp.concatenate([w[:, :c3], w[:, code_n:], w[:, c3:code_n],
                              jnp.zeros((D, OD_PAD_N - w.shape[1]), F32)], axis=1).astype(BF16)
    p = norm_mod_matmul(x, g_pre, mods, w_perm, 0, 1, tn=384, name="odd_in_proj")
    mu = op['mu']
    taps = lambda m: jnp.stack([0.5 * m, 1.0 - m, 0.5 * m], axis=1)
    code = dwconv3(p, 0, c3, taps(mu[:c3]), jnp.zeros((c3,), F32), False, "rwkv_shift")
    lo_col = c3 + AT_Q + 2 * AT_KW
    mu_lo = jnp.pad(mu[c3:], (0, 384 - (code_n - c3)))
    lora = dwconv3(p, lo_col, 384, taps(mu_lo), jnp.zeros((384,), F32), False, "rwkv_shift_lora", cb=128)
    lw, kd, be, kap, g, bonus = rwkv_prepare(code, lora, op)
    y2 = rwkv_scan(code, lw, kd, be, kap)
    o_l = rwkv_output(y2, bonus, g, op)
    q, k, v = attention_prepare(p, op['q_norm'], op['k_norm'])
    a_l = flash_attention_t(q, k, v)
    return outproj_residual(o_l, a_l, op['w_out'].astype(BF16), x[:N_LAT], g_post, mods, 2, name="odd_out_proj")


def _router_kernel(x_ref, g_ref, mod_ref, rw_ref, rb_ref, s1_ref, s3_ref, s2_ref, t_ref, idx_ref, wt_ref, sh_ref):
    x = x_ref[...]
    ms = jnp.mean(x * x, axis=-1, keepdims=True)
    t = x * lax.rsqrt(ms + EPS) * g_ref[...] * (1.0 + mod_ref[4:5, :]) + mod_ref[3:4, :]
    t_ref[...] = t
    tb = t.astype(BF16)
    sh_ref[...] = _dot((_silu(_dot(tb, s1_ref[...])) * _dot(tb, s3_ref[...])).astype(BF16), s2_ref[...])
    th, tl = _split(t)
    wh, wl = _split(rw_ref[...])
    lg = _dot_nt(wh, th) + (_dot_nt(wh, tl) + _dot_nt(wl, th))
    sc = _sigmoid(lg)
    sel = sc + rb_ref[...]
    tm = sel.shape[1]
    gsz = N_EXP // N_GRP
    ninf = -jnp.inf
    sel3 = sel.reshape(N_GRP, gsz, tm)
    i3 = lax.broadcasted_iota(jnp.int32, sel3.shape, 1)
    m1 = jnp.max(sel3, axis=1, keepdims=True)
    first = jnp.min(jnp.where(sel3 == m1, i3, gsz), axis=1, keepdims=True)
    m2 = jnp.max(jnp.where(i3 == first, ninf, sel3), axis=1, keepdims=True)
    grp = (m1 + m2).reshape(N_GRP, tm)
    gi = lax.broadcasted_iota(jnp.int32, grp.shape, 0)
    gmask = jnp.zeros(grp.shape, F32)
    for _ in range(TOPK_GRP):
        m = jnp.max(grp, axis=0, keepdims=True)
        pick = jnp.min(jnp.where(grp == m, gi, N_GRP), axis=0, keepdims=True)
        hit = gi == pick
        gmask = jnp.where(hit, 1.0, gmask)
        grp = jnp.where(hit, ninf, grp)
    emask = jnp.broadcast_to(gmask.reshape(N_GRP, 1, tm), (N_GRP, gsz, tm)).reshape(N_EXP, tm)
    msel = jnp.where(emask > 0.5, sel, ninf)
    ei = lax.broadcasted_iota(jnp.int32, msel.shape, 0)
    idxs, ws = [], []
    for _ in range(TOP_K):
        m = jnp.max(msel, axis=0, keepdims=True)
        pick = jnp.min(jnp.where(msel == m, ei, N_EXP), axis=0, keepdims=True)
        hit = ei == pick
        idxs.append(pick)
        ws.append(jnp.sum(jnp.where(hit, sc, 0.0), axis=0, keepdims=True))
        msel = jnp.where(hit, ninf, msel)
    w = jnp.concatenate(ws, axis=0)
    idx_ref[...] = jnp.concatenate(idxs, axis=0)
    wt_ref[...] = w / jnp.sum(w, axis=0, keepdims=True) * ROUTED_SCALE


def moe_router(x, M, g, mods, mp, tm=256):
    full = lambda shape: pl.BlockSpec(shape, lambda i: (0,) * len(shape))
    return pl.pallas_call(
        _router_kernel,
        out_shape=(jax.ShapeDtypeStruct((M, D), F32), jax.ShapeDtypeStruct((TOP_K, M), jnp.int32),
                   jax.ShapeDtypeStruct((TOP_K, M), F32), jax.ShapeDtypeStruct((M, D), F32)),
        grid=(M // tm,),
        in_specs=[pl.BlockSpec((tm, D), lambda i: (i, 0)), full((1, D)),
                  pl.BlockSpec((None, 6, D), lambda i: (_seq_of_rowblock(i, tm), 0, 0)),
                  full((N_EXP, D)), full((N_EXP, 1)), full((D, EXP_FF)), full((D, EXP_FF)), full((EXP_FF, D))],
        out_specs=(pl.BlockSpec((tm, D), lambda i: (i, 0)), pl.BlockSpec((TOP_K, tm), lambda i: (0, i)),
                   pl.BlockSpec((TOP_K, tm), lambda i: (0, i)), pl.BlockSpec((tm, D), lambda i: (i, 0))),
        compiler_params=_cp(("parallel",)), name="moe_router",
    )(x, g.reshape(1, D), mods, jnp.transpose(mp['router_w']), mp['router_bias'].reshape(N_EXP, 1),
      mp['s1'].astype(BF16), mp['s3'].astype(BF16), mp['s2'].astype(BF16))


def _gather_rows(idx_ref, n, src_hbm, dst, sem, slot):
    def body(r, carry):
        pltpu.make_async_copy(src_hbm.at[pl.ds(idx_ref[0, r], 1)], dst.at[slot, pl.ds(r, 1)], sem.at[slot]).start()
        return carry

    lax.fori_loop(0, n, body, 0)


def _wait_rows(n, src_hbm, dst, sem, slot):
    pltpu.make_async_copy(src_hbm.at[pl.ds(0, n)], dst.at[slot], sem.at[slot]).wait()


def _expert_kernel(be_ref, tok_ref, tokn_ref, w_ref, t_hbm, w1_ref, w3_ref, w2_ref, o_ref, xbuf, sem):
    i = pl.program_id(0)
    n = pl.num_programs(0)
    slot = i % 2

    @pl.when(i == 0)
    def _():
        _gather_rows(tok_ref, MOE_BLK, t_hbm, xbuf, sem, 0)

    @pl.when(i + 1 < n)
    def _():
        _gather_rows(tokn_ref, MOE_BLK, t_hbm, xbuf, sem, 1 - slot)

    _wait_rows(MOE_BLK, t_hbm, xbuf, sem, slot)
    xb = xbuf[slot].astype(BF16)
    h = _silu(_dot(xb, w1_ref[...])) * _dot(xb, w3_ref[...])
    o_ref[...] = _dot(h.astype(BF16), w2_ref[...]) * w_ref[...]


def moe_experts(t, buf_tok, buf_w, block_e, w1, w3, w2):
    nb = block_e.shape[0]
    tok3 = buf_tok.reshape(nb, 1, MOE_BLK)
    nxt = lambda i, be: (jnp.minimum(i + 1, nb - 1), 0, 0)
    grid_spec = pltpu.PrefetchScalarGridSpec(
        num_scalar_prefetch=1, grid=(nb,),
        in_specs=[pl.BlockSpec((None, 1, MOE_BLK), lambda i, be: (i, 0, 0), memory_space=pltpu.SMEM),
                  pl.BlockSpec((None, 1, MOE_BLK), nxt, memory_space=pltpu.SMEM),
                  pl.BlockSpec((MOE_BLK, 1), lambda i, be: (i, 0)),
                  pl.BlockSpec(memory_space=pl.ANY),
                  pl.BlockSpec((None, D, EXP_FF), lambda i, be: (be[i], 0, 0)),
                  pl.BlockSpec((None, D, EXP_FF), lambda i, be: (be[i], 0, 0)),
                  pl.BlockSpec((None, EXP_FF, D), lambda i, be: (be[i], 0, 0))],
        out_specs=pl.BlockSpec((MOE_BLK, D), lambda i, be: (i, 0)),
        scratch_shapes=[pltpu.VMEM((2, MOE_BLK, D), F32), pltpu.SemaphoreType.DMA((2,))])
    return pl.pallas_call(
        _expert_kernel, out_shape=jax.ShapeDtypeStruct((nb * MOE_BLK, D), F32), grid_spec=grid_spec,
        compiler_params=_cp(("arbitrary",)), name="moe_experts",
    )(block_e, tok3, tok3, buf_w.reshape(nb * MOE_BLK, 1), t, w1, w3, w2)


MOE_TT = 32


def _combine_kernel(dst_ref, dstn_ref, ys_hbm, sh_ref, x_ref, g_ref, mod_ref, o_ref, buf, sem):
    i = pl.program_id(0)
    n = pl.num_programs(0)
    slot = i % 2
    nrow = MOE_TT * TOP_K

    @pl.when(i == 0)
    def _():
        _gather_rows(dst_ref, nrow, ys_hbm, buf, sem, 0)

    @pl.when(i + 1 < n)
    def _():
        _gather_rows(dstn_ref, nrow, ys_hbm, buf, sem, 1 - slot)

    _wait_rows(nrow, ys_hbm, buf, sem, slot)
    f = sh_ref[...]
    for k in range(TOP_K):
        f = f + buf[slot, k * MOE_TT:(k + 1) * MOE_TT, :]
    ms = jnp.mean(f * f, axis=-1, keepdims=True)
    o_ref[...] = x_ref[...] + mod_ref[5:6, :] * (f * lax.rsqrt(ms + EPS) * g_ref[...])


def moe_combine(ys, dest, sh, x, M, g, mods):
    nt = M // MOE_TT
    nrow = MOE_TT * TOP_K
    d3 = dest.reshape(nt, MOE_TT, TOP_K).transpose(0, 2, 1).reshape(nt, 1, nrow)
    nxt = lambda i: (jnp.minimum(i + 1, nt - 1), 0, 0)
    return pl.pallas_call(
        _combine_kernel, out_shape=jax.ShapeDtypeStruct((M, D), F32), grid=(nt,),
        in_specs=[pl.BlockSpec((None, 1, nrow), lambda i: (i, 0, 0), memory_space=pltpu.SMEM),
                  pl.BlockSpec((None, 1, nrow), nxt, memory_space=pltpu.SMEM),
                  pl.BlockSpec(memory_space=pl.ANY),
                  pl.BlockSpec((MOE_TT, D), lambda i: (i, 0)),
                  pl.BlockSpec((MOE_TT, D), lambda i: (i, 0)),
                  pl.BlockSpec((1, D), lambda i: (0, 0)),
                  pl.BlockSpec((None, 6, D), lambda i: (_seq_of_rowblock(i, MOE_TT), 0, 0))],
        out_specs=pl.BlockSpec((MOE_TT, D), lambda i: (i, 0)),
        scratch_shapes=[pltpu.VMEM((2, nrow, D), F32), pltpu.SemaphoreType.DMA((2,))],
        compiler_params=_cp(("arbitrary",)), name="moe_combine",
    )(d3, d3, ys, sh, x, g.reshape(1, D), mods)


def moe_layer(x, M, g_pre, g_post, mods, mp):
    t, idx_t, wts_t, sh = moe_router(x, M, g_pre, mods, mp)
    mk = M * TOP_K
    nb = -(-(mk + N_EXP * (MOE_BLK - 1)) // MOE_BLK)
    flat_e = jnp.transpose(idx_t).reshape(mk)
    flat_w = jnp.transpose(wts_t).reshape(mk)
    onehot = (flat_e[:, None] == jnp.arange(N_EXP, dtype=jnp.int32)[None, :]).astype(jnp.int32)
    csum = jnp.cumsum(onehot, axis=0)
    rank = jnp.take_along_axis(csum, flat_e[:, None], axis=1)[:, 0] - 1
    counts = csum[-1]
    padded = (counts + MOE_BLK - 1) // MOE_BLK * MOE_BLK
    pend = jnp.cumsum(padded)
    dest = (pend - padded)[flat_e] + rank
    flat_tok = jnp.arange(mk, dtype=jnp.int32) // TOP_K
    buf_tok = jnp.zeros((nb * MOE_BLK,), jnp.int32).at[dest].set(flat_tok)
    buf_w = jnp.zeros((nb * MOE_BLK,), F32).at[dest].set(flat_w)
    block_e = jnp.minimum(jnp.searchsorted(pend, jnp.arange(nb, dtype=jnp.int32) * MOE_BLK, side='right'),
                          N_EXP - 1).astype(jnp.int32)
    ys = moe_experts(t, buf_tok, buf_w, block_e, mp['w1'].astype(BF16), mp['w3'].astype(BF16),
                     mp['w2'].astype(BF16))
    return moe_combine(ys, dest.astype(jnp.int32), sh, x, M, g_post, mods)


MOE_T = 512
MOE_CAP = 128
MOE_EPS = 4


def _router2_kernel(x_ref, g_ref, mod_ref, rw_ref, rb_ref, s1_ref, s3_ref, s2_ref, t_ref, wt_ref, cnt_ref, sh_ref):
    x = x_ref[...]
    ms = jnp.mean(x * x, axis=-1, keepdims=True)
    t = x * lax.rsqrt(ms + EPS) * g_ref[...] * (1.0 + mod_ref[4:5, :]) + mod_ref[3:4, :]
    tb = t.astype(BF16)
    t_ref[...] = tb
    sh_ref[...] = _dot((_silu(_dot(tb, s1_ref[...])) * _dot(tb, s3_ref[...])).astype(BF16), s2_ref[...])
    th, tl = _split(t)
    wh, wl = _split(rw_ref[...])
    lg = _dot_nt(wh, th) + (_dot_nt(wh, tl) + _dot_nt(wl, th))
    sc = _sigmoid(lg)
    sel = sc + rb_ref[...]
    tm = sel.shape[1]
    gsz = N_EXP // N_GRP
    ninf = -jnp.inf
    sel3 = sel.reshape(N_GRP, gsz, tm)
    i3 = lax.broadcasted_iota(jnp.int32, sel3.shape, 1)
    m1 = jnp.max(sel3, axis=1, keepdims=True)
    first = jnp.min(jnp.where(sel3 == m1, i3, gsz), axis=1, keepdims=True)
    m2 = jnp.max(jnp.where(i3 == first, ninf, sel3), axis=1, keepdims=True)
    grp = (m1 + m2).reshape(N_GRP, tm)
    gi = lax.broadcasted_iota(jnp.int32, grp.shape, 0)
    gmask = jnp.zeros(grp.shape, F32)
    for _ in range(TOPK_GRP):
        m = jnp.max(grp, axis=0, keepdims=True)
        pick = jnp.min(jnp.where(grp == m, gi, N_GRP), axis=0, keepdims=True)
        hit = gi == pick
        gmask = jnp.where(hit, 1.0, gmask)
        grp = jnp.where(hit, ninf, grp)
    emask = jnp.broadcast_to(gmask.reshape(N_GRP, 1, tm), (N_GRP, gsz, tm)).reshape(N_EXP, tm)
    msel = jnp.where(emask > 0.5, sel, ninf)
    ei = lax.broadcasted_iota(jnp.int32, msel.shape, 0)
    chosen = jnp.zeros(msel.shape, F32)
    for _ in range(TOP_K):
        m = jnp.max(msel, axis=0, keepdims=True)
        pick = jnp.min(jnp.where(msel == m, ei, N_EXP), axis=0, keepdims=True)
        hit = ei == pick
        chosen = jnp.where(hit, 1.0, chosen)
        msel = jnp.where(hit, ninf, msel)
    w = chosen * sc
    wt = w / jnp.sum(w, axis=0, keepdims=True) * ROUTED_SCALE
    wt_ref[...] = wt
    cnt_ref[...] = jnp.sum((wt > 0.0).astype(F32), axis=1, keepdims=True).astype(jnp.int32)


def moe_router2(x, M, g, mods, mp):
    tm = MOE_T
    full = lambda shape: pl.BlockSpec(shape, lambda i: (0,) * len(shape))
    return pl.pallas_call(
        _router2_kernel,
        out_shape=(jax.ShapeDtypeStruct((M, D), BF16), jax.ShapeDtypeStruct((M // tm, N_EXP, tm), F32),
                   jax.ShapeDtypeStruct((M // tm, N_EXP, 1), jnp.int32), jax.ShapeDtypeStruct((M, D), F32)),
        grid=(M // tm,),
        in_specs=[pl.BlockSpec((tm, D), lambda i: (i, 0)), full((1, D)),
                  pl.BlockSpec((None, 6, D), lambda i: (_seq_of_rowblock(i, tm), 0, 0)),
                  full((N_EXP, D)), full((N_EXP, 1)), full((D, EXP_FF)), full((D, EXP_FF)), full((EXP_FF, D))],
        out_specs=(pl.BlockSpec((tm, D), lambda i: (i, 0)), pl.BlockSpec((None, N_EXP, tm), lambda i: (i, 0, 0)),
                   pl.BlockSpec((None, N_EXP, 1), lambda i: (i, 0, 0)), pl.BlockSpec((tm, D), lambda i: (i, 0))),
        compiler_params=_cp(("parallel",)), name="moe_router",
    )(x, g.reshape(1, D), mods, jnp.transpose(mp['router_w']), mp['router_bias'].reshape(N_EXP, 1),
      mp['s1'].astype(BF16), mp['s3'].astype(BF16), mp['s2'].astype(BF16))


def _moe2_kernel(cnt_ref, t_ref, wt_ref, sh_ref, x_ref, g_ref, mod_ref, w1_ref, w3_ref, w2_ref, o_ref,
                 rank_sc, acc_sc):
    i = pl.program_id(0)
    eb = pl.program_id(1)
    T = MOE_T

    @pl.when(eb == 0)
    def _():
        picked = (wt_ref[...] > 0.0).astype(BF16)
        before = (lax.broadcasted_iota(jnp.int32, (T, T), 0) < lax.broadcasted_iota(jnp.int32, (T, T), 1))
        rank_sc[...] = _dot(picked, before.astype(BF16))
        acc_sc[...] = jnp.zeros_like(acc_sc)

    slot = lax.broadcasted_iota(jnp.int32, (MOE_CAP, T), 0).astype(F32)
    for j in range(MOE_EPS):
        e = eb * MOE_EPS + j
        n_tok = cnt_ref[i * N_EXP + e]
        w_row = wt_ref[pl.ds(e, 1), :]
        r_row = rank_sc[pl.ds(e, 1), :]

        def chunk(ci, carry, j=j, w_row=w_row, r_row=r_row):
            hit = ((r_row - (ci * MOE_CAP).astype(F32)) == slot) & (w_row > 0.0)
            pb = hit.astype(F32).astype(BF16)
            xg = _dot(pb, t_ref[...]).astype(BF16)
            h = _silu(_dot(xg, w1_ref[j])) * _dot(xg, w3_ref[j])
            y = _dot(h.astype(BF16), w2_ref[j])
            w_slot = jnp.sum(jnp.where(hit, w_row, 0.0), axis=1, keepdims=True)
            yw = (y * w_slot).astype(BF16)
            acc_sc[...] += lax.dot_general(pb, yw, (((0,), (0,)), ((), ())), preferred_element_type=F32)
            return carry

        lax.fori_loop(0, (n_tok + MOE_CAP - 1) // MOE_CAP, chunk, 0)

    @pl.when(eb == pl.num_programs(1) - 1)
    def _():
        f = acc_sc[...] + sh_ref[...]
        ms = jnp.mean(f * f, axis=-1, keepdims=True)
        o_ref[...] = x_ref[...] + mod_ref[5:6, :] * (f * lax.rsqrt(ms + EPS) * g_ref[...])


def moe_layer2(x, M, g_pre, g_post, mods, mp):
    t, wt, cnt, sh = moe_router2(x, M, g_pre, mods, mp)
    T = MOE_T
    grid_spec = pltpu.PrefetchScalarGridSpec(
        num_scalar_prefetch=1, grid=(M // T, N_EXP // MOE_EPS),
        in_specs=[pl.BlockSpec((T, D), lambda i, e, c: (i, 0)),
                  pl.BlockSpec((N_EXP, T), lambda i, e, c: (0, i)),
                  pl.BlockSpec((T, D), lambda i, e, c: (i, 0)),
                  pl.BlockSpec((T, D), lambda i, e, c: (i, 0)),
                  pl.BlockSpec((1, D), lambda i, e, c: (0, 0)),
                  pl.BlockSpec((None, 6, D), lambda i, e, c: (_seq_of_rowblock(i, T), 0, 0)),
                  pl.BlockSpec((MOE_EPS, D, EXP_FF), lambda i, e, c: (e, 0, 0)),
                  pl.BlockSpec((MOE_EPS, D, EXP_FF), lambda i, e, c: (e, 0, 0)),
                  pl.BlockSpec((MOE_EPS, EXP_FF, D), lambda i, e, c: (e, 0, 0))],
        out_specs=pl.BlockSpec((T, D), lambda i, e, c: (i, 0)),
        scratch_shapes=[pltpu.VMEM((N_EXP, T), F32), pltpu.VMEM((T, D), F32)])
    return pl.pallas_call(
        _moe2_kernel, out_shape=jax.ShapeDtypeStruct((M, D), F32), grid_spec=grid_spec,
        compiler_params=_cp(("parallel", "arbitrary")), name="moe_experts",
    )(cnt.reshape(-1), t, wt, sh, x, g_post.reshape(1, D), mods, mp['w1'].astype(BF16), mp['w3'].astype(BF16),
      mp['w2'].astype(BF16))


def _moe3_kernel(cnt_ref, t_ref, wt_ref, w1_ref, w3_ref, w2_ref, o_ref, rank_sc, *, nsub):
    i = pl.program_id(0)
    eb = pl.program_id(1)
    T, CAP, EPS = MOE_T, MOE_CAP, MOE_EPS

    @pl.when(eb == 0)
    def _():
        before = (lax.broadcasted_iota(jnp.int32, (T, T), 0) < lax.broadcasted_iota(jnp.int32, (T, T), 1))
        before = before.astype(BF16)
        for s in range(nsub):
            rank_sc[s] = _dot((wt_ref[s] > 0.0).astype(BF16), before)
        o_ref[...] = jnp.zeros_like(o_ref)

    slot = lax.broadcasted_iota(jnp.int32, (CAP, T), 0).astype(F32)

    def one_hot(s, e, first_slot):
        w_row = wt_ref[s, pl.ds(e, 1), :]
        r_row = rank_sc[s, pl.ds(e, 1), :]
        hit = ((r_row - first_slot) == slot) & (w_row > 0.0)
        w_slot = jnp.sum(jnp.where(hit, w_row, 0.0), axis=1, keepdims=True)
        return hit.astype(F32).astype(BF16), w_slot

    def swiglu(xg, j):
        h = _silu(_dot(xg, w1_ref[j])) * _dot(xg, w3_ref[j])
        return _dot(h.astype(BF16), w2_ref[j])

    hot = [[one_hot(s, eb * EPS + j, 0.0) for j in range(EPS)] for s in range(nsub)]
    pb = [jnp.concatenate([hot[s][j][0] for j in range(EPS)], axis=0) for s in range(nsub)]
    xg = [_dot(pb[s], t_ref[s * T:(s + 1) * T, :]).astype(BF16) for s in range(nsub)]
    y = [swiglu(jnp.concatenate([xg[s][j * CAP:(j + 1) * CAP] for s in range(nsub)], axis=0), j)
         for j in range(EPS)]
    for s in range(nsub):
        yw = jnp.concatenate([y[j][s * CAP:(s + 1) * CAP] * hot[s][j][1] for j in range(EPS)], axis=0)
        o_ref[s * T:(s + 1) * T, :] += lax.dot_general(pb[s], yw.astype(BF16), (((0,), (0,)), ((), ())),
                                                       preferred_element_type=F32)

    def pair(idx, carry):
        s = idx // EPS
        j = idx % EPS
        e = eb * EPS + j
        n_tok = cnt_ref[(i * nsub + s) * N_EXP + e]
        rows = pl.ds(pl.multiple_of(s * T, T), T)

        def chunk(ci, c2):
            p1, w_slot = one_hot(s, e, (ci * CAP).astype(F32))
            yw = (swiglu(_dot(p1, t_ref[rows, :]).astype(BF16), j) * w_slot).astype(BF16)
            o_ref[rows, :] += lax.dot_general(p1, yw, (((0,), (0,)), ((), ())), preferred_element_type=F32)
            return c2

        lax.fori_loop(1, (n_tok + CAP - 1) // CAP, chunk, 0)
        return carry

    lax.fori_loop(0, nsub * EPS, pair, 0)


def _moe_out_kernel(r_ref, sh_ref, x_ref, g_ref, mod_ref, o_ref):
    f = r_ref[...] + sh_ref[...]
    ms = jnp.mean(f * f, axis=-1, keepdims=True)
    o_ref[...] = x_ref[...] + mod_ref[5:6, :] * (f * lax.rsqrt(ms + EPS) * g_ref[...])


def moe_layer3(x, M, g_pre, g_post, mods, mp, nsub):
    t, wt, cnt, sh = moe_router2(x, M, g_pre, mods, mp)
    T = MOE_T
    TS = nsub * T
    grid_spec = pltpu.PrefetchScalarGridSpec(
        num_scalar_prefetch=1, grid=(M // TS, N_EXP // MOE_EPS),
        in_specs=[pl.BlockSpec((TS, D), lambda i, e, c: (i, 0)),
                  pl.BlockSpec((nsub, N_EXP, T), lambda i, e, c: (i, 0, 0)),
                  pl.BlockSpec((MOE_EPS, D, EXP_FF), lambda i, e, c: (e, 0, 0)),
                  pl.BlockSpec((MOE_EPS, D, EXP_FF), lambda i, e, c: (e, 0, 0)),
                  pl.BlockSpec((MOE_EPS, EXP_FF, D), lambda i, e, c: (e, 0, 0))],
        out_specs=pl.BlockSpec((TS, D), lambda i, e, c: (i, 0)),
        scratch_shapes=[pltpu.VMEM((nsub, N_EXP, T), F32)])
    routed = pl.pallas_call(
        functools.partial(_moe3_kernel, nsub=nsub), out_shape=jax.ShapeDtypeStruct((M, D), F32),
        grid_spec=grid_spec, compiler_params=_cp(("parallel", "arbitrary")), name="moe_experts",
    )(cnt.reshape(-1), t, wt, mp['w1'].astype(BF16), mp['w3'].astype(BF16), mp['w2'].astype(BF16))
    tm = 512
    return pl.pallas_call(
        _moe_out_kernel, out_shape=jax.ShapeDtypeStruct((M, D), F32), grid=(M // tm,),
        in_specs=[pl.BlockSpec((tm, D), lambda i: (i, 0)), pl.BlockSpec((tm, D), lambda i: (i, 0)),
                  pl.BlockSpec((tm, D), lambda i: (i, 0)), pl.BlockSpec((1, D), lambda i: (0, 0)),
                  pl.BlockSpec((None, 6, D), lambda i: (_seq_of_rowblock(i, tm), 0, 0))],
        out_specs=pl.BlockSpec((tm, D), lambda i: (i, 0)),
        compiler_params=_cp(("parallel",)), name="moe_output")(routed, sh, x, g_post.reshape(1, D), mods)


def kernel(x, c, ctx, c_ctx, mod_w, mod_b, norm_mix_pre, norm_mix_post, norm_ffn_pre, norm_ffn_post, router_w, router_bias, expert_w1, expert_w3, expert_w2, shared_w1, shared_w3, shared_w2, ev_w_in, ev_w_out, ssd_conv_w, ssd_conv_b, ssd_dt_bias, ssd_a_log, ssd_d, ssd_norm_w, hy_conv_w, hy_conv_b, hy_mlp_w0, hy_mlp_b0, hy_freq0, hy_mlp_w1, hy_mlp_b1, hy_freq1, hy_mlp_w2, hy_bias, od_w_in, od_w_out, rw_mu, rw_w0, rw_w_up, rw_a0, rw_a_up, rw_g_up, rw_k_k, rw_k_a, rw_r_k, rw_ln_w, rw_ln_b, at_q_norm, at_k_norm):
    xs = jnp.concatenate([x.reshape(N_LAT, D), ctx.reshape(BATCH * CTX, D)], axis=0)
    cvecs = jnp.zeros((8, D), F32).at[0:BATCH].set(c).at[BATCH].set(c_ctx)
    assert mod_w.shape[0] == 2, "one even (SSD | Hyena) layer followed by one odd (RWKV | attention) layer"

    def moe_params(i):
        return dict(router_w=router_w[i], router_bias=router_bias[i], w1=expert_w1[i], w3=expert_w3[i],
                    w2=expert_w2[i], s1=shared_w1[i], s3=shared_w3[i], s2=shared_w2[i])

    mods = modulation(cvecs, mod_w[0], mod_b[0])[:BATCH + 1].reshape(BATCH + 1, 6, D)
    ep = dict(w_in=ev_w_in[0], w_out=ev_w_out[0], ssd_conv_w=ssd_conv_w[0], ssd_conv_b=ssd_conv_b[0],
              ssd_dt_bias=ssd_dt_bias[0], ssd_a_log=ssd_a_log[0], ssd_d=ssd_d[0], ssd_norm_w=ssd_norm_w[0],
              hy_conv_w=hy_conv_w[0], hy_conv_b=hy_conv_b[0], hy_mlp_w0=hy_mlp_w0[0], hy_mlp_b0=hy_mlp_b0[0],
              hy_freq0=hy_freq0[0], hy_mlp_w1=hy_mlp_w1[0], hy_mlp_b1=hy_mlp_b1[0], hy_freq1=hy_freq1[0],
              hy_mlp_w2=hy_mlp_w2[0], hy_bias=hy_bias[0])
    xs = even_mixer(xs, mods, norm_mix_pre[0], norm_mix_post[0], ep)
    xs = moe_layer3(xs, N_ROWS, norm_ffn_pre[0], norm_ffn_post[0], mods, moe_params(0), 3)
    mods = modulation(cvecs, mod_w[1], mod_b[1])[:BATCH + 1].reshape(BATCH + 1, 6, D)
    op = dict(w_in=od_w_in[0], w_out=od_w_out[0], mu=rw_mu[0], w0=rw_w0[0], w_up=rw_w_up[0], a0=rw_a0[0],
              a_up=rw_a_up[0], g_up=rw_g_up[0], k_k=rw_k_k[0], k_a=rw_k_a[0], r_k=rw_r_k[0], ln_w=rw_ln_w[0],
              ln_b=rw_ln_b[0], q_norm=at_q_norm[0], k_norm=at_k_norm[0])
    xl = odd_mixer(xs, mods, norm_mix_pre[1], norm_mix_post[1], op)
    xl = moe_layer3(xl, N_LAT, norm_ffn_pre[1], norm_ffn_post[1], mods, moe_params(1), 4)
    return xl.reshape(BATCH, SEQ, D)
```

```python
import functools
import math

import numpy as np
import jax
import jax.numpy as jnp
from jax import lax
from jax.experimental import pallas as pl
from jax.experimental.pallas import tpu as pltpu

F32 = jnp.float32
BF16 = jnp.bfloat16

D = 1024
BATCH = 2
SEQ = 8192
CTX = 256
N_LAT = BATCH * SEQ
N_ROWS = N_LAT + BATCH * CTX
EPS = 1e-6
GRID_W = 64

SSD_HEADS = 16
SSD_P = 64
SSD_G = 2
SSD_S = 128
SSD_Q = 128
HY_W = 1024
HY_EMB = 33
HY_HID = 64

RW_H = 16
RW_N = 64
RW_CHUNK = 128
RW_GN_EPS = 64e-5

AT_KV = 4
AT_HD = 64

N_EXP = 64
TOP_K = 8
N_GRP = 8
TOPK_GRP = 4
EXP_FF = 256
ROUTED_SCALE = 2.5
MOE_BLK = 128

VMEM_LIMIT = 56 * 1024 * 1024


def _cp(sem, vmem=None):
    return pltpu.CompilerParams(dimension_semantics=sem, vmem_limit_bytes=vmem or VMEM_LIMIT)


def _dot(a, b):
    return jnp.dot(a, b, preferred_element_type=F32)


def _dot_nt(a, b):
    return lax.dot_general(a, b, (((1,), (1,)), ((), ())), preferred_element_type=F32)


def _split(x):
    hi = x.astype(BF16)
    lo = (x - hi.astype(F32)).astype(BF16)
    return hi, lo


def _dot3(a, b):
    ah, al = _split(a)
    bh, bl = _split(b)
    return _dot(ah, bh) + (_dot(ah, bl) + _dot(al, bh))


def _dot2l(a, b):
    ah, al = _split(a)
    return _dot(ah, b) + _dot(al, b)


def _dot2r(a, b):
    bh, bl = _split(b)
    return _dot(a, bh) + _dot(a, bl)


def _silu(x):
    return x * (1.0 / (1.0 + jnp.exp(-x)))


def _sigmoid(x):
    return 1.0 / (1.0 + jnp.exp(-x))


def _softplus(x):
    return jnp.maximum(x, 0.0) + jnp.log(1.0 + jnp.exp(-jnp.abs(x)))


def _seq_of_rowblock(i, tm):
    return jnp.minimum((i * tm) // SEQ, 2)


def _mm_kernel(a_ref, b_ref, o_ref, *, passes):
    a = a_ref[...]
    b = b_ref[...]
    if passes == 3:
        o_ref[...] = _dot3(a.astype(F32), b.astype(F32))
    else:
        o_ref[...] = _dot(a.astype(BF16), b.astype(BF16))


def matmul(a, b, tm, tn, passes=1, name="mm"):
    M, K = a.shape
    N = b.shape[1]
    return pl.pallas_call(
        functools.partial(_mm_kernel, passes=passes),
        out_shape=jax.ShapeDtypeStruct((M, N), F32),
        grid=(M // tm, N // tn),
        in_specs=[pl.BlockSpec((tm, K), lambda i, j: (i, 0)),
                  pl.BlockSpec((K, tn), lambda i, j: (0, j))],
        out_specs=pl.BlockSpec((tm, tn), lambda i, j: (i, j)),
        compiler_params=_cp(("parallel", "parallel")), name=name)(a, b)


def _nmm_kernel(x_ref, g_ref, mod_ref, w_ref, o_ref, a_sc, *, shift_i, scale_i):
    @pl.when(pl.program_id(1) == 0)
    def _():
        x = x_ref[...]
        ms = jnp.mean(x * x, axis=-1, keepdims=True)
        y = x * lax.rsqrt(ms + EPS) * g_ref[...]
        h = y * (1.0 + mod_ref[scale_i:scale_i + 1, :]) + mod_ref[shift_i:shift_i + 1, :]
        a_sc[...] = h.astype(BF16)

    o_ref[...] = _dot(a_sc[...], w_ref[...])


def norm_mod_matmul(x, g, mods, w, shift_i, scale_i, tm=512, tn=None, name="nmm"):
    M = x.shape[0]
    N = w.shape[1]
    tn = tn or N
    return pl.pallas_call(
        functools.partial(_nmm_kernel, shift_i=shift_i, scale_i=scale_i),
        out_shape=jax.ShapeDtypeStruct((M, N), F32),
        grid=(M // tm, N // tn),
        in_specs=[pl.BlockSpec((tm, D), lambda i, j: (i, 0)),
                  pl.BlockSpec((1, D), lambda i, j: (0, 0)),
                  pl.BlockSpec((None, 6, D), lambda i, j: (_seq_of_rowblock(i, tm), 0, 0)),
                  pl.BlockSpec((D, tn), lambda i, j: (0, j))],
        out_specs=pl.BlockSpec((tm, tn), lambda i, j: (i, j)),
        scratch_shapes=[pltpu.VMEM((tm, D), BF16)],
        compiler_params=_cp(("parallel", "arbitrary")), name=name)(x, g.reshape(1, D), mods, w)


def _outproj_kernel(a1_ref, a2_ref, w_ref, x_ref, g_ref, mod_ref, o_ref, *, gate_i):
    y = _dot(a1_ref[...].astype(BF16), w_ref[0:D, :]) + _dot(a2_ref[...].astype(BF16), w_ref[D:2 * D, :])
    ms = jnp.mean(y * y, axis=-1, keepdims=True)
    o_ref[...] = x_ref[...] + mod_ref[gate_i:gate_i + 1, :] * (y * lax.rsqrt(ms + EPS) * g_ref[...])


def outproj_residual(a1, a2, w, x, g, mods, gate_i, tm=256, name="outproj"):
    M = a1.shape[0]
    return pl.pallas_call(
        functools.partial(_outproj_kernel, gate_i=gate_i),
        out_shape=jax.ShapeDtypeStruct((M, D), F32),
        grid=(M // tm,),
        in_specs=[pl.BlockSpec((tm, D), lambda i: (i, 0)),
                  pl.BlockSpec((tm, D), lambda i: (i, 0)),
                  pl.BlockSpec((2 * D, D), lambda i: (0, 0)),
                  pl.BlockSpec((tm, D), lambda i: (i, 0)),
                  pl.BlockSpec((1, D), lambda i: (0, 0)),
                  pl.BlockSpec((None, 6, D), lambda i: (_seq_of_rowblock(i, tm), 0, 0))],
        out_specs=pl.BlockSpec((tm, D), lambda i: (i, 0)),
        compiler_params=_cp(("parallel",)), name=name)(a1, a2, w, x, g.reshape(1, D), mods)


def _mod_kernel(c_ref, w_ref, b_ref, o_ref):
    o_ref[...] = _dot3(_silu(c_ref[...]), w_ref[...]) + b_ref[...]


def modulation(cvecs, w, b):
    N = w.shape[1]
    tn = 1024
    return pl.pallas_call(
        _mod_kernel, out_shape=jax.ShapeDtypeStruct((8, N), F32), grid=(N // tn,),
        in_specs=[pl.BlockSpec((8, D), lambda j: (0, 0)),
                  pl.BlockSpec((D, tn), lambda j: (0, j)),
                  pl.BlockSpec((1, tn), lambda j: (0, j))],
        out_specs=pl.BlockSpec((8, tn), lambda j: (0, j)),
        compiler_params=_cp(("parallel",)), name="modulation")(cvecs, w, b.reshape(1, N))


def _neighbours(x_ref, base, ch, length):
    cur = x_ref[pl.ds(base, ch), :]
    rows = lax.broadcasted_iota(jnp.int32, cur.shape, 0)
    pbase = pl.multiple_of(jnp.maximum(base - 8, 0), 8)
    nbase = pl.multiple_of(jnp.minimum(base + ch, length - 8), 8)
    prev_row = x_ref[pl.ds(pbase, 8), :][7:8, :] * (base > 0).astype(F32)
    next_row = x_ref[pl.ds(nbase, 8), :][0:1, :] * (base + ch < length).astype(F32)
    xm1 = jnp.where(rows == 0, prev_row, pltpu.roll(cur, 1, 0))
    xp1 = jnp.where(rows == ch - 1, next_row, pltpu.roll(cur, ch - 1, 0))
    return xm1, cur, xp1


def _conv3_kernel(x_ref, w_ref, b_ref, o_ref, *, length, ch, act):
    def body(c, carry):
        base = pl.multiple_of(c * ch, ch)
        xm1, cur, xp1 = _neighbours(x_ref, base, ch, length)
        y = xm1 * w_ref[0:1, :] + cur * w_ref[1:2, :] + xp1 * w_ref[2:3, :] + b_ref[...]
        if act:
            y = _silu(y)
        o_ref[pl.ds(base, ch), :] = y
        return carry

    lax.fori_loop(0, length // ch, body, 0)


def dwconv3(p, col0, ncols, w, b, act, name, cb=256):
    wt = jnp.transpose(w)
    b2 = b.reshape(1, ncols)
    outs = []
    for (length, row0, ch) in ((SEQ, 0, 512), (CTX, N_LAT, 256)):
        rb0 = row0 // length
        outs.append(pl.pallas_call(
            functools.partial(_conv3_kernel, length=length, ch=ch, act=act),
            out_shape=jax.ShapeDtypeStruct((BATCH * length, ncols), F32),
            grid=(BATCH, ncols // cb),
            in_specs=[pl.BlockSpec((length, cb), lambda s, j: (rb0 + s, col0 // cb + j)),
                      pl.BlockSpec((3, cb), lambda s, j: (0, j)),
                      pl.BlockSpec((1, cb), lambda s, j: (0, j))],
            out_specs=pl.BlockSpec((length, cb), lambda s, j: (s, j)),
            compiler_params=_cp(("parallel", "parallel")), name=name)(p, wt, b2))
    return jnp.concatenate(outs, axis=0)


def _ssd_kernel(xs_ref, bm_ref, cm_ref, dt_ref, dtT_ref, bias_ref, biasT_ref, alog_ref, alogT_ref,
                y_ref, st_ref):
    d = pl.program_id(0)
    c = pl.program_id(2)
    Q = SSD_Q
    HG = SSD_HEADS // SSD_G

    @pl.when(c == 0)
    def _():
        st_ref[...] = jnp.zeros_like(st_ref)

    isb = d == 1
    sgn = 1 - 2 * d
    ii = lax.broadcasted_iota(jnp.int32, (Q, Q), 0)
    jj = lax.broadcasted_iota(jnp.int32, (Q, Q), 1)
    tri = (jj <= ii).astype(BF16)
    triT = (ii <= jj).astype(BF16)
    mask = sgn * (ii - jj) >= 0
    xs = xs_ref[...]
    G = range(SSD_G)
    dt = [_softplus(dt_ref[g] + bias_ref[g]) for g in G]
    dtT = [_softplus(dtT_ref[g] + biasT_ref[g]) for g in G]
    a = [dt[g] * (-jnp.exp(alog_ref[g])) for g in G]
    aT = [dtT[g] * (-jnp.exp(alogT_ref[g])) for g in G]
    cs = [_dot2r(tri, a[g]) for g in G]
    csT = [_dot2l(aT[g], triT) for g in G]
    tot = [cs[g][Q - 1:Q, :] for g in G]
    p = [jnp.where(isb, a[g] - cs[g], cs[g]) for g in G]
    pT = [jnp.where(isb, aT[g] - csT[g], csT[g]) for g in G]
    dec_out = [jnp.exp(jnp.where(isb, tot[g], 0.0) + p[g]) for g in G]
    dec_state = [jnp.exp(jnp.where(isb, 0.0, tot[g]) - p[g]) for g in G]
    chunk_dec = [jnp.exp(tot[g]) for g in G]
    bm = [bm_ref[:, g * SSD_S:(g + 1) * SSD_S].astype(BF16) for g in G]
    cm = [cm_ref[:, g * SSD_S:(g + 1) * SSD_S].astype(BF16) for g in G]
    cb = [_dot_nt(cm[g], bm[g]) for g in G]
    GH = [(g, h) for g in G for h in range(HG)]
    NH = range(len(GH))
    lm = [(cb[g] * jnp.exp(jnp.where(mask, p[g][:, h:h + 1] - pT[g][h:h + 1, :], -1e30))).astype(BF16)
          for g, h in GH]
    xh = [xs[:, n * SSD_P:(n + 1) * SSD_P] * dt[g][:, h:h + 1] for n, (g, h) in enumerate(GH)]
    s_old = [st_ref[n] for n in NH]
    y_in = [_dot(lm[n], xh[n].astype(BF16)) for n in NH]
    y_st = [_dot(cm[g], s_old[n].astype(BF16)) for n, (g, h) in enumerate(GH)]
    upd = [lax.dot_general(bm[g], (xh[n] * dec_state[g][:, h:h + 1]).astype(BF16), (((0,), (0,)), ((), ())),
                           preferred_element_type=F32) for n, (g, h) in enumerate(GH)]
    for n, (g, h) in enumerate(GH):
        st_ref[n] = chunk_dec[g][:, h:h + 1] * s_old[n] + upd[n]
    y_ref[...] = jnp.concatenate([y_in[n] + dec_out[g][:, h:h + 1] * y_st[n] for n, (g, h) in enumerate(GH)],
                                 axis=1)


def _ssd_rowblock(d, b, c):
    n_ctx = CTX // SSD_Q
    n_lat = SEQ // SSD_Q
    cc = jnp.where(d == 0, c, n_ctx - 1 - c)
    lc = jnp.where(d == 0, c - n_ctx, n_ctx + n_lat - 1 - c)
    return jnp.where(c < n_ctx, N_LAT // SSD_Q + b * n_ctx + cc, b * n_lat + lc)


def ssd_scan(xbc, dt_raw, dt_bias, a_log):
    HG = SSD_HEADS // SSD_G
    W = SSD_HEADS * SSD_P
    dsel = dt_raw[:, :2 * SSD_HEADS].reshape(N_ROWS, 2, SSD_G, HG).transpose(1, 2, 0, 3)
    dselT = dsel.transpose(0, 1, 3, 2)
    bias = dt_bias.reshape(2, SSD_G, 1, HG)
    biasT = dt_bias.reshape(2, SSD_G, HG, 1)
    alog = a_log.reshape(2, SSD_G, 1, HG)
    alogT = a_log.reshape(2, SSD_G, HG, 1)
    nch = (CTX + SEQ) // SSD_Q
    rb = _ssd_rowblock
    GS = SSD_G * SSD_S
    par = lambda shape: pl.BlockSpec((None,) + shape, lambda d, b, c: (d, 0, 0, 0))
    return pl.pallas_call(
        _ssd_kernel,
        out_shape=jax.ShapeDtypeStruct((2, N_ROWS, W), F32),
        grid=(2, BATCH, nch),
        in_specs=[pl.BlockSpec((SSD_Q, W), lambda d, b, c: (rb(d, b, c), 0)),
                  pl.BlockSpec((SSD_Q, GS), lambda d, b, c: (rb(d, b, c), W // GS)),
                  pl.BlockSpec((SSD_Q, GS), lambda d, b, c: (rb(d, b, c), W // GS + 1)),
                  pl.BlockSpec((None, SSD_G, SSD_Q, HG), lambda d, b, c: (d, 0, rb(d, b, c), 0)),
                  pl.BlockSpec((None, SSD_G, HG, SSD_Q), lambda d, b, c: (d, 0, 0, rb(d, b, c))),
                  par((SSD_G, 1, HG)), par((SSD_G, HG, 1)), par((SSD_G, 1, HG)), par((SSD_G, HG, 1))],
        out_specs=pl.BlockSpec((None, SSD_Q, W), lambda d, b, c: (d, rb(d, b, c), 0)),
        scratch_shapes=[pltpu.VMEM((SSD_HEADS, SSD_S, SSD_P), F32)],
        compiler_params=_cp(("parallel", "parallel", "arbitrary")), name="ssd_scan",
    )(xbc, xbc, xbc, dsel, dselT, bias, biasT, alog, alogT)


def _ssd_out_kernel(yf_ref, yb_ref, xs_ref, z_ref, dskip_ref, nw_ref, o_ref):
    y = yf_ref[...] + yb_ref[...] + xs_ref[...] * dskip_ref[...]
    y = y * _silu(z_ref[...])
    gs = SSD_HEADS * SSD_P // SSD_G
    parts = []
    for g in range(SSD_G):
        yg = y[:, g * gs:(g + 1) * gs]
        parts.append(yg * lax.rsqrt(jnp.mean(yg * yg, axis=-1, keepdims=True) + EPS))
    o_ref[...] = jnp.concatenate(parts, axis=1) * nw_ref[...]


def ssd_output(y2, xbc, p, d_skip, norm_w, tm=256):
    W = SSD_HEADS * SSD_P
    dexp = jnp.repeat(d_skip, SSD_P).reshape(1, W)
    return pl.pallas_call(
        _ssd_out_kernel, out_shape=jax.ShapeDtypeStruct((N_ROWS, W), F32), grid=(N_ROWS // tm,),
        in_specs=[pl.BlockSpec((None, tm, W), lambda i: (0, i, 0)),
                  pl.BlockSpec((None, tm, W), lambda i: (1, i, 0)),
                  pl.BlockSpec((tm, W), lambda i: (i, 0)),
                  pl.BlockSpec((tm, W), lambda i: (i, 0)),
                  pl.BlockSpec((1, W), lambda i: (0, 0)),
                  pl.BlockSpec((1, W), lambda i: (0, 0))],
        out_specs=pl.BlockSpec((tm, W), lambda i: (i, 0)),
        compiler_params=_cp(("parallel",)), name="ssd_output")(y2, y2, xbc, p, dexp, norm_w.reshape(1, W))


def _hyfilt_kernel(f_ref, w0_ref, b0_ref, fr0_ref, w1_ref, b1_ref, fr1_ref, w2_ref, dl_ref, h_ref, ss_ref, *,
                   n_tiles):
    f = f_ref[...]
    h = jnp.sin(fr0_ref[...] * (_dot3(f, w0_ref[...]) + b0_ref[...]))
    h = jnp.sin(fr1_ref[...] * (_dot3(h, w1_ref[...]) + b1_ref[...]))
    h = _dot3(h, w2_ref[...])
    h = h * jnp.exp(-f[:, 0:1] * dl_ref[...])
    side = pl.program_id(0) // n_tiles
    j = pl.program_id(0) % n_tiles

    @pl.when(j == 0)
    def _():
        ss_ref[...] = jnp.zeros_like(ss_ref)

    ss_ref[...] += jnp.sum(h * h, axis=0, keepdims=True)
    row = lax.broadcasted_iota(jnp.int32, (h.shape[0], 1), 0) + j * h.shape[0]
    h_ref[...] = jnp.where((side == 1) & (row == 0), 0.0, h)


def hyena_filter_taps(L, hp):
    pos = jnp.arange(L, dtype=F32)
    t = pos / (L - 1)
    bands = (HY_EMB - 1) // 2
    freqs = jnp.linspace(1e-4, bands - 1, bands, dtype=F32)
    ang = (2.0 * math.pi / L) * pos[:, None] * freqs[None, :]
    feats = jnp.concatenate([t[:, None], jnp.cos(ang), -jnp.sin(ang)], axis=-1)
    feats = jnp.pad(feats, ((0, 0), (0, 128 - HY_EMB)))
    feats = jnp.concatenate([feats, jnp.flip(feats, axis=0)], axis=0)
    w0 = jnp.pad(hp['hy_mlp_w0'], ((0, 128 - HY_EMB), (0, 0)))
    min_decay = math.log(1e-2) / 1.5
    max_decay = math.log(1e-2) / 0.3
    deltas = jnp.abs(jnp.linspace(min_decay, max_decay, HY_W, dtype=F32))
    dl = jnp.tile(deltas, 2).reshape(1, 2 * HY_W)
    w2 = hp['hy_mlp_w2'].reshape(HY_HID, 2, 2, HY_W).transpose(0, 2, 1, 3).reshape(HY_HID, 4 * HY_W)
    tl = min(L, 512)
    n_tiles = L // tl
    NS = 2 * HY_W
    full = lambda shape: pl.BlockSpec(shape, lambda i: (0, 0))
    return pl.pallas_call(
        functools.partial(_hyfilt_kernel, n_tiles=n_tiles),
        out_shape=(jax.ShapeDtypeStruct((2 * L, NS), F32), jax.ShapeDtypeStruct((1, 2 * NS), F32)),
        grid=(2 * n_tiles,),
        in_specs=[pl.BlockSpec((tl, 128), lambda i: (i, 0)), full((128, HY_HID)), full((1, HY_HID)),
                  full((1, HY_HID)), full((HY_HID, HY_HID)), full((1, HY_HID)), full((1, HY_HID)),
                  pl.BlockSpec((HY_HID, NS), lambda i: (0, i // n_tiles)), full((1, NS))],
        out_specs=(pl.BlockSpec((tl, NS), lambda i: (i, 0)), pl.BlockSpec((1, NS), lambda i: (0, i // n_tiles))),
        compiler_params=_cp(("arbitrary",)), name="hyena_filter",
    )(feats, w0, hp['hy_mlp_b0'].reshape(1, -1), hp['hy_freq0'].reshape(1, -1), hp['hy_mlp_w1'],
      hp['hy_mlp_b1'].reshape(1, -1), hp['hy_freq1'].reshape(1, -1), w2, dl)


def _cis(num, den):
    ang = (2.0 * math.pi / den) * (num % den).astype(F32)
    return jnp.cos(ang), -jnp.sin(ang)


def _fft_consts(NB, BS):
    N = NB * BS
    h = NB // 2
    k1 = jnp.arange(h, dtype=jnp.int32)
    j = jnp.arange(NB, dtype=jnp.int32)
    re, im = _cis(j[None, :] * (2 * k1[:, None] + 1), 2 * NB)
    f1 = jnp.concatenate([re, im], axis=0)
    neg = jnp.where(j >= h, -1.0, 1.0)[None, :]
    f1_data = f1[:, :h]
    f1_filt = f1 * neg
    f1_inv = (2.0 / N) * jnp.concatenate([re[:, :h].T, im[:, :h].T], axis=1)
    r = jnp.arange(BS, dtype=jnp.int32)
    k2 = jnp.arange(BS, dtype=jnp.int32)
    kk = 2 * k1[:, None, None] + 2 * NB * k2[None, :, None] + 1
    gre, gim = _cis(kk * r[None, None, :], 2 * N)
    gf = jnp.concatenate([jnp.concatenate([gre, -gim], axis=2), jnp.concatenate([gim, gre], axis=2)], axis=1)
    gret, gimt = gre.transpose(0, 2, 1), gim.transpose(0, 2, 1)
    gi = jnp.concatenate([jnp.concatenate([gret, gimt], axis=2), jnp.concatenate([-gimt, gret], axis=2)], axis=1)
    return (f1_data.astype(BF16), f1_filt.astype(BF16), f1_inv.astype(BF16), gf.astype(BF16), gi.astype(BF16))


FFT_PAD = 8


def _fft_fwd_kernel(u_ref, f1_ref, g_ref, o_ref, t_sc, *, NB, BS, nj, kg):
    pitch = NB + FFT_PAD
    @pl.when(pl.program_id(2) == 0)
    def _():
        f1 = f1_ref[...]

        def body(r, carry):
            xr = u_ref[pl.ds(r, nj, stride=BS), :].astype(BF16)
            t_sc[pl.ds(pl.multiple_of(r * pitch, 8), NB), :] = _dot(f1, xr)
            return carry

        lax.fori_loop(0, BS, body, 0, unroll=8)

    k0 = pl.program_id(2) * kg
    for i in range(kg):
        are = t_sc[pl.ds(k0 + i, BS, stride=pitch), :]
        aim = t_sc[pl.ds(k0 + i + NB // 2, BS, stride=pitch), :]
        a = jnp.concatenate([are, aim], axis=0).astype(BF16)
        o_ref[i] = _dot(g_ref[i], a)


def fft_fwd(u, col0, nbatch, nj, f1, gf, NB, BS, ct=128, kg=8):
    h = NB // 2
    kg = min(kg, h)
    return pl.pallas_call(
        functools.partial(_fft_fwd_kernel, NB=NB, BS=BS, nj=nj, kg=kg),
        out_shape=jax.ShapeDtypeStruct((nbatch, h, 2 * BS, HY_W), F32),
        grid=(nbatch, HY_W // ct, h // kg),
        in_specs=[pl.BlockSpec((nj * BS, ct), lambda b, c, k: (b, col0 // ct + c)),
                  pl.BlockSpec((NB, nj), lambda b, c, k: (0, 0)),
                  pl.BlockSpec((kg, 2 * BS, 2 * BS), lambda b, c, k: (k, 0, 0))],
        out_specs=pl.BlockSpec((None, kg, 2 * BS, ct), lambda b, c, k: (b, k, 0, c)),
        scratch_shapes=[pltpu.VMEM((BS * (NB + FFT_PAD), ct), F32)],
        compiler_params=_cp(("parallel", "parallel", "arbitrary")), name="hyena_fft_fwd")(u, f1, gf)


def _cmul(u, h, half):
    ure, uim = u[:half], u[half:]
    hre, him = h[:half], h[half:]
    return jnp.concatenate([ure * hre - uim * him, ure * him + uim * hre], axis=0)


def _fft_inv_kernel(us_ref, hs_ref, gi_ref, f1i_ref, o_ref, t_sc, *, NB, BS, kg):
    ks = pl.program_id(2)
    pitch = 2 * BS + FFT_PAD
    for i in range(kg):
        y = _cmul(us_ref[i], hs_ref[i], BS).astype(BF16)
        row = pl.multiple_of((ks * kg + i) * pitch, 8)
        t_sc[pl.ds(row, 2 * BS), :] = _dot(gi_ref[i], y)

    @pl.when(ks == pl.num_programs(2) - 1)
    def _():
        f1i = f1i_ref[...]

        def body(r, carry):
            bre = t_sc[pl.ds(r, NB // 2, stride=pitch), :]
            bim = t_sc[pl.ds(r + BS, NB // 2, stride=pitch), :]
            b = jnp.concatenate([bre, bim], axis=0).astype(BF16)
            o_ref[pl.ds(r, NB // 2, stride=BS), :] = _dot(f1i, b)
            return carry

        lax.fori_loop(0, BS, body, 0, unroll=8)


def fft_inv(us, hs, gi, f1i, NB, BS, ct=128, kg=8):
    nbatch, h = us.shape[0], NB // 2
    kg = min(kg, h)
    L = h * BS
    return pl.pallas_call(
        functools.partial(_fft_inv_kernel, NB=NB, BS=BS, kg=kg),
        out_shape=jax.ShapeDtypeStruct((nbatch * L, HY_W), F32),
        grid=(nbatch, HY_W // ct, h // kg),
        in_specs=[pl.BlockSpec((None, kg, 2 * BS, ct), lambda b, c, k: (b, k, 0, c)),
                  pl.BlockSpec((None, kg, 2 * BS, ct), lambda b, c, k: (0, k, 0, c)),
                  pl.BlockSpec((kg, 2 * BS, 2 * BS), lambda b, c, k: (k, 0, 0)),
                  pl.BlockSpec((h, NB), lambda b, c, k: (0, 0))],
        out_specs=pl.BlockSpec((L, ct), lambda b, c, k: (b, c)),
        scratch_shapes=[pltpu.VMEM((h * (2 * BS + FFT_PAD), ct), F32)],
        compiler_params=_cp(("parallel", "parallel", "arbitrary")), name="hyena_fft_inv")(us, hs, gi, f1i)


def _dft_consts(L):
    N = 2 * L
    k = jnp.arange(L, dtype=jnp.int32)
    n = jnp.arange(N, dtype=jnp.int32)
    re, im = _cis(n[None, :] * (2 * k[:, None] + 1), 2 * N)
    f = jnp.concatenate([re, im], axis=0)
    neg = jnp.where(n >= L, -1.0, 1.0)[None, :]
    fi = (2.0 / N) * jnp.concatenate([re[:, :L].T, im[:, :L].T], axis=1)
    return f[:, :L].astype(BF16), (f * neg).astype(BF16), fi.astype(BF16)


def _cdft_kernel(f_ref, x_ref, o_ref):
    o_ref[...] = _dot(f_ref[...], x_ref[...].astype(BF16))


def dft_fwd(x, f, row0, col0, nbatch, ct=256):
    M, K = f.shape
    return pl.pallas_call(
        _cdft_kernel, out_shape=jax.ShapeDtypeStruct((nbatch, M, HY_W), F32),
        grid=(nbatch, HY_W // ct),
        in_specs=[pl.BlockSpec((M, K), lambda b, c: (0, 0)),
                  pl.BlockSpec((K, ct), lambda b, c: (row0 // K + b, col0 // ct + c))],
        out_specs=pl.BlockSpec((None, M, ct), lambda b, c: (b, 0, c)),
        compiler_params=_cp(("parallel", "parallel")), name="hyena_dft_fwd")(f, x)


def _cdft_inv_kernel(us_ref, hs_ref, fi_ref, o_ref):
    half = us_ref.shape[0] // 2
    o_ref[...] = _dot(fi_ref[...], _cmul(us_ref[...], hs_ref[...], half).astype(BF16))


def dft_inv(us, hs, fi, ct=256):
    nbatch, M2, _ = us.shape
    L = fi.shape[0]
    return pl.pallas_call(
        _cdft_inv_kernel, out_shape=jax.ShapeDtypeStruct((nbatch * L, HY_W), F32),
        grid=(nbatch, HY_W // ct),
        in_specs=[pl.BlockSpec((None, M2, ct), lambda b, c: (b, 0, c)),
                  pl.BlockSpec((None, M2, ct), lambda b, c: (0, 0, c)),
                  pl.BlockSpec((L, M2), lambda b, c: (0, 0))],
        out_specs=pl.BlockSpec((L, ct), lambda b, c: (b, c)),
        compiler_params=_cp(("parallel", "parallel")), name="hyena_dft_inv")(us, hs, fi)


def _hy_gate_kernel(g_ref, y_ref, u_ref, ss_ref, b_ref, o_ref):
    scale = lax.rsqrt(ss_ref[0:1, :] + ss_ref[1:2, :] + 1e-6)
    o_ref[...] = g_ref[...] * (y_ref[...] * scale + u_ref[...] * b_ref[...])


def hy_gate(gate, gcol, grow, y, uin, ucol, urow, ss, order, bias, tm=256):
    M = y.shape[0]
    return pl.pallas_call(
        _hy_gate_kernel, out_shape=jax.ShapeDtypeStruct((M, HY_W), F32), grid=(M // tm,),
        in_specs=[pl.BlockSpec((tm, HY_W), lambda i: (grow // tm + i, gcol // HY_W)),
                  pl.BlockSpec((tm, HY_W), lambda i: (i, 0)),
                  pl.BlockSpec((tm, HY_W), lambda i: (urow // tm + i, ucol // HY_W)),
                  pl.BlockSpec((None, 2, HY_W), lambda i: (order, 0, 0)),
                  pl.BlockSpec((None, 1, HY_W), lambda i: (order, 0, 0))],
        out_specs=pl.BlockSpec((tm, HY_W), lambda i: (i, 0)),
        compiler_params=_cp(("parallel",)), name="hyena_gate")(gate, y, uin, ss, bias)


def hyena(u, hp):
    C = HY_W
    bias = hp['hy_bias'].reshape(2, 1, C)
    NB = BS = int(round(math.sqrt(2 * SEQ)))
    f1d, f1f, f1i, gf, gi = _fft_consts(NB, BS)
    taps, ss = hyena_filter_taps(SEQ, hp)
    ss = ss.reshape(2, 2, C).transpose(1, 0, 2)
    zin, zcol = u, 2 * C
    for order in range(2):
        hs = fft_fwd(taps, order * C, 1, NB, f1f, gf, NB, BS)
        us = fft_fwd(zin, zcol, BATCH, NB // 2, f1d, gf, NB, BS)
        y = fft_inv(us, hs, gi, f1i, NB, BS)
        zin = hy_gate(u, order * C, 0, y, zin, zcol, 0, ss, order, bias)
        zcol = 0
    z_lat = zin
    fd, ff, fi = _dft_consts(CTX)
    taps, ss = hyena_filter_taps(CTX, hp)
    ss = ss.reshape(2, 2, C).transpose(1, 0, 2)
    zin, zcol, zrow = u, 2 * C, N_LAT
    for order in range(2):
        hs = dft_fwd(taps, ff, 0, order * C, 1)
        us = dft_fwd(zin, fd, zrow, zcol, BATCH)
        y = dft_inv(us, hs, fi)
        zin = hy_gate(u, order * C, N_LAT, y, zin, zcol, zrow, ss, order, bias)
        zcol, zrow = 0, 0
    return jnp.concatenate([z_lat, zin], axis=0)


EV_SSD_IN = SSD_HEADS * SSD_P
EV_XBC = EV_SSD_IN + 2 * SSD_G * SSD_S
EV_PAD_N = 5760


def even_mixer(x, mods, g_pre, g_post, ep):
    o1 = EV_SSD_IN
    o2 = o1 + EV_XBC
    o3 = o2 + 2 * SSD_HEADS
    w = ep['w_in']
    n_in = w.shape[1]
    w_perm = jnp.concatenate([w[:, :o2], w[:, o3:], w[:, o2:o3],
                              jnp.zeros((D, EV_PAD_N - n_in), F32)], axis=1).astype(BF16)
    p = norm_mod_matmul(x, g_pre, mods, w_perm, 0, 1, tn=1920, name="even_in_proj")
    xbc = dwconv3(p, o1, EV_XBC, ep['ssd_conv_w'], ep['ssd_conv_b'], True, "ssd_conv")
    u = dwconv3(p, o2, 3 * HY_W, ep['hy_conv_w'], ep['hy_conv_b'], False, "hyena_conv")
    dt_raw = p[:, o2 + 3 * HY_W:o2 + 3 * HY_W + 2 * SSD_HEADS]
    y2 = ssd_scan(xbc, dt_raw, ep['ssd_dt_bias'], ep['ssd_a_log'])
    s = ssd_output(y2, xbc, p, ep['ssd_d'], ep['ssd_norm_w'])
    zh = hyena(u, ep)
    return outproj_residual(s, zh, ep['w_out'].astype(BF16), x, g_post, mods, 2, name="even_out_proj")


def _head_sum(x, e, et):
    return _dot2l(_dot2l(x, e), et)


def _rw_prep_kernel(r_ref, k_ref, v_ref, lo_ref, w0_ref, wup_ref, a0_ref, aup_ref, gup_ref, kk_ref, ka_ref,
                    rk_ref, e_ref, et_ref, lw_ref, kd_ref, be_ref, kap_ref, g_ref, bonus_ref):
    r, k, v = r_ref[...], k_ref[...], v_ref[...]
    lo = lo_ref[...]
    wc, ac, gc = lo[:, 0:64], lo[:, 64:128], lo[:, 128:384]
    e, et = e_ref[...], et_ref[...]
    kk = k * kk_ref[...]
    kap = kk * lax.rsqrt(_head_sum(kk * kk, e, et) + 1e-12)
    kap_ref[...] = kap
    g_ref[...] = _dot3(_sigmoid(gc), gup_ref[...])
    kd_sum = jnp.zeros_like(k)
    for d in range(2):
        wlog = -_softplus(-(w0_ref[d:d + 1, :] + _dot3(jnp.tanh(wc), wup_ref[d]))) - 0.5
        lw_ref[d] = -jnp.exp(wlog)
        a = _sigmoid(a0_ref[d:d + 1, :] + _dot3(ac, aup_ref[d]))
        kd = k * (1.0 + (a - 1.0) * ka_ref[...])
        kd_ref[d] = kd
        be_ref[d] = kap * a
        kd_sum = kd_sum + kd
    bonus_ref[...] = _head_sum(r * kd_sum * rk_ref[...], e, et) * v


def rwkv_prepare(code, lora, op, tm=256):
    W = RW_H * RW_N
    heads = jnp.arange(W, dtype=jnp.int32) // RW_N
    e = (heads[:, None] == jnp.arange(128, dtype=jnp.int32)[None, :]).astype(BF16)
    et = jnp.transpose(e)
    gup = jnp.pad(op['g_up'], ((0, 256 - op['g_up'].shape[0]), (0, 0)))
    row = lambda a: a.reshape(1, W)
    full2 = lambda shape: pl.BlockSpec(shape, lambda i: (0,) * len(shape))
    outs = pl.pallas_call(
        _rw_prep_kernel,
        out_shape=(jax.ShapeDtypeStruct((2, N_ROWS, W), F32), jax.ShapeDtypeStruct((2, N_ROWS, W), F32),
                   jax.ShapeDtypeStruct((2, N_ROWS, W), F32), jax.ShapeDtypeStruct((N_ROWS, W), F32),
                   jax.ShapeDtypeStruct((N_ROWS, W), F32), jax.ShapeDtypeStruct((N_ROWS, W), F32)),
        grid=(N_ROWS // tm,),
        in_specs=[pl.BlockSpec((tm, W), lambda i: (i, 0)), pl.BlockSpec((tm, W), lambda i: (i, 1)),
                  pl.BlockSpec((tm, W), lambda i: (i, 2)), pl.BlockSpec((tm, 384), lambda i: (i, 0)),
                  full2((2, W)), full2((2, 64, W)), full2((2, W)), full2((2, 64, W)), full2((256, W)),
                  full2((1, W)), full2((1, W)), full2((1, W)), full2((W, 128)), full2((128, W))],
        out_specs=(pl.BlockSpec((2, tm, W), lambda i: (0, i, 0)), pl.BlockSpec((2, tm, W), lambda i: (0, i, 0)),
                   pl.BlockSpec((2, tm, W), lambda i: (0, i, 0)), pl.BlockSpec((tm, W), lambda i: (i, 0)),
                   pl.BlockSpec((tm, W), lambda i: (i, 0)), pl.BlockSpec((tm, W), lambda i: (i, 0))),
        compiler_params=_cp(("parallel",)), name="rwkv_prepare",
    )(code, code, code, lora, op['w0'], op['w_up'], op['a0'], op['a_up'], gup, row(op['k_k']), row(op['k_a']),
      row(op['r_k']), e, et)
    return outs


def _tri_inv(n, eye, masks):
    bd = lambda a, b: _dot(a.astype(BF16), b.astype(BF16))
    d0 = jnp.where(masks[0], n, 0.0)
    d2 = bd(d0, d0)
    d4 = bd(d2, d2)
    t = bd(bd(eye + d0, eye + d2), eye + d4)
    for m in masks[1:]:
        e = jnp.where(m, n, 0.0)
        t = t + bd(t, bd(e, t))
    return t


def _rw_scan_kernel(r_ref, v_ref, lw_ref, kd_ref, be_ref, kap_ref, y_ref, st_ref):
    d = pl.program_id(0)
    c = pl.program_id(2)
    C = RW_CHUNK
    N = RW_N

    @pl.when(c == 0)
    def _():
        st_ref[...] = jnp.zeros_like(st_ref)

    isb = d == 1
    sgn = 1 - 2 * d
    ii = lax.broadcasted_iota(jnp.int32, (C, C), 0)
    jj = lax.broadcasted_iota(jnp.int32, (C, C), 1)
    dif = sgn * (ii - jj)
    incl = dif >= 0
    strict = dif > 0
    tri = incl.astype(BF16)
    eye = (ii == jj).astype(F32)
    blk = [(ii >> s) == (jj >> s) for s in range(3, C.bit_length() - 1)]
    masks = [blk[0]] + [blk[l] & ~blk[l - 1] for l in range(1, len(blk))] + [~blk[-1]]
    lw = lw_ref[...]
    cum = _dot2r(tri, lw)
    ec = jnp.exp(cum)
    en = jnp.exp(-cum)
    ea = jnp.exp(cum - lw)
    last = jnp.where(isb, cum[0:1, :], cum[C - 1:C, :])
    el = jnp.exp(last - cum)
    kap = kap_ref[...]
    r = r_ref[...]
    v = v_ref[...]
    a_t = -kap * ea
    r_t = r * ec
    b_t = be_ref[...] * en
    k_t = kd_ref[...] * en
    b_l = be_ref[...] * el
    k_l = kd_ref[...] * el
    pc = jnp.exp(last)
    H = range(RW_H)
    sl = [slice(h * N, (h + 1) * N) for h in H]
    bd = lambda a, b: _dot(a.astype(BF16), b.astype(BF16))
    tn = lambda a, b: lax.dot_general(a, b, (((0,), (0,)), ((), ())), preferred_element_type=F32)
    sc = [_dot_nt(jnp.concatenate([a_t[:, sl[h]], r_t[:, sl[h]]], axis=0).astype(BF16),
                  jnp.concatenate([b_t[:, sl[h]], k_t[:, sl[h]]], axis=0).astype(BF16)) for h in H]
    n_ab = [jnp.where(strict, sc[h][0:C, 0:C], 0.0) for h in H]
    a_ak = [jnp.where(strict, sc[h][0:C, C:2 * C], 0.0).astype(BF16) for h in H]
    m_rb = [jnp.where(incl, sc[h][C:2 * C, 0:C], 0.0).astype(BF16) for h in H]
    m_rk = [jnp.where(incl, sc[h][C:2 * C, C:2 * C], 0.0).astype(BF16) for h in H]
    vh = [v[:, sl[h]].astype(BF16) for h in H]
    d0 = [jnp.where(masks[0], n_ab[h], 0.0) for h in H]
    d2 = [bd(d0[h], d0[h]) for h in H]
    d4 = [bd(d2[h], d2[h]) for h in H]
    t = [bd(eye + d0[h], eye + d2[h]) for h in H]
    t = [bd(t[h], eye + d4[h]) for h in H]
    for m in masks[1:]:
        et = [bd(jnp.where(m, n_ab[h], 0.0), t[h]) for h in H]
        t = [t[h] + bd(t[h], et[h]) for h in H]
    av = [_dot(a_ak[h], vh[h]) for h in H]
    wub = [bd(t[h], jnp.concatenate([a_t[:, sl[h]], av[h]], axis=1)).astype(BF16) for h in H]
    mv = [_dot(m_rk[h], vh[h]) for h in H]
    kv = [tn(k_l[:, sl[h]].astype(BF16), vh[h]) for h in H]
    qy = [_dot(m_rb[h], wub[h]) + jnp.concatenate([r_t[:, sl[h]], mv[h]], axis=1) for h in H]
    pp = [tn(b_l[:, sl[h]].astype(BF16), wub[h]) + jnp.concatenate([eye[0:N, 0:N] * pc[:, sl[h]], kv[h]], axis=1)
          for h in H]
    h_old = [st_ref[h] for h in H]
    ys = [_dot3(qy[h][:, 0:N], h_old[h]) + qy[h][:, N:2 * N] for h in H]
    for h in H:
        st_ref[h] = _dot3(pp[h][:, 0:N], h_old[h]) + pp[h][:, N:2 * N]
    y_ref[...] = jnp.concatenate(ys, axis=1)


def _rw_rowblock(d, b, c):
    n_ctx = CTX // RW_CHUNK
    n_lat = SEQ // RW_CHUNK
    cc = jnp.where(d == 0, c, n_ctx - 1 - c)
    lc = jnp.where(d == 0, c - n_ctx, n_ctx + n_lat - 1 - c)
    return jnp.where(c < n_ctx, N_LAT // RW_CHUNK + b * n_ctx + cc, b * n_lat + lc)


def rwkv_scan(code, lw, kd, be, kap):
    W = RW_H * RW_N
    nch = (CTX + SEQ) // RW_CHUNK
    rb = lambda d, b, c: _rw_rowblock(d, b, c)
    return pl.pallas_call(
        _rw_scan_kernel,
        out_shape=jax.ShapeDtypeStruct((2, N_ROWS, W), F32),
        grid=(2, BATCH, nch),
        in_specs=[pl.BlockSpec((RW_CHUNK, W), lambda d, b, c: (rb(d, b, c), 0)),
                  pl.BlockSpec((RW_CHUNK, W), lambda d, b, c: (rb(d, b, c), 2)),
                  pl.BlockSpec((None, RW_CHUNK, W), lambda d, b, c: (d, rb(d, b, c), 0)),
                  pl.BlockSpec((None, RW_CHUNK, W), lambda d, b, c: (d, rb(d, b, c), 0)),
                  pl.BlockSpec((None, RW_CHUNK, W), lambda d, b, c: (d, rb(d, b, c), 0)),
                  pl.BlockSpec((RW_CHUNK, W), lambda d, b, c: (rb(d, b, c), 0))],
        out_specs=pl.BlockSpec((None, RW_CHUNK, W), lambda d, b, c: (d, rb(d, b, c), 0)),
        scratch_shapes=[pltpu.VMEM((RW_H, RW_N, RW_N), F32)],
        compiler_params=_cp(("parallel", "parallel", "arbitrary")), name="rwkv_scan",
    )(code, code, lw, kd, be, kap)


def _rw_out_kernel(yf_ref, yb_ref, bonus_ref, g_ref, lnw_ref, lnb_ref, e_ref, et_ref, o_ref):
    e, et = e_ref[...], et_ref[...]
    y = yf_ref[...] + yb_ref[...]
    mean = _head_sum(y, e, et) * (1.0 / RW_N)
    yc = y - mean
    var = _head_sum(yc * yc, e, et) * (1.0 / RW_N)
    yn = yc * lax.rsqrt(var + RW_GN_EPS) * lnw_ref[...] + lnb_ref[...]
    o_ref[...] = (yn + bonus_ref[...]) * g_ref[...]


def rwkv_output(y2, bonus, g, op, tm=256):
    W = RW_H * RW_N
    heads = jnp.arange(W, dtype=jnp.int32) // RW_N
    e = (heads[:, None] == jnp.arange(128, dtype=jnp.int32)[None, :]).astype(BF16)
    et = jnp.transpose(e)
    M = N_LAT
    return pl.pallas_call(
        _rw_out_kernel, out_shape=jax.ShapeDtypeStruct((M, W), F32), grid=(M // tm,),
        in_specs=[pl.BlockSpec((None, tm, W), lambda i: (0, i, 0)), pl.BlockSpec((None, tm, W), lambda i: (1, i, 0)),
                  pl.BlockSpec((tm, W), lambda i: (i, 0)), pl.BlockSpec((tm, W), lambda i: (i, 0)),
                  pl.BlockSpec((1, W), lambda i: (0, 0)), pl.BlockSpec((1, W), lambda i: (0, 0)),
                  pl.BlockSpec((W, 128), lambda i: (0, 0)), pl.BlockSpec((128, W), lambda i: (0, 0))],
        out_specs=pl.BlockSpec((tm, W), lambda i: (i, 0)),
        compiler_params=_cp(("parallel",)), name="rwkv_output",
    )(y2, y2, bonus, g, op['ln_w'].reshape(1, W), op['ln_b'].reshape(1, W), e, et)


AT_Q = RW_H * AT_HD
AT_KW = AT_KV * AT_HD
AT_TQ = 512
AT_TK = 768


def _rope_tables(tm):
    half = AT_HD // 2
    inv = 10000.0 ** (-jnp.arange(0, half, 2, dtype=F32) / half)
    pos = jnp.arange(SEQ, dtype=jnp.int32)
    row = (pos // GRID_W).astype(F32)[:, None] * inv
    col = (pos % GRID_W).astype(F32)[:, None] * inv
    cos_h = jnp.concatenate([jnp.cos(row), jnp.cos(row), jnp.cos(col), jnp.cos(col)], axis=1)
    sin_h = jnp.concatenate([-jnp.sin(row), jnp.sin(row), -jnp.sin(col), jnp.sin(col)], axis=1)
    cos_t = jnp.concatenate([jnp.tile(cos_h, (1, 2)), jnp.ones((tm, 128), F32)], axis=0)
    sin_t = jnp.concatenate([jnp.tile(sin_h, (1, 2)), jnp.zeros((tm, 128), F32)], axis=0)
    return cos_t, sin_t


def _rot_partner(x):
    q = AT_HD // 4
    w = x.shape[1]
    lane = lax.broadcasted_iota(jnp.int32, x.shape, 1)
    return jnp.where((lane % (2 * q)) < q, pltpu.roll(x, w - q, 1), pltpu.roll(x, q, 1))


def _at_prep_kernel(q_ref, k_ref, v_ref, cos_ref, sin_ref, qn_ref, kn_ref, e_ref, et_ref, qo_ref, ko_ref, vo_ref):
    e, et = e_ref[...], et_ref[...]
    cos2, sin2 = cos_ref[...], sin_ref[...]

    def norm_rope(x, gain, nrep):
        ms = _head_sum(x * x, e[:x.shape[1]], et[:, :x.shape[1]]) * (1.0 / AT_HD)
        xn = x * lax.rsqrt(ms + EPS) * gain
        cos = jnp.tile(cos2, (1, nrep))
        sin = jnp.tile(sin2, (1, nrep))
        return xn * cos + _rot_partner(xn) * sin

    qn = norm_rope(q_ref[...], qn_ref[...], AT_Q // 128) * (AT_HD ** -0.5 * math.log2(math.e))
    qo_ref[...] = jnp.transpose(qn).astype(BF16)
    ko_ref[...] = norm_rope(k_ref[...], kn_ref[...], AT_KW // 128).astype(BF16)
    vo_ref[...] = jnp.transpose(v_ref[...]).astype(BF16)


def attention_prepare(p, q_norm, k_norm, tm=256):
    cos_t, sin_t = _rope_tables(tm)
    heads = jnp.arange(AT_Q, dtype=jnp.int32) // AT_HD
    e = (heads[:, None] == jnp.arange(128, dtype=jnp.int32)[None, :]).astype(BF16)
    et = jnp.transpose(e)
    tab = lambda i: jnp.where(i * tm < N_LAT, ((i * tm) % SEQ) // tm, SEQ // tm)
    n_lat_t, n_seq_t, n_ctx_t = N_LAT // tm, SEQ // tm, CTX // tm
    kvb = lambda i: jnp.where(i < n_lat_t, (i // n_seq_t) * (n_seq_t + n_ctx_t) + n_ctx_t + i % n_seq_t,
                              ((i - n_lat_t) // n_ctx_t) * (n_seq_t + n_ctx_t) + (i - n_lat_t) % n_ctx_t)
    qcol = (3 * RW_H * RW_N) // AT_Q
    kcol = (3 * RW_H * RW_N + AT_Q) // AT_KW
    return pl.pallas_call(
        _at_prep_kernel,
        out_shape=(jax.ShapeDtypeStruct((AT_Q, N_ROWS), BF16), jax.ShapeDtypeStruct((N_ROWS, AT_KW), BF16),
                   jax.ShapeDtypeStruct((AT_KW, N_ROWS), BF16)),
        grid=(N_ROWS // tm,),
        in_specs=[pl.BlockSpec((tm, AT_Q), lambda i: (i, qcol)),
                  pl.BlockSpec((tm, AT_KW), lambda i: (i, kcol)),
                  pl.BlockSpec((tm, AT_KW), lambda i: (i, kcol + 1)),
                  pl.BlockSpec((tm, 128), lambda i: (tab(i), 0)),
                  pl.BlockSpec((tm, 128), lambda i: (tab(i), 0)),
                  pl.BlockSpec((1, AT_Q), lambda i: (0, 0)),
                  pl.BlockSpec((1, AT_KW), lambda i: (0, 0)),
                  pl.BlockSpec((AT_Q, 128), lambda i: (0, 0)),
                  pl.BlockSpec((128, AT_Q), lambda i: (0, 0))],
        out_specs=(pl.BlockSpec((AT_Q, tm), lambda i: (0, i)), pl.BlockSpec((tm, AT_KW), lambda i: (kvb(i), 0)),
                   pl.BlockSpec((AT_KW, tm), lambda i: (0, kvb(i)))),
        compiler_params=_cp(("parallel",)), name="attn_prepare",
    )(p, p, p, cos_t, sin_t, jnp.tile(q_norm, AT_Q // AT_HD).reshape(1, AT_Q),
      jnp.tile(k_norm, AT_KV).reshape(1, AT_KW), e, et)


def _flash_t_kernel(qt_ref, k_ref, vt_ref, o_ref, m_sc, l_sc, acc_sc):
    ki = pl.program_id(2)
    nq = AT_Q // AT_HD
    gq = nq // AT_KV

    @pl.when(ki == 0)
    def _():
        m_sc[...] = jnp.full_like(m_sc, -1e30)
        l_sc[...] = jnp.zeros_like(l_sc)
        acc_sc[...] = jnp.zeros_like(acc_sc)

    for g in range(AT_KV):
        kg = k_ref[:, g * AT_HD:(g + 1) * AT_HD]
        vtg1 = jnp.concatenate([vt_ref[g * AT_HD:(g + 1) * AT_HD, :],
                                jnp.ones((16, vt_ref.shape[1]), BF16)], axis=0)
        hs = range(g * gq, (g + 1) * gq)
        st = [_dot(kg, qt_ref[h * AT_HD:(h + 1) * AT_HD, :]) for h in hs]
        m_old = [m_sc[h] for h in hs]
        m_new = [jnp.maximum(m_old[i], jnp.max(st[i], axis=0, keepdims=True)) for i in range(gq)]
        alpha = [jnp.exp2(m_old[i] - m_new[i]) for i in range(gq)]
        pt = [jnp.exp2(st[i] - m_new[i]) for i in range(gq)]
        pv = [_dot(vtg1, pt[i].astype(BF16)) for i in range(gq)]
        for i, h in enumerate(hs):
            l_sc[h] = alpha[i] * l_sc[h] + pv[i][AT_HD:AT_HD + 1, :]
            m_sc[h] = m_new[i]
            rows = pl.ds(h * AT_HD, AT_HD)
            acc_sc[rows, :] = alpha[i] * acc_sc[rows, :] + pv[i][0:AT_HD, :]

    @pl.when(ki == pl.num_programs(2) - 1)
    def _():
        inv = jnp.concatenate([jnp.broadcast_to(1.0 / l_sc[h], (AT_HD, l_sc.shape[2])) for h in range(nq)], axis=0)
        o_ref[...] = jnp.transpose(acc_sc[...] * inv)


def flash_attention_t(qt, k, vt):
    nq = AT_Q // AT_HD
    nk = (CTX + SEQ) // AT_TK
    kv_rb = lambda b, ki: b * nk + ki
    return pl.pallas_call(
        _flash_t_kernel,
        out_shape=jax.ShapeDtypeStruct((N_LAT, AT_Q), F32),
        grid=(BATCH, SEQ // AT_TQ, nk),
        in_specs=[pl.BlockSpec((AT_Q, AT_TQ), lambda b, qi, ki: (0, b * (SEQ // AT_TQ) + qi)),
                  pl.BlockSpec((AT_TK, AT_KW), lambda b, qi, ki: (kv_rb(b, ki), 0)),
                  pl.BlockSpec((AT_KW, AT_TK), lambda b, qi, ki: (0, kv_rb(b, ki)))],
        out_specs=pl.BlockSpec((AT_TQ, AT_Q), lambda b, qi, ki: (b * (SEQ // AT_TQ) + qi, 0)),
        scratch_shapes=[pltpu.VMEM((nq, 1, AT_TQ), F32), pltpu.VMEM((nq, 1, AT_TQ), F32),
                        pltpu.VMEM((AT_Q, AT_TQ), F32)],
        compiler_params=_cp(("parallel", "parallel", "arbitrary")), name="flash_attention")(qt, k, vt)


OD_PAD_N = 4992


def odd_mixer(x, mods, g_pre, g_post, op):
    W = RW_H * RW_N
    w = op['w_in']
    c3 = 3 * W
    code_n = c3 + 64 + 64 + 160
    w_perm = jnp.concatenate([w[:, :c3], w[:, code_n:], w[:, c3:code_n],
                              jnp.zeros((D, OD_PAD_N - w.shape[1]), F32)], axis=1).astype(BF16)
    p = norm_mod_matmul(x, g_pre, mods, w_perm, 0, 1, tn=1664, name="odd_in_proj")
    mu = op['mu']
    taps = lambda m: jnp.stack([0.5 * m, 1.0 - m, 0.5 * m], axis=1)
    code = dwconv3(p, 0, c3, taps(mu[:c3]), jnp.zeros((c3,), F32), False, "rwkv_shift")
    lo_col = c3 + AT_Q + 2 * AT_KW
    mu_lo = jnp.pad(mu[c3:], (0, 384 - (code_n - c3)))
    lora = dwconv3(p, lo_col, 384, taps(mu_lo), jnp.zeros((384,), F32), False, "rwkv_shift_lora", cb=128)
    lw, kd, be, kap, g, bonus = rwkv_prepare(code, lora, op)
    y2 = rwkv_scan(code, lw, kd, be, kap)
    o_l = rwkv_output(y2, bonus, g, op)
    q, k, v = attention_prepare(p, op['q_norm'], op['k_norm'])
    a_l = flash_attention_t(q, k, v)
    return outproj_residual(o_l, a_l, op['w_out'].astype(BF16), x[:N_LAT], g_post, mods, 2, name="odd_out_proj")


def _router_kernel(x_ref, g_ref, mod_ref, rw_ref, rb_ref, s1_ref, s3_ref, s2_ref, t_ref, idx_ref, wt_ref, sh_ref):
    x = x_ref[...]
    ms = jnp.mean(x * x, axis=-1, keepdims=True)
    t = x * lax.rsqrt(ms + EPS) * g_ref[...] * (1.0 + mod_ref[4:5, :]) + mod_ref[3:4, :]
    t_ref[...] = t
    tb = t.astype(BF16)
    sh_ref[...] = _dot((_silu(_dot(tb, s1_ref[...])) * _dot(tb, s3_ref[...])).astype(BF16), s2_ref[...])
    th, tl = _split(t)
    wh, wl = _split(rw_ref[...])
    lg = _dot_nt(wh, th) + (_dot_nt(wh, tl) + _dot_nt(wl, th))
    sc = _sigmoid(lg)
    sel = sc + rb_ref[...]
    tm = sel.shape[1]
    gsz = N_EXP // N_GRP
    ninf = -jnp.inf
    sel3 = sel.reshape(N_GRP, gsz, tm)
    i3 = lax.broadcasted_iota(jnp.int32, sel3.shape, 1)
    m1 = jnp.max(sel3, axis=1, keepdims=True)
    first = jnp.min(jnp.where(sel3 == m1, i3, gsz), axis=1, keepdims=True)
    m2 = jnp.max(jnp.where(i3 == first, ninf, sel3), axis=1, keepdims=True)
    grp = (m1 + m2).reshape(N_GRP, tm)
    gi = lax.broadcasted_iota(jnp.int32, grp.shape, 0)
    gmask = jnp.zeros(grp.shape, F32)
    for _ in range(TOPK_GRP):
        m = jnp.max(grp, axis=0, keepdims=True)
        pick = jnp.min(jnp.where(grp == m, gi, N_GRP), axis=0, keepdims=True)
        hit = gi == pick
        gmask = jnp.where(hit, 1.0, gmask)
        grp = jnp.where(hit, ninf, grp)
    emask = jnp.broadcast_to(gmask.reshape(N_GRP, 1, tm), (N_GRP, gsz, tm)).reshape(N_EXP, tm)
    msel = jnp.where(emask > 0.5, sel, ninf)
    ei = lax.broadcasted_iota(jnp.int32, msel.shape, 0)
    idxs, ws = [], []
    for _ in range(TOP_K):
        m = jnp.max(msel, axis=0, keepdims=True)
        pick = jnp.min(jnp.where(msel == m, ei, N_EXP), axis=0, keepdims=True)
        hit = ei == pick
        idxs.append(pick)
        ws.append(jnp.sum(jnp.where(hit, sc, 0.0), axis=0, keepdims=True))
        msel = jnp.where(hit, ninf, msel)
    w = jnp.concatenate(ws, axis=0)
    idx_ref[...] = jnp.concatenate(idxs, axis=0)
    wt_ref[...] = w / jnp.sum(w, axis=0, keepdims=True) * ROUTED_SCALE


def moe_router(x, M, g, mods, mp, tm=256):
    full = lambda shape: pl.BlockSpec(shape, lambda i: (0,) * len(shape))
    return pl.pallas_call(
        _router_kernel,
        out_shape=(jax.ShapeDtypeStruct((M, D), F32), jax.ShapeDtypeStruct((TOP_K, M), jnp.int32),
                   jax.ShapeDtypeStruct((TOP_K, M), F32), jax.ShapeDtypeStruct((M, D), F32)),
        grid=(M // tm,),
        in_specs=[pl.BlockSpec((tm, D), lambda i: (i, 0)), full((1, D)),
                  pl.BlockSpec((None, 6, D), lambda i: (_seq_of_rowblock(i, tm), 0, 0)),
                  full((N_EXP, D)), full((N_EXP, 1)), full((D, EXP_FF)), full((D, EXP_FF)), full((EXP_FF, D))],
        out_specs=(pl.BlockSpec((tm, D), lambda i: (i, 0)), pl.BlockSpec((TOP_K, tm), lambda i: (0, i)),
                   pl.BlockSpec((TOP_K, tm), lambda i: (0, i)), pl.BlockSpec((tm, D), lambda i: (i, 0))),
        compiler_params=_cp(("parallel",)), name="moe_router",
    )(x, g.reshape(1, D), mods, jnp.transpose(mp['router_w']), mp['router_bias'].reshape(N_EXP, 1),
      mp['s1'].astype(BF16), mp['s3'].astype(BF16), mp['s2'].astype(BF16))


def _gather_rows(idx_ref, n, src_hbm, dst, sem, slot):
    def body(r, carry):
        pltpu.make_async_copy(src_hbm.at[pl.ds(idx_ref[0, r], 1)], dst.at[slot, pl.ds(r, 1)], sem.at[slot]).start()
        return carry

    lax.fori_loop(0, n, body, 0)


def _wait_rows(n, src_hbm, dst, sem, slot):
    pltpu.make_async_copy(src_hbm.at[pl.ds(0, n)], dst.at[slot], sem.at[slot]).wait()


def _expert_kernel(be_ref, tok_ref, tokn_ref, w_ref, t_hbm, w1_ref, w3_ref, w2_ref, o_ref, xbuf, sem):
    i = pl.program_id(0)
    n = pl.num_programs(0)
    slot = i % 2

    @pl.when(i == 0)
    def _():
        _gather_rows(tok_ref, MOE_BLK, t_hbm, xbuf, sem, 0)

    @pl.when(i + 1 < n)
    def _():
        _gather_rows(tokn_ref, MOE_BLK, t_hbm, xbuf, sem, 1 - slot)

    _wait_rows(MOE_BLK, t_hbm, xbuf, sem, slot)
    xb = xbuf[slot].astype(BF16)
    h = _silu(_dot(xb, w1_ref[...])) * _dot(xb, w3_ref[...])
    o_ref[...] = _dot(h.astype(BF16), w2_ref[...]) * w_ref[...]


def moe_experts(t, buf_tok, buf_w, block_e, w1, w3, w2):
    nb = block_e.shape[0]
    tok3 = buf_tok.reshape(nb, 1, MOE_BLK)
    nxt = lambda i, be: (jnp.minimum(i + 1, nb - 1), 0, 0)
    grid_spec = pltpu.PrefetchScalarGridSpec(
        num_scalar_prefetch=1, grid=(nb,),
        in_specs=[pl.BlockSpec((None, 1, MOE_BLK), lambda i, be: (i, 0, 0), memory_space=pltpu.SMEM),
                  pl.BlockSpec((None, 1, MOE_BLK), nxt, memory_space=pltpu.SMEM),
                  pl.BlockSpec((MOE_BLK, 1), lambda i, be: (i, 0)),
                  pl.BlockSpec(memory_space=pl.ANY),
                  pl.BlockSpec((None, D, EXP_FF), lambda i, be: (be[i], 0, 0)),
                  pl.BlockSpec((None, D, EXP_FF), lambda i, be: (be[i], 0, 0)),
                  pl.BlockSpec((None, EXP_FF, D), lambda i, be: (be[i], 0, 0))],
        out_specs=pl.BlockSpec((MOE_BLK, D), lambda i, be: (i, 0)),
        scratch_shapes=[pltpu.VMEM((2, MOE_BLK, D), F32), pltpu.SemaphoreType.DMA((2,))])
    return pl.pallas_call(
        _expert_kernel, out_shape=jax.ShapeDtypeStruct((nb * MOE_BLK, D), F32), grid_spec=grid_spec,
        compiler_params=_cp(("arbitrary",)), name="moe_experts",
    )(block_e, tok3, tok3, buf_w.reshape(nb * MOE_BLK, 1), t, w1, w3, w2)


MOE_TT = 32


def _combine_kernel(dst_ref, dstn_ref, ys_hbm, sh_ref, x_ref, g_ref, mod_ref, o_ref, buf, sem):
    i = pl.program_id(0)
    n = pl.num_programs(0)
    slot = i % 2
    nrow = MOE_TT * TOP_K

    @pl.when(i == 0)
    def _():
        _gather_rows(dst_ref, nrow, ys_hbm, buf, sem, 0)

    @pl.when(i + 1 < n)
    def _():
        _gather_rows(dstn_ref, nrow, ys_hbm, buf, sem, 1 - slot)

    _wait_rows(nrow, ys_hbm, buf, sem, slot)
    f = sh_ref[...]
    for k in range(TOP_K):
        f = f + buf[slot, k * MOE_TT:(k + 1) * MOE_TT, :]
    ms = jnp.mean(f * f, axis=-1, keepdims=True)
    o_ref[...] = x_ref[...] + mod_ref[5:6, :] * (f * lax.rsqrt(ms + EPS) * g_ref[...])


def moe_combine(ys, dest, sh, x, M, g, mods):
    nt = M // MOE_TT
    nrow = MOE_TT * TOP_K
    d3 = dest.reshape(nt, MOE_TT, TOP_K).transpose(0, 2, 1).reshape(nt, 1, nrow)
    nxt = lambda i: (jnp.minimum(i + 1, nt - 1), 0, 0)
    return pl.pallas_call(
        _combine_kernel, out_shape=jax.ShapeDtypeStruct((M, D), F32), grid=(nt,),
        in_specs=[pl.BlockSpec((None, 1, nrow), lambda i: (i, 0, 0), memory_space=pltpu.SMEM),
                  pl.BlockSpec((None, 1, nrow), nxt, memory_space=pltpu.SMEM),
                  pl.BlockSpec(memory_space=pl.ANY),
                  pl.BlockSpec((MOE_TT, D), lambda i: (i, 0)),
                  pl.BlockSpec((MOE_TT, D), lambda i: (i, 0)),
                  pl.BlockSpec((1, D), lambda i: (0, 0)),
                  pl.BlockSpec((None, 6, D), lambda i: (_seq_of_rowblock(i, MOE_TT), 0, 0))],
        out_specs=pl.BlockSpec((MOE_TT, D), lambda i: (i, 0)),
        scratch_shapes=[pltpu.VMEM((2, nrow, D), F32), pltpu.SemaphoreType.DMA((2,))],
        compiler_params=_cp(("arbitrary",)), name="moe_combine",
    )(d3, d3, ys, sh, x, g.reshape(1, D), mods)


def moe_layer(x, M, g_pre, g_post, mods, mp):
    t, idx_t, wts_t, sh = moe_router(x, M, g_pre, mods, mp)
    mk = M * TOP_K
    nb = -(-(mk + N_EXP * (MOE_BLK - 1)) // MOE_BLK)
    flat_e = jnp.transpose(idx_t).reshape(mk)
    flat_w = jnp.transpose(wts_t).reshape(mk)
    onehot = (flat_e[:, None] == jnp.arange(N_EXP, dtype=jnp.int32)[None, :]).astype(jnp.int32)
    csum = jnp.cumsum(onehot, axis=0)
    rank = jnp.take_along_axis(csum, flat_e[:, None], axis=1)[:, 0] - 1
    counts = csum[-1]
    padded = (counts + MOE_BLK - 1) // MOE_BLK * MOE_BLK
    pend = jnp.cumsum(padded)
    dest = (pend - padded)[flat_e] + rank
    flat_tok = jnp.arange(mk, dtype=jnp.int32) // TOP_K
    buf_tok = jnp.zeros((nb * MOE_BLK,), jnp.int32).at[dest].set(flat_tok)
    buf_w = jnp.zeros((nb * MOE_BLK,), F32).at[dest].set(flat_w)
    block_e = jnp.minimum(jnp.searchsorted(pend, jnp.arange(nb, dtype=jnp.int32) * MOE_BLK, side='right'),
                          N_EXP - 1).astype(jnp.int32)
    ys = moe_experts(t, buf_tok, buf_w, block_e, mp['w1'].astype(BF16), mp['w3'].astype(BF16),
                     mp['w2'].astype(BF16))
    return moe_combine(ys, dest.astype(jnp.int32), sh, x, M, g_post, mods)


MOE_T = 512
MOE_CAP = 128
MOE_EPS = 4


def _router2_kernel(x_ref, g_ref, mod_ref, rw_ref, rb_ref, s1_ref, s3_ref, s2_ref, t_ref, wt_ref, cnt_ref, sh_ref):
    x = x_ref[...]
    ms = jnp.mean(x * x, axis=-1, keepdims=True)
    t = x * lax.rsqrt(ms + EPS) * g_ref[...] * (1.0 + mod_ref[4:5, :]) + mod_ref[3:4, :]
    tb = t.astype(BF16)
    t_ref[...] = tb
    sh_ref[...] = _dot((_silu(_dot(tb, s1_ref[...])) * _dot(tb, s3_ref[...])).astype(BF16), s2_ref[...])
    th, tl = _split(t)
    wh, wl = _split(rw_ref[...])
    lg = _dot_nt(wh, th) + (_dot_nt(wh, tl) + _dot_nt(wl, th))
    sc = _sigmoid(lg)
    sel = sc + rb_ref[...]
    tm = sel.shape[1]
    gsz = N_EXP // N_GRP
    ninf = -jnp.inf
    sel3 = sel.reshape(N_GRP, gsz, tm)
    i3 = lax.broadcasted_iota(jnp.int32, sel3.shape, 1)
    m1 = jnp.max(sel3, axis=1, keepdims=True)
    first = jnp.min(jnp.where(sel3 == m1, i3, gsz), axis=1, keepdims=True)
    m2 = jnp.max(jnp.where(i3 == first, ninf, sel3), axis=1, keepdims=True)
    grp = (m1 + m2).reshape(N_GRP, tm)
    gi = lax.broadcasted_iota(jnp.int32, grp.shape, 0)
    gmask = jnp.zeros(grp.shape, F32)
    for _ in range(TOPK_GRP):
        m = jnp.max(grp, axis=0, keepdims=True)
        pick = jnp.min(jnp.where(grp == m, gi, N_GRP), axis=0, keepdims=True)
        hit = gi == pick
        gmask = jnp.where(hit, 1.0, gmask)
        grp = jnp.where(hit, ninf, grp)
    emask = jnp.broadcast_to(gmask.reshape(N_GRP, 1, tm), (N_GRP, gsz, tm)).reshape(N_EXP, tm)
    msel = jnp.where(emask > 0.5, sel, ninf)
    ei = lax.broadcasted_iota(jnp.int32, msel.shape, 0)
    chosen = jnp.zeros(msel.shape, F32)
    for _ in range(TOP_K):
        m = jnp.max(msel, axis=0, keepdims=True)
        pick = jnp.min(jnp.where(msel == m, ei, N_EXP), axis=0, keepdims=True)
        hit = ei == pick
        chosen = jnp.where(hit, 1.0, chosen)
        msel = jnp.where(hit, ninf, msel)
    w = chosen * sc
    wt = w / jnp.sum(w, axis=0, keepdims=True) * ROUTED_SCALE
    wt_ref[...] = wt
    cnt_ref[...] = jnp.sum((wt > 0.0).astype(F32), axis=1, keepdims=True).astype(jnp.int32)


def moe_router2(x, M, g, mods, mp):
    tm = MOE_T
    full = lambda shape: pl.BlockSpec(shape, lambda i: (0,) * len(shape))
    return pl.pallas_call(
        _router2_kernel,
        out_shape=(jax.ShapeDtypeStruct((M, D), BF16), jax.ShapeDtypeStruct((M // tm, N_EXP, tm), F32),
                   jax.ShapeDtypeStruct((M // tm, N_EXP, 1), jnp.int32), jax.ShapeDtypeStruct((M, D), F32)),
        grid=(M // tm,),
        in_specs=[pl.BlockSpec((tm, D), lambda i: (i, 0)), full((1, D)),
                  pl.BlockSpec((None, 6, D), lambda i: (_seq_of_rowblock(i, tm), 0, 0)),
                  full((N_EXP, D)), full((N_EXP, 1)), full((D, EXP_FF)), full((D, EXP_FF)), full((EXP_FF, D))],
        out_specs=(pl.BlockSpec((tm, D), lambda i: (i, 0)), pl.BlockSpec((None, N_EXP, tm), lambda i: (i, 0, 0)),
                   pl.BlockSpec((None, N_EXP, 1), lambda i: (i, 0, 0)), pl.BlockSpec((tm, D), lambda i: (i, 0))),
        compiler_params=_cp(("parallel",)), name="moe_router",
    )(x, g.reshape(1, D), mods, jnp.transpose(mp['router_w']), mp['router_bias'].reshape(N_EXP, 1),
      mp['s1'].astype(BF16), mp['s3'].astype(BF16), mp['s2'].astype(BF16))


def _moe2_kernel(cnt_ref, t_ref, wt_ref, sh_ref, x_ref, g_ref, mod_ref, w1_ref, w3_ref, w2_ref, o_ref,
                 rank_sc, acc_sc):
    i = pl.program_id(0)
    eb = pl.program_id(1)
    T = MOE_T

    @pl.when(eb == 0)
    def _():
        picked = (wt_ref[...] > 0.0).astype(BF16)
        before = (lax.broadcasted_iota(jnp.int32, (T, T), 0) < lax.broadcasted_iota(jnp.int32, (T, T), 1))
        rank_sc[...] = _dot(picked, before.astype(BF16))
        acc_sc[...] = jnp.zeros_like(acc_sc)

    slot = lax.broadcasted_iota(jnp.int32, (MOE_CAP, T), 0).astype(F32)
    for j in range(MOE_EPS):
        e = eb * MOE_EPS + j
        n_tok = cnt_ref[i * N_EXP + e]
        w_row = wt_ref[pl.ds(e, 1), :]
        r_row = rank_sc[pl.ds(e, 1), :]

        def chunk(ci, carry, j=j, w_row=w_row, r_row=r_row):
            hit = ((r_row - (ci * MOE_CAP).astype(F32)) == slot) & (w_row > 0.0)
            pb = hit.astype(F32).astype(BF16)
            xg = _dot(pb, t_ref[...]).astype(BF16)
            h = _silu(_dot(xg, w1_ref[j])) * _dot(xg, w3_ref[j])
            y = _dot(h.astype(BF16), w2_ref[j])
            w_slot = jnp.sum(jnp.where(hit, w_row, 0.0), axis=1, keepdims=True)
            yw = (y * w_slot).astype(BF16)
            acc_sc[...] += lax.dot_general(pb, yw, (((0,), (0,)), ((), ())), preferred_element_type=F32)
            return carry

        lax.fori_loop(0, (n_tok + MOE_CAP - 1) // MOE_CAP, chunk, 0)

    @pl.when(eb == pl.num_programs(1) - 1)
    def _():
        f = acc_sc[...] + sh_ref[...]
        ms = jnp.mean(f * f, axis=-1, keepdims=True)
        o_ref[...] = x_ref[...] + mod_ref[5:6, :] * (f * lax.rsqrt(ms + EPS) * g_ref[...])


def moe_layer2(x, M, g_pre, g_post, mods, mp):
    t, wt, cnt, sh = moe_router2(x, M, g_pre, mods, mp)
    T = MOE_T
    grid_spec = pltpu.PrefetchScalarGridSpec(
        num_scalar_prefetch=1, grid=(M // T, N_EXP // MOE_EPS),
        in_specs=[pl.BlockSpec((T, D), lambda i, e, c: (i, 0)),
                  pl.BlockSpec((N_EXP, T), lambda i, e, c: (0, i)),
                  pl.BlockSpec((T, D), lambda i, e, c: (i, 0)),
                  pl.BlockSpec((T, D), lambda i, e, c: (i, 0)),
                  pl.BlockSpec((1, D), lambda i, e, c: (0, 0)),
                  pl.BlockSpec((None, 6, D), lambda i, e, c: (_seq_of_rowblock(i, T), 0, 0)),
                  pl.BlockSpec((MOE_EPS, D, EXP_FF), lambda i, e, c: (e, 0, 0)),
                  pl.BlockSpec((MOE_EPS, D, EXP_FF), lambda i, e, c: (e, 0, 0)),
                  pl.BlockSpec((MOE_EPS, EXP_FF, D), lambda i, e, c: (e, 0, 0))],
        out_specs=pl.BlockSpec((T, D), lambda i, e, c: (i, 0)),
        scratch_shapes=[pltpu.VMEM((N_EXP, T), F32), pltpu.VMEM((T, D), F32)])
    return pl.pallas_call(
        _moe2_kernel, out_shape=jax.ShapeDtypeStruct((M, D), F32), grid_spec=grid_spec,
        compiler_params=_cp(("parallel", "arbitrary")), name="moe_experts",
    )(cnt.reshape(-1), t, wt, sh, x, g_post.reshape(1, D), mods, mp['w1'].astype(BF16), mp['w3'].astype(BF16),
      mp['w2'].astype(BF16))


def _moe3_kernel(cnt_ref, t_ref, wt_ref, w1_ref, w3_ref, w2_ref, o_ref, rank_sc, *, nsub):
    i = pl.program_id(0)
    eb = pl.program_id(1)
    T, CAP, EPS = MOE_T, MOE_CAP, MOE_EPS

    @pl.when(eb == 0)
    def _():
        before = (lax.broadcasted_iota(jnp.int32, (T, T), 0) < lax.broadcasted_iota(jnp.int32, (T, T), 1))
        before = before.astype(BF16)
        for s in range(nsub):
            rank_sc[s] = _dot((wt_ref[s] > 0.0).astype(BF16), before)
        o_ref[...] = jnp.zeros_like(o_ref)

    slot = lax.broadcasted_iota(jnp.int32, (CAP, T), 0).astype(F32)

    def one_hot(s, e, first_slot):
        w_row = wt_ref[s, pl.ds(e, 1), :]
        r_row = rank_sc[s, pl.ds(e, 1), :]
        hit = ((r_row - first_slot) == slot) & (w_row > 0.0)
        w_slot = jnp.sum(jnp.where(hit, w_row, 0.0), axis=1, keepdims=True)
        return hit.astype(F32).astype(BF16), w_slot

    def swiglu(xg, j):
        h = _silu(_dot(xg, w1_ref[j])) * _dot(xg, w3_ref[j])
        return _dot(h.astype(BF16), w2_ref[j])

    hot = [[one_hot(s, eb * EPS + j, 0.0) for j in range(EPS)] for s in range(nsub)]
    pb = [jnp.concatenate([hot[s][j][0] for j in range(EPS)], axis=0) for s in range(nsub)]
    xg = [_dot(pb[s], t_ref[s * T:(s + 1) * T, :]).astype(BF16) for s in range(nsub)]
    y = [swiglu(jnp.concatenate([xg[s][j * CAP:(j + 1) * CAP] for s in range(nsub)], axis=0), j)
         for j in range(EPS)]
    for s in range(nsub):
        yw = jnp.concatenate([y[j][s * CAP:(s + 1) * CAP] * hot[s][j][1] for j in range(EPS)], axis=0)
        o_ref[s * T:(s + 1) * T, :] += lax.dot_general(pb[s], yw.astype(BF16), (((0,), (0,)), ((), ())),
                                                       preferred_element_type=F32)

    def pair(idx, carry):
        s = idx // EPS
        j = idx % EPS
        e = eb * EPS + j
        n_tok = cnt_ref[(i * nsub + s) * N_EXP + e]
        rows = pl.ds(pl.multiple_of(s * T, T), T)

        def chunk(ci, c2):
            p1, w_slot = one_hot(s, e, (ci * CAP).astype(F32))
            yw = (swiglu(_dot(p1, t_ref[rows, :]).astype(BF16), j) * w_slot).astype(BF16)
            o_ref[rows, :] += lax.dot_general(p1, yw, (((0,), (0,)), ((), ())), preferred_element_type=F32)
            return c2

        lax.fori_loop(1, (n_tok + CAP - 1) // CAP, chunk, 0)
        return carry

    lax.fori_loop(0, nsub * EPS, pair, 0)


def _moe_out_kernel(r_ref, sh_ref, x_ref, g_ref, mod_ref, o_ref):
    f = r_ref[...] + sh_ref[...]
    ms = jnp.mean(f * f, axis=-1, keepdims=True)
    o_ref[...] = x_ref[...] + mod_ref[5:6, :] * (f * lax.rsqrt(ms + EPS) * g_ref[...])


def moe_layer3(x, M, g_pre, g_post, mods, mp, nsub):
    t, wt, cnt, sh = moe_router2(x, M, g_pre, mods, mp)
    T = MOE_T
    TS = nsub * T
    grid_spec = pltpu.PrefetchScalarGridSpec(
        num_scalar_prefetch=1, grid=(M // TS, N_EXP // MOE_EPS),
        in_specs=[pl.BlockSpec((TS, D), lambda i, e, c: (i, 0)),
                  pl.BlockSpec((nsub, N_EXP, T), lambda i, e, c: (i, 0, 0)),
                  pl.BlockSpec((MOE_EPS, D, EXP_FF), lambda i, e, c: (e, 0, 0)),
                  pl.BlockSpec((MOE_EPS, D, EXP_FF), lambda i, e, c: (e, 0, 0)),
                  pl.BlockSpec((MOE_EPS, EXP_FF, D), lambda i, e, c: (e, 0, 0))],
        out_specs=pl.BlockSpec((TS, D), lambda i, e, c: (i, 0)),
        scratch_shapes=[pltpu.VMEM((nsub, N_EXP, T), F32)])
    routed = pl.pallas_call(
        functools.partial(_moe3_kernel, nsub=nsub), out_shape=jax.ShapeDtypeStruct((M, D), F32),
        grid_spec=grid_spec, compiler_params=_cp(("parallel", "arbitrary")), name="moe_experts",
    )(cnt.reshape(-1), t, wt, mp['w1'].astype(BF16), mp['w3'].astype(BF16), mp['w2'].astype(BF16))
    tm = 512
    return pl.pallas_call(
        _moe_out_kernel, out_shape=jax.ShapeDtypeStruct((M, D), F32), grid=(M // tm,),
        in_specs=[pl.BlockSpec((tm, D), lambda i: (i, 0)), pl.BlockSpec((tm, D), lambda i: (i, 0)),
                  pl.BlockSpec((tm, D), lambda i: (i, 0)), pl.BlockSpec((1, D), lambda i: (0, 0)),
                  pl.BlockSpec((None, 6, D), lambda i: (_seq_of_rowblock(i, tm), 0, 0))],
        out_specs=pl.BlockSpec((tm, D), lambda i: (i, 0)),
        compiler_params=_cp(("parallel",)), name="moe_output")(routed, sh, x, g_post.reshape(1, D), mods)


def kernel(x, c, ctx, c_ctx, mod_w, mod_b, norm_mix_pre, norm_mix_post, norm_ffn_pre, norm_ffn_post, router_w, router_bias, expert_w1, expert_w3, expert_w2, shared_w1, shared_w3, shared_w2, ev_w_in, ev_w_out, ssd_conv_w, ssd_conv_b, ssd_dt_bias, ssd_a_log, ssd_d, ssd_norm_w, hy_conv_w, hy_conv_b, hy_mlp_w0, hy_mlp_b0, hy_freq0, hy_mlp_w1, hy_mlp_b1, hy_freq1, hy_mlp_w2, hy_bias, od_w_in, od_w_out, rw_mu, rw_w0, rw_w_up, rw_a0, rw_a_up, rw_g_up, rw_k_k, rw_k_a, rw_r_k, rw_ln_w, rw_ln_b, at_q_norm, at_k_norm):
    xs = jnp.concatenate([x.reshape(N_LAT, D), ctx.reshape(BATCH * CTX, D)], axis=0)
    cvecs = jnp.zeros((8, D), F32).at[0:BATCH].set(c).at[BATCH].set(c_ctx)
    assert mod_w.shape[0] == 2, "one even (SSD | Hyena) layer followed by one odd (RWKV | attention) layer"

    def moe_params(i):
        return dict(router_w=router_w[i], router_bias=router_bias[i], w1=expert_w1[i], w3=expert_w3[i],
                    w2=expert_w2[i], s1=shared_w1[i], s3=shared_w3[i], s2=shared_w2[i])

    mods = modulation(cvecs, mod_w[0], mod_b[0])[:BATCH + 1].reshape(BATCH + 1, 6, D)
    ep = dict(w_in=ev_w_in[0], w_out=ev_w_out[0], ssd_conv_w=ssd_conv_w[0], ssd_conv_b=ssd_conv_b[0],
              ssd_dt_bias=ssd_dt_bias[0], ssd_a_log=ssd_a_log[0], ssd_d=ssd_d[0], ssd_norm_w=ssd_norm_w[0],
              hy_conv_w=hy_conv_w[0], hy_conv_b=hy_conv_b[0], hy_mlp_w0=hy_mlp_w0[0], hy_mlp_b0=hy_mlp_b0[0],
              hy_freq0=hy_freq0[0], hy_mlp_w1=hy_mlp_w1[0], hy_mlp_b1=hy_mlp_b1[0], hy_freq1=hy_freq1[0],
              hy_mlp_w2=hy_mlp_w2[0], hy_bias=hy_bias[0])
    xs = even_mixer(xs, mods, norm_mix_pre[0], norm_mix_post[0], ep)
    xs = moe_layer3(xs, N_ROWS, norm_ffn_pre[0], norm_ffn_post[0], mods, moe_params(0), 3)
    mods = modulation(cvecs, mod_w[1], mod_b[1])[:BATCH + 1].reshape(BATCH + 1, 6, D)
    op = dict(w_in=od_w_in[0], w_out=od_w_out[0], mu=rw_mu[0], w0=rw_w0[0], w_up=rw_w_up[0], a0=rw_a0[0],
              a_up=rw_a_up[0], g_up=rw_g_up[0], k_k=rw_k_k[0], k_a=rw_k_a[0], r_k=rw_r_k[0], ln_w=rw_ln_w[0],
              ln_b=rw_ln_b[0], q_norm=at_q_norm[0], k_norm=at_k_norm[0])
    xl = odd_mixer(xs, mods, norm_mix_pre[1], norm_mix_post[1], op)
    xl = moe_layer3(xl, N_LAT, norm_ffn_pre[1], norm_ffn_post[1], mods, moe_params(1), 4)
    return xl.reshape(BATCH, SEQ, D)
```

```python
import functools
import math

import numpy as np
import jax
import jax.numpy as jnp
from jax import lax
from jax.experimental import pallas as pl
from jax.experimental.pallas import tpu as pltpu

F32 = jnp.float32
BF16 = jnp.bfloat16

D = 1024
BATCH = 2
SEQ = 8192
CTX = 256
N_LAT = BATCH * SEQ
N_ROWS = N_LAT + BATCH * CTX
EPS = 1e-6
GRID_W = 64

SSD_HEADS = 16
SSD_P = 64
SSD_G = 2
SSD_S = 128
SSD_Q = 128
HY_W = 1024
HY_EMB = 33
HY_HID = 64

RW_H = 16
RW_N = 64
RW_CHUNK = 128
RW_GN_EPS = 64e-5

AT_KV = 4
AT_HD = 64

N_EXP = 64
TOP_K = 8
N_GRP = 8
TOPK_GRP = 4
EXP_FF = 256
ROUTED_SCALE = 2.5
MOE_BLK = 128

VMEM_LIMIT = 56 * 1024 * 1024


def _cp(sem, vmem=None):
    return pltpu.CompilerParams(dimension_semantics=sem, vmem_limit_bytes=vmem or VMEM_LIMIT)


def _dot(a, b):
    return jnp.dot(a, b, preferred_element_type=F32)


def _dot_nt(a, b):
    return lax.dot_general(a, b, (((1,), (1,)), ((), ())), preferred_element_type=F32)


def _split(x):
    hi = x.astype(BF16)
    lo = (x - hi.astype(F32)).astype(BF16)
    return hi, lo


def _dot3(a, b):
    ah, al = _split(a)
    bh, bl = _split(b)
    return _dot(ah, bh) + (_dot(ah, bl) + _dot(al, bh))


def _dot2l(a, b):
    ah, al = _split(a)
    return _dot(ah, b) + _dot(al, b)


def _dot2r(a, b):
    bh, bl = _split(b)
    return _dot(a, bh) + _dot(a, bl)


def _silu(x):
    return x * (1.0 / (1.0 + jnp.exp(-x)))


def _sigmoid(x):
    return 1.0 / (1.0 + jnp.exp(-x))


def _softplus(x):
    return jnp.maximum(x, 0.0) + jnp.log(1.0 + jnp.exp(-jnp.abs(x)))


def _seq_of_rowblock(i, tm):
    return jnp.minimum((i * tm) // SEQ, 2)


def _mm_kernel(a_ref, b_ref, o_ref, *, passes):
    a = a_ref[...]
    b = b_ref[...]
    if passes == 3:
        o_ref[...] = _dot3(a.astype(F32), b.astype(F32))
    else:
        o_ref[...] = _dot(a.astype(BF16), b.astype(BF16))


def matmul(a, b, tm, tn, passes=1, name="mm"):
    M, K = a.shape
    N = b.shape[1]
    return pl.pallas_call(
        functools.partial(_mm_kernel, passes=passes),
        out_shape=jax.ShapeDtypeStruct((M, N), F32),
        grid=(M // tm, N // tn),
        in_specs=[pl.BlockSpec((tm, K), lambda i, j: (i, 0)),
                  pl.BlockSpec((K, tn), lambda i, j: (0, j))],
        out_specs=pl.BlockSpec((tm, tn), lambda i, j: (i, j)),
        compiler_params=_cp(("parallel", "parallel")), name=name)(a, b)


def _nmm_kernel(x_ref, g_ref, mod_ref, w_ref, o_ref, a_sc, *, shift_i, scale_i):
    @pl.when(pl.program_id(1) == 0)
    def _():
        x = x_ref[...]
        ms = jnp.mean(x * x, axis=-1, keepdims=True)
        y = x * lax.rsqrt(ms + EPS) * g_ref[...]
        h = y * (1.0 + mod_ref[scale_i:scale_i + 1, :]) + mod_ref[shift_i:shift_i + 1, :]
        a_sc[...] = h.astype(BF16)

    o_ref[...] = _dot(a_sc[...], w_ref[...])


def norm_mod_matmul(x, g, mods, w, shift_i, scale_i, tm=512, tn=None, name="nmm"):
    M = x.shape[0]
    N = w.shape[1]
    tn = tn or N
    return pl.pallas_call(
        functools.partial(_nmm_kernel, shift_i=shift_i, scale_i=scale_i),
        out_shape=jax.ShapeDtypeStruct((M, N), F32),
        grid=(M // tm, N // tn),
        in_specs=[pl.BlockSpec((tm, D), lambda i, j: (i, 0)),
                  pl.BlockSpec((1, D), lambda i, j: (0, 0)),
                  pl.BlockSpec((None, 6, D), lambda i, j: (_seq_of_rowblock(i, tm), 0, 0)),
                  pl.BlockSpec((D, tn), lambda i, j: (0, j))],
        out_specs=pl.BlockSpec((tm, tn), lambda i, j: (i, j)),
        scratch_shapes=[pltpu.VMEM((tm, D), BF16)],
        compiler_params=_cp(("parallel", "arbitrary")), name=name)(x, g.reshape(1, D), mods, w)


def _outproj_kernel(a1_ref, a2_ref, w_ref, x_ref, g_ref, mod_ref, o_ref, *, gate_i):
    y = _dot(a1_ref[...].astype(BF16), w_ref[0:D, :]) + _dot(a2_ref[...].astype(BF16), w_ref[D:2 * D, :])
    ms = jnp.mean(y * y, axis=-1, keepdims=True)
    o_ref[...] = x_ref[...] + mod_ref[gate_i:gate_i + 1, :] * (y * lax.rsqrt(ms + EPS) * g_ref[...])


def outproj_residual(a1, a2, w, x, g, mods, gate_i, tm=256, name="outproj"):
    M = a1.shape[0]
    return pl.pallas_call(
        functools.partial(_outproj_kernel, gate_i=gate_i),
        out_shape=jax.ShapeDtypeStruct((M, D), F32),
        grid=(M // tm,),
        in_specs=[pl.BlockSpec((tm, D), lambda i: (i, 0)),
                  pl.BlockSpec((tm, D), lambda i: (i, 0)),
                  pl.BlockSpec((2 * D, D), lambda i: (0, 0)),
                  pl.BlockSpec((tm, D), lambda i: (i, 0)),
                  pl.BlockSpec((1, D), lambda i: (0, 0)),
                  pl.BlockSpec((None, 6, D), lambda i: (_seq_of_rowblock(i, tm), 0, 0))],
        out_specs=pl.BlockSpec((tm, D), lambda i: (i, 0)),
        compiler_params=_cp(("parallel",)), name=name)(a1, a2, w, x, g.reshape(1, D), mods)


def _mod_kernel(c_ref, w_ref, b_ref, o_ref):
    o_ref[...] = _dot3(_silu(c_ref[...]), w_ref[...]) + b_ref[...]


def modulation(cvecs, w, b):
    N = w.shape[1]
    tn = 1024
    return pl.pallas_call(
        _mod_kernel, out_shape=jax.ShapeDtypeStruct((8, N), F32), grid=(N // tn,),
        in_specs=[pl.BlockSpec((8, D), lambda j: (0, 0)),
                  pl.BlockSpec((D, tn), lambda j: (0, j)),
                  pl.BlockSpec((1, tn), lambda j: (0, j))],
        out_specs=pl.BlockSpec((8, tn), lambda j: (0, j)),
        compiler_params=_cp(("parallel",)), name="modulation")(cvecs, w, b.reshape(1, N))


CONV_TM = 256


def _conv3_kernel(x_ref, prev_ref, next_ref, w_ref, b_ref, o_ref, *, act):
    tm = CONV_TM
    row0 = pl.program_id(0) * tm
    seq_len = jnp.where(row0 < N_LAT, SEQ, CTX)
    pos = jnp.where(row0 < N_LAT, row0 % SEQ, (row0 - N_LAT) % CTX)
    cur = x_ref[...]
    rows = lax.broadcasted_iota(jnp.int32, cur.shape, 0)
    prev_row = prev_ref[7:8, :] * (pos > 0).astype(F32)
    next_row = next_ref[0:1, :] * (pos + tm < seq_len).astype(F32)
    xm1 = jnp.where(rows == 0, prev_row, pltpu.roll(cur, 1, 0))
    xp1 = jnp.where(rows == tm - 1, next_row, pltpu.roll(cur, tm - 1, 0))
    y = xm1 * w_ref[0:1, :] + cur * w_ref[1:2, :] + xp1 * w_ref[2:3, :] + b_ref[...]
    o_ref[...] = _silu(y) if act else y


def dwconv3(p, col0, ncols, w, b, act, name, cb=512):
    tm = CONV_TM
    cb = math.gcd(cb, math.gcd(col0, ncols)) if col0 else math.gcd(cb, ncols)
    r8 = tm // 8
    n8 = N_ROWS // 8
    c0 = col0 // cb
    return pl.pallas_call(
        functools.partial(_conv3_kernel, act=act),
        out_shape=jax.ShapeDtypeStruct((N_ROWS, ncols), F32),
        grid=(N_ROWS // tm, ncols // cb),
        in_specs=[pl.BlockSpec((tm, cb), lambda i, j: (i, c0 + j)),
                  pl.BlockSpec((8, cb), lambda i, j: (jnp.maximum(i * r8 - 1, 0), c0 + j)),
                  pl.BlockSpec((8, cb), lambda i, j: (jnp.minimum((i + 1) * r8, n8 - 1), c0 + j)),
                  pl.BlockSpec((3, cb), lambda i, j: (0, j)),
                  pl.BlockSpec((1, cb), lambda i, j: (0, j))],
        out_specs=pl.BlockSpec((tm, cb), lambda i, j: (i, j)),
        compiler_params=_cp(("parallel", "parallel")), name=name)(p, p, p, jnp.transpose(w), b.reshape(1, ncols))


def _ssd_kernel(xs_ref, bm_ref, cm_ref, dt_ref, dtT_ref, bias_ref, biasT_ref, alog_ref, alogT_ref,
                y_ref, st_ref):
    d = pl.program_id(0)
    c = pl.program_id(2)
    Q = SSD_Q
    HG = SSD_HEADS // SSD_G

    @pl.when(c == 0)
    def _():
        st_ref[...] = jnp.zeros_like(st_ref)

    isb = d == 1
    sgn = 1 - 2 * d
    ii = lax.broadcasted_iota(jnp.int32, (Q, Q), 0)
    jj = lax.broadcasted_iota(jnp.int32, (Q, Q), 1)
    tri = (jj <= ii).astype(BF16)
    triT = (ii <= jj).astype(BF16)
    mask = sgn * (ii - jj) >= 0
    xs = xs_ref[...]
    G = range(SSD_G)
    dt = [_softplus(dt_ref[g] + bias_ref[g]) for g in G]
    dtT = [_softplus(dtT_ref[g] + biasT_ref[g]) for g in G]
    a = [dt[g] * (-jnp.exp(alog_ref[g])) for g in G]
    aT = [dtT[g] * (-jnp.exp(alogT_ref[g])) for g in G]
    cs = [_dot2r(tri, a[g]) for g in G]
    csT = [_dot2l(aT[g], triT) for g in G]
    tot = [cs[g][Q - 1:Q, :] for g in G]
    p = [jnp.where(isb, a[g] - cs[g], cs[g]) for g in G]
    pT = [jnp.where(isb, aT[g] - csT[g], csT[g]) for g in G]
    dec_out = [jnp.exp(jnp.where(isb, tot[g], 0.0) + p[g]) for g in G]
    dec_state = [jnp.exp(jnp.where(isb, 0.0, tot[g]) - p[g]) for g in G]
    chunk_dec = [jnp.exp(tot[g]) for g in G]
    bm = [bm_ref[:, g * SSD_S:(g + 1) * SSD_S].astype(BF16) for g in G]
    cm = [cm_ref[:, g * SSD_S:(g + 1) * SSD_S].astype(BF16) for g in G]
    cb = [_dot_nt(cm[g], bm[g]) for g in G]
    GH = [(g, h) for g in G for h in range(HG)]
    NH = range(len(GH))
    lm = [(cb[g] * jnp.exp(jnp.where(mask, p[g][:, h:h + 1] - pT[g][h:h + 1, :], -1e30))).astype(BF16)
          for g, h in GH]
    xh = [xs[:, n * SSD_P:(n + 1) * SSD_P] * dt[g][:, h:h + 1] for n, (g, h) in enumerate(GH)]
    s_old = [st_ref[n] for n in NH]
    y_in = [_dot(lm[n], xh[n].astype(BF16)) for n in NH]
    y_st = [_dot(cm[g], s_old[n].astype(BF16)) for n, (g, h) in enumerate(GH)]
    upd = [lax.dot_general(bm[g], (xh[n] * dec_state[g][:, h:h + 1]).astype(BF16), (((0,), (0,)), ((), ())),
                           preferred_element_type=F32) for n, (g, h) in enumerate(GH)]
    for n, (g, h) in enumerate(GH):
        st_ref[n] = chunk_dec[g][:, h:h + 1] * s_old[n] + upd[n]
    y_ref[...] = jnp.concatenate([y_in[n] + dec_out[g][:, h:h + 1] * y_st[n] for n, (g, h) in enumerate(GH)],
                                 axis=1)


def _ssd_rowblock(d, b, c):
    n_ctx = CTX // SSD_Q
    n_lat = SEQ // SSD_Q
    cc = jnp.where(d == 0, c, n_ctx - 1 - c)
    lc = jnp.where(d == 0, c - n_ctx, n_ctx + n_lat - 1 - c)
    return jnp.where(c < n_ctx, N_LAT // SSD_Q + b * n_ctx + cc, b * n_lat + lc)


def ssd_scan(xbc, dt_raw, dt_bias, a_log):
    HG = SSD_HEADS // SSD_G
    W = SSD_HEADS * SSD_P
    dsel = dt_raw[:, :2 * SSD_HEADS].reshape(N_ROWS, 2, SSD_G, HG).transpose(1, 2, 0, 3)
    dselT = dsel.transpose(0, 1, 3, 2)
    bias = dt_bias.reshape(2, SSD_G, 1, HG)
    biasT = dt_bias.reshape(2, SSD_G, HG, 1)
    alog = a_log.reshape(2, SSD_G, 1, HG)
    alogT = a_log.reshape(2, SSD_G, HG, 1)
    nch = (CTX + SEQ) // SSD_Q
    rb = _ssd_rowblock
    GS = SSD_G * SSD_S
    par = lambda shape: pl.BlockSpec((None,) + shape, lambda d, b, c: (d, 0, 0, 0))
    return pl.pallas_call(
        _ssd_kernel,
        out_shape=jax.ShapeDtypeStruct((2, N_ROWS, W), F32),
        grid=(2, BATCH, nch),
        in_specs=[pl.BlockSpec((SSD_Q, W), lambda d, b, c: (rb(d, b, c), 0)),
                  pl.BlockSpec((SSD_Q, GS), lambda d, b, c: (rb(d, b, c), W // GS)),
                  pl.BlockSpec((SSD_Q, GS), lambda d, b, c: (rb(d, b, c), W // GS + 1)),
                  pl.BlockSpec((None, SSD_G, SSD_Q, HG), lambda d, b, c: (d, 0, rb(d, b, c), 0)),
                  pl.BlockSpec((None, SSD_G, HG, SSD_Q), lambda d, b, c: (d, 0, 0, rb(d, b, c))),
                  par((SSD_G, 1, HG)), par((SSD_G, HG, 1)), par((SSD_G, 1, HG)), par((SSD_G, HG, 1))],
        out_specs=pl.BlockSpec((None, SSD_Q, W), lambda d, b, c: (d, rb(d, b, c), 0)),
        scratch_shapes=[pltpu.VMEM((SSD_HEADS, SSD_S, SSD_P), F32)],
        compiler_params=_cp(("parallel", "parallel", "arbitrary")), name="ssd_scan",
    )(xbc, xbc, xbc, dsel, dselT, bias, biasT, alog, alogT)


def _ssd_out_kernel(yf_ref, yb_ref, xs_ref, z_ref, dskip_ref, nw_ref, o_ref):
    y = yf_ref[...] + yb_ref[...] + xs_ref[...] * dskip_ref[...]
    y = y * _silu(z_ref[...])
    gs = SSD_HEADS * SSD_P // SSD_G
    parts = []
    for g in range(SSD_G):
        yg = y[:, g * gs:(g + 1) * gs]
        parts.append(yg * lax.rsqrt(jnp.mean(yg * yg, axis=-1, keepdims=True) + EPS))
    o_ref[...] = jnp.concatenate(parts, axis=1) * nw_ref[...]


def ssd_output(y2, xbc, p, d_skip, norm_w, tm=256):
    W = SSD_HEADS * SSD_P
    dexp = jnp.repeat(d_skip, SSD_P).reshape(1, W)
    return pl.pallas_call(
        _ssd_out_kernel, out_shape=jax.ShapeDtypeStruct((N_ROWS, W), F32), grid=(N_ROWS // tm,),
        in_specs=[pl.BlockSpec((None, tm, W), lambda i: (0, i, 0)),
                  pl.BlockSpec((None, tm, W), lambda i: (1, i, 0)),
                  pl.BlockSpec((tm, W), lambda i: (i, 0)),
                  pl.BlockSpec((tm, W), lambda i: (i, 0)),
                  pl.BlockSpec((1, W), lambda i: (0, 0)),
                  pl.BlockSpec((1, W), lambda i: (0, 0))],
        out_specs=pl.BlockSpec((tm, W), lambda i: (i, 0)),
        compiler_params=_cp(("parallel",)), name="ssd_output")(y2, y2, xbc, p, dexp, norm_w.reshape(1, W))


def _hyfilt_kernel(f_ref, w0_ref, b0_ref, fr0_ref, w1_ref, b1_ref, fr1_ref, w2_ref, dl_ref, h_ref, ss_ref, *,
                   n_tiles):
    f = f_ref[...]
    h = jnp.sin(fr0_ref[...] * (_dot3(f, w0_ref[...]) + b0_ref[...]))
    h = jnp.sin(fr1_ref[...] * (_dot3(h, w1_ref[...]) + b1_ref[...]))
    h = _dot3(h, w2_ref[...])
    h = h * jnp.exp(-f[:, 0:1] * dl_ref[...])
    side = pl.program_id(0) // n_tiles
    j = pl.program_id(0) % n_tiles

    @pl.when(j == 0)
    def _():
        ss_ref[...] = jnp.zeros_like(ss_ref)

    ss_ref[...] += jnp.sum(h * h, axis=0, keepdims=True)
    row = lax.broadcasted_iota(jnp.int32, (h.shape[0], 1), 0) + j * h.shape[0]
    h_ref[...] = jnp.where((side == 1) & (row == 0), 0.0, h)


def hyena_filter_taps(L, hp):
    pos = jnp.arange(L, dtype=F32)
    t = pos / (L - 1)
    bands = (HY_EMB - 1) // 2
    freqs = jnp.linspace(1e-4, bands - 1, bands, dtype=F32)
    ang = (2.0 * math.pi / L) * pos[:, None] * freqs[None, :]
    feats = jnp.concatenate([t[:, None], jnp.cos(ang), -jnp.sin(ang)], axis=-1)
    feats = jnp.pad(feats, ((0, 0), (0, 128 - HY_EMB)))
    feats = jnp.concatenate([feats, jnp.flip(feats, axis=0)], axis=0)
    w0 = jnp.pad(hp['hy_mlp_w0'], ((0, 128 - HY_EMB), (0, 0)))
    min_decay = math.log(1e-2) / 1.5
    max_decay = math.log(1e-2) / 0.3
    deltas = jnp.abs(jnp.linspace(min_decay, max_decay, HY_W, dtype=F32))
    dl = jnp.tile(deltas, 2).reshape(1, 2 * HY_W)
    w2 = hp['hy_mlp_w2'].reshape(HY_HID, 2, 2, HY_W).transpose(0, 2, 1, 3).reshape(HY_HID, 4 * HY_W)
    tl = min(L, 512)
    n_tiles = L // tl
    NS = 2 * HY_W
    full = lambda shape: pl.BlockSpec(shape, lambda i: (0, 0))
    return pl.pallas_call(
        functools.partial(_hyfilt_kernel, n_tiles=n_tiles),
        out_shape=(jax.ShapeDtypeStruct((2 * L, NS), F32), jax.ShapeDtypeStruct((1, 2 * NS), F32)),
        grid=(2 * n_tiles,),
        in_specs=[pl.BlockSpec((tl, 128), lambda i: (i, 0)), full((128, HY_HID)), full((1, HY_HID)),
                  full((1, HY_HID)), full((HY_HID, HY_HID)), full((1, HY_HID)), full((1, HY_HID)),
                  pl.BlockSpec((HY_HID, NS), lambda i: (0, i // n_tiles)), full((1, NS))],
        out_specs=(pl.BlockSpec((tl, NS), lambda i: (i, 0)), pl.BlockSpec((1, NS), lambda i: (0, i // n_tiles))),
        compiler_params=_cp(("arbitrary",)), name="hyena_filter",
    )(feats, w0, hp['hy_mlp_b0'].reshape(1, -1), hp['hy_freq0'].reshape(1, -1), hp['hy_mlp_w1'],
      hp['hy_mlp_b1'].reshape(1, -1), hp['hy_freq1'].reshape(1, -1), w2, dl)


def _cis(num, den):
    ang = (2.0 * math.pi / den) * (num % den).astype(F32)
    return jnp.cos(ang), -jnp.sin(ang)


def _fft_consts(NB, BS):
    N = NB * BS
    h = NB // 2
    k1 = jnp.arange(h, dtype=jnp.int32)
    j = jnp.arange(NB, dtype=jnp.int32)
    re, im = _cis(j[None, :] * (2 * k1[:, None] + 1), 2 * NB)
    f1 = jnp.concatenate([re, im], axis=0)
    neg = jnp.where(j >= h, -1.0, 1.0)[None, :]
    f1_data = f1[:, :h]
    f1_filt = f1 * neg
    f1_inv = (2.0 / N) * jnp.concatenate([re[:, :h].T, im[:, :h].T], axis=1)
    r = jnp.arange(BS, dtype=jnp.int32)
    k2 = jnp.arange(BS, dtype=jnp.int32)
    kk = 2 * k1[:, None, None] + 2 * NB * k2[None, :, None] + 1
    gre, gim = _cis(kk * r[None, None, :], 2 * N)
    gf = jnp.concatenate([jnp.concatenate([gre, -gim], axis=2), jnp.concatenate([gim, gre], axis=2)], axis=1)
    gret, gimt = gre.transpose(0, 2, 1), gim.transpose(0, 2, 1)
    gi = jnp.concatenate([jnp.concatenate([gret, gimt], axis=2), jnp.concatenate([-gimt, gret], axis=2)], axis=1)
    return (f1_data.astype(BF16), f1_filt.astype(BF16), f1_inv.astype(BF16), gf.astype(BF16), gi.astype(BF16))


FFT_PAD = 8


def _fft_fwd_kernel(u_ref, f1_ref, g_ref, o_ref, t_sc, *, NB, BS, nj, kg):
    pitch = NB + FFT_PAD
    @pl.when(pl.program_id(2) == 0)
    def _():
        f1 = f1_ref[...]

        def body(r, carry):
            xr = u_ref[pl.ds(r, nj, stride=BS), :].astype(BF16)
            t_sc[pl.ds(pl.multiple_of(r * pitch, 8), NB), :] = _dot(f1, xr)
            return carry

        lax.fori_loop(0, BS, body, 0, unroll=8)

    k0 = pl.program_id(2) * kg
    for i in range(kg):
        are = t_sc[pl.ds(k0 + i, BS, stride=pitch), :]
        aim = t_sc[pl.ds(k0 + i + NB // 2, BS, stride=pitch), :]
        a = jnp.concatenate([are, aim], axis=0).astype(BF16)
        o_ref[i] = _dot(g_ref[i], a)


def fft_fwd(u, col0, nbatch, nj, f1, gf, NB, BS, ct=128, kg=8):
    h = NB // 2
    kg = min(kg, h)
    return pl.pallas_call(
        functools.partial(_fft_fwd_kernel, NB=NB, BS=BS, nj=nj, kg=kg),
        out_shape=jax.ShapeDtypeStruct((nbatch, h, 2 * BS, HY_W), F32),
        grid=(nbatch, HY_W // ct, h // kg),
        in_specs=[pl.BlockSpec((nj * BS, ct), lambda b, c, k: (b, col0 // ct + c)),
                  pl.BlockSpec((NB, nj), lambda b, c, k: (0, 0)),
                  pl.BlockSpec((kg, 2 * BS, 2 * BS), lambda b, c, k: (k, 0, 0))],
        out_specs=pl.BlockSpec((None, kg, 2 * BS, ct), lambda b, c, k: (b, k, 0, c)),
        scratch_shapes=[pltpu.VMEM((BS * (NB + FFT_PAD), ct), F32)],
        compiler_params=_cp(("parallel", "parallel", "arbitrary")), name="hyena_fft_fwd")(u, f1, gf)


def _cmul(u, h, half):
    ure, uim = u[:half], u[half:]
    hre, him = h[:half], h[half:]
    return jnp.concatenate([ure * hre - uim * him, ure * him + uim * hre], axis=0)


def _fft_inv_kernel(us_ref, hs_ref, gi_ref, f1i_ref, o_ref, t_sc, *, NB, BS, kg):
    ks = pl.program_id(2)
    pitch = 2 * BS + FFT_PAD
    for i in range(kg):
        y = _cmul(us_ref[i], hs_ref[i], BS).astype(BF16)
        row = pl.multiple_of((ks * kg + i) * pitch, 8)
        t_sc[pl.ds(row, 2 * BS), :] = _dot(gi_ref[i], y)

    @pl.when(ks == pl.num_programs(2) - 1)
    def _():
        f1i = f1i_ref[...]

        def body(r, carry):
            bre = t_sc[pl.ds(r, NB // 2, stride=pitch), :]
            bim = t_sc[pl.ds(r + BS, NB // 2, stride=pitch), :]
            b = jnp.concatenate([bre, bim], axis=0).astype(BF16)
            o_ref[pl.ds(r, NB // 2, stride=BS), :] = _dot(f1i, b)
            return carry

        lax.fori_loop(0, BS, body, 0, unroll=8)


def fft_inv(us, hs, gi, f1i, NB, BS, ct=128, kg=8):
    nbatch, h = us.shape[0], NB // 2
    kg = min(kg, h)
    L = h * BS
    return pl.pallas_call(
        functools.partial(_fft_inv_kernel, NB=NB, BS=BS, kg=kg),
        out_shape=jax.ShapeDtypeStruct((nbatch * L, HY_W), F32),
        grid=(nbatch, HY_W // ct, h // kg),
        in_specs=[pl.BlockSpec((None, kg, 2 * BS, ct), lambda b, c, k: (b, k, 0, c)),
                  pl.BlockSpec((None, kg, 2 * BS, ct), lambda b, c, k: (0, k, 0, c)),
                  pl.BlockSpec((kg, 2 * BS, 2 * BS), lambda b, c, k: (k, 0, 0)),
                  pl.BlockSpec((h, NB), lambda b, c, k: (0, 0))],
        out_specs=pl.BlockSpec((L, ct), lambda b, c, k: (b, c)),
        scratch_shapes=[pltpu.VMEM((h * (2 * BS + FFT_PAD), ct), F32)],
        compiler_params=_cp(("parallel", "parallel", "arbitrary")), name="hyena_fft_inv")(us, hs, gi, f1i)


def _dft_consts(L):
    N = 2 * L
    k = jnp.arange(L, dtype=jnp.int32)
    n = jnp.arange(N, dtype=jnp.int32)
    re, im = _cis(n[None, :] * (2 * k[:, None] + 1), 2 * N)
    f = jnp.concatenate([re, im], axis=0)
    neg = jnp.where(n >= L, -1.0, 1.0)[None, :]
    fi = (2.0 / N) * jnp.concatenate([re[:, :L].T, im[:, :L].T], axis=1)
    return f[:, :L].astype(BF16), (f * neg).astype(BF16), fi.astype(BF16)


def _cdft_kernel(f_ref, x_ref, o_ref):
    o_ref[...] = _dot(f_ref[...], x_ref[...].astype(BF16))


def dft_fwd(x, f, row0, col0, nbatch, ct=256):
    M, K = f.shape
    return pl.pallas_call(
        _cdft_kernel, out_shape=jax.ShapeDtypeStruct((nbatch, M, HY_W), F32),
        grid=(nbatch, HY_W // ct),
        in_specs=[pl.BlockSpec((M, K), lambda b, c: (0, 0)),
                  pl.BlockSpec((K, ct), lambda b, c: (row0 // K + b, col0 // ct + c))],
        out_specs=pl.BlockSpec((None, M, ct), lambda b, c: (b, 0, c)),
        compiler_params=_cp(("parallel", "parallel")), name="hyena_dft_fwd")(f, x)


def _cdft_inv_kernel(us_ref, hs_ref, fi_ref, o_ref):
    half = us_ref.shape[0] // 2
    o_ref[...] = _dot(fi_ref[...], _cmul(us_ref[...], hs_ref[...], half).astype(BF16))


def dft_inv(us, hs, fi, ct=256):
    nbatch, M2, _ = us.shape
    L = fi.shape[0]
    return pl.pallas_call(
        _cdft_inv_kernel, out_shape=jax.ShapeDtypeStruct((nbatch * L, HY_W), F32),
        grid=(nbatch, HY_W // ct),
        in_specs=[pl.BlockSpec((None, M2, ct), lambda b, c: (b, 0, c)),
                  pl.BlockSpec((None, M2, ct), lambda b, c: (0, 0, c)),
                  pl.BlockSpec((L, M2), lambda b, c: (0, 0))],
        out_specs=pl.BlockSpec((L, ct), lambda b, c: (b, c)),
        compiler_params=_cp(("parallel", "parallel")), name="hyena_dft_inv")(us, hs, fi)


def _hy_gate_kernel(g_ref, y_ref, u_ref, ss_ref, b_ref, o_ref):
    scale = lax.rsqrt(ss_ref[0:1, :] + ss_ref[1:2, :] + 1e-6)
    o_ref[...] = g_ref[...] * (y_ref[...] * scale + u_ref[...] * b_ref[...])


def _hy_gate2_kernel(g_ref, yl_ref, yc_ref, ul_ref, uc_ref, ssl_ref, ssc_ref, b_ref, o_ref, *, n_lat_t):
    is_lat = pl.program_id(0) < n_lat_t
    y = jnp.where(is_lat, yl_ref[...], yc_ref[...])
    uin = jnp.where(is_lat, ul_ref[...], uc_ref[...])
    ss = jnp.where(is_lat, ssl_ref[...], ssc_ref[...])
    scale = lax.rsqrt(ss[0:1, :] + ss[1:2, :] + 1e-6)
    o_ref[...] = g_ref[...] * (y * scale + uin * b_ref[...])


def hy_gate(gate, gcol, grow, y, uin, ucol, urow, ss, order, bias, tm=256):
    M = y.shape[0]
    return pl.pallas_call(
        _hy_gate_kernel, out_shape=jax.ShapeDtypeStruct((M, HY_W), F32), grid=(M // tm,),
        in_specs=[pl.BlockSpec((tm, HY_W), lambda i: (grow // tm + i, gcol // HY_W)),
                  pl.BlockSpec((tm, HY_W), lambda i: (i, 0)),
                  pl.BlockSpec((tm, HY_W), lambda i: (urow // tm + i, ucol // HY_W)),
                  pl.BlockSpec((None, 2, HY_W), lambda i: (order, 0, 0)),
                  pl.BlockSpec((None, 1, HY_W), lambda i: (order, 0, 0))],
        out_specs=pl.BlockSpec((tm, HY_W), lambda i: (i, 0)),
        compiler_params=_cp(("parallel",)), name="hyena_gate")(gate, y, uin, ss, bias)


def hyena(u, hp):
    C = HY_W
    bias = hp['hy_bias'].reshape(2, 1, C)
    NB = BS = int(round(math.sqrt(2 * SEQ)))
    f1d, f1f, f1i, gf, gi = _fft_consts(NB, BS)
    taps, ss = hyena_filter_taps(SEQ, hp)
    ss = ss.reshape(2, 2, C).transpose(1, 0, 2)
    conv_l = lambda zin, zcol, order: fft_inv(fft_fwd(zin, zcol, BATCH, NB // 2, f1d, gf, NB, BS),
                                              fft_fwd(taps, order * C, 1, NB, f1f, gf, NB, BS), gi, f1i, NB, BS)
    z1_lat = hy_gate(u, 0, 0, conv_l(u, 2 * C, 0), u, 2 * C, 0, ss, 0, bias)
    y2_lat = conv_l(z1_lat, 0, 1)
    fd, ff, fi = _dft_consts(CTX)
    taps_c, ss_c = hyena_filter_taps(CTX, hp)
    ss_c = ss_c.reshape(2, 2, C).transpose(1, 0, 2)
    conv_c = lambda zin, zrow, zcol, order: dft_inv(dft_fwd(zin, fd, zrow, zcol, BATCH),
                                                    dft_fwd(taps_c, ff, 0, order * C, 1), fi)
    z1_ctx = hy_gate(u, 0, N_LAT, conv_c(u, N_LAT, 2 * C, 0), u, 2 * C, N_LAT, ss_c, 0, bias)
    y2_ctx = conv_c(z1_ctx, 0, 0, 1)
    tm = 256
    n_lat_t = N_LAT // tm
    lat = lambda i: (jnp.minimum(i, n_lat_t - 1), 0)
    ctx = lambda i: (jnp.maximum(i - n_lat_t, 0), 0)
    return pl.pallas_call(
        functools.partial(_hy_gate2_kernel, n_lat_t=n_lat_t),
        out_shape=jax.ShapeDtypeStruct((N_ROWS, C), F32), grid=(N_ROWS // tm,),
        in_specs=[pl.BlockSpec((tm, C), lambda i: (i, 1)),
                  pl.BlockSpec((tm, C), lat), pl.BlockSpec((tm, C), ctx),
                  pl.BlockSpec((tm, C), lat), pl.BlockSpec((tm, C), ctx),
                  pl.BlockSpec((None, 2, C), lambda i: (1, 0, 0)), pl.BlockSpec((None, 2, C), lambda i: (1, 0, 0)),
                  pl.BlockSpec((None, 1, C), lambda i: (1, 0, 0))],
        out_specs=pl.BlockSpec((tm, C), lambda i: (i, 0)),
        compiler_params=_cp(("parallel",)), name="hyena_gate2",
    )(u, y2_lat, y2_ctx, z1_lat, z1_ctx, ss, ss_c, bias)


EV_SSD_IN = SSD_HEADS * SSD_P
EV_XBC = EV_SSD_IN + 2 * SSD_G * SSD_S
EV_PAD_N = 5760


def even_mixer(x, mods, g_pre, g_post, ep):
    o1 = EV_SSD_IN
    o2 = o1 + EV_XBC
    o3 = o2 + 2 * SSD_HEADS
    w = ep['w_in']
    n_in = w.shape[1]
    w_perm = jnp.concatenate([w[:, :o2], w[:, o3:], w[:, o2:o3],
                              jnp.zeros((D, EV_PAD_N - n_in), F32)], axis=1).astype(BF16)
    p = norm_mod_matmul(x, g_pre, mods, w_perm, 0, 1, tn=1920, name="even_in_proj")
    xbc = dwconv3(p, o1, EV_XBC, ep['ssd_conv_w'], ep['ssd_conv_b'], True, "ssd_conv")
    u = dwconv3(p, o2, 3 * HY_W, ep['hy_conv_w'], ep['hy_conv_b'], False, "hyena_conv")
    dt_raw = p[:, o2 + 3 * HY_W:o2 + 3 * HY_W + 2 * SSD_HEADS]
    y2 = ssd_scan(xbc, dt_raw, ep['ssd_dt_bias'], ep['ssd_a_log'])
    s = ssd_output(y2, xbc, p, ep['ssd_d'], ep['ssd_norm_w'])
    zh = hyena(u, ep)
    return outproj_residual(s, zh, ep['w_out'].astype(BF16), x, g_post, mods, 2, name="even_out_proj")


def _head_sum(x, e, et):
    return _dot2l(_dot2l(x, e), et)


def _rw_prep_kernel(r_ref, k_ref, v_ref, lo_ref, w0_ref, wup_ref, a0_ref, aup_ref, gup_ref, kk_ref, ka_ref,
                    rk_ref, e_ref, et_ref, lw_ref, kd_ref, be_ref, kap_ref, g_ref, bonus_ref):
    r, k, v = r_ref[...], k_ref[...], v_ref[...]
    lo = lo_ref[...]
    wc, ac, gc = lo[:, 0:64], lo[:, 64:128], lo[:, 128:384]
    e, et = e_ref[...], et_ref[...]
    kk = k * kk_ref[...]
    kap = kk * lax.rsqrt(_head_sum(kk * kk, e, et) + 1e-12)
    kap_ref[...] = kap
    g_ref[...] = _dot3(_sigmoid(gc), gup_ref[...])
    kd_sum = jnp.zeros_like(k)
    for d in range(2):
        wlog = -_softplus(-(w0_ref[d:d + 1, :] + _dot3(jnp.tanh(wc), wup_ref[d]))) - 0.5
        lw_ref[d] = -jnp.exp(wlog)
        a = _sigmoid(a0_ref[d:d + 1, :] + _dot3(ac, aup_ref[d]))
        kd = k * (1.0 + (a - 1.0) * ka_ref[...])
        kd_ref[d] = kd
        be_ref[d] = kap * a
        kd_sum = kd_sum + kd
    bonus_ref[...] = _head_sum(r * kd_sum * rk_ref[...], e, et) * v


def rwkv_prepare(code, lora, op, tm=256):
    W = RW_H * RW_N
    heads = jnp.arange(W, dtype=jnp.int32) // RW_N
    e = (heads[:, None] == jnp.arange(128, dtype=jnp.int32)[None, :]).astype(BF16)
    et = jnp.transpose(e)
    gup = jnp.pad(op['g_up'], ((0, 256 - op['g_up'].shape[0]), (0, 0)))
    row = lambda a: a.reshape(1, W)
    full2 = lambda shape: pl.BlockSpec(shape, lambda i: (0,) * len(shape))
    outs = pl.pallas_call(
        _rw_prep_kernel,
        out_shape=(jax.ShapeDtypeStruct((2, N_ROWS, W), F32), jax.ShapeDtypeStruct((2, N_ROWS, W), F32),
                   jax.ShapeDtypeStruct((2, N_ROWS, W), F32), jax.ShapeDtypeStruct((N_ROWS, W), F32),
                   jax.ShapeDtypeStruct((N_ROWS, W), F32), jax.ShapeDtypeStruct((N_ROWS, W), F32)),
        grid=(N_ROWS // tm,),
        in_specs=[pl.BlockSpec((tm, W), lambda i: (i, 0)), pl.BlockSpec((tm, W), lambda i: (i, 1)),
                  pl.BlockSpec((tm, W), lambda i: (i, 2)), pl.BlockSpec((tm, 384), lambda i: (i, 0)),
                  full2((2, W)), full2((2, 64, W)), full2((2, W)), full2((2, 64, W)), full2((256, W)),
                  full2((1, W)), full2((1, W)), full2((1, W)), full2((W, 128)), full2((128, W))],
        out_specs=(pl.BlockSpec((2, tm, W), lambda i: (0, i, 0)), pl.BlockSpec((2, tm, W), lambda i: (0, i, 0)),
                   pl.BlockSpec((2, tm, W), lambda i: (0, i, 0)), pl.BlockSpec((tm, W), lambda i: (i, 0)),
                   pl.BlockSpec((tm, W), lambda i: (i, 0)), pl.BlockSpec((tm, W), lambda i: (i, 0))),
        compiler_params=_cp(("parallel",)), name="rwkv_prepare",
    )(code, code, code, lora, op['w0'], op['w_up'], op['a0'], op['a_up'], gup, row(op['k_k']), row(op['k_a']),
      row(op['r_k']), e, et)
    return outs


def _tri_inv(n, eye, masks):
    bd = lambda a, b: _dot(a.astype(BF16), b.astype(BF16))
    d0 = jnp.where(masks[0], n, 0.0)
    d2 = bd(d0, d0)
    d4 = bd(d2, d2)
    t = bd(bd(eye + d0, eye + d2), eye + d4)
    for m in masks[1:]:
        e = jnp.where(m, n, 0.0)
        t = t + bd(t, bd(e, t))
    return t


def _rw_scan_kernel(r_ref, v_ref, lw_ref, kd_ref, be_ref, kap_ref, y_ref, st_ref):
    d = pl.program_id(0)
    c = pl.program_id(2)
    C = RW_CHUNK
    N = RW_N

    @pl.when(c == 0)
    def _():
        st_ref[...] = jnp.zeros_like(st_ref)

    isb = d == 1
    sgn = 1 - 2 * d
    ii = lax.broadcasted_iota(jnp.int32, (C, C), 0)
    jj = lax.broadcasted_iota(jnp.int32, (C, C), 1)
    dif = sgn * (ii - jj)
    incl = dif >= 0
    strict = dif > 0
    tri = incl.astype(BF16)
    eye = (ii == jj).astype(F32)
    blk = [(ii >> s) == (jj >> s) for s in range(3, C.bit_length() - 1)]
    masks = [blk[0]] + [blk[l] & ~blk[l - 1] for l in range(1, len(blk))] + [~blk[-1]]
    lw = lw_ref[...]
    cum = _dot2r(tri, lw)
    ec = jnp.exp(cum)
    en = jnp.exp(-cum)
    ea = jnp.exp(cum - lw)
    last = jnp.where(isb, cum[0:1, :], cum[C - 1:C, :])
    el = jnp.exp(last - cum)
    kap = kap_ref[...]
    r = r_ref[...]
    v = v_ref[...]
    a_t = -kap * ea
    r_t = r * ec
    b_t = be_ref[...] * en
    k_t = kd_ref[...] * en
    b_l = be_ref[...] * el
    k_l = kd_ref[...] * el
    pc = jnp.exp(last)
    H = range(RW_H)
    sl = [slice(h * N, (h + 1) * N) for h in H]
    bd = lambda a, b: _dot(a.astype(BF16), b.astype(BF16))
    tn = lambda a, b: lax.dot_general(a, b, (((0,), (0,)), ((), ())), preferred_element_type=F32)
    sc = [_dot_nt(jnp.concatenate([a_t[:, sl[h]], r_t[:, sl[h]]], axis=0).astype(BF16),
                  jnp.concatenate([b_t[:, sl[h]], k_t[:, sl[h]]], axis=0).astype(BF16)) for h in H]
    n_ab = [jnp.where(strict, sc[h][0:C, 0:C], 0.0) for h in H]
    a_ak = [jnp.where(strict, sc[h][0:C, C:2 * C], 0.0).astype(BF16) for h in H]
    m_rb = [jnp.where(incl, sc[h][C:2 * C, 0:C], 0.0).astype(BF16) for h in H]
    m_rk = [jnp.where(incl, sc[h][C:2 * C, C:2 * C], 0.0).astype(BF16) for h in H]
    vh = [v[:, sl[h]].astype(BF16) for h in H]
    d0 = [jnp.where(masks[0], n_ab[h], 0.0) for h in H]
    d2 = [bd(d0[h], d0[h]) for h in H]
    d4 = [bd(d2[h], d2[h]) for h in H]
    t = [bd(eye + d0[h], eye + d2[h]) for h in H]
    t = [bd(t[h], eye + d4[h]) for h in H]
    for m in masks[1:]:
        et = [bd(jnp.where(m, n_ab[h], 0.0), t[h]) for h in H]
        t = [t[h] + bd(t[h], et[h]) for h in H]
    av = [_dot(a_ak[h], vh[h]) for h in H]
    wub = [bd(t[h], jnp.concatenate([a_t[:, sl[h]], av[h]], axis=1)).astype(BF16) for h in H]
    mv = [_dot(m_rk[h], vh[h]) for h in H]
    kv = [tn(k_l[:, sl[h]].astype(BF16), vh[h]) for h in H]
    qy = [_dot(m_rb[h], wub[h]) + jnp.concatenate([r_t[:, sl[h]], mv[h]], axis=1) for h in H]
    pp = [tn(b_l[:, sl[h]].astype(BF16), wub[h]) + jnp.concatenate([eye[0:N, 0:N] * pc[:, sl[h]], kv[h]], axis=1)
          for h in H]
    h_old = [st_ref[h] for h in H]
    ys = [_dot3(qy[h][:, 0:N], h_old[h]) + qy[h][:, N:2 * N] for h in H]
    for h in H:
        st_ref[h] = _dot3(pp[h][:, 0:N], h_old[h]) + pp[h][:, N:2 * N]
    y_ref[...] = jnp.concatenate(ys, axis=1)


def _rw_rowblock(d, b, c):
    n_ctx = CTX // RW_CHUNK
    n_lat = SEQ // RW_CHUNK
    cc = jnp.where(d == 0, c, n_ctx - 1 - c)
    lc = jnp.where(d == 0, c - n_ctx, n_ctx + n_lat - 1 - c)
    return jnp.where(c < n_ctx, N_LAT // RW_CHUNK + b * n_ctx + cc, b * n_lat + lc)


def rwkv_scan(code, lw, kd, be, kap):
    W = RW_H * RW_N
    nch = (CTX + SEQ) // RW_CHUNK
    rb = lambda d, b, c: _rw_rowblock(d, b, c)
    return pl.pallas_call(
        _rw_scan_kernel,
        out_shape=jax.ShapeDtypeStruct((2, N_ROWS, W), F32),
        grid=(2, BATCH, nch),
        in_specs=[pl.BlockSpec((RW_CHUNK, W), lambda d, b, c: (rb(d, b, c), 0)),
                  pl.BlockSpec((RW_CHUNK, W), lambda d, b, c: (rb(d, b, c), 2)),
                  pl.BlockSpec((None, RW_CHUNK, W), lambda d, b, c: (d, rb(d, b, c), 0)),
                  pl.BlockSpec((None, RW_CHUNK, W), lambda d, b, c: (d, rb(d, b, c), 0)),
                  pl.BlockSpec((None, RW_CHUNK, W), lambda d, b, c: (d, rb(d, b, c), 0)),
                  pl.BlockSpec((RW_CHUNK, W), lambda d, b, c: (rb(d, b, c), 0))],
        out_specs=pl.BlockSpec((None, RW_CHUNK, W), lambda d, b, c: (d, rb(d, b, c), 0)),
        scratch_shapes=[pltpu.VMEM((RW_H, RW_N, RW_N), F32)],
        compiler_params=_cp(("parallel", "parallel", "arbitrary")), name="rwkv_scan",
    )(code, code, lw, kd, be, kap)


def _rw_out_kernel(yf_ref, yb_ref, bonus_ref, g_ref, lnw_ref, lnb_ref, e_ref, et_ref, o_ref):
    e, et = e_ref[...], et_ref[...]
    y = yf_ref[...] + yb_ref[...]
    mean = _head_sum(y, e, et) * (1.0 / RW_N)
    yc = y - mean
    var = _head_sum(yc * yc, e, et) * (1.0 / RW_N)
    yn = yc * lax.rsqrt(var + RW_GN_EPS) * lnw_ref[...] + lnb_ref[...]
    o_ref[...] = (yn + bonus_ref[...]) * g_ref[...]


def rwkv_output(y2, bonus, g, op, tm=256):
    W = RW_H * RW_N
    heads = jnp.arange(W, dtype=jnp.int32) // RW_N
    e = (heads[:, None] == jnp.arange(128, dtype=jnp.int32)[None, :]).astype(BF16)
    et = jnp.transpose(e)
    M = N_LAT
    return pl.pallas_call(
        _rw_out_kernel, out_shape=jax.ShapeDtypeStruct((M, W), F32), grid=(M // tm,),
        in_specs=[pl.BlockSpec((None, tm, W), lambda i: (0, i, 0)), pl.BlockSpec((None, tm, W), lambda i: (1, i, 0)),
                  pl.BlockSpec((tm, W), lambda i: (i, 0)), pl.BlockSpec((tm, W), lambda i: (i, 0)),
                  pl.BlockSpec((1, W), lambda i: (0, 0)), pl.BlockSpec((1, W), lambda i: (0, 0)),
                  pl.BlockSpec((W, 128), lambda i: (0, 0)), pl.BlockSpec((128, W), lambda i: (0, 0))],
        out_specs=pl.BlockSpec((tm, W), lambda i: (i, 0)),
        compiler_params=_cp(("parallel",)), name="rwkv_output",
    )(y2, y2, bonus, g, op['ln_w'].reshape(1, W), op['ln_b'].reshape(1, W), e, et)


AT_Q = RW_H * AT_HD
AT_KW = AT_KV * AT_HD
AT_TQ = 512
AT_TK = 768


def _rope_tables(tm):
    half = AT_HD // 2
    inv = 10000.0 ** (-jnp.arange(0, half, 2, dtype=F32) / half)
    pos = jnp.arange(SEQ, dtype=jnp.int32)
    row = (pos // GRID_W).astype(F32)[:, None] * inv
    col = (pos % GRID_W).astype(F32)[:, None] * inv
    cos_h = jnp.concatenate([jnp.cos(row), jnp.cos(row), jnp.cos(col), jnp.cos(col)], axis=1)
    sin_h = jnp.concatenate([-jnp.sin(row), jnp.sin(row), -jnp.sin(col), jnp.sin(col)], axis=1)
    cos_t = jnp.concatenate([jnp.tile(cos_h, (1, 2)), jnp.ones((tm, 128), F32)], axis=0)
    sin_t = jnp.concatenate([jnp.tile(sin_h, (1, 2)), jnp.zeros((tm, 128), F32)], axis=0)
    return cos_t, sin_t


def _rot_partner(x):
    q = AT_HD // 4
    w = x.shape[1]
    lane = lax.broadcasted_iota(jnp.int32, x.shape, 1)
    return jnp.where((lane % (2 * q)) < q, pltpu.roll(x, w - q, 1), pltpu.roll(x, q, 1))


def _at_prep_kernel(q_ref, k_ref, v_ref, cos_ref, sin_ref, qn_ref, kn_ref, e_ref, et_ref, qo_ref, ko_ref, vo_ref):
    e, et = e_ref[...], et_ref[...]
    cos2, sin2 = cos_ref[...], sin_ref[...]

    def norm_rope(x, gain, nrep):
        ms = _head_sum(x * x, e[:x.shape[1]], et[:, :x.shape[1]]) * (1.0 / AT_HD)
        xn = x * lax.rsqrt(ms + EPS) * gain
        cos = jnp.tile(cos2, (1, nrep))
        sin = jnp.tile(sin2, (1, nrep))
        return xn * cos + _rot_partner(xn) * sin

    qn = norm_rope(q_ref[...], qn_ref[...], AT_Q // 128) * (AT_HD ** -0.5 * math.log2(math.e))
    qo_ref[...] = jnp.transpose(qn).astype(BF16)
    ko_ref[...] = norm_rope(k_ref[...], kn_ref[...], AT_KW // 128).astype(BF16)
    vo_ref[...] = jnp.transpose(v_ref[...]).astype(BF16)


def attention_prepare(p, q_norm, k_norm, tm=256):
    cos_t, sin_t = _rope_tables(tm)
    heads = jnp.arange(AT_Q, dtype=jnp.int32) // AT_HD
    e = (heads[:, None] == jnp.arange(128, dtype=jnp.int32)[None, :]).astype(BF16)
    et = jnp.transpose(e)
    tab = lambda i: jnp.where(i * tm < N_LAT, ((i * tm) % SEQ) // tm, SEQ // tm)
    n_lat_t, n_seq_t, n_ctx_t = N_LAT // tm, SEQ // tm, CTX // tm
    kvb = lambda i: jnp.where(i < n_lat_t, (i // n_seq_t) * (n_seq_t + n_ctx_t) + n_ctx_t + i % n_seq_t,
                              ((i - n_lat_t) // n_ctx_t) * (n_seq_t + n_ctx_t) + (i - n_lat_t) % n_ctx_t)
    qcol = (3 * RW_H * RW_N) // AT_Q
    kcol = (3 * RW_H * RW_N + AT_Q) // AT_KW
    return pl.pallas_call(
        _at_prep_kernel,
        out_shape=(jax.ShapeDtypeStruct((AT_Q, N_ROWS), BF16), jax.ShapeDtypeStruct((N_ROWS, AT_KW), BF16),
                   jax.ShapeDtypeStruct((AT_KW, N_ROWS), BF16)),
        grid=(N_ROWS // tm,),
        in_specs=[pl.BlockSpec((tm, AT_Q), lambda i: (i, qcol)),
                  pl.BlockSpec((tm, AT_KW), lambda i: (i, kcol)),
                  pl.BlockSpec((tm, AT_KW), lambda i: (i, kcol + 1)),
                  pl.BlockSpec((tm, 128), lambda i: (tab(i), 0)),
                  pl.BlockSpec((tm, 128), lambda i: (tab(i), 0)),
                  pl.BlockSpec((1, AT_Q), lambda i: (0, 0)),
                  pl.BlockSpec((1, AT_KW), lambda i: (0, 0)),
                  pl.BlockSpec((AT_Q, 128), lambda i: (0, 0)),
                  pl.BlockSpec((128, AT_Q), lambda i: (0, 0))],
        out_specs=(pl.BlockSpec((AT_Q, tm), lambda i: (0, i)), pl.BlockSpec((tm, AT_KW), lambda i: (kvb(i), 0)),
                   pl.BlockSpec((AT_KW, tm), lambda i: (0, kvb(i)))),
        compiler_params=_cp(("parallel",)), name="attn_prepare",
    )(p, p, p, cos_t, sin_t, jnp.tile(q_norm, AT_Q // AT_HD).reshape(1, AT_Q),
      jnp.tile(k_norm, AT_KV).reshape(1, AT_KW), e, et)


def _flash_t_kernel(qt_ref, k_ref, vt_ref, o_ref, m_sc, l_sc, acc_sc):
    ki = pl.program_id(2)
    nq = AT_Q // AT_HD
    gq = nq // AT_KV

    @pl.when(ki == 0)
    def _():
        m_sc[...] = jnp.full_like(m_sc, -1e30)
        l_sc[...] = jnp.zeros_like(l_sc)
        acc_sc[...] = jnp.zeros_like(acc_sc)

    for g in range(AT_KV):
        kg = k_ref[:, g * AT_HD:(g + 1) * AT_HD]
        vtg1 = jnp.concatenate([vt_ref[g * AT_HD:(g + 1) * AT_HD, :],
                                jnp.ones((16, vt_ref.shape[1]), BF16)], axis=0)
        hs = range(g * gq, (g + 1) * gq)
        st = [_dot(kg, qt_ref[h * AT_HD:(h + 1) * AT_HD, :]) for h in hs]
        m_old = [m_sc[h] for h in hs]
        m_new = [jnp.maximum(m_old[i], jnp.max(st[i], axis=0, keepdims=True)) for i in range(gq)]
        alpha = [jnp.exp2(m_old[i] - m_new[i]) for i in range(gq)]
        pt = [jnp.exp2(st[i] - m_new[i]) for i in range(gq)]
        pv = [_dot(vtg1, pt[i].astype(BF16)) for i in range(gq)]
        for i, h in enumerate(hs):
            l_sc[h] = alpha[i] * l_sc[h] + pv[i][AT_HD:AT_HD + 1, :]
            m_sc[h] = m_new[i]
            rows = pl.ds(h * AT_HD, AT_HD)
            acc_sc[rows, :] = alpha[i] * acc_sc[rows, :] + pv[i][0:AT_HD, :]

    @pl.when(ki == pl.num_programs(2) - 1)
    def _():
        inv = jnp.concatenate([jnp.broadcast_to(1.0 / l_sc[h], (AT_HD, l_sc.shape[2])) for h in range(nq)], axis=0)
        o_ref[...] = jnp.transpose(acc_sc[...] * inv)


def flash_attention_t(qt, k, vt):
    nq = AT_Q // AT_HD
    nk = (CTX + SEQ) // AT_TK
    kv_rb = lambda b, ki: b * nk + ki
    return pl.pallas_call(
        _flash_t_kernel,
        out_shape=jax.ShapeDtypeStruct((N_LAT, AT_Q), F32),
        grid=(BATCH, SEQ // AT_TQ, nk),
        in_specs=[pl.BlockSpec((AT_Q, AT_TQ), lambda b, qi, ki: (0, b * (SEQ // AT_TQ) + qi)),
                  pl.BlockSpec((AT_TK, AT_KW), lambda b, qi, ki: (kv_rb(b, ki), 0)),
                  pl.BlockSpec((AT_KW, AT_TK), lambda b, qi, ki: (0, kv_rb(b, ki)))],
        out_specs=pl.BlockSpec((AT_TQ, AT_Q), lambda b, qi, ki: (b * (SEQ // AT_TQ) + qi, 0)),
        scratch_shapes=[pltpu.VMEM((nq, 1, AT_TQ), F32), pltpu.VMEM((nq, 1, AT_TQ), F32),
                        pltpu.VMEM((AT_Q, AT_TQ), F32)],
        compiler_params=_cp(("parallel", "parallel", "arbitrary")), name="flash_attention")(qt, k, vt)


OD_PAD_N = 4992


def odd_mixer(x, mods, g_pre, g_post, op):
    W = RW_H * RW_N
    w = op['w_in']
    c3 = 3 * W
    code_n = c3 + 64 + 64 + 160
    w_perm = jnp.concatenate([w[:, :c3], w[:, code_n:], w[:, c3:code_n],
                              jnp.zeros((D, OD_PAD_N - w.shape[1]), F32)], axis=1).astype(BF16)
    p = norm_mod_matmul(x, g_pre, mods, w_perm, 0, 1, tn=1664, name="odd_in_proj")
    mu = op['mu']
    taps = lambda m: jnp.stack([0.5 * m, 1.0 - m, 0.5 * m], axis=1)
    code = dwconv3(p, 0, c3, taps(mu[:c3]), jnp.zeros((c3,), F32), False, "rwkv_shift")
    lo_col = c3 + AT_Q + 2 * AT_KW
    mu_lo = jnp.pad(mu[c3:], (0, 384 - (code_n - c3)))
    lora = dwconv3(p, lo_col, 384, taps(mu_lo), jnp.zeros((384,), F32), False, "rwkv_shift_lora", cb=128)
    lw, kd, be, kap, g, bonus = rwkv_prepare(code, lora, op)
    y2 = rwkv_scan(code, lw, kd, be, kap)
    o_l = rwkv_output(y2, bonus, g, op)
    q, k, v = attention_prepare(p, op['q_norm'], op['k_norm'])
    a_l = flash_attention_t(q, k, v)
    return outproj_residual(o_l, a_l, op['w_out'].astype(BF16), x, g_post, mods, 2, name="odd_out_proj")


def _router_kernel(x_ref, g_ref, mod_ref, rw_ref, rb_ref, s1_ref, s3_ref, s2_ref, t_ref, idx_ref, wt_ref, sh_ref):
    x = x_ref[...]
    ms = jnp.mean(x * x, axis=-1, keepdims=True)
    t = x * lax.rsqrt(ms + EPS) * g_ref[...] * (1.0 + mod_ref[4:5, :]) + mod_ref[3:4, :]
    t_ref[...] = t
    tb = t.astype(BF16)
    sh_ref[...] = _dot((_silu(_dot(tb, s1_ref[...])) * _dot(tb, s3_ref[...])).astype(BF16), s2_ref[...])
    th, tl = _split(t)
    wh, wl = _split(rw_ref[...])
    lg = _dot_nt(wh, th) + (_dot_nt(wh, tl) + _dot_nt(wl, th))
    sc = _sigmoid(lg)
    sel = sc + rb_ref[...]
    tm = sel.shape[1]
    gsz = N_EXP // N_GRP
    ninf = -jnp.inf
    sel3 = sel.reshape(N_GRP, gsz, tm)
    i3 = lax.broadcasted_iota(jnp.int32, sel3.shape, 1)
    m1 = jnp.max(sel3, axis=1, keepdims=True)
    first = jnp.min(jnp.where(sel3 == m1, i3, gsz), axis=1, keepdims=True)
    m2 = jnp.max(jnp.where(i3 == first, ninf, sel3), axis=1, keepdims=True)
    grp = (m1 + m2).reshape(N_GRP, tm)
    gi = lax.broadcasted_iota(jnp.int32, grp.shape, 0)
    gmask = jnp.zeros(grp.shape, F32)
    for _ in range(TOPK_GRP):
        m = jnp.max(grp, axis=0, keepdims=True)
        pick = jnp.min(jnp.where(grp == m, gi, N_GRP), axis=0, keepdims=True)
        hit = gi == pick
        gmask = jnp.where(hit, 1.0, gmask)
        grp = jnp.where(hit, ninf, grp)
    emask = jnp.broadcast_to(gmask.reshape(N_GRP, 1, tm), (N_GRP, gsz, tm)).reshape(N_EXP, tm)
    msel = jnp.where(emask > 0.5, sel, ninf)
    ei = lax.broadcasted_iota(jnp.int32, msel.shape, 0)
    idxs, ws = [], []
    for _ in range(TOP_K):
        m = jnp.max(msel, axis=0, keepdims=True)
        pick = jnp.min(jnp.where(msel == m, ei, N_EXP), axis=0, keepdims=True)
        hit = ei == pick
        idxs.append(pick)
        ws.append(jnp.sum(jnp.where(hit, sc, 0.0), axis=0, keepdims=True))
        msel = jnp.where(hit, ninf, msel)
    w = jnp.concatenate(ws, axis=0)
    idx_ref[...] = jnp.concatenate(idxs, axis=0)
    wt_ref[...] = w / jnp.sum(w, axis=0, keepdims=True) * ROUTED_SCALE


def moe_router(x, M, g, mods, mp, tm=256):
    full = lambda shape: pl.BlockSpec(shape, lambda i: (0,) * len(shape))
    return pl.pallas_call(
        _router_kernel,
        out_shape=(jax.ShapeDtypeStruct((M, D), F32), jax.ShapeDtypeStruct((TOP_K, M), jnp.int32),
                   jax.ShapeDtypeStruct((TOP_K, M), F32), jax.ShapeDtypeStruct((M, D), F32)),
        grid=(M // tm,),
        in_specs=[pl.BlockSpec((tm, D), lambda i: (i, 0)), full((1, D)),
                  pl.BlockSpec((None, 6, D), lambda i: (_seq_of_rowblock(i, tm), 0, 0)),
                  full((N_EXP, D)), full((N_EXP, 1)), full((D, EXP_FF)), full((D, EXP_FF)), full((EXP_FF, D))],
        out_specs=(pl.BlockSpec((tm, D), lambda i: (i, 0)), pl.BlockSpec((TOP_K, tm), lambda i: (0, i)),
                   pl.BlockSpec((TOP_K, tm), lambda i: (0, i)), pl.BlockSpec((tm, D), lambda i: (i, 0))),
        compiler_params=_cp(("parallel",)), name="moe_router",
    )(x, g.reshape(1, D), mods, jnp.transpose(mp['router_w']), mp['router_bias'].reshape(N_EXP, 1),
      mp['s1'].astype(BF16), mp['s3'].astype(BF16), mp['s2'].astype(BF16))


def _gather_rows(idx_ref, n, src_hbm, dst, sem, slot):
    def body(r, carry):
        pltpu.make_async_copy(src_hbm.at[pl.ds(idx_ref[0, r], 1)], dst.at[slot, pl.ds(r, 1)], sem.at[slot]).start()
        return carry

    lax.fori_loop(0, n, body, 0)


def _wait_rows(n, src_hbm, dst, sem, slot):
    pltpu.make_async_copy(src_hbm.at[pl.ds(0, n)], dst.at[slot], sem.at[slot]).wait()


def _expert_kernel(be_ref, tok_ref, tokn_ref, w_ref, t_hbm, w1_ref, w3_ref, w2_ref, o_ref, xbuf, sem):
    i = pl.program_id(0)
    n = pl.num_programs(0)
    slot = i % 2

    @pl.when(i == 0)
    def _():
        _gather_rows(tok_ref, MOE_BLK, t_hbm, xbuf, sem, 0)

    @pl.when(i + 1 < n)
    def _():
        _gather_rows(tokn_ref, MOE_BLK, t_hbm, xbuf, sem, 1 - slot)

    _wait_rows(MOE_BLK, t_hbm, xbuf, sem, slot)
    xb = xbuf[slot].astype(BF16)
    h = _silu(_dot(xb, w1_ref[...])) * _dot(xb, w3_ref[...])
    o_ref[...] = _dot(h.astype(BF16), w2_ref[...]) * w_ref[...]


def moe_experts(t, buf_tok, buf_w, block_e, w1, w3, w2):
    nb = block_e.shape[0]
    tok3 = buf_tok.reshape(nb, 1, MOE_BLK)
    nxt = lambda i, be: (jnp.minimum(i + 1, nb - 1), 0, 0)
    grid_spec = pltpu.PrefetchScalarGridSpec(
        num_scalar_prefetch=1, grid=(nb,),
        in_specs=[pl.BlockSpec((None, 1, MOE_BLK), lambda i, be: (i, 0, 0), memory_space=pltpu.SMEM),
                  pl.BlockSpec((None, 1, MOE_BLK), nxt, memory_space=pltpu.SMEM),
                  pl.BlockSpec((MOE_BLK, 1), lambda i, be: (i, 0)),
                  pl.BlockSpec(memory_space=pl.ANY),
                  pl.BlockSpec((None, D, EXP_FF), lambda i, be: (be[i], 0, 0)),
                  pl.BlockSpec((None, D, EXP_FF), lambda i, be: (be[i], 0, 0)),
                  pl.BlockSpec((None, EXP_FF, D), lambda i, be: (be[i], 0, 0))],
        out_specs=pl.BlockSpec((MOE_BLK, D), lambda i, be: (i, 0)),
        scratch_shapes=[pltpu.VMEM((2, MOE_BLK, D), F32), pltpu.SemaphoreType.DMA((2,))])
    return pl.pallas_call(
        _expert_kernel, out_shape=jax.ShapeDtypeStruct((nb * MOE_BLK, D), F32), grid_spec=grid_spec,
        compiler_params=_cp(("arbitrary",)), name="moe_experts",
    )(block_e, tok3, tok3, buf_w.reshape(nb * MOE_BLK, 1), t, w1, w3, w2)


MOE_TT = 32


def _combine_kernel(dst_ref, dstn_ref, ys_hbm, sh_ref, x_ref, g_ref, mod_ref, o_ref, buf, sem):
    i = pl.program_id(0)
    n = pl.num_programs(0)
    slot = i % 2
    nrow = MOE_TT * TOP_K

    @pl.when(i == 0)
    def _():
        _gather_rows(dst_ref, nrow, ys_hbm, buf, sem, 0)

    @pl.when(i + 1 < n)
    def _():
        _gather_rows(dstn_ref, nrow, ys_hbm, buf, sem, 1 - slot)

    _wait_rows(nrow, ys_hbm, buf, sem, slot)
    f = sh_ref[...]
    for k in range(TOP_K):
        f = f + buf[slot, k * MOE_TT:(k + 1) * MOE_TT, :]
    ms = jnp.mean(f * f, axis=-1, keepdims=True)
    o_ref[...] = x_ref[...] + mod_ref[5:6, :] * (f * lax.rsqrt(ms + EPS) * g_ref[...])


def moe_combine(ys, dest, sh, x, M, g, mods):
    nt = M // MOE_TT
    nrow = MOE_TT * TOP_K
    d3 = dest.reshape(nt, MOE_TT, TOP_K).transpose(0, 2, 1).reshape(nt, 1, nrow)
    nxt = lambda i: (jnp.minimum(i + 1, nt - 1), 0, 0)
    return pl.pallas_call(
        _combine_kernel, out_shape=jax.ShapeDtypeStruct((M, D), F32), grid=(nt,),
        in_specs=[pl.BlockSpec((None, 1, nrow), lambda i: (i, 0, 0), memory_space=pltpu.SMEM),
                  pl.BlockSpec((None, 1, nrow), nxt, memory_space=pltpu.SMEM),
                  pl.BlockSpec(memory_space=pl.ANY),
                  pl.BlockSpec((MOE_TT, D), lambda i: (i, 0)),
                  pl.BlockSpec((MOE_TT, D), lambda i: (i, 0)),
                  pl.BlockSpec((1, D), lambda i: (0, 0)),
                  pl.BlockSpec((None, 6, D), lambda i: (_seq_of_rowblock(i, MOE_TT), 0, 0))],
        out_specs=pl.BlockSpec((MOE_TT, D), lambda i: (i, 0)),
        scratch_shapes=[pltpu.VMEM((2, nrow, D), F32), pltpu.SemaphoreType.DMA((2,))],
        compiler_params=_cp(("arbitrary",)), name="moe_combine",
    )(d3, d3, ys, sh, x, g.reshape(1, D), mods)


def moe_layer(x, M, g_pre, g_post, mods, mp):
    t, idx_t, wts_t, sh = moe_router(x, M, g_pre, mods, mp)
    mk = M * TOP_K
    nb = -(-(mk + N_EXP * (MOE_BLK - 1)) // MOE_BLK)
    flat_e = jnp.transpose(idx_t).reshape(mk)
    flat_w = jnp.transpose(wts_t).reshape(mk)
    onehot = (flat_e[:, None] == jnp.arange(N_EXP, dtype=jnp.int32)[None, :]).astype(jnp.int32)
    csum = jnp.cumsum(onehot, axis=0)
    rank = jnp.take_along_axis(csum, flat_e[:, None], axis=1)[:, 0] - 1
    counts = csum[-1]
    padded = (counts + MOE_BLK - 1) // MOE_BLK * MOE_BLK
    pend = jnp.cumsum(padded)
    dest = (pend - padded)[flat_e] + rank
    flat_tok = jnp.arange(mk, dtype=jnp.int32) // TOP_K
    buf_tok = jnp.zeros((nb * MOE_BLK,), jnp.int32).at[dest].set(flat_tok)
    buf_w = jnp.zeros((nb * MOE_BLK,), F32).at[dest].set(flat_w)
    block_e = jnp.minimum(jnp.searchsorted(pend, jnp.arange(nb, dtype=jnp.int32) * MOE_BLK, side='right'),
                          N_EXP - 1).astype(jnp.int32)
    ys = moe_experts(t, buf_tok, buf_w, block_e, mp['w1'].astype(BF16), mp['w3'].astype(BF16),
                     mp['w2'].astype(BF16))
    return moe_combine(ys, dest.astype(jnp.int32), sh, x, M, g_post, mods)


MOE_T = 256
MOE_CAP = 64
MOE_EPS = 4


def _router2_kernel(x_ref, g_ref, mod_ref, rw_ref, rb_ref, s1_ref, s3_ref, s2_ref, t_ref, wt_ref, cnt_ref, sh_ref):
    x = x_ref[...]
    ms = jnp.mean(x * x, axis=-1, keepdims=True)
    t = x * lax.rsqrt(ms + EPS) * g_ref[...] * (1.0 + mod_ref[4:5, :]) + mod_ref[3:4, :]
    tb = t.astype(BF16)
    t_ref[...] = tb
    sh_ref[...] = _dot((_silu(_dot(tb, s1_ref[...])) * _dot(tb, s3_ref[...])).astype(BF16), s2_ref[...])
    th, tl = _split(t)
    wh, wl = _split(rw_ref[...])
    lg = _dot_nt(wh, th) + (_dot_nt(wh, tl) + _dot_nt(wl, th))
    sc = _sigmoid(lg)
    sel = sc + rb_ref[...]
    tm = sel.shape[1]
    gsz = N_EXP // N_GRP
    ninf = -jnp.inf
    sel3 = sel.reshape(N_GRP, gsz, tm)
    i3 = lax.broadcasted_iota(jnp.int32, sel3.shape, 1)
    m1 = jnp.max(sel3, axis=1, keepdims=True)
    first = jnp.min(jnp.where(sel3 == m1, i3, gsz), axis=1, keepdims=True)
    m2 = jnp.max(jnp.where(i3 == first, ninf, sel3), axis=1, keepdims=True)
    grp = (m1 + m2).reshape(N_GRP, tm)
    gi = lax.broadcasted_iota(jnp.int32, grp.shape, 0)
    gmask = jnp.zeros(grp.shape, F32)
    for _ in range(TOPK_GRP):
        m = jnp.max(grp, axis=0, keepdims=True)
        pick = jnp.min(jnp.where(grp == m, gi, N_GRP), axis=0, keepdims=True)
        hit = gi == pick
        gmask = jnp.where(hit, 1.0, gmask)
        grp = jnp.where(hit, ninf, grp)
    emask = jnp.broadcast_to(gmask.reshape(N_GRP, 1, tm), (N_GRP, gsz, tm)).reshape(N_EXP, tm)
    msel = jnp.where(emask > 0.5, sel, ninf)
    ei = lax.broadcasted_iota(jnp.int32, msel.shape, 0)
    chosen = jnp.zeros(msel.shape, F32)
    for _ in range(TOP_K):
        m = jnp.max(msel, axis=0, keepdims=True)
        pick = jnp.min(jnp.where(msel == m, ei, N_EXP), axis=0, keepdims=True)
        hit = ei == pick
        chosen = jnp.where(hit, 1.0, chosen)
        msel = jnp.where(hit, ninf, msel)
    w = chosen * sc
    wt = w / jnp.sum(w, axis=0, keepdims=True) * ROUTED_SCALE
    wt_ref[...] = wt
    cnt_ref[...] = jnp.sum((wt > 0.0).astype(F32), axis=1, keepdims=True).astype(jnp.int32)


def moe_router2(x, M, g, mods, mp):
    tm = MOE_T
    full = lambda shape: pl.BlockSpec(shape, lambda i: (0,) * len(shape))
    return pl.pallas_call(
        _router2_kernel,
        out_shape=(jax.ShapeDtypeStruct((M, D), BF16), jax.ShapeDtypeStruct((M // tm, N_EXP, tm), F32),
                   jax.ShapeDtypeStruct((M // tm, N_EXP, 1), jnp.int32), jax.ShapeDtypeStruct((M, D), F32)),
        grid=(M // tm,),
        in_specs=[pl.BlockSpec((tm, D), lambda i: (i, 0)), full((1, D)),
                  pl.BlockSpec((None, 6, D), lambda i: (_seq_of_rowblock(i, tm), 0, 0)),
                  full((N_EXP, D)), full((N_EXP, 1)), full((D, EXP_FF)), full((D, EXP_FF)), full((EXP_FF, D))],
        out_specs=(pl.BlockSpec((tm, D), lambda i: (i, 0)), pl.BlockSpec((None, N_EXP, tm), lambda i: (i, 0, 0)),
                   pl.BlockSpec((None, N_EXP, 1), lambda i: (i, 0, 0)), pl.BlockSpec((tm, D), lambda i: (i, 0))),
        compiler_params=_cp(("parallel",)), name="moe_router",
    )(x, g.reshape(1, D), mods, jnp.transpose(mp['router_w']), mp['router_bias'].reshape(N_EXP, 1),
      mp['s1'].astype(BF16), mp['s3'].astype(BF16), mp['s2'].astype(BF16))


def _moe2_kernel(cnt_ref, t_ref, wt_ref, sh_ref, x_ref, g_ref, mod_ref, w1_ref, w3_ref, w2_ref, o_ref,
                 rank_sc, acc_sc):
    i = pl.program_id(0)
    eb = pl.program_id(1)
    T = MOE_T

    @pl.when(eb == 0)
    def _():
        picked = (wt_ref[...] > 0.0).astype(BF16)
        before = (lax.broadcasted_iota(jnp.int32, (T, T), 0) < lax.broadcasted_iota(jnp.int32, (T, T), 1))
        rank_sc[...] = _dot(picked, before.astype(BF16))
        acc_sc[...] = jnp.zeros_like(acc_sc)

    slot = lax.broadcasted_iota(jnp.int32, (MOE_CAP, T), 0).astype(F32)
    for j in range(MOE_EPS):
        e = eb * MOE_EPS + j
        n_tok = cnt_ref[i * N_EXP + e]
        w_row = wt_ref[pl.ds(e, 1), :]
        r_row = rank_sc[pl.ds(e, 1), :]

        def chunk(ci, carry, j=j, w_row=w_row, r_row=r_row):
            hit = ((r_row - (ci * MOE_CAP).astype(F32)) == slot) & (w_row > 0.0)
            pb = hit.astype(F32).astype(BF16)
            xg = _dot(pb, t_ref[...]).astype(BF16)
            h = _silu(_dot(xg, w1_ref[j])) * _dot(xg, w3_ref[j])
            y = _dot(h.astype(BF16), w2_ref[j])
            w_slot = jnp.sum(jnp.where(hit, w_row, 0.0), axis=1, keepdims=True)
            yw = (y * w_slot).astype(BF16)
            acc_sc[...] += lax.dot_general(pb, yw, (((0,), (0,)), ((), ())), preferred_element_type=F32)
            return carry

        lax.fori_loop(0, (n_tok + MOE_CAP - 1) // MOE_CAP, chunk, 0)

    @pl.when(eb == pl.num_programs(1) - 1)
    def _():
        f = acc_sc[...] + sh_ref[...]
        ms = jnp.mean(f * f, axis=-1, keepdims=True)
        o_ref[...] = x_ref[...] + mod_ref[5:6, :] * (f * lax.rsqrt(ms + EPS) * g_ref[...])


def moe_layer2(x, M, g_pre, g_post, mods, mp):
    t, wt, cnt, sh = moe_router2(x, M, g_pre, mods, mp)
    T = MOE_T
    grid_spec = pltpu.PrefetchScalarGridSpec(
        num_scalar_prefetch=1, grid=(M // T, N_EXP // MOE_EPS),
        in_specs=[pl.BlockSpec((T, D), lambda i, e, c: (i, 0)),
                  pl.BlockSpec((N_EXP, T), lambda i, e, c: (0, i)),
                  pl.BlockSpec((T, D), lambda i, e, c: (i, 0)),
                  pl.BlockSpec((T, D), lambda i, e, c: (i, 0)),
                  pl.BlockSpec((1, D), lambda i, e, c: (0, 0)),
                  pl.BlockSpec((None, 6, D), lambda i, e, c: (_seq_of_rowblock(i, T), 0, 0)),
                  pl.BlockSpec((MOE_EPS, D, EXP_FF), lambda i, e, c: (e, 0, 0)),
                  pl.BlockSpec((MOE_EPS, D, EXP_FF), lambda i, e, c: (e, 0, 0)),
                  pl.BlockSpec((MOE_EPS, EXP_FF, D), lambda i, e, c: (e, 0, 0))],
        out_specs=pl.BlockSpec((T, D), lambda i, e, c: (i, 0)),
        scratch_shapes=[pltpu.VMEM((N_EXP, T), F32), pltpu.VMEM((T, D), F32)])
    return pl.pallas_call(
        _moe2_kernel, out_shape=jax.ShapeDtypeStruct((M, D), F32), grid_spec=grid_spec,
        compiler_params=_cp(("parallel", "arbitrary")), name="moe_experts",
    )(cnt.reshape(-1), t, wt, sh, x, g_post.reshape(1, D), mods, mp['w1'].astype(BF16), mp['w3'].astype(BF16),
      mp['w2'].astype(BF16))


def _moe3_kernel(cnt_ref, t_ref, wt_ref, w1_ref, w3_ref, w2_ref, o_ref, rank_sc, *, nsub):
    i = pl.program_id(0)
    eb = pl.program_id(1)
    T, CAP, EPS = MOE_T, MOE_CAP, MOE_EPS

    @pl.when(eb == 0)
    def _():
        before = (lax.broadcasted_iota(jnp.int32, (T, T), 0) < lax.broadcasted_iota(jnp.int32, (T, T), 1))
        before = before.astype(BF16)
        for s in range(nsub):
            rank_sc[s] = _dot((wt_ref[s] > 0.0).astype(BF16), before)
        o_ref[...] = jnp.zeros_like(o_ref)

    slot = lax.broadcasted_iota(jnp.int32, (CAP, T), 0).astype(F32)

    def one_hot(s, e, first_slot):
        w_row = wt_ref[s, pl.ds(e, 1), :]
        r_row = rank_sc[s, pl.ds(e, 1), :]
        hit = ((r_row - first_slot) == slot) & (w_row > 0.0)
        w_slot = jnp.sum(jnp.where(hit, w_row, 0.0), axis=1, keepdims=True)
        return hit.astype(F32).astype(BF16), w_slot

    def swiglu(xg, j):
        h = _silu(_dot(xg, w1_ref[j])) * _dot(xg, w3_ref[j])
        return _dot(h.astype(BF16), w2_ref[j])

    hot = [[one_hot(s, eb * EPS + j, 0.0) for j in range(EPS)] for s in range(nsub)]
    pb = [jnp.concatenate([hot[s][j][0] for j in range(EPS)], axis=0) for s in range(nsub)]
    xg = [_dot(pb[s], t_ref[s * T:(s + 1) * T, :]).astype(BF16) for s in range(nsub)]
    y = [swiglu(jnp.concatenate([xg[s][j * CAP:(j + 1) * CAP] for s in range(nsub)], axis=0), j)
         for j in range(EPS)]
    for s in range(nsub):
        yw = jnp.concatenate([y[j][s * CAP:(s + 1) * CAP] * hot[s][j][1] for j in range(EPS)], axis=0)
        o_ref[s * T:(s + 1) * T, :] += lax.dot_general(pb[s], yw.astype(BF16), (((0,), (0,)), ((), ())),
                                                       preferred_element_type=F32)

    def pair(idx, carry):
        s = idx // EPS
        j = idx % EPS
        e = eb * EPS + j
        n_tok = cnt_ref[(i * nsub + s) * N_EXP + e]
        rows = pl.ds(pl.multiple_of(s * T, T), T)

        def chunk(ci, c2):
            p1, w_slot = one_hot(s, e, (ci * CAP).astype(F32))
            yw = (swiglu(_dot(p1, t_ref[rows, :]).astype(BF16), j) * w_slot).astype(BF16)
            o_ref[rows, :] += lax.dot_general(p1, yw, (((0,), (0,)), ((), ())), preferred_element_type=F32)
            return c2

        lax.fori_loop(1, (n_tok + CAP - 1) // CAP, chunk, 0)
        return carry

    lax.fori_loop(0, nsub * EPS, pair, 0)


def _moe_out_kernel(r_ref, sh_ref, x_ref, g_ref, mod_ref, o_ref):
    f = r_ref[...] + sh_ref[...]
    ms = jnp.mean(f * f, axis=-1, keepdims=True)
    o_ref[...] = x_ref[...] + mod_ref[5:6, :] * (f * lax.rsqrt(ms + EPS) * g_ref[...])


def moe_layer3(x, M, g_pre, g_post, mods, mp, nsub):
    t, wt, cnt, sh = moe_router2(x, M, g_pre, mods, mp)
    T = MOE_T
    TS = nsub * T
    grid_spec = pltpu.PrefetchScalarGridSpec(
        num_scalar_prefetch=1, grid=(M // TS, N_EXP // MOE_EPS),
        in_specs=[pl.BlockSpec((TS, D), lambda i, e, c: (i, 0)),
                  pl.BlockSpec((nsub, N_EXP, T), lambda i, e, c: (i, 0, 0)),
                  pl.BlockSpec((MOE_EPS, D, EXP_FF), lambda i, e, c: (e, 0, 0)),
                  pl.BlockSpec((MOE_EPS, D, EXP_FF), lambda i, e, c: (e, 0, 0)),
                  pl.BlockSpec((MOE_EPS, EXP_FF, D), lambda i, e, c: (e, 0, 0))],
        out_specs=pl.BlockSpec((TS, D), lambda i, e, c: (i, 0)),
        scratch_shapes=[pltpu.VMEM((nsub, N_EXP, T), F32)])
    routed = pl.pallas_call(
        functools.partial(_moe3_kernel, nsub=nsub), out_shape=jax.ShapeDtypeStruct((M, D), F32),
        grid_spec=grid_spec, compiler_params=_cp(("parallel", "arbitrary")), name="moe_experts",
    )(cnt.reshape(-1), t, wt, mp['w1'].astype(BF16), mp['w3'].astype(BF16), mp['w2'].astype(BF16))
    tm = 512
    return pl.pallas_call(
        _moe_out_kernel, out_shape=jax.ShapeDtypeStruct((M, D), F32), grid=(M // tm,),
        in_specs=[pl.BlockSpec((tm, D), lambda i: (i, 0)), pl.BlockSpec((tm, D), lambda i: (i, 0)),
                  pl.BlockSpec((tm, D), lambda i: (i, 0)), pl.BlockSpec((1, D), lambda i: (0, 0)),
                  pl.BlockSpec((None, 6, D), lambda i: (_seq_of_rowblock(i, tm), 0, 0))],
        out_specs=pl.BlockSpec((tm, D), lambda i: (i, 0)),
        compiler_params=_cp(("parallel",)), name="moe_output")(routed, sh, x, g_post.reshape(1, D), mods)


def kernel(x, c, ctx, c_ctx, mod_w, mod_b, norm_mix_pre, norm_mix_post, norm_ffn_pre, norm_ffn_post, router_w, router_bias, expert_w1, expert_w3, expert_w2, shared_w1, shared_w3, shared_w2, ev_w_in, ev_w_out, ssd_conv_w, ssd_conv_b, ssd_dt_bias, ssd_a_log, ssd_d, ssd_norm_w, hy_conv_w, hy_conv_b, hy_mlp_w0, hy_mlp_b0, hy_freq0, hy_mlp_w1, hy_mlp_b1, hy_freq1, hy_mlp_w2, hy_bias, od_w_in, od_w_out, rw_mu, rw_w0, rw_w_up, rw_a0, rw_a_up, rw_g_up, rw_k_k, rw_k_a, rw_r_k, rw_ln_w, rw_ln_b, at_q_norm, at_k_norm):
    xs = jnp.concatenate([x.reshape(N_LAT, D), ctx.reshape(BATCH * CTX, D)], axis=0)
    cvecs = jnp.zeros((8, D), F32).at[0:BATCH].set(c).at[BATCH].set(c_ctx)
    assert mod_w.shape[0] == 2, "one even (SSD | Hyena) layer followed by one odd (RWKV | attention) layer"

    def moe_params(i):
        return dict(router_w=router_w[i], router_bias=router_bias[i], w1=expert_w1[i], w3=expert_w3[i],
                    w2=expert_w2[i], s1=shared_w1[i], s3=shared_w3[i], s2=shared_w2[i])

    mods = modulation(cvecs, mod_w[0], mod_b[0])[:BATCH + 1].reshape(BATCH + 1, 6, D)
    ep = dict(w_in=ev_w_in[0], w_out=ev_w_out[0], ssd_conv_w=ssd_conv_w[0], ssd_conv_b=ssd_conv_b[0],
              ssd_dt_bias=ssd_dt_bias[0], ssd_a_log=ssd_a_log[0], ssd_d=ssd_d[0], ssd_norm_w=ssd_norm_w[0],
              hy_conv_w=hy_conv_w[0], hy_conv_b=hy_conv_b[0], hy_mlp_w0=hy_mlp_w0[0], hy_mlp_b0=hy_mlp_b0[0],
              hy_freq0=hy_freq0[0], hy_mlp_w1=hy_mlp_w1[0], hy_mlp_b1=hy_mlp_b1[0], hy_freq1=hy_freq1[0],
              hy_mlp_w2=hy_mlp_w2[0], hy_bias=hy_bias[0])
    xs = even_mixer(xs, mods, norm_mix_pre[0], norm_mix_post[0], ep)
    xs = moe_layer3(xs, N_ROWS, norm_ffn_pre[0], norm_ffn_post[0], mods, moe_params(0), 6)
    mods = modulation(cvecs, mod_w[1], mod_b[1])[:BATCH + 1].reshape(BATCH + 1, 6, D)
    op = dict(w_in=od_w_in[0], w_out=od_w_out[0], mu=rw_mu[0], w0=rw_w0[0], w_up=rw_w_up[0], a0=rw_a0[0],
              a_up=rw_a_up[0], g_up=rw_g_up[0], k_k=rw_k_k[0], k_a=rw_k_a[0], r_k=rw_r_k[0], ln_w=rw_ln_w[0],
              ln_b=rw_ln_b[0], q_norm=at_q_norm[0], k_norm=at_k_norm[0])
    xl = odd_mixer(xs, mods, norm_mix_pre[1], norm_mix_post[1], op)
    xl = moe_layer3(xl, N_LAT, norm_ffn_pre[1], norm_ffn_post[1], mods, moe_params(1), 8)
    return xl.reshape(BATCH, SEQ, D)
```

```python
import functools
import math

import numpy as np
import jax
import jax.numpy as jnp
from jax import lax
from jax.experimental import pallas as pl
from jax.experimental.pallas import tpu as pltpu

F32 = jnp.float32
BF16 = jnp.bfloat16

D = 1024
BATCH = 2
SEQ = 8192
CTX = 256
N_LAT = BATCH * SEQ
N_ROWS = N_LAT + BATCH * CTX
EPS = 1e-6
GRID_W = 64

SSD_HEADS = 16
SSD_P = 64
SSD_G = 2
SSD_S = 128
SSD_Q = 128
HY_W = 1024
HY_EMB = 33
HY_HID = 64

RW_H = 16
RW_N = 64
RW_CHUNK = 128
RW_GN_EPS = 64e-5

AT_KV = 4
AT_HD = 64

N_EXP = 64
TOP_K = 8
N_GRP = 8
TOPK_GRP = 4
EXP_FF = 256
ROUTED_SCALE = 2.5
MOE_BLK = 128

VMEM_LIMIT = 56 * 1024 * 1024


def _cp(sem, vmem=None):
    return pltpu.CompilerParams(dimension_semantics=sem, vmem_limit_bytes=vmem or VMEM_LIMIT)


def _dot(a, b):
    return jnp.dot(a, b, preferred_element_type=F32)


def _dot_nt(a, b):
    return lax.dot_general(a, b, (((1,), (1,)), ((), ())), preferred_element_type=F32)


def _split(x):
    hi = x.astype(BF16)
    lo = (x - hi.astype(F32)).astype(BF16)
    return hi, lo


def _dot3(a, b):
    ah, al = _split(a)
    bh, bl = _split(b)
    return _dot(ah, bh) + (_dot(ah, bl) + _dot(al, bh))


def _dot2l(a, b):
    ah, al = _split(a)
    return _dot(ah, b) + _dot(al, b)


def _dot2r(a, b):
    bh, bl = _split(b)
    return _dot(a, bh) + _dot(a, bl)


def _silu(x):
    return x * (1.0 / (1.0 + jnp.exp(-x)))


def _sigmoid(x):
    return 1.0 / (1.0 + jnp.exp(-x))


def _softplus(x):
    return jnp.maximum(x, 0.0) + jnp.log(1.0 + jnp.exp(-jnp.abs(x)))


def _seq_of_rowblock(i, tm):
    return jnp.minimum((i * tm) // SEQ, 2)


def _mm_kernel(a_ref, b_ref, o_ref, *, passes):
    a = a_ref[...]
    b = b_ref[...]
    if passes == 3:
        o_ref[...] = _dot3(a.astype(F32), b.astype(F32))
    else:
        o_ref[...] = _dot(a.astype(BF16), b.astype(BF16))


def matmul(a, b, tm, tn, passes=1, name="mm"):
    M, K = a.shape
    N = b.shape[1]
    return pl.pallas_call(
        functools.partial(_mm_kernel, passes=passes),
        out_shape=jax.ShapeDtypeStruct((M, N), F32),
        grid=(M // tm, N // tn),
        in_specs=[pl.BlockSpec((tm, K), lambda i, j: (i, 0)),
                  pl.BlockSpec((K, tn), lambda i, j: (0, j))],
        out_specs=pl.BlockSpec((tm, tn), lambda i, j: (i, j)),
        compiler_params=_cp(("parallel", "parallel")), name=name)(a, b)


def _nmm_kernel(x_ref, g_ref, mod_ref, w_ref, o_ref, a_sc, *, shift_i, scale_i):
    @pl.when(pl.program_id(1) == 0)
    def _():
        x = x_ref[...]
        ms = jnp.mean(x * x, axis=-1, keepdims=True)
        y = x * lax.rsqrt(ms + EPS) * g_ref[...]
        h = y * (1.0 + mod_ref[scale_i:scale_i + 1, :]) + mod_ref[shift_i:shift_i + 1, :]
        a_sc[...] = h.astype(BF16)

    o_ref[...] = _dot(a_sc[...], w_ref[...])


def norm_mod_matmul(x, g, mods, w, shift_i, scale_i, tm=512, tn=None, name="nmm"):
    M = x.shape[0]
    N = w.shape[1]
    tn = tn or N
    return pl.pallas_call(
        functools.partial(_nmm_kernel, shift_i=shift_i, scale_i=scale_i),
        out_shape=jax.ShapeDtypeStruct((M, N), F32),
        grid=(M // tm, N // tn),
        in_specs=[pl.BlockSpec((tm, D), lambda i, j: (i, 0)),
                  pl.BlockSpec((1, D), lambda i, j: (0, 0)),
                  pl.BlockSpec((None, 6, D), lambda i, j: (_seq_of_rowblock(i, tm), 0, 0)),
                  pl.BlockSpec((D, tn), lambda i, j: (0, j))],
        out_specs=pl.BlockSpec((tm, tn), lambda i, j: (i, j)),
        scratch_shapes=[pltpu.VMEM((tm, D), BF16)],
        compiler_params=_cp(("parallel", "arbitrary")), name=name)(x, g.reshape(1, D), mods, w)


def _outproj_kernel(a1_ref, a2_ref, w_ref, x_ref, g_ref, mod_ref, o_ref, *, gate_i):
    y = _dot(a1_ref[...].astype(BF16), w_ref[0:D, :]) + _dot(a2_ref[...].astype(BF16), w_ref[D:2 * D, :])
    ms = jnp.mean(y * y, axis=-1, keepdims=True)
    o_ref[...] = x_ref[...] + mod_ref[gate_i:gate_i + 1, :] * (y * lax.rsqrt(ms + EPS) * g_ref[...])


def outproj_residual(a1, a2, w, x, g, mods, gate_i, tm=256, name="outproj"):
    M = a1.shape[0]
    return pl.pallas_call(
        functools.partial(_outproj_kernel, gate_i=gate_i),
        out_shape=jax.ShapeDtypeStruct((M, D), F32),
        grid=(M // tm,),
        in_specs=[pl.BlockSpec((tm, D), lambda i: (i, 0)),
                  pl.BlockSpec((tm, D), lambda i: (i, 0)),
                  pl.BlockSpec((2 * D, D), lambda i: (0, 0)),
                  pl.BlockSpec((tm, D), lambda i: (i, 0)),
                  pl.BlockSpec((1, D), lambda i: (0, 0)),
                  pl.BlockSpec((None, 6, D), lambda i: (_seq_of_rowblock(i, tm), 0, 0))],
        out_specs=pl.BlockSpec((tm, D), lambda i: (i, 0)),
        compiler_params=_cp(("parallel",)), name=name)(a1, a2, w, x, g.reshape(1, D), mods)


def _mod_kernel(c_ref, w_ref, b_ref, o_ref):
    o_ref[...] = _dot3(_silu(c_ref[...]), w_ref[...]) + b_ref[...]


def modulation(cvecs, w, b):
    N = w.shape[1]
    tn = 1024
    return pl.pallas_call(
        _mod_kernel, out_shape=jax.ShapeDtypeStruct((8, N), F32), grid=(N // tn,),
        in_specs=[pl.BlockSpec((8, D), lambda j: (0, 0)),
                  pl.BlockSpec((D, tn), lambda j: (0, j)),
                  pl.BlockSpec((1, tn), lambda j: (0, j))],
        out_specs=pl.BlockSpec((8, tn), lambda j: (0, j)),
        compiler_params=_cp(("parallel",)), name="modulation")(cvecs, w, b.reshape(1, N))


CONV_TM = 256


def _conv3_kernel(x_ref, prev_ref, next_ref, w_ref, b_ref, o_ref, *, act):
    tm = CONV_TM
    row0 = pl.program_id(0) * tm
    seq_len = jnp.where(row0 < N_LAT, SEQ, CTX)
    pos = jnp.where(row0 < N_LAT, row0 % SEQ, (row0 - N_LAT) % CTX)
    cur = x_ref[...]
    rows = lax.broadcasted_iota(jnp.int32, cur.shape, 0)
    prev_row = prev_ref[7:8, :] * (pos > 0).astype(F32)
    next_row = next_ref[0:1, :] * (pos + tm < seq_len).astype(F32)
    xm1 = jnp.where(rows == 0, prev_row, pltpu.roll(cur, 1, 0))
    xp1 = jnp.where(rows == tm - 1, next_row, pltpu.roll(cur, tm - 1, 0))
    y = xm1 * w_ref[0:1, :] + cur * w_ref[1:2, :] + xp1 * w_ref[2:3, :] + b_ref[...]
    o_ref[...] = _silu(y) if act else y


def dwconv3(p, col0, ncols, w, b, act, name, cb=1024):
    tm = CONV_TM
    cb = math.gcd(cb, math.gcd(col0, ncols)) if col0 else math.gcd(cb, ncols)
    r8 = tm // 8
    n8 = N_ROWS // 8
    c0 = col0 // cb
    return pl.pallas_call(
        functools.partial(_conv3_kernel, act=act),
        out_shape=jax.ShapeDtypeStruct((N_ROWS, ncols), F32),
        grid=(N_ROWS // tm, ncols // cb),
        in_specs=[pl.BlockSpec((tm, cb), lambda i, j: (i, c0 + j)),
                  pl.BlockSpec((8, cb), lambda i, j: (jnp.maximum(i * r8 - 1, 0), c0 + j)),
                  pl.BlockSpec((8, cb), lambda i, j: (jnp.minimum((i + 1) * r8, n8 - 1), c0 + j)),
                  pl.BlockSpec((3, cb), lambda i, j: (0, j)),
                  pl.BlockSpec((1, cb), lambda i, j: (0, j))],
        out_specs=pl.BlockSpec((tm, cb), lambda i, j: (i, j)),
        compiler_params=_cp(("parallel", "parallel")), name=name)(p, p, p, jnp.transpose(w), b.reshape(1, ncols))


def _ssd_kernel(xs_ref, bm_ref, cm_ref, dt_ref, dtT_ref, bias_ref, biasT_ref, alog_ref, alogT_ref,
                y_ref, st_ref):
    d = pl.program_id(0)
    c = pl.program_id(2)
    Q = SSD_Q
    HG = SSD_HEADS // SSD_G

    @pl.when(c == 0)
    def _():
        st_ref[...] = jnp.zeros_like(st_ref)

    isb = d == 1
    sgn = 1 - 2 * d
    ii = lax.broadcasted_iota(jnp.int32, (Q, Q), 0)
    jj = lax.broadcasted_iota(jnp.int32, (Q, Q), 1)
    tri = (jj <= ii).astype(BF16)
    triT = (ii <= jj).astype(BF16)
    mask = sgn * (ii - jj) >= 0
    xs = xs_ref[...]
    G = range(SSD_G)
    dt = [_softplus(dt_ref[g] + bias_ref[g]) for g in G]
    dtT = [_softplus(dtT_ref[g] + biasT_ref[g]) for g in G]
    a = [dt[g] * (-jnp.exp(alog_ref[g])) for g in G]
    aT = [dtT[g] * (-jnp.exp(alogT_ref[g])) for g in G]
    cs = [_dot2r(tri, a[g]) for g in G]
    csT = [_dot2l(aT[g], triT) for g in G]
    tot = [cs[g][Q - 1:Q, :] for g in G]
    p = [jnp.where(isb, a[g] - cs[g], cs[g]) for g in G]
    pT = [jnp.where(isb, aT[g] - csT[g], csT[g]) for g in G]
    dec_out = [jnp.exp(jnp.where(isb, tot[g], 0.0) + p[g]) for g in G]
    dec_state = [jnp.exp(jnp.where(isb, 0.0, tot[g]) - p[g]) for g in G]
    chunk_dec = [jnp.exp(tot[g]) for g in G]
    bm = [bm_ref[:, g * SSD_S:(g + 1) * SSD_S].astype(BF16) for g in G]
    cm = [cm_ref[:, g * SSD_S:(g + 1) * SSD_S].astype(BF16) for g in G]
    cb = [_dot_nt(cm[g], bm[g]) for g in G]
    GH = [(g, h) for g in G for h in range(HG)]
    NH = range(len(GH))
    lm = [(cb[g] * jnp.exp(jnp.where(mask, p[g][:, h:h + 1] - pT[g][h:h + 1, :], -1e30))).astype(BF16)
          for g, h in GH]
    xh = [xs[:, n * SSD_P:(n + 1) * SSD_P] * dt[g][:, h:h + 1] for n, (g, h) in enumerate(GH)]
    s_old = [st_ref[n] for n in NH]
    y_in = [_dot(lm[n], xh[n].astype(BF16)) for n in NH]
    y_st = [_dot(cm[g], s_old[n].astype(BF16)) for n, (g, h) in enumerate(GH)]
    upd = [lax.dot_general(bm[g], (xh[n] * dec_state[g][:, h:h + 1]).astype(BF16), (((0,), (0,)), ((), ())),
                           preferred_element_type=F32) for n, (g, h) in enumerate(GH)]
    for n, (g, h) in enumerate(GH):
        st_ref[n] = chunk_dec[g][:, h:h + 1] * s_old[n] + upd[n]
    y_ref[...] = jnp.concatenate([y_in[n] + dec_out[g][:, h:h + 1] * y_st[n] for n, (g, h) in enumerate(GH)],
                                 axis=1)


def _ssd_rowblock(d, b, c):
    n_ctx = CTX // SSD_Q
    n_lat = SEQ // SSD_Q
    cc = jnp.where(d == 0, c, n_ctx - 1 - c)
    lc = jnp.where(d == 0, c - n_ctx, n_ctx + n_lat - 1 - c)
    return jnp.where(c < n_ctx, N_LAT // SSD_Q + b * n_ctx + cc, b * n_lat + lc)


def ssd_scan(xbc, dt_raw, dt_bias, a_log):
    HG = SSD_HEADS // SSD_G
    W = SSD_HEADS * SSD_P
    dsel = dt_raw[:, :2 * SSD_HEADS].reshape(N_ROWS, 2, SSD_G, HG).transpose(1, 2, 0, 3)
    dselT = dsel.transpose(0, 1, 3, 2)
    bias = dt_bias.reshape(2, SSD_G, 1, HG)
    biasT = dt_bias.reshape(2, SSD_G, HG, 1)
    alog = a_log.reshape(2, SSD_G, 1, HG)
    alogT = a_log.reshape(2, SSD_G, HG, 1)
    nch = (CTX + SEQ) // SSD_Q
    rb = _ssd_rowblock
    GS = SSD_G * SSD_S
    par = lambda shape: pl.BlockSpec((None,) + shape, lambda d, b, c: (d, 0, 0, 0))
    return pl.pallas_call(
        _ssd_kernel,
        out_shape=jax.ShapeDtypeStruct((2, N_ROWS, W), F32),
        grid=(2, BATCH, nch),
        in_specs=[pl.BlockSpec((SSD_Q, W), lambda d, b, c: (rb(d, b, c), 0)),
                  pl.BlockSpec((SSD_Q, GS), lambda d, b, c: (rb(d, b, c), W // GS)),
                  pl.BlockSpec((SSD_Q, GS), lambda d, b, c: (rb(d, b, c), W // GS + 1)),
                  pl.BlockSpec((None, SSD_G, SSD_Q, HG), lambda d, b, c: (d, 0, rb(d, b, c), 0)),
                  pl.BlockSpec((None, SSD_G, HG, SSD_Q), lambda d, b, c: (d, 0, 0, rb(d, b, c))),
                  par((SSD_G, 1, HG)), par((SSD_G, HG, 1)), par((SSD_G, 1, HG)), par((SSD_G, HG, 1))],
        out_specs=pl.BlockSpec((None, SSD_Q, W), lambda d, b, c: (d, rb(d, b, c), 0)),
        scratch_shapes=[pltpu.VMEM((SSD_HEADS, SSD_S, SSD_P), F32)],
        compiler_params=_cp(("parallel", "parallel", "arbitrary")), name="ssd_scan",
    )(xbc, xbc, xbc, dsel, dselT, bias, biasT, alog, alogT)


def _ssd_out_kernel(yf_ref, yb_ref, xs_ref, z_ref, dskip_ref, nw_ref, o_ref):
    y = yf_ref[...] + yb_ref[...] + xs_ref[...] * dskip_ref[...]
    y = y * _silu(z_ref[...])
    gs = SSD_HEADS * SSD_P // SSD_G
    parts = []
    for g in range(SSD_G):
        yg = y[:, g * gs:(g + 1) * gs]
        parts.append(yg * lax.rsqrt(jnp.mean(yg * yg, axis=-1, keepdims=True) + EPS))
    o_ref[...] = jnp.concatenate(parts, axis=1) * nw_ref[...]


def ssd_output(y2, xbc, p, zcol, d_skip, norm_w, tm=256):
    W = SSD_HEADS * SSD_P
    dexp = jnp.repeat(d_skip, SSD_P).reshape(1, W)
    return pl.pallas_call(
        _ssd_out_kernel, out_shape=jax.ShapeDtypeStruct((N_ROWS, W), F32), grid=(N_ROWS // tm,),
        in_specs=[pl.BlockSpec((None, tm, W), lambda i: (0, i, 0)),
                  pl.BlockSpec((None, tm, W), lambda i: (1, i, 0)),
                  pl.BlockSpec((tm, W), lambda i: (i, 0)),
                  pl.BlockSpec((tm, W), lambda i: (i, zcol // W)),
                  pl.BlockSpec((1, W), lambda i: (0, 0)),
                  pl.BlockSpec((1, W), lambda i: (0, 0))],
        out_specs=pl.BlockSpec((tm, W), lambda i: (i, 0)),
        compiler_params=_cp(("parallel",)), name="ssd_output")(y2, y2, xbc, p, dexp, norm_w.reshape(1, W))


def _hyfilt_kernel(f_ref, w0_ref, b0_ref, fr0_ref, w1_ref, b1_ref, fr1_ref, w2_ref, dl_ref, h_ref, ss_ref, *,
                   n_tiles):
    f = f_ref[...]
    h = jnp.sin(fr0_ref[...] * (_dot3(f, w0_ref[...]) + b0_ref[...]))
    h = jnp.sin(fr1_ref[...] * (_dot3(h, w1_ref[...]) + b1_ref[...]))
    h = _dot3(h, w2_ref[...])
    h = h * jnp.exp(-f[:, 0:1] * dl_ref[...])
    side = pl.program_id(0) // n_tiles
    j = pl.program_id(0) % n_tiles

    @pl.when(j == 0)
    def _():
        ss_ref[...] = jnp.zeros_like(ss_ref)

    ss_ref[...] += jnp.sum(h * h, axis=0, keepdims=True)
    row = lax.broadcasted_iota(jnp.int32, (h.shape[0], 1), 0) + j * h.shape[0]
    h_ref[...] = jnp.where((side == 1) & (row == 0), 0.0, h)


def hyena_filter_taps(L, hp):
    pos = jnp.arange(L, dtype=F32)
    t = pos / (L - 1)
    bands = (HY_EMB - 1) // 2
    freqs = jnp.linspace(1e-4, bands - 1, bands, dtype=F32)
    ang = (2.0 * math.pi / L) * pos[:, None] * freqs[None, :]
    feats = jnp.concatenate([t[:, None], jnp.cos(ang), -jnp.sin(ang)], axis=-1)
    feats = jnp.pad(feats, ((0, 0), (0, 128 - HY_EMB)))
    feats = jnp.concatenate([feats, jnp.flip(feats, axis=0)], axis=0)
    w0 = jnp.pad(hp['hy_mlp_w0'], ((0, 128 - HY_EMB), (0, 0)))
    min_decay = math.log(1e-2) / 1.5
    max_decay = math.log(1e-2) / 0.3
    deltas = jnp.abs(jnp.linspace(min_decay, max_decay, HY_W, dtype=F32))
    dl = jnp.tile(deltas, 2).reshape(1, 2 * HY_W)
    w2 = hp['hy_mlp_w2'].reshape(HY_HID, 2, 2, HY_W).transpose(0, 2, 1, 3).reshape(HY_HID, 4 * HY_W)
    tl = min(L, 512)
    n_tiles = L // tl
    NS = 2 * HY_W
    full = lambda shape: pl.BlockSpec(shape, lambda i: (0, 0))
    return pl.pallas_call(
        functools.partial(_hyfilt_kernel, n_tiles=n_tiles),
        out_shape=(jax.ShapeDtypeStruct((2 * L, NS), F32), jax.ShapeDtypeStruct((1, 2 * NS), F32)),
        grid=(2 * n_tiles,),
        in_specs=[pl.BlockSpec((tl, 128), lambda i: (i, 0)), full((128, HY_HID)), full((1, HY_HID)),
                  full((1, HY_HID)), full((HY_HID, HY_HID)), full((1, HY_HID)), full((1, HY_HID)),
                  pl.BlockSpec((HY_HID, NS), lambda i: (0, i // n_tiles)), full((1, NS))],
        out_specs=(pl.BlockSpec((tl, NS), lambda i: (i, 0)), pl.BlockSpec((1, NS), lambda i: (0, i // n_tiles))),
        compiler_params=_cp(("arbitrary",)), name="hyena_filter",
    )(feats, w0, hp['hy_mlp_b0'].reshape(1, -1), hp['hy_freq0'].reshape(1, -1), hp['hy_mlp_w1'],
      hp['hy_mlp_b1'].reshape(1, -1), hp['hy_freq1'].reshape(1, -1), w2, dl)


def _cis(num, den):
    ang = (2.0 * math.pi / den) * (num % den).astype(F32)
    return jnp.cos(ang), -jnp.sin(ang)


def _fft_consts(NB, BS):
    N = NB * BS
    h = NB // 2
    k1 = jnp.arange(h, dtype=jnp.int32)
    j = jnp.arange(NB, dtype=jnp.int32)
    re, im = _cis(j[None, :] * (2 * k1[:, None] + 1), 2 * NB)
    f1 = jnp.concatenate([re, im], axis=0)
    neg = jnp.where(j >= h, -1.0, 1.0)[None, :]
    f1_data = f1[:, :h]
    f1_filt = f1 * neg
    f1_inv = (2.0 / N) * jnp.concatenate([re[:, :h].T, im[:, :h].T], axis=1)
    r = jnp.arange(BS, dtype=jnp.int32)
    k2 = jnp.arange(BS, dtype=jnp.int32)
    kk = 2 * k1[:, None, None] + 2 * NB * k2[None, :, None] + 1
    gre, gim = _cis(kk * r[None, None, :], 2 * N)
    gf = jnp.concatenate([jnp.concatenate([gre, -gim], axis=2), jnp.concatenate([gim, gre], axis=2)], axis=1)
    gret, gimt = gre.transpose(0, 2, 1), gim.transpose(0, 2, 1)
    gi = jnp.concatenate([jnp.concatenate([gret, gimt], axis=2), jnp.concatenate([-gimt, gret], axis=2)], axis=1)
    return (f1_data.astype(BF16), f1_filt.astype(BF16), f1_inv.astype(BF16), gf.astype(BF16), gi.astype(BF16))


FFT_PAD = 8


FFT_LW = 128


def _fft_fwd_kernel(ua_ref, ub_ref, f1_ref, g_ref, o_ref, t_sc, *, NB, BS, nj, kg):
    pitch = NB + FFT_PAD
    u_refs = (ua_ref, ub_ref)

    @pl.when(pl.program_id(2) == 0)
    def _():
        f1 = f1_ref[...]

        def body(r, carry):
            xr = jnp.concatenate([u[pl.ds(r, nj, stride=BS), :] for u in u_refs], axis=1).astype(BF16)
            res = _dot(f1, xr)
            for hh in range(2):
                t_sc[hh, pl.ds(pl.multiple_of(r * pitch, 8), NB), :] = res[:, hh * FFT_LW:(hh + 1) * FFT_LW]
            return carry

        lax.fori_loop(0, BS, body, 0, unroll=8)

    k0 = pl.program_id(2) * kg
    for i in range(kg):
        are = jnp.concatenate([t_sc[hh, pl.ds(k0 + i, BS, stride=pitch), :] for hh in range(2)], axis=1)
        aim = jnp.concatenate([t_sc[hh, pl.ds(k0 + i + NB // 2, BS, stride=pitch), :] for hh in range(2)], axis=1)
        a = jnp.concatenate([are, aim], axis=0).astype(BF16)
        o_ref[i] = _dot(g_ref[i], a)


def fft_fwd(u, col0, nbatch, nj, f1, gf, NB, BS, kg=8):
    h = NB // 2
    kg = min(kg, h)
    lw = FFT_LW
    ct = 2 * lw
    return pl.pallas_call(
        functools.partial(_fft_fwd_kernel, NB=NB, BS=BS, nj=nj, kg=kg),
        out_shape=jax.ShapeDtypeStruct((nbatch, h, 2 * BS, HY_W), F32),
        grid=(nbatch, HY_W // ct, h // kg),
        in_specs=[pl.BlockSpec((nj * BS, lw), lambda b, c, k: (b, col0 // lw + 2 * c), pipeline_mode=pl.Buffered(1)),
                  pl.BlockSpec((nj * BS, lw), lambda b, c, k: (b, col0 // lw + 2 * c + 1),
                               pipeline_mode=pl.Buffered(1)),
                  pl.BlockSpec((NB, nj), lambda b, c, k: (0, 0)),
                  pl.BlockSpec((kg, 2 * BS, 2 * BS), lambda b, c, k: (k, 0, 0))],
        out_specs=pl.BlockSpec((None, kg, 2 * BS, ct), lambda b, c, k: (b, k, 0, c)),
        scratch_shapes=[pltpu.VMEM((2, BS * (NB + FFT_PAD), lw), F32)],
        compiler_params=_cp(("parallel", "parallel", "arbitrary")), name="hyena_fft_fwd")(u, u, f1, gf)


def _cmul(u, h, half):
    ure, uim = u[:half], u[half:]
    hre, him = h[:half], h[half:]
    return jnp.concatenate([ure * hre - uim * him, ure * him + uim * hre], axis=0)


def _fft_inv_kernel(us_ref, hs_ref, gi_ref, f1i_ref, o_ref, t_sc, y_sc, *, NB, BS, kg):
    ks = pl.program_id(2)
    pitch = 2 * BS + FFT_PAD
    for i in range(kg):
        y = _cmul(us_ref[i], hs_ref[i], BS).astype(BF16)
        row = pl.multiple_of((ks * kg + i) * pitch, 8)
        res = _dot(gi_ref[i], y)
        for hh in range(2):
            t_sc[hh, pl.ds(row, 2 * BS), :] = res[:, hh * FFT_LW:(hh + 1) * FFT_LW]

    @pl.when(ks == pl.num_programs(2) - 1)
    def _():
        f1i = f1i_ref[...]

        def body(r, carry):
            bre = jnp.concatenate([t_sc[hh, pl.ds(r, NB // 2, stride=pitch), :] for hh in range(2)], axis=1)
            bim = jnp.concatenate([t_sc[hh, pl.ds(r + BS, NB // 2, stride=pitch), :] for hh in range(2)], axis=1)
            b = jnp.concatenate([bre, bim], axis=0).astype(BF16)
            res = _dot(f1i, b)
            for hh in range(2):
                y_sc[hh, pl.ds(r, NB // 2, stride=BS), :] = res[:, hh * FFT_LW:(hh + 1) * FFT_LW]
            return carry

        lax.fori_loop(0, BS, body, 0, unroll=8)
        o_ref[...] = jnp.concatenate([y_sc[0], y_sc[1]], axis=1)


def fft_inv(us, hs, gi, f1i, NB, BS, kg=8):
    nbatch, h = us.shape[0], NB // 2
    kg = min(kg, h)
    L = h * BS
    ct = 2 * FFT_LW
    return pl.pallas_call(
        functools.partial(_fft_inv_kernel, NB=NB, BS=BS, kg=kg),
        out_shape=jax.ShapeDtypeStruct((nbatch * L, HY_W), F32),
        grid=(nbatch, HY_W // ct, h // kg),
        in_specs=[pl.BlockSpec((None, kg, 2 * BS, ct), lambda b, c, k: (b, k, 0, c)),
                  pl.BlockSpec((None, kg, 2 * BS, ct), lambda b, c, k: (0, k, 0, c)),
                  pl.BlockSpec((kg, 2 * BS, 2 * BS), lambda b, c, k: (k, 0, 0)),
                  pl.BlockSpec((h, NB), lambda b, c, k: (0, 0))],
        out_specs=pl.BlockSpec((L, ct), lambda b, c, k: (b, c)),
        scratch_shapes=[pltpu.VMEM((2, h * (2 * BS + FFT_PAD), FFT_LW), F32), pltpu.VMEM((2, L, FFT_LW), F32)],
        compiler_params=_cp(("parallel", "parallel", "arbitrary")), name="hyena_fft_inv")(us, hs, gi, f1i)


def _dft_consts(L):
    N = 2 * L
    k = jnp.arange(L, dtype=jnp.int32)
    n = jnp.arange(N, dtype=jnp.int32)
    re, im = _cis(n[None, :] * (2 * k[:, None] + 1), 2 * N)
    f = jnp.concatenate([re, im], axis=0)
    neg = jnp.where(n >= L, -1.0, 1.0)[None, :]
    fi = (2.0 / N) * jnp.concatenate([re[:, :L].T, im[:, :L].T], axis=1)
    return f[:, :L].astype(BF16), (f * neg).astype(BF16), fi.astype(BF16)


def _cdft_kernel(f_ref, x_ref, o_ref):
    o_ref[...] = _dot(f_ref[...], x_ref[...].astype(BF16))


def dft_fwd(x, f, row0, col0, nbatch, ct=256):
    M, K = f.shape
    return pl.pallas_call(
        _cdft_kernel, out_shape=jax.ShapeDtypeStruct((nbatch, M, HY_W), F32),
        grid=(nbatch, HY_W // ct),
        in_specs=[pl.BlockSpec((M, K), lambda b, c: (0, 0)),
                  pl.BlockSpec((K, ct), lambda b, c: (row0 // K + b, col0 // ct + c))],
        out_specs=pl.BlockSpec((None, M, ct), lambda b, c: (b, 0, c)),
        compiler_params=_cp(("parallel", "parallel")), name="hyena_dft_fwd")(f, x)


def _cdft_inv_kernel(us_ref, hs_ref, fi_ref, o_ref):
    half = us_ref.shape[0] // 2
    o_ref[...] = _dot(fi_ref[...], _cmul(us_ref[...], hs_ref[...], half).astype(BF16))


def dft_inv(us, hs, fi, ct=256):
    nbatch, M2, _ = us.shape
    L = fi.shape[0]
    return pl.pallas_call(
        _cdft_inv_kernel, out_shape=jax.ShapeDtypeStruct((nbatch * L, HY_W), F32),
        grid=(nbatch, HY_W // ct),
        in_specs=[pl.BlockSpec((None, M2, ct), lambda b, c: (b, 0, c)),
                  pl.BlockSpec((None, M2, ct), lambda b, c: (0, 0, c)),
                  pl.BlockSpec((L, M2), lambda b, c: (0, 0))],
        out_specs=pl.BlockSpec((L, ct), lambda b, c: (b, c)),
        compiler_params=_cp(("parallel", "parallel")), name="hyena_dft_inv")(us, hs, fi)


def _hy_gate_kernel(g_ref, y_ref, u_ref, ss_ref, b_ref, o_ref):
    scale = lax.rsqrt(ss_ref[0:1, :] + ss_ref[1:2, :] + 1e-6)
    o_ref[...] = g_ref[...] * (y_ref[...] * scale + u_ref[...] * b_ref[...])


def _hy_gate2_kernel(g_ref, yl_ref, yc_ref, ul_ref, uc_ref, ssl_ref, ssc_ref, b_ref, o_ref, *, n_lat_t):
    is_lat = pl.program_id(0) < n_lat_t
    y = jnp.where(is_lat, yl_ref[...], yc_ref[...])
    uin = jnp.where(is_lat, ul_ref[...], uc_ref[...])
    ss = jnp.where(is_lat, ssl_ref[...], ssc_ref[...])
    scale = lax.rsqrt(ss[0:1, :] + ss[1:2, :] + 1e-6)
    o_ref[...] = g_ref[...] * (y * scale + uin * b_ref[...])


def hy_gate(gate, gcol, grow, y, uin, ucol, urow, ss, order, bias, tm=256):
    M = y.shape[0]
    return pl.pallas_call(
        _hy_gate_kernel, out_shape=jax.ShapeDtypeStruct((M, HY_W), F32), grid=(M // tm,),
        in_specs=[pl.BlockSpec((tm, HY_W), lambda i: (grow // tm + i, gcol // HY_W)),
                  pl.BlockSpec((tm, HY_W), lambda i: (i, 0)),
                  pl.BlockSpec((tm, HY_W), lambda i: (urow // tm + i, ucol // HY_W)),
                  pl.BlockSpec((None, 2, HY_W), lambda i: (order, 0, 0)),
                  pl.BlockSpec((None, 1, HY_W), lambda i: (order, 0, 0))],
        out_specs=pl.BlockSpec((tm, HY_W), lambda i: (i, 0)),
        compiler_params=_cp(("parallel",)), name="hyena_gate")(gate, y, uin, ss, bias)


def hyena(u, hp):
    C = HY_W
    bias = hp['hy_bias'].reshape(2, 1, C)
    NB = BS = int(round(math.sqrt(2 * SEQ)))
    f1d, f1f, f1i, gf, gi = _fft_consts(NB, BS)
    taps, ss = hyena_filter_taps(SEQ, hp)
    ss = ss.reshape(2, 2, C).transpose(1, 0, 2)
    conv_l = lambda zin, zcol, order: fft_inv(fft_fwd(zin, zcol, BATCH, NB // 2, f1d, gf, NB, BS),
                                              fft_fwd(taps, order * C, 1, NB, f1f, gf, NB, BS), gi, f1i, NB, BS)
    z1_lat = hy_gate(u, 0, 0, conv_l(u, 2 * C, 0), u, 2 * C, 0, ss, 0, bias)
    y2_lat = conv_l(z1_lat, 0, 1)
    fd, ff, fi = _dft_consts(CTX)
    taps_c, ss_c = hyena_filter_taps(CTX, hp)
    ss_c = ss_c.reshape(2, 2, C).transpose(1, 0, 2)
    conv_c = lambda zin, zrow, zcol, order: dft_inv(dft_fwd(zin, fd, zrow, zcol, BATCH),
                                                    dft_fwd(taps_c, ff, 0, order * C, 1), fi)
    z1_ctx = hy_gate(u, 0, N_LAT, conv_c(u, N_LAT, 2 * C, 0), u, 2 * C, N_LAT, ss_c, 0, bias)
    y2_ctx = conv_c(z1_ctx, 0, 0, 1)
    tm = 256
    n_lat_t = N_LAT // tm
    lat = lambda i: (jnp.minimum(i, n_lat_t - 1), 0)
    ctx = lambda i: (jnp.maximum(i - n_lat_t, 0), 0)
    return pl.pallas_call(
        functools.partial(_hy_gate2_kernel, n_lat_t=n_lat_t),
        out_shape=jax.ShapeDtypeStruct((N_ROWS, C), F32), grid=(N_ROWS // tm,),
        in_specs=[pl.BlockSpec((tm, C), lambda i: (i, 1)),
                  pl.BlockSpec((tm, C), lat), pl.BlockSpec((tm, C), ctx),
                  pl.BlockSpec((tm, C), lat), pl.BlockSpec((tm, C), ctx),
                  pl.BlockSpec((None, 2, C), lambda i: (1, 0, 0)), pl.BlockSpec((None, 2, C), lambda i: (1, 0, 0)),
                  pl.BlockSpec((None, 1, C), lambda i: (1, 0, 0))],
        out_specs=pl.BlockSpec((tm, C), lambda i: (i, 0)),
        compiler_params=_cp(("parallel",)), name="hyena_gate2",
    )(u, y2_lat, y2_ctx, z1_lat, z1_ctx, ss, ss_c, bias)


EV_SSD_IN = SSD_HEADS * SSD_P
EV_XBC = EV_SSD_IN + 2 * SSD_G * SSD_S
EV_PAD_N = 5760


def even_mixer(x, mods, g_pre, g_post, ep):
    o1 = EV_SSD_IN
    o2 = o1 + EV_XBC
    o3 = o2 + 2 * SSD_HEADS
    w = ep['w_in']
    n_in = w.shape[1]
    hw = 3 * HY_W
    w_perm = jnp.concatenate([w[:, o3:], w[:, :o2], w[:, o2:o3],
                              jnp.zeros((D, EV_PAD_N - n_in), F32)], axis=1).astype(BF16)
    p = norm_mod_matmul(x, g_pre, mods, w_perm, 0, 1, tn=1920, name="even_in_proj")
    xbc = dwconv3(p, hw + o1, EV_XBC, ep['ssd_conv_w'], ep['ssd_conv_b'], True, "ssd_conv")
    u = dwconv3(p, 0, hw, ep['hy_conv_w'], ep['hy_conv_b'], False, "hyena_conv")
    dt_raw = p[:, hw + o2:hw + o2 + 2 * SSD_HEADS]
    y2 = ssd_scan(xbc, dt_raw, ep['ssd_dt_bias'], ep['ssd_a_log'])
    s = ssd_output(y2, xbc, p, hw, ep['ssd_d'], ep['ssd_norm_w'])
    zh = hyena(u, ep)
    return outproj_residual(s, zh, ep['w_out'].astype(BF16), x, g_post, mods, 2, name="even_out_proj")


def _head_sum(x, e, et):
    return _dot2l(_dot2l(x, e), et)


def _rw_prep_kernel(r_ref, k_ref, v_ref, lo_ref, w0_ref, wup_ref, a0_ref, aup_ref, gup_ref, kk_ref, ka_ref,
                    rk_ref, e_ref, et_ref, lw_ref, kd_ref, be_ref, kap_ref, g_ref, bonus_ref):
    r, k, v = r_ref[...], k_ref[...], v_ref[...]
    lo = lo_ref[...]
    wc, ac, gc = lo[:, 0:64], lo[:, 64:128], lo[:, 128:384]
    e, et = e_ref[...], et_ref[...]
    kk = k * kk_ref[...]
    kap = kk * lax.rsqrt(_head_sum(kk * kk, e, et) + 1e-12)
    kap_ref[...] = kap
    g_ref[...] = _dot3(_sigmoid(gc), gup_ref[...])
    kd_sum = jnp.zeros_like(k)
    for d in range(2):
        wlog = -_softplus(-(w0_ref[d:d + 1, :] + _dot3(jnp.tanh(wc), wup_ref[d]))) - 0.5
        lw_ref[d] = -jnp.exp(wlog)
        a = _sigmoid(a0_ref[d:d + 1, :] + _dot3(ac, aup_ref[d]))
        kd = k * (1.0 + (a - 1.0) * ka_ref[...])
        kd_ref[d] = kd
        be_ref[d] = kap * a
        kd_sum = kd_sum + kd
    bonus_ref[...] = _head_sum(r * kd_sum * rk_ref[...], e, et) * v


def rwkv_prepare(code, lora, op, tm=256):
    W = RW_H * RW_N
    heads = jnp.arange(W, dtype=jnp.int32) // RW_N
    e = (heads[:, None] == jnp.arange(128, dtype=jnp.int32)[None, :]).astype(BF16)
    et = jnp.transpose(e)
    gup = jnp.pad(op['g_up'], ((0, 256 - op['g_up'].shape[0]), (0, 0)))
    row = lambda a: a.reshape(1, W)
    full2 = lambda shape: pl.BlockSpec(shape, lambda i: (0,) * len(shape))
    outs = pl.pallas_call(
        _rw_prep_kernel,
        out_shape=(jax.ShapeDtypeStruct((2, N_ROWS, W), F32), jax.ShapeDtypeStruct((2, N_ROWS, W), F32),
                   jax.ShapeDtypeStruct((2, N_ROWS, W), F32), jax.ShapeDtypeStruct((N_ROWS, W), F32),
                   jax.ShapeDtypeStruct((N_ROWS, W), F32), jax.ShapeDtypeStruct((N_ROWS, W), F32)),
        grid=(N_ROWS // tm,),
        in_specs=[pl.BlockSpec((tm, W), lambda i: (i, 0)), pl.BlockSpec((tm, W), lambda i: (i, 1)),
                  pl.BlockSpec((tm, W), lambda i: (i, 2)), pl.BlockSpec((tm, 384), lambda i: (i, 0)),
                  full2((2, W)), full2((2, 64, W)), full2((2, W)), full2((2, 64, W)), full2((256, W)),
                  full2((1, W)), full2((1, W)), full2((1, W)), full2((W, 128)), full2((128, W))],
        out_specs=(pl.BlockSpec((2, tm, W), lambda i: (0, i, 0)), pl.BlockSpec((2, tm, W), lambda i: (0, i, 0)),
                   pl.BlockSpec((2, tm, W), lambda i: (0, i, 0)), pl.BlockSpec((tm, W), lambda i: (i, 0)),
                   pl.BlockSpec((tm, W), lambda i: (i, 0)), pl.BlockSpec((tm, W), lambda i: (i, 0))),
        compiler_params=_cp(("parallel",)), name="rwkv_prepare",
    )(code, code, code, lora, op['w0'], op['w_up'], op['a0'], op['a_up'], gup, row(op['k_k']), row(op['k_a']),
      row(op['r_k']), e, et)
    return outs


def _tri_inv(n, eye, masks):
    bd = lambda a, b: _dot(a.astype(BF16), b.astype(BF16))
    d0 = jnp.where(masks[0], n, 0.0)
    d2 = bd(d0, d0)
    d4 = bd(d2, d2)
    t = bd(bd(eye + d0, eye + d2), eye + d4)
    for m in masks[1:]:
        e = jnp.where(m, n, 0.0)
        t = t + bd(t, bd(e, t))
    return t


def _rw_scan_kernel(r_ref, v_ref, lw_ref, kd_ref, be_ref, kap_ref, y_ref, st_ref):
    d = pl.program_id(0)
    c = pl.program_id(2)
    C = RW_CHUNK
    N = RW_N

    @pl.when(c == 0)
    def _():
        st_ref[...] = jnp.zeros_like(st_ref)

    isb = d == 1
    sgn = 1 - 2 * d
    ii = lax.broadcasted_iota(jnp.int32, (C, C), 0)
    jj = lax.broadcasted_iota(jnp.int32, (C, C), 1)
    dif = sgn * (ii - jj)
    incl = dif >= 0
    strict = dif > 0
    tri = incl.astype(BF16)
    eye = (ii == jj).astype(F32)
    blk = [(ii >> s) == (jj >> s) for s in range(3, C.bit_length() - 1)]
    masks = [blk[0]] + [blk[l] & ~blk[l - 1] for l in range(1, len(blk))] + [~blk[-1]]
    lw = lw_ref[...]
    cum = _dot2r(tri, lw)
    ec = jnp.exp(cum)
    en = jnp.exp(-cum)
    ea = jnp.exp(cum - lw)
    last = jnp.where(isb, cum[0:1, :], cum[C - 1:C, :])
    el = jnp.exp(last - cum)
    kap = kap_ref[...]
    r = r_ref[...]
    v = v_ref[...]
    a_t = -kap * ea
    r_t = r * ec
    b_t = be_ref[...] * en
    k_t = kd_ref[...] * en
    b_l = be_ref[...] * el
    k_l = kd_ref[...] * el
    pc = jnp.exp(last)
    H = range(RW_H)
    sl = [slice(h * N, (h + 1) * N) for h in H]
    bd = lambda a, b: _dot(a.astype(BF16), b.astype(BF16))
    tn = lambda a, b: lax.dot_general(a, b, (((0,), (0,)), ((), ())), preferred_element_type=F32)
    sc = [_dot_nt(jnp.concatenate([a_t[:, sl[h]], r_t[:, sl[h]]], axis=0).astype(BF16),
                  jnp.concatenate([b_t[:, sl[h]], k_t[:, sl[h]]], axis=0).astype(BF16)) for h in H]
    n_ab = [jnp.where(strict, sc[h][0:C, 0:C], 0.0) for h in H]
    a_ak = [jnp.where(strict, sc[h][0:C, C:2 * C], 0.0).astype(BF16) for h in H]
    m_rb = [jnp.where(incl, sc[h][C:2 * C, 0:C], 0.0).astype(BF16) for h in H]
    m_rk = [jnp.where(incl, sc[h][C:2 * C, C:2 * C], 0.0).astype(BF16) for h in H]
    vh = [v[:, sl[h]].astype(BF16) for h in H]
    d0 = [jnp.where(masks[0], n_ab[h], 0.0) for h in H]
    d2 = [bd(d0[h], d0[h]) for h in H]
    d4 = [bd(d2[h], d2[h]) for h in H]
    t = [bd(eye + d0[h], eye + d2[h]) for h in H]
    t = [bd(t[h], eye + d4[h]) for h in H]
    for m in masks[1:]:
        et = [bd(jnp.where(m, n_ab[h], 0.0), t[h]) for h in H]
        t = [t[h] + bd(t[h], et[h]) for h in H]
    amv = [_dot(jnp.concatenate([a_ak[h], m_rk[h]], axis=0), vh[h]) for h in H]
    wub = [bd(t[h], jnp.concatenate([a_t[:, sl[h]], amv[h][0:C]], axis=1)).astype(BF16) for h in H]
    kv = [tn(k_l[:, sl[h]].astype(BF16), vh[h]) for h in H]
    qy = [_dot(m_rb[h], wub[h]) + jnp.concatenate([r_t[:, sl[h]], amv[h][C:2 * C]], axis=1) for h in H]
    pp = [tn(b_l[:, sl[h]].astype(BF16), wub[h]) + jnp.concatenate([eye[0:N, 0:N] * pc[:, sl[h]], kv[h]], axis=1)
          for h in H]
    h_old = [st_ref[h] for h in H]
    ys = [bd(qy[h][:, 0:N], h_old[h]) + qy[h][:, N:2 * N] for h in H]
    for h in H:
        st_ref[h] = _dot3(pp[h][:, 0:N], h_old[h]) + pp[h][:, N:2 * N]
    y_ref[...] = jnp.concatenate(ys, axis=1)


def _rw_rowblock(d, b, c):
    n_ctx = CTX // RW_CHUNK
    n_lat = SEQ // RW_CHUNK
    cc = jnp.where(d == 0, c, n_ctx - 1 - c)
    lc = jnp.where(d == 0, c - n_ctx, n_ctx + n_lat - 1 - c)
    return jnp.where(c < n_ctx, N_LAT // RW_CHUNK + b * n_ctx + cc, b * n_lat + lc)


def rwkv_scan(code, lw, kd, be, kap):
    W = RW_H * RW_N
    nch = (CTX + SEQ) // RW_CHUNK
    rb = lambda d, b, c: _rw_rowblock(d, b, c)
    return pl.pallas_call(
        _rw_scan_kernel,
        out_shape=jax.ShapeDtypeStruct((2, N_ROWS, W), F32),
        grid=(2, BATCH, nch),
        in_specs=[pl.BlockSpec((RW_CHUNK, W), lambda d, b, c: (rb(d, b, c), 0)),
                  pl.BlockSpec((RW_CHUNK, W), lambda d, b, c: (rb(d, b, c), 2)),
                  pl.BlockSpec((None, RW_CHUNK, W), lambda d, b, c: (d, rb(d, b, c), 0)),
                  pl.BlockSpec((None, RW_CHUNK, W), lambda d, b, c: (d, rb(d, b, c), 0)),
                  pl.BlockSpec((None, RW_CHUNK, W), lambda d, b, c: (d, rb(d, b, c), 0)),
                  pl.BlockSpec((RW_CHUNK, W), lambda d, b, c: (rb(d, b, c), 0))],
        out_specs=pl.BlockSpec((None, RW_CHUNK, W), lambda d, b, c: (d, rb(d, b, c), 0)),
        scratch_shapes=[pltpu.VMEM((RW_H, RW_N, RW_N), F32)],
        compiler_params=_cp(("parallel", "parallel", "arbitrary")), name="rwkv_scan",
    )(code, code, lw, kd, be, kap)


def _rw_out_kernel(yf_ref, yb_ref, bonus_ref, g_ref, lnw_ref, lnb_ref, e_ref, et_ref, o_ref):
    e, et = e_ref[...], et_ref[...]
    y = yf_ref[...] + yb_ref[...]
    mean = _head_sum(y, e, et) * (1.0 / RW_N)
    yc = y - mean
    var = _head_sum(yc * yc, e, et) * (1.0 / RW_N)
    yn = yc * lax.rsqrt(var + RW_GN_EPS) * lnw_ref[...] + lnb_ref[...]
    o_ref[...] = (yn + bonus_ref[...]) * g_ref[...]


def rwkv_output(y2, bonus, g, op, tm=256):
    W = RW_H * RW_N
    heads = jnp.arange(W, dtype=jnp.int32) // RW_N
    e = (heads[:, None] == jnp.arange(128, dtype=jnp.int32)[None, :]).astype(BF16)
    et = jnp.transpose(e)
    M = N_LAT
    return pl.pallas_call(
        _rw_out_kernel, out_shape=jax.ShapeDtypeStruct((M, W), F32), grid=(M // tm,),
        in_specs=[pl.BlockSpec((None, tm, W), lambda i: (0, i, 0)), pl.BlockSpec((None, tm, W), lambda i: (1, i, 0)),
                  pl.BlockSpec((tm, W), lambda i: (i, 0)), pl.BlockSpec((tm, W), lambda i: (i, 0)),
                  pl.BlockSpec((1, W), lambda i: (0, 0)), pl.BlockSpec((1, W), lambda i: (0, 0)),
                  pl.BlockSpec((W, 128), lambda i: (0, 0)), pl.BlockSpec((128, W), lambda i: (0, 0))],
        out_specs=pl.BlockSpec((tm, W), lambda i: (i, 0)),
        compiler_params=_cp(("parallel",)), name="rwkv_output",
    )(y2, y2, bonus, g, op['ln_w'].reshape(1, W), op['ln_b'].reshape(1, W), e, et)


AT_Q = RW_H * AT_HD
AT_KW = AT_KV * AT_HD
AT_TQ = 512
AT_TK = 768


def _rope_tables(tm):
    half = AT_HD // 2
    inv = 10000.0 ** (-jnp.arange(0, half, 2, dtype=F32) / half)
    pos = jnp.arange(SEQ, dtype=jnp.int32)
    row = (pos // GRID_W).astype(F32)[:, None] * inv
    col = (pos % GRID_W).astype(F32)[:, None] * inv
    cos_h = jnp.concatenate([jnp.cos(row), jnp.cos(row), jnp.cos(col), jnp.cos(col)], axis=1)
    sin_h = jnp.concatenate([-jnp.sin(row), jnp.sin(row), -jnp.sin(col), jnp.sin(col)], axis=1)
    cos_t = jnp.concatenate([jnp.tile(cos_h, (1, 2)), jnp.ones((tm, 128), F32)], axis=0)
    sin_t = jnp.concatenate([jnp.tile(sin_h, (1, 2)), jnp.zeros((tm, 128), F32)], axis=0)
    return cos_t, sin_t


def _rot_partner(x):
    q = AT_HD // 4
    w = x.shape[1]
    lane = lax.broadcasted_iota(jnp.int32, x.shape, 1)
    return jnp.where((lane % (2 * q)) < q, pltpu.roll(x, w - q, 1), pltpu.roll(x, q, 1))


def _at_prep_kernel(q_ref, k_ref, v_ref, cos_ref, sin_ref, qn_ref, kn_ref, e_ref, et_ref, qo_ref, ko_ref, vo_ref):
    e, et = e_ref[...], et_ref[...]
    cos2, sin2 = cos_ref[...], sin_ref[...]

    def norm_rope(x, gain, nrep):
        ms = _head_sum(x * x, e[:x.shape[1]], et[:, :x.shape[1]]) * (1.0 / AT_HD)
        xn = x * lax.rsqrt(ms + EPS) * gain
        cos = jnp.tile(cos2, (1, nrep))
        sin = jnp.tile(sin2, (1, nrep))
        return xn * cos + _rot_partner(xn) * sin

    qn = norm_rope(q_ref[...], qn_ref[...], AT_Q // 128) * (AT_HD ** -0.5 * math.log2(math.e))
    qo_ref[...] = jnp.transpose(qn).astype(BF16)
    ko_ref[...] = norm_rope(k_ref[...], kn_ref[...], AT_KW // 128).astype(BF16)
    vo_ref[...] = jnp.transpose(v_ref[...]).astype(BF16)


def attention_prepare(p, q_norm, k_norm, tm=256):
    cos_t, sin_t = _rope_tables(tm)
    heads = jnp.arange(AT_Q, dtype=jnp.int32) // AT_HD
    e = (heads[:, None] == jnp.arange(128, dtype=jnp.int32)[None, :]).astype(BF16)
    et = jnp.transpose(e)
    tab = lambda i: jnp.where(i * tm < N_LAT, ((i * tm) % SEQ) // tm, SEQ // tm)
    n_lat_t, n_seq_t, n_ctx_t = N_LAT // tm, SEQ // tm, CTX // tm
    kvb = lambda i: jnp.where(i < n_lat_t, (i // n_seq_t) * (n_seq_t + n_ctx_t) + n_ctx_t + i % n_seq_t,
                              ((i - n_lat_t) // n_ctx_t) * (n_seq_t + n_ctx_t) + (i - n_lat_t) % n_ctx_t)
    qcol = (3 * RW_H * RW_N) // AT_Q
    kcol = (3 * RW_H * RW_N + AT_Q) // AT_KW
    return pl.pallas_call(
        _at_prep_kernel,
        out_shape=(jax.ShapeDtypeStruct((AT_Q, N_ROWS), BF16), jax.ShapeDtypeStruct((N_ROWS, AT_KW), BF16),
                   jax.ShapeDtypeStruct((AT_KW, N_ROWS), BF16)),
        grid=(N_ROWS // tm,),
        in_specs=[pl.BlockSpec((tm, AT_Q), lambda i: (i, qcol)),
                  pl.BlockSpec((tm, AT_KW), lambda i: (i, kcol)),
                  pl.BlockSpec((tm, AT_KW), lambda i: (i, kcol + 1)),
                  pl.BlockSpec((tm, 128), lambda i: (tab(i), 0)),
                  pl.BlockSpec((tm, 128), lambda i: (tab(i), 0)),
                  pl.BlockSpec((1, AT_Q), lambda i: (0, 0)),
                  pl.BlockSpec((1, AT_KW), lambda i: (0, 0)),
                  pl.BlockSpec((AT_Q, 128), lambda i: (0, 0)),
                  pl.BlockSpec((128, AT_Q), lambda i: (0, 0))],
        out_specs=(pl.BlockSpec((AT_Q, tm), lambda i: (0, i)), pl.BlockSpec((tm, AT_KW), lambda i: (kvb(i), 0)),
                   pl.BlockSpec((AT_KW, tm), lambda i: (0, kvb(i)))),
        compiler_params=_cp(("parallel",)), name="attn_prepare",
    )(p, p, p, cos_t, sin_t, jnp.tile(q_norm, AT_Q // AT_HD).reshape(1, AT_Q),
      jnp.tile(k_norm, AT_KV).reshape(1, AT_KW), e, et)


def _flash_t_kernel(qt_ref, k_ref, vt_ref, o_ref, m_sc, l_sc, acc_sc):
    ki = pl.program_id(2)
    nq = AT_Q // AT_HD
    gq = nq // AT_KV

    @pl.when(ki == 0)
    def _():
        m_sc[...] = jnp.full_like(m_sc, -1e30)
        l_sc[...] = jnp.zeros_like(l_sc)
        acc_sc[...] = jnp.zeros_like(acc_sc)

    for g in range(AT_KV):
        kg = k_ref[:, g * AT_HD:(g + 1) * AT_HD]
        vtg1 = jnp.concatenate([vt_ref[g * AT_HD:(g + 1) * AT_HD, :],
                                jnp.ones((16, vt_ref.shape[1]), BF16)], axis=0)
        hs = range(g * gq, (g + 1) * gq)
        st = [_dot(kg, qt_ref[h * AT_HD:(h + 1) * AT_HD, :]) for h in hs]
        m_old = [m_sc[h] for h in hs]
        m_new = [jnp.maximum(m_old[i], jnp.max(st[i], axis=0, keepdims=True)) for i in range(gq)]
        alpha = [jnp.exp2(m_old[i] - m_new[i]) for i in range(gq)]
        pt = [jnp.exp2(st[i] - m_new[i]) for i in range(gq)]
        pv = [_dot(vtg1, pt[i].astype(BF16)) for i in range(gq)]
        for i, h in enumerate(hs):
            l_sc[h] = alpha[i] * l_sc[h] + pv[i][AT_HD:AT_HD + 1, :]
            m_sc[h] = m_new[i]
            rows = pl.ds(h * AT_HD, AT_HD)
            acc_sc[rows, :] = alpha[i] * acc_sc[rows, :] + pv[i][0:AT_HD, :]

    @pl.when(ki == pl.num_programs(2) - 1)
    def _():
        inv = jnp.concatenate([jnp.broadcast_to(1.0 / l_sc[h], (AT_HD, l_sc.shape[2])) for h in range(nq)], axis=0)
        o_ref[...] = jnp.transpose(acc_sc[...] * inv)


def flash_attention_t(qt, k, vt):
    nq = AT_Q // AT_HD
    nk = (CTX + SEQ) // AT_TK
    kv_rb = lambda b, ki: b * nk + ki
    return pl.pallas_call(
        _flash_t_kernel,
        out_shape=jax.ShapeDtypeStruct((N_LAT, AT_Q), F32),
        grid=(BATCH, SEQ // AT_TQ, nk),
        in_specs=[pl.BlockSpec((AT_Q, AT_TQ), lambda b, qi, ki: (0, b * (SEQ // AT_TQ) + qi)),
                  pl.BlockSpec((AT_TK, AT_KW), lambda b, qi, ki: (kv_rb(b, ki), 0)),
                  pl.BlockSpec((AT_KW, AT_TK), lambda b, qi, ki: (0, kv_rb(b, ki)))],
        out_specs=pl.BlockSpec((AT_TQ, AT_Q), lambda b, qi, ki: (b * (SEQ // AT_TQ) + qi, 0)),
        scratch_shapes=[pltpu.VMEM((nq, 1, AT_TQ), F32), pltpu.VMEM((nq, 1, AT_TQ), F32),
                        pltpu.VMEM((AT_Q, AT_TQ), F32)],
        compiler_params=_cp(("parallel", "parallel", "arbitrary")), name="flash_attention")(qt, k, vt)


OD_PAD_N = 4992


def odd_mixer(x, mods, g_pre, g_post, op):
    W = RW_H * RW_N
    w = op['w_in']
    c3 = 3 * W
    code_n = c3 + 64 + 64 + 160
    w_perm = jnp.concatenate([w[:, :c3], w[:, code_n:], w[:, c3:code_n],
                              jnp.zeros((D, OD_PAD_N - w.shape[1]), F32)], axis=1).astype(BF16)
    p = norm_mod_matmul(x, g_pre, mods, w_perm, 0, 1, tn=1664, name="odd_in_proj")
    mu = op['mu']
    taps = lambda m: jnp.stack([0.5 * m, 1.0 - m, 0.5 * m], axis=1)
    code = dwconv3(p, 0, c3, taps(mu[:c3]), jnp.zeros((c3,), F32), False, "rwkv_shift")
    lo_col = c3 + AT_Q + 2 * AT_KW
    mu_lo = jnp.pad(mu[c3:], (0, 384 - (code_n - c3)))
    lora = dwconv3(p, lo_col, 384, taps(mu_lo), jnp.zeros((384,), F32), False, "rwkv_shift_lora", cb=128)
    lw, kd, be, kap, g, bonus = rwkv_prepare(code, lora, op)
    y2 = rwkv_scan(code, lw, kd, be, kap)
    o_l = rwkv_output(y2, bonus, g, op)
    q, k, v = attention_prepare(p, op['q_norm'], op['k_norm'])
    a_l = flash_attention_t(q, k, v)
    return outproj_residual(o_l, a_l, op['w_out'].astype(BF16), x, g_post, mods, 2, name="odd_out_proj")


def _router_kernel(x_ref, g_ref, mod_ref, rw_ref, rb_ref, s1_ref, s3_ref, s2_ref, t_ref, idx_ref, wt_ref, sh_ref):
    x = x_ref[...]
    ms = jnp.mean(x * x, axis=-1, keepdims=True)
    t = x * lax.rsqrt(ms + EPS) * g_ref[...] * (1.0 + mod_ref[4:5, :]) + mod_ref[3:4, :]
    t_ref[...] = t
    tb = t.astype(BF16)
    sh_ref[...] = _dot((_silu(_dot(tb, s1_ref[...])) * _dot(tb, s3_ref[...])).astype(BF16), s2_ref[...])
    th, tl = _split(t)
    wh, wl = _split(rw_ref[...])
    lg = _dot_nt(wh, th) + (_dot_nt(wh, tl) + _dot_nt(wl, th))
    sc = _sigmoid(lg)
    sel = sc + rb_ref[...]
    tm = sel.shape[1]
    gsz = N_EXP // N_GRP
    ninf = -jnp.inf
    sel3 = sel.reshape(N_GRP, gsz, tm)
    i3 = lax.broadcasted_iota(jnp.int32, sel3.shape, 1)
    m1 = jnp.max(sel3, axis=1, keepdims=True)
    first = jnp.min(jnp.where(sel3 == m1, i3, gsz), axis=1, keepdims=True)
    m2 = jnp.max(jnp.where(i3 == first, ninf, sel3), axis=1, keepdims=True)
    grp = (m1 + m2).reshape(N_GRP, tm)
    gi = lax.broadcasted_iota(jnp.int32, grp.shape, 0)
    gmask = jnp.zeros(grp.shape, F32)
    for _ in range(TOPK_GRP):
        m = jnp.max(grp, axis=0, keepdims=True)
        pick = jnp.min(jnp.where(grp == m, gi, N_GRP), axis=0, keepdims=True)
        hit = gi == pick
        gmask = jnp.where(hit, 1.0, gmask)
        grp = jnp.where(hit, ninf, grp)
    emask = jnp.broadcast_to(gmask.reshape(N_GRP, 1, tm), (N_GRP, gsz, tm)).reshape(N_EXP, tm)
    msel = jnp.where(emask > 0.5, sel, ninf)
    ei = lax.broadcasted_iota(jnp.int32, msel.shape, 0)
    idxs, ws = [], []
    for _ in range(TOP_K):
        m = jnp.max(msel, axis=0, keepdims=True)
        pick = jnp.min(jnp.where(msel == m, ei, N_EXP), axis=0, keepdims=True)
        hit = ei == pick
        idxs.append(pick)
        ws.append(jnp.sum(jnp.where(hit, sc, 0.0), axis=0, keepdims=True))
        msel = jnp.where(hit, ninf, msel)
    w = jnp.concatenate(ws, axis=0)
    idx_ref[...] = jnp.concatenate(idxs, axis=0)
    wt_ref[...] = w / jnp.sum(w, axis=0, keepdims=True) * ROUTED_SCALE


def moe_router(x, M, g, mods, mp, tm=256):
    full = lambda shape: pl.BlockSpec(shape, lambda i: (0,) * len(shape))
    return pl.pallas_call(
        _router_kernel,
        out_shape=(jax.ShapeDtypeStruct((M, D), F32), jax.ShapeDtypeStruct((TOP_K, M), jnp.int32),
                   jax.ShapeDtypeStruct((TOP_K, M), F32), jax.ShapeDtypeStruct((M, D), F32)),
        grid=(M // tm,),
        in_specs=[pl.BlockSpec((tm, D), lambda i: (i, 0)), full((1, D)),
                  pl.BlockSpec((None, 6, D), lambda i: (_seq_of_rowblock(i, tm), 0, 0)),
                  full((N_EXP, D)), full((N_EXP, 1)), full((D, EXP_FF)), full((D, EXP_FF)), full((EXP_FF, D))],
        out_specs=(pl.BlockSpec((tm, D), lambda i: (i, 0)), pl.BlockSpec((TOP_K, tm), lambda i: (0, i)),
                   pl.BlockSpec((TOP_K, tm), lambda i: (0, i)), pl.BlockSpec((tm, D), lambda i: (i, 0))),
        compiler_params=_cp(("parallel",)), name="moe_router",
    )(x, g.reshape(1, D), mods, jnp.transpose(mp['router_w']), mp['router_bias'].reshape(N_EXP, 1),
      mp['s1'].astype(BF16), mp['s3'].astype(BF16), mp['s2'].astype(BF16))


def _gather_rows(idx_ref, n, src_hbm, dst, sem, slot):
    def body(r, carry):
        pltpu.make_async_copy(src_hbm.at[pl.ds(idx_ref[0, r], 1)], dst.at[slot, pl.ds(r, 1)], sem.at[slot]).start()
        return carry

    lax.fori_loop(0, n, body, 0)


def _wait_rows(n, src_hbm, dst, sem, slot):
    pltpu.make_async_copy(src_hbm.at[pl.ds(0, n)], dst.at[slot], sem.at[slot]).wait()


def _expert_kernel(be_ref, tok_ref, tokn_ref, w_ref, t_hbm, w1_ref, w3_ref, w2_ref, o_ref, xbuf, sem):
    i = pl.program_id(0)
    n = pl.num_programs(0)
    slot = i % 2

    @pl.when(i == 0)
    def _():
        _gather_rows(tok_ref, MOE_BLK, t_hbm, xbuf, sem, 0)

    @pl.when(i + 1 < n)
    def _():
        _gather_rows(tokn_ref, MOE_BLK, t_hbm, xbuf, sem, 1 - slot)

    _wait_rows(MOE_BLK, t_hbm, xbuf, sem, slot)
    xb = xbuf[slot].astype(BF16)
    h = _silu(_dot(xb, w1_ref[...])) * _dot(xb, w3_ref[...])
    o_ref[...] = _dot(h.astype(BF16), w2_ref[...]) * w_ref[...]


def moe_experts(t, buf_tok, buf_w, block_e, w1, w3, w2):
    nb = block_e.shape[0]
    tok3 = buf_tok.reshape(nb, 1, MOE_BLK)
    nxt = lambda i, be: (jnp.minimum(i + 1, nb - 1), 0, 0)
    grid_spec = pltpu.PrefetchScalarGridSpec(
        num_scalar_prefetch=1, grid=(nb,),
        in_specs=[pl.BlockSpec((None, 1, MOE_BLK), lambda i, be: (i, 0, 0), memory_space=pltpu.SMEM),
                  pl.BlockSpec((None, 1, MOE_BLK), nxt, memory_space=pltpu.SMEM),
                  pl.BlockSpec((MOE_BLK, 1), lambda i, be: (i, 0)),
                  pl.BlockSpec(memory_space=pl.ANY),
                  pl.BlockSpec((None, D, EXP_FF), lambda i, be: (be[i], 0, 0)),
                  pl.BlockSpec((None, D, EXP_FF), lambda i, be: (be[i], 0, 0)),
                  pl.BlockSpec((None, EXP_FF, D), lambda i, be: (be[i], 0, 0))],
        out_specs=pl.BlockSpec((MOE_BLK, D), lambda i, be: (i, 0)),
        scratch_shapes=[pltpu.VMEM((2, MOE_BLK, D), F32), pltpu.SemaphoreType.DMA((2,))])
    return pl.pallas_call(
        _expert_kernel, out_shape=jax.ShapeDtypeStruct((nb * MOE_BLK, D), F32), grid_spec=grid_spec,
        compiler_params=_cp(("arbitrary",)), name="moe_experts",
    )(block_e, tok3, tok3, buf_w.reshape(nb * MOE_BLK, 1), t, w1, w3, w2)


MOE_TT = 32


def _combine_kernel(dst_ref, dstn_ref, ys_hbm, sh_ref, x_ref, g_ref, mod_ref, o_ref, buf, sem):
    i = pl.program_id(0)
    n = pl.num_programs(0)
    slot = i % 2
    nrow = MOE_TT * TOP_K

    @pl.when(i == 0)
    def _():
        _gather_rows(dst_ref, nrow, ys_hbm, buf, sem, 0)

    @pl.when(i + 1 < n)
    def _():
        _gather_rows(dstn_ref, nrow, ys_hbm, buf, sem, 1 - slot)

    _wait_rows(nrow, ys_hbm, buf, sem, slot)
    f = sh_ref[...]
    for k in range(TOP_K):
        f = f + buf[slot, k * MOE_TT:(k + 1) * MOE_TT, :]
    ms = jnp.mean(f * f, axis=-1, keepdims=True)
    o_ref[...] = x_ref[...] + mod_ref[5:6, :] * (f * lax.rsqrt(ms + EPS) * g_ref[...])


def moe_combine(ys, dest, sh, x, M, g, mods):
    nt = M // MOE_TT
    nrow = MOE_TT * TOP_K
    d3 = dest.reshape(nt, MOE_TT, TOP_K).transpose(0, 2, 1).reshape(nt, 1, nrow)
    nxt = lambda i: (jnp.minimum(i + 1, nt - 1), 0, 0)
    return pl.pallas_call(
        _combine_kernel, out_shape=jax.ShapeDtypeStruct((M, D), F32), grid=(nt,),
        in_specs=[pl.BlockSpec((None, 1, nrow), lambda i: (i, 0, 0), memory_space=pltpu.SMEM),
                  pl.BlockSpec((None, 1, nrow), nxt, memory_space=pltpu.SMEM),
                  pl.BlockSpec(memory_space=pl.ANY),
                  pl.BlockSpec((MOE_TT, D), lambda i: (i, 0)),
                  pl.BlockSpec((MOE_TT, D), lambda i: (i, 0)),
                  pl.BlockSpec((1, D), lambda i: (0, 0)),
                  pl.BlockSpec((None, 6, D), lambda i: (_seq_of_rowblock(i, MOE_TT), 0, 0))],
        out_specs=pl.BlockSpec((MOE_TT, D), lambda i: (i, 0)),
        scratch_shapes=[pltpu.VMEM((2, nrow, D), F32), pltpu.SemaphoreType.DMA((2,))],
        compiler_params=_cp(("arbitrary",)), name="moe_combine",
    )(d3, d3, ys, sh, x, g.reshape(1, D), mods)


def moe_layer(x, M, g_pre, g_post, mods, mp):
    t, idx_t, wts_t, sh = moe_router(x, M, g_pre, mods, mp)
    mk = M * TOP_K
    nb = -(-(mk + N_EXP * (MOE_BLK - 1)) // MOE_BLK)
    flat_e = jnp.transpose(idx_t).reshape(mk)
    flat_w = jnp.transpose(wts_t).reshape(mk)
    onehot = (flat_e[:, None] == jnp.arange(N_EXP, dtype=jnp.int32)[None, :]).astype(jnp.int32)
    csum = jnp.cumsum(onehot, axis=0)
    rank = jnp.take_along_axis(csum, flat_e[:, None], axis=1)[:, 0] - 1
    counts = csum[-1]
    padded = (counts + MOE_BLK - 1) // MOE_BLK * MOE_BLK
    pend = jnp.cumsum(padded)
    dest = (pend - padded)[flat_e] + rank
    flat_tok = jnp.arange(mk, dtype=jnp.int32) // TOP_K
    buf_tok = jnp.zeros((nb * MOE_BLK,), jnp.int32).at[dest].set(flat_tok)
    buf_w = jnp.zeros((nb * MOE_BLK,), F32).at[dest].set(flat_w)
    block_e = jnp.minimum(jnp.searchsorted(pend, jnp.arange(nb, dtype=jnp.int32) * MOE_BLK, side='right'),
                          N_EXP - 1).astype(jnp.int32)
    ys = moe_experts(t, buf_tok, buf_w, block_e, mp['w1'].astype(BF16), mp['w3'].astype(BF16),
                     mp['w2'].astype(BF16))
    return moe_combine(ys, dest.astype(jnp.int32), sh, x, M, g_post, mods)


MOE_T = 256
MOE_CAP = 64
MOE_EPS = 4


def _router2_kernel(x_ref, g_ref, mod_ref, rw_ref, rb_ref, s1_ref, s3_ref, s2_ref, t_ref, wt_ref, cnt_ref, sh_ref):
    x = x_ref[...]
    ms = jnp.mean(x * x, axis=-1, keepdims=True)
    t = x * lax.rsqrt(ms + EPS) * g_ref[...] * (1.0 + mod_ref[4:5, :]) + mod_ref[3:4, :]
    tb = t.astype(BF16)
    t_ref[...] = tb
    sh_ref[...] = _dot((_silu(_dot(tb, s1_ref[...])) * _dot(tb, s3_ref[...])).astype(BF16), s2_ref[...])
    th, tl = _split(t)
    wh, wl = _split(rw_ref[...])
    lg = _dot_nt(wh, th) + (_dot_nt(wh, tl) + _dot_nt(wl, th))
    sc = _sigmoid(lg)
    sel = sc + rb_ref[...]
    tm = sel.shape[1]
    gsz = N_EXP // N_GRP
    ninf = -jnp.inf
    sel3 = sel.reshape(N_GRP, gsz, tm)
    i3 = lax.broadcasted_iota(jnp.int32, sel3.shape, 1)
    m1 = jnp.max(sel3, axis=1, keepdims=True)
    first = jnp.min(jnp.where(sel3 == m1, i3, gsz), axis=1, keepdims=True)
    m2 = jnp.max(jnp.where(i3 == first, ninf, sel3), axis=1, keepdims=True)
    grp = (m1 + m2).reshape(N_GRP, tm)
    gi = lax.broadcasted_iota(jnp.int32, grp.shape, 0)
    gmask = jnp.zeros(grp.shape, F32)
    for _ in range(TOPK_GRP):
        m = jnp.max(grp, axis=0, keepdims=True)
        pick = jnp.min(jnp.where(grp == m, gi, N_GRP), axis=0, keepdims=True)
        hit = gi == pick
        gmask = jnp.where(hit, 1.0, gmask)
        grp = jnp.where(hit, ninf, grp)
    emask = jnp.broadcast_to(gmask.reshape(N_GRP, 1, tm), (N_GRP, gsz, tm)).reshape(N_EXP, tm)
    msel = jnp.where(emask > 0.5, sel, ninf)
    ei = lax.broadcasted_iota(jnp.int32, msel.shape, 0)
    chosen = jnp.zeros(msel.shape, F32)
    for _ in range(TOP_K):
        m = jnp.max(msel, axis=0, keepdims=True)
        pick = jnp.min(jnp.where(msel == m, ei, N_EXP), axis=0, keepdims=True)
        hit = ei == pick
        chosen = jnp.where(hit, 1.0, chosen)
        msel = jnp.where(hit, ninf, msel)
    w = chosen * sc
    wt = w / jnp.sum(w, axis=0, keepdims=True) * ROUTED_SCALE
    wt_ref[...] = wt
    cnt_ref[...] = jnp.sum((wt > 0.0).astype(F32), axis=1, keepdims=True).astype(jnp.int32)


def moe_router2(x, M, g, mods, mp):
    tm = MOE_T
    full = lambda shape: pl.BlockSpec(shape, lambda i: (0,) * len(shape))
    return pl.pallas_call(
        _router2_kernel,
        out_shape=(jax.ShapeDtypeStruct((M, D), BF16), jax.ShapeDtypeStruct((M // tm, N_EXP, tm), F32),
                   jax.ShapeDtypeStruct((M // tm, N_EXP, 1), jnp.int32), jax.ShapeDtypeStruct((M, D), F32)),
        grid=(M // tm,),
        in_specs=[pl.BlockSpec((tm, D), lambda i: (i, 0)), full((1, D)),
                  pl.BlockSpec((None, 6, D), lambda i: (_seq_of_rowblock(i, tm), 0, 0)),
                  full((N_EXP, D)), full((N_EXP, 1)), full((D, EXP_FF)), full((D, EXP_FF)), full((EXP_FF, D))],
        out_specs=(pl.BlockSpec((tm, D), lambda i: (i, 0)), pl.BlockSpec((None, N_EXP, tm), lambda i: (i, 0, 0)),
                   pl.BlockSpec((None, N_EXP, 1), lambda i: (i, 0, 0)), pl.BlockSpec((tm, D), lambda i: (i, 0))),
        compiler_params=_cp(("parallel",)), name="moe_router",
    )(x, g.reshape(1, D), mods, jnp.transpose(mp['router_w']), mp['router_bias'].reshape(N_EXP, 1),
      mp['s1'].astype(BF16), mp['s3'].astype(BF16), mp['s2'].astype(BF16))


def _moe2_kernel(cnt_ref, t_ref, wt_ref, sh_ref, x_ref, g_ref, mod_ref, w1_ref, w3_ref, w2_ref, o_ref,
                 rank_sc, acc_sc):
    i = pl.program_id(0)
    eb = pl.program_id(1)
    T = MOE_T

    @pl.when(eb == 0)
    def _():
        picked = (wt_ref[...] > 0.0).astype(BF16)
        before = (lax.broadcasted_iota(jnp.int32, (T, T), 0) < lax.broadcasted_iota(jnp.int32, (T, T), 1))
        rank_sc[...] = _dot(picked, before.astype(BF16))
        acc_sc[...] = jnp.zeros_like(acc_sc)

    slot = lax.broadcasted_iota(jnp.int32, (MOE_CAP, T), 0).astype(F32)
    for j in range(MOE_EPS):
        e = eb * MOE_EPS + j
        n_tok = cnt_ref[i * N_EXP + e]
        w_row = wt_ref[pl.ds(e, 1), :]
        r_row = rank_sc[pl.ds(e, 1), :]

        def chunk(ci, carry, j=j, w_row=w_row, r_row=r_row):
            hit = ((r_row - (ci * MOE_CAP).astype(F32)) == slot) & (w_row > 0.0)
            pb = hit.astype(F32).astype(BF16)
            xg = _dot(pb, t_ref[...]).astype(BF16)
            h = _silu(_dot(xg, w1_ref[j])) * _dot(xg, w3_ref[j])
            y = _dot(h.astype(BF16), w2_ref[j])
            w_slot = jnp.sum(jnp.where(hit, w_row, 0.0), axis=1, keepdims=True)
            yw = (y * w_slot).astype(BF16)
            acc_sc[...] += lax.dot_general(pb, yw, (((0,), (0,)), ((), ())), preferred_element_type=F32)
            return carry

        lax.fori_loop(0, (n_tok + MOE_CAP - 1) // MOE_CAP, chunk, 0)

    @pl.when(eb == pl.num_programs(1) - 1)
    def _():
        f = acc_sc[...] + sh_ref[...]
        ms = jnp.mean(f * f, axis=-1, keepdims=True)
        o_ref[...] = x_ref[...] + mod_ref[5:6, :] * (f * lax.rsqrt(ms + EPS) * g_ref[...])


def moe_layer2(x, M, g_pre, g_post, mods, mp):
    t, wt, cnt, sh = moe_router2(x, M, g_pre, mods, mp)
    T = MOE_T
    grid_spec = pltpu.PrefetchScalarGridSpec(
        num_scalar_prefetch=1, grid=(M // T, N_EXP // MOE_EPS),
        in_specs=[pl.BlockSpec((T, D), lambda i, e, c: (i, 0)),
                  pl.BlockSpec((N_EXP, T), lambda i, e, c: (0, i)),
                  pl.BlockSpec((T, D), lambda i, e, c: (i, 0)),
                  pl.BlockSpec((T, D), lambda i, e, c: (i, 0)),
                  pl.BlockSpec((1, D), lambda i, e, c: (0, 0)),
                  pl.BlockSpec((None, 6, D), lambda i, e, c: (_seq_of_rowblock(i, T), 0, 0)),
                  pl.BlockSpec((MOE_EPS, D, EXP_FF), lambda i, e, c: (e, 0, 0)),
                  pl.BlockSpec((MOE_EPS, D, EXP_FF), lambda i, e, c: (e, 0, 0)),
                  pl.BlockSpec((MOE_EPS, EXP_FF, D), lambda i, e, c: (e, 0, 0))],
        out_specs=pl.BlockSpec((T, D), lambda i, e, c: (i, 0)),
        scratch_shapes=[pltpu.VMEM((N_EXP, T), F32), pltpu.VMEM((T, D), F32)])
    return pl.pallas_call(
        _moe2_kernel, out_shape=jax.ShapeDtypeStruct((M, D), F32), grid_spec=grid_spec,
        compiler_params=_cp(("parallel", "arbitrary")), name="moe_experts",
    )(cnt.reshape(-1), t, wt, sh, x, g_post.reshape(1, D), mods, mp['w1'].astype(BF16), mp['w3'].astype(BF16),
      mp['w2'].astype(BF16))


def _moe3_kernel(cnt_ref, t_ref, wt_ref, w1_ref, w3_ref, w2_ref, o_ref, rank_sc, *, nsub):
    i = pl.program_id(0)
    eb = pl.program_id(1)
    T, CAP, EPS = MOE_T, MOE_CAP, MOE_EPS

    @pl.when(eb == 0)
    def _():
        before = (lax.broadcasted_iota(jnp.int32, (T, T), 0) < lax.broadcasted_iota(jnp.int32, (T, T), 1))
        before = before.astype(BF16)
        for s in range(nsub):
            rank_sc[s] = _dot((wt_ref[s] > 0.0).astype(BF16), before)
        o_ref[...] = jnp.zeros_like(o_ref)

    slot = lax.broadcasted_iota(jnp.int32, (CAP, T), 0).astype(F32)

    def one_hot(s, e, first_slot):
        w_row = wt_ref[s, pl.ds(e, 1), :]
        r_row = rank_sc[s, pl.ds(e, 1), :]
        hit = ((r_row - first_slot) == slot) & (w_row > 0.0)
        w_slot = jnp.sum(jnp.where(hit, w_row, 0.0), axis=1, keepdims=True)
        return hit.astype(F32).astype(BF16), w_slot

    def swiglu(xg, j):
        h = _silu(_dot(xg, w1_ref[j])) * _dot(xg, w3_ref[j])
        return _dot(h.astype(BF16), w2_ref[j])

    hot = [[one_hot(s, eb * EPS + j, 0.0) for j in range(EPS)] for s in range(nsub)]
    pb = [jnp.concatenate([hot[s][j][0] for j in range(EPS)], axis=0) for s in range(nsub)]
    xg = [_dot(pb[s], t_ref[s * T:(s + 1) * T, :]).astype(BF16) for s in range(nsub)]
    y = [swiglu(jnp.concatenate([xg[s][j * CAP:(j + 1) * CAP] for s in range(nsub)], axis=0), j)
         for j in range(EPS)]
    for s in range(nsub):
        yw = jnp.concatenate([y[j][s * CAP:(s + 1) * CAP] * hot[s][j][1] for j in range(EPS)], axis=0)
        o_ref[s * T:(s + 1) * T, :] += lax.dot_general(pb[s], yw.astype(BF16), (((0,), (0,)), ((), ())),
                                                       preferred_element_type=F32)

    def pair(idx, carry):
        s = idx // EPS
        j = idx % EPS
        e = eb * EPS + j
        n_tok = cnt_ref[(i * nsub + s) * N_EXP + e]
        rows = pl.ds(pl.multiple_of(s * T, T), T)

        def chunk(ci, c2):
            p1, w_slot = one_hot(s, e, (ci * CAP).astype(F32))
            yw = (swiglu(_dot(p1, t_ref[rows, :]).astype(BF16), j) * w_slot).astype(BF16)
            o_ref[rows, :] += lax.dot_general(p1, yw, (((0,), (0,)), ((), ())), preferred_element_type=F32)
            return c2

        lax.fori_loop(1, (n_tok + CAP - 1) // CAP, chunk, 0)
        return carry

    lax.fori_loop(0, nsub * EPS, pair, 0)


def _moe_out_kernel(r_ref, sh_ref, x_ref, g_ref, mod_ref, o_ref):
    f = r_ref[...] + sh_ref[...]
    ms = jnp.mean(f * f, axis=-1, keepdims=True)
    o_ref[...] = x_ref[...] + mod_ref[5:6, :] * (f * lax.rsqrt(ms + EPS) * g_ref[...])


def moe_layer3(x, M, g_pre, g_post, mods, mp, nsub):
    t, wt, cnt, sh = moe_router2(x, M, g_pre, mods, mp)
    T = MOE_T
    TS = nsub * T
    grid_spec = pltpu.PrefetchScalarGridSpec(
        num_scalar_prefetch=1, grid=(M // TS, N_EXP // MOE_EPS),
        in_specs=[pl.BlockSpec((TS, D), lambda i, e, c: (i, 0)),
                  pl.BlockSpec((nsub, N_EXP, T), lambda i, e, c: (i, 0, 0)),
                  pl.BlockSpec((MOE_EPS, D, EXP_FF), lambda i, e, c: (e, 0, 0)),
                  pl.BlockSpec((MOE_EPS, D, EXP_FF), lambda i, e, c: (e, 0, 0)),
                  pl.BlockSpec((MOE_EPS, EXP_FF, D), lambda i, e, c: (e, 0, 0))],
        out_specs=pl.BlockSpec((TS, D), lambda i, e, c: (i, 0)),
        scratch_shapes=[pltpu.VMEM((nsub, N_EXP, T), F32)])
    routed = pl.pallas_call(
        functools.partial(_moe3_kernel, nsub=nsub), out_shape=jax.ShapeDtypeStruct((M, D), F32),
        grid_spec=grid_spec, compiler_params=_cp(("parallel", "arbitrary")), name="moe_experts",
    )(cnt.reshape(-1), t, wt, mp['w1'].astype(BF16), mp['w3'].astype(BF16), mp['w2'].astype(BF16))
    tm = 512
    return pl.pallas_call(
        _moe_out_kernel, out_shape=jax.ShapeDtypeStruct((M, D), F32), grid=(M // tm,),
        in_specs=[pl.BlockSpec((tm, D), lambda i: (i, 0)), pl.BlockSpec((tm, D), lambda i: (i, 0)),
                  pl.BlockSpec((tm, D), lambda i: (i, 0)), pl.BlockSpec((1, D), lambda i: (0, 0)),
                  pl.BlockSpec((None, 6, D), lambda i: (_seq_of_rowblock(i, tm), 0, 0))],
        out_specs=pl.BlockSpec((tm, D), lambda i: (i, 0)),
        compiler_params=_cp(("parallel",)), name="moe_output")(routed, sh, x, g_post.reshape(1, D), mods)


def kernel(x, c, ctx, c_ctx, mod_w, mod_b, norm_mix_pre, norm_mix_post, norm_ffn_pre, norm_ffn_post, router_w, router_bias, expert_w1, expert_w3, expert_w2, shared_w1, shared_w3, shared_w2, ev_w_in, ev_w_out, ssd_conv_w, ssd_conv_b, ssd_dt_bias, ssd_a_log, ssd_d, ssd_norm_w, hy_conv_w, hy_conv_b, hy_mlp_w0, hy_mlp_b0, hy_freq0, hy_mlp_w1, hy_mlp_b1, hy_freq1, hy_mlp_w2, hy_bias, od_w_in, od_w_out, rw_mu, rw_w0, rw_w_up, rw_a0, rw_a_up, rw_g_up, rw_k_k, rw_k_a, rw_r_k, rw_ln_w, rw_ln_b, at_q_norm, at_k_norm):
    xs = jnp.concatenate([x.reshape(N_LAT, D), ctx.reshape(BATCH * CTX, D)], axis=0)
    cvecs = jnp.zeros((8, D), F32).at[0:BATCH].set(c).at[BATCH].set(c_ctx)
    assert mod_w.shape[0] == 2, "one even (SSD | Hyena) layer followed by one odd (RWKV | attention) layer"

    def moe_params(i):
        return dict(router_w=router_w[i], router_bias=router_bias[i], w1=expert_w1[i], w3=expert_w3[i],
                    w2=expert_w2[i], s1=shared_w1[i], s3=shared_w3[i], s2=shared_w2[i])

    mods = modulation(cvecs, mod_w[0], mod_b[0])[:BATCH + 1].reshape(BATCH + 1, 6, D)
    ep = dict(w_in=ev_w_in[0], w_out=ev_w_out[0], ssd_conv_w=ssd_conv_w[0], ssd_conv_b=ssd_conv_b[0],
              ssd_dt_bias=ssd_dt_bias[0], ssd_a_log=ssd_a_log[0], ssd_d=ssd_d[0], ssd_norm_w=ssd_norm_w[0],
              hy_conv_w=hy_conv_w[0], hy_conv_b=hy_conv_b[0], hy_mlp_w0=hy_mlp_w0[0], hy_mlp_b0=hy_mlp_b0[0],
              hy_freq0=hy_freq0[0], hy_mlp_w1=hy_mlp_w1[0], hy_mlp_b1=hy_mlp_b1[0], hy_freq1=hy_freq1[0],
              hy_mlp_w2=hy_mlp_w2[0], hy_bias=hy_bias[0])
    xs = even_mixer(xs, mods, norm_mix_pre[0], norm_mix_post[0], ep)
    xs = moe_layer3(xs, N_ROWS, norm_ffn_pre[0], norm_ffn_post[0], mods, moe_params(0), 6)
    mods = modulation(cvecs, mod_w[1], mod_b[1])[:BATCH + 1].reshape(BATCH + 1, 6, D)
    op = dict(w_in=od_w_in[0], w_out=od_w_out[0], mu=rw_mu[0], w0=rw_w0[0], w_up=rw_w_up[0], a0=rw_a0[0],
              a_up=rw_a_up[0], g_up=rw_g_up[0], k_k=rw_k_k[0], k_a=rw_k_a[0], r_k=rw_r_k[0], ln_w=rw_ln_w[0],
              ln_b=rw_ln_b[0], q_norm=at_q_norm[0], k_norm=at_k_norm[0])
    xl = odd_mixer(xs, mods, norm_mix_pre[1], norm_mix_post[1], op)
    xl = moe_layer3(xl, N_LAT, norm_ffn_pre[1], norm_ffn_post[1], mods, moe_params(1), 8)
    return xl.reshape(BATCH, SEQ, D)
```

```python
import functools
import math

import jax
import jax.numpy as jnp
from jax import lax
from jax.experimental import pallas as pl
from jax.experimental.pallas import tpu as pltpu

F32 = jnp.float32
BF16 = jnp.bfloat16

D = 1024
BATCH = 2
SEQ = 8192
CTX = 256
N_LAT = BATCH * SEQ
N_ROWS = N_LAT + BATCH * CTX
EPS = 1e-6
GRID_W = 64

SSD_HEADS = 16
SSD_P = 64
SSD_G = 2
SSD_S = 128
SSD_Q = 128
HY_W = 1024
HY_EMB = 33
HY_HID = 64

RW_H = 16
RW_N = 64
RW_CHUNK = 128
RW_GN_EPS = 64e-5

AT_KV = 4
AT_HD = 64

N_EXP = 64
TOP_K = 8
N_GRP = 8
TOPK_GRP = 4
EXP_FF = 256
ROUTED_SCALE = 2.5

VMEM_LIMIT = 56 * 1024 * 1024


def _cp(sem, vmem=None):
    return pltpu.CompilerParams(dimension_semantics=sem, vmem_limit_bytes=vmem or VMEM_LIMIT)


def _dot(a, b):
    return jnp.dot(a, b, preferred_element_type=F32)


def _dot_nt(a, b):
    return lax.dot_general(a, b, (((1,), (1,)), ((), ())), preferred_element_type=F32)


def _split(x):
    hi = x.astype(BF16)
    lo = (x - hi.astype(F32)).astype(BF16)
    return hi, lo


def _dot3(a, b):
    ah, al = _split(a)
    bh, bl = _split(b)
    return _dot(ah, bh) + (_dot(ah, bl) + _dot(al, bh))


def _dot2l(a, b):
    ah, al = _split(a)
    return _dot(ah, b) + _dot(al, b)


def _dot2r(a, b):
    bh, bl = _split(b)
    return _dot(a, bh) + _dot(a, bl)


def _silu(x):
    return x * (1.0 / (1.0 + jnp.exp(-x)))


def _sigmoid(x):
    return 1.0 / (1.0 + jnp.exp(-x))


def _softplus(x):
    return jnp.maximum(x, 0.0) + jnp.log(1.0 + jnp.exp(-jnp.abs(x)))


def _seq_of_rowblock(i, tm):
    return jnp.minimum((i * tm) // SEQ, 2)


def _nmm_kernel(x_ref, g_ref, mod_ref, w_ref, o_ref, a_sc, *, shift_i, scale_i):
    @pl.when(pl.program_id(1) == 0)
    def _():
        x = x_ref[...]
        ms = jnp.mean(x * x, axis=-1, keepdims=True)
        y = x * lax.rsqrt(ms + EPS) * g_ref[...]
        h = y * (1.0 + mod_ref[scale_i:scale_i + 1, :]) + mod_ref[shift_i:shift_i + 1, :]
        a_sc[...] = h.astype(BF16)

    o_ref[...] = _dot(a_sc[...], w_ref[...])


def norm_mod_matmul(x, g, mods, w, shift_i, scale_i, tm=512, tn=None, name="nmm"):
    M = x.shape[0]
    N = w.shape[1]
    tn = tn or N
    return pl.pallas_call(
        functools.partial(_nmm_kernel, shift_i=shift_i, scale_i=scale_i),
        out_shape=jax.ShapeDtypeStruct((M, N), F32),
        grid=(M // tm, N // tn),
        in_specs=[pl.BlockSpec((tm, D), lambda i, j: (i, 0)),
                  pl.BlockSpec((1, D), lambda i, j: (0, 0)),
                  pl.BlockSpec((None, 6, D), lambda i, j: (_seq_of_rowblock(i, tm), 0, 0)),
                  pl.BlockSpec((D, tn), lambda i, j: (0, j))],
        out_specs=pl.BlockSpec((tm, tn), lambda i, j: (i, j)),
        scratch_shapes=[pltpu.VMEM((tm, D), BF16)],
        compiler_params=_cp(("parallel", "arbitrary")), name=name)(x, g.reshape(1, D), mods, w)


def _outproj_kernel(a1_ref, a2_ref, w_ref, x_ref, g_ref, mod_ref, o_ref, *, gate_i):
    y = _dot(a1_ref[...].astype(BF16), w_ref[0:D, :]) + _dot(a2_ref[...].astype(BF16), w_ref[D:2 * D, :])
    ms = jnp.mean(y * y, axis=-1, keepdims=True)
    o_ref[...] = x_ref[...] + mod_ref[gate_i:gate_i + 1, :] * (y * lax.rsqrt(ms + EPS) * g_ref[...])


def outproj_residual(a1, a2, w, x, g, mods, gate_i, tm=256, name="outproj"):
    M = a1.shape[0]
    return pl.pallas_call(
        functools.partial(_outproj_kernel, gate_i=gate_i),
        out_shape=jax.ShapeDtypeStruct((M, D), F32),
        grid=(M // tm,),
        in_specs=[pl.BlockSpec((tm, D), lambda i: (i, 0)),
                  pl.BlockSpec((tm, D), lambda i: (i, 0)),
                  pl.BlockSpec((2 * D, D), lambda i: (0, 0)),
                  pl.BlockSpec((tm, D), lambda i: (i, 0)),
                  pl.BlockSpec((1, D), lambda i: (0, 0)),
                  pl.BlockSpec((None, 6, D), lambda i: (_seq_of_rowblock(i, tm), 0, 0))],
        out_specs=pl.BlockSpec((tm, D), lambda i: (i, 0)),
        compiler_params=_cp(("parallel",)), name=name)(a1, a2, w, x, g.reshape(1, D), mods)


def _mod_kernel(c_ref, w_ref, b_ref, o_ref):
    o_ref[...] = _dot3(_silu(c_ref[...]), w_ref[...]) + b_ref[...]


def modulation(cvecs, w, b):
    N = w.shape[1]
    tn = 1024
    return pl.pallas_call(
        _mod_kernel, out_shape=jax.ShapeDtypeStruct((8, N), F32), grid=(N // tn,),
        in_specs=[pl.BlockSpec((8, D), lambda j: (0, 0)),
                  pl.BlockSpec((D, tn), lambda j: (0, j)),
                  pl.BlockSpec((1, tn), lambda j: (0, j))],
        out_specs=pl.BlockSpec((8, tn), lambda j: (0, j)),
        compiler_params=_cp(("parallel",)), name="modulation")(cvecs, w, b.reshape(1, N))


CONV_TM = 256


def _conv3_kernel(x_ref, prev_ref, next_ref, w_ref, b_ref, o_ref, *, act):
    tm = CONV_TM
    row0 = pl.program_id(0) * tm
    seq_len = jnp.where(row0 < N_LAT, SEQ, CTX)
    pos = jnp.where(row0 < N_LAT, row0 % SEQ, (row0 - N_LAT) % CTX)
    cur = x_ref[...]
    rows = lax.broadcasted_iota(jnp.int32, cur.shape, 0)
    prev_row = prev_ref[7:8, :] * (pos > 0).astype(F32)
    next_row = next_ref[0:1, :] * (pos + tm < seq_len).astype(F32)
    xm1 = jnp.where(rows == 0, prev_row, pltpu.roll(cur, 1, 0))
    xp1 = jnp.where(rows == tm - 1, next_row, pltpu.roll(cur, tm - 1, 0))
    y = xm1 * w_ref[0:1, :] + cur * w_ref[1:2, :] + xp1 * w_ref[2:3, :] + b_ref[...]
    o_ref[...] = _silu(y) if act else y


def dwconv3(p, col0, ncols, w, b, act, name, cb=1024):
    tm = CONV_TM
    cb = math.gcd(cb, math.gcd(col0, ncols)) if col0 else math.gcd(cb, ncols)
    r8 = tm // 8
    n8 = N_ROWS // 8
    c0 = col0 // cb
    return pl.pallas_call(
        functools.partial(_conv3_kernel, act=act),
        out_shape=jax.ShapeDtypeStruct((N_ROWS, ncols), F32),
        grid=(N_ROWS // tm, ncols // cb),
        in_specs=[pl.BlockSpec((tm, cb), lambda i, j: (i, c0 + j)),
                  pl.BlockSpec((8, cb), lambda i, j: (jnp.maximum(i * r8 - 1, 0), c0 + j)),
                  pl.BlockSpec((8, cb), lambda i, j: (jnp.minimum((i + 1) * r8, n8 - 1), c0 + j)),
                  pl.BlockSpec((3, cb), lambda i, j: (0, j)),
                  pl.BlockSpec((1, cb), lambda i, j: (0, j))],
        out_specs=pl.BlockSpec((tm, cb), lambda i, j: (i, j)),
        compiler_params=_cp(("parallel", "parallel")), name=name)(p, p, p, jnp.transpose(w), b.reshape(1, ncols))


def _ssd_kernel(xs_ref, bm_ref, cm_ref, dt_ref, dtT_ref, bias_ref, biasT_ref, alog_ref, alogT_ref,
                y_ref, st_ref):
    d = pl.program_id(0)
    c = pl.program_id(2)
    Q = SSD_Q
    HG = SSD_HEADS // SSD_G

    @pl.when(c == 0)
    def _():
        st_ref[...] = jnp.zeros_like(st_ref)

    isb = d == 1
    sgn = 1 - 2 * d
    ii = lax.broadcasted_iota(jnp.int32, (Q, Q), 0)
    jj = lax.broadcasted_iota(jnp.int32, (Q, Q), 1)
    tri = (jj <= ii).astype(BF16)
    triT = (ii <= jj).astype(BF16)
    mask = sgn * (ii - jj) >= 0
    xs = xs_ref[...]
    G = range(SSD_G)
    dt = [_softplus(dt_ref[g] + bias_ref[g]) for g in G]
    dtT = [_softplus(dtT_ref[g] + biasT_ref[g]) for g in G]
    a = [dt[g] * (-jnp.exp(alog_ref[g])) for g in G]
    aT = [dtT[g] * (-jnp.exp(alogT_ref[g])) for g in G]
    cs = [_dot2r(tri, a[g]) for g in G]
    csT = [_dot2l(aT[g], triT) for g in G]
    tot = [cs[g][Q - 1:Q, :] for g in G]
    p = [jnp.where(isb, a[g] - cs[g], cs[g]) for g in G]
    pT = [jnp.where(isb, aT[g] - csT[g], csT[g]) for g in G]
    dec_out = [jnp.exp(jnp.where(isb, tot[g], 0.0) + p[g]) for g in G]
    dec_state = [jnp.exp(jnp.where(isb, 0.0, tot[g]) - p[g]) for g in G]
    chunk_dec = [jnp.exp(tot[g]) for g in G]
    bm = [bm_ref[:, g * SSD_S:(g + 1) * SSD_S].astype(BF16) for g in G]
    cm = [cm_ref[:, g * SSD_S:(g + 1) * SSD_S].astype(BF16) for g in G]
    cb = [_dot_nt(cm[g], bm[g]) for g in G]
    nh = SSD_HEADS

    def spread(cols, width):
        v = jnp.concatenate(cols, axis=1)
        head = lax.broadcasted_iota(jnp.int32, (nh, nh * width), 1) // width
        e = (head == lax.broadcasted_iota(jnp.int32, (nh, nh * width), 0)).astype(BF16)
        h1 = v.astype(BF16)
        r1 = v - h1.astype(F32)
        h2 = r1.astype(BF16)
        h3 = (r1 - h2.astype(F32)).astype(BF16)
        return _dot(h1, e) + (_dot(h2, e) + _dot(h3, e))

    dt_x = spread(dt, SSD_P)
    dout_x = spread(dec_out, SSD_P)
    dst_x = spread(dec_state, SSD_P)
    p_x = spread(p, Q)
    xh_all = xs * dt_x
    xdec_all = (xh_all * dst_x).astype(BF16)
    xh_all = xh_all.astype(BF16)
    GH = [(g, h) for g in G for h in range(HG)]
    NH = range(len(GH))
    lm = [(cb[g] * jnp.exp(jnp.where(mask, p_x[:, n * Q:(n + 1) * Q] - pT[g][h:h + 1, :], -1e30))).astype(BF16)
          for n, (g, h) in enumerate(GH)]
    s_old = [st_ref[n] for n in NH]
    y_in = [_dot(lm[n], xh_all[:, n * SSD_P:(n + 1) * SSD_P]) for n in NH]
    y_st = [_dot(cm[g], s_old[n].astype(BF16)) for n, (g, h) in enumerate(GH)]
    upd = [lax.dot_general(bm[g], xdec_all[:, n * SSD_P:(n + 1) * SSD_P], (((0,), (0,)), ((), ())),
                           preferred_element_type=F32) for n, (g, h) in enumerate(GH)]
    for n, (g, h) in enumerate(GH):
        st_ref[n] = chunk_dec[g][:, h:h + 1] * s_old[n] + upd[n]
    y_ref[...] = jnp.concatenate(y_in, axis=1) + dout_x * jnp.concatenate(y_st, axis=1)


def _ssd_rowblock(d, b, c):
    n_ctx = CTX // SSD_Q
    n_lat = SEQ // SSD_Q
    cc = jnp.where(d == 0, c, n_ctx - 1 - c)
    lc = jnp.where(d == 0, c - n_ctx, n_ctx + n_lat - 1 - c)
    return jnp.where(c < n_ctx, N_LAT // SSD_Q + b * n_ctx + cc, b * n_lat + lc)


def ssd_scan(xbc, dt_raw, dt_bias, a_log):
    HG = SSD_HEADS // SSD_G
    W = SSD_HEADS * SSD_P
    dsel = dt_raw[:, :2 * SSD_HEADS].reshape(N_ROWS, 2, SSD_G, HG).transpose(1, 2, 0, 3)
    dselT = dsel.transpose(0, 1, 3, 2)
    bias = dt_bias.reshape(2, SSD_G, 1, HG)
    biasT = dt_bias.reshape(2, SSD_G, HG, 1)
    alog = a_log.reshape(2, SSD_G, 1, HG)
    alogT = a_log.reshape(2, SSD_G, HG, 1)
    nch = (CTX + SEQ) // SSD_Q
    rb = _ssd_rowblock
    GS = SSD_G * SSD_S
    par = lambda shape: pl.BlockSpec((None,) + shape, lambda d, b, c: (d, 0, 0, 0))
    return pl.pallas_call(
        _ssd_kernel,
        out_shape=jax.ShapeDtypeStruct((2, N_ROWS, W), F32),
        grid=(2, BATCH, nch),
        in_specs=[pl.BlockSpec((SSD_Q, W), lambda d, b, c: (rb(d, b, c), 0)),
                  pl.BlockSpec((SSD_Q, GS), lambda d, b, c: (rb(d, b, c), W // GS)),
                  pl.BlockSpec((SSD_Q, GS), lambda d, b, c: (rb(d, b, c), W // GS + 1)),
                  pl.BlockSpec((None, SSD_G, SSD_Q, HG), lambda d, b, c: (d, 0, rb(d, b, c), 0)),
                  pl.BlockSpec((None, SSD_G, HG, SSD_Q), lambda d, b, c: (d, 0, 0, rb(d, b, c))),
                  par((SSD_G, 1, HG)), par((SSD_G, HG, 1)), par((SSD_G, 1, HG)), par((SSD_G, HG, 1))],
        out_specs=pl.BlockSpec((None, SSD_Q, W), lambda d, b, c: (d, rb(d, b, c), 0)),
        scratch_shapes=[pltpu.VMEM((SSD_HEADS, SSD_S, SSD_P), F32)],
        compiler_params=_cp(("parallel", "parallel", "arbitrary")), name="ssd_scan",
    )(xbc, xbc, xbc, dsel, dselT, bias, biasT, alog, alogT)


def _ssd_out_kernel(yf_ref, yb_ref, xs_ref, z_ref, dskip_ref, nw_ref, o_ref):
    y = yf_ref[...] + yb_ref[...] + xs_ref[...] * dskip_ref[...]
    y = y * _silu(z_ref[...])
    gs = SSD_HEADS * SSD_P // SSD_G
    parts = []
    for g in range(SSD_G):
        yg = y[:, g * gs:(g + 1) * gs]
        parts.append(yg * lax.rsqrt(jnp.mean(yg * yg, axis=-1, keepdims=True) + EPS))
    o_ref[...] = jnp.concatenate(parts, axis=1) * nw_ref[...]


def ssd_output(y2, xbc, p, zcol, d_skip, norm_w, tm=256):
    W = SSD_HEADS * SSD_P
    dexp = jnp.repeat(d_skip, SSD_P).reshape(1, W)
    return pl.pallas_call(
        _ssd_out_kernel, out_shape=jax.ShapeDtypeStruct((N_ROWS, W), F32), grid=(N_ROWS // tm,),
        in_specs=[pl.BlockSpec((None, tm, W), lambda i: (0, i, 0)),
                  pl.BlockSpec((None, tm, W), lambda i: (1, i, 0)),
                  pl.BlockSpec((tm, W), lambda i: (i, 0)),
                  pl.BlockSpec((tm, W), lambda i: (i, zcol // W)),
                  pl.BlockSpec((1, W), lambda i: (0, 0)),
                  pl.BlockSpec((1, W), lambda i: (0, 0))],
        out_specs=pl.BlockSpec((tm, W), lambda i: (i, 0)),
        compiler_params=_cp(("parallel",)), name="ssd_output")(y2, y2, xbc, p, dexp, norm_w.reshape(1, W))


def _hyfilt_kernel(f_ref, w0_ref, b0_ref, fr0_ref, w1_ref, b1_ref, fr1_ref, w2_ref, dl_ref, h_ref, ss_ref, *,
                   n_tiles):
    f = f_ref[...]
    h = jnp.sin(fr0_ref[...] * (_dot3(f, w0_ref[...]) + b0_ref[...]))
    h = jnp.sin(fr1_ref[...] * (_dot3(h, w1_ref[...]) + b1_ref[...]))
    h = _dot3(h, w2_ref[...])
    h = h * jnp.exp(-f[:, 0:1] * dl_ref[...])
    side = pl.program_id(0) // n_tiles
    j = pl.program_id(0) % n_tiles

    @pl.when(j == 0)
    def _():
        ss_ref[...] = jnp.zeros_like(ss_ref)

    ss_ref[...] += jnp.sum(h * h, axis=0, keepdims=True)
    row = lax.broadcasted_iota(jnp.int32, (h.shape[0], 1), 0) + j * h.shape[0]
    h_ref[...] = jnp.where((side == 1) & (row == 0), 0.0, h)


def hyena_filter_taps(L, hp):
    pos = jnp.arange(L, dtype=F32)
    t = pos / (L - 1)
    bands = (HY_EMB - 1) // 2
    freqs = jnp.linspace(1e-4, bands - 1, bands, dtype=F32)
    ang = (2.0 * math.pi / L) * pos[:, None] * freqs[None, :]
    feats = jnp.concatenate([t[:, None], jnp.cos(ang), -jnp.sin(ang)], axis=-1)
    feats = jnp.pad(feats, ((0, 0), (0, 128 - HY_EMB)))
    feats = jnp.concatenate([feats, jnp.flip(feats, axis=0)], axis=0)
    w0 = jnp.pad(hp['hy_mlp_w0'], ((0, 128 - HY_EMB), (0, 0)))
    min_decay = math.log(1e-2) / 1.5
    max_decay = math.log(1e-2) / 0.3
    deltas = jnp.abs(jnp.linspace(min_decay, max_decay, HY_W, dtype=F32))
    dl = jnp.tile(deltas, 2).reshape(1, 2 * HY_W)
    w2 = hp['hy_mlp_w2'].reshape(HY_HID, 2, 2, HY_W).transpose(0, 2, 1, 3).reshape(HY_HID, 4 * HY_W)
    tl = min(L, 512)
    n_tiles = L // tl
    NS = 2 * HY_W
    full = lambda shape: pl.BlockSpec(shape, lambda i: (0, 0))
    return pl.pallas_call(
        functools.partial(_hyfilt_kernel, n_tiles=n_tiles),
        out_shape=(jax.ShapeDtypeStruct((2 * L, NS), F32), jax.ShapeDtypeStruct((1, 2 * NS), F32)),
        grid=(2 * n_tiles,),
        in_specs=[pl.BlockSpec((tl, 128), lambda i: (i, 0)), full((128, HY_HID)), full((1, HY_HID)),
                  full((1, HY_HID)), full((HY_HID, HY_HID)), full((1, HY_HID)), full((1, HY_HID)),
                  pl.BlockSpec((HY_HID, NS), lambda i: (0, i // n_tiles)), full((1, NS))],
        out_specs=(pl.BlockSpec((tl, NS), lambda i: (i, 0)), pl.BlockSpec((1, NS), lambda i: (0, i // n_tiles))),
        compiler_params=_cp(("arbitrary",)), name="hyena_filter",
    )(feats, w0, hp['hy_mlp_b0'].reshape(1, -1), hp['hy_freq0'].reshape(1, -1), hp['hy_mlp_w1'],
      hp['hy_mlp_b1'].reshape(1, -1), hp['hy_freq1'].reshape(1, -1), w2, dl)


def _cis(num, den):
    ang = (2.0 * math.pi / den) * (num % den).astype(F32)
    return jnp.cos(ang), -jnp.sin(ang)


def _fft_consts(NB, BS):
    N = NB * BS
    h = NB // 2
    k1 = jnp.arange(h, dtype=jnp.int32)
    j = jnp.arange(NB, dtype=jnp.int32)
    re, im = _cis(j[None, :] * (2 * k1[:, None] + 1), 2 * NB)
    f1 = jnp.concatenate([re, im], axis=0)
    neg = jnp.where(j >= h, -1.0, 1.0)[None, :]
    f1_data = f1[:, :h]
    f1_filt = f1 * neg
    f1_inv = (2.0 / N) * jnp.concatenate([re[:, :h].T, im[:, :h].T], axis=1)
    r = jnp.arange(BS, dtype=jnp.int32)
    k2 = jnp.arange(BS, dtype=jnp.int32)
    kk = 2 * k1[:, None, None] + 2 * NB * k2[None, :, None] + 1
    gre, gim = _cis(kk * r[None, None, :], 2 * N)
    gf = jnp.concatenate([jnp.concatenate([gre, -gim], axis=2), jnp.concatenate([gim, gre], axis=2)], axis=1)
    gret, gimt = gre.transpose(0, 2, 1), gim.transpose(0, 2, 1)
    gi = jnp.concatenate([jnp.concatenate([gret, gimt], axis=2), jnp.concatenate([-gimt, gret], axis=2)], axis=1)
    return (f1_data.astype(BF16), f1_filt.astype(BF16), f1_inv.astype(BF16), gf.astype(BF16), gi.astype(BF16))


FFT_PAD = 8


FFT_LW = 128


def _fft_fwd_kernel(ua_ref, ub_ref, f1_ref, g_ref, o_ref, t_sc, *, NB, BS, nj, kg):
    pitch = NB + FFT_PAD
    u_refs = (ua_ref, ub_ref)

    @pl.when(pl.program_id(2) == 0)
    def _():
        f1 = f1_ref[...]

        def body(r, carry):
            xr = jnp.concatenate([u[pl.ds(r, nj, stride=BS), :] for u in u_refs], axis=1).astype(BF16)
            res = _dot(f1, xr)
            for hh in range(2):
                t_sc[hh, pl.ds(pl.multiple_of(r * pitch, 8), NB), :] = res[:, hh * FFT_LW:(hh + 1) * FFT_LW]
            return carry

        lax.fori_loop(0, BS, body, 0, unroll=8)

    k0 = pl.program_id(2) * kg
    for i in range(kg):
        are = jnp.concatenate([t_sc[hh, pl.ds(k0 + i, BS, stride=pitch), :] for hh in range(2)], axis=1)
        aim = jnp.concatenate([t_sc[hh, pl.ds(k0 + i + NB // 2, BS, stride=pitch), :] for hh in range(2)], axis=1)
        a = jnp.concatenate([are, aim], axis=0).astype(BF16)
        o_ref[i] = _dot(g_ref[i], a)


def fft_fwd(u, col0, nbatch, nj, f1, gf, NB, BS, kg=8):
    h = NB // 2
    kg = min(kg, h)
    lw = FFT_LW
    ct = 2 * lw
    return pl.pallas_call(
        functools.partial(_fft_fwd_kernel, NB=NB, BS=BS, nj=nj, kg=kg),
        out_shape=jax.ShapeDtypeStruct((nbatch, h, 2 * BS, HY_W), F32),
        grid=(nbatch, HY_W // ct, h // kg),
        in_specs=[pl.BlockSpec((nj * BS, lw), lambda b, c, k: (b, col0 // lw + 2 * c), pipeline_mode=pl.Buffered(1)),
                  pl.BlockSpec((nj * BS, lw), lambda b, c, k: (b, col0 // lw + 2 * c + 1),
                               pipeline_mode=pl.Buffered(1)),
                  pl.BlockSpec((NB, nj), lambda b, c, k: (0, 0)),
                  pl.BlockSpec((kg, 2 * BS, 2 * BS), lambda b, c, k: (k, 0, 0))],
        out_specs=pl.BlockSpec((None, kg, 2 * BS, ct), lambda b, c, k: (b, k, 0, c)),
        scratch_shapes=[pltpu.VMEM((2, BS * (NB + FFT_PAD), lw), F32)],
        compiler_params=_cp(("parallel", "parallel", "arbitrary")), name="hyena_fft_fwd")(u, u, f1, gf)


def _cmul(u, h, half):
    ure, uim = u[:half], u[half:]
    hre, him = h[:half], h[half:]
    return jnp.concatenate([ure * hre - uim * him, ure * him + uim * hre], axis=0)


def _fft_inv_kernel(us_ref, hs_ref, gi_ref, f1i_ref, o_ref, t_sc, y_sc, *, NB, BS, kg):
    ks = pl.program_id(2)
    pitch = 2 * BS + FFT_PAD
    for i in range(kg):
        y = _cmul(us_ref[i], hs_ref[i], BS).astype(BF16)
        row = pl.multiple_of((ks * kg + i) * pitch, 8)
        res = _dot(gi_ref[i], y)
        for hh in range(2):
            t_sc[hh, pl.ds(row, 2 * BS), :] = res[:, hh * FFT_LW:(hh + 1) * FFT_LW]

    @pl.when(ks == pl.num_programs(2) - 1)
    def _():
        f1i = f1i_ref[...]

        def body(r, carry):
            bre = jnp.concatenate([t_sc[hh, pl.ds(r, NB // 2, stride=pitch), :] for hh in range(2)], axis=1)
            bim = jnp.concatenate([t_sc[hh, pl.ds(r + BS, NB // 2, stride=pitch), :] for hh in range(2)], axis=1)
            b = jnp.concatenate([bre, bim], axis=0).astype(BF16)
            res = _dot(f1i, b)
            for hh in range(2):
                y_sc[hh, pl.ds(r, NB // 2, stride=BS), :] = res[:, hh * FFT_LW:(hh + 1) * FFT_LW]
            return carry

        lax.fori_loop(0, BS, body, 0, unroll=8)
        o_ref[...] = jnp.concatenate([y_sc[0], y_sc[1]], axis=1)


def fft_inv(us, hs, gi, f1i, NB, BS, kg=8):
    nbatch, h = us.shape[0], NB // 2
    kg = min(kg, h)
    L = h * BS
    ct = 2 * FFT_LW
    return pl.pallas_call(
        functools.partial(_fft_inv_kernel, NB=NB, BS=BS, kg=kg),
        out_shape=jax.ShapeDtypeStruct((nbatch * L, HY_W), F32),
        grid=(nbatch, HY_W // ct, h // kg),
        in_specs=[pl.BlockSpec((None, kg, 2 * BS, ct), lambda b, c, k: (b, k, 0, c)),
                  pl.BlockSpec((None, kg, 2 * BS, ct), lambda b, c, k: (0, k, 0, c)),
                  pl.BlockSpec((kg, 2 * BS, 2 * BS), lambda b, c, k: (k, 0, 0)),
                  pl.BlockSpec((h, NB), lambda b, c, k: (0, 0))],
        out_specs=pl.BlockSpec((L, ct), lambda b, c, k: (b, c)),
        scratch_shapes=[pltpu.VMEM((2, h * (2 * BS + FFT_PAD), FFT_LW), F32), pltpu.VMEM((2, L, FFT_LW), F32)],
        compiler_params=_cp(("parallel", "parallel", "arbitrary")), name="hyena_fft_inv")(us, hs, gi, f1i)


def _dft_consts(L):
    N = 2 * L
    k = jnp.arange(L, dtype=jnp.int32)
    n = jnp.arange(N, dtype=jnp.int32)
    re, im = _cis(n[None, :] * (2 * k[:, None] + 1), 2 * N)
    f = jnp.concatenate([re, im], axis=0)
    neg = jnp.where(n >= L, -1.0, 1.0)[None, :]
    fi = (2.0 / N) * jnp.concatenate([re[:, :L].T, im[:, :L].T], axis=1)
    return f[:, :L].astype(BF16), (f * neg).astype(BF16), fi.astype(BF16)


def _cdft_kernel(f_ref, x_ref, o_ref):
    o_ref[...] = _dot(f_ref[...], x_ref[...].astype(BF16))


def dft_fwd(x, f, row0, col0, nbatch, ct=256):
    M, K = f.shape
    return pl.pallas_call(
        _cdft_kernel, out_shape=jax.ShapeDtypeStruct((nbatch, M, HY_W), F32),
        grid=(nbatch, HY_W // ct),
        in_specs=[pl.BlockSpec((M, K), lambda b, c: (0, 0)),
                  pl.BlockSpec((K, ct), lambda b, c: (row0 // K + b, col0 // ct + c))],
        out_specs=pl.BlockSpec((None, M, ct), lambda b, c: (b, 0, c)),
        compiler_params=_cp(("parallel", "parallel")), name="hyena_dft_fwd")(f, x)


def _cdft_inv_kernel(us_ref, hs_ref, fi_ref, o_ref):
    half = us_ref.shape[0] // 2
    o_ref[...] = _dot(fi_ref[...], _cmul(us_ref[...], hs_ref[...], half).astype(BF16))


def dft_inv(us, hs, fi, ct=256):
    nbatch, M2, _ = us.shape
    L = fi.shape[0]
    return pl.pallas_call(
        _cdft_inv_kernel, out_shape=jax.ShapeDtypeStruct((nbatch * L, HY_W), F32),
        grid=(nbatch, HY_W // ct),
        in_specs=[pl.BlockSpec((None, M2, ct), lambda b, c: (b, 0, c)),
                  pl.BlockSpec((None, M2, ct), lambda b, c: (0, 0, c)),
                  pl.BlockSpec((L, M2), lambda b, c: (0, 0))],
        out_specs=pl.BlockSpec((L, ct), lambda b, c: (b, c)),
        compiler_params=_cp(("parallel", "parallel")), name="hyena_dft_inv")(us, hs, fi)


def _hy_gate_kernel(g_ref, y_ref, u_ref, ss_ref, b_ref, o_ref):
    scale = lax.rsqrt(ss_ref[0:1, :] + ss_ref[1:2, :] + 1e-6)
    o_ref[...] = g_ref[...] * (y_ref[...] * scale + u_ref[...] * b_ref[...])


def _hy_gate2_kernel(g_ref, yl_ref, yc_ref, ul_ref, uc_ref, ssl_ref, ssc_ref, b_ref, o_ref, *, n_lat_t):
    is_lat = pl.program_id(0) < n_lat_t
    y = jnp.where(is_lat, yl_ref[...], yc_ref[...])
    uin = jnp.where(is_lat, ul_ref[...], uc_ref[...])
    ss = jnp.where(is_lat, ssl_ref[...], ssc_ref[...])
    scale = lax.rsqrt(ss[0:1, :] + ss[1:2, :] + 1e-6)
    o_ref[...] = g_ref[...] * (y * scale + uin * b_ref[...])


def hy_gate(gate, gcol, grow, y, uin, ucol, urow, ss, order, bias, tm=256):
    M = y.shape[0]
    return pl.pallas_call(
        _hy_gate_kernel, out_shape=jax.ShapeDtypeStruct((M, HY_W), F32), grid=(M // tm,),
        in_specs=[pl.BlockSpec((tm, HY_W), lambda i: (grow // tm + i, gcol // HY_W)),
                  pl.BlockSpec((tm, HY_W), lambda i: (i, 0)),
                  pl.BlockSpec((tm, HY_W), lambda i: (urow // tm + i, ucol // HY_W)),
                  pl.BlockSpec((None, 2, HY_W), lambda i: (order, 0, 0)),
                  pl.BlockSpec((None, 1, HY_W), lambda i: (order, 0, 0))],
        out_specs=pl.BlockSpec((tm, HY_W), lambda i: (i, 0)),
        compiler_params=_cp(("parallel",)), name="hyena_gate")(gate, y, uin, ss, bias)


def hyena(u, hp):
    C = HY_W
    bias = hp['hy_bias'].reshape(2, 1, C)
    NB = BS = int(round(math.sqrt(2 * SEQ)))
    f1d, f1f, f1i, gf, gi = _fft_consts(NB, BS)
    taps, ss = hyena_filter_taps(SEQ, hp)
    ss = ss.reshape(2, 2, C).transpose(1, 0, 2)
    conv_l = lambda zin, zcol, order: fft_inv(fft_fwd(zin, zcol, BATCH, NB // 2, f1d, gf, NB, BS),
                                              fft_fwd(taps, order * C, 1, NB, f1f, gf, NB, BS), gi, f1i, NB, BS)
    z1_lat = hy_gate(u, 0, 0, conv_l(u, 2 * C, 0), u, 2 * C, 0, ss, 0, bias)
    y2_lat = conv_l(z1_lat, 0, 1)
    fd, ff, fi = _dft_consts(CTX)
    taps_c, ss_c = hyena_filter_taps(CTX, hp)
    ss_c = ss_c.reshape(2, 2, C).transpose(1, 0, 2)
    conv_c = lambda zin, zrow, zcol, order: dft_inv(dft_fwd(zin, fd, zrow, zcol, BATCH),
                                                    dft_fwd(taps_c, ff, 0, order * C, 1), fi)
    z1_ctx = hy_gate(u, 0, N_LAT, conv_c(u, N_LAT, 2 * C, 0), u, 2 * C, N_LAT, ss_c, 0, bias)
    y2_ctx = conv_c(z1_ctx, 0, 0, 1)
    tm = 256
    n_lat_t = N_LAT // tm
    lat = lambda i: (jnp.minimum(i, n_lat_t - 1), 0)
    ctx = lambda i: (jnp.maximum(i - n_lat_t, 0), 0)
    return pl.pallas_call(
        functools.partial(_hy_gate2_kernel, n_lat_t=n_lat_t),
        out_shape=jax.ShapeDtypeStruct((N_ROWS, C), F32), grid=(N_ROWS // tm,),
        in_specs=[pl.BlockSpec((tm, C), lambda i: (i, 1)),
                  pl.BlockSpec((tm, C), lat), pl.BlockSpec((tm, C), ctx),
                  pl.BlockSpec((tm, C), lat), pl.BlockSpec((tm, C), ctx),
                  pl.BlockSpec((None, 2, C), lambda i: (1, 0, 0)), pl.BlockSpec((None, 2, C), lambda i: (1, 0, 0)),
                  pl.BlockSpec((None, 1, C), lambda i: (1, 0, 0))],
        out_specs=pl.BlockSpec((tm, C), lambda i: (i, 0)),
        compiler_params=_cp(("parallel",)), name="hyena_gate2",
    )(u, y2_lat, y2_ctx, z1_lat, z1_ctx, ss, ss_c, bias)


EV_SSD_IN = SSD_HEADS * SSD_P
EV_XBC = EV_SSD_IN + 2 * SSD_G * SSD_S
EV_PAD_N = 5760


def even_mixer(x, mods, g_pre, g_post, ep):
    o1 = EV_SSD_IN
    o2 = o1 + EV_XBC
    o3 = o2 + 2 * SSD_HEADS
    w = ep['w_in']
    n_in = w.shape[1]
    hw = 3 * HY_W
    w_perm = jnp.concatenate([w[:, o3:], w[:, :o2], w[:, o2:o3],
                              jnp.zeros((D, EV_PAD_N - n_in), F32)], axis=1).astype(BF16)
    p = norm_mod_matmul(x, g_pre, mods, w_perm, 0, 1, tn=1920, name="even_in_proj")
    xbc = dwconv3(p, hw + o1, EV_XBC, ep['ssd_conv_w'], ep['ssd_conv_b'], True, "ssd_conv")
    u = dwconv3(p, 0, hw, ep['hy_conv_w'], ep['hy_conv_b'], False, "hyena_conv")
    dt_raw = p[:, hw + o2:hw + o2 + 2 * SSD_HEADS]
    y2 = ssd_scan(xbc, dt_raw, ep['ssd_dt_bias'], ep['ssd_a_log'])
    s = ssd_output(y2, xbc, p, hw, ep['ssd_d'], ep['ssd_norm_w'])
    zh = hyena(u, ep)
    return outproj_residual(s, zh, ep['w_out'].astype(BF16), x, g_post, mods, 2, name="even_out_proj")


def _head_sum(x, e, et):
    return _dot2l(_dot2l(x, e), et)


def _rw_prep_kernel(r_ref, k_ref, v_ref, lo_ref, w0_ref, wup_ref, a0_ref, aup_ref, gup_ref, kk_ref, ka_ref,
                    rk_ref, e_ref, et_ref, lw_ref, kd_ref, be_ref, kap_ref, g_ref, bonus_ref):
    r, k, v = r_ref[...], k_ref[...], v_ref[...]
    lo = lo_ref[...]
    wc, ac, gc = lo[:, 0:64], lo[:, 64:128], lo[:, 128:384]
    e, et = e_ref[...], et_ref[...]
    kk = k * kk_ref[...]
    kap = kk * lax.rsqrt(_head_sum(kk * kk, e, et) + 1e-12)
    kap_ref[...] = kap
    g_ref[...] = _dot(_sigmoid(gc).astype(BF16), gup_ref[...].astype(BF16))
    kd_sum = jnp.zeros_like(k)
    for d in range(2):
        wlog = -_softplus(-(w0_ref[d:d + 1, :] + _dot3(jnp.tanh(wc), wup_ref[d]))) - 0.5
        lw_ref[d] = -jnp.exp(wlog)
        a = _sigmoid(a0_ref[d:d + 1, :] + _dot(ac.astype(BF16), aup_ref[d].astype(BF16)))
        kd = k * (1.0 + (a - 1.0) * ka_ref[...])
        kd_ref[d] = kd
        be_ref[d] = kap * a
        kd_sum = kd_sum + kd
    bonus_ref[...] = _head_sum(r * kd_sum * rk_ref[...], e, et) * v


def rwkv_prepare(code, lora, op, tm=256):
    W = RW_H * RW_N
    heads = jnp.arange(W, dtype=jnp.int32) // RW_N
    e = (heads[:, None] == jnp.arange(128, dtype=jnp.int32)[None, :]).astype(BF16)
    et = jnp.transpose(e)
    gup = jnp.pad(op['g_up'], ((0, 256 - op['g_up'].shape[0]), (0, 0)))
    row = lambda a: a.reshape(1, W)
    full2 = lambda shape: pl.BlockSpec(shape, lambda i: (0,) * len(shape))
    outs = pl.pallas_call(
        _rw_prep_kernel,
        out_shape=(jax.ShapeDtypeStruct((2, N_ROWS, W), F32), jax.ShapeDtypeStruct((2, N_ROWS, W), F32),
                   jax.ShapeDtypeStruct((2, N_ROWS, W), F32), jax.ShapeDtypeStruct((N_ROWS, W), F32),
                   jax.ShapeDtypeStruct((N_ROWS, W), F32), jax.ShapeDtypeStruct((N_ROWS, W), F32)),
        grid=(N_ROWS // tm,),
        in_specs=[pl.BlockSpec((tm, W), lambda i: (i, 0)), pl.BlockSpec((tm, W), lambda i: (i, 1)),
                  pl.BlockSpec((tm, W), lambda i: (i, 2)), pl.BlockSpec((tm, 384), lambda i: (i, 0)),
                  full2((2, W)), full2((2, 64, W)), full2((2, W)), full2((2, 64, W)), full2((256, W)),
                  full2((1, W)), full2((1, W)), full2((1, W)), full2((W, 128)), full2((128, W))],
        out_specs=(pl.BlockSpec((2, tm, W), lambda i: (0, i, 0)), pl.BlockSpec((2, tm, W), lambda i: (0, i, 0)),
                   pl.BlockSpec((2, tm, W), lambda i: (0, i, 0)), pl.BlockSpec((tm, W), lambda i: (i, 0)),
                   pl.BlockSpec((tm, W), lambda i: (i, 0)), pl.BlockSpec((tm, W), lambda i: (i, 0))),
        compiler_params=_cp(("parallel",)), name="rwkv_prepare",
    )(code, code, code, lora, op['w0'], op['w_up'], op['a0'], op['a_up'], gup, row(op['k_k']), row(op['k_a']),
      row(op['r_k']), e, et)
    return outs


def _rw_scan_kernel(r_ref, v_ref, lw_ref, kd_ref, be_ref, kap_ref, y_ref, st_ref):
    d = pl.program_id(0)
    c = pl.program_id(2)
    C = RW_CHUNK
    N = RW_N

    @pl.when(c == 0)
    def _():
        st_ref[...] = jnp.zeros_like(st_ref)

    isb = d == 1
    sgn = 1 - 2 * d
    ii = lax.broadcasted_iota(jnp.int32, (C, C), 0)
    jj = lax.broadcasted_iota(jnp.int32, (C, C), 1)
    dif = sgn * (ii - jj)
    incl = dif >= 0
    strict = dif > 0
    tri = incl.astype(BF16)
    eye = (ii == jj).astype(F32)
    blk = [(ii >> s) == (jj >> s) for s in range(3, C.bit_length() - 1)]
    masks = [blk[0]] + [blk[l] & ~blk[l - 1] for l in range(1, len(blk))] + [~blk[-1]]
    lw = lw_ref[...]
    cum = _dot2r(tri, lw)
    ec = jnp.exp(cum)
    en = jnp.exp(-cum)
    ea = jnp.exp(cum - lw)
    last = jnp.where(isb, cum[0:1, :], cum[C - 1:C, :])
    el = jnp.exp(last - cum)
    kap = kap_ref[...]
    r = r_ref[...]
    v = v_ref[...]
    a_t = -kap * ea
    r_t = r * ec
    b_t = be_ref[...] * en
    k_t = kd_ref[...] * en
    b_l = be_ref[...] * el
    k_l = kd_ref[...] * el
    pc = jnp.exp(last)
    H = range(RW_H)
    sl = [slice(h * N, (h + 1) * N) for h in H]
    bd = lambda a, b: _dot(a.astype(BF16), b.astype(BF16))
    tn = lambda a, b: lax.dot_general(a, b, (((0,), (0,)), ((), ())), preferred_element_type=F32)
    sc = [_dot_nt(jnp.concatenate([a_t[:, sl[h]], r_t[:, sl[h]]], axis=0).astype(BF16),
                  jnp.concatenate([b_t[:, sl[h]], k_t[:, sl[h]]], axis=0).astype(BF16)) for h in H]
    n_ab = [jnp.where(strict, sc[h][0:C, 0:C], 0.0) for h in H]
    a_ak = [jnp.where(strict, sc[h][0:C, C:2 * C], 0.0).astype(BF16) for h in H]
    m_rb = [jnp.where(incl, sc[h][C:2 * C, 0:C], 0.0).astype(BF16) for h in H]
    m_rk = [jnp.where(incl, sc[h][C:2 * C, C:2 * C], 0.0).astype(BF16) for h in H]
    vh = [v[:, sl[h]].astype(BF16) for h in H]
    d0 = [jnp.where(masks[0], n_ab[h], 0.0) for h in H]
    d2 = [bd(d0[h], d0[h]) for h in H]
    d4 = [bd(d2[h], d2[h]) for h in H]
    t = [bd(eye + d0[h], eye + d2[h]) for h in H]
    t = [bd(t[h], eye + d4[h]) for h in H]
    for m in masks[1:]:
        et = [bd(jnp.where(m, n_ab[h], 0.0), t[h]) for h in H]
        t = [t[h] + bd(t[h], et[h]) for h in H]
    amv = [_dot(jnp.concatenate([a_ak[h], m_rk[h]], axis=0), vh[h]) for h in H]
    wub = [bd(t[h], jnp.concatenate([a_t[:, sl[h]], amv[h][0:C]], axis=1)).astype(BF16) for h in H]
    kv = [tn(k_l[:, sl[h]].astype(BF16), vh[h]) for h in H]
    qy = [_dot(m_rb[h], wub[h]) + jnp.concatenate([r_t[:, sl[h]], amv[h][C:2 * C]], axis=1) for h in H]
    pp = [tn(b_l[:, sl[h]].astype(BF16), wub[h]) + jnp.concatenate([eye[0:N, 0:N] * pc[:, sl[h]], kv[h]], axis=1)
          for h in H]
    h_old = [st_ref[h] for h in H]
    ys = [bd(qy[h][:, 0:N], h_old[h]) + qy[h][:, N:2 * N] for h in H]
    for h in H:
        st_ref[h] = _dot3(pp[h][:, 0:N], h_old[h]) + pp[h][:, N:2 * N]
    y_ref[...] = jnp.concatenate(ys, axis=1)


def _rw_rowblock(d, b, c):
    n_ctx = CTX // RW_CHUNK
    n_lat = SEQ // RW_CHUNK
    cc = jnp.where(d == 0, c, n_ctx - 1 - c)
    lc = jnp.where(d == 0, c - n_ctx, n_ctx + n_lat - 1 - c)
    return jnp.where(c < n_ctx, N_LAT // RW_CHUNK + b * n_ctx + cc, b * n_lat + lc)


def rwkv_scan(code, lw, kd, be, kap):
    W = RW_H * RW_N
    nch = (CTX + SEQ) // RW_CHUNK
    rb = lambda d, b, c: _rw_rowblock(d, b, c)
    return pl.pallas_call(
        _rw_scan_kernel,
        out_shape=jax.ShapeDtypeStruct((2, N_ROWS, W), F32),
        grid=(2, BATCH, nch),
        in_specs=[pl.BlockSpec((RW_CHUNK, W), lambda d, b, c: (rb(d, b, c), 0)),
                  pl.BlockSpec((RW_CHUNK, W), lambda d, b, c: (rb(d, b, c), 2)),
                  pl.BlockSpec((None, RW_CHUNK, W), lambda d, b, c: (d, rb(d, b, c), 0)),
                  pl.BlockSpec((None, RW_CHUNK, W), lambda d, b, c: (d, rb(d, b, c), 0)),
                  pl.BlockSpec((None, RW_CHUNK, W), lambda d, b, c: (d, rb(d, b, c), 0)),
                  pl.BlockSpec((RW_CHUNK, W), lambda d, b, c: (rb(d, b, c), 0))],
        out_specs=pl.BlockSpec((None, RW_CHUNK, W), lambda d, b, c: (d, rb(d, b, c), 0)),
        scratch_shapes=[pltpu.VMEM((RW_H, RW_N, RW_N), F32)],
        compiler_params=_cp(("parallel", "parallel", "arbitrary")), name="rwkv_scan",
    )(code, code, lw, kd, be, kap)


def _rw_out_kernel(yf_ref, yb_ref, bonus_ref, g_ref, lnw_ref, lnb_ref, e_ref, et_ref, o_ref):
    e, et = e_ref[...], et_ref[...]
    y = yf_ref[...] + yb_ref[...]
    mean = _head_sum(y, e, et) * (1.0 / RW_N)
    yc = y - mean
    var = _head_sum(yc * yc, e, et) * (1.0 / RW_N)
    yn = yc * lax.rsqrt(var + RW_GN_EPS) * lnw_ref[...] + lnb_ref[...]
    o_ref[...] = (yn + bonus_ref[...]) * g_ref[...]


def rwkv_output(y2, bonus, g, op, tm=256):
    W = RW_H * RW_N
    heads = jnp.arange(W, dtype=jnp.int32) // RW_N
    e = (heads[:, None] == jnp.arange(128, dtype=jnp.int32)[None, :]).astype(BF16)
    et = jnp.transpose(e)
    M = N_LAT
    return pl.pallas_call(
        _rw_out_kernel, out_shape=jax.ShapeDtypeStruct((M, W), F32), grid=(M // tm,),
        in_specs=[pl.BlockSpec((None, tm, W), lambda i: (0, i, 0)), pl.BlockSpec((None, tm, W), lambda i: (1, i, 0)),
                  pl.BlockSpec((tm, W), lambda i: (i, 0)), pl.BlockSpec((tm, W), lambda i: (i, 0)),
                  pl.BlockSpec((1, W), lambda i: (0, 0)), pl.BlockSpec((1, W), lambda i: (0, 0)),
                  pl.BlockSpec((W, 128), lambda i: (0, 0)), pl.BlockSpec((128, W), lambda i: (0, 0))],
        out_specs=pl.BlockSpec((tm, W), lambda i: (i, 0)),
        compiler_params=_cp(("parallel",)), name="rwkv_output",
    )(y2, y2, bonus, g, op['ln_w'].reshape(1, W), op['ln_b'].reshape(1, W), e, et)


AT_Q = RW_H * AT_HD
AT_KW = AT_KV * AT_HD
AT_TQ = 512
AT_TK = 768


def _rope_tables(tm):
    half = AT_HD // 2
    inv = 10000.0 ** (-jnp.arange(0, half, 2, dtype=F32) / half)
    pos = jnp.arange(SEQ, dtype=jnp.int32)
    row = (pos // GRID_W).astype(F32)[:, None] * inv
    col = (pos % GRID_W).astype(F32)[:, None] * inv
    cos_h = jnp.concatenate([jnp.cos(row), jnp.cos(row), jnp.cos(col), jnp.cos(col)], axis=1)
    sin_h = jnp.concatenate([-jnp.sin(row), jnp.sin(row), -jnp.sin(col), jnp.sin(col)], axis=1)
    cos_t = jnp.concatenate([jnp.tile(cos_h, (1, 2)), jnp.ones((tm, 128), F32)], axis=0)
    sin_t = jnp.concatenate([jnp.tile(sin_h, (1, 2)), jnp.zeros((tm, 128), F32)], axis=0)
    return cos_t, sin_t


def _rot_partner(x):
    q = AT_HD // 4
    w = x.shape[1]
    lane = lax.broadcasted_iota(jnp.int32, x.shape, 1)
    return jnp.where((lane % (2 * q)) < q, pltpu.roll(x, w - q, 1), pltpu.roll(x, q, 1))


def _at_prep_kernel(q_ref, k_ref, v_ref, cos_ref, sin_ref, qn_ref, kn_ref, e_ref, et_ref, qo_ref, ko_ref, vo_ref):
    e, et = e_ref[...], et_ref[...]
    cos2, sin2 = cos_ref[...], sin_ref[...]

    def norm_rope(x, gain, nrep):
        ms = _head_sum(x * x, e[:x.shape[1]], et[:, :x.shape[1]]) * (1.0 / AT_HD)
        xn = x * lax.rsqrt(ms + EPS) * gain
        cos = jnp.tile(cos2, (1, nrep))
        sin = jnp.tile(sin2, (1, nrep))
        return xn * cos + _rot_partner(xn) * sin

    qn = norm_rope(q_ref[...], qn_ref[...], AT_Q // 128) * (AT_HD ** -0.5 * math.log2(math.e))
    qo_ref[...] = jnp.transpose(qn).astype(BF16)
    ko_ref[...] = norm_rope(k_ref[...], kn_ref[...], AT_KW // 128).astype(BF16)
    vo_ref[...] = jnp.transpose(v_ref[...]).astype(BF16)


def attention_prepare(p, q_norm, k_norm, tm=256):
    cos_t, sin_t = _rope_tables(tm)
    heads = jnp.arange(AT_Q, dtype=jnp.int32) // AT_HD
    e = (heads[:, None] == jnp.arange(128, dtype=jnp.int32)[None, :]).astype(BF16)
    et = jnp.transpose(e)
    tab = lambda i: jnp.where(i * tm < N_LAT, ((i * tm) % SEQ) // tm, SEQ // tm)
    n_lat_t, n_seq_t, n_ctx_t = N_LAT // tm, SEQ // tm, CTX // tm
    kvb = lambda i: jnp.where(i < n_lat_t, (i // n_seq_t) * (n_seq_t + n_ctx_t) + n_ctx_t + i % n_seq_t,
                              ((i - n_lat_t) // n_ctx_t) * (n_seq_t + n_ctx_t) + (i - n_lat_t) % n_ctx_t)
    qcol = (3 * RW_H * RW_N) // AT_Q
    kcol = (3 * RW_H * RW_N + AT_Q) // AT_KW
    return pl.pallas_call(
        _at_prep_kernel,
        out_shape=(jax.ShapeDtypeStruct((AT_Q, N_ROWS), BF16), jax.ShapeDtypeStruct((N_ROWS, AT_KW), BF16),
                   jax.ShapeDtypeStruct((AT_KW, N_ROWS), BF16)),
        grid=(N_ROWS // tm,),
        in_specs=[pl.BlockSpec((tm, AT_Q), lambda i: (i, qcol)),
                  pl.BlockSpec((tm, AT_KW), lambda i: (i, kcol)),
                  pl.BlockSpec((tm, AT_KW), lambda i: (i, kcol + 1)),
                  pl.BlockSpec((tm, 128), lambda i: (tab(i), 0)),
                  pl.BlockSpec((tm, 128), lambda i: (tab(i), 0)),
                  pl.BlockSpec((1, AT_Q), lambda i: (0, 0)),
                  pl.BlockSpec((1, AT_KW), lambda i: (0, 0)),
                  pl.BlockSpec((AT_Q, 128), lambda i: (0, 0)),
                  pl.BlockSpec((128, AT_Q), lambda i: (0, 0))],
        out_specs=(pl.BlockSpec((AT_Q, tm), lambda i: (0, i)), pl.BlockSpec((tm, AT_KW), lambda i: (kvb(i), 0)),
                   pl.BlockSpec((AT_KW, tm), lambda i: (0, kvb(i)))),
        compiler_params=_cp(("parallel",)), name="attn_prepare",
    )(p, p, p, cos_t, sin_t, jnp.tile(q_norm, AT_Q // AT_HD).reshape(1, AT_Q),
      jnp.tile(k_norm, AT_KV).reshape(1, AT_KW), e, et)


def _flash_t_kernel(qt_ref, k_ref, vt_ref, o_ref, m_sc, l_sc, acc_sc):
    ki = pl.program_id(2)
    nq = AT_Q // AT_HD
    gq = nq // AT_KV

    @pl.when(ki == 0)
    def _():
        m_sc[...] = jnp.full_like(m_sc, -1e30)
        l_sc[...] = jnp.zeros_like(l_sc)
        acc_sc[...] = jnp.zeros_like(acc_sc)

    for g in range(AT_KV):
        kg = k_ref[:, g * AT_HD:(g + 1) * AT_HD]
        vtg1 = jnp.concatenate([vt_ref[g * AT_HD:(g + 1) * AT_HD, :],
                                jnp.ones((16, vt_ref.shape[1]), BF16)], axis=0)
        hs = range(g * gq, (g + 1) * gq)
        st = [_dot(kg, qt_ref[h * AT_HD:(h + 1) * AT_HD, :]) for h in hs]
        m_old = [m_sc[h] for h in hs]
        m_new = [jnp.maximum(m_old[i], jnp.max(st[i], axis=0, keepdims=True)) for i in range(gq)]
        alpha = [jnp.exp2(m_old[i] - m_new[i]) for i in range(gq)]
        pt = [jnp.exp2(st[i] - m_new[i]).astype(BF16) for i in range(gq)]
        pv = [_dot(vtg1, pt[i]) for i in range(gq)]
        for i, h in enumerate(hs):
            l_sc[h] = alpha[i] * l_sc[h] + pv[i][AT_HD:AT_HD + 1, :]
            m_sc[h] = m_new[i]
            rows = pl.ds(h * AT_HD, AT_HD)
            acc_sc[rows, :] = alpha[i] * acc_sc[rows, :] + pv[i][0:AT_HD, :]

    @pl.when(ki == pl.num_programs(2) - 1)
    def _():
        inv = jnp.concatenate([jnp.broadcast_to(1.0 / l_sc[h], (AT_HD, l_sc.shape[2])) for h in range(nq)], axis=0)
        o_ref[...] = jnp.transpose(acc_sc[...] * inv)


def flash_attention_t(qt, k, vt):
    nq = AT_Q // AT_HD
    nk = (CTX + SEQ) // AT_TK
    kv_rb = lambda b, ki: b * nk + ki
    return pl.pallas_call(
        _flash_t_kernel,
        out_shape=jax.ShapeDtypeStruct((N_LAT, AT_Q), F32),
        grid=(BATCH, SEQ // AT_TQ, nk),
        in_specs=[pl.BlockSpec((AT_Q, AT_TQ), lambda b, qi, ki: (0, b * (SEQ // AT_TQ) + qi)),
                  pl.BlockSpec((AT_TK, AT_KW), lambda b, qi, ki: (kv_rb(b, ki), 0)),
                  pl.BlockSpec((AT_KW, AT_TK), lambda b, qi, ki: (0, kv_rb(b, ki)))],
        out_specs=pl.BlockSpec((AT_TQ, AT_Q), lambda b, qi, ki: (b * (SEQ // AT_TQ) + qi, 0)),
        scratch_shapes=[pltpu.VMEM((nq, 1, AT_TQ), F32), pltpu.VMEM((nq, 1, AT_TQ), F32),
                        pltpu.VMEM((AT_Q, AT_TQ), F32)],
        compiler_params=_cp(("parallel", "parallel", "arbitrary")), name="flash_attention")(qt, k, vt)


OD_PAD_N = 4992


def odd_mixer(x, mods, g_pre, g_post, op):
    W = RW_H * RW_N
    w = op['w_in']
    c3 = 3 * W
    code_n = c3 + 64 + 64 + 160
    w_perm = jnp.concatenate([w[:, :c3], w[:, code_n:], w[:, c3:code_n],
                              jnp.zeros((D, OD_PAD_N - w.shape[1]), F32)], axis=1).astype(BF16)
    p = norm_mod_matmul(x, g_pre, mods, w_perm, 0, 1, tn=1664, name="odd_in_proj")
    mu = op['mu']
    taps = lambda m: jnp.stack([0.5 * m, 1.0 - m, 0.5 * m], axis=1)
    code = dwconv3(p, 0, c3, taps(mu[:c3]), jnp.zeros((c3,), F32), False, "rwkv_shift")
    lo_col = c3 + AT_Q + 2 * AT_KW
    mu_lo = jnp.pad(mu[c3:], (0, 384 - (code_n - c3)))
    lora = dwconv3(p, lo_col, 384, taps(mu_lo), jnp.zeros((384,), F32), False, "rwkv_shift_lora", cb=128)
    lw, kd, be, kap, g, bonus = rwkv_prepare(code, lora, op)
    y2 = rwkv_scan(code, lw, kd, be, kap)
    o_l = rwkv_output(y2, bonus, g, op)
    q, k, v = attention_prepare(p, op['q_norm'], op['k_norm'])
    a_l = flash_attention_t(q, k, v)
    return outproj_residual(o_l, a_l, op['w_out'].astype(BF16), x, g_post, mods, 2, name="odd_out_proj")


MOE_T = 256
MOE_CAP = 64
MOE_EPS = 4


def _router2_kernel(x_ref, g_ref, mod_ref, rw_ref, rb_ref, s1_ref, s3_ref, s2_ref, t_ref, wt_ref, cnt_ref, sh_ref):
    x = x_ref[...]
    ms = jnp.mean(x * x, axis=-1, keepdims=True)
    t = x * lax.rsqrt(ms + EPS) * g_ref[...] * (1.0 + mod_ref[4:5, :]) + mod_ref[3:4, :]
    tb = t.astype(BF16)
    t_ref[...] = tb
    sh_ref[...] = _dot((_silu(_dot(tb, s1_ref[...])) * _dot(tb, s3_ref[...])).astype(BF16), s2_ref[...])
    th, tl = _split(t)
    wh, wl = _split(rw_ref[...])
    lg = _dot_nt(wh, th) + (_dot_nt(wh, tl) + _dot_nt(wl, th))
    sc = _sigmoid(lg)
    sel = sc + rb_ref[...]
    tm = sel.shape[1]
    gsz = N_EXP // N_GRP
    ninf = -jnp.inf
    sel3 = sel.reshape(N_GRP, gsz, tm)
    i3 = lax.broadcasted_iota(jnp.int32, sel3.shape, 1)
    m1 = jnp.max(sel3, axis=1, keepdims=True)
    first = jnp.min(jnp.where(sel3 == m1, i3, gsz), axis=1, keepdims=True)
    m2 = jnp.max(jnp.where(i3 == first, ninf, sel3), axis=1, keepdims=True)
    grp = (m1 + m2).reshape(N_GRP, tm)
    gi = lax.broadcasted_iota(jnp.int32, grp.shape, 0)
    gmask = jnp.zeros(grp.shape, F32)
    for _ in range(TOPK_GRP):
        m = jnp.max(grp, axis=0, keepdims=True)
        pick = jnp.min(jnp.where(grp == m, gi, N_GRP), axis=0, keepdims=True)
        hit = gi == pick
        gmask = jnp.where(hit, 1.0, gmask)
        grp = jnp.where(hit, ninf, grp)
    emask = jnp.broadcast_to(gmask.reshape(N_GRP, 1, tm), (N_GRP, gsz, tm)).reshape(N_EXP, tm)
    msel = jnp.where(emask > 0.5, sel, ninf)
    ei = lax.broadcasted_iota(jnp.int32, msel.shape, 0)
    chosen = jnp.zeros(msel.shape, F32)
    for _ in range(TOP_K):
        m = jnp.max(msel, axis=0, keepdims=True)
        pick = jnp.min(jnp.where(msel == m, ei, N_EXP), axis=0, keepdims=True)
        hit = ei == pick
        chosen = jnp.where(hit, 1.0, chosen)
        msel = jnp.where(hit, ninf, msel)
    w = chosen * sc
    wt = w / jnp.sum(w, axis=0, keepdims=True) * ROUTED_SCALE
    wt_ref[...] = wt
    cnt_ref[...] = jnp.sum((wt > 0.0).astype(F32), axis=1, keepdims=True).astype(jnp.int32)


def moe_router2(x, M, g, mods, mp):
    tm = MOE_T
    full = lambda shape: pl.BlockSpec(shape, lambda i: (0,) * len(shape))
    return pl.pallas_call(
        _router2_kernel,
        out_shape=(jax.ShapeDtypeStruct((M, D), BF16), jax.ShapeDtypeStruct((M // tm, N_EXP, tm), F32),
                   jax.ShapeDtypeStruct((M // tm, N_EXP, 1), jnp.int32), jax.ShapeDtypeStruct((M, D), F32)),
        grid=(M // tm,),
        in_specs=[pl.BlockSpec((tm, D), lambda i: (i, 0)), full((1, D)),
                  pl.BlockSpec((None, 6, D), lambda i: (_seq_of_rowblock(i, tm), 0, 0)),
                  full((N_EXP, D)), full((N_EXP, 1)), full((D, EXP_FF)), full((D, EXP_FF)), full((EXP_FF, D))],
        out_specs=(pl.BlockSpec((tm, D), lambda i: (i, 0)), pl.BlockSpec((None, N_EXP, tm), lambda i: (i, 0, 0)),
                   pl.BlockSpec((None, N_EXP, 1), lambda i: (i, 0, 0)), pl.BlockSpec((tm, D), lambda i: (i, 0))),
        compiler_params=_cp(("parallel",)), name="moe_router",
    )(x, g.reshape(1, D), mods, jnp.transpose(mp['router_w']), mp['router_bias'].reshape(N_EXP, 1),
      mp['s1'].astype(BF16), mp['s3'].astype(BF16), mp['s2'].astype(BF16))


def _moe3_kernel(cnt_ref, t_ref, wt_ref, w1_ref, w3_ref, w2_ref, o_ref, rank_sc, *, nsub):
    i = pl.program_id(0)
    eb = pl.program_id(1)
    T, CAP, EPS = MOE_T, MOE_CAP, MOE_EPS

    @pl.when(eb == 0)
    def _():
        before = (lax.broadcasted_iota(jnp.int32, (T, T), 0) < lax.broadcasted_iota(jnp.int32, (T, T), 1))
        before = before.astype(BF16)
        for s in range(nsub):
            rank_sc[s] = _dot((wt_ref[s] > 0.0).astype(BF16), before)
        o_ref[...] = jnp.zeros_like(o_ref)

    slot = lax.broadcasted_iota(jnp.int32, (CAP, T), 0).astype(F32)

    def one_hot(s, e, first_slot):
        w_row = wt_ref[s, pl.ds(e, 1), :]
        r_row = rank_sc[s, pl.ds(e, 1), :]
        hit = ((r_row - first_slot) == slot) & (w_row > 0.0)
        w_slot = jnp.sum(jnp.where(hit, w_row, 0.0), axis=1, keepdims=True)
        return hit.astype(F32).astype(BF16), w_slot

    def swiglu(xg, j):
        h = _silu(_dot(xg, w1_ref[j])) * _dot(xg, w3_ref[j])
        return _dot(h.astype(BF16), w2_ref[j])

    hot = [[one_hot(s, eb * EPS + j, 0.0) for j in range(EPS)] for s in range(nsub)]
    pb = [jnp.concatenate([hot[s][j][0] for j in range(EPS)], axis=0) for s in range(nsub)]
    xg = [_dot(pb[s], t_ref[s * T:(s + 1) * T, :]).astype(BF16) for s in range(nsub)]
    y = [swiglu(jnp.concatenate([xg[s][j * CAP:(j + 1) * CAP] for s in range(nsub)], axis=0), j)
         for j in range(EPS)]
    for s in range(nsub):
        yw = jnp.concatenate([y[j][s * CAP:(s + 1) * CAP] * hot[s][j][1] for j in range(EPS)], axis=0)
        o_ref[s * T:(s + 1) * T, :] += lax.dot_general(pb[s], yw.astype(BF16), (((0,), (0,)), ((), ())),
                                                       preferred_element_type=F32)

    def pair(idx, carry):
        s = idx // EPS
        j = idx % EPS
        e = eb * EPS + j
        n_tok = cnt_ref[(i * nsub + s) * N_EXP + e]
        rows = pl.ds(pl.multiple_of(s * T, T), T)

        def chunk(ci, c2):
            p1, w_slot = one_hot(s, e, (ci * CAP).astype(F32))
            yw = (swiglu(_dot(p1, t_ref[rows, :]).astype(BF16), j) * w_slot).astype(BF16)
            o_ref[rows, :] += lax.dot_general(p1, yw, (((0,), (0,)), ((), ())), preferred_element_type=F32)
            return c2

        lax.fori_loop(1, (n_tok + CAP - 1) // CAP, chunk, 0)
        return carry

    lax.fori_loop(0, nsub * EPS, pair, 0)


def _moe_out_kernel(r_ref, sh_ref, x_ref, g_ref, mod_ref, o_ref):
    f = r_ref[...] + sh_ref[...]
    ms = jnp.mean(f * f, axis=-1, keepdims=True)
    o_ref[...] = x_ref[...] + mod_ref[5:6, :] * (f * lax.rsqrt(ms + EPS) * g_ref[...])


def moe_layer3(x, M, g_pre, g_post, mods, mp, nsub):
    t, wt, cnt, sh = moe_router2(x, M, g_pre, mods, mp)
    T = MOE_T
    TS = nsub * T
    grid_spec = pltpu.PrefetchScalarGridSpec(
        num_scalar_prefetch=1, grid=(M // TS, N_EXP // MOE_EPS),
        in_specs=[pl.BlockSpec((TS, D), lambda i, e, c: (i, 0)),
                  pl.BlockSpec((nsub, N_EXP, T), lambda i, e, c: (i, 0, 0)),
                  pl.BlockSpec((MOE_EPS, D, EXP_FF), lambda i, e, c: (e, 0, 0)),
                  pl.BlockSpec((MOE_EPS, D, EXP_FF), lambda i, e, c: (e, 0, 0)),
                  pl.BlockSpec((MOE_EPS, EXP_FF, D), lambda i, e, c: (e, 0, 0))],
        out_specs=pl.BlockSpec((TS, D), lambda i, e, c: (i, 0)),
        scratch_shapes=[pltpu.VMEM((nsub, N_EXP, T), F32)])
    routed = pl.pallas_call(
        functools.partial(_moe3_kernel, nsub=nsub), out_shape=jax.ShapeDtypeStruct((M, D), F32),
        grid_spec=grid_spec, compiler_params=_cp(("parallel", "arbitrary")), name="moe_experts",
    )(cnt.reshape(-1), t, wt, mp['w1'].astype(BF16), mp['w3'].astype(BF16), mp['w2'].astype(BF16))
    tm = 512
    return pl.pallas_call(
        _moe_out_kernel, out_shape=jax.ShapeDtypeStruct((M, D), F32), grid=(M // tm,),
        in_specs=[pl.BlockSpec((tm, D), lambda i: (i, 0)), pl.BlockSpec((tm, D), lambda i: (i, 0)),
                  pl.BlockSpec((tm, D), lambda i: (i, 0)), pl.BlockSpec((1, D), lambda i: (0, 0)),
                  pl.BlockSpec((None, 6, D), lambda i: (_seq_of_rowblock(i, tm), 0, 0))],
        out_specs=pl.BlockSpec((tm, D), lambda i: (i, 0)),
        compiler_params=_cp(("parallel",)), name="moe_output")(routed, sh, x, g_post.reshape(1, D), mods)


def kernel(x, c, ctx, c_ctx, mod_w, mod_b, norm_mix_pre, norm_mix_post, norm_ffn_pre, norm_ffn_post, router_w, router_bias, expert_w1, expert_w3, expert_w2, shared_w1, shared_w3, shared_w2, ev_w_in, ev_w_out, ssd_conv_w, ssd_conv_b, ssd_dt_bias, ssd_a_log, ssd_d, ssd_norm_w, hy_conv_w, hy_conv_b, hy_mlp_w0, hy_mlp_b0, hy_freq0, hy_mlp_w1, hy_mlp_b1, hy_freq1, hy_mlp_w2, hy_bias, od_w_in, od_w_out, rw_mu, rw_w0, rw_w_up, rw_a0, rw_a_up, rw_g_up, rw_k_k, rw_k_a, rw_r_k, rw_ln_w, rw_ln_b, at_q_norm, at_k_norm):
    xs = jnp.concatenate([x.reshape(N_LAT, D), ctx.reshape(BATCH * CTX, D)], axis=0)
    cvecs = jnp.zeros((8, D), F32).at[0:BATCH].set(c).at[BATCH].set(c_ctx)
    assert mod_w.shape[0] == 2, "one even (SSD | Hyena) layer followed by one odd (RWKV | attention) layer"

    def moe_params(i):
        return dict(router_w=router_w[i], router_bias=router_bias[i], w1=expert_w1[i], w3=expert_w3[i],
                    w2=expert_w2[i], s1=shared_w1[i], s3=shared_w3[i], s2=shared_w2[i])

    mods = modulation(cvecs, mod_w[0], mod_b[0])[:BATCH + 1].reshape(BATCH + 1, 6, D)
    ep = dict(w_in=ev_w_in[0], w_out=ev_w_out[0], ssd_conv_w=ssd_conv_w[0], ssd_conv_b=ssd_conv_b[0],
              ssd_dt_bias=ssd_dt_bias[0], ssd_a_log=ssd_a_log[0], ssd_d=ssd_d[0], ssd_norm_w=ssd_norm_w[0],
              hy_conv_w=hy_conv_w[0], hy_conv_b=hy_conv_b[0], hy_mlp_w0=hy_mlp_w0[0], hy_mlp_b0=hy_mlp_b0[0],
              hy_freq0=hy_freq0[0], hy_mlp_w1=hy_mlp_w1[0], hy_mlp_b1=hy_mlp_b1[0], hy_freq1=hy_freq1[0],
              hy_mlp_w2=hy_mlp_w2[0], hy_bias=hy_bias[0])
    xs = even_mixer(xs, mods, norm_mix_pre[0], norm_mix_post[0], ep)
    xs = moe_layer3(xs, N_ROWS, norm_ffn_pre[0], norm_ffn_post[0], mods, moe_params(0), 6)
    mods = modulation(cvecs, mod_w[1], mod_b[1])[:BATCH + 1].reshape(BATCH + 1, 6, D)
    op = dict(w_in=od_w_in[0], w_out=od_w_out[0], mu=rw_mu[0], w0=rw_w0[0], w_up=rw_w_up[0], a0=rw_a0[0],
              a_up=rw_a_up[0], g_up=rw_g_up[0], k_k=rw_k_k[0], k_a=rw_k_a[0], r_k=rw_r_k[0], ln_w=rw_ln_w[0],
              ln_b=rw_ln_b[0], q_norm=at_q_norm[0], k_norm=at_k_norm[0])
    xl = odd_mixer(xs, mods, norm_mix_pre[1], norm_mix_post[1], op)
    xl = moe_layer3(xl, N_LAT, norm_ffn_pre[1], norm_ffn_post[1], mods, moe_params(1), 8)
    return xl.reshape(BATCH, SEQ, D)
```

```python
import functools
import math

import jax
import jax.numpy as jnp
from jax import lax
from jax.experimental import pallas as pl
from jax.experimental.pallas import tpu as pltpu

F32 = jnp.float32
BF16 = jnp.bfloat16

D = 1024
BATCH = 2
SEQ = 8192
CTX = 256
N_LAT = BATCH * SEQ
N_ROWS = N_LAT + BATCH * CTX
EPS = 1e-6
GRID_W = 64

SSD_HEADS = 16
SSD_P = 64
SSD_G = 2
SSD_S = 128
SSD_Q = 128
HY_W = 1024
HY_EMB = 33
HY_HID = 64

RW_H = 16
RW_N = 64
RW_CHUNK = 128
RW_GN_EPS = 64e-5

AT_KV = 4
AT_HD = 64

N_EXP = 64
TOP_K = 8
N_GRP = 8
TOPK_GRP = 4
EXP_FF = 256
ROUTED_SCALE = 2.5

VMEM_LIMIT = 56 * 1024 * 1024


def _cp(sem, vmem=None):
    return pltpu.CompilerParams(dimension_semantics=sem, vmem_limit_bytes=vmem or VMEM_LIMIT)


def _dot(a, b):
    return jnp.dot(a, b, preferred_element_type=F32)


def _dot_nt(a, b):
    return lax.dot_general(a, b, (((1,), (1,)), ((), ())), preferred_element_type=F32)


def _split(x):
    hi = x.astype(BF16)
    lo = (x - hi.astype(F32)).astype(BF16)
    return hi, lo


def _dot3(a, b):
    ah, al = _split(a)
    bh, bl = _split(b)
    return _dot(ah, bh) + (_dot(ah, bl) + _dot(al, bh))


def _dot2l(a, b):
    ah, al = _split(a)
    return _dot(ah, b) + _dot(al, b)


def _dot2r(a, b):
    bh, bl = _split(b)
    return _dot(a, bh) + _dot(a, bl)


def _silu(x):
    return x * (1.0 / (1.0 + jnp.exp(-x)))


def _sigmoid(x):
    return 1.0 / (1.0 + jnp.exp(-x))


def _softplus(x):
    return jnp.maximum(x, 0.0) + jnp.log(1.0 + jnp.exp(-jnp.abs(x)))


def _seq_of_rowblock(i, tm):
    return jnp.minimum((i * tm) // SEQ, 2)


def _nmm_kernel(x_ref, g_ref, mod_ref, w_ref, o_ref, a_sc, *, shift_i, scale_i):
    @pl.when(pl.program_id(1) == 0)
    def _():
        x = x_ref[...]
        ms = jnp.mean(x * x, axis=-1, keepdims=True)
        y = x * lax.rsqrt(ms + EPS) * g_ref[...]
        h = y * (1.0 + mod_ref[scale_i:scale_i + 1, :]) + mod_ref[shift_i:shift_i + 1, :]
        a_sc[...] = h.astype(BF16)

    o_ref[...] = _dot(a_sc[...], w_ref[...])


def norm_mod_matmul(x, g, mods, w, shift_i, scale_i, tm=512, tn=None, name="nmm"):
    M = x.shape[0]
    N = w.shape[1]
    tn = tn or N
    return pl.pallas_call(
        functools.partial(_nmm_kernel, shift_i=shift_i, scale_i=scale_i),
        out_shape=jax.ShapeDtypeStruct((M, N), F32),
        grid=(M // tm, N // tn),
        in_specs=[pl.BlockSpec((tm, D), lambda i, j: (i, 0)),
                  pl.BlockSpec((1, D), lambda i, j: (0, 0)),
                  pl.BlockSpec((None, 6, D), lambda i, j: (_seq_of_rowblock(i, tm), 0, 0)),
                  pl.BlockSpec((D, tn), lambda i, j: (0, j))],
        out_specs=pl.BlockSpec((tm, tn), lambda i, j: (i, j)),
        scratch_shapes=[pltpu.VMEM((tm, D), BF16)],
        compiler_params=_cp(("parallel", "arbitrary")), name=name)(x, g.reshape(1, D), mods, w)


def _outproj_kernel(a1_ref, a2_ref, w_ref, x_ref, g_ref, mod_ref, o_ref, *, gate_i):
    y = _dot(a1_ref[...].astype(BF16), w_ref[0:D, :]) + _dot(a2_ref[...].astype(BF16), w_ref[D:2 * D, :])
    ms = jnp.mean(y * y, axis=-1, keepdims=True)
    o_ref[...] = x_ref[...] + mod_ref[gate_i:gate_i + 1, :] * (y * lax.rsqrt(ms + EPS) * g_ref[...])


def outproj_residual(a1, a2, w, x, g, mods, gate_i, tm=256, name="outproj"):
    M = a1.shape[0]
    return pl.pallas_call(
        functools.partial(_outproj_kernel, gate_i=gate_i),
        out_shape=jax.ShapeDtypeStruct((M, D), F32),
        grid=(M // tm,),
        in_specs=[pl.BlockSpec((tm, D), lambda i: (i, 0)),
                  pl.BlockSpec((tm, D), lambda i: (i, 0)),
                  pl.BlockSpec((2 * D, D), lambda i: (0, 0)),
                  pl.BlockSpec((tm, D), lambda i: (i, 0)),
                  pl.BlockSpec((1, D), lambda i: (0, 0)),
                  pl.BlockSpec((None, 6, D), lambda i: (_seq_of_rowblock(i, tm), 0, 0))],
        out_specs=pl.BlockSpec((tm, D), lambda i: (i, 0)),
        compiler_params=_cp(("parallel",)), name=name)(a1, a2, w, x, g.reshape(1, D), mods)


def _mod_kernel(c_ref, w_ref, b_ref, o_ref):
    o_ref[...] = _dot3(_silu(c_ref[...]), w_ref[...]) + b_ref[...]


def modulation(cvecs, w, b):
    N = w.shape[1]
    tn = 1024
    return pl.pallas_call(
        _mod_kernel, out_shape=jax.ShapeDtypeStruct((8, N), F32), grid=(N // tn,),
        in_specs=[pl.BlockSpec((8, D), lambda j: (0, 0)),
                  pl.BlockSpec((D, tn), lambda j: (0, j)),
                  pl.BlockSpec((1, tn), lambda j: (0, j))],
        out_specs=pl.BlockSpec((8, tn), lambda j: (0, j)),
        compiler_params=_cp(("parallel",)), name="modulation")(cvecs, w, b.reshape(1, N))


CONV_TM = 256


def _conv3_kernel(x_ref, prev_ref, next_ref, w_ref, b_ref, o_ref, *, act):
    tm = CONV_TM
    row0 = pl.program_id(0) * tm
    seq_len = jnp.where(row0 < N_LAT, SEQ, CTX)
    pos = jnp.where(row0 < N_LAT, row0 % SEQ, (row0 - N_LAT) % CTX)
    cur = x_ref[...]
    rows = lax.broadcasted_iota(jnp.int32, cur.shape, 0)
    prev_row = prev_ref[7:8, :] * (pos > 0).astype(F32)
    next_row = next_ref[0:1, :] * (pos + tm < seq_len).astype(F32)
    xm1 = jnp.where(rows == 0, prev_row, pltpu.roll(cur, 1, 0))
    xp1 = jnp.where(rows == tm - 1, next_row, pltpu.roll(cur, tm - 1, 0))
    y = xm1 * w_ref[0:1, :] + cur * w_ref[1:2, :] + xp1 * w_ref[2:3, :] + b_ref[...]
    o_ref[...] = _silu(y) if act else y


def dwconv3(p, col0, ncols, w, b, act, name, cb=1024):
    tm = CONV_TM
    cb = math.gcd(cb, math.gcd(col0, ncols)) if col0 else math.gcd(cb, ncols)
    r8 = tm // 8
    n8 = N_ROWS // 8
    c0 = col0 // cb
    return pl.pallas_call(
        functools.partial(_conv3_kernel, act=act),
        out_shape=jax.ShapeDtypeStruct((N_ROWS, ncols), F32),
        grid=(N_ROWS // tm, ncols // cb),
        in_specs=[pl.BlockSpec((tm, cb), lambda i, j: (i, c0 + j)),
                  pl.BlockSpec((8, cb), lambda i, j: (jnp.maximum(i * r8 - 1, 0), c0 + j)),
                  pl.BlockSpec((8, cb), lambda i, j: (jnp.minimum((i + 1) * r8, n8 - 1), c0 + j)),
                  pl.BlockSpec((3, cb), lambda i, j: (0, j)),
                  pl.BlockSpec((1, cb), lambda i, j: (0, j))],
        out_specs=pl.BlockSpec((tm, cb), lambda i, j: (i, j)),
        compiler_params=_cp(("parallel", "parallel")), name=name)(p, p, p, jnp.transpose(w), b.reshape(1, ncols))


def _ssd_kernel(xs_ref, bm_ref, cm_ref, dt_ref, dtT_ref, bias_ref, biasT_ref, alog_ref, alogT_ref,
                y_ref, st_ref):
    d = pl.program_id(0)
    c = pl.program_id(2)
    Q = SSD_Q
    HG = SSD_HEADS // SSD_G

    @pl.when(c == 0)
    def _():
        st_ref[...] = jnp.zeros_like(st_ref)

    isb = d == 1
    sgn = 1 - 2 * d
    ii = lax.broadcasted_iota(jnp.int32, (Q, Q), 0)
    jj = lax.broadcasted_iota(jnp.int32, (Q, Q), 1)
    tri = (jj <= ii).astype(BF16)
    triT = (ii <= jj).astype(BF16)
    mask = sgn * (ii - jj) >= 0
    xs = xs_ref[...]
    G = range(SSD_G)
    dt = [_softplus(dt_ref[g] + bias_ref[g]) for g in G]
    dtT = [_softplus(dtT_ref[g] + biasT_ref[g]) for g in G]
    a = [dt[g] * (-jnp.exp(alog_ref[g])) for g in G]
    aT = [dtT[g] * (-jnp.exp(alogT_ref[g])) for g in G]
    cs = [_dot2r(tri, a[g]) for g in G]
    csT = [_dot2l(aT[g], triT) for g in G]
    tot = [cs[g][Q - 1:Q, :] for g in G]
    p = [jnp.where(isb, a[g] - cs[g], cs[g]) for g in G]
    pT = [jnp.where(isb, aT[g] - csT[g], csT[g]) for g in G]
    dec_out = [jnp.exp(jnp.where(isb, tot[g], 0.0) + p[g]) for g in G]
    dec_state = [jnp.exp(jnp.where(isb, 0.0, tot[g]) - p[g]) for g in G]
    chunk_dec = [jnp.exp(tot[g]) for g in G]
    bm = [bm_ref[:, g * SSD_S:(g + 1) * SSD_S].astype(BF16) for g in G]
    cm = [cm_ref[:, g * SSD_S:(g + 1) * SSD_S].astype(BF16) for g in G]
    cb = [_dot_nt(cm[g], bm[g]) for g in G]
    nh = SSD_HEADS

    def spread(cols, width):
        v = jnp.concatenate(cols, axis=1)
        head = lax.broadcasted_iota(jnp.int32, (nh, nh * width), 1) // width
        e = (head == lax.broadcasted_iota(jnp.int32, (nh, nh * width), 0)).astype(BF16)
        h1 = v.astype(BF16)
        r1 = v - h1.astype(F32)
        h2 = r1.astype(BF16)
        h3 = (r1 - h2.astype(F32)).astype(BF16)
        return _dot(h1, e) + (_dot(h2, e) + _dot(h3, e))

    dt_x = spread(dt, SSD_P)
    dout_x = spread(dec_out, SSD_P)
    dst_x = spread(dec_state, SSD_P)
    p_x = spread(p, Q)
    xh_all = xs * dt_x
    xdec_all = (xh_all * dst_x).astype(BF16)
    xh_all = xh_all.astype(BF16)
    GH = [(g, h) for g in G for h in range(HG)]
    NH = range(len(GH))
    lm = [(cb[g] * jnp.exp(jnp.where(mask, p_x[:, n * Q:(n + 1) * Q] - pT[g][h:h + 1, :], -1e30))).astype(BF16)
          for n, (g, h) in enumerate(GH)]
    s_old = [st_ref[n] for n in NH]
    y_in = [_dot(lm[n], xh_all[:, n * SSD_P:(n + 1) * SSD_P]) for n in NH]
    y_st = [_dot(cm[g], s_old[n].astype(BF16)) for n, (g, h) in enumerate(GH)]
    upd = [lax.dot_general(bm[g], xdec_all[:, n * SSD_P:(n + 1) * SSD_P], (((0,), (0,)), ((), ())),
                           preferred_element_type=F32) for n, (g, h) in enumerate(GH)]
    for n, (g, h) in enumerate(GH):
        st_ref[n] = chunk_dec[g][:, h:h + 1] * s_old[n] + upd[n]
    y_ref[...] = jnp.concatenate(y_in, axis=1) + dout_x * jnp.concatenate(y_st, axis=1)


def _ssd_rowblock(d, b, c):
    n_ctx = CTX // SSD_Q
    n_lat = SEQ // SSD_Q
    cc = jnp.where(d == 0, c, n_ctx - 1 - c)
    lc = jnp.where(d == 0, c - n_ctx, n_ctx + n_lat - 1 - c)
    return jnp.where(c < n_ctx, N_LAT // SSD_Q + b * n_ctx + cc, b * n_lat + lc)


def ssd_scan(xbc, dt_raw, dt_bias, a_log):
    HG = SSD_HEADS // SSD_G
    W = SSD_HEADS * SSD_P
    dsel = dt_raw[:, :2 * SSD_HEADS].reshape(N_ROWS, 2, SSD_G, HG).transpose(1, 2, 0, 3)
    dselT = dsel.transpose(0, 1, 3, 2)
    bias = dt_bias.reshape(2, SSD_G, 1, HG)
    biasT = dt_bias.reshape(2, SSD_G, HG, 1)
    alog = a_log.reshape(2, SSD_G, 1, HG)
    alogT = a_log.reshape(2, SSD_G, HG, 1)
    nch = (CTX + SEQ) // SSD_Q
    rb = _ssd_rowblock
    GS = SSD_G * SSD_S
    par = lambda shape: pl.BlockSpec((None,) + shape, lambda d, b, c: (d, 0, 0, 0))
    return pl.pallas_call(
        _ssd_kernel,
        out_shape=jax.ShapeDtypeStruct((2, N_ROWS, W), F32),
        grid=(2, BATCH, nch),
        in_specs=[pl.BlockSpec((SSD_Q, W), lambda d, b, c: (rb(d, b, c), 0)),
                  pl.BlockSpec((SSD_Q, GS), lambda d, b, c: (rb(d, b, c), W // GS)),
                  pl.BlockSpec((SSD_Q, GS), lambda d, b, c: (rb(d, b, c), W // GS + 1)),
                  pl.BlockSpec((None, SSD_G, SSD_Q, HG), lambda d, b, c: (d, 0, rb(d, b, c), 0)),
                  pl.BlockSpec((None, SSD_G, HG, SSD_Q), lambda d, b, c: (d, 0, 0, rb(d, b, c))),
                  par((SSD_G, 1, HG)), par((SSD_G, HG, 1)), par((SSD_G, 1, HG)), par((SSD_G, HG, 1))],
        out_specs=pl.BlockSpec((None, SSD_Q, W), lambda d, b, c: (d, rb(d, b, c), 0)),
        scratch_shapes=[pltpu.VMEM((SSD_HEADS, SSD_S, SSD_P), F32)],
        compiler_params=_cp(("parallel", "parallel", "arbitrary")), name="ssd_scan",
    )(xbc, xbc, xbc, dsel, dselT, bias, biasT, alog, alogT)


def _ssd_out_kernel(yf_ref, yb_ref, xs_ref, z_ref, dskip_ref, nw_ref, o_ref):
    y = yf_ref[...] + yb_ref[...] + xs_ref[...] * dskip_ref[...]
    y = y * _silu(z_ref[...])
    gs = SSD_HEADS * SSD_P // SSD_G
    parts = []
    for g in range(SSD_G):
        yg = y[:, g * gs:(g + 1) * gs]
        parts.append(yg * lax.rsqrt(jnp.mean(yg * yg, axis=-1, keepdims=True) + EPS))
    o_ref[...] = jnp.concatenate(parts, axis=1) * nw_ref[...]


def ssd_output(y2, xbc, p, zcol, d_skip, norm_w, tm=256):
    W = SSD_HEADS * SSD_P
    dexp = jnp.repeat(d_skip, SSD_P).reshape(1, W)
    return pl.pallas_call(
        _ssd_out_kernel, out_shape=jax.ShapeDtypeStruct((N_ROWS, W), F32), grid=(N_ROWS // tm,),
        in_specs=[pl.BlockSpec((None, tm, W), lambda i: (0, i, 0)),
                  pl.BlockSpec((None, tm, W), lambda i: (1, i, 0)),
                  pl.BlockSpec((tm, W), lambda i: (i, 0)),
                  pl.BlockSpec((tm, W), lambda i: (i, zcol // W)),
                  pl.BlockSpec((1, W), lambda i: (0, 0)),
                  pl.BlockSpec((1, W), lambda i: (0, 0))],
        out_specs=pl.BlockSpec((tm, W), lambda i: (i, 0)),
        compiler_params=_cp(("parallel",)), name="ssd_output")(y2, y2, xbc, p, dexp, norm_w.reshape(1, W))


def _hyfilt_kernel(f_ref, w0_ref, b0_ref, fr0_ref, w1_ref, b1_ref, fr1_ref, w2_ref, dl_ref, h_ref, ss_ref, *,
                   n_tiles):
    f = f_ref[...]
    h = jnp.sin(fr0_ref[...] * (_dot3(f, w0_ref[...]) + b0_ref[...]))
    h = jnp.sin(fr1_ref[...] * (_dot3(h, w1_ref[...]) + b1_ref[...]))
    h = _dot3(h, w2_ref[...])
    h = h * jnp.exp(-f[:, 0:1] * dl_ref[...])
    side = pl.program_id(0) // n_tiles
    j = pl.program_id(0) % n_tiles

    @pl.when(j == 0)
    def _():
        ss_ref[...] = jnp.zeros_like(ss_ref)

    ss_ref[...] += jnp.sum(h * h, axis=0, keepdims=True)
    row = lax.broadcasted_iota(jnp.int32, (h.shape[0], 1), 0) + j * h.shape[0]
    h_ref[...] = jnp.where((side == 1) & (row == 0), 0.0, h)


def hyena_filter_taps(L, hp):
    pos = jnp.arange(L, dtype=F32)
    t = pos / (L - 1)
    bands = (HY_EMB - 1) // 2
    freqs = jnp.linspace(1e-4, bands - 1, bands, dtype=F32)
    ang = (2.0 * math.pi / L) * pos[:, None] * freqs[None, :]
    feats = jnp.concatenate([t[:, None], jnp.cos(ang), -jnp.sin(ang)], axis=-1)
    feats = jnp.pad(feats, ((0, 0), (0, 128 - HY_EMB)))
    feats = jnp.concatenate([feats, jnp.flip(feats, axis=0)], axis=0)
    w0 = jnp.pad(hp['hy_mlp_w0'], ((0, 128 - HY_EMB), (0, 0)))
    min_decay = math.log(1e-2) / 1.5
    max_decay = math.log(1e-2) / 0.3
    deltas = jnp.abs(jnp.linspace(min_decay, max_decay, HY_W, dtype=F32))
    dl = jnp.tile(deltas, 2).reshape(1, 2 * HY_W)
    w2 = hp['hy_mlp_w2'].reshape(HY_HID, 2, 2, HY_W).transpose(0, 2, 1, 3).reshape(HY_HID, 4 * HY_W)
    tl = min(L, 512)
    n_tiles = L // tl
    NS = 2 * HY_W
    full = lambda shape: pl.BlockSpec(shape, lambda i: (0, 0))
    return pl.pallas_call(
        functools.partial(_hyfilt_kernel, n_tiles=n_tiles),
        out_shape=(jax.ShapeDtypeStruct((2 * L, NS), F32), jax.ShapeDtypeStruct((1, 2 * NS), F32)),
        grid=(2 * n_tiles,),
        in_specs=[pl.BlockSpec((tl, 128), lambda i: (i, 0)), full((128, HY_HID)), full((1, HY_HID)),
                  full((1, HY_HID)), full((HY_HID, HY_HID)), full((1, HY_HID)), full((1, HY_HID)),
                  pl.BlockSpec((HY_HID, NS), lambda i: (0, i // n_tiles)), full((1, NS))],
        out_specs=(pl.BlockSpec((tl, NS), lambda i: (i, 0)), pl.BlockSpec((1, NS), lambda i: (0, i // n_tiles))),
        compiler_params=_cp(("arbitrary",)), name="hyena_filter",
    )(feats, w0, hp['hy_mlp_b0'].reshape(1, -1), hp['hy_freq0'].reshape(1, -1), hp['hy_mlp_w1'],
      hp['hy_mlp_b1'].reshape(1, -1), hp['hy_freq1'].reshape(1, -1), w2, dl)


def _cis(num, den):
    ang = (2.0 * math.pi / den) * (num % den).astype(F32)
    return jnp.cos(ang), -jnp.sin(ang)


def _fft_consts(NB, BS):
    N = NB * BS
    h = NB // 2
    k1 = jnp.arange(h, dtype=jnp.int32)
    j = jnp.arange(NB, dtype=jnp.int32)
    re, im = _cis(j[None, :] * (2 * k1[:, None] + 1), 2 * NB)
    f1 = jnp.concatenate([re, im], axis=0)
    neg = jnp.where(j >= h, -1.0, 1.0)[None, :]
    f1_data = f1[:, :h]
    f1_filt = f1 * neg
    f1_inv = (2.0 / N) * jnp.concatenate([re[:, :h].T, im[:, :h].T], axis=1)
    r = jnp.arange(BS, dtype=jnp.int32)
    k2 = jnp.arange(BS, dtype=jnp.int32)
    kk = 2 * k1[:, None, None] + 2 * NB * k2[None, :, None] + 1
    gre, gim = _cis(kk * r[None, None, :], 2 * N)
    gf = jnp.concatenate([jnp.concatenate([gre, -gim], axis=2), jnp.concatenate([gim, gre], axis=2)], axis=1)
    gret, gimt = gre.transpose(0, 2, 1), gim.transpose(0, 2, 1)
    gi = jnp.concatenate([jnp.concatenate([gret, gimt], axis=2), jnp.concatenate([-gimt, gret], axis=2)], axis=1)
    return (f1_data.astype(BF16), f1_filt.astype(BF16), f1_inv.astype(BF16), gf.astype(BF16), gi.astype(BF16))


FFT_PAD = 8


FFT_LW = 128


def _fft_fwd_kernel(ua_ref, ub_ref, f1_ref, g_ref, o_ref, t_sc, *, NB, BS, nj, kg):
    pitch = NB + FFT_PAD
    u_refs = (ua_ref, ub_ref)

    @pl.when(pl.program_id(2) == 0)
    def _():
        f1 = f1_ref[...]

        def body(r, carry):
            xr = jnp.concatenate([u[pl.ds(r, nj, stride=BS), :] for u in u_refs], axis=1).astype(BF16)
            res = _dot(f1, xr)
            for hh in range(2):
                t_sc[hh, pl.ds(pl.multiple_of(r * pitch, 8), NB), :] = res[:, hh * FFT_LW:(hh + 1) * FFT_LW]
            return carry

        lax.fori_loop(0, BS, body, 0, unroll=8)

    k0 = pl.program_id(2) * kg
    for i in range(kg):
        are = jnp.concatenate([t_sc[hh, pl.ds(k0 + i, BS, stride=pitch), :] for hh in range(2)], axis=1)
        aim = jnp.concatenate([t_sc[hh, pl.ds(k0 + i + NB // 2, BS, stride=pitch), :] for hh in range(2)], axis=1)
        a = jnp.concatenate([are, aim], axis=0).astype(BF16)
        o_ref[i] = _dot(g_ref[i], a)


def fft_fwd(u, col0, nbatch, nj, f1, gf, NB, BS, kg=8):
    h = NB // 2
    kg = min(kg, h)
    lw = FFT_LW
    ct = 2 * lw
    return pl.pallas_call(
        functools.partial(_fft_fwd_kernel, NB=NB, BS=BS, nj=nj, kg=kg),
        out_shape=jax.ShapeDtypeStruct((nbatch, h, 2 * BS, HY_W), F32),
        grid=(nbatch, HY_W // ct, h // kg),
        in_specs=[pl.BlockSpec((nj * BS, lw), lambda b, c, k: (b, col0 // lw + 2 * c), pipeline_mode=pl.Buffered(1)),
                  pl.BlockSpec((nj * BS, lw), lambda b, c, k: (b, col0 // lw + 2 * c + 1),
                               pipeline_mode=pl.Buffered(1)),
                  pl.BlockSpec((NB, nj), lambda b, c, k: (0, 0)),
                  pl.BlockSpec((kg, 2 * BS, 2 * BS), lambda b, c, k: (k, 0, 0))],
        out_specs=pl.BlockSpec((None, kg, 2 * BS, ct), lambda b, c, k: (b, k, 0, c)),
        scratch_shapes=[pltpu.VMEM((2, BS * (NB + FFT_PAD), lw), F32)],
        compiler_params=_cp(("parallel", "parallel", "arbitrary")), name="hyena_fft_fwd")(u, u, f1, gf)


def _cmul(u, h, half):
    ure, uim = u[:half], u[half:]
    hre, him = h[:half], h[half:]
    return jnp.concatenate([ure * hre - uim * him, ure * him + uim * hre], axis=0)


def _fft_inv_kernel(us_ref, hs_ref, gi_ref, f1i_ref, o_ref, t_sc, y_sc, *, NB, BS, kg):
    ks = pl.program_id(2)
    pitch = 2 * BS + FFT_PAD
    for i in range(kg):
        y = _cmul(us_ref[i], hs_ref[i], BS).astype(BF16)
        row = pl.multiple_of((ks * kg + i) * pitch, 8)
        res = _dot(gi_ref[i], y)
        for hh in range(2):
            t_sc[hh, pl.ds(row, 2 * BS), :] = res[:, hh * FFT_LW:(hh + 1) * FFT_LW]

    @pl.when(ks == pl.num_programs(2) - 1)
    def _():
        f1i = f1i_ref[...]

        def body(r, carry):
            bre = jnp.concatenate([t_sc[hh, pl.ds(r, NB // 2, stride=pitch), :] for hh in range(2)], axis=1)
            bim = jnp.concatenate([t_sc[hh, pl.ds(r + BS, NB // 2, stride=pitch), :] for hh in range(2)], axis=1)
            b = jnp.concatenate([bre, bim], axis=0).astype(BF16)
            res = _dot(f1i, b)
            for hh in range(2):
                y_sc[hh, pl.ds(r, NB // 2, stride=BS), :] = res[:, hh * FFT_LW:(hh + 1) * FFT_LW]
            return carry

        lax.fori_loop(0, BS, body, 0, unroll=8)
        o_ref[...] = jnp.concatenate([y_sc[0], y_sc[1]], axis=1)


def fft_inv(us, hs, gi, f1i, NB, BS, kg=8):
    nbatch, h = us.shape[0], NB // 2
    kg = min(kg, h)
    L = h * BS
    ct = 2 * FFT_LW
    return pl.pallas_call(
        functools.partial(_fft_inv_kernel, NB=NB, BS=BS, kg=kg),
        out_shape=jax.ShapeDtypeStruct((nbatch * L, HY_W), F32),
        grid=(nbatch, HY_W // ct, h // kg),
        in_specs=[pl.BlockSpec((None, kg, 2 * BS, ct), lambda b, c, k: (b, k, 0, c)),
                  pl.BlockSpec((None, kg, 2 * BS, ct), lambda b, c, k: (0, k, 0, c)),
                  pl.BlockSpec((kg, 2 * BS, 2 * BS), lambda b, c, k: (k, 0, 0)),
                  pl.BlockSpec((h, NB), lambda b, c, k: (0, 0))],
        out_specs=pl.BlockSpec((L, ct), lambda b, c, k: (b, c)),
        scratch_shapes=[pltpu.VMEM((2, h * (2 * BS + FFT_PAD), FFT_LW), F32), pltpu.VMEM((2, L, FFT_LW), F32)],
        compiler_params=_cp(("parallel", "parallel", "arbitrary")), name="hyena_fft_inv")(us, hs, gi, f1i)


def _dft_consts(L):
    N = 2 * L
    k = jnp.arange(L, dtype=jnp.int32)
    n = jnp.arange(N, dtype=jnp.int32)
    re, im = _cis(n[None, :] * (2 * k[:, None] + 1), 2 * N)
    f = jnp.concatenate([re, im], axis=0)
    neg = jnp.where(n >= L, -1.0, 1.0)[None, :]
    fi = (2.0 / N) * jnp.concatenate([re[:, :L].T, im[:, :L].T], axis=1)
    return f[:, :L].astype(BF16), (f * neg).astype(BF16), fi.astype(BF16)


def _cdft_kernel(f_ref, x_ref, o_ref):
    o_ref[...] = _dot(f_ref[...], x_ref[...].astype(BF16))


def dft_fwd(x, f, row0, col0, nbatch, ct=256):
    M, K = f.shape
    return pl.pallas_call(
        _cdft_kernel, out_shape=jax.ShapeDtypeStruct((nbatch, M, HY_W), F32),
        grid=(nbatch, HY_W // ct),
        in_specs=[pl.BlockSpec((M, K), lambda b, c: (0, 0)),
                  pl.BlockSpec((K, ct), lambda b, c: (row0 // K + b, col0 // ct + c))],
        out_specs=pl.BlockSpec((None, M, ct), lambda b, c: (b, 0, c)),
        compiler_params=_cp(("parallel", "parallel")), name="hyena_dft_fwd")(f, x)


def _cdft_inv_kernel(us_ref, hs_ref, fi_ref, o_ref):
    half = us_ref.shape[0] // 2
    o_ref[...] = _dot(fi_ref[...], _cmul(us_ref[...], hs_ref[...], half).astype(BF16))


def dft_inv(us, hs, fi, ct=256):
    nbatch, M2, _ = us.shape
    L = fi.shape[0]
    return pl.pallas_call(
        _cdft_inv_kernel, out_shape=jax.ShapeDtypeStruct((nbatch * L, HY_W), F32),
        grid=(nbatch, HY_W // ct),
        in_specs=[pl.BlockSpec((None, M2, ct), lambda b, c: (b, 0, c)),
                  pl.BlockSpec((None, M2, ct), lambda b, c: (0, 0, c)),
                  pl.BlockSpec((L, M2), lambda b, c: (0, 0))],
        out_specs=pl.BlockSpec((L, ct), lambda b, c: (b, c)),
        compiler_params=_cp(("parallel", "parallel")), name="hyena_dft_inv")(us, hs, fi)


def _hy_gate_kernel(g_ref, y_ref, u_ref, ss_ref, b_ref, o_ref):
    scale = lax.rsqrt(ss_ref[0:1, :] + ss_ref[1:2, :] + 1e-6)
    o_ref[...] = g_ref[...] * (y_ref[...] * scale + u_ref[...] * b_ref[...])


def _hy_gate2_kernel(g_ref, yl_ref, yc_ref, ul_ref, uc_ref, ssl_ref, ssc_ref, b_ref, o_ref, *, n_lat_t):
    is_lat = pl.program_id(0) < n_lat_t
    y = jnp.where(is_lat, yl_ref[...], yc_ref[...])
    uin = jnp.where(is_lat, ul_ref[...], uc_ref[...])
    ss = jnp.where(is_lat, ssl_ref[...], ssc_ref[...])
    scale = lax.rsqrt(ss[0:1, :] + ss[1:2, :] + 1e-6)
    o_ref[...] = g_ref[...] * (y * scale + uin * b_ref[...])


def hy_gate(gate, gcol, grow, y, uin, ucol, urow, ss, order, bias, tm=256):
    M = y.shape[0]
    return pl.pallas_call(
        _hy_gate_kernel, out_shape=jax.ShapeDtypeStruct((M, HY_W), F32), grid=(M // tm,),
        in_specs=[pl.BlockSpec((tm, HY_W), lambda i: (grow // tm + i, gcol // HY_W)),
                  pl.BlockSpec((tm, HY_W), lambda i: (i, 0)),
                  pl.BlockSpec((tm, HY_W), lambda i: (urow // tm + i, ucol // HY_W)),
                  pl.BlockSpec((None, 2, HY_W), lambda i: (order, 0, 0)),
                  pl.BlockSpec((None, 1, HY_W), lambda i: (order, 0, 0))],
        out_specs=pl.BlockSpec((tm, HY_W), lambda i: (i, 0)),
        compiler_params=_cp(("parallel",)), name="hyena_gate")(gate, y, uin, ss, bias)


def hyena(u, hp):
    C = HY_W
    bias = hp['hy_bias'].reshape(2, 1, C)
    NB = BS = int(round(math.sqrt(2 * SEQ)))
    f1d, f1f, f1i, gf, gi = _fft_consts(NB, BS)
    taps, ss = hyena_filter_taps(SEQ, hp)
    ss = ss.reshape(2, 2, C).transpose(1, 0, 2)
    conv_l = lambda zin, zcol, order: fft_inv(fft_fwd(zin, zcol, BATCH, NB // 2, f1d, gf, NB, BS),
                                              fft_fwd(taps, order * C, 1, NB, f1f, gf, NB, BS), gi, f1i, NB, BS)
    z1_lat = hy_gate(u, 0, 0, conv_l(u, 2 * C, 0), u, 2 * C, 0, ss, 0, bias)
    y2_lat = conv_l(z1_lat, 0, 1)
    fd, ff, fi = _dft_consts(CTX)
    taps_c, ss_c = hyena_filter_taps(CTX, hp)
    ss_c = ss_c.reshape(2, 2, C).transpose(1, 0, 2)
    conv_c = lambda zin, zrow, zcol, order: dft_inv(dft_fwd(zin, fd, zrow, zcol, BATCH),
                                                    dft_fwd(taps_c, ff, 0, order * C, 1), fi)
    z1_ctx = hy_gate(u, 0, N_LAT, conv_c(u, N_LAT, 2 * C, 0), u, 2 * C, N_LAT, ss_c, 0, bias)
    y2_ctx = conv_c(z1_ctx, 0, 0, 1)
    tm = 256
    n_lat_t = N_LAT // tm
    lat = lambda i: (jnp.minimum(i, n_lat_t - 1), 0)
    ctx = lambda i: (jnp.maximum(i - n_lat_t, 0), 0)
    return pl.pallas_call(
        functools.partial(_hy_gate2_kernel, n_lat_t=n_lat_t),
        out_shape=jax.ShapeDtypeStruct((N_ROWS, C), F32), grid=(N_ROWS // tm,),
        in_specs=[pl.BlockSpec((tm, C), lambda i: (i, 1)),
                  pl.BlockSpec((tm, C), lat), pl.BlockSpec((tm, C), ctx),
                  pl.BlockSpec((tm, C), lat), pl.BlockSpec((tm, C), ctx),
                  pl.BlockSpec((None, 2, C), lambda i: (1, 0, 0)), pl.BlockSpec((None, 2, C), lambda i: (1, 0, 0)),
                  pl.BlockSpec((None, 1, C), lambda i: (1, 0, 0))],
        out_specs=pl.BlockSpec((tm, C), lambda i: (i, 0)),
        compiler_params=_cp(("parallel",)), name="hyena_gate2",
    )(u, y2_lat, y2_ctx, z1_lat, z1_ctx, ss, ss_c, bias)


EV_SSD_IN = SSD_HEADS * SSD_P
EV_XBC = EV_SSD_IN + 2 * SSD_G * SSD_S
EV_PAD_N = 5760


def even_mixer(x, mods, g_pre, g_post, ep):
    o1 = EV_SSD_IN
    o2 = o1 + EV_XBC
    o3 = o2 + 2 * SSD_HEADS
    w = ep['w_in']
    n_in = w.shape[1]
    hw = 3 * HY_W
    w_perm = jnp.concatenate([w[:, o3:], w[:, :o2], w[:, o2:o3],
                              jnp.zeros((D, EV_PAD_N - n_in), F32)], axis=1).astype(BF16)
    p = norm_mod_matmul(x, g_pre, mods, w_perm, 0, 1, tn=1920, name="even_in_proj")
    xbc = dwconv3(p, hw + o1, EV_XBC, ep['ssd_conv_w'], ep['ssd_conv_b'], True, "ssd_conv")
    u = dwconv3(p, 0, hw, ep['hy_conv_w'], ep['hy_conv_b'], False, "hyena_conv")
    dt_raw = p[:, hw + o2:hw + o2 + 2 * SSD_HEADS]
    y2 = ssd_scan(xbc, dt_raw, ep['ssd_dt_bias'], ep['ssd_a_log'])
    s = ssd_output(y2, xbc, p, hw, ep['ssd_d'], ep['ssd_norm_w'])
    zh = hyena(u, ep)
    return outproj_residual(s, zh, ep['w_out'].astype(BF16), x, g_post, mods, 2, name="even_out_proj")


def _head_sum(x, e, et):
    return _dot2l(_dot2l(x, e), et)


def _rw_prep_kernel(r_ref, k_ref, v_ref, lo_ref, w0_ref, wup_ref, a0_ref, aup_ref, gup_ref, kk_ref, ka_ref,
                    rk_ref, e_ref, et_ref, lw_ref, kd_ref, be_ref, kap_ref, g_ref, bonus_ref):
    r, k, v = r_ref[...], k_ref[...], v_ref[...]
    lo = lo_ref[...]
    wc, ac, gc = lo[:, 0:64], lo[:, 64:128], lo[:, 128:384]
    e, et = e_ref[...], et_ref[...]
    kk = k * kk_ref[...]
    kap = kk * lax.rsqrt(_head_sum(kk * kk, e, et) + 1e-12)
    kap_ref[...] = kap
    g_ref[...] = _dot(_sigmoid(gc).astype(BF16), gup_ref[...].astype(BF16))
    kd_sum = jnp.zeros_like(k)
    for d in range(2):
        wlog = -_softplus(-(w0_ref[d:d + 1, :] + _dot3(jnp.tanh(wc), wup_ref[d]))) - 0.5
        lw_ref[d] = -jnp.exp(wlog)
        a = _sigmoid(a0_ref[d:d + 1, :] + _dot(ac.astype(BF16), aup_ref[d].astype(BF16)))
        kd = k * (1.0 + (a - 1.0) * ka_ref[...])
        kd_ref[d] = kd
        be_ref[d] = kap * a
        kd_sum = kd_sum + kd
    bonus_ref[...] = _head_sum(r * kd_sum * rk_ref[...], e, et) * v


def rwkv_prepare(code, lora, op, tm=256):
    W = RW_H * RW_N
    heads = jnp.arange(W, dtype=jnp.int32) // RW_N
    e = (heads[:, None] == jnp.arange(128, dtype=jnp.int32)[None, :]).astype(BF16)
    et = jnp.transpose(e)
    gup = jnp.pad(op['g_up'], ((0, 256 - op['g_up'].shape[0]), (0, 0)))
    row = lambda a: a.reshape(1, W)
    full2 = lambda shape: pl.BlockSpec(shape, lambda i: (0,) * len(shape))
    outs = pl.pallas_call(
        _rw_prep_kernel,
        out_shape=(jax.ShapeDtypeStruct((2, N_ROWS, W), F32), jax.ShapeDtypeStruct((2, N_ROWS, W), F32),
                   jax.ShapeDtypeStruct((2, N_ROWS, W), F32), jax.ShapeDtypeStruct((N_ROWS, W), F32),
                   jax.ShapeDtypeStruct((N_ROWS, W), F32), jax.ShapeDtypeStruct((N_ROWS, W), F32)),
        grid=(N_ROWS // tm,),
        in_specs=[pl.BlockSpec((tm, W), lambda i: (i, 0)), pl.BlockSpec((tm, W), lambda i: (i, 1)),
                  pl.BlockSpec((tm, W), lambda i: (i, 2)), pl.BlockSpec((tm, 384), lambda i: (i, 0)),
                  full2((2, W)), full2((2, 64, W)), full2((2, W)), full2((2, 64, W)), full2((256, W)),
                  full2((1, W)), full2((1, W)), full2((1, W)), full2((W, 128)), full2((128, W))],
        out_specs=(pl.BlockSpec((2, tm, W), lambda i: (0, i, 0)), pl.BlockSpec((2, tm, W), lambda i: (0, i, 0)),
                   pl.BlockSpec((2, tm, W), lambda i: (0, i, 0)), pl.BlockSpec((tm, W), lambda i: (i, 0)),
                   pl.BlockSpec((tm, W), lambda i: (i, 0)), pl.BlockSpec((tm, W), lambda i: (i, 0))),
        compiler_params=_cp(("parallel",)), name="rwkv_prepare",
    )(code, code, code, lora, op['w0'], op['w_up'], op['a0'], op['a_up'], gup, row(op['k_k']), row(op['k_a']),
      row(op['r_k']), e, et)
    return outs


def _rw_scan_kernel(r_ref, v_ref, lw_ref, kd_ref, be_ref, kap_ref, y_ref, st_ref):
    d = pl.program_id(0)
    c = pl.program_id(2)
    C = RW_CHUNK
    N = RW_N

    @pl.when(c == 0)
    def _():
        st_ref[...] = jnp.zeros_like(st_ref)

    isb = d == 1
    sgn = 1 - 2 * d
    ii = lax.broadcasted_iota(jnp.int32, (C, C), 0)
    jj = lax.broadcasted_iota(jnp.int32, (C, C), 1)
    dif = sgn * (ii - jj)
    incl = dif >= 0
    strict = dif > 0
    tri = incl.astype(BF16)
    eye = (ii == jj).astype(F32)
    blk = [(ii >> s) == (jj >> s) for s in range(3, C.bit_length() - 1)]
    masks = [blk[0]] + [blk[l] & ~blk[l - 1] for l in range(1, len(blk))] + [~blk[-1]]
    lw = lw_ref[...]
    cum = _dot2r(tri, lw)
    ec = jnp.exp(cum)
    en = jnp.exp(-cum)
    ea = jnp.exp(cum - lw)
    last = jnp.where(isb, cum[0:1, :], cum[C - 1:C, :])
    el = jnp.exp(last - cum)
    kap = kap_ref[...]
    r = r_ref[...]
    v = v_ref[...]
    a_t = -kap * ea
    r_t = r * ec
    b_t = be_ref[...] * en
    k_t = kd_ref[...] * en
    b_l = be_ref[...] * el
    k_l = kd_ref[...] * el
    pc = jnp.exp(last)
    H = range(RW_H)
    sl = [slice(h * N, (h + 1) * N) for h in H]
    bd = lambda a, b: _dot(a.astype(BF16), b.astype(BF16))
    tn = lambda a, b: lax.dot_general(a, b, (((0,), (0,)), ((), ())), preferred_element_type=F32)
    sc = [_dot_nt(jnp.concatenate([a_t[:, sl[h]], r_t[:, sl[h]]], axis=0).astype(BF16),
                  jnp.concatenate([b_t[:, sl[h]], k_t[:, sl[h]]], axis=0).astype(BF16)) for h in H]
    n_ab = [jnp.where(strict, sc[h][0:C, 0:C], 0.0) for h in H]
    a_ak = [jnp.where(strict, sc[h][0:C, C:2 * C], 0.0).astype(BF16) for h in H]
    m_rb = [jnp.where(incl, sc[h][C:2 * C, 0:C], 0.0).astype(BF16) for h in H]
    m_rk = [jnp.where(incl, sc[h][C:2 * C, C:2 * C], 0.0).astype(BF16) for h in H]
    vh = [v[:, sl[h]].astype(BF16) for h in H]
    d0 = [jnp.where(masks[0], n_ab[h], 0.0) for h in H]
    d2 = [bd(d0[h], d0[h]) for h in H]
    d4 = [bd(d2[h], d2[h]) for h in H]
    t = [bd(eye + d0[h], eye + d2[h]) for h in H]
    t = [bd(t[h], eye + d4[h]) for h in H]
    for m in masks[1:]:
        et = [bd(jnp.where(m, n_ab[h], 0.0), t[h]) for h in H]
        t = [t[h] + bd(t[h], et[h]) for h in H]
    amv = [_dot(jnp.concatenate([a_ak[h], m_rk[h]], axis=0), vh[h]) for h in H]
    wub = [bd(t[h], jnp.concatenate([a_t[:, sl[h]], amv[h][0:C]], axis=1)).astype(BF16) for h in H]
    kv = [tn(k_l[:, sl[h]].astype(BF16), vh[h]) for h in H]
    qy = [_dot(m_rb[h], wub[h]) + jnp.concatenate([r_t[:, sl[h]], amv[h][C:2 * C]], axis=1) for h in H]
    pp = [tn(b_l[:, sl[h]].astype(BF16), wub[h]) + jnp.concatenate([eye[0:N, 0:N] * pc[:, sl[h]], kv[h]], axis=1)
          for h in H]
    h_old = [st_ref[h] for h in H]
    ys = [bd(qy[h][:, 0:N], h_old[h]) + qy[h][:, N:2 * N] for h in H]
    for h in H:
        st_ref[h] = _dot3(pp[h][:, 0:N], h_old[h]) + pp[h][:, N:2 * N]
    y_ref[...] = jnp.concatenate(ys, axis=1)


def _rw_rowblock(d, b, c):
    n_ctx = CTX // RW_CHUNK
    n_lat = SEQ // RW_CHUNK
    cc = jnp.where(d == 0, c, n_ctx - 1 - c)
    lc = jnp.where(d == 0, c - n_ctx, n_ctx + n_lat - 1 - c)
    return jnp.where(c < n_ctx, N_LAT // RW_CHUNK + b * n_ctx + cc, b * n_lat + lc)


def rwkv_scan(code, lw, kd, be, kap):
    W = RW_H * RW_N
    nch = (CTX + SEQ) // RW_CHUNK
    rb = lambda d, b, c: _rw_rowblock(d, b, c)
    return pl.pallas_call(
        _rw_scan_kernel,
        out_shape=jax.ShapeDtypeStruct((2, N_ROWS, W), F32),
        grid=(2, BATCH, nch),
        in_specs=[pl.BlockSpec((RW_CHUNK, W), lambda d, b, c: (rb(d, b, c), 0)),
                  pl.BlockSpec((RW_CHUNK, W), lambda d, b, c: (rb(d, b, c), 2)),
                  pl.BlockSpec((None, RW_CHUNK, W), lambda d, b, c: (d, rb(d, b, c), 0)),
                  pl.BlockSpec((None, RW_CHUNK, W), lambda d, b, c: (d, rb(d, b, c), 0)),
                  pl.BlockSpec((None, RW_CHUNK, W), lambda d, b, c: (d, rb(d, b, c), 0)),
                  pl.BlockSpec((RW_CHUNK, W), lambda d, b, c: (rb(d, b, c), 0))],
        out_specs=pl.BlockSpec((None, RW_CHUNK, W), lambda d, b, c: (d, rb(d, b, c), 0)),
        scratch_shapes=[pltpu.VMEM((RW_H, RW_N, RW_N), F32)],
        compiler_params=_cp(("parallel", "parallel", "arbitrary")), name="rwkv_scan",
    )(code, code, lw, kd, be, kap)


def _rw_out_kernel(yf_ref, yb_ref, bonus_ref, g_ref, lnw_ref, lnb_ref, e_ref, et_ref, o_ref):
    e, et = e_ref[...], et_ref[...]
    y = yf_ref[...] + yb_ref[...]
    mean = _head_sum(y, e, et) * (1.0 / RW_N)
    yc = y - mean
    var = _head_sum(yc * yc, e, et) * (1.0 / RW_N)
    yn = yc * lax.rsqrt(var + RW_GN_EPS) * lnw_ref[...] + lnb_ref[...]
    o_ref[...] = (yn + bonus_ref[...]) * g_ref[...]


def rwkv_output(y2, bonus, g, op, tm=256):
    W = RW_H * RW_N
    heads = jnp.arange(W, dtype=jnp.int32) // RW_N
    e = (heads[:, None] == jnp.arange(128, dtype=jnp.int32)[None, :]).astype(BF16)
    et = jnp.transpose(e)
    M = N_LAT
    return pl.pallas_call(
        _rw_out_kernel, out_shape=jax.ShapeDtypeStruct((M, W), F32), grid=(M // tm,),
        in_specs=[pl.BlockSpec((None, tm, W), lambda i: (0, i, 0)), pl.BlockSpec((None, tm, W), lambda i: (1, i, 0)),
                  pl.BlockSpec((tm, W), lambda i: (i, 0)), pl.BlockSpec((tm, W), lambda i: (i, 0)),
                  pl.BlockSpec((1, W), lambda i: (0, 0)), pl.BlockSpec((1, W), lambda i: (0, 0)),
                  pl.BlockSpec((W, 128), lambda i: (0, 0)), pl.BlockSpec((128, W), lambda i: (0, 0))],
        out_specs=pl.BlockSpec((tm, W), lambda i: (i, 0)),
        compiler_params=_cp(("parallel",)), name="rwkv_output",
    )(y2, y2, bonus, g, op['ln_w'].reshape(1, W), op['ln_b'].reshape(1, W), e, et)


AT_Q = RW_H * AT_HD
AT_KW = AT_KV * AT_HD
AT_TQ = 512
AT_TK = 768


def _rope_tables(tm):
    half = AT_HD // 2
    inv = 10000.0 ** (-jnp.arange(0, half, 2, dtype=F32) / half)
    pos = jnp.arange(SEQ, dtype=jnp.int32)
    row = (pos // GRID_W).astype(F32)[:, None] * inv
    col = (pos % GRID_W).astype(F32)[:, None] * inv
    cos_h = jnp.concatenate([jnp.cos(row), jnp.cos(row), jnp.cos(col), jnp.cos(col)], axis=1)
    sin_h = jnp.concatenate([-jnp.sin(row), jnp.sin(row), -jnp.sin(col), jnp.sin(col)], axis=1)
    cos_t = jnp.concatenate([jnp.tile(cos_h, (1, 2)), jnp.ones((tm, 128), F32)], axis=0)
    sin_t = jnp.concatenate([jnp.tile(sin_h, (1, 2)), jnp.zeros((tm, 128), F32)], axis=0)
    return cos_t, sin_t


def _rot_partner(x):
    q = AT_HD // 4
    w = x.shape[1]
    lane = lax.broadcasted_iota(jnp.int32, x.shape, 1)
    return jnp.where((lane % (2 * q)) < q, pltpu.roll(x, w - q, 1), pltpu.roll(x, q, 1))


def _at_prep_kernel(q_ref, k_ref, v_ref, cos_ref, sin_ref, qn_ref, kn_ref, e_ref, et_ref, qo_ref, ko_ref, vo_ref):
    e, et = e_ref[...], et_ref[...]
    cos2, sin2 = cos_ref[...], sin_ref[...]

    def norm_rope(x, gain, nrep):
        ms = _head_sum(x * x, e[:x.shape[1]], et[:, :x.shape[1]]) * (1.0 / AT_HD)
        xn = x * lax.rsqrt(ms + EPS) * gain
        cos = jnp.tile(cos2, (1, nrep))
        sin = jnp.tile(sin2, (1, nrep))
        return xn * cos + _rot_partner(xn) * sin

    qn = norm_rope(q_ref[...], qn_ref[...], AT_Q // 128) * (AT_HD ** -0.5 * math.log2(math.e))
    qo_ref[...] = jnp.transpose(qn).astype(BF16)
    ko_ref[...] = norm_rope(k_ref[...], kn_ref[...], AT_KW // 128).astype(BF16)
    vo_ref[...] = jnp.transpose(v_ref[...]).astype(BF16)


def attention_prepare(p, q_norm, k_norm, tm=256):
    cos_t, sin_t = _rope_tables(tm)
    heads = jnp.arange(AT_Q, dtype=jnp.int32) // AT_HD
    e = (heads[:, None] == jnp.arange(128, dtype=jnp.int32)[None, :]).astype(BF16)
    et = jnp.transpose(e)
    tab = lambda i: jnp.where(i * tm < N_LAT, ((i * tm) % SEQ) // tm, SEQ // tm)
    n_lat_t, n_seq_t, n_ctx_t = N_LAT // tm, SEQ // tm, CTX // tm
    kvb = lambda i: jnp.where(i < n_lat_t, (i // n_seq_t) * (n_seq_t + n_ctx_t) + n_ctx_t + i % n_seq_t,
                              ((i - n_lat_t) // n_ctx_t) * (n_seq_t + n_ctx_t) + (i - n_lat_t) % n_ctx_t)
    qcol = (3 * RW_H * RW_N) // AT_Q
    kcol = (3 * RW_H * RW_N + AT_Q) // AT_KW
    return pl.pallas_call(
        _at_prep_kernel,
        out_shape=(jax.ShapeDtypeStruct((AT_Q, N_ROWS), BF16), jax.ShapeDtypeStruct((N_ROWS, AT_KW), BF16),
                   jax.ShapeDtypeStruct((AT_KW, N_ROWS), BF16)),
        grid=(N_ROWS // tm,),
        in_specs=[pl.BlockSpec((tm, AT_Q), lambda i: (i, qcol)),
                  pl.BlockSpec((tm, AT_KW), lambda i: (i, kcol)),
                  pl.BlockSpec((tm, AT_KW), lambda i: (i, kcol + 1)),
                  pl.BlockSpec((tm, 128), lambda i: (tab(i), 0)),
                  pl.BlockSpec((tm, 128), lambda i: (tab(i), 0)),
                  pl.BlockSpec((1, AT_Q), lambda i: (0, 0)),
                  pl.BlockSpec((1, AT_KW), lambda i: (0, 0)),
                  pl.BlockSpec((AT_Q, 128), lambda i: (0, 0)),
                  pl.BlockSpec((128, AT_Q), lambda i: (0, 0))],
        out_specs=(pl.BlockSpec((AT_Q, tm), lambda i: (0, i)), pl.BlockSpec((tm, AT_KW), lambda i: (kvb(i), 0)),
                   pl.BlockSpec((AT_KW, tm), lambda i: (0, kvb(i)))),
        compiler_params=_cp(("parallel",)), name="attn_prepare",
    )(p, p, p, cos_t, sin_t, jnp.tile(q_norm, AT_Q // AT_HD).reshape(1, AT_Q),
      jnp.tile(k_norm, AT_KV).reshape(1, AT_KW), e, et)


AT_REBASE = 64.0


def _flash_t_kernel(qt_ref, k_ref, vt_ref, o_ref, m_sc, l_sc, acc_sc, p_sc):
    ki = pl.program_id(2)
    nq = AT_Q // AT_HD
    gq = nq // AT_KV

    @pl.when(ki == 0)
    def _():
        m_sc[...] = jnp.full_like(m_sc, -1e30)
        l_sc[...] = jnp.zeros_like(l_sc)
        acc_sc[...] = jnp.zeros_like(acc_sc)

    def k_group(g):
        return k_ref[:, g * AT_HD:(g + 1) * AT_HD]

    def v_group(g):
        return jnp.concatenate([vt_ref[g * AT_HD:(g + 1) * AT_HD, :],
                                jnp.ones((16, vt_ref.shape[1]), BF16)], axis=0)

    def scores(g):
        kg = k_group(g)
        return [_dot(kg, qt_ref[h * AT_HD:(h + 1) * AT_HD, :]) for h in range(g * gq, (g + 1) * gq)]

    gap = None
    for g in range(AT_KV):
        st = scores(g)
        m_cur = [m_sc[h] for h in range(g * gq, (g + 1) * gq)]
        for i in range(gq):
            over_i = jnp.max(st[i], axis=0, keepdims=True) - m_cur[i]
            gap = over_i if gap is None else jnp.maximum(gap, over_i)
            p_sc[g * gq + i] = jnp.exp2(st[i] - m_cur[i]).astype(BF16)
    rebase = jnp.max(gap) > AT_REBASE

    @pl.when(jnp.logical_not(rebase))
    def _():
        for g in range(AT_KV):
            vtg1 = v_group(g)
            hs = range(g * gq, (g + 1) * gq)
            pv = [_dot(vtg1, p_sc[h]) for h in hs]
            for i, h in enumerate(hs):
                l_sc[h] = l_sc[h] + pv[i][AT_HD:AT_HD + 1, :]
                rows = pl.ds(h * AT_HD, AT_HD)
                acc_sc[rows, :] = acc_sc[rows, :] + pv[i][0:AT_HD, :]

    @pl.when(rebase)
    def _():
        for g in range(AT_KV):
            vtg1 = v_group(g)
            hs = range(g * gq, (g + 1) * gq)
            st = scores(g)
            m_old = [m_sc[h] for h in hs]
            m_new = [jnp.maximum(m_old[i], jnp.max(st[i], axis=0, keepdims=True)) for i in range(gq)]
            alpha = [jnp.exp2(m_old[i] - m_new[i]) for i in range(gq)]
            pt = [jnp.exp2(st[i] - m_new[i]).astype(BF16) for i in range(gq)]
            pv = [_dot(vtg1, pt[i]) for i in range(gq)]
            for i, h in enumerate(hs):
                l_sc[h] = alpha[i] * l_sc[h] + pv[i][AT_HD:AT_HD + 1, :]
                m_sc[h] = m_new[i]
                rows = pl.ds(h * AT_HD, AT_HD)
                acc_sc[rows, :] = alpha[i] * acc_sc[rows, :] + pv[i][0:AT_HD, :]

    @pl.when(ki == pl.num_programs(2) - 1)
    def _():
        inv = jnp.concatenate([jnp.broadcast_to(1.0 / l_sc[h], (AT_HD, l_sc.shape[2])) for h in range(nq)], axis=0)
        o_ref[...] = jnp.transpose(acc_sc[...] * inv)


def flash_attention_t(qt, k, vt):
    nq = AT_Q // AT_HD
    nk = (CTX + SEQ) // AT_TK
    kv_rb = lambda b, ki: b * nk + ki
    return pl.pallas_call(
        _flash_t_kernel,
        out_shape=jax.ShapeDtypeStruct((N_LAT, AT_Q), F32),
        grid=(BATCH, SEQ // AT_TQ, nk),
        in_specs=[pl.BlockSpec((AT_Q, AT_TQ), lambda b, qi, ki: (0, b * (SEQ // AT_TQ) + qi)),
                  pl.BlockSpec((AT_TK, AT_KW), lambda b, qi, ki: (kv_rb(b, ki), 0)),
                  pl.BlockSpec((AT_KW, AT_TK), lambda b, qi, ki: (0, kv_rb(b, ki)))],
        out_specs=pl.BlockSpec((AT_TQ, AT_Q), lambda b, qi, ki: (b * (SEQ // AT_TQ) + qi, 0)),
        scratch_shapes=[pltpu.VMEM((nq, 1, AT_TQ), F32), pltpu.VMEM((nq, 1, AT_TQ), F32),
                        pltpu.VMEM((AT_Q, AT_TQ), F32), pltpu.VMEM((nq, AT_TK, AT_TQ), BF16)],
        compiler_params=_cp(("parallel", "parallel", "arbitrary")), name="flash_attention")(qt, k, vt)


OD_PAD_N = 4992


def odd_mixer(x, mods, g_pre, g_post, op):
    W = RW_H * RW_N
    w = op['w_in']
    c3 = 3 * W
    code_n = c3 + 64 + 64 + 160
    w_perm = jnp.concatenate([w[:, :c3], w[:, code_n:], w[:, c3:code_n],
                              jnp.zeros((D, OD_PAD_N - w.shape[1]), F32)], axis=1).astype(BF16)
    p = norm_mod_matmul(x, g_pre, mods, w_perm, 0, 1, tn=1664, name="odd_in_proj")
    mu = op['mu']
    taps = lambda m: jnp.stack([0.5 * m, 1.0 - m, 0.5 * m], axis=1)
    code = dwconv3(p, 0, c3, taps(mu[:c3]), jnp.zeros((c3,), F32), False, "rwkv_shift")
    lo_col = c3 + AT_Q + 2 * AT_KW
    mu_lo = jnp.pad(mu[c3:], (0, 384 - (code_n - c3)))
    lora = dwconv3(p, lo_col, 384, taps(mu_lo), jnp.zeros((384,), F32), False, "rwkv_shift_lora", cb=128)
    lw, kd, be, kap, g, bonus = rwkv_prepare(code, lora, op)
    y2 = rwkv_scan(code, lw, kd, be, kap)
    o_l = rwkv_output(y2, bonus, g, op)
    q, k, v = attention_prepare(p, op['q_norm'], op['k_norm'])
    a_l = flash_attention_t(q, k, v)
    return outproj_residual(o_l, a_l, op['w_out'].astype(BF16), x, g_post, mods, 2, name="odd_out_proj")


MOE_T = 256
MOE_CAP = 64
MOE_EPS = 4


def _router2_kernel(x_ref, g_ref, mod_ref, rw_ref, rb_ref, s1_ref, s3_ref, s2_ref, t_ref, wt_ref, cnt_ref, sh_ref):
    x = x_ref[...]
    ms = jnp.mean(x * x, axis=-1, keepdims=True)
    t = x * lax.rsqrt(ms + EPS) * g_ref[...] * (1.0 + mod_ref[4:5, :]) + mod_ref[3:4, :]
    tb = t.astype(BF16)
    t_ref[...] = tb
    sh_ref[...] = _dot((_silu(_dot(tb, s1_ref[...])) * _dot(tb, s3_ref[...])).astype(BF16), s2_ref[...])
    th, tl = _split(t)
    wh, wl = _split(rw_ref[...])
    lg = _dot_nt(wh, th) + (_dot_nt(wh, tl) + _dot_nt(wl, th))
    sc = _sigmoid(lg)
    sel = sc + rb_ref[...]
    tm = sel.shape[1]
    gsz = N_EXP // N_GRP
    ninf = -jnp.inf
    sel3 = sel.reshape(N_GRP, gsz, tm)
    i3 = lax.broadcasted_iota(jnp.int32, sel3.shape, 1)
    m1 = jnp.max(sel3, axis=1, keepdims=True)
    first = jnp.min(jnp.where(sel3 == m1, i3, gsz), axis=1, keepdims=True)
    m2 = jnp.max(jnp.where(i3 == first, ninf, sel3), axis=1, keepdims=True)
    grp = (m1 + m2).reshape(N_GRP, tm)
    gi = lax.broadcasted_iota(jnp.int32, grp.shape, 0)
    gmask = jnp.zeros(grp.shape, F32)
    for _ in range(TOPK_GRP):
        m = jnp.max(grp, axis=0, keepdims=True)
        pick = jnp.min(jnp.where(grp == m, gi, N_GRP), axis=0, keepdims=True)
        hit = gi == pick
        gmask = jnp.where(hit, 1.0, gmask)
        grp = jnp.where(hit, ninf, grp)
    emask = jnp.broadcast_to(gmask.reshape(N_GRP, 1, tm), (N_GRP, gsz, tm)).reshape(N_EXP, tm)
    msel = jnp.where(emask > 0.5, sel, ninf)
    ei = lax.broadcasted_iota(jnp.int32, msel.shape, 0)
    chosen = jnp.zeros(msel.shape, F32)
    for _ in range(TOP_K):
        m = jnp.max(msel, axis=0, keepdims=True)
        pick = jnp.min(jnp.where(msel == m, ei, N_EXP), axis=0, keepdims=True)
        hit = ei == pick
        chosen = jnp.where(hit, 1.0, chosen)
        msel = jnp.where(hit, ninf, msel)
    w = chosen * sc
    wt = w / jnp.sum(w, axis=0, keepdims=True) * ROUTED_SCALE
    wt_ref[...] = wt
    cnt_ref[...] = jnp.sum((wt > 0.0).astype(F32), axis=1, keepdims=True).astype(jnp.int32)


def moe_router2(x, M, g, mods, mp):
    tm = MOE_T
    full = lambda shape: pl.BlockSpec(shape, lambda i: (0,) * len(shape))
    return pl.pallas_call(
        _router2_kernel,
        out_shape=(jax.ShapeDtypeStruct((M, D), BF16), jax.ShapeDtypeStruct((M // tm, N_EXP, tm), F32),
                   jax.ShapeDtypeStruct((M // tm, N_EXP, 1), jnp.int32), jax.ShapeDtypeStruct((M, D), F32)),
        grid=(M // tm,),
        in_specs=[pl.BlockSpec((tm, D), lambda i: (i, 0)), full((1, D)),
                  pl.BlockSpec((None, 6, D), lambda i: (_seq_of_rowblock(i, tm), 0, 0)),
                  full((N_EXP, D)), full((N_EXP, 1)), full((D, EXP_FF)), full((D, EXP_FF)), full((EXP_FF, D))],
        out_specs=(pl.BlockSpec((tm, D), lambda i: (i, 0)), pl.BlockSpec((None, N_EXP, tm), lambda i: (i, 0, 0)),
                   pl.BlockSpec((None, N_EXP, 1), lambda i: (i, 0, 0)), pl.BlockSpec((tm, D), lambda i: (i, 0))),
        compiler_params=_cp(("parallel",)), name="moe_router",
    )(x, g.reshape(1, D), mods, jnp.transpose(mp['router_w']), mp['router_bias'].reshape(N_EXP, 1),
      mp['s1'].astype(BF16), mp['s3'].astype(BF16), mp['s2'].astype(BF16))


def _moe3_kernel(cnt_ref, t_ref, wt_ref, w1_ref, w3_ref, w2_ref, o_ref, rank_sc, *, nsub):
    i = pl.program_id(0)
    eb = pl.program_id(1)
    T, CAP, EPS = MOE_T, MOE_CAP, MOE_EPS

    @pl.when(eb == 0)
    def _():
        before = (lax.broadcasted_iota(jnp.int32, (T, T), 0) < lax.broadcasted_iota(jnp.int32, (T, T), 1))
        before = before.astype(BF16)
        for s in range(nsub):
            rank_sc[s] = _dot((wt_ref[s] > 0.0).astype(BF16), before)
        o_ref[...] = jnp.zeros_like(o_ref)

    slot = lax.broadcasted_iota(jnp.int32, (CAP, T), 0).astype(F32)

    def one_hot(s, e, first_slot):
        w_row = wt_ref[s, pl.ds(e, 1), :]
        r_row = rank_sc[s, pl.ds(e, 1), :]
        hit = ((r_row - first_slot) == slot) & (w_row > 0.0)
        w_slot = jnp.sum(jnp.where(hit, w_row, 0.0), axis=1, keepdims=True)
        return hit.astype(F32).astype(BF16), w_slot

    def swiglu(xg, j):
        h = _silu(_dot(xg, w1_ref[j])) * _dot(xg, w3_ref[j])
        return _dot(h.astype(BF16), w2_ref[j])

    hot = [[one_hot(s, eb * EPS + j, 0.0) for j in range(EPS)] for s in range(nsub)]
    pb = [jnp.concatenate([hot[s][j][0] for j in range(EPS)], axis=0) for s in range(nsub)]
    xg = [_dot(pb[s], t_ref[s * T:(s + 1) * T, :]).astype(BF16) for s in range(nsub)]
    y = [swiglu(jnp.concatenate([xg[s][j * CAP:(j + 1) * CAP] for s in range(nsub)], axis=0), j)
         for j in range(EPS)]
    for s in range(nsub):
        yw = jnp.concatenate([y[j][s * CAP:(s + 1) * CAP] * hot[s][j][1] for j in range(EPS)], axis=0)
        o_ref[s * T:(s + 1) * T, :] += lax.dot_general(pb[s], yw.astype(BF16), (((0,), (0,)), ((), ())),
                                                       preferred_element_type=F32)

    def pair(idx, carry):
        s = idx // EPS
        j = idx % EPS
        e = eb * EPS + j
        n_tok = cnt_ref[(i * nsub + s) * N_EXP + e]
        rows = pl.ds(pl.multiple_of(s * T, T), T)

        def chunk(ci, c2):
            p1, w_slot = one_hot(s, e, (ci * CAP).astype(F32))
            yw = (swiglu(_dot(p1, t_ref[rows, :]).astype(BF16), j) * w_slot).astype(BF16)
            o_ref[rows, :] += lax.dot_general(p1, yw, (((0,), (0,)), ((), ())), preferred_element_type=F32)
            return c2

        lax.fori_loop(1, (n_tok + CAP - 1) // CAP, chunk, 0)
        return carry

    lax.fori_loop(0, nsub * EPS, pair, 0)


def _moe_out_kernel(r_ref, sh_ref, x_ref, g_ref, mod_ref, o_ref):
    f = r_ref[...] + sh_ref[...]
    ms = jnp.mean(f * f, axis=-1, keepdims=True)
    o_ref[...] = x_ref[...] + mod_ref[5:6, :] * (f * lax.rsqrt(ms + EPS) * g_ref[...])


def moe_layer3(x, M, g_pre, g_post, mods, mp, nsub):
    t, wt, cnt, sh = moe_router2(x, M, g_pre, mods, mp)
    T = MOE_T
    TS = nsub * T
    grid_spec = pltpu.PrefetchScalarGridSpec(
        num_scalar_prefetch=1, grid=(M // TS, N_EXP // MOE_EPS),
        in_specs=[pl.BlockSpec((TS, D), lambda i, e, c: (i, 0)),
                  pl.BlockSpec((nsub, N_EXP, T), lambda i, e, c: (i, 0, 0)),
                  pl.BlockSpec((MOE_EPS, D, EXP_FF), lambda i, e, c: (e, 0, 0)),
                  pl.BlockSpec((MOE_EPS, D, EXP_FF), lambda i, e, c: (e, 0, 0)),
                  pl.BlockSpec((MOE_EPS, EXP_FF, D), lambda i, e, c: (e, 0, 0))],
        out_specs=pl.BlockSpec((TS, D), lambda i, e, c: (i, 0)),
        scratch_shapes=[pltpu.VMEM((nsub, N_EXP, T), F32)])
    routed = pl.pallas_call(
        functools.partial(_moe3_kernel, nsub=nsub), out_shape=jax.ShapeDtypeStruct((M, D), F32),
        grid_spec=grid_spec, compiler_params=_cp(("parallel", "arbitrary")), name="moe_experts",
    )(cnt.reshape(-1), t, wt, mp['w1'].astype(BF16), mp['w3'].astype(BF16), mp['w2'].astype(BF16))
    tm = 512
    return pl.pallas_call(
        _moe_out_kernel, out_shape=jax.ShapeDtypeStruct((M, D), F32), grid=(M // tm,),
        in_specs=[pl.BlockSpec((tm, D), lambda i: (i, 0)), pl.BlockSpec((tm, D), lambda i: (i, 0)),
                  pl.BlockSpec((tm, D), lambda i: (i, 0)), pl.BlockSpec((1, D), lambda i: (0, 0)),
                  pl.BlockSpec((None, 6, D), lambda i: (_seq_of_rowblock(i, tm), 0, 0))],
        out_specs=pl.BlockSpec((tm, D), lambda i: (i, 0)),
        compiler_params=_cp(("parallel",)), name="moe_output")(routed, sh, x, g_post.reshape(1, D), mods)


def kernel(x, c, ctx, c_ctx, mod_w, mod_b, norm_mix_pre, norm_mix_post, norm_ffn_pre, norm_ffn_post, router_w, router_bias, expert_w1, expert_w3, expert_w2, shared_w1, shared_w3, shared_w2, ev_w_in, ev_w_out, ssd_conv_w, ssd_conv_b, ssd_dt_bias, ssd_a_log, ssd_d, ssd_norm_w, hy_conv_w, hy_conv_b, hy_mlp_w0, hy_mlp_b0, hy_freq0, hy_mlp_w1, hy_mlp_b1, hy_freq1, hy_mlp_w2, hy_bias, od_w_in, od_w_out, rw_mu, rw_w0, rw_w_up, rw_a0, rw_a_up, rw_g_up, rw_k_k, rw_k_a, rw_r_k, rw_ln_w, rw_ln_b, at_q_norm, at_k_norm):
    xs = jnp.concatenate([x.reshape(N_LAT, D), ctx.reshape(BATCH * CTX, D)], axis=0)
    cvecs = jnp.zeros((8, D), F32).at[0:BATCH].set(c).at[BATCH].set(c_ctx)
    assert mod_w.shape[0] == 2, "one even (SSD | Hyena) layer followed by one odd (RWKV | attention) layer"

    def moe_params(i):
        return dict(router_w=router_w[i], router_bias=router_bias[i], w1=expert_w1[i], w3=expert_w3[i],
                    w2=expert_w2[i], s1=shared_w1[i], s3=shared_w3[i], s2=shared_w2[i])

    mods = modulation(cvecs, mod_w[0], mod_b[0])[:BATCH + 1].reshape(BATCH + 1, 6, D)
    ep = dict(w_in=ev_w_in[0], w_out=ev_w_out[0], ssd_conv_w=ssd_conv_w[0], ssd_conv_b=ssd_conv_b[0],
              ssd_dt_bias=ssd_dt_bias[0], ssd_a_log=ssd_a_log[0], ssd_d=ssd_d[0], ssd_norm_w=ssd_norm_w[0],
              hy_conv_w=hy_conv_w[0], hy_conv_b=hy_conv_b[0], hy_mlp_w0=hy_mlp_w0[0], hy_mlp_b0=hy_mlp_b0[0],
              hy_freq0=hy_freq0[0], hy_mlp_w1=hy_mlp_w1[0], hy_mlp_b1=hy_mlp_b1[0], hy_freq1=hy_freq1[0],
              hy_mlp_w2=hy_mlp_w2[0], hy_bias=hy_bias[0])
    xs = even_mixer(xs, mods, norm_mix_pre[0], norm_mix_post[0], ep)
    xs = moe_layer3(xs, N_ROWS, norm_ffn_pre[0], norm_ffn_post[0], mods, moe_params(0), 6)
    mods = modulation(cvecs, mod_w[1], mod_b[1])[:BATCH + 1].reshape(BATCH + 1, 6, D)
    op = dict(w_in=od_w_in[0], w_out=od_w_out[0], mu=rw_mu[0], w0=rw_w0[0], w_up=rw_w_up[0], a0=rw_a0[0],
              a_up=rw_a_up[0], g_up=rw_g_up[0], k_k=rw_k_k[0], k_a=rw_k_a[0], r_k=rw_r_k[0], ln_w=rw_ln_w[0],
              ln_b=rw_ln_b[0], q_norm=at_q_norm[0], k_norm=at_k_norm[0])
    xl = odd_mixer(xs, mods, norm_mix_pre[1], norm_mix_post[1], op)
    xl = moe_layer3(xl, N_LAT, norm_ffn_pre[1], norm_ffn_post[1], mods, moe_params(1), 8)
    return xl.reshape(BATCH, SEQ, D)
```

```python
import functools
import math

import jax
import jax.numpy as jnp
from jax import lax
from jax.experimental import pallas as pl
from jax.experimental.pallas import tpu as pltpu

F32 = jnp.float32
BF16 = jnp.bfloat16

D = 1024
BATCH = 2
SEQ = 8192
CTX = 256
N_LAT = BATCH * SEQ
N_ROWS = N_LAT + BATCH * CTX
EPS = 1e-6
GRID_W = 64

SSD_HEADS = 16
SSD_P = 64
SSD_G = 2
SSD_S = 128
SSD_Q = 128
HY_W = 1024
HY_EMB = 33
HY_HID = 64

RW_H = 16
RW_N = 64
RW_CHUNK = 128
RW_GN_EPS = 64e-5

AT_KV = 4
AT_HD = 64

N_EXP = 64
TOP_K = 8
N_GRP = 8
TOPK_GRP = 4
EXP_FF = 256
ROUTED_SCALE = 2.5

VMEM_LIMIT = 56 * 1024 * 1024


def _cp(sem, vmem=None):
    return pltpu.CompilerParams(dimension_semantics=sem, vmem_limit_bytes=vmem or VMEM_LIMIT)


def _dot(a, b):
    return jnp.dot(a, b, preferred_element_type=F32)


def _dot_nt(a, b):
    return lax.dot_general(a, b, (((1,), (1,)), ((), ())), preferred_element_type=F32)


def _split(x):
    hi = x.astype(BF16)
    lo = (x - hi.astype(F32)).astype(BF16)
    return hi, lo


def _dot3(a, b):
    ah, al = _split(a)
    bh, bl = _split(b)
    return _dot(ah, bh) + (_dot(ah, bl) + _dot(al, bh))


def _dot2l(a, b):
    ah, al = _split(a)
    return _dot(ah, b) + _dot(al, b)


def _dot2r(a, b):
    bh, bl = _split(b)
    return _dot(a, bh) + _dot(a, bl)


def _silu(x):
    return x * (1.0 / (1.0 + jnp.exp(-x)))


def _sigmoid(x):
    return 1.0 / (1.0 + jnp.exp(-x))


def _softplus(x):
    return jnp.maximum(x, 0.0) + jnp.log(1.0 + jnp.exp(-jnp.abs(x)))


def _seq_of_rowblock(i, tm):
    return jnp.minimum((i * tm) // SEQ, 2)


def _nmm_kernel(x_ref, g_ref, mod_ref, w_ref, o_ref, a_sc, *, shift_i, scale_i):
    @pl.when(pl.program_id(1) == 0)
    def _():
        x = x_ref[...]
        ms = jnp.mean(x * x, axis=-1, keepdims=True)
        y = x * lax.rsqrt(ms + EPS) * g_ref[...]
        h = y * (1.0 + mod_ref[scale_i:scale_i + 1, :]) + mod_ref[shift_i:shift_i + 1, :]
        a_sc[...] = h.astype(BF16)

    o_ref[...] = _dot(a_sc[...], w_ref[...])


def norm_mod_matmul(x, g, mods, w, shift_i, scale_i, tm=512, tn=None, name="nmm"):
    M = x.shape[0]
    N = w.shape[1]
    tn = tn or N
    return pl.pallas_call(
        functools.partial(_nmm_kernel, shift_i=shift_i, scale_i=scale_i),
        out_shape=jax.ShapeDtypeStruct((M, N), F32),
        grid=(M // tm, N // tn),
        in_specs=[pl.BlockSpec((tm, D), lambda i, j: (i, 0)),
                  pl.BlockSpec((1, D), lambda i, j: (0, 0)),
                  pl.BlockSpec((None, 6, D), lambda i, j: (_seq_of_rowblock(i, tm), 0, 0)),
                  pl.BlockSpec((D, tn), lambda i, j: (0, j))],
        out_specs=pl.BlockSpec((tm, tn), lambda i, j: (i, j)),
        scratch_shapes=[pltpu.VMEM((tm, D), BF16)],
        compiler_params=_cp(("parallel", "arbitrary")), name=name)(x, g.reshape(1, D), mods, w)


def _outproj_kernel(a1_ref, a2_ref, w_ref, x_ref, g_ref, mod_ref, o_ref, *, gate_i):
    y = _dot(a1_ref[...].astype(BF16), w_ref[0:D, :]) + _dot(a2_ref[...].astype(BF16), w_ref[D:2 * D, :])
    ms = jnp.mean(y * y, axis=-1, keepdims=True)
    o_ref[...] = x_ref[...] + mod_ref[gate_i:gate_i + 1, :] * (y * lax.rsqrt(ms + EPS) * g_ref[...])


def outproj_residual(a1, a2, w, x, g, mods, gate_i, tm=256, name="outproj"):
    M = a1.shape[0]
    return pl.pallas_call(
        functools.partial(_outproj_kernel, gate_i=gate_i),
        out_shape=jax.ShapeDtypeStruct((M, D), F32),
        grid=(M // tm,),
        in_specs=[pl.BlockSpec((tm, D), lambda i: (i, 0)),
                  pl.BlockSpec((tm, D), lambda i: (i, 0)),
                  pl.BlockSpec((2 * D, D), lambda i: (0, 0)),
                  pl.BlockSpec((tm, D), lambda i: (i, 0)),
                  pl.BlockSpec((1, D), lambda i: (0, 0)),
                  pl.BlockSpec((None, 6, D), lambda i: (_seq_of_rowblock(i, tm), 0, 0))],
        out_specs=pl.BlockSpec((tm, D), lambda i: (i, 0)),
        compiler_params=_cp(("parallel",)), name=name)(a1, a2, w, x, g.reshape(1, D), mods)


def _mod_kernel(c_ref, w_ref, b_ref, o_ref):
    o_ref[...] = _dot3(_silu(c_ref[...]), w_ref[...]) + b_ref[...]


def modulation(cvecs, w, b):
    N = w.shape[1]
    tn = 1024
    return pl.pallas_call(
        _mod_kernel, out_shape=jax.ShapeDtypeStruct((8, N), F32), grid=(N // tn,),
        in_specs=[pl.BlockSpec((8, D), lambda j: (0, 0)),
                  pl.BlockSpec((D, tn), lambda j: (0, j)),
                  pl.BlockSpec((1, tn), lambda j: (0, j))],
        out_specs=pl.BlockSpec((8, tn), lambda j: (0, j)),
        compiler_params=_cp(("parallel",)), name="modulation")(cvecs, w, b.reshape(1, N))


CONV_TM = 256


def _conv3_kernel(x_ref, prev_ref, next_ref, w_ref, b_ref, o_ref, *, act):
    tm = CONV_TM
    row0 = pl.program_id(0) * tm
    seq_len = jnp.where(row0 < N_LAT, SEQ, CTX)
    pos = jnp.where(row0 < N_LAT, row0 % SEQ, (row0 - N_LAT) % CTX)
    cur = x_ref[...]
    rows = lax.broadcasted_iota(jnp.int32, cur.shape, 0)
    prev_row = prev_ref[7:8, :] * (pos > 0).astype(F32)
    next_row = next_ref[0:1, :] * (pos + tm < seq_len).astype(F32)
    xm1 = jnp.where(rows == 0, prev_row, pltpu.roll(cur, 1, 0))
    xp1 = jnp.where(rows == tm - 1, next_row, pltpu.roll(cur, tm - 1, 0))
    y = xm1 * w_ref[0:1, :] + cur * w_ref[1:2, :] + xp1 * w_ref[2:3, :] + b_ref[...]
    o_ref[...] = _silu(y) if act else y


def dwconv3(p, col0, ncols, w, b, act, name, cb=1024):
    tm = CONV_TM
    cb = math.gcd(cb, math.gcd(col0, ncols)) if col0 else math.gcd(cb, ncols)
    r8 = tm // 8
    n8 = N_ROWS // 8
    c0 = col0 // cb
    return pl.pallas_call(
        functools.partial(_conv3_kernel, act=act),
        out_shape=jax.ShapeDtypeStruct((N_ROWS, ncols), F32),
        grid=(N_ROWS // tm, ncols // cb),
        in_specs=[pl.BlockSpec((tm, cb), lambda i, j: (i, c0 + j)),
                  pl.BlockSpec((8, cb), lambda i, j: (jnp.maximum(i * r8 - 1, 0), c0 + j)),
                  pl.BlockSpec((8, cb), lambda i, j: (jnp.minimum((i + 1) * r8, n8 - 1), c0 + j)),
                  pl.BlockSpec((3, cb), lambda i, j: (0, j)),
                  pl.BlockSpec((1, cb), lambda i, j: (0, j))],
        out_specs=pl.BlockSpec((tm, cb), lambda i, j: (i, j)),
        compiler_params=_cp(("parallel", "parallel")), name=name)(p, p, p, jnp.transpose(w), b.reshape(1, ncols))


def _ssd_kernel(xs_ref, bm_ref, cm_ref, dt_ref, dtT_ref, bias_ref, biasT_ref, alog_ref, alogT_ref,
                y_ref, st_ref):
    d = pl.program_id(0)
    c = pl.program_id(2)
    Q = SSD_Q
    HG = SSD_HEADS // SSD_G

    @pl.when(c == 0)
    def _():
        st_ref[...] = jnp.zeros_like(st_ref)

    isb = d == 1
    sgn = 1 - 2 * d
    ii = lax.broadcasted_iota(jnp.int32, (Q, Q), 0)
    jj = lax.broadcasted_iota(jnp.int32, (Q, Q), 1)
    tri = (jj <= ii).astype(BF16)
    triT = (ii <= jj).astype(BF16)
    mask = sgn * (ii - jj) >= 0
    xs = xs_ref[...]
    G = range(SSD_G)
    dt = [_softplus(dt_ref[g] + bias_ref[g]) for g in G]
    dtT = [_softplus(dtT_ref[g] + biasT_ref[g]) for g in G]
    a = [dt[g] * (-jnp.exp(alog_ref[g])) for g in G]
    aT = [dtT[g] * (-jnp.exp(alogT_ref[g])) for g in G]
    cs = [_dot2r(tri, a[g]) for g in G]
    csT = [_dot2l(aT[g], triT) for g in G]
    tot = [cs[g][Q - 1:Q, :] for g in G]
    p = [jnp.where(isb, a[g] - cs[g], cs[g]) for g in G]
    pT = [jnp.where(isb, aT[g] - csT[g], csT[g]) for g in G]
    dec_out = [jnp.exp(jnp.where(isb, tot[g], 0.0) + p[g]) for g in G]
    dec_state = [jnp.exp(jnp.where(isb, 0.0, tot[g]) - p[g]) for g in G]
    chunk_dec = [jnp.exp(tot[g]) for g in G]
    bm = [bm_ref[:, g * SSD_S:(g + 1) * SSD_S].astype(BF16) for g in G]
    cm = [cm_ref[:, g * SSD_S:(g + 1) * SSD_S].astype(BF16) for g in G]
    cb = [_dot_nt(cm[g], bm[g]) for g in G]
    nh = SSD_HEADS

    def spread(cols, width):
        v = jnp.concatenate(cols, axis=1)
        head = lax.broadcasted_iota(jnp.int32, (nh, nh * width), 1) // width
        e = (head == lax.broadcasted_iota(jnp.int32, (nh, nh * width), 0)).astype(BF16)
        h1 = v.astype(BF16)
        r1 = v - h1.astype(F32)
        h2 = r1.astype(BF16)
        h3 = (r1 - h2.astype(F32)).astype(BF16)
        return _dot(h1, e) + (_dot(h2, e) + _dot(h3, e))

    dt_x = spread(dt, SSD_P)
    dout_x = spread(dec_out, SSD_P)
    dst_x = spread(dec_state, SSD_P)
    p_x = spread(p, Q)
    xh_all = xs * dt_x
    xdec_all = (xh_all * dst_x).astype(BF16)
    xh_all = xh_all.astype(BF16)
    GH = [(g, h) for g in G for h in range(HG)]
    NH = range(len(GH))
    lm = [(cb[g] * jnp.exp(jnp.where(mask, p_x[:, n * Q:(n + 1) * Q] - pT[g][h:h + 1, :], -1e30))).astype(BF16)
          for n, (g, h) in enumerate(GH)]
    s_old = [st_ref[n] for n in NH]
    y_in = [_dot(lm[n], xh_all[:, n * SSD_P:(n + 1) * SSD_P]) for n in NH]
    y_st = [_dot(cm[g], s_old[n].astype(BF16)) for n, (g, h) in enumerate(GH)]
    upd = [lax.dot_general(bm[g], xdec_all[:, n * SSD_P:(n + 1) * SSD_P], (((0,), (0,)), ((), ())),
                           preferred_element_type=F32) for n, (g, h) in enumerate(GH)]
    for n, (g, h) in enumerate(GH):
        st_ref[n] = chunk_dec[g][:, h:h + 1] * s_old[n] + upd[n]
    y_ref[...] = jnp.concatenate(y_in, axis=1) + dout_x * jnp.concatenate(y_st, axis=1)


def _ssd_rowblock(d, b, c):
    n_ctx = CTX // SSD_Q
    n_lat = SEQ // SSD_Q
    cc = jnp.where(d == 0, c, n_ctx - 1 - c)
    lc = jnp.where(d == 0, c - n_ctx, n_ctx + n_lat - 1 - c)
    return jnp.where(c < n_ctx, N_LAT // SSD_Q + b * n_ctx + cc, b * n_lat + lc)


def ssd_scan(xbc, dt_raw, dt_bias, a_log):
    HG = SSD_HEADS // SSD_G
    W = SSD_HEADS * SSD_P
    dsel = dt_raw[:, :2 * SSD_HEADS].reshape(N_ROWS, 2, SSD_G, HG).transpose(1, 2, 0, 3)
    dselT = dsel.transpose(0, 1, 3, 2)
    bias = dt_bias.reshape(2, SSD_G, 1, HG)
    biasT = dt_bias.reshape(2, SSD_G, HG, 1)
    alog = a_log.reshape(2, SSD_G, 1, HG)
    alogT = a_log.reshape(2, SSD_G, HG, 1)
    nch = (CTX + SEQ) // SSD_Q
    rb = _ssd_rowblock
    GS = SSD_G * SSD_S
    par = lambda shape: pl.BlockSpec((None,) + shape, lambda d, b, c: (d, 0, 0, 0))
    return pl.pallas_call(
        _ssd_kernel,
        out_shape=jax.ShapeDtypeStruct((2, N_ROWS, W), F32),
        grid=(2, BATCH, nch),
        in_specs=[pl.BlockSpec((SSD_Q, W), lambda d, b, c: (rb(d, b, c), 0)),
                  pl.BlockSpec((SSD_Q, GS), lambda d, b, c: (rb(d, b, c), W // GS)),
                  pl.BlockSpec((SSD_Q, GS), lambda d, b, c: (rb(d, b, c), W // GS + 1)),
                  pl.BlockSpec((None, SSD_G, SSD_Q, HG), lambda d, b, c: (d, 0, rb(d, b, c), 0)),
                  pl.BlockSpec((None, SSD_G, HG, SSD_Q), lambda d, b, c: (d, 0, 0, rb(d, b, c))),
                  par((SSD_G, 1, HG)), par((SSD_G, HG, 1)), par((SSD_G, 1, HG)), par((SSD_G, HG, 1))],
        out_specs=pl.BlockSpec((None, SSD_Q, W), lambda d, b, c: (d, rb(d, b, c), 0)),
        scratch_shapes=[pltpu.VMEM((SSD_HEADS, SSD_S, SSD_P), F32)],
        compiler_params=_cp(("parallel", "parallel", "arbitrary")), name="ssd_scan",
    )(xbc, xbc, xbc, dsel, dselT, bias, biasT, alog, alogT)


def _ssd_out_kernel(yf_ref, yb_ref, xs_ref, z_ref, dskip_ref, nw_ref, o_ref):
    y = yf_ref[...] + yb_ref[...] + xs_ref[...] * dskip_ref[...]
    y = y * _silu(z_ref[...])
    gs = SSD_HEADS * SSD_P // SSD_G
    parts = []
    for g in range(SSD_G):
        yg = y[:, g * gs:(g + 1) * gs]
        parts.append(yg * lax.rsqrt(jnp.mean(yg * yg, axis=-1, keepdims=True) + EPS))
    o_ref[...] = jnp.concatenate(parts, axis=1) * nw_ref[...]


def ssd_output(y2, xbc, p, zcol, d_skip, norm_w, tm=256):
    W = SSD_HEADS * SSD_P
    dexp = jnp.repeat(d_skip, SSD_P).reshape(1, W)
    return pl.pallas_call(
        _ssd_out_kernel, out_shape=jax.ShapeDtypeStruct((N_ROWS, W), F32), grid=(N_ROWS // tm,),
        in_specs=[pl.BlockSpec((None, tm, W), lambda i: (0, i, 0)),
                  pl.BlockSpec((None, tm, W), lambda i: (1, i, 0)),
                  pl.BlockSpec((tm, W), lambda i: (i, 0)),
                  pl.BlockSpec((tm, W), lambda i: (i, zcol // W)),
                  pl.BlockSpec((1, W), lambda i: (0, 0)),
                  pl.BlockSpec((1, W), lambda i: (0, 0))],
        out_specs=pl.BlockSpec((tm, W), lambda i: (i, 0)),
        compiler_params=_cp(("parallel",)), name="ssd_output")(y2, y2, xbc, p, dexp, norm_w.reshape(1, W))


def _hyfilt_kernel(f_ref, w0_ref, b0_ref, fr0_ref, w1_ref, b1_ref, fr1_ref, w2_ref, dl_ref, h_ref, ss_ref, *,
                   n_tiles):
    f = f_ref[...]
    h = jnp.sin(fr0_ref[...] * (_dot3(f, w0_ref[...]) + b0_ref[...]))
    h = jnp.sin(fr1_ref[...] * (_dot3(h, w1_ref[...]) + b1_ref[...]))
    h = _dot3(h, w2_ref[...])
    h = h * jnp.exp(-f[:, 0:1] * dl_ref[...])
    side = pl.program_id(0) // n_tiles
    j = pl.program_id(0) % n_tiles

    @pl.when(j == 0)
    def _():
        ss_ref[...] = jnp.zeros_like(ss_ref)

    ss_ref[...] += jnp.sum(h * h, axis=0, keepdims=True)
    row = lax.broadcasted_iota(jnp.int32, (h.shape[0], 1), 0) + j * h.shape[0]
    h_ref[...] = jnp.where((side == 1) & (row == 0), 0.0, h)


def hyena_filter_taps(L, hp):
    pos = jnp.arange(L, dtype=F32)
    t = pos / (L - 1)
    bands = (HY_EMB - 1) // 2
    freqs = jnp.linspace(1e-4, bands - 1, bands, dtype=F32)
    ang = (2.0 * math.pi / L) * pos[:, None] * freqs[None, :]
    feats = jnp.concatenate([t[:, None], jnp.cos(ang), -jnp.sin(ang)], axis=-1)
    feats = jnp.pad(feats, ((0, 0), (0, 128 - HY_EMB)))
    feats = jnp.concatenate([feats, jnp.flip(feats, axis=0)], axis=0)
    w0 = jnp.pad(hp['hy_mlp_w0'], ((0, 128 - HY_EMB), (0, 0)))
    min_decay = math.log(1e-2) / 1.5
    max_decay = math.log(1e-2) / 0.3
    deltas = jnp.abs(jnp.linspace(min_decay, max_decay, HY_W, dtype=F32))
    dl = jnp.tile(deltas, 2).reshape(1, 2 * HY_W)
    w2 = hp['hy_mlp_w2'].reshape(HY_HID, 2, 2, HY_W).transpose(0, 2, 1, 3).reshape(HY_HID, 4 * HY_W)
    tl = min(L, 512)
    n_tiles = L // tl
    NS = 2 * HY_W
    full = lambda shape: pl.BlockSpec(shape, lambda i: (0, 0))
    return pl.pallas_call(
        functools.partial(_hyfilt_kernel, n_tiles=n_tiles),
        out_shape=(jax.ShapeDtypeStruct((2 * L, NS), F32), jax.ShapeDtypeStruct((1, 2 * NS), F32)),
        grid=(2 * n_tiles,),
        in_specs=[pl.BlockSpec((tl, 128), lambda i: (i, 0)), full((128, HY_HID)), full((1, HY_HID)),
                  full((1, HY_HID)), full((HY_HID, HY_HID)), full((1, HY_HID)), full((1, HY_HID)),
                  pl.BlockSpec((HY_HID, NS), lambda i: (0, i // n_tiles)), full((1, NS))],
        out_specs=(pl.BlockSpec((tl, NS), lambda i: (i, 0)), pl.BlockSpec((1, NS), lambda i: (0, i // n_tiles))),
        compiler_params=_cp(("arbitrary",)), name="hyena_filter",
    )(feats, w0, hp['hy_mlp_b0'].reshape(1, -1), hp['hy_freq0'].reshape(1, -1), hp['hy_mlp_w1'],
      hp['hy_mlp_b1'].reshape(1, -1), hp['hy_freq1'].reshape(1, -1), w2, dl)


def _cis(num, den):
    ang = (2.0 * math.pi / den) * (num % den).astype(F32)
    return jnp.cos(ang), -jnp.sin(ang)


def _fft_consts(NB, BS):
    N = NB * BS
    h = NB // 2
    k1 = jnp.arange(h, dtype=jnp.int32)
    j = jnp.arange(NB, dtype=jnp.int32)
    re, im = _cis(j[None, :] * (2 * k1[:, None] + 1), 2 * NB)
    f1 = jnp.concatenate([re, im], axis=0)
    neg = jnp.where(j >= h, -1.0, 1.0)[None, :]
    f1_data = f1[:, :h]
    f1_filt = f1 * neg
    f1_inv = (2.0 / N) * jnp.concatenate([re[:, :h].T, im[:, :h].T], axis=1)
    r = jnp.arange(BS, dtype=jnp.int32)
    k2 = jnp.arange(BS, dtype=jnp.int32)
    kk = 2 * k1[:, None, None] + 2 * NB * k2[None, :, None] + 1
    gre, gim = _cis(kk * r[None, None, :], 2 * N)
    gf = jnp.concatenate([jnp.concatenate([gre, -gim], axis=2), jnp.concatenate([gim, gre], axis=2)], axis=1)
    gret, gimt = gre.transpose(0, 2, 1), gim.transpose(0, 2, 1)
    gi = jnp.concatenate([jnp.concatenate([gret, gimt], axis=2), jnp.concatenate([-gimt, gret], axis=2)], axis=1)
    return (f1_data.astype(BF16), f1_filt.astype(BF16), f1_inv.astype(BF16), gf.astype(BF16), gi.astype(BF16))


FFT_PAD = 8


FFT_LW = 128


def _fft_fwd_kernel(ua_ref, ub_ref, f1_ref, g_ref, o_ref, t_sc, *, NB, BS, nj, kg):
    pitch = NB + FFT_PAD
    u_refs = (ua_ref, ub_ref)

    @pl.when(pl.program_id(2) == 0)
    def _():
        f1 = f1_ref[...]

        def body(r, carry):
            xr = jnp.concatenate([u[pl.ds(r, nj, stride=BS), :] for u in u_refs], axis=1).astype(BF16)
            res = _dot(f1, xr)
            for hh in range(2):
                t_sc[hh, pl.ds(pl.multiple_of(r * pitch, 8), NB), :] = res[:, hh * FFT_LW:(hh + 1) * FFT_LW]
            return carry

        lax.fori_loop(0, BS, body, 0, unroll=8)

    k0 = pl.program_id(2) * kg
    for i in range(kg):
        are = jnp.concatenate([t_sc[hh, pl.ds(k0 + i, BS, stride=pitch), :] for hh in range(2)], axis=1)
        aim = jnp.concatenate([t_sc[hh, pl.ds(k0 + i + NB // 2, BS, stride=pitch), :] for hh in range(2)], axis=1)
        a = jnp.concatenate([are, aim], axis=0).astype(BF16)
        o_ref[i] = _dot(g_ref[i], a)


def fft_fwd(u, col0, nbatch, nj, f1, gf, NB, BS, kg=8):
    h = NB // 2
    kg = min(kg, h)
    lw = FFT_LW
    ct = 2 * lw
    return pl.pallas_call(
        functools.partial(_fft_fwd_kernel, NB=NB, BS=BS, nj=nj, kg=kg),
        out_shape=jax.ShapeDtypeStruct((nbatch, h, 2 * BS, HY_W), F32),
        grid=(nbatch, HY_W // ct, h // kg),
        in_specs=[pl.BlockSpec((nj * BS, lw), lambda b, c, k: (b, col0 // lw + 2 * c), pipeline_mode=pl.Buffered(1)),
                  pl.BlockSpec((nj * BS, lw), lambda b, c, k: (b, col0 // lw + 2 * c + 1),
                               pipeline_mode=pl.Buffered(1)),
                  pl.BlockSpec((NB, nj), lambda b, c, k: (0, 0)),
                  pl.BlockSpec((kg, 2 * BS, 2 * BS), lambda b, c, k: (k, 0, 0))],
        out_specs=pl.BlockSpec((None, kg, 2 * BS, ct), lambda b, c, k: (b, k, 0, c)),
        scratch_shapes=[pltpu.VMEM((2, BS * (NB + FFT_PAD), lw), F32)],
        compiler_params=_cp(("parallel", "parallel", "arbitrary")), name="hyena_fft_fwd")(u, u, f1, gf)


def _cmul(u, h, half):
    ure, uim = u[:half], u[half:]
    hre, him = h[:half], h[half:]
    return jnp.concatenate([ure * hre - uim * him, ure * him + uim * hre], axis=0)


def _fft_inv_kernel(us_ref, hs_ref, gi_ref, f1i_ref, o_ref, t_sc, y_sc, *, NB, BS, kg):
    ks = pl.program_id(2)
    pitch = 2 * BS + FFT_PAD
    for i in range(kg):
        y = _cmul(us_ref[i], hs_ref[i], BS).astype(BF16)
        row = pl.multiple_of((ks * kg + i) * pitch, 8)
        res = _dot(gi_ref[i], y)
        for hh in range(2):
            t_sc[hh, pl.ds(row, 2 * BS), :] = res[:, hh * FFT_LW:(hh + 1) * FFT_LW]

    @pl.when(ks == pl.num_programs(2) - 1)
    def _():
        f1i = f1i_ref[...]

        def body(r, carry):
            bre = jnp.concatenate([t_sc[hh, pl.ds(r, NB // 2, stride=pitch), :] for hh in range(2)], axis=1)
            bim = jnp.concatenate([t_sc[hh, pl.ds(r + BS, NB // 2, stride=pitch), :] for hh in range(2)], axis=1)
            b = jnp.concatenate([bre, bim], axis=0).astype(BF16)
            res = _dot(f1i, b)
            for hh in range(2):
                y_sc[hh, pl.ds(r, NB // 2, stride=BS), :] = res[:, hh * FFT_LW:(hh + 1) * FFT_LW]
            return carry

        lax.fori_loop(0, BS, body, 0, unroll=8)
        o_ref[...] = jnp.concatenate([y_sc[0], y_sc[1]], axis=1)


def fft_inv(us, hs, gi, f1i, NB, BS, kg=8):
    nbatch, h = us.shape[0], NB // 2
    kg = min(kg, h)
    L = h * BS
    ct = 2 * FFT_LW
    return pl.pallas_call(
        functools.partial(_fft_inv_kernel, NB=NB, BS=BS, kg=kg),
        out_shape=jax.ShapeDtypeStruct((nbatch * L, HY_W), F32),
        grid=(nbatch, HY_W // ct, h // kg),
        in_specs=[pl.BlockSpec((None, kg, 2 * BS, ct), lambda b, c, k: (b, k, 0, c)),
                  pl.BlockSpec((None, kg, 2 * BS, ct), lambda b, c, k: (0, k, 0, c)),
                  pl.BlockSpec((kg, 2 * BS, 2 * BS), lambda b, c, k: (k, 0, 0)),
                  pl.BlockSpec((h, NB), lambda b, c, k: (0, 0))],
        out_specs=pl.BlockSpec((L, ct), lambda b, c, k: (b, c)),
        scratch_shapes=[pltpu.VMEM((2, h * (2 * BS + FFT_PAD), FFT_LW), F32), pltpu.VMEM((2, L, FFT_LW), F32)],
        compiler_params=_cp(("parallel", "parallel", "arbitrary")), name="hyena_fft_inv")(us, hs, gi, f1i)


def _dft_consts(L):
    N = 2 * L
    k = jnp.arange(L, dtype=jnp.int32)
    n = jnp.arange(N, dtype=jnp.int32)
    re, im = _cis(n[None, :] * (2 * k[:, None] + 1), 2 * N)
    f = jnp.concatenate([re, im], axis=0)
    neg = jnp.where(n >= L, -1.0, 1.0)[None, :]
    fi = (2.0 / N) * jnp.concatenate([re[:, :L].T, im[:, :L].T], axis=1)
    return f[:, :L].astype(BF16), (f * neg).astype(BF16), fi.astype(BF16)


def _cdft_kernel(f_ref, x_ref, o_ref):
    o_ref[...] = _dot(f_ref[...], x_ref[...].astype(BF16))


def dft_fwd(x, f, row0, col0, nbatch, ct=256):
    M, K = f.shape
    return pl.pallas_call(
        _cdft_kernel, out_shape=jax.ShapeDtypeStruct((nbatch, M, HY_W), F32),
        grid=(nbatch, HY_W // ct),
        in_specs=[pl.BlockSpec((M, K), lambda b, c: (0, 0)),
                  pl.BlockSpec((K, ct), lambda b, c: (row0 // K + b, col0 // ct + c))],
        out_specs=pl.BlockSpec((None, M, ct), lambda b, c: (b, 0, c)),
        compiler_params=_cp(("parallel", "parallel")), name="hyena_dft_fwd")(f, x)


def _cdft_inv_kernel(us_ref, hs_ref, fi_ref, o_ref):
    half = us_ref.shape[0] // 2
    o_ref[...] = _dot(fi_ref[...], _cmul(us_ref[...], hs_ref[...], half).astype(BF16))


def dft_inv(us, hs, fi, ct=256):
    nbatch, M2, _ = us.shape
    L = fi.shape[0]
    return pl.pallas_call(
        _cdft_inv_kernel, out_shape=jax.ShapeDtypeStruct((nbatch * L, HY_W), F32),
        grid=(nbatch, HY_W // ct),
        in_specs=[pl.BlockSpec((None, M2, ct), lambda b, c: (b, 0, c)),
                  pl.BlockSpec((None, M2, ct), lambda b, c: (0, 0, c)),
                  pl.BlockSpec((L, M2), lambda b, c: (0, 0))],
        out_specs=pl.BlockSpec((L, ct), lambda b, c: (b, c)),
        compiler_params=_cp(("parallel", "parallel")), name="hyena_dft_inv")(us, hs, fi)


def _hy_gate_kernel(g_ref, y_ref, u_ref, ss_ref, b_ref, o_ref):
    scale = lax.rsqrt(ss_ref[0:1, :] + ss_ref[1:2, :] + 1e-6)
    o_ref[...] = g_ref[...] * (y_ref[...] * scale + u_ref[...] * b_ref[...])


def _hy_gate2_kernel(g_ref, yl_ref, yc_ref, ul_ref, uc_ref, ssl_ref, ssc_ref, b_ref, o_ref, *, n_lat_t):
    is_lat = pl.program_id(0) < n_lat_t
    y = jnp.where(is_lat, yl_ref[...], yc_ref[...])
    uin = jnp.where(is_lat, ul_ref[...], uc_ref[...])
    ss = jnp.where(is_lat, ssl_ref[...], ssc_ref[...])
    scale = lax.rsqrt(ss[0:1, :] + ss[1:2, :] + 1e-6)
    o_ref[...] = g_ref[...] * (y * scale + uin * b_ref[...])


def hy_gate(gate, gcol, grow, y, uin, ucol, urow, ss, order, bias, tm=256):
    M = y.shape[0]
    return pl.pallas_call(
        _hy_gate_kernel, out_shape=jax.ShapeDtypeStruct((M, HY_W), F32), grid=(M // tm,),
        in_specs=[pl.BlockSpec((tm, HY_W), lambda i: (grow // tm + i, gcol // HY_W)),
                  pl.BlockSpec((tm, HY_W), lambda i: (i, 0)),
                  pl.BlockSpec((tm, HY_W), lambda i: (urow // tm + i, ucol // HY_W)),
                  pl.BlockSpec((None, 2, HY_W), lambda i: (order, 0, 0)),
                  pl.BlockSpec((None, 1, HY_W), lambda i: (order, 0, 0))],
        out_specs=pl.BlockSpec((tm, HY_W), lambda i: (i, 0)),
        compiler_params=_cp(("parallel",)), name="hyena_gate")(gate, y, uin, ss, bias)


def hyena(u, hp):
    C = HY_W
    bias = hp['hy_bias'].reshape(2, 1, C)
    NB = BS = int(round(math.sqrt(2 * SEQ)))
    f1d, f1f, f1i, gf, gi = _fft_consts(NB, BS)
    taps, ss = hyena_filter_taps(SEQ, hp)
    ss = ss.reshape(2, 2, C).transpose(1, 0, 2)
    conv_l = lambda zin, zcol, order: fft_inv(fft_fwd(zin, zcol, BATCH, NB // 2, f1d, gf, NB, BS),
                                              fft_fwd(taps, order * C, 1, NB, f1f, gf, NB, BS), gi, f1i, NB, BS)
    z1_lat = hy_gate(u, 0, 0, conv_l(u, 2 * C, 0), u, 2 * C, 0, ss, 0, bias)
    y2_lat = conv_l(z1_lat, 0, 1)
    fd, ff, fi = _dft_consts(CTX)
    taps_c, ss_c = hyena_filter_taps(CTX, hp)
    ss_c = ss_c.reshape(2, 2, C).transpose(1, 0, 2)
    conv_c = lambda zin, zrow, zcol, order: dft_inv(dft_fwd(zin, fd, zrow, zcol, BATCH),
                                                    dft_fwd(taps_c, ff, 0, order * C, 1), fi)
    z1_ctx = hy_gate(u, 0, N_LAT, conv_c(u, N_LAT, 2 * C, 0), u, 2 * C, N_LAT, ss_c, 0, bias)
    y2_ctx = conv_c(z1_ctx, 0, 0, 1)
    tm = 256
    n_lat_t = N_LAT // tm
    lat = lambda i: (jnp.minimum(i, n_lat_t - 1), 0)
    ctx = lambda i: (jnp.maximum(i - n_lat_t, 0), 0)
    return pl.pallas_call(
        functools.partial(_hy_gate2_kernel, n_lat_t=n_lat_t),
        out_shape=jax.ShapeDtypeStruct((N_ROWS, C), F32), grid=(N_ROWS // tm,),
        in_specs=[pl.BlockSpec((tm, C), lambda i: (i, 1)),
                  pl.BlockSpec((tm, C), lat), pl.BlockSpec((tm, C), ctx),
                  pl.BlockSpec((tm, C), lat), pl.BlockSpec((tm, C), ctx),
                  pl.BlockSpec((None, 2, C), lambda i: (1, 0, 0)), pl.BlockSpec((None, 2, C), lambda i: (1, 0, 0)),
                  pl.BlockSpec((None, 1, C), lambda i: (1, 0, 0))],
        out_specs=pl.BlockSpec((tm, C), lambda i: (i, 0)),
        compiler_params=_cp(("parallel",)), name="hyena_gate2",
    )(u, y2_lat, y2_ctx, z1_lat, z1_ctx, ss, ss_c, bias)


EV_SSD_IN = SSD_HEADS * SSD_P
EV_XBC = EV_SSD_IN + 2 * SSD_G * SSD_S
EV_PAD_N = 5760


def even_mixer(x, mods, g_pre, g_post, ep):
    o1 = EV_SSD_IN
    o2 = o1 + EV_XBC
    o3 = o2 + 2 * SSD_HEADS
    w = ep['w_in']
    n_in = w.shape[1]
    hw = 3 * HY_W
    w_perm = jnp.concatenate([w[:, o3:], w[:, :o2], w[:, o2:o3],
                              jnp.zeros((D, EV_PAD_N - n_in), F32)], axis=1).astype(BF16)
    p = norm_mod_matmul(x, g_pre, mods, w_perm, 0, 1, tn=1920, name="even_in_proj")
    xbc = dwconv3(p, hw + o1, EV_XBC, ep['ssd_conv_w'], ep['ssd_conv_b'], True, "ssd_conv")
    u = dwconv3(p, 0, hw, ep['hy_conv_w'], ep['hy_conv_b'], False, "hyena_conv")
    dt_raw = p[:, hw + o2:hw + o2 + 2 * SSD_HEADS]
    y2 = ssd_scan(xbc, dt_raw, ep['ssd_dt_bias'], ep['ssd_a_log'])
    s = ssd_output(y2, xbc, p, hw, ep['ssd_d'], ep['ssd_norm_w'])
    zh = hyena(u, ep)
    return outproj_residual(s, zh, ep['w_out'].astype(BF16), x, g_post, mods, 2, name="even_out_proj")


def _head_sum(x, e, et):
    return _dot2l(_dot2l(x, e), et)


def _rw_prep_kernel(r_ref, k_ref, v_ref, lo_ref, w0_ref, wup_ref, a0_ref, aup_ref, gup_ref, kk_ref, ka_ref,
                    rk_ref, e_ref, et_ref, lw_ref, kd_ref, be_ref, kap_ref, g_ref, bonus_ref):
    r, k, v = r_ref[...], k_ref[...], v_ref[...]
    lo = lo_ref[...]
    wc, ac, gc = lo[:, 0:64], lo[:, 64:128], lo[:, 128:384]
    e, et = e_ref[...], et_ref[...]
    kk = k * kk_ref[...]
    kap = kk * lax.rsqrt(_head_sum(kk * kk, e, et) + 1e-12)
    kap_ref[...] = kap
    g_ref[...] = _dot(_sigmoid(gc).astype(BF16), gup_ref[...].astype(BF16))
    kd_sum = jnp.zeros_like(k)
    for d in range(2):
        wlog = -_softplus(-(w0_ref[d:d + 1, :] + _dot3(jnp.tanh(wc), wup_ref[d]))) - 0.5
        lw_ref[d] = -jnp.exp(wlog)
        a = _sigmoid(a0_ref[d:d + 1, :] + _dot(ac.astype(BF16), aup_ref[d].astype(BF16)))
        kd = k * (1.0 + (a - 1.0) * ka_ref[...])
        kd_ref[d] = kd
        be_ref[d] = kap * a
        kd_sum = kd_sum + kd
    bonus_ref[...] = _head_sum(r * kd_sum * rk_ref[...], e, et) * v


def rwkv_prepare(code, lora, op, tm=256):
    W = RW_H * RW_N
    heads = jnp.arange(W, dtype=jnp.int32) // RW_N
    e = (heads[:, None] == jnp.arange(128, dtype=jnp.int32)[None, :]).astype(BF16)
    et = jnp.transpose(e)
    gup = jnp.pad(op['g_up'], ((0, 256 - op['g_up'].shape[0]), (0, 0)))
    row = lambda a: a.reshape(1, W)
    full2 = lambda shape: pl.BlockSpec(shape, lambda i: (0,) * len(shape))
    outs = pl.pallas_call(
        _rw_prep_kernel,
        out_shape=(jax.ShapeDtypeStruct((2, N_ROWS, W), F32), jax.ShapeDtypeStruct((2, N_ROWS, W), F32),
                   jax.ShapeDtypeStruct((2, N_ROWS, W), F32), jax.ShapeDtypeStruct((N_ROWS, W), F32),
                   jax.ShapeDtypeStruct((N_ROWS, W), F32), jax.ShapeDtypeStruct((N_ROWS, W), F32)),
        grid=(N_ROWS // tm,),
        in_specs=[pl.BlockSpec((tm, W), lambda i: (i, 0)), pl.BlockSpec((tm, W), lambda i: (i, 1)),
                  pl.BlockSpec((tm, W), lambda i: (i, 2)), pl.BlockSpec((tm, 384), lambda i: (i, 0)),
                  full2((2, W)), full2((2, 64, W)), full2((2, W)), full2((2, 64, W)), full2((256, W)),
                  full2((1, W)), full2((1, W)), full2((1, W)), full2((W, 128)), full2((128, W))],
        out_specs=(pl.BlockSpec((2, tm, W), lambda i: (0, i, 0)), pl.BlockSpec((2, tm, W), lambda i: (0, i, 0)),
                   pl.BlockSpec((2, tm, W), lambda i: (0, i, 0)), pl.BlockSpec((tm, W), lambda i: (i, 0)),
                   pl.BlockSpec((tm, W), lambda i: (i, 0)), pl.BlockSpec((tm, W), lambda i: (i, 0))),
        compiler_params=_cp(("parallel",)), name="rwkv_prepare",
    )(code, code, code, lora, op['w0'], op['w_up'], op['a0'], op['a_up'], gup, row(op['k_k']), row(op['k_a']),
      row(op['r_k']), e, et)
    return outs


def _rw_scan_kernel(r_ref, v_ref, lw_ref, kd_ref, be_ref, kap_ref, y_ref, st_ref):
    d = pl.program_id(0)
    c = pl.program_id(2)
    C = RW_CHUNK
    N = RW_N

    @pl.when(c == 0)
    def _():
        st_ref[...] = jnp.zeros_like(st_ref)

    isb = d == 1
    sgn = 1 - 2 * d
    ii = lax.broadcasted_iota(jnp.int32, (C, C), 0)
    jj = lax.broadcasted_iota(jnp.int32, (C, C), 1)
    dif = sgn * (ii - jj)
    incl = dif >= 0
    strict = dif > 0
    tri = incl.astype(BF16)
    eye = (ii == jj).astype(F32)
    blk = [(ii >> s) == (jj >> s) for s in range(3, C.bit_length() - 1)]
    masks = [blk[0]] + [blk[l] & ~blk[l - 1] for l in range(1, len(blk))] + [~blk[-1]]
    lw = lw_ref[...]
    cum = _dot2r(tri, lw)
    ec = jnp.exp(cum)
    en = jnp.exp(-cum)
    ea = jnp.exp(cum - lw)
    last = jnp.where(isb, cum[0:1, :], cum[C - 1:C, :])
    el = jnp.exp(last - cum)
    kap = kap_ref[...]
    r = r_ref[...]
    v = v_ref[...]
    a_t = -kap * ea
    r_t = r * ec
    b_t = be_ref[...] * en
    k_t = kd_ref[...] * en
    b_l = be_ref[...] * el
    k_l = kd_ref[...] * el
    pc = jnp.exp(last)
    H = range(RW_H)
    sl = [slice(h * N, (h + 1) * N) for h in H]
    bd = lambda a, b: _dot(a.astype(BF16), b.astype(BF16))
    tn = lambda a, b: lax.dot_general(a, b, (((0,), (0,)), ((), ())), preferred_element_type=F32)
    sc = [_dot_nt(jnp.concatenate([a_t[:, sl[h]], r_t[:, sl[h]]], axis=0).astype(BF16),
                  jnp.concatenate([b_t[:, sl[h]], k_t[:, sl[h]]], axis=0).astype(BF16)) for h in H]
    n_ab = [jnp.where(strict, sc[h][0:C, 0:C], 0.0) for h in H]
    a_ak = [jnp.where(strict, sc[h][0:C, C:2 * C], 0.0).astype(BF16) for h in H]
    m_rb = [jnp.where(incl, sc[h][C:2 * C, 0:C], 0.0).astype(BF16) for h in H]
    m_rk = [jnp.where(incl, sc[h][C:2 * C, C:2 * C], 0.0).astype(BF16) for h in H]
    vh = [v[:, sl[h]].astype(BF16) for h in H]
    d0 = [jnp.where(masks[0], n_ab[h], 0.0) for h in H]
    d2 = [bd(d0[h], d0[h]) for h in H]
    d4 = [bd(d2[h], d2[h]) for h in H]
    t = [bd(eye + d0[h], eye + d2[h]) for h in H]
    t = [bd(t[h], eye + d4[h]) for h in H]
    for m in masks[1:]:
        et = [bd(jnp.where(m, n_ab[h], 0.0), t[h]) for h in H]
        t = [t[h] + bd(t[h], et[h]) for h in H]
    amv = [_dot(jnp.concatenate([a_ak[h], m_rk[h]], axis=0), vh[h]) for h in H]
    wub = [bd(t[h], jnp.concatenate([a_t[:, sl[h]], amv[h][0:C]], axis=1)).astype(BF16) for h in H]
    kv = [tn(k_l[:, sl[h]].astype(BF16), vh[h]) for h in H]
    qy = [_dot(m_rb[h], wub[h]) + jnp.concatenate([r_t[:, sl[h]], amv[h][C:2 * C]], axis=1) for h in H]
    pp = [tn(b_l[:, sl[h]].astype(BF16), wub[h]) + jnp.concatenate([eye[0:N, 0:N] * pc[:, sl[h]], kv[h]], axis=1)
          for h in H]
    h_old = [st_ref[h] for h in H]
    ys = [bd(qy[h][:, 0:N], h_old[h]) + qy[h][:, N:2 * N] for h in H]
    for h in H:
        st_ref[h] = _dot3(pp[h][:, 0:N], h_old[h]) + pp[h][:, N:2 * N]
    y_ref[...] = jnp.concatenate(ys, axis=1)


def _rw_rowblock(d, b, c):
    n_ctx = CTX // RW_CHUNK
    n_lat = SEQ // RW_CHUNK
    cc = jnp.where(d == 0, c, n_ctx - 1 - c)
    lc = jnp.where(d == 0, c - n_ctx, n_ctx + n_lat - 1 - c)
    return jnp.where(c < n_ctx, N_LAT // RW_CHUNK + b * n_ctx + cc, b * n_lat + lc)


def rwkv_scan(code, lw, kd, be, kap):
    W = RW_H * RW_N
    nch = (CTX + SEQ) // RW_CHUNK
    rb = lambda d, b, c: _rw_rowblock(d, b, c)
    return pl.pallas_call(
        _rw_scan_kernel,
        out_shape=jax.ShapeDtypeStruct((2, N_ROWS, W), F32),
        grid=(2, BATCH, nch),
        in_specs=[pl.BlockSpec((RW_CHUNK, W), lambda d, b, c: (rb(d, b, c), 0)),
                  pl.BlockSpec((RW_CHUNK, W), lambda d, b, c: (rb(d, b, c), 2)),
                  pl.BlockSpec((None, RW_CHUNK, W), lambda d, b, c: (d, rb(d, b, c), 0)),
                  pl.BlockSpec((None, RW_CHUNK, W), lambda d, b, c: (d, rb(d, b, c), 0)),
                  pl.BlockSpec((None, RW_CHUNK, W), lambda d, b, c: (d, rb(d, b, c), 0)),
                  pl.BlockSpec((RW_CHUNK, W), lambda d, b, c: (rb(d, b, c), 0))],
        out_specs=pl.BlockSpec((None, RW_CHUNK, W), lambda d, b, c: (d, rb(d, b, c), 0)),
        scratch_shapes=[pltpu.VMEM((RW_H, RW_N, RW_N), F32)],
        compiler_params=_cp(("parallel", "parallel", "arbitrary")), name="rwkv_scan",
    )(code, code, lw, kd, be, kap)


def _rw_out_kernel(yf_ref, yb_ref, bonus_ref, g_ref, lnw_ref, lnb_ref, e_ref, et_ref, o_ref):
    e, et = e_ref[...], et_ref[...]
    y = yf_ref[...] + yb_ref[...]
    mean = _head_sum(y, e, et) * (1.0 / RW_N)
    yc = y - mean
    var = _head_sum(yc * yc, e, et) * (1.0 / RW_N)
    yn = yc * lax.rsqrt(var + RW_GN_EPS) * lnw_ref[...] + lnb_ref[...]
    o_ref[...] = (yn + bonus_ref[...]) * g_ref[...]


def rwkv_output(y2, bonus, g, op, tm=256):
    W = RW_H * RW_N
    heads = jnp.arange(W, dtype=jnp.int32) // RW_N
    e = (heads[:, None] == jnp.arange(128, dtype=jnp.int32)[None, :]).astype(BF16)
    et = jnp.transpose(e)
    M = N_LAT
    return pl.pallas_call(
        _rw_out_kernel, out_shape=jax.ShapeDtypeStruct((M, W), F32), grid=(M // tm,),
        in_specs=[pl.BlockSpec((None, tm, W), lambda i: (0, i, 0)), pl.BlockSpec((None, tm, W), lambda i: (1, i, 0)),
                  pl.BlockSpec((tm, W), lambda i: (i, 0)), pl.BlockSpec((tm, W), lambda i: (i, 0)),
                  pl.BlockSpec((1, W), lambda i: (0, 0)), pl.BlockSpec((1, W), lambda i: (0, 0)),
                  pl.BlockSpec((W, 128), lambda i: (0, 0)), pl.BlockSpec((128, W), lambda i: (0, 0))],
        out_specs=pl.BlockSpec((tm, W), lambda i: (i, 0)),
        compiler_params=_cp(("parallel",)), name="rwkv_output",
    )(y2, y2, bonus, g, op['ln_w'].reshape(1, W), op['ln_b'].reshape(1, W), e, et)


AT_Q = RW_H * AT_HD
AT_KW = AT_KV * AT_HD
AT_TQ = 512
AT_TK = 768


def _rope_tables(tm):
    half = AT_HD // 2
    inv = 10000.0 ** (-jnp.arange(0, half, 2, dtype=F32) / half)
    pos = jnp.arange(SEQ, dtype=jnp.int32)
    row = (pos // GRID_W).astype(F32)[:, None] * inv
    col = (pos % GRID_W).astype(F32)[:, None] * inv
    cos_h = jnp.concatenate([jnp.cos(row), jnp.cos(row), jnp.cos(col), jnp.cos(col)], axis=1)
    sin_h = jnp.concatenate([-jnp.sin(row), jnp.sin(row), -jnp.sin(col), jnp.sin(col)], axis=1)
    cos_t = jnp.concatenate([jnp.tile(cos_h, (1, 2)), jnp.ones((tm, 128), F32)], axis=0)
    sin_t = jnp.concatenate([jnp.tile(sin_h, (1, 2)), jnp.zeros((tm, 128), F32)], axis=0)
    return cos_t, sin_t


def _rot_partner(x):
    q = AT_HD // 4
    w = x.shape[1]
    lane = lax.broadcasted_iota(jnp.int32, x.shape, 1)
    return jnp.where((lane % (2 * q)) < q, pltpu.roll(x, w - q, 1), pltpu.roll(x, q, 1))


def _at_prep_kernel(q_ref, k_ref, v_ref, cos_ref, sin_ref, qn_ref, kn_ref, e_ref, et_ref, qo_ref, ko_ref, vo_ref):
    e, et = e_ref[...], et_ref[...]
    cos2, sin2 = cos_ref[...], sin_ref[...]

    def norm_rope(x, gain, nrep):
        ms = _head_sum(x * x, e[:x.shape[1]], et[:, :x.shape[1]]) * (1.0 / AT_HD)
        xn = x * lax.rsqrt(ms + EPS) * gain
        cos = jnp.tile(cos2, (1, nrep))
        sin = jnp.tile(sin2, (1, nrep))
        return xn * cos + _rot_partner(xn) * sin

    qn = norm_rope(q_ref[...], qn_ref[...], AT_Q // 128) * (AT_HD ** -0.5 * math.log2(math.e))
    qo_ref[...] = jnp.transpose(qn).astype(BF16)
    ko_ref[...] = norm_rope(k_ref[...], kn_ref[...], AT_KW // 128).astype(BF16)
    vo_ref[...] = jnp.transpose(v_ref[...]).astype(BF16)


def attention_prepare(p, q_norm, k_norm, tm=256):
    cos_t, sin_t = _rope_tables(tm)
    heads = jnp.arange(AT_Q, dtype=jnp.int32) // AT_HD
    e = (heads[:, None] == jnp.arange(128, dtype=jnp.int32)[None, :]).astype(BF16)
    et = jnp.transpose(e)
    tab = lambda i: jnp.where(i * tm < N_LAT, ((i * tm) % SEQ) // tm, SEQ // tm)
    n_lat_t, n_seq_t, n_ctx_t = N_LAT // tm, SEQ // tm, CTX // tm
    kvb = lambda i: jnp.where(i < n_lat_t, (i // n_seq_t) * (n_seq_t + n_ctx_t) + n_ctx_t + i % n_seq_t,
                              ((i - n_lat_t) // n_ctx_t) * (n_seq_t + n_ctx_t) + (i - n_lat_t) % n_ctx_t)
    qcol = (3 * RW_H * RW_N) // AT_Q
    kcol = (3 * RW_H * RW_N + AT_Q) // AT_KW
    return pl.pallas_call(
        _at_prep_kernel,
        out_shape=(jax.ShapeDtypeStruct((AT_Q, N_ROWS), BF16), jax.ShapeDtypeStruct((N_ROWS, AT_KW), BF16),
                   jax.ShapeDtypeStruct((AT_KW, N_ROWS), BF16)),
        grid=(N_ROWS // tm,),
        in_specs=[pl.BlockSpec((tm, AT_Q), lambda i: (i, qcol)),
                  pl.BlockSpec((tm, AT_KW), lambda i: (i, kcol)),
                  pl.BlockSpec((tm, AT_KW), lambda i: (i, kcol + 1)),
                  pl.BlockSpec((tm, 128), lambda i: (tab(i), 0)),
                  pl.BlockSpec((tm, 128), lambda i: (tab(i), 0)),
                  pl.BlockSpec((1, AT_Q), lambda i: (0, 0)),
                  pl.BlockSpec((1, AT_KW), lambda i: (0, 0)),
                  pl.BlockSpec((AT_Q, 128), lambda i: (0, 0)),
                  pl.BlockSpec((128, AT_Q), lambda i: (0, 0))],
        out_specs=(pl.BlockSpec((AT_Q, tm), lambda i: (0, i)), pl.BlockSpec((tm, AT_KW), lambda i: (kvb(i), 0)),
                   pl.BlockSpec((AT_KW, tm), lambda i: (0, kvb(i)))),
        compiler_params=_cp(("parallel",)), name="attn_prepare",
    )(p, p, p, cos_t, sin_t, jnp.tile(q_norm, AT_Q // AT_HD).reshape(1, AT_Q),
      jnp.tile(k_norm, AT_KV).reshape(1, AT_KW), e, et)


AT_REBASE = 64.0
AT_SEED = 128


def _flash_t_kernel(qt_ref, k_ref, vt_ref, o_ref, m_sc, l_sc, acc_sc, p_sc):
    ki = pl.program_id(2)
    nq = AT_Q // AT_HD
    gq = nq // AT_KV

    @pl.when(ki == 0)
    def _():
        for g in range(AT_KV):
            k0 = k_ref[0:AT_SEED, g * AT_HD:(g + 1) * AT_HD]
            for h in range(g * gq, (g + 1) * gq):
                m_sc[h] = jnp.max(_dot(k0, qt_ref[h * AT_HD:(h + 1) * AT_HD, :]), axis=0, keepdims=True)
        l_sc[...] = jnp.zeros_like(l_sc)
        acc_sc[...] = jnp.zeros_like(acc_sc)

    def k_group(g):
        return k_ref[:, g * AT_HD:(g + 1) * AT_HD]

    def v_group(g):
        return jnp.concatenate([vt_ref[g * AT_HD:(g + 1) * AT_HD, :],
                                jnp.ones((16, vt_ref.shape[1]), BF16)], axis=0)

    def scores(g):
        kg = k_group(g)
        return [_dot(kg, qt_ref[h * AT_HD:(h + 1) * AT_HD, :]) for h in range(g * gq, (g + 1) * gq)]

    gap = None
    for g in range(AT_KV):
        st = scores(g)
        m_cur = [m_sc[h] for h in range(g * gq, (g + 1) * gq)]
        for i in range(gq):
            over_i = jnp.max(st[i], axis=0, keepdims=True) - m_cur[i]
            gap = over_i if gap is None else jnp.maximum(gap, over_i)
            p_sc[g * gq + i] = jnp.exp2(st[i] - m_cur[i]).astype(BF16)
    rebase = jnp.max(gap) > AT_REBASE

    @pl.when(jnp.logical_not(rebase))
    def _():
        for g in range(AT_KV):
            vtg1 = v_group(g)
            hs = range(g * gq, (g + 1) * gq)
            pv = [_dot(vtg1, p_sc[h]) for h in hs]
            for i, h in enumerate(hs):
                l_sc[h] = l_sc[h] + pv[i][AT_HD:AT_HD + 1, :]
                rows = pl.ds(h * AT_HD, AT_HD)
                acc_sc[rows, :] = acc_sc[rows, :] + pv[i][0:AT_HD, :]

    @pl.when(rebase)
    def _():
        for g in range(AT_KV):
            vtg1 = v_group(g)
            hs = range(g * gq, (g + 1) * gq)
            st = scores(g)
            m_old = [m_sc[h] for h in hs]
            m_new = [jnp.maximum(m_old[i], jnp.max(st[i], axis=0, keepdims=True)) for i in range(gq)]
            alpha = [jnp.exp2(m_old[i] - m_new[i]) for i in range(gq)]
            pt = [jnp.exp2(st[i] - m_new[i]).astype(BF16) for i in range(gq)]
            pv = [_dot(vtg1, pt[i]) for i in range(gq)]
            for i, h in enumerate(hs):
                l_sc[h] = alpha[i] * l_sc[h] + pv[i][AT_HD:AT_HD + 1, :]
                m_sc[h] = m_new[i]
                rows = pl.ds(h * AT_HD, AT_HD)
                acc_sc[rows, :] = alpha[i] * acc_sc[rows, :] + pv[i][0:AT_HD, :]

    @pl.when(ki == pl.num_programs(2) - 1)
    def _():
        inv = jnp.concatenate([jnp.broadcast_to(1.0 / l_sc[h], (AT_HD, l_sc.shape[2])) for h in range(nq)], axis=0)
        o_ref[...] = jnp.transpose(acc_sc[...] * inv)


def flash_attention_t(qt, k, vt):
    nq = AT_Q // AT_HD
    nk = (CTX + SEQ) // AT_TK
    kv_rb = lambda b, ki: b * nk + ki
    return pl.pallas_call(
        _flash_t_kernel,
        out_shape=jax.ShapeDtypeStruct((N_LAT, AT_Q), F32),
        grid=(BATCH, SEQ // AT_TQ, nk),
        in_specs=[pl.BlockSpec((AT_Q, AT_TQ), lambda b, qi, ki: (0, b * (SEQ // AT_TQ) + qi)),
                  pl.BlockSpec((AT_TK, AT_KW), lambda b, qi, ki: (kv_rb(b, ki), 0)),
                  pl.BlockSpec((AT_KW, AT_TK), lambda b, qi, ki: (0, kv_rb(b, ki)))],
        out_specs=pl.BlockSpec((AT_TQ, AT_Q), lambda b, qi, ki: (b * (SEQ // AT_TQ) + qi, 0)),
        scratch_shapes=[pltpu.VMEM((nq, 1, AT_TQ), F32), pltpu.VMEM((nq, 1, AT_TQ), F32),
                        pltpu.VMEM((AT_Q, AT_TQ), F32), pltpu.VMEM((nq, AT_TK, AT_TQ), BF16)],
        compiler_params=_cp(("parallel", "parallel", "arbitrary")), name="flash_attention")(qt, k, vt)


OD_PAD_N = 4992


def odd_mixer(x, mods, g_pre, g_post, op):
    W = RW_H * RW_N
    w = op['w_in']
    c3 = 3 * W
    code_n = c3 + 64 + 64 + 160
    w_perm = jnp.concatenate([w[:, :c3], w[:, code_n:], w[:, c3:code_n],
                              jnp.zeros((D, OD_PAD_N - w.shape[1]), F32)], axis=1).astype(BF16)
    p = norm_mod_matmul(x, g_pre, mods, w_perm, 0, 1, tn=1664, name="odd_in_proj")
    mu = op['mu']
    taps = lambda m: jnp.stack([0.5 * m, 1.0 - m, 0.5 * m], axis=1)
    code = dwconv3(p, 0, c3, taps(mu[:c3]), jnp.zeros((c3,), F32), False, "rwkv_shift")
    lo_col = c3 + AT_Q + 2 * AT_KW
    mu_lo = jnp.pad(mu[c3:], (0, 384 - (code_n - c3)))
    lora = dwconv3(p, lo_col, 384, taps(mu_lo), jnp.zeros((384,), F32), False, "rwkv_shift_lora", cb=128)
    lw, kd, be, kap, g, bonus = rwkv_prepare(code, lora, op)
    y2 = rwkv_scan(code, lw, kd, be, kap)
    o_l = rwkv_output(y2, bonus, g, op)
    q, k, v = attention_prepare(p, op['q_norm'], op['k_norm'])
    a_l = flash_attention_t(q, k, v)
    return outproj_residual(o_l, a_l, op['w_out'].astype(BF16), x, g_post, mods, 2, name="odd_out_proj")


MOE_T = 256
MOE_CAP = 64
MOE_EPS = 4


def _router2_kernel(x_ref, g_ref, mod_ref, rw_ref, rb_ref, s1_ref, s3_ref, s2_ref, t_ref, wt_ref, cnt_ref, sh_ref):
    x = x_ref[...]
    ms = jnp.mean(x * x, axis=-1, keepdims=True)
    t = x * lax.rsqrt(ms + EPS) * g_ref[...] * (1.0 + mod_ref[4:5, :]) + mod_ref[3:4, :]
    tb = t.astype(BF16)
    t_ref[...] = tb
    sh_ref[...] = _dot((_silu(_dot(tb, s1_ref[...])) * _dot(tb, s3_ref[...])).astype(BF16), s2_ref[...])
    th, tl = _split(t)
    wh, wl = _split(rw_ref[...])
    lg = _dot_nt(wh, th) + (_dot_nt(wh, tl) + _dot_nt(wl, th))
    sc = _sigmoid(lg)
    sel = sc + rb_ref[...]
    tm = sel.shape[1]
    gsz = N_EXP // N_GRP
    ninf = -jnp.inf
    sel3 = sel.reshape(N_GRP, gsz, tm)
    i3 = lax.broadcasted_iota(jnp.int32, sel3.shape, 1)
    m1 = jnp.max(sel3, axis=1, keepdims=True)
    first = jnp.min(jnp.where(sel3 == m1, i3, gsz), axis=1, keepdims=True)
    m2 = jnp.max(jnp.where(i3 == first, ninf, sel3), axis=1, keepdims=True)
    grp = (m1 + m2).reshape(N_GRP, tm)
    gi = lax.broadcasted_iota(jnp.int32, grp.shape, 0)
    gmask = jnp.zeros(grp.shape, F32)
    for _ in range(TOPK_GRP):
        m = jnp.max(grp, axis=0, keepdims=True)
        pick = jnp.min(jnp.where(grp == m, gi, N_GRP), axis=0, keepdims=True)
        hit = gi == pick
        gmask = jnp.where(hit, 1.0, gmask)
        grp = jnp.where(hit, ninf, grp)
    emask = jnp.broadcast_to(gmask.reshape(N_GRP, 1, tm), (N_GRP, gsz, tm)).reshape(N_EXP, tm)
    msel = jnp.where(emask > 0.5, sel, ninf)
    ei = lax.broadcasted_iota(jnp.int32, msel.shape, 0)
    chosen = jnp.zeros(msel.shape, F32)
    for _ in range(TOP_K):
        m = jnp.max(msel, axis=0, keepdims=True)
        pick = jnp.min(jnp.where(msel == m, ei, N_EXP), axis=0, keepdims=True)
        hit = ei == pick
        chosen = jnp.where(hit, 1.0, chosen)
        msel = jnp.where(hit, ninf, msel)
    w = chosen * sc
    wt = w / jnp.sum(w, axis=0, keepdims=True) * ROUTED_SCALE
    wt_ref[...] = wt
    cnt_ref[...] = jnp.sum((wt > 0.0).astype(F32), axis=1, keepdims=True).astype(jnp.int32)


def moe_router2(x, M, g, mods, mp):
    tm = MOE_T
    full = lambda shape: pl.BlockSpec(shape, lambda i: (0,) * len(shape))
    return pl.pallas_call(
        _router2_kernel,
        out_shape=(jax.ShapeDtypeStruct((M, D), BF16), jax.ShapeDtypeStruct((M // tm, N_EXP, tm), F32),
                   jax.ShapeDtypeStruct((M // tm, N_EXP, 1), jnp.int32), jax.ShapeDtypeStruct((M, D), F32)),
        grid=(M // tm,),
        in_specs=[pl.BlockSpec((tm, D), lambda i: (i, 0)), full((1, D)),
                  pl.BlockSpec((None, 6, D), lambda i: (_seq_of_rowblock(i, tm), 0, 0)),
                  full((N_EXP, D)), full((N_EXP, 1)), full((D, EXP_FF)), full((D, EXP_FF)), full((EXP_FF, D))],
        out_specs=(pl.BlockSpec((tm, D), lambda i: (i, 0)), pl.BlockSpec((None, N_EXP, tm), lambda i: (i, 0, 0)),
                   pl.BlockSpec((None, N_EXP, 1), lambda i: (i, 0, 0)), pl.BlockSpec((tm, D), lambda i: (i, 0))),
        compiler_params=_cp(("parallel",)), name="moe_router",
    )(x, g.reshape(1, D), mods, jnp.transpose(mp['router_w']), mp['router_bias'].reshape(N_EXP, 1),
      mp['s1'].astype(BF16), mp['s3'].astype(BF16), mp['s2'].astype(BF16))


def _moe3_kernel(cnt_ref, ovf_ref, t_ref, wt_ref, w1_ref, w3_ref, w2_ref, o_ref, rank_sc, *, nsub):
    i = pl.program_id(0)
    eb = pl.program_id(1)
    T, CAP, EPS = MOE_T, MOE_CAP, MOE_EPS

    @pl.when(eb == 0)
    def _():
        before = (lax.broadcasted_iota(jnp.int32, (T, T), 0) < lax.broadcasted_iota(jnp.int32, (T, T), 1))
        before = before.astype(BF16)
        for s in range(nsub):
            rank_sc[s] = _dot((wt_ref[s] > 0.0).astype(BF16), before)
        o_ref[...] = jnp.zeros_like(o_ref)

    slot = lax.broadcasted_iota(jnp.int32, (CAP, T), 0).astype(F32)

    def one_hot(s, e, first_slot):
        w_row = wt_ref[s, pl.ds(e, 1), :]
        r_row = rank_sc[s, pl.ds(e, 1), :]
        hit = ((r_row - first_slot) == slot) & (w_row > 0.0)
        w_slot = jnp.sum(jnp.where(hit, w_row, 0.0), axis=1, keepdims=True)
        return hit.astype(F32).astype(BF16), w_slot

    def swiglu(xg, j):
        h = _silu(_dot(xg, w1_ref[j])) * _dot(xg, w3_ref[j])
        return _dot(h.astype(BF16), w2_ref[j])

    hot = [[one_hot(s, eb * EPS + j, 0.0) for j in range(EPS)] for s in range(nsub)]
    pb = [jnp.concatenate([hot[s][j][0] for j in range(EPS)], axis=0) for s in range(nsub)]
    xg = [_dot(pb[s], t_ref[s * T:(s + 1) * T, :]).astype(BF16) for s in range(nsub)]
    y = [swiglu(jnp.concatenate([xg[s][j * CAP:(j + 1) * CAP] for s in range(nsub)], axis=0), j)
         for j in range(EPS)]
    for s in range(nsub):
        yw = jnp.concatenate([y[j][s * CAP:(s + 1) * CAP] * hot[s][j][1] for j in range(EPS)], axis=0)
        o_ref[s * T:(s + 1) * T, :] += lax.dot_general(pb[s], yw.astype(BF16), (((0,), (0,)), ((), ())),
                                                       preferred_element_type=F32)

    def pair(idx, carry):
        s = idx // EPS
        j = idx % EPS
        e = eb * EPS + j
        n_tok = cnt_ref[(i * nsub + s) * N_EXP + e]
        rows = pl.ds(pl.multiple_of(s * T, T), T)

        def chunk(ci, c2):
            p1, w_slot = one_hot(s, e, (ci * CAP).astype(F32))
            yw = (swiglu(_dot(p1, t_ref[rows, :]).astype(BF16), j) * w_slot).astype(BF16)
            o_ref[rows, :] += lax.dot_general(p1, yw, (((0,), (0,)), ((), ())), preferred_element_type=F32)
            return c2

        lax.fori_loop(1, (n_tok + CAP - 1) // CAP, chunk, 0)
        return carry

    @pl.when(ovf_ref[i * (N_EXP // EPS) + eb] > 0)
    def _():
        lax.fori_loop(0, nsub * EPS, pair, 0)


def _moe_out_kernel(r_ref, sh_ref, x_ref, g_ref, mod_ref, o_ref):
    f = r_ref[...] + sh_ref[...]
    ms = jnp.mean(f * f, axis=-1, keepdims=True)
    o_ref[...] = x_ref[...] + mod_ref[5:6, :] * (f * lax.rsqrt(ms + EPS) * g_ref[...])


def moe_layer3(x, M, g_pre, g_post, mods, mp, nsub):
    t, wt, cnt, sh = moe_router2(x, M, g_pre, mods, mp)
    T = MOE_T
    TS = nsub * T
    ovf = jnp.any(cnt.reshape(M // TS, nsub, N_EXP // MOE_EPS, MOE_EPS) > MOE_CAP, axis=(1, 3)).astype(jnp.int32)
    grid_spec = pltpu.PrefetchScalarGridSpec(
        num_scalar_prefetch=2, grid=(M // TS, N_EXP // MOE_EPS),
        in_specs=[pl.BlockSpec((TS, D), lambda i, e, c, o: (i, 0)),
                  pl.BlockSpec((nsub, N_EXP, T), lambda i, e, c, o: (i, 0, 0)),
                  pl.BlockSpec((MOE_EPS, D, EXP_FF), lambda i, e, c, o: (e, 0, 0)),
                  pl.BlockSpec((MOE_EPS, D, EXP_FF), lambda i, e, c, o: (e, 0, 0)),
                  pl.BlockSpec((MOE_EPS, EXP_FF, D), lambda i, e, c, o: (e, 0, 0))],
        out_specs=pl.BlockSpec((TS, D), lambda i, e, c, o: (i, 0)),
        scratch_shapes=[pltpu.VMEM((nsub, N_EXP, T), F32)])
    routed = pl.pallas_call(
        functools.partial(_moe3_kernel, nsub=nsub), out_shape=jax.ShapeDtypeStruct((M, D), F32),
        grid_spec=grid_spec, compiler_params=_cp(("parallel", "arbitrary")), name="moe_experts",
    )(cnt.reshape(-1), ovf.reshape(-1), t, wt, mp['w1'].astype(BF16), mp['w3'].astype(BF16),
      mp['w2'].astype(BF16))
    tm = 512
    return pl.pallas_call(
        _moe_out_kernel, out_shape=jax.ShapeDtypeStruct((M, D), F32), grid=(M // tm,),
        in_specs=[pl.BlockSpec((tm, D), lambda i: (i, 0)), pl.BlockSpec((tm, D), lambda i: (i, 0)),
                  pl.BlockSpec((tm, D), lambda i: (i, 0)), pl.BlockSpec((1, D), lambda i: (0, 0)),
                  pl.BlockSpec((None, 6, D), lambda i: (_seq_of_rowblock(i, tm), 0, 0))],
        out_specs=pl.BlockSpec((tm, D), lambda i: (i, 0)),
        compiler_params=_cp(("parallel",)), name="moe_output")(routed, sh, x, g_post.reshape(1, D), mods)


def kernel(x, c, ctx, c_ctx, mod_w, mod_b, norm_mix_pre, norm_mix_post, norm_ffn_pre, norm_ffn_post, router_w, router_bias, expert_w1, expert_w3, expert_w2, shared_w1, shared_w3, shared_w2, ev_w_in, ev_w_out, ssd_conv_w, ssd_conv_b, ssd_dt_bias, ssd_a_log, ssd_d, ssd_norm_w, hy_conv_w, hy_conv_b, hy_mlp_w0, hy_mlp_b0, hy_freq0, hy_mlp_w1, hy_mlp_b1, hy_freq1, hy_mlp_w2, hy_bias, od_w_in, od_w_out, rw_mu, rw_w0, rw_w_up, rw_a0, rw_a_up, rw_g_up, rw_k_k, rw_k_a, rw_r_k, rw_ln_w, rw_ln_b, at_q_norm, at_k_norm):
    xs = jnp.concatenate([x.reshape(N_LAT, D), ctx.reshape(BATCH * CTX, D)], axis=0)
    cvecs = jnp.zeros((8, D), F32).at[0:BATCH].set(c).at[BATCH].set(c_ctx)
    assert mod_w.shape[0] == 2, "one even (SSD | Hyena) layer followed by one odd (RWKV | attention) layer"

    def moe_params(i):
        return dict(router_w=router_w[i], router_bias=router_bias[i], w1=expert_w1[i], w3=expert_w3[i],
                    w2=expert_w2[i], s1=shared_w1[i], s3=shared_w3[i], s2=shared_w2[i])

    mods = modulation(cvecs, mod_w[0], mod_b[0])[:BATCH + 1].reshape(BATCH + 1, 6, D)
    ep = dict(w_in=ev_w_in[0], w_out=ev_w_out[0], ssd_conv_w=ssd_conv_w[0], ssd_conv_b=ssd_conv_b[0],
              ssd_dt_bias=ssd_dt_bias[0], ssd_a_log=ssd_a_log[0], ssd_d=ssd_d[0], ssd_norm_w=ssd_norm_w[0],
              hy_conv_w=hy_conv_w[0], hy_conv_b=hy_conv_b[0], hy_mlp_w0=hy_mlp_w0[0], hy_mlp_b0=hy_mlp_b0[0],
              hy_freq0=hy_freq0[0], hy_mlp_w1=hy_mlp_w1[0], hy_mlp_b1=hy_mlp_b1[0], hy_freq1=hy_freq1[0],
              hy_mlp_w2=hy_mlp_w2[0], hy_bias=hy_bias[0])
    xs = even_mixer(xs, mods, norm_mix_pre[0], norm_mix_post[0], ep)
    xs = moe_layer3(xs, N_ROWS, norm_ffn_pre[0], norm_ffn_post[0], mods, moe_params(0), 6)
    mods = modulation(cvecs, mod_w[1], mod_b[1])[:BATCH + 1].reshape(BATCH + 1, 6, D)
    op = dict(w_in=od_w_in[0], w_out=od_w_out[0], mu=rw_mu[0], w0=rw_w0[0], w_up=rw_w_up[0], a0=rw_a0[0],
              a_up=rw_a_up[0], g_up=rw_g_up[0], k_k=rw_k_k[0], k_a=rw_k_a[0], r_k=rw_r_k[0], ln_w=rw_ln_w[0],
              ln_b=rw_ln_b[0], q_norm=at_q_norm[0], k_norm=at_k_norm[0])
    xl = odd_mixer(xs, mods, norm_mix_pre[1], norm_mix_post[1], op)
    xl = moe_layer3(xl, N_LAT, norm_ffn_pre[1], norm_ffn_post[1], mods, moe_params(1), 8)
    return xl.reshape(BATCH, SEQ, D)
```

```python
import functools
import math

import jax
import jax.numpy as jnp
from jax import lax
from jax.experimental import pallas as pl
from jax.experimental.pallas import tpu as pltpu

F32 = jnp.float32
BF16 = jnp.bfloat16

D = 1024
BATCH = 2
SEQ = 8192
CTX = 256
N_LAT = BATCH * SEQ
N_ROWS = N_LAT + BATCH * CTX
EPS = 1e-6
GRID_W = 64

SSD_HEADS = 16
SSD_P = 64
SSD_G = 2
SSD_S = 128
SSD_Q = 128
HY_W = 1024
HY_EMB = 33
HY_HID = 64

RW_H = 16
RW_N = 64
RW_CHUNK = 128
RW_GN_EPS = 64e-5

AT_KV = 4
AT_HD = 64

N_EXP = 64
TOP_K = 8
N_GRP = 8
TOPK_GRP = 4
EXP_FF = 256
ROUTED_SCALE = 2.5

VMEM_LIMIT = 56 * 1024 * 1024


def _cp(sem, vmem=None):
    return pltpu.CompilerParams(dimension_semantics=sem, vmem_limit_bytes=vmem or VMEM_LIMIT)


def _dot(a, b):
    return jnp.dot(a, b, preferred_element_type=F32)


def _dot_nt(a, b):
    return lax.dot_general(a, b, (((1,), (1,)), ((), ())), preferred_element_type=F32)


def _split(x):
    hi = x.astype(BF16)
    lo = (x - hi.astype(F32)).astype(BF16)
    return hi, lo


def _dot3(a, b):
    ah, al = _split(a)
    bh, bl = _split(b)
    return _dot(ah, bh) + (_dot(ah, bl) + _dot(al, bh))


def _dot2l(a, b):
    ah, al = _split(a)
    return _dot(ah, b) + _dot(al, b)


def _dot2r(a, b):
    bh, bl = _split(b)
    return _dot(a, bh) + _dot(a, bl)


def _silu(x):
    return x * (1.0 / (1.0 + jnp.exp(-x)))


def _sigmoid(x):
    return 1.0 / (1.0 + jnp.exp(-x))


def _softplus(x):
    return jnp.maximum(x, 0.0) + jnp.log(1.0 + jnp.exp(-jnp.abs(x)))


def _seq_of_rowblock(i, tm):
    return jnp.minimum((i * tm) // SEQ, 2)


def _nmm_kernel(x_ref, g_ref, mod_ref, w_ref, o_ref, a_sc, *, shift_i, scale_i):
    @pl.when(pl.program_id(1) == 0)
    def _():
        x = x_ref[...]
        ms = jnp.mean(x * x, axis=-1, keepdims=True)
        y = x * lax.rsqrt(ms + EPS) * g_ref[...]
        h = y * (1.0 + mod_ref[scale_i:scale_i + 1, :]) + mod_ref[shift_i:shift_i + 1, :]
        a_sc[...] = h.astype(BF16)

    o_ref[...] = _dot(a_sc[...], w_ref[...])


def norm_mod_matmul(x, g, mods, w, shift_i, scale_i, tm=512, tn=None, name="nmm"):
    M = x.shape[0]
    N = w.shape[1]
    tn = tn or N
    return pl.pallas_call(
        functools.partial(_nmm_kernel, shift_i=shift_i, scale_i=scale_i),
        out_shape=jax.ShapeDtypeStruct((M, N), F32),
        grid=(M // tm, N // tn),
        in_specs=[pl.BlockSpec((tm, D), lambda i, j: (i, 0)),
                  pl.BlockSpec((1, D), lambda i, j: (0, 0)),
                  pl.BlockSpec((None, 6, D), lambda i, j: (_seq_of_rowblock(i, tm), 0, 0)),
                  pl.BlockSpec((D, tn), lambda i, j: (0, j))],
        out_specs=pl.BlockSpec((tm, tn), lambda i, j: (i, j)),
        scratch_shapes=[pltpu.VMEM((tm, D), BF16)],
        compiler_params=_cp(("parallel", "arbitrary")), name=name)(x, g.reshape(1, D), mods, w)


def _outproj_kernel(a1_ref, a2_ref, w_ref, x_ref, g_ref, mod_ref, o_ref, *, gate_i):
    y = _dot(a1_ref[...].astype(BF16), w_ref[0:D, :]) + _dot(a2_ref[...].astype(BF16), w_ref[D:2 * D, :])
    ms = jnp.mean(y * y, axis=-1, keepdims=True)
    o_ref[...] = x_ref[...] + mod_ref[gate_i:gate_i + 1, :] * (y * lax.rsqrt(ms + EPS) * g_ref[...])


def outproj_residual(a1, a2, w, x, g, mods, gate_i, tm=512, name="outproj"):
    M = a1.shape[0]
    return pl.pallas_call(
        functools.partial(_outproj_kernel, gate_i=gate_i),
        out_shape=jax.ShapeDtypeStruct((M, D), F32),
        grid=(M // tm,),
        in_specs=[pl.BlockSpec((tm, D), lambda i: (i, 0)),
                  pl.BlockSpec((tm, D), lambda i: (i, 0)),
                  pl.BlockSpec((2 * D, D), lambda i: (0, 0)),
                  pl.BlockSpec((tm, D), lambda i: (i, 0)),
                  pl.BlockSpec((1, D), lambda i: (0, 0)),
                  pl.BlockSpec((None, 6, D), lambda i: (_seq_of_rowblock(i, tm), 0, 0))],
        out_specs=pl.BlockSpec((tm, D), lambda i: (i, 0)),
        compiler_params=_cp(("parallel",)), name=name)(a1, a2, w, x, g.reshape(1, D), mods)


def _mod_kernel(c_ref, w_ref, b_ref, o_ref):
    o_ref[...] = _dot3(_silu(c_ref[...]), w_ref[...]) + b_ref[...]


def modulation(cvecs, w, b):
    N = w.shape[1]
    tn = 1024
    return pl.pallas_call(
        _mod_kernel, out_shape=jax.ShapeDtypeStruct((8, N), F32), grid=(N // tn,),
        in_specs=[pl.BlockSpec((8, D), lambda j: (0, 0)),
                  pl.BlockSpec((D, tn), lambda j: (0, j)),
                  pl.BlockSpec((1, tn), lambda j: (0, j))],
        out_specs=pl.BlockSpec((8, tn), lambda j: (0, j)),
        compiler_params=_cp(("parallel",)), name="modulation")(cvecs, w, b.reshape(1, N))


CONV_TM = 256


def _conv3_kernel(x_ref, prev_ref, next_ref, w_ref, b_ref, o_ref, *, act):
    tm = CONV_TM
    row0 = pl.program_id(0) * tm
    seq_len = jnp.where(row0 < N_LAT, SEQ, CTX)
    pos = jnp.where(row0 < N_LAT, row0 % SEQ, (row0 - N_LAT) % CTX)
    cur = x_ref[...]
    rows = lax.broadcasted_iota(jnp.int32, cur.shape, 0)
    prev_row = prev_ref[7:8, :] * (pos > 0).astype(F32)
    next_row = next_ref[0:1, :] * (pos + tm < seq_len).astype(F32)
    xm1 = jnp.where(rows == 0, prev_row, pltpu.roll(cur, 1, 0))
    xp1 = jnp.where(rows == tm - 1, next_row, pltpu.roll(cur, tm - 1, 0))
    y = xm1 * w_ref[0:1, :] + cur * w_ref[1:2, :] + xp1 * w_ref[2:3, :] + b_ref[...]
    o_ref[...] = _silu(y) if act else y


def dwconv3(p, col0, ncols, w, b, act, name, cb=1024):
    tm = CONV_TM
    cb = math.gcd(cb, math.gcd(col0, ncols)) if col0 else math.gcd(cb, ncols)
    r8 = tm // 8
    n8 = N_ROWS // 8
    c0 = col0 // cb
    return pl.pallas_call(
        functools.partial(_conv3_kernel, act=act),
        out_shape=jax.ShapeDtypeStruct((N_ROWS, ncols), F32),
        grid=(N_ROWS // tm, ncols // cb),
        in_specs=[pl.BlockSpec((tm, cb), lambda i, j: (i, c0 + j)),
                  pl.BlockSpec((8, cb), lambda i, j: (jnp.maximum(i * r8 - 1, 0), c0 + j)),
                  pl.BlockSpec((8, cb), lambda i, j: (jnp.minimum((i + 1) * r8, n8 - 1), c0 + j)),
                  pl.BlockSpec((3, cb), lambda i, j: (0, j)),
                  pl.BlockSpec((1, cb), lambda i, j: (0, j))],
        out_specs=pl.BlockSpec((tm, cb), lambda i, j: (i, j)),
        compiler_params=_cp(("parallel", "parallel")), name=name)(p, p, p, jnp.transpose(w), b.reshape(1, ncols))


def _ssd_kernel(xs_ref, bm_ref, cm_ref, dt_ref, dtT_ref, bias_ref, biasT_ref, alog_ref, alogT_ref,
                y_ref, st_ref):
    d = pl.program_id(0)
    c = pl.program_id(2)
    Q = SSD_Q
    HG = SSD_HEADS // SSD_G

    @pl.when(c == 0)
    def _():
        st_ref[...] = jnp.zeros_like(st_ref)

    isb = d == 1
    sgn = 1 - 2 * d
    ii = lax.broadcasted_iota(jnp.int32, (Q, Q), 0)
    jj = lax.broadcasted_iota(jnp.int32, (Q, Q), 1)
    tri = (jj <= ii).astype(BF16)
    triT = (ii <= jj).astype(BF16)
    mask = sgn * (ii - jj) >= 0
    xs = xs_ref[...]
    G = range(SSD_G)
    dt = [_softplus(dt_ref[g] + bias_ref[g]) for g in G]
    dtT = [_softplus(dtT_ref[g] + biasT_ref[g]) for g in G]
    a = [dt[g] * (-jnp.exp(alog_ref[g])) for g in G]
    aT = [dtT[g] * (-jnp.exp(alogT_ref[g])) for g in G]
    cs = [_dot2r(tri, a[g]) for g in G]
    csT = [_dot2l(aT[g], triT) for g in G]
    tot = [cs[g][Q - 1:Q, :] for g in G]
    p = [jnp.where(isb, a[g] - cs[g], cs[g]) for g in G]
    pT = [jnp.where(isb, aT[g] - csT[g], csT[g]) for g in G]
    dec_out = [jnp.exp(jnp.where(isb, tot[g], 0.0) + p[g]) for g in G]
    dec_state = [jnp.exp(jnp.where(isb, 0.0, tot[g]) - p[g]) for g in G]
    chunk_dec = [jnp.exp(tot[g]) for g in G]
    bm = [bm_ref[:, g * SSD_S:(g + 1) * SSD_S].astype(BF16) for g in G]
    cm = [cm_ref[:, g * SSD_S:(g + 1) * SSD_S].astype(BF16) for g in G]
    cb = [_dot_nt(cm[g], bm[g]) for g in G]
    nh = SSD_HEADS

    def spread(cols, width):
        v = jnp.concatenate(cols, axis=1)
        head = lax.broadcasted_iota(jnp.int32, (nh, nh * width), 1) // width
        e = (head == lax.broadcasted_iota(jnp.int32, (nh, nh * width), 0)).astype(BF16)
        h1 = v.astype(BF16)
        r1 = v - h1.astype(F32)
        h2 = r1.astype(BF16)
        h3 = (r1 - h2.astype(F32)).astype(BF16)
        return _dot(h1, e) + (_dot(h2, e) + _dot(h3, e))

    dt_x = spread(dt, SSD_P)
    dout_x = spread(dec_out, SSD_P)
    dst_x = spread(dec_state, SSD_P)
    p_x = spread(p, Q)
    xh_all = xs * dt_x
    xdec_all = (xh_all * dst_x).astype(BF16)
    xh_all = xh_all.astype(BF16)
    GH = [(g, h) for g in G for h in range(HG)]
    NH = range(len(GH))
    lm = [(cb[g] * jnp.exp(jnp.where(mask, p_x[:, n * Q:(n + 1) * Q] - pT[g][h:h + 1, :], -1e30))).astype(BF16)
          for n, (g, h) in enumerate(GH)]
    s_old = [st_ref[n] for n in NH]
    y_in = [_dot(lm[n], xh_all[:, n * SSD_P:(n + 1) * SSD_P]) for n in NH]
    y_st = [_dot(cm[g], s_old[n].astype(BF16)) for n, (g, h) in enumerate(GH)]
    upd = [lax.dot_general(bm[g], xdec_all[:, n * SSD_P:(n + 1) * SSD_P], (((0,), (0,)), ((), ())),
                           preferred_element_type=F32) for n, (g, h) in enumerate(GH)]
    for n, (g, h) in enumerate(GH):
        st_ref[n] = chunk_dec[g][:, h:h + 1] * s_old[n] + upd[n]
    y_ref[...] = jnp.concatenate(y_in, axis=1) + dout_x * jnp.concatenate(y_st, axis=1)


def _ssd_rowblock(d, b, c):
    n_ctx = CTX // SSD_Q
    n_lat = SEQ // SSD_Q
    cc = jnp.where(d == 0, c, n_ctx - 1 - c)
    lc = jnp.where(d == 0, c - n_ctx, n_ctx + n_lat - 1 - c)
    return jnp.where(c < n_ctx, N_LAT // SSD_Q + b * n_ctx + cc, b * n_lat + lc)


def ssd_scan(xbc, dt_raw, dt_bias, a_log):
    HG = SSD_HEADS // SSD_G
    W = SSD_HEADS * SSD_P
    dsel = dt_raw[:, :2 * SSD_HEADS].reshape(N_ROWS, 2, SSD_G, HG).transpose(1, 2, 0, 3)
    dselT = dsel.transpose(0, 1, 3, 2)
    bias = dt_bias.reshape(2, SSD_G, 1, HG)
    biasT = dt_bias.reshape(2, SSD_G, HG, 1)
    alog = a_log.reshape(2, SSD_G, 1, HG)
    alogT = a_log.reshape(2, SSD_G, HG, 1)
    nch = (CTX + SEQ) // SSD_Q
    rb = _ssd_rowblock
    GS = SSD_G * SSD_S
    par = lambda shape: pl.BlockSpec((None,) + shape, lambda d, b, c: (d, 0, 0, 0))
    return pl.pallas_call(
        _ssd_kernel,
        out_shape=jax.ShapeDtypeStruct((2, N_ROWS, W), F32),
        grid=(2, BATCH, nch),
        in_specs=[pl.BlockSpec((SSD_Q, W), lambda d, b, c: (rb(d, b, c), 0)),
                  pl.BlockSpec((SSD_Q, GS), lambda d, b, c: (rb(d, b, c), W // GS)),
                  pl.BlockSpec((SSD_Q, GS), lambda d, b, c: (rb(d, b, c), W // GS + 1)),
                  pl.BlockSpec((None, SSD_G, SSD_Q, HG), lambda d, b, c: (d, 0, rb(d, b, c), 0)),
                  pl.BlockSpec((None, SSD_G, HG, SSD_Q), lambda d, b, c: (d, 0, 0, rb(d, b, c))),
                  par((SSD_G, 1, HG)), par((SSD_G, HG, 1)), par((SSD_G, 1, HG)), par((SSD_G, HG, 1))],
        out_specs=pl.BlockSpec((None, SSD_Q, W), lambda d, b, c: (d, rb(d, b, c), 0)),
        scratch_shapes=[pltpu.VMEM((SSD_HEADS, SSD_S, SSD_P), F32)],
        compiler_params=_cp(("parallel", "parallel", "arbitrary")), name="ssd_scan",
    )(xbc, xbc, xbc, dsel, dselT, bias, biasT, alog, alogT)


def _ssd_out_kernel(yf_ref, yb_ref, xs_ref, z_ref, dskip_ref, nw_ref, o_ref):
    y = yf_ref[...] + yb_ref[...] + xs_ref[...] * dskip_ref[...]
    y = y * _silu(z_ref[...])
    gs = SSD_HEADS * SSD_P // SSD_G
    parts = []
    for g in range(SSD_G):
        yg = y[:, g * gs:(g + 1) * gs]
        parts.append(yg * lax.rsqrt(jnp.mean(yg * yg, axis=-1, keepdims=True) + EPS))
    o_ref[...] = jnp.concatenate(parts, axis=1) * nw_ref[...]


def ssd_output(y2, xbc, p, zcol, d_skip, norm_w, tm=512):
    W = SSD_HEADS * SSD_P
    dexp = jnp.repeat(d_skip, SSD_P).reshape(1, W)
    return pl.pallas_call(
        _ssd_out_kernel, out_shape=jax.ShapeDtypeStruct((N_ROWS, W), F32), grid=(N_ROWS // tm,),
        in_specs=[pl.BlockSpec((None, tm, W), lambda i: (0, i, 0)),
                  pl.BlockSpec((None, tm, W), lambda i: (1, i, 0)),
                  pl.BlockSpec((tm, W), lambda i: (i, 0)),
                  pl.BlockSpec((tm, W), lambda i: (i, zcol // W)),
                  pl.BlockSpec((1, W), lambda i: (0, 0)),
                  pl.BlockSpec((1, W), lambda i: (0, 0))],
        out_specs=pl.BlockSpec((tm, W), lambda i: (i, 0)),
        compiler_params=_cp(("parallel",)), name="ssd_output")(y2, y2, xbc, p, dexp, norm_w.reshape(1, W))


def _hyfilt_kernel(f_ref, w0_ref, b0_ref, fr0_ref, w1_ref, b1_ref, fr1_ref, w2_ref, dl_ref, h_ref, ss_ref, *,
                   n_tiles):
    f = f_ref[...]
    h = jnp.sin(fr0_ref[...] * (_dot3(f, w0_ref[...]) + b0_ref[...]))
    h = jnp.sin(fr1_ref[...] * (_dot3(h, w1_ref[...]) + b1_ref[...]))
    h = _dot3(h, w2_ref[...])
    h = h * jnp.exp(-f[:, 0:1] * dl_ref[...])
    side = pl.program_id(0) // n_tiles
    j = pl.program_id(0) % n_tiles

    @pl.when(j == 0)
    def _():
        ss_ref[...] = jnp.zeros_like(ss_ref)

    ss_ref[...] += jnp.sum(h * h, axis=0, keepdims=True)
    row = lax.broadcasted_iota(jnp.int32, (h.shape[0], 1), 0) + j * h.shape[0]
    h_ref[...] = jnp.where((side == 1) & (row == 0), 0.0, h)


def hyena_filter_taps(L, hp):
    pos = jnp.arange(L, dtype=F32)
    t = pos / (L - 1)
    bands = (HY_EMB - 1) // 2
    freqs = jnp.linspace(1e-4, bands - 1, bands, dtype=F32)
    ang = (2.0 * math.pi / L) * pos[:, None] * freqs[None, :]
    feats = jnp.concatenate([t[:, None], jnp.cos(ang), -jnp.sin(ang)], axis=-1)
    feats = jnp.pad(feats, ((0, 0), (0, 128 - HY_EMB)))
    feats = jnp.concatenate([feats, jnp.flip(feats, axis=0)], axis=0)
    w0 = jnp.pad(hp['hy_mlp_w0'], ((0, 128 - HY_EMB), (0, 0)))
    min_decay = math.log(1e-2) / 1.5
    max_decay = math.log(1e-2) / 0.3
    deltas = jnp.abs(jnp.linspace(min_decay, max_decay, HY_W, dtype=F32))
    dl = jnp.tile(deltas, 2).reshape(1, 2 * HY_W)
    w2 = hp['hy_mlp_w2'].reshape(HY_HID, 2, 2, HY_W).transpose(0, 2, 1, 3).reshape(HY_HID, 4 * HY_W)
    tl = min(L, 512)
    n_tiles = L // tl
    NS = 2 * HY_W
    full = lambda shape: pl.BlockSpec(shape, lambda i: (0, 0))
    return pl.pallas_call(
        functools.partial(_hyfilt_kernel, n_tiles=n_tiles),
        out_shape=(jax.ShapeDtypeStruct((2 * L, NS), F32), jax.ShapeDtypeStruct((1, 2 * NS), F32)),
        grid=(2 * n_tiles,),
        in_specs=[pl.BlockSpec((tl, 128), lambda i: (i, 0)), full((128, HY_HID)), full((1, HY_HID)),
                  full((1, HY_HID)), full((HY_HID, HY_HID)), full((1, HY_HID)), full((1, HY_HID)),
                  pl.BlockSpec((HY_HID, NS), lambda i: (0, i // n_tiles)), full((1, NS))],
        out_specs=(pl.BlockSpec((tl, NS), lambda i: (i, 0)), pl.BlockSpec((1, NS), lambda i: (0, i // n_tiles))),
        compiler_params=_cp(("arbitrary",)), name="hyena_filter",
    )(feats, w0, hp['hy_mlp_b0'].reshape(1, -1), hp['hy_freq0'].reshape(1, -1), hp['hy_mlp_w1'],
      hp['hy_mlp_b1'].reshape(1, -1), hp['hy_freq1'].reshape(1, -1), w2, dl)


def _cis(num, den):
    ang = (2.0 * math.pi / den) * (num % den).astype(F32)
    return jnp.cos(ang), -jnp.sin(ang)


def _fft_consts(NB, BS):
    N = NB * BS
    h = NB // 2
    k1 = jnp.arange(h, dtype=jnp.int32)
    j = jnp.arange(NB, dtype=jnp.int32)
    re, im = _cis(j[None, :] * (2 * k1[:, None] + 1), 2 * NB)
    f1 = jnp.concatenate([re, im], axis=0)
    neg = jnp.where(j >= h, -1.0, 1.0)[None, :]
    f1_data = f1[:, :h]
    f1_filt = f1 * neg
    f1_inv = (2.0 / N) * jnp.concatenate([re[:, :h].T, im[:, :h].T], axis=1)
    r = jnp.arange(BS, dtype=jnp.int32)
    k2 = jnp.arange(BS, dtype=jnp.int32)
    kk = 2 * k1[:, None, None] + 2 * NB * k2[None, :, None] + 1
    gre, gim = _cis(kk * r[None, None, :], 2 * N)
    gf = jnp.concatenate([jnp.concatenate([gre, -gim], axis=2), jnp.concatenate([gim, gre], axis=2)], axis=1)
    gret, gimt = gre.transpose(0, 2, 1), gim.transpose(0, 2, 1)
    gi = jnp.concatenate([jnp.concatenate([gret, gimt], axis=2), jnp.concatenate([-gimt, gret], axis=2)], axis=1)
    return (f1_data.astype(BF16), f1_filt.astype(BF16), f1_inv.astype(BF16), gf.astype(BF16), gi.astype(BF16))


FFT_PAD = 8


FFT_LW = 128


def _fft_fwd_kernel(ua_ref, ub_ref, f1_ref, g_ref, o_ref, t_sc, *, NB, BS, nj, kg):
    pitch = NB + FFT_PAD
    u_refs = (ua_ref, ub_ref)

    @pl.when(pl.program_id(2) == 0)
    def _():
        f1 = f1_ref[...]

        def body(r, carry):
            xr = jnp.concatenate([u[pl.ds(r, nj, stride=BS), :] for u in u_refs], axis=1).astype(BF16)
            res = _dot(f1, xr)
            for hh in range(2):
                t_sc[hh, pl.ds(pl.multiple_of(r * pitch, 8), NB), :] = res[:, hh * FFT_LW:(hh + 1) * FFT_LW]
            return carry

        lax.fori_loop(0, BS, body, 0, unroll=8)

    k0 = pl.program_id(2) * kg
    for i in range(kg):
        are = jnp.concatenate([t_sc[hh, pl.ds(k0 + i, BS, stride=pitch), :] for hh in range(2)], axis=1)
        aim = jnp.concatenate([t_sc[hh, pl.ds(k0 + i + NB // 2, BS, stride=pitch), :] for hh in range(2)], axis=1)
        a = jnp.concatenate([are, aim], axis=0).astype(BF16)
        o_ref[i] = _dot(g_ref[i], a)


def fft_fwd(u, col0, nbatch, nj, f1, gf, NB, BS, kg=8):
    h = NB // 2
    kg = min(kg, h)
    lw = FFT_LW
    ct = 2 * lw
    return pl.pallas_call(
        functools.partial(_fft_fwd_kernel, NB=NB, BS=BS, nj=nj, kg=kg),
        out_shape=jax.ShapeDtypeStruct((nbatch, h, 2 * BS, HY_W), F32),
        grid=(nbatch, HY_W // ct, h // kg),
        in_specs=[pl.BlockSpec((nj * BS, lw), lambda b, c, k: (b, col0 // lw + 2 * c), pipeline_mode=pl.Buffered(1)),
                  pl.BlockSpec((nj * BS, lw), lambda b, c, k: (b, col0 // lw + 2 * c + 1),
                               pipeline_mode=pl.Buffered(1)),
                  pl.BlockSpec((NB, nj), lambda b, c, k: (0, 0)),
                  pl.BlockSpec((kg, 2 * BS, 2 * BS), lambda b, c, k: (k, 0, 0))],
        out_specs=pl.BlockSpec((None, kg, 2 * BS, ct), lambda b, c, k: (b, k, 0, c)),
        scratch_shapes=[pltpu.VMEM((2, BS * (NB + FFT_PAD), lw), F32)],
        compiler_params=_cp(("parallel", "parallel", "arbitrary")), name="hyena_fft_fwd")(u, u, f1, gf)


def _cmul(u, h, half):
    ure, uim = u[:half], u[half:]
    hre, him = h[:half], h[half:]
    return jnp.concatenate([ure * hre - uim * him, ure * him + uim * hre], axis=0)


def _fft_inv_kernel(us_ref, hs_ref, gi_ref, f1i_ref, o_ref, t_sc, y_sc, *, NB, BS, kg):
    ks = pl.program_id(2)
    pitch = 2 * BS + FFT_PAD
    for i in range(kg):
        y = _cmul(us_ref[i], hs_ref[i], BS).astype(BF16)
        row = pl.multiple_of((ks * kg + i) * pitch, 8)
        res = _dot(gi_ref[i], y)
        for hh in range(2):
            t_sc[hh, pl.ds(row, 2 * BS), :] = res[:, hh * FFT_LW:(hh + 1) * FFT_LW]

    @pl.when(ks == pl.num_programs(2) - 1)
    def _():
        f1i = f1i_ref[...]

        def body(r, carry):
            bre = jnp.concatenate([t_sc[hh, pl.ds(r, NB // 2, stride=pitch), :] for hh in range(2)], axis=1)
            bim = jnp.concatenate([t_sc[hh, pl.ds(r + BS, NB // 2, stride=pitch), :] for hh in range(2)], axis=1)
            b = jnp.concatenate([bre, bim], axis=0).astype(BF16)
            res = _dot(f1i, b)
            for hh in range(2):
                y_sc[hh, pl.ds(r, NB // 2, stride=BS), :] = res[:, hh * FFT_LW:(hh + 1) * FFT_LW]
            return carry

        lax.fori_loop(0, BS, body, 0, unroll=8)
        o_ref[...] = jnp.concatenate([y_sc[0], y_sc[1]], axis=1)


def fft_inv(us, hs, gi, f1i, NB, BS, kg=8):
    nbatch, h = us.shape[0], NB // 2
    kg = min(kg, h)
    L = h * BS
    ct = 2 * FFT_LW
    return pl.pallas_call(
        functools.partial(_fft_inv_kernel, NB=NB, BS=BS, kg=kg),
        out_shape=jax.ShapeDtypeStruct((nbatch * L, HY_W), F32),
        grid=(nbatch, HY_W // ct, h // kg),
        in_specs=[pl.BlockSpec((None, kg, 2 * BS, ct), lambda b, c, k: (b, k, 0, c)),
                  pl.BlockSpec((None, kg, 2 * BS, ct), lambda b, c, k: (0, k, 0, c)),
                  pl.BlockSpec((kg, 2 * BS, 2 * BS), lambda b, c, k: (k, 0, 0)),
                  pl.BlockSpec((h, NB), lambda b, c, k: (0, 0))],
        out_specs=pl.BlockSpec((L, ct), lambda b, c, k: (b, c)),
        scratch_shapes=[pltpu.VMEM((2, h * (2 * BS + FFT_PAD), FFT_LW), F32), pltpu.VMEM((2, L, FFT_LW), F32)],
        compiler_params=_cp(("parallel", "parallel", "arbitrary")), name="hyena_fft_inv")(us, hs, gi, f1i)


def _dft_consts(L):
    N = 2 * L
    k = jnp.arange(L, dtype=jnp.int32)
    n = jnp.arange(N, dtype=jnp.int32)
    re, im = _cis(n[None, :] * (2 * k[:, None] + 1), 2 * N)
    f = jnp.concatenate([re, im], axis=0)
    neg = jnp.where(n >= L, -1.0, 1.0)[None, :]
    fi = (2.0 / N) * jnp.concatenate([re[:, :L].T, im[:, :L].T], axis=1)
    return f[:, :L].astype(BF16), (f * neg).astype(BF16), fi.astype(BF16)


def _cdft_kernel(f_ref, x_ref, o_ref):
    o_ref[...] = _dot(f_ref[...], x_ref[...].astype(BF16))


def dft_fwd(x, f, row0, col0, nbatch, ct=256):
    M, K = f.shape
    return pl.pallas_call(
        _cdft_kernel, out_shape=jax.ShapeDtypeStruct((nbatch, M, HY_W), F32),
        grid=(nbatch, HY_W // ct),
        in_specs=[pl.BlockSpec((M, K), lambda b, c: (0, 0)),
                  pl.BlockSpec((K, ct), lambda b, c: (row0 // K + b, col0 // ct + c))],
        out_specs=pl.BlockSpec((None, M, ct), lambda b, c: (b, 0, c)),
        compiler_params=_cp(("parallel", "parallel")), name="hyena_dft_fwd")(f, x)


def _cdft_inv_kernel(us_ref, hs_ref, fi_ref, o_ref):
    half = us_ref.shape[0] // 2
    o_ref[...] = _dot(fi_ref[...], _cmul(us_ref[...], hs_ref[...], half).astype(BF16))


def dft_inv(us, hs, fi, ct=256):
    nbatch, M2, _ = us.shape
    L = fi.shape[0]
    return pl.pallas_call(
        _cdft_inv_kernel, out_shape=jax.ShapeDtypeStruct((nbatch * L, HY_W), F32),
        grid=(nbatch, HY_W // ct),
        in_specs=[pl.BlockSpec((None, M2, ct), lambda b, c: (b, 0, c)),
                  pl.BlockSpec((None, M2, ct), lambda b, c: (0, 0, c)),
                  pl.BlockSpec((L, M2), lambda b, c: (0, 0))],
        out_specs=pl.BlockSpec((L, ct), lambda b, c: (b, c)),
        compiler_params=_cp(("parallel", "parallel")), name="hyena_dft_inv")(us, hs, fi)


def _hy_gate_kernel(g_ref, y_ref, u_ref, ss_ref, b_ref, o_ref):
    scale = lax.rsqrt(ss_ref[0:1, :] + ss_ref[1:2, :] + 1e-6)
    o_ref[...] = g_ref[...] * (y_ref[...] * scale + u_ref[...] * b_ref[...])


def _hy_gate2_kernel(g_ref, yl_ref, yc_ref, ul_ref, uc_ref, ssl_ref, ssc_ref, b_ref, o_ref, *, n_lat_t):
    is_lat = pl.program_id(0) < n_lat_t
    y = jnp.where(is_lat, yl_ref[...], yc_ref[...])
    uin = jnp.where(is_lat, ul_ref[...], uc_ref[...])
    ss = jnp.where(is_lat, ssl_ref[...], ssc_ref[...])
    scale = lax.rsqrt(ss[0:1, :] + ss[1:2, :] + 1e-6)
    o_ref[...] = g_ref[...] * (y * scale + uin * b_ref[...])


def hy_gate(gate, gcol, grow, y, uin, ucol, urow, ss, order, bias, tm=512):
    M = y.shape[0]
    return pl.pallas_call(
        _hy_gate_kernel, out_shape=jax.ShapeDtypeStruct((M, HY_W), F32), grid=(M // tm,),
        in_specs=[pl.BlockSpec((tm, HY_W), lambda i: (grow // tm + i, gcol // HY_W)),
                  pl.BlockSpec((tm, HY_W), lambda i: (i, 0)),
                  pl.BlockSpec((tm, HY_W), lambda i: (urow // tm + i, ucol // HY_W)),
                  pl.BlockSpec((None, 2, HY_W), lambda i: (order, 0, 0)),
                  pl.BlockSpec((None, 1, HY_W), lambda i: (order, 0, 0))],
        out_specs=pl.BlockSpec((tm, HY_W), lambda i: (i, 0)),
        compiler_params=_cp(("parallel",)), name="hyena_gate")(gate, y, uin, ss, bias)


def hyena(u, hp):
    C = HY_W
    bias = hp['hy_bias'].reshape(2, 1, C)
    NB = BS = int(round(math.sqrt(2 * SEQ)))
    f1d, f1f, f1i, gf, gi = _fft_consts(NB, BS)
    taps, ss = hyena_filter_taps(SEQ, hp)
    ss = ss.reshape(2, 2, C).transpose(1, 0, 2)
    conv_l = lambda zin, zcol, order: fft_inv(fft_fwd(zin, zcol, BATCH, NB // 2, f1d, gf, NB, BS),
                                              fft_fwd(taps, order * C, 1, NB, f1f, gf, NB, BS), gi, f1i, NB, BS)
    z1_lat = hy_gate(u, 0, 0, conv_l(u, 2 * C, 0), u, 2 * C, 0, ss, 0, bias)
    y2_lat = conv_l(z1_lat, 0, 1)
    fd, ff, fi = _dft_consts(CTX)
    taps_c, ss_c = hyena_filter_taps(CTX, hp)
    ss_c = ss_c.reshape(2, 2, C).transpose(1, 0, 2)
    conv_c = lambda zin, zrow, zcol, order: dft_inv(dft_fwd(zin, fd, zrow, zcol, BATCH),
                                                    dft_fwd(taps_c, ff, 0, order * C, 1), fi)
    z1_ctx = hy_gate(u, 0, N_LAT, conv_c(u, N_LAT, 2 * C, 0), u, 2 * C, N_LAT, ss_c, 0, bias)
    y2_ctx = conv_c(z1_ctx, 0, 0, 1)
    tm = 512
    n_lat_t = N_LAT // tm
    lat = lambda i: (jnp.minimum(i, n_lat_t - 1), 0)
    ctx = lambda i: (jnp.maximum(i - n_lat_t, 0), 0)
    return pl.pallas_call(
        functools.partial(_hy_gate2_kernel, n_lat_t=n_lat_t),
        out_shape=jax.ShapeDtypeStruct((N_ROWS, C), F32), grid=(N_ROWS // tm,),
        in_specs=[pl.BlockSpec((tm, C), lambda i: (i, 1)),
                  pl.BlockSpec((tm, C), lat), pl.BlockSpec((tm, C), ctx),
                  pl.BlockSpec((tm, C), lat), pl.BlockSpec((tm, C), ctx),
                  pl.BlockSpec((None, 2, C), lambda i: (1, 0, 0)), pl.BlockSpec((None, 2, C), lambda i: (1, 0, 0)),
                  pl.BlockSpec((None, 1, C), lambda i: (1, 0, 0))],
        out_specs=pl.BlockSpec((tm, C), lambda i: (i, 0)),
        compiler_params=_cp(("parallel",)), name="hyena_gate2",
    )(u, y2_lat, y2_ctx, z1_lat, z1_ctx, ss, ss_c, bias)


EV_SSD_IN = SSD_HEADS * SSD_P
EV_XBC = EV_SSD_IN + 2 * SSD_G * SSD_S
EV_PAD_N = 5760


def even_mixer(x, mods, g_pre, g_post, ep):
    o1 = EV_SSD_IN
    o2 = o1 + EV_XBC
    o3 = o2 + 2 * SSD_HEADS
    w = ep['w_in']
    n_in = w.shape[1]
    hw = 3 * HY_W
    w_perm = jnp.concatenate([w[:, o3:], w[:, :o2], w[:, o2:o3],
                              jnp.zeros((D, EV_PAD_N - n_in), F32)], axis=1).astype(BF16)
    p = norm_mod_matmul(x, g_pre, mods, w_perm, 0, 1, tn=1920, name="even_in_proj")
    xbc = dwconv3(p, hw + o1, EV_XBC, ep['ssd_conv_w'], ep['ssd_conv_b'], True, "ssd_conv")
    u = dwconv3(p, 0, hw, ep['hy_conv_w'], ep['hy_conv_b'], False, "hyena_conv")
    dt_raw = p[:, hw + o2:hw + o2 + 2 * SSD_HEADS]
    y2 = ssd_scan(xbc, dt_raw, ep['ssd_dt_bias'], ep['ssd_a_log'])
    s = ssd_output(y2, xbc, p, hw, ep['ssd_d'], ep['ssd_norm_w'])
    zh = hyena(u, ep)
    return outproj_residual(s, zh, ep['w_out'].astype(BF16), x, g_post, mods, 2, name="even_out_proj")


def _head_sum(x, e, et):
    return _dot2l(_dot2l(x, e), et)


def _rw_prep_kernel(r_ref, k_ref, v_ref, lo_ref, w0_ref, wup_ref, a0_ref, aup_ref, gup_ref, kk_ref, ka_ref,
                    rk_ref, e_ref, et_ref, lw_ref, kd_ref, be_ref, kap_ref, g_ref, bonus_ref):
    r, k, v = r_ref[...], k_ref[...], v_ref[...]
    lo = lo_ref[...]
    wc, ac, gc = lo[:, 0:64], lo[:, 64:128], lo[:, 128:384]
    e, et = e_ref[...], et_ref[...]
    kk = k * kk_ref[...]
    kap = kk * lax.rsqrt(_head_sum(kk * kk, e, et) + 1e-12)
    kap_ref[...] = kap
    g_ref[...] = _dot(_sigmoid(gc).astype(BF16), gup_ref[...].astype(BF16))
    kd_sum = jnp.zeros_like(k)
    for d in range(2):
        wlog = -_softplus(-(w0_ref[d:d + 1, :] + _dot3(jnp.tanh(wc), wup_ref[d]))) - 0.5
        lw_ref[d] = -jnp.exp(wlog)
        a = _sigmoid(a0_ref[d:d + 1, :] + _dot(ac.astype(BF16), aup_ref[d].astype(BF16)))
        kd = k * (1.0 + (a - 1.0) * ka_ref[...])
        kd_ref[d] = kd
        be_ref[d] = kap * a
        kd_sum = kd_sum + kd
    bonus_ref[...] = _head_sum(r * kd_sum * rk_ref[...], e, et) * v


def rwkv_prepare(code, lora, op, tm=256):
    W = RW_H * RW_N
    heads = jnp.arange(W, dtype=jnp.int32) // RW_N
    e = (heads[:, None] == jnp.arange(128, dtype=jnp.int32)[None, :]).astype(BF16)
    et = jnp.transpose(e)
    gup = jnp.pad(op['g_up'], ((0, 256 - op['g_up'].shape[0]), (0, 0)))
    row = lambda a: a.reshape(1, W)
    full2 = lambda shape: pl.BlockSpec(shape, lambda i: (0,) * len(shape))
    outs = pl.pallas_call(
        _rw_prep_kernel,
        out_shape=(jax.ShapeDtypeStruct((2, N_ROWS, W), F32), jax.ShapeDtypeStruct((2, N_ROWS, W), F32),
                   jax.ShapeDtypeStruct((2, N_ROWS, W), F32), jax.ShapeDtypeStruct((N_ROWS, W), F32),
                   jax.ShapeDtypeStruct((N_ROWS, W), F32), jax.ShapeDtypeStruct((N_ROWS, W), F32)),
        grid=(N_ROWS // tm,),
        in_specs=[pl.BlockSpec((tm, W), lambda i: (i, 0)), pl.BlockSpec((tm, W), lambda i: (i, 1)),
                  pl.BlockSpec((tm, W), lambda i: (i, 2)), pl.BlockSpec((tm, 384), lambda i: (i, 0)),
                  full2((2, W)), full2((2, 64, W)), full2((2, W)), full2((2, 64, W)), full2((256, W)),
                  full2((1, W)), full2((1, W)), full2((1, W)), full2((W, 128)), full2((128, W))],
        out_specs=(pl.BlockSpec((2, tm, W), lambda i: (0, i, 0)), pl.BlockSpec((2, tm, W), lambda i: (0, i, 0)),
                   pl.BlockSpec((2, tm, W), lambda i: (0, i, 0)), pl.BlockSpec((tm, W), lambda i: (i, 0)),
                   pl.BlockSpec((tm, W), lambda i: (i, 0)), pl.BlockSpec((tm, W), lambda i: (i, 0))),
        compiler_params=_cp(("parallel",)), name="rwkv_prepare",
    )(code, code, code, lora, op['w0'], op['w_up'], op['a0'], op['a_up'], gup, row(op['k_k']), row(op['k_a']),
      row(op['r_k']), e, et)
    return outs


def _rw_scan_kernel(r_ref, v_ref, lw_ref, kd_ref, be_ref, kap_ref, y_ref, st_ref):
    d = pl.program_id(0)
    c = pl.program_id(2)
    C = RW_CHUNK
    N = RW_N

    @pl.when(c == 0)
    def _():
        st_ref[...] = jnp.zeros_like(st_ref)

    isb = d == 1
    sgn = 1 - 2 * d
    ii = lax.broadcasted_iota(jnp.int32, (C, C), 0)
    jj = lax.broadcasted_iota(jnp.int32, (C, C), 1)
    dif = sgn * (ii - jj)
    incl = dif >= 0
    strict = dif > 0
    tri = incl.astype(BF16)
    eye = (ii == jj).astype(F32)
    blk = [(ii >> s) == (jj >> s) for s in range(3, C.bit_length() - 1)]
    masks = [blk[0]] + [blk[l] & ~blk[l - 1] for l in range(1, len(blk))] + [~blk[-1]]
    lw = lw_ref[...]
    cum = _dot2r(tri, lw)
    ec = jnp.exp(cum)
    en = jnp.exp(-cum)
    ea = jnp.exp(cum - lw)
    last = jnp.where(isb, cum[0:1, :], cum[C - 1:C, :])
    el = jnp.exp(last - cum)
    kap = kap_ref[...]
    r = r_ref[...]
    v = v_ref[...]
    a_t = -kap * ea
    r_t = r * ec
    b_t = be_ref[...] * en
    k_t = kd_ref[...] * en
    b_l = be_ref[...] * el
    k_l = kd_ref[...] * el
    pc = jnp.exp(last)
    H = range(RW_H)
    sl = [slice(h * N, (h + 1) * N) for h in H]
    bd = lambda a, b: _dot(a.astype(BF16), b.astype(BF16))
    tn = lambda a, b: lax.dot_general(a, b, (((0,), (0,)), ((), ())), preferred_element_type=F32)
    sc = [_dot_nt(jnp.concatenate([a_t[:, sl[h]], r_t[:, sl[h]]], axis=0).astype(BF16),
                  jnp.concatenate([b_t[:, sl[h]], k_t[:, sl[h]]], axis=0).astype(BF16)) for h in H]
    n_ab = [jnp.where(strict, sc[h][0:C, 0:C], 0.0) for h in H]
    a_ak = [jnp.where(strict, sc[h][0:C, C:2 * C], 0.0).astype(BF16) for h in H]
    m_rb = [jnp.where(incl, sc[h][C:2 * C, 0:C], 0.0).astype(BF16) for h in H]
    m_rk = [jnp.where(incl, sc[h][C:2 * C, C:2 * C], 0.0).astype(BF16) for h in H]
    vh = [v[:, sl[h]].astype(BF16) for h in H]
    d0 = [jnp.where(masks[0], n_ab[h], 0.0) for h in H]
    d2 = [bd(d0[h], d0[h]) for h in H]
    d4 = [bd(d2[h], d2[h]) for h in H]
    t = [bd(eye + d0[h], eye + d2[h]) for h in H]
    t = [bd(t[h], eye + d4[h]) for h in H]
    for m in masks[1:]:
        et = [bd(jnp.where(m, n_ab[h], 0.0), t[h]) for h in H]
        t = [t[h] + bd(t[h], et[h]) for h in H]
    amv = [_dot(jnp.concatenate([a_ak[h], m_rk[h]], axis=0), vh[h]) for h in H]
    wub = [bd(t[h], jnp.concatenate([a_t[:, sl[h]], amv[h][0:C]], axis=1)).astype(BF16) for h in H]
    kv = [tn(k_l[:, sl[h]].astype(BF16), vh[h]) for h in H]
    qy = [_dot(m_rb[h], wub[h]) + jnp.concatenate([r_t[:, sl[h]], amv[h][C:2 * C]], axis=1) for h in H]
    pp = [tn(b_l[:, sl[h]].astype(BF16), wub[h]) + jnp.concatenate([eye[0:N, 0:N] * pc[:, sl[h]], kv[h]], axis=1)
          for h in H]
    h_old = [st_ref[h] for h in H]
    ys = [bd(qy[h][:, 0:N], h_old[h]) + qy[h][:, N:2 * N] for h in H]
    for h in H:
        st_ref[h] = _dot3(pp[h][:, 0:N], h_old[h]) + pp[h][:, N:2 * N]
    y_ref[...] = jnp.concatenate(ys, axis=1)


def _rw_rowblock(d, b, c):
    n_ctx = CTX // RW_CHUNK
    n_lat = SEQ // RW_CHUNK
    cc = jnp.where(d == 0, c, n_ctx - 1 - c)
    lc = jnp.where(d == 0, c - n_ctx, n_ctx + n_lat - 1 - c)
    return jnp.where(c < n_ctx, N_LAT // RW_CHUNK + b * n_ctx + cc, b * n_lat + lc)


def rwkv_scan(code, lw, kd, be, kap):
    W = RW_H * RW_N
    nch = (CTX + SEQ) // RW_CHUNK
    rb = lambda d, b, c: _rw_rowblock(d, b, c)
    return pl.pallas_call(
        _rw_scan_kernel,
        out_shape=jax.ShapeDtypeStruct((2, N_ROWS, W), F32),
        grid=(2, BATCH, nch),
        in_specs=[pl.BlockSpec((RW_CHUNK, W), lambda d, b, c: (rb(d, b, c), 0)),
                  pl.BlockSpec((RW_CHUNK, W), lambda d, b, c: (rb(d, b, c), 2)),
                  pl.BlockSpec((None, RW_CHUNK, W), lambda d, b, c: (d, rb(d, b, c), 0)),
                  pl.BlockSpec((None, RW_CHUNK, W), lambda d, b, c: (d, rb(d, b, c), 0)),
                  pl.BlockSpec((None, RW_CHUNK, W), lambda d, b, c: (d, rb(d, b, c), 0)),
                  pl.BlockSpec((RW_CHUNK, W), lambda d, b, c: (rb(d, b, c), 0))],
        out_specs=pl.BlockSpec((None, RW_CHUNK, W), lambda d, b, c: (d, rb(d, b, c), 0)),
        scratch_shapes=[pltpu.VMEM((RW_H, RW_N, RW_N), F32)],
        compiler_params=_cp(("parallel", "parallel", "arbitrary")), name="rwkv_scan",
    )(code, code, lw, kd, be, kap)


def _rw_out_kernel(yf_ref, yb_ref, bonus_ref, g_ref, lnw_ref, lnb_ref, e_ref, et_ref, o_ref):
    e, et = e_ref[...], et_ref[...]
    y = yf_ref[...] + yb_ref[...]
    mean = _head_sum(y, e, et) * (1.0 / RW_N)
    yc = y - mean
    var = _head_sum(yc * yc, e, et) * (1.0 / RW_N)
    yn = yc * lax.rsqrt(var + RW_GN_EPS) * lnw_ref[...] + lnb_ref[...]
    o_ref[...] = (yn + bonus_ref[...]) * g_ref[...]


def rwkv_output(y2, bonus, g, op, tm=512):
    W = RW_H * RW_N
    heads = jnp.arange(W, dtype=jnp.int32) // RW_N
    e = (heads[:, None] == jnp.arange(128, dtype=jnp.int32)[None, :]).astype(BF16)
    et = jnp.transpose(e)
    M = N_LAT
    return pl.pallas_call(
        _rw_out_kernel, out_shape=jax.ShapeDtypeStruct((M, W), F32), grid=(M // tm,),
        in_specs=[pl.BlockSpec((None, tm, W), lambda i: (0, i, 0)), pl.BlockSpec((None, tm, W), lambda i: (1, i, 0)),
                  pl.BlockSpec((tm, W), lambda i: (i, 0)), pl.BlockSpec((tm, W), lambda i: (i, 0)),
                  pl.BlockSpec((1, W), lambda i: (0, 0)), pl.BlockSpec((1, W), lambda i: (0, 0)),
                  pl.BlockSpec((W, 128), lambda i: (0, 0)), pl.BlockSpec((128, W), lambda i: (0, 0))],
        out_specs=pl.BlockSpec((tm, W), lambda i: (i, 0)),
        compiler_params=_cp(("parallel",)), name="rwkv_output",
    )(y2, y2, bonus, g, op['ln_w'].reshape(1, W), op['ln_b'].reshape(1, W), e, et)


AT_Q = RW_H * AT_HD
AT_KW = AT_KV * AT_HD
AT_TQ = 512
AT_TK = 768


def _rope_tables(tm):
    half = AT_HD // 2
    inv = 10000.0 ** (-jnp.arange(0, half, 2, dtype=F32) / half)
    pos = jnp.arange(SEQ, dtype=jnp.int32)
    row = (pos // GRID_W).astype(F32)[:, None] * inv
    col = (pos % GRID_W).astype(F32)[:, None] * inv
    cos_h = jnp.concatenate([jnp.cos(row), jnp.cos(row), jnp.cos(col), jnp.cos(col)], axis=1)
    sin_h = jnp.concatenate([-jnp.sin(row), jnp.sin(row), -jnp.sin(col), jnp.sin(col)], axis=1)
    cos_t = jnp.concatenate([jnp.tile(cos_h, (1, 2)), jnp.ones((tm, 128), F32)], axis=0)
    sin_t = jnp.concatenate([jnp.tile(sin_h, (1, 2)), jnp.zeros((tm, 128), F32)], axis=0)
    return cos_t, sin_t


def _rot_partner(x):
    q = AT_HD // 4
    w = x.shape[1]
    lane = lax.broadcasted_iota(jnp.int32, x.shape, 1)
    return jnp.where((lane % (2 * q)) < q, pltpu.roll(x, w - q, 1), pltpu.roll(x, q, 1))


def _at_prep_kernel(q_ref, k_ref, v_ref, cos_ref, sin_ref, qn_ref, kn_ref, e_ref, et_ref, qo_ref, ko_ref, vo_ref):
    e, et = e_ref[...], et_ref[...]
    cos2, sin2 = cos_ref[...], sin_ref[...]

    def norm_rope(x, gain, nrep):
        ms = _head_sum(x * x, e[:x.shape[1]], et[:, :x.shape[1]]) * (1.0 / AT_HD)
        xn = x * lax.rsqrt(ms + EPS) * gain
        cos = jnp.tile(cos2, (1, nrep))
        sin = jnp.tile(sin2, (1, nrep))
        return xn * cos + _rot_partner(xn) * sin

    qn = norm_rope(q_ref[...], qn_ref[...], AT_Q // 128) * (AT_HD ** -0.5 * math.log2(math.e))
    qo_ref[...] = jnp.transpose(qn).astype(BF16)
    ko_ref[...] = norm_rope(k_ref[...], kn_ref[...], AT_KW // 128).astype(BF16)
    vo_ref[...] = jnp.transpose(v_ref[...]).astype(BF16)


def attention_prepare(p, q_norm, k_norm, tm=256):
    cos_t, sin_t = _rope_tables(tm)
    heads = jnp.arange(AT_Q, dtype=jnp.int32) // AT_HD
    e = (heads[:, None] == jnp.arange(128, dtype=jnp.int32)[None, :]).astype(BF16)
    et = jnp.transpose(e)
    tab = lambda i: jnp.where(i * tm < N_LAT, ((i * tm) % SEQ) // tm, SEQ // tm)
    n_lat_t, n_seq_t, n_ctx_t = N_LAT // tm, SEQ // tm, CTX // tm
    kvb = lambda i: jnp.where(i < n_lat_t, (i // n_seq_t) * (n_seq_t + n_ctx_t) + n_ctx_t + i % n_seq_t,
                              ((i - n_lat_t) // n_ctx_t) * (n_seq_t + n_ctx_t) + (i - n_lat_t) % n_ctx_t)
    qcol = (3 * RW_H * RW_N) // AT_Q
    kcol = (3 * RW_H * RW_N + AT_Q) // AT_KW
    return pl.pallas_call(
        _at_prep_kernel,
        out_shape=(jax.ShapeDtypeStruct((AT_Q, N_ROWS), BF16), jax.ShapeDtypeStruct((N_ROWS, AT_KW), BF16),
                   jax.ShapeDtypeStruct((AT_KW, N_ROWS), BF16)),
        grid=(N_ROWS // tm,),
        in_specs=[pl.BlockSpec((tm, AT_Q), lambda i: (i, qcol)),
                  pl.BlockSpec((tm, AT_KW), lambda i: (i, kcol)),
                  pl.BlockSpec((tm, AT_KW), lambda i: (i, kcol + 1)),
                  pl.BlockSpec((tm, 128), lambda i: (tab(i), 0)),
                  pl.BlockSpec((tm, 128), lambda i: (tab(i), 0)),
                  pl.BlockSpec((1, AT_Q), lambda i: (0, 0)),
                  pl.BlockSpec((1, AT_KW), lambda i: (0, 0)),
                  pl.BlockSpec((AT_Q, 128), lambda i: (0, 0)),
                  pl.BlockSpec((128, AT_Q), lambda i: (0, 0))],
        out_specs=(pl.BlockSpec((AT_Q, tm), lambda i: (0, i)), pl.BlockSpec((tm, AT_KW), lambda i: (kvb(i), 0)),
                   pl.BlockSpec((AT_KW, tm), lambda i: (0, kvb(i)))),
        compiler_params=_cp(("parallel",)), name="attn_prepare",
    )(p, p, p, cos_t, sin_t, jnp.tile(q_norm, AT_Q // AT_HD).reshape(1, AT_Q),
      jnp.tile(k_norm, AT_KV).reshape(1, AT_KW), e, et)


AT_REBASE = 64.0
AT_SEED = 128


def _flash_t_kernel(qt_ref, k_ref, vt_ref, o_ref, m_sc, l_sc, acc_sc, p_sc):
    ki = pl.program_id(2)
    nq = AT_Q // AT_HD
    gq = nq // AT_KV

    @pl.when(ki == 0)
    def _():
        for g in range(AT_KV):
            k0 = k_ref[0:AT_SEED, g * AT_HD:(g + 1) * AT_HD]
            for h in range(g * gq, (g + 1) * gq):
                m_sc[h] = jnp.max(_dot(k0, qt_ref[h * AT_HD:(h + 1) * AT_HD, :]), axis=0, keepdims=True)
        l_sc[...] = jnp.zeros_like(l_sc)
        acc_sc[...] = jnp.zeros_like(acc_sc)

    def k_group(g):
        return k_ref[:, g * AT_HD:(g + 1) * AT_HD]

    def v_group(g):
        return jnp.concatenate([vt_ref[g * AT_HD:(g + 1) * AT_HD, :],
                                jnp.ones((16, vt_ref.shape[1]), BF16)], axis=0)

    def scores(g):
        kg = k_group(g)
        return [_dot(kg, qt_ref[h * AT_HD:(h + 1) * AT_HD, :]) for h in range(g * gq, (g + 1) * gq)]

    gap = None
    for g in range(AT_KV):
        st = scores(g)
        m_cur = [m_sc[h] for h in range(g * gq, (g + 1) * gq)]
        for i in range(gq):
            over_i = jnp.max(st[i], axis=0, keepdims=True) - m_cur[i]
            gap = over_i if gap is None else jnp.maximum(gap, over_i)
            p_sc[g * gq + i] = jnp.exp2(st[i] - m_cur[i]).astype(BF16)
    rebase = jnp.max(gap) > AT_REBASE

    @pl.when(jnp.logical_not(rebase))
    def _():
        for g in range(AT_KV):
            vtg1 = v_group(g)
            hs = range(g * gq, (g + 1) * gq)
            pv = [_dot(vtg1, p_sc[h]) for h in hs]
            for i, h in enumerate(hs):
                l_sc[h] = l_sc[h] + pv[i][AT_HD:AT_HD + 1, :]
                rows = pl.ds(h * AT_HD, AT_HD)
                acc_sc[rows, :] = acc_sc[rows, :] + pv[i][0:AT_HD, :]

    @pl.when(rebase)
    def _():
        for g in range(AT_KV):
            vtg1 = v_group(g)
            hs = range(g * gq, (g + 1) * gq)
            st = scores(g)
            m_old = [m_sc[h] for h in hs]
            m_new = [jnp.maximum(m_old[i], jnp.max(st[i], axis=0, keepdims=True)) for i in range(gq)]
            alpha = [jnp.exp2(m_old[i] - m_new[i]) for i in range(gq)]
            pt = [jnp.exp2(st[i] - m_new[i]).astype(BF16) for i in range(gq)]
            pv = [_dot(vtg1, pt[i]) for i in range(gq)]
            for i, h in enumerate(hs):
                l_sc[h] = alpha[i] * l_sc[h] + pv[i][AT_HD:AT_HD + 1, :]
                m_sc[h] = m_new[i]
                rows = pl.ds(h * AT_HD, AT_HD)
                acc_sc[rows, :] = alpha[i] * acc_sc[rows, :] + pv[i][0:AT_HD, :]

    @pl.when(ki == pl.num_programs(2) - 1)
    def _():
        inv = jnp.concatenate([jnp.broadcast_to(1.0 / l_sc[h], (AT_HD, l_sc.shape[2])) for h in range(nq)], axis=0)
        o_ref[...] = jnp.transpose(acc_sc[...] * inv)


def flash_attention_t(qt, k, vt):
    nq = AT_Q // AT_HD
    nk = (CTX + SEQ) // AT_TK
    kv_rb = lambda b, ki: b * nk + ki
    return pl.pallas_call(
        _flash_t_kernel,
        out_shape=jax.ShapeDtypeStruct((N_LAT, AT_Q), F32),
        grid=(BATCH, SEQ // AT_TQ, nk),
        in_specs=[pl.BlockSpec((AT_Q, AT_TQ), lambda b, qi, ki: (0, b * (SEQ // AT_TQ) + qi)),
                  pl.BlockSpec((AT_TK, AT_KW), lambda b, qi, ki: (kv_rb(b, ki), 0)),
                  pl.BlockSpec((AT_KW, AT_TK), lambda b, qi, ki: (0, kv_rb(b, ki)))],
        out_specs=pl.BlockSpec((AT_TQ, AT_Q), lambda b, qi, ki: (b * (SEQ // AT_TQ) + qi, 0)),
        scratch_shapes=[pltpu.VMEM((nq, 1, AT_TQ), F32), pltpu.VMEM((nq, 1, AT_TQ), F32),
                        pltpu.VMEM((AT_Q, AT_TQ), F32), pltpu.VMEM((nq, AT_TK, AT_TQ), BF16)],
        compiler_params=_cp(("parallel", "parallel", "arbitrary")), name="flash_attention")(qt, k, vt)


OD_PAD_N = 4992


def odd_mixer(x, mods, g_pre, g_post, op):
    W = RW_H * RW_N
    w = op['w_in']
    c3 = 3 * W
    code_n = c3 + 64 + 64 + 160
    w_perm = jnp.concatenate([w[:, :c3], w[:, code_n:], w[:, c3:code_n],
                              jnp.zeros((D, OD_PAD_N - w.shape[1]), F32)], axis=1).astype(BF16)
    p = norm_mod_matmul(x, g_pre, mods, w_perm, 0, 1, tn=1664, name="odd_in_proj")
    mu = op['mu']
    taps = lambda m: jnp.stack([0.5 * m, 1.0 - m, 0.5 * m], axis=1)
    code = dwconv3(p, 0, c3, taps(mu[:c3]), jnp.zeros((c3,), F32), False, "rwkv_shift")
    lo_col = c3 + AT_Q + 2 * AT_KW
    mu_lo = jnp.pad(mu[c3:], (0, 384 - (code_n - c3)))
    lora = dwconv3(p, lo_col, 384, taps(mu_lo), jnp.zeros((384,), F32), False, "rwkv_shift_lora", cb=128)
    lw, kd, be, kap, g, bonus = rwkv_prepare(code, lora, op)
    y2 = rwkv_scan(code, lw, kd, be, kap)
    o_l = rwkv_output(y2, bonus, g, op)
    q, k, v = attention_prepare(p, op['q_norm'], op['k_norm'])
    a_l = flash_attention_t(q, k, v)
    return outproj_residual(o_l, a_l, op['w_out'].astype(BF16), x, g_post, mods, 2, name="odd_out_proj")


MOE_T = 256
MOE_CAP = 64
MOE_EPS = 4


def _router2_kernel(x_ref, g_ref, mod_ref, rw_ref, rb_ref, s1_ref, s3_ref, s2_ref, t_ref, wt_ref, cnt_ref, sh_ref):
    x = x_ref[...]
    ms = jnp.mean(x * x, axis=-1, keepdims=True)
    t = x * lax.rsqrt(ms + EPS) * g_ref[...] * (1.0 + mod_ref[4:5, :]) + mod_ref[3:4, :]
    tb = t.astype(BF16)
    t_ref[...] = tb
    sh_ref[...] = _dot((_silu(_dot(tb, s1_ref[...])) * _dot(tb, s3_ref[...])).astype(BF16), s2_ref[...])
    th, tl = _split(t)
    wh, wl = _split(rw_ref[...])
    lg = _dot_nt(wh, th) + (_dot_nt(wh, tl) + _dot_nt(wl, th))
    sc = _sigmoid(lg)
    sel = sc + rb_ref[...]
    tm = sel.shape[1]
    gsz = N_EXP // N_GRP
    ninf = -jnp.inf
    sel3 = sel.reshape(N_GRP, gsz, tm)
    i3 = lax.broadcasted_iota(jnp.int32, sel3.shape, 1)
    m1 = jnp.max(sel3, axis=1, keepdims=True)
    first = jnp.min(jnp.where(sel3 == m1, i3, gsz), axis=1, keepdims=True)
    m2 = jnp.max(jnp.where(i3 == first, ninf, sel3), axis=1, keepdims=True)
    grp = (m1 + m2).reshape(N_GRP, tm)
    gi = lax.broadcasted_iota(jnp.int32, grp.shape, 0)
    gmask = jnp.zeros(grp.shape, F32)
    for _ in range(TOPK_GRP):
        m = jnp.max(grp, axis=0, keepdims=True)
        pick = jnp.min(jnp.where(grp == m, gi, N_GRP), axis=0, keepdims=True)
        hit = gi == pick
        gmask = jnp.where(hit, 1.0, gmask)
        grp = jnp.where(hit, ninf, grp)
    emask = jnp.broadcast_to(gmask.reshape(N_GRP, 1, tm), (N_GRP, gsz, tm)).reshape(N_EXP, tm)
    msel = jnp.where(emask > 0.5, sel, ninf)
    ei = lax.broadcasted_iota(jnp.int32, msel.shape, 0)
    chosen = jnp.zeros(msel.shape, F32)
    for _ in range(TOP_K):
        m = jnp.max(msel, axis=0, keepdims=True)
        pick = jnp.min(jnp.where(msel == m, ei, N_EXP), axis=0, keepdims=True)
        hit = ei == pick
        chosen = jnp.where(hit, 1.0, chosen)
        msel = jnp.where(hit, ninf, msel)
    w = chosen * sc
    wt = w / jnp.sum(w, axis=0, keepdims=True) * ROUTED_SCALE
    wt_ref[...] = wt
    cnt_ref[...] = jnp.sum((wt > 0.0).astype(F32), axis=1, keepdims=True).astype(jnp.int32)


def moe_router2(x, M, g, mods, mp):
    tm = MOE_T
    full = lambda shape: pl.BlockSpec(shape, lambda i: (0,) * len(shape))
    return pl.pallas_call(
        _router2_kernel,
        out_shape=(jax.ShapeDtypeStruct((M, D), BF16), jax.ShapeDtypeStruct((M // tm, N_EXP, tm), F32),
                   jax.ShapeDtypeStruct((M // tm, N_EXP, 1), jnp.int32), jax.ShapeDtypeStruct((M, D), F32)),
        grid=(M // tm,),
        in_specs=[pl.BlockSpec((tm, D), lambda i: (i, 0)), full((1, D)),
                  pl.BlockSpec((None, 6, D), lambda i: (_seq_of_rowblock(i, tm), 0, 0)),
                  full((N_EXP, D)), full((N_EXP, 1)), full((D, EXP_FF)), full((D, EXP_FF)), full((EXP_FF, D))],
        out_specs=(pl.BlockSpec((tm, D), lambda i: (i, 0)), pl.BlockSpec((None, N_EXP, tm), lambda i: (i, 0, 0)),
                   pl.BlockSpec((None, N_EXP, 1), lambda i: (i, 0, 0)), pl.BlockSpec((tm, D), lambda i: (i, 0))),
        compiler_params=_cp(("parallel",)), name="moe_router",
    )(x, g.reshape(1, D), mods, jnp.transpose(mp['router_w']), mp['router_bias'].reshape(N_EXP, 1),
      mp['s1'].astype(BF16), mp['s3'].astype(BF16), mp['s2'].astype(BF16))


def _moe3_kernel(cnt_ref, ovf_ref, t_ref, wt_ref, w1_ref, w3_ref, w2_ref, o_ref, rank_sc, *, nsub):
    i = pl.program_id(0)
    eb = pl.program_id(1)
    T, CAP, EPS = MOE_T, MOE_CAP, MOE_EPS

    @pl.when(eb == 0)
    def _():
        before = (lax.broadcasted_iota(jnp.int32, (T, T), 0) < lax.broadcasted_iota(jnp.int32, (T, T), 1))
        before = before.astype(BF16)
        for s in range(nsub):
            rank_sc[s] = _dot((wt_ref[s] > 0.0).astype(BF16), before)
        o_ref[...] = jnp.zeros_like(o_ref)

    slot = lax.broadcasted_iota(jnp.int32, (CAP, T), 0).astype(F32)

    def one_hot(s, e, first_slot):
        w_row = wt_ref[s, pl.ds(e, 1), :]
        r_row = rank_sc[s, pl.ds(e, 1), :]
        hit = ((r_row - first_slot) == slot) & (w_row > 0.0)
        w_slot = jnp.sum(jnp.where(hit, w_row, 0.0), axis=1, keepdims=True)
        return hit.astype(F32).astype(BF16), w_slot

    def swiglu(xg, j):
        h = _silu(_dot(xg, w1_ref[j])) * _dot(xg, w3_ref[j])
        return _dot(h.astype(BF16), w2_ref[j])

    hot = [[one_hot(s, eb * EPS + j, 0.0) for j in range(EPS)] for s in range(nsub)]
    pb = [jnp.concatenate([hot[s][j][0] for j in range(EPS)], axis=0) for s in range(nsub)]
    xg = [_dot(pb[s], t_ref[s * T:(s + 1) * T, :]).astype(BF16) for s in range(nsub)]
    y = [swiglu(jnp.concatenate([xg[s][j * CAP:(j + 1) * CAP] for s in range(nsub)], axis=0), j)
         for j in range(EPS)]
    for s in range(nsub):
        yw = jnp.concatenate([y[j][s * CAP:(s + 1) * CAP] * hot[s][j][1] for j in range(EPS)], axis=0)
        o_ref[s * T:(s + 1) * T, :] += lax.dot_general(pb[s], yw.astype(BF16), (((0,), (0,)), ((), ())),
                                                       preferred_element_type=F32)

    def pair(idx, carry):
        s = idx // EPS
        j = idx % EPS
        e = eb * EPS + j
        n_tok = cnt_ref[(i * nsub + s) * N_EXP + e]
        rows = pl.ds(pl.multiple_of(s * T, T), T)

        def chunk(ci, c2):
            p1, w_slot = one_hot(s, e, (ci * CAP).astype(F32))
            yw = (swiglu(_dot(p1, t_ref[rows, :]).astype(BF16), j) * w_slot).astype(BF16)
            o_ref[rows, :] += lax.dot_general(p1, yw, (((0,), (0,)), ((), ())), preferred_element_type=F32)
            return c2

        lax.fori_loop(1, (n_tok + CAP - 1) // CAP, chunk, 0)
        return carry

    @pl.when(ovf_ref[i * (N_EXP // EPS) + eb] > 0)
    def _():
        lax.fori_loop(0, nsub * EPS, pair, 0)


def _moe_out_kernel(r_ref, sh_ref, x_ref, g_ref, mod_ref, o_ref):
    f = r_ref[...] + sh_ref[...]
    ms = jnp.mean(f * f, axis=-1, keepdims=True)
    o_ref[...] = x_ref[...] + mod_ref[5:6, :] * (f * lax.rsqrt(ms + EPS) * g_ref[...])


def moe_layer3(x, M, g_pre, g_post, mods, mp, nsub):
    t, wt, cnt, sh = moe_router2(x, M, g_pre, mods, mp)
    T = MOE_T
    TS = nsub * T
    ovf = jnp.any(cnt.reshape(M // TS, nsub, N_EXP // MOE_EPS, MOE_EPS) > MOE_CAP, axis=(1, 3)).astype(jnp.int32)
    grid_spec = pltpu.PrefetchScalarGridSpec(
        num_scalar_prefetch=2, grid=(M // TS, N_EXP // MOE_EPS),
        in_specs=[pl.BlockSpec((TS, D), lambda i, e, c, o: (i, 0)),
                  pl.BlockSpec((nsub, N_EXP, T), lambda i, e, c, o: (i, 0, 0)),
                  pl.BlockSpec((MOE_EPS, D, EXP_FF), lambda i, e, c, o: (e, 0, 0)),
                  pl.BlockSpec((MOE_EPS, D, EXP_FF), lambda i, e, c, o: (e, 0, 0)),
                  pl.BlockSpec((MOE_EPS, EXP_FF, D), lambda i, e, c, o: (e, 0, 0))],
        out_specs=pl.BlockSpec((TS, D), lambda i, e, c, o: (i, 0)),
        scratch_shapes=[pltpu.VMEM((nsub, N_EXP, T), F32)])
    routed = pl.pallas_call(
        functools.partial(_moe3_kernel, nsub=nsub), out_shape=jax.ShapeDtypeStruct((M, D), F32),
        grid_spec=grid_spec, compiler_params=_cp(("parallel", "arbitrary")), name="moe_experts",
    )(cnt.reshape(-1), ovf.reshape(-1), t, wt, mp['w1'].astype(BF16), mp['w3'].astype(BF16),
      mp['w2'].astype(BF16))
    tm = 512
    return pl.pallas_call(
        _moe_out_kernel, out_shape=jax.ShapeDtypeStruct((M, D), F32), grid=(M // tm,),
        in_specs=[pl.BlockSpec((tm, D), lambda i: (i, 0)), pl.BlockSpec((tm, D), lambda i: (i, 0)),
                  pl.BlockSpec((tm, D), lambda i: (i, 0)), pl.BlockSpec((1, D), lambda i: (0, 0)),
                  pl.BlockSpec((None, 6, D), lambda i: (_seq_of_rowblock(i, tm), 0, 0))],
        out_specs=pl.BlockSpec((tm, D), lambda i: (i, 0)),
        compiler_params=_cp(("parallel",)), name="moe_output")(routed, sh, x, g_post.reshape(1, D), mods)


def kernel(x, c, ctx, c_ctx, mod_w, mod_b, norm_mix_pre, norm_mix_post, norm_ffn_pre, norm_ffn_post, router_w, router_bias, expert_w1, expert_w3, expert_w2, shared_w1, shared_w3, shared_w2, ev_w_in, ev_w_out, ssd_conv_w, ssd_conv_b, ssd_dt_bias, ssd_a_log, ssd_d, ssd_norm_w, hy_conv_w, hy_conv_b, hy_mlp_w0, hy_mlp_b0, hy_freq0, hy_mlp_w1, hy_mlp_b1, hy_freq1, hy_mlp_w2, hy_bias, od_w_in, od_w_out, rw_mu, rw_w0, rw_w_up, rw_a0, rw_a_up, rw_g_up, rw_k_k, rw_k_a, rw_r_k, rw_ln_w, rw_ln_b, at_q_norm, at_k_norm):
    xs = jnp.concatenate([x.reshape(N_LAT, D), ctx.reshape(BATCH * CTX, D)], axis=0)
    cvecs = jnp.zeros((8, D), F32).at[0:BATCH].set(c).at[BATCH].set(c_ctx)
    assert mod_w.shape[0] == 2, "one even (SSD | Hyena) layer followed by one odd (RWKV | attention) layer"

    def moe_params(i):
        return dict(router_w=router_w[i], router_bias=router_bias[i], w1=expert_w1[i], w3=expert_w3[i],
                    w2=expert_w2[i], s1=shared_w1[i], s3=shared_w3[i], s2=shared_w2[i])

    mods = modulation(cvecs, mod_w[0], mod_b[0])[:BATCH + 1].reshape(BATCH + 1, 6, D)
    ep = dict(w_in=ev_w_in[0], w_out=ev_w_out[0], ssd_conv_w=ssd_conv_w[0], ssd_conv_b=ssd_conv_b[0],
              ssd_dt_bias=ssd_dt_bias[0], ssd_a_log=ssd_a_log[0], ssd_d=ssd_d[0], ssd_norm_w=ssd_norm_w[0],
              hy_conv_w=hy_conv_w[0], hy_conv_b=hy_conv_b[0], hy_mlp_w0=hy_mlp_w0[0], hy_mlp_b0=hy_mlp_b0[0],
              hy_freq0=hy_freq0[0], hy_mlp_w1=hy_mlp_w1[0], hy_mlp_b1=hy_mlp_b1[0], hy_freq1=hy_freq1[0],
              hy_mlp_w2=hy_mlp_w2[0], hy_bias=hy_bias[0])
    xs = even_mixer(xs, mods, norm_mix_pre[0], norm_mix_post[0], ep)
    xs = moe_layer3(xs, N_ROWS, norm_ffn_pre[0], norm_ffn_post[0], mods, moe_params(0), 6)
    mods = modulation(cvecs, mod_w[1], mod_b[1])[:BATCH + 1].reshape(BATCH + 1, 6, D)
    op = dict(w_in=od_w_in[0], w_out=od_w_out[0], mu=rw_mu[0], w0=rw_w0[0], w_up=rw_w_up[0], a0=rw_a0[0],
              a_up=rw_a_up[0], g_up=rw_g_up[0], k_k=rw_k_k[0], k_a=rw_k_a[0], r_k=rw_r_k[0], ln_w=rw_ln_w[0],
              ln_b=rw_ln_b[0], q_norm=at_q_norm[0], k_norm=at_k_norm[0])
    xl = odd_mixer(xs, mods, norm_mix_pre[1], norm_mix_post[1], op)
    xl = moe_layer3(xl, N_LAT, norm_ffn_pre[1], norm_ffn_post[1], mods, moe_params(1), 8)
    return xl.reshape(BATCH, SEQ, D)
```

```python
import functools
import math

import jax
import jax.numpy as jnp
from jax import lax
from jax.experimental import pallas as pl
from jax.experimental.pallas import tpu as pltpu

F32 = jnp.float32
BF16 = jnp.bfloat16

D = 1024
BATCH = 2
SEQ = 8192
CTX = 256
N_LAT = BATCH * SEQ
N_ROWS = N_LAT + BATCH * CTX
EPS = 1e-6
GRID_W = 64

SSD_HEADS = 16
SSD_P = 64
SSD_G = 2
SSD_S = 128
SSD_Q = 128
HY_W = 1024
HY_EMB = 33
HY_HID = 64

RW_H = 16
RW_N = 64
RW_CHUNK = 128
RW_GN_EPS = 64e-5

AT_KV = 4
AT_HD = 64

N_EXP = 64
TOP_K = 8
N_GRP = 8
TOPK_GRP = 4
EXP_FF = 256
ROUTED_SCALE = 2.5

VMEM_LIMIT = 56 * 1024 * 1024


def _cp(sem, vmem=None):
    return pltpu.CompilerParams(dimension_semantics=sem, vmem_limit_bytes=vmem or VMEM_LIMIT)


def _dot(a, b):
    return jnp.dot(a, b, preferred_element_type=F32)


def _dot_nt(a, b):
    return lax.dot_general(a, b, (((1,), (1,)), ((), ())), preferred_element_type=F32)


def _split(x):
    hi = x.astype(BF16)
    lo = (x - hi.astype(F32)).astype(BF16)
    return hi, lo


def _dot3(a, b):
    ah, al = _split(a)
    bh, bl = _split(b)
    return _dot(ah, bh) + (_dot(ah, bl) + _dot(al, bh))


def _dot2l(a, b):
    ah, al = _split(a)
    return _dot(ah, b) + _dot(al, b)


def _dot2r(a, b):
    bh, bl = _split(b)
    return _dot(a, bh) + _dot(a, bl)


def _silu(x):
    return x * (1.0 / (1.0 + jnp.exp(-x)))


def _sigmoid(x):
    return 1.0 / (1.0 + jnp.exp(-x))


def _softplus(x):
    return jnp.maximum(x, 0.0) + jnp.log(1.0 + jnp.exp(-jnp.abs(x)))


def _seq_of_rowblock(i, tm):
    return jnp.minimum((i * tm) // SEQ, 2)


def _nmm_kernel(x_ref, g_ref, mod_ref, w_ref, o_ref, a_sc, *, shift_i, scale_i):
    @pl.when(pl.program_id(1) == 0)
    def _():
        x = x_ref[...]
        ms = jnp.mean(x * x, axis=-1, keepdims=True)
        y = x * lax.rsqrt(ms + EPS) * g_ref[...]
        h = y * (1.0 + mod_ref[scale_i:scale_i + 1, :]) + mod_ref[shift_i:shift_i + 1, :]
        a_sc[...] = h.astype(BF16)

    o_ref[...] = _dot(a_sc[...], w_ref[...])


def norm_mod_matmul(x, g, mods, w, shift_i, scale_i, tm=512, tn=None, name="nmm"):
    M = x.shape[0]
    N = w.shape[1]
    tn = tn or N
    return pl.pallas_call(
        functools.partial(_nmm_kernel, shift_i=shift_i, scale_i=scale_i),
        out_shape=jax.ShapeDtypeStruct((M, N), F32),
        grid=(M // tm, N // tn),
        in_specs=[pl.BlockSpec((tm, D), lambda i, j: (i, 0)),
                  pl.BlockSpec((1, D), lambda i, j: (0, 0)),
                  pl.BlockSpec((None, 6, D), lambda i, j: (_seq_of_rowblock(i, tm), 0, 0)),
                  pl.BlockSpec((D, tn), lambda i, j: (0, j))],
        out_specs=pl.BlockSpec((tm, tn), lambda i, j: (i, j)),
        scratch_shapes=[pltpu.VMEM((tm, D), BF16)],
        compiler_params=_cp(("parallel", "arbitrary")), name=name)(x, g.reshape(1, D), mods, w)


def _outproj_kernel(a1_ref, a2_ref, w_ref, x_ref, g_ref, mod_ref, o_ref, *, gate_i):
    y = _dot(a1_ref[...].astype(BF16), w_ref[0:D, :]) + _dot(a2_ref[...].astype(BF16), w_ref[D:2 * D, :])
    ms = jnp.mean(y * y, axis=-1, keepdims=True)
    o_ref[...] = x_ref[...] + mod_ref[gate_i:gate_i + 1, :] * (y * lax.rsqrt(ms + EPS) * g_ref[...])


def outproj_residual(a1, a2, w, x, g, mods, gate_i, tm=512, name="outproj"):
    M = a1.shape[0]
    return pl.pallas_call(
        functools.partial(_outproj_kernel, gate_i=gate_i),
        out_shape=jax.ShapeDtypeStruct((M, D), F32),
        grid=(M // tm,),
        in_specs=[pl.BlockSpec((tm, D), lambda i: (i, 0)),
                  pl.BlockSpec((tm, D), lambda i: (i, 0)),
                  pl.BlockSpec((2 * D, D), lambda i: (0, 0)),
                  pl.BlockSpec((tm, D), lambda i: (i, 0)),
                  pl.BlockSpec((1, D), lambda i: (0, 0)),
                  pl.BlockSpec((None, 6, D), lambda i: (_seq_of_rowblock(i, tm), 0, 0))],
        out_specs=pl.BlockSpec((tm, D), lambda i: (i, 0)),
        compiler_params=_cp(("parallel",)), name=name)(a1, a2, w, x, g.reshape(1, D), mods)


def _mod_kernel(c_ref, w_ref, b_ref, o_ref):
    o_ref[...] = _dot3(_silu(c_ref[...]), w_ref[...]) + b_ref[...]


def modulation(cvecs, w, b):
    N = w.shape[1]
    tn = 1024
    return pl.pallas_call(
        _mod_kernel, out_shape=jax.ShapeDtypeStruct((8, N), F32), grid=(N // tn,),
        in_specs=[pl.BlockSpec((8, D), lambda j: (0, 0)),
                  pl.BlockSpec((D, tn), lambda j: (0, j)),
                  pl.BlockSpec((1, tn), lambda j: (0, j))],
        out_specs=pl.BlockSpec((8, tn), lambda j: (0, j)),
        compiler_params=_cp(("parallel",)), name="modulation")(cvecs, w, b.reshape(1, N))


CONV_TM = 256


def _conv3_kernel(x_ref, prev_ref, next_ref, w_ref, b_ref, o_ref, *, act):
    tm = CONV_TM
    row0 = pl.program_id(0) * tm
    seq_len = jnp.where(row0 < N_LAT, SEQ, CTX)
    pos = jnp.where(row0 < N_LAT, row0 % SEQ, (row0 - N_LAT) % CTX)
    cur = x_ref[...]
    rows = lax.broadcasted_iota(jnp.int32, cur.shape, 0)
    prev_row = prev_ref[7:8, :] * (pos > 0).astype(F32)
    next_row = next_ref[0:1, :] * (pos + tm < seq_len).astype(F32)
    xm1 = jnp.where(rows == 0, prev_row, pltpu.roll(cur, 1, 0))
    xp1 = jnp.where(rows == tm - 1, next_row, pltpu.roll(cur, tm - 1, 0))
    y = xm1 * w_ref[0:1, :] + cur * w_ref[1:2, :] + xp1 * w_ref[2:3, :] + b_ref[...]
    o_ref[...] = _silu(y) if act else y


def dwconv3(p, col0, ncols, w, b, act, name, cb=1024):
    tm = CONV_TM
    cb = math.gcd(cb, math.gcd(col0, ncols)) if col0 else math.gcd(cb, ncols)
    r8 = tm // 8
    n8 = N_ROWS // 8
    c0 = col0 // cb
    return pl.pallas_call(
        functools.partial(_conv3_kernel, act=act),
        out_shape=jax.ShapeDtypeStruct((N_ROWS, ncols), F32),
        grid=(N_ROWS // tm, ncols // cb),
        in_specs=[pl.BlockSpec((tm, cb), lambda i, j: (i, c0 + j)),
                  pl.BlockSpec((8, cb), lambda i, j: (jnp.maximum(i * r8 - 1, 0), c0 + j)),
                  pl.BlockSpec((8, cb), lambda i, j: (jnp.minimum((i + 1) * r8, n8 - 1), c0 + j)),
                  pl.BlockSpec((3, cb), lambda i, j: (0, j)),
                  pl.BlockSpec((1, cb), lambda i, j: (0, j))],
        out_specs=pl.BlockSpec((tm, cb), lambda i, j: (i, j)),
        compiler_params=_cp(("parallel", "parallel")), name=name)(p, p, p, jnp.transpose(w), b.reshape(1, ncols))


def _ssd_kernel(xs_ref, bm_ref, cm_ref, dt_ref, dtT_ref, bias_ref, biasT_ref, alog_ref, alogT_ref,
                y_ref, st_ref):
    d = pl.program_id(0)
    c = pl.program_id(2)
    Q = SSD_Q
    HG = SSD_HEADS // SSD_G

    @pl.when(c == 0)
    def _():
        st_ref[...] = jnp.zeros_like(st_ref)

    isb = d == 1
    sgn = 1 - 2 * d
    ii = lax.broadcasted_iota(jnp.int32, (Q, Q), 0)
    jj = lax.broadcasted_iota(jnp.int32, (Q, Q), 1)
    tri = (jj <= ii).astype(BF16)
    triT = (ii <= jj).astype(BF16)
    mask = sgn * (ii - jj) >= 0
    xs = xs_ref[...]
    G = range(SSD_G)
    dt = [_softplus(dt_ref[g] + bias_ref[g]) for g in G]
    dtT = [_softplus(dtT_ref[g] + biasT_ref[g]) for g in G]
    a = [dt[g] * (-jnp.exp(alog_ref[g])) for g in G]
    aT = [dtT[g] * (-jnp.exp(alogT_ref[g])) for g in G]
    cs = [_dot2r(tri, a[g]) for g in G]
    csT = [_dot2l(aT[g], triT) for g in G]
    tot = [cs[g][Q - 1:Q, :] for g in G]
    p = [jnp.where(isb, a[g] - cs[g], cs[g]) for g in G]
    pT = [jnp.where(isb, aT[g] - csT[g], csT[g]) for g in G]
    dec_out = [jnp.exp(jnp.where(isb, tot[g], 0.0) + p[g]) for g in G]
    dec_state = [jnp.exp(jnp.where(isb, 0.0, tot[g]) - p[g]) for g in G]
    chunk_dec = [jnp.exp(tot[g]) for g in G]
    bm = [bm_ref[:, g * SSD_S:(g + 1) * SSD_S].astype(BF16) for g in G]
    cm = [cm_ref[:, g * SSD_S:(g + 1) * SSD_S].astype(BF16) for g in G]
    cb = [_dot_nt(cm[g], bm[g]) for g in G]
    nh = SSD_HEADS

    def spread(cols, width):
        v = jnp.concatenate(cols, axis=1)
        head = lax.broadcasted_iota(jnp.int32, (nh, nh * width), 1) // width
        e = (head == lax.broadcasted_iota(jnp.int32, (nh, nh * width), 0)).astype(BF16)
        h1 = v.astype(BF16)
        r1 = v - h1.astype(F32)
        h2 = r1.astype(BF16)
        h3 = (r1 - h2.astype(F32)).astype(BF16)
        return _dot(h1, e) + (_dot(h2, e) + _dot(h3, e))

    dt_x = spread(dt, SSD_P)
    dout_x = spread(dec_out, SSD_P)
    dst_x = spread(dec_state, SSD_P)
    p_x = spread(p, Q)
    xh_all = xs * dt_x
    xdec_all = (xh_all * dst_x).astype(BF16)
    xh_all = xh_all.astype(BF16)
    GH = [(g, h) for g in G for h in range(HG)]
    NH = range(len(GH))
    lm = [(cb[g] * jnp.exp(jnp.where(mask, p_x[:, n * Q:(n + 1) * Q] - pT[g][h:h + 1, :], -1e30))).astype(BF16)
          for n, (g, h) in enumerate(GH)]
    s_old = [st_ref[n] for n in NH]
    y_in = [_dot(lm[n], xh_all[:, n * SSD_P:(n + 1) * SSD_P]) for n in NH]
    y_st = [_dot(cm[g], s_old[n].astype(BF16)) for n, (g, h) in enumerate(GH)]
    upd = [lax.dot_general(bm[g], xdec_all[:, n * SSD_P:(n + 1) * SSD_P], (((0,), (0,)), ((), ())),
                           preferred_element_type=F32) for n, (g, h) in enumerate(GH)]
    for n, (g, h) in enumerate(GH):
        st_ref[n] = chunk_dec[g][:, h:h + 1] * s_old[n] + upd[n]
    y_ref[...] = jnp.concatenate(y_in, axis=1) + dout_x * jnp.concatenate(y_st, axis=1)


def _ssd_rowblock(d, b, c):
    n_ctx = CTX // SSD_Q
    n_lat = SEQ // SSD_Q
    cc = jnp.where(d == 0, c, n_ctx - 1 - c)
    lc = jnp.where(d == 0, c - n_ctx, n_ctx + n_lat - 1 - c)
    return jnp.where(c < n_ctx, N_LAT // SSD_Q + b * n_ctx + cc, b * n_lat + lc)


def ssd_scan(xbc, dt_raw, dt_bias, a_log):
    HG = SSD_HEADS // SSD_G
    W = SSD_HEADS * SSD_P
    dsel = dt_raw[:, :2 * SSD_HEADS].reshape(N_ROWS, 2, SSD_G, HG).transpose(1, 2, 0, 3)
    dselT = dsel.transpose(0, 1, 3, 2)
    bias = dt_bias.reshape(2, SSD_G, 1, HG)
    biasT = dt_bias.reshape(2, SSD_G, HG, 1)
    alog = a_log.reshape(2, SSD_G, 1, HG)
    alogT = a_log.reshape(2, SSD_G, HG, 1)
    nch = (CTX + SEQ) // SSD_Q
    rb = _ssd_rowblock
    GS = SSD_G * SSD_S
    par = lambda shape: pl.BlockSpec((None,) + shape, lambda d, b, c: (d, 0, 0, 0))
    return pl.pallas_call(
        _ssd_kernel,
        out_shape=jax.ShapeDtypeStruct((2, N_ROWS, W), F32),
        grid=(2, BATCH, nch),
        in_specs=[pl.BlockSpec((SSD_Q, W), lambda d, b, c: (rb(d, b, c), 0)),
                  pl.BlockSpec((SSD_Q, GS), lambda d, b, c: (rb(d, b, c), W // GS)),
                  pl.BlockSpec((SSD_Q, GS), lambda d, b, c: (rb(d, b, c), W // GS + 1)),
                  pl.BlockSpec((None, SSD_G, SSD_Q, HG), lambda d, b, c: (d, 0, rb(d, b, c), 0)),
                  pl.BlockSpec((None, SSD_G, HG, SSD_Q), lambda d, b, c: (d, 0, 0, rb(d, b, c))),
                  par((SSD_G, 1, HG)), par((SSD_G, HG, 1)), par((SSD_G, 1, HG)), par((SSD_G, HG, 1))],
        out_specs=pl.BlockSpec((None, SSD_Q, W), lambda d, b, c: (d, rb(d, b, c), 0)),
        scratch_shapes=[pltpu.VMEM((SSD_HEADS, SSD_S, SSD_P), F32)],
        compiler_params=_cp(("parallel", "parallel", "arbitrary")), name="ssd_scan",
    )(xbc, xbc, xbc, dsel, dselT, bias, biasT, alog, alogT)


def _ssd_out_kernel(yf_ref, yb_ref, xs_ref, z_ref, dskip_ref, nw_ref, o_ref):
    y = yf_ref[...] + yb_ref[...] + xs_ref[...] * dskip_ref[...]
    y = y * _silu(z_ref[...])
    gs = SSD_HEADS * SSD_P // SSD_G
    parts = []
    for g in range(SSD_G):
        yg = y[:, g * gs:(g + 1) * gs]
        parts.append(yg * lax.rsqrt(jnp.mean(yg * yg, axis=-1, keepdims=True) + EPS))
    o_ref[...] = jnp.concatenate(parts, axis=1) * nw_ref[...]


def ssd_output(y2, xbc, p, zcol, d_skip, norm_w, tm=512):
    W = SSD_HEADS * SSD_P
    dexp = jnp.repeat(d_skip, SSD_P).reshape(1, W)
    return pl.pallas_call(
        _ssd_out_kernel, out_shape=jax.ShapeDtypeStruct((N_ROWS, W), F32), grid=(N_ROWS // tm,),
        in_specs=[pl.BlockSpec((None, tm, W), lambda i: (0, i, 0)),
                  pl.BlockSpec((None, tm, W), lambda i: (1, i, 0)),
                  pl.BlockSpec((tm, W), lambda i: (i, 0)),
                  pl.BlockSpec((tm, W), lambda i: (i, zcol // W)),
                  pl.BlockSpec((1, W), lambda i: (0, 0)),
                  pl.BlockSpec((1, W), lambda i: (0, 0))],
        out_specs=pl.BlockSpec((tm, W), lambda i: (i, 0)),
        compiler_params=_cp(("parallel",)), name="ssd_output")(y2, y2, xbc, p, dexp, norm_w.reshape(1, W))


def _hyfilt_kernel(f_ref, w0_ref, b0_ref, fr0_ref, w1_ref, b1_ref, fr1_ref, w2_ref, dl_ref, h_ref, ss_ref, *,
                   n_tiles):
    f = f_ref[...]
    h = jnp.sin(fr0_ref[...] * (_dot3(f, w0_ref[...]) + b0_ref[...]))
    h = jnp.sin(fr1_ref[...] * (_dot3(h, w1_ref[...]) + b1_ref[...]))
    h = _dot3(h, w2_ref[...])
    h = h * jnp.exp(-f[:, 0:1] * dl_ref[...])
    side = pl.program_id(0) // n_tiles
    j = pl.program_id(0) % n_tiles

    @pl.when(j == 0)
    def _():
        ss_ref[...] = jnp.zeros_like(ss_ref)

    ss_ref[...] += jnp.sum(h * h, axis=0, keepdims=True)
    row = lax.broadcasted_iota(jnp.int32, (h.shape[0], 1), 0) + j * h.shape[0]
    h_ref[...] = jnp.where((side == 1) & (row == 0), 0.0, h)


def hyena_filter_taps(L, hp):
    pos = jnp.arange(L, dtype=F32)
    t = pos / (L - 1)
    bands = (HY_EMB - 1) // 2
    freqs = jnp.linspace(1e-4, bands - 1, bands, dtype=F32)
    ang = (2.0 * math.pi / L) * pos[:, None] * freqs[None, :]
    feats = jnp.concatenate([t[:, None], jnp.cos(ang), -jnp.sin(ang)], axis=-1)
    feats = jnp.pad(feats, ((0, 0), (0, 128 - HY_EMB)))
    feats = jnp.concatenate([feats, jnp.flip(feats, axis=0)], axis=0)
    w0 = jnp.pad(hp['hy_mlp_w0'], ((0, 128 - HY_EMB), (0, 0)))
    min_decay = math.log(1e-2) / 1.5
    max_decay = math.log(1e-2) / 0.3
    deltas = jnp.abs(jnp.linspace(min_decay, max_decay, HY_W, dtype=F32))
    dl = jnp.tile(deltas, 2).reshape(1, 2 * HY_W)
    w2 = hp['hy_mlp_w2'].reshape(HY_HID, 2, 2, HY_W).transpose(0, 2, 1, 3).reshape(HY_HID, 4 * HY_W)
    tl = min(L, 512)
    n_tiles = L // tl
    NS = 2 * HY_W
    full = lambda shape: pl.BlockSpec(shape, lambda i: (0, 0))
    return pl.pallas_call(
        functools.partial(_hyfilt_kernel, n_tiles=n_tiles),
        out_shape=(jax.ShapeDtypeStruct((2 * L, NS), F32), jax.ShapeDtypeStruct((1, 2 * NS), F32)),
        grid=(2 * n_tiles,),
        in_specs=[pl.BlockSpec((tl, 128), lambda i: (i, 0)), full((128, HY_HID)), full((1, HY_HID)),
                  full((1, HY_HID)), full((HY_HID, HY_HID)), full((1, HY_HID)), full((1, HY_HID)),
                  pl.BlockSpec((HY_HID, NS), lambda i: (0, i // n_tiles)), full((1, NS))],
        out_specs=(pl.BlockSpec((tl, NS), lambda i: (i, 0)), pl.BlockSpec((1, NS), lambda i: (0, i // n_tiles))),
        compiler_params=_cp(("arbitrary",)), name="hyena_filter",
    )(feats, w0, hp['hy_mlp_b0'].reshape(1, -1), hp['hy_freq0'].reshape(1, -1), hp['hy_mlp_w1'],
      hp['hy_mlp_b1'].reshape(1, -1), hp['hy_freq1'].reshape(1, -1), w2, dl)


def _cis(num, den):
    ang = (2.0 * math.pi / den) * (num % den).astype(F32)
    return jnp.cos(ang), -jnp.sin(ang)


def _fft_consts(NB, BS):
    N = NB * BS
    h = NB // 2
    k1 = jnp.arange(h, dtype=jnp.int32)
    j = jnp.arange(NB, dtype=jnp.int32)
    re, im = _cis(j[None, :] * (2 * k1[:, None] + 1), 2 * NB)
    f1 = jnp.concatenate([re, im], axis=0)
    neg = jnp.where(j >= h, -1.0, 1.0)[None, :]
    f1_data = f1[:, :h]
    f1_filt = f1 * neg
    f1_inv = (2.0 / N) * jnp.concatenate([re[:, :h].T, im[:, :h].T], axis=1)
    r = jnp.arange(BS, dtype=jnp.int32)
    k2 = jnp.arange(BS, dtype=jnp.int32)
    kk = 2 * k1[:, None, None] + 2 * NB * k2[None, :, None] + 1
    gre, gim = _cis(kk * r[None, None, :], 2 * N)
    gf = jnp.concatenate([jnp.concatenate([gre, -gim], axis=2), jnp.concatenate([gim, gre], axis=2)], axis=1)
    gret, gimt = gre.transpose(0, 2, 1), gim.transpose(0, 2, 1)
    gi = jnp.concatenate([jnp.concatenate([gret, gimt], axis=2), jnp.concatenate([-gimt, gret], axis=2)], axis=1)
    return (f1_data.astype(BF16), f1_filt.astype(BF16), f1_inv.astype(BF16), gf.astype(BF16), gi.astype(BF16))


FFT_PAD = 8


FFT_LW = 128


def _fft_fwd_kernel(ua_ref, ub_ref, f1_ref, g_ref, o_ref, t_sc, *, NB, BS, nj, kg):
    pitch = NB + FFT_PAD
    u_refs = (ua_ref, ub_ref)

    @pl.when(pl.program_id(2) == 0)
    def _():
        f1 = f1_ref[...]

        def body(r, carry):
            xr = jnp.concatenate([u[pl.ds(r, nj, stride=BS), :] for u in u_refs], axis=1).astype(BF16)
            res = _dot(f1, xr)
            for hh in range(2):
                t_sc[hh, pl.ds(pl.multiple_of(r * pitch, 8), NB), :] = res[:, hh * FFT_LW:(hh + 1) * FFT_LW]
            return carry

        lax.fori_loop(0, BS, body, 0, unroll=8)

    k0 = pl.program_id(2) * kg
    for i in range(kg):
        are = jnp.concatenate([t_sc[hh, pl.ds(k0 + i, BS, stride=pitch), :] for hh in range(2)], axis=1)
        aim = jnp.concatenate([t_sc[hh, pl.ds(k0 + i + NB // 2, BS, stride=pitch), :] for hh in range(2)], axis=1)
        a = jnp.concatenate([are, aim], axis=0).astype(BF16)
        o_ref[i] = _dot(g_ref[i], a)


def fft_fwd(u, col0, nbatch, nj, f1, gf, NB, BS, kg=8):
    h = NB // 2
    kg = min(kg, h)
    lw = FFT_LW
    ct = 2 * lw
    return pl.pallas_call(
        functools.partial(_fft_fwd_kernel, NB=NB, BS=BS, nj=nj, kg=kg),
        out_shape=jax.ShapeDtypeStruct((nbatch, h, 2 * BS, HY_W), F32),
        grid=(nbatch, HY_W // ct, h // kg),
        in_specs=[pl.BlockSpec((nj * BS, lw), lambda b, c, k: (b, col0 // lw + 2 * c), pipeline_mode=pl.Buffered(1)),
                  pl.BlockSpec((nj * BS, lw), lambda b, c, k: (b, col0 // lw + 2 * c + 1),
                               pipeline_mode=pl.Buffered(1)),
                  pl.BlockSpec((NB, nj), lambda b, c, k: (0, 0)),
                  pl.BlockSpec((kg, 2 * BS, 2 * BS), lambda b, c, k: (k, 0, 0))],
        out_specs=pl.BlockSpec((None, kg, 2 * BS, ct), lambda b, c, k: (b, k, 0, c)),
        scratch_shapes=[pltpu.VMEM((2, BS * (NB + FFT_PAD), lw), F32)],
        compiler_params=_cp(("parallel", "parallel", "arbitrary")), name="hyena_fft_fwd")(u, u, f1, gf)


def _cmul(u, h, half):
    ure, uim = u[:half], u[half:]
    hre, him = h[:half], h[half:]
    return jnp.concatenate([ure * hre - uim * him, ure * him + uim * hre], axis=0)


def _fft_inv_kernel(us_ref, hs_ref, gi_ref, f1i_ref, o_ref, t_sc, y_sc, *, NB, BS, kg):
    ks = pl.program_id(2)
    pitch = 2 * BS + FFT_PAD
    for i in range(kg):
        y = _cmul(us_ref[i], hs_ref[i], BS).astype(BF16)
        row = pl.multiple_of((ks * kg + i) * pitch, 8)
        res = _dot(gi_ref[i], y)
        for hh in range(2):
            t_sc[hh, pl.ds(row, 2 * BS), :] = res[:, hh * FFT_LW:(hh + 1) * FFT_LW]

    @pl.when(ks == pl.num_programs(2) - 1)
    def _():
        f1i = f1i_ref[...]

        def body(r, carry):
            bre = jnp.concatenate([t_sc[hh, pl.ds(r, NB // 2, stride=pitch), :] for hh in range(2)], axis=1)
            bim = jnp.concatenate([t_sc[hh, pl.ds(r + BS, NB // 2, stride=pitch), :] for hh in range(2)], axis=1)
            b = jnp.concatenate([bre, bim], axis=0).astype(BF16)
            res = _dot(f1i, b)
            for hh in range(2):
                y_sc[hh, pl.ds(r, NB // 2, stride=BS), :] = res[:, hh * FFT_LW:(hh + 1) * FFT_LW]
            return carry

        lax.fori_loop(0, BS, body, 0, unroll=8)
        o_ref[...] = jnp.concatenate([y_sc[0], y_sc[1]], axis=1)


def fft_inv(us, hs, gi, f1i, NB, BS, kg=8):
    nbatch, h = us.shape[0], NB // 2
    kg = min(kg, h)
    L = h * BS
    ct = 2 * FFT_LW
    return pl.pallas_call(
        functools.partial(_fft_inv_kernel, NB=NB, BS=BS, kg=kg),
        out_shape=jax.ShapeDtypeStruct((nbatch * L, HY_W), F32),
        grid=(nbatch, HY_W // ct, h // kg),
        in_specs=[pl.BlockSpec((None, kg, 2 * BS, ct), lambda b, c, k: (b, k, 0, c)),
                  pl.BlockSpec((None, kg, 2 * BS, ct), lambda b, c, k: (0, k, 0, c)),
                  pl.BlockSpec((kg, 2 * BS, 2 * BS), lambda b, c, k: (k, 0, 0)),
                  pl.BlockSpec((h, NB), lambda b, c, k: (0, 0))],
        out_specs=pl.BlockSpec((L, ct), lambda b, c, k: (b, c)),
        scratch_shapes=[pltpu.VMEM((2, h * (2 * BS + FFT_PAD), FFT_LW), F32), pltpu.VMEM((2, L, FFT_LW), F32)],
        compiler_params=_cp(("parallel", "parallel", "arbitrary")), name="hyena_fft_inv")(us, hs, gi, f1i)


def _dft_consts(L):
    N = 2 * L
    k = jnp.arange(L, dtype=jnp.int32)
    n = jnp.arange(N, dtype=jnp.int32)
    re, im = _cis(n[None, :] * (2 * k[:, None] + 1), 2 * N)
    f = jnp.concatenate([re, im], axis=0)
    neg = jnp.where(n >= L, -1.0, 1.0)[None, :]
    fi = (2.0 / N) * jnp.concatenate([re[:, :L].T, im[:, :L].T], axis=1)
    return f[:, :L].astype(BF16), (f * neg).astype(BF16), fi.astype(BF16)


def _cdft_kernel(f_ref, x_ref, o_ref):
    o_ref[...] = _dot(f_ref[...], x_ref[...].astype(BF16))


def dft_fwd(x, f, row0, col0, nbatch, ct=256):
    M, K = f.shape
    return pl.pallas_call(
        _cdft_kernel, out_shape=jax.ShapeDtypeStruct((nbatch, M, HY_W), F32),
        grid=(nbatch, HY_W // ct),
        in_specs=[pl.BlockSpec((M, K), lambda b, c: (0, 0)),
                  pl.BlockSpec((K, ct), lambda b, c: (row0 // K + b, col0 // ct + c))],
        out_specs=pl.BlockSpec((None, M, ct), lambda b, c: (b, 0, c)),
        compiler_params=_cp(("parallel", "parallel")), name="hyena_dft_fwd")(f, x)


def _cdft_inv_kernel(us_ref, hs_ref, fi_ref, o_ref):
    half = us_ref.shape[0] // 2
    o_ref[...] = _dot(fi_ref[...], _cmul(us_ref[...], hs_ref[...], half).astype(BF16))


def dft_inv(us, hs, fi, ct=256):
    nbatch, M2, _ = us.shape
    L = fi.shape[0]
    return pl.pallas_call(
        _cdft_inv_kernel, out_shape=jax.ShapeDtypeStruct((nbatch * L, HY_W), F32),
        grid=(nbatch, HY_W // ct),
        in_specs=[pl.BlockSpec((None, M2, ct), lambda b, c: (b, 0, c)),
                  pl.BlockSpec((None, M2, ct), lambda b, c: (0, 0, c)),
                  pl.BlockSpec((L, M2), lambda b, c: (0, 0))],
        out_specs=pl.BlockSpec((L, ct), lambda b, c: (b, c)),
        compiler_params=_cp(("parallel", "parallel")), name="hyena_dft_inv")(us, hs, fi)


def _hy_gate_kernel(g_ref, y_ref, u_ref, ss_ref, b_ref, o_ref):
    scale = lax.rsqrt(ss_ref[0:1, :] + ss_ref[1:2, :] + 1e-6)
    o_ref[...] = g_ref[...] * (y_ref[...] * scale + u_ref[...] * b_ref[...])


def _hy_gate2_kernel(g_ref, yl_ref, yc_ref, ul_ref, uc_ref, ssl_ref, ssc_ref, b_ref, o_ref, *, n_lat_t):
    is_lat = pl.program_id(0) < n_lat_t
    y = jnp.where(is_lat, yl_ref[...], yc_ref[...])
    uin = jnp.where(is_lat, ul_ref[...], uc_ref[...])
    ss = jnp.where(is_lat, ssl_ref[...], ssc_ref[...])
    scale = lax.rsqrt(ss[0:1, :] + ss[1:2, :] + 1e-6)
    o_ref[...] = g_ref[...] * (y * scale + uin * b_ref[...])


def hy_gate(gate, gcol, grow, y, uin, ucol, urow, ss, order, bias, tm=512):
    M = y.shape[0]
    return pl.pallas_call(
        _hy_gate_kernel, out_shape=jax.ShapeDtypeStruct((M, HY_W), F32), grid=(M // tm,),
        in_specs=[pl.BlockSpec((tm, HY_W), lambda i: (grow // tm + i, gcol // HY_W)),
                  pl.BlockSpec((tm, HY_W), lambda i: (i, 0)),
                  pl.BlockSpec((tm, HY_W), lambda i: (urow // tm + i, ucol // HY_W)),
                  pl.BlockSpec((None, 2, HY_W), lambda i: (order, 0, 0)),
                  pl.BlockSpec((None, 1, HY_W), lambda i: (order, 0, 0))],
        out_specs=pl.BlockSpec((tm, HY_W), lambda i: (i, 0)),
        compiler_params=_cp(("parallel",)), name="hyena_gate")(gate, y, uin, ss, bias)


def hyena(u, hp):
    C = HY_W
    bias = hp['hy_bias'].reshape(2, 1, C)
    NB = BS = int(round(math.sqrt(2 * SEQ)))
    f1d, f1f, f1i, gf, gi = _fft_consts(NB, BS)
    taps, ss = hyena_filter_taps(SEQ, hp)
    ss = ss.reshape(2, 2, C).transpose(1, 0, 2)
    conv_l = lambda zin, zcol, order: fft_inv(fft_fwd(zin, zcol, BATCH, NB // 2, f1d, gf, NB, BS),
                                              fft_fwd(taps, order * C, 1, NB, f1f, gf, NB, BS), gi, f1i, NB, BS)
    z1_lat = hy_gate(u, 0, 0, conv_l(u, 2 * C, 0), u, 2 * C, 0, ss, 0, bias)
    y2_lat = conv_l(z1_lat, 0, 1)
    fd, ff, fi = _dft_consts(CTX)
    taps_c, ss_c = hyena_filter_taps(CTX, hp)
    ss_c = ss_c.reshape(2, 2, C).transpose(1, 0, 2)
    conv_c = lambda zin, zrow, zcol, order: dft_inv(dft_fwd(zin, fd, zrow, zcol, BATCH),
                                                    dft_fwd(taps_c, ff, 0, order * C, 1), fi)
    z1_ctx = hy_gate(u, 0, N_LAT, conv_c(u, N_LAT, 2 * C, 0), u, 2 * C, N_LAT, ss_c, 0, bias)
    y2_ctx = conv_c(z1_ctx, 0, 0, 1)
    tm = 512
    n_lat_t = N_LAT // tm
    lat = lambda i: (jnp.minimum(i, n_lat_t - 1), 0)
    ctx = lambda i: (jnp.maximum(i - n_lat_t, 0), 0)
    return pl.pallas_call(
        functools.partial(_hy_gate2_kernel, n_lat_t=n_lat_t),
        out_shape=jax.ShapeDtypeStruct((N_ROWS, C), F32), grid=(N_ROWS // tm,),
        in_specs=[pl.BlockSpec((tm, C), lambda i: (i, 1)),
                  pl.BlockSpec((tm, C), lat), pl.BlockSpec((tm, C), ctx),
                  pl.BlockSpec((tm, C), lat), pl.BlockSpec((tm, C), ctx),
                  pl.BlockSpec((None, 2, C), lambda i: (1, 0, 0)), pl.BlockSpec((None, 2, C), lambda i: (1, 0, 0)),
                  pl.BlockSpec((None, 1, C), lambda i: (1, 0, 0))],
        out_specs=pl.BlockSpec((tm, C), lambda i: (i, 0)),
        compiler_params=_cp(("parallel",)), name="hyena_gate2",
    )(u, y2_lat, y2_ctx, z1_lat, z1_ctx, ss, ss_c, bias)


EV_SSD_IN = SSD_HEADS * SSD_P
EV_XBC = EV_SSD_IN + 2 * SSD_G * SSD_S
EV_PAD_N = 5760


def even_mixer(x, mods, g_pre, g_post, ep):
    o1 = EV_SSD_IN
    o2 = o1 + EV_XBC
    o3 = o2 + 2 * SSD_HEADS
    w = ep['w_in']
    n_in = w.shape[1]
    hw = 3 * HY_W
    w_perm = jnp.concatenate([w[:, o3:], w[:, :o2], w[:, o2:o3],
                              jnp.zeros((D, EV_PAD_N - n_in), F32)], axis=1).astype(BF16)
    p = norm_mod_matmul(x, g_pre, mods, w_perm, 0, 1, tn=1920, name="even_in_proj")
    xbc = dwconv3(p, hw + o1, EV_XBC, ep['ssd_conv_w'], ep['ssd_conv_b'], True, "ssd_conv")
    u = dwconv3(p, 0, hw, ep['hy_conv_w'], ep['hy_conv_b'], False, "hyena_conv")
    dt_raw = p[:, hw + o2:hw + o2 + 2 * SSD_HEADS]
    y2 = ssd_scan(xbc, dt_raw, ep['ssd_dt_bias'], ep['ssd_a_log'])
    s = ssd_output(y2, xbc, p, hw, ep['ssd_d'], ep['ssd_norm_w'])
    zh = hyena(u, ep)
    return outproj_residual(s, zh, ep['w_out'].astype(BF16), x, g_post, mods, 2, name="even_out_proj")


def _head_sum(x, e, et):
    return _dot2l(_dot2l(x, e), et)


def _rw_prep_kernel(r_ref, k_ref, v_ref, lo_ref, w0_ref, wup_ref, a0_ref, aup_ref, gup_ref, kk_ref, ka_ref,
                    rk_ref, e_ref, et_ref, lw_ref, kd_ref, be_ref, kap_ref, g_ref, bonus_ref):
    r, k, v = r_ref[...], k_ref[...], v_ref[...]
    lo = lo_ref[...]
    wc, ac, gc = lo[:, 0:64], lo[:, 64:128], lo[:, 128:384]
    e, et = e_ref[...], et_ref[...]
    kk = k * kk_ref[...]
    kap = kk * lax.rsqrt(_head_sum(kk * kk, e, et) + 1e-12)
    kap_ref[...] = kap
    g_ref[...] = _dot(_sigmoid(gc).astype(BF16), gup_ref[...].astype(BF16))
    kd_sum = jnp.zeros_like(k)
    for d in range(2):
        wlog = -_softplus(-(w0_ref[d:d + 1, :] + _dot3(jnp.tanh(wc), wup_ref[d]))) - 0.5
        lw_ref[d] = -jnp.exp(wlog)
        a = _sigmoid(a0_ref[d:d + 1, :] + _dot(ac.astype(BF16), aup_ref[d].astype(BF16)))
        kd = k * (1.0 + (a - 1.0) * ka_ref[...])
        kd_ref[d] = kd
        be_ref[d] = kap * a
        kd_sum = kd_sum + kd
    bonus_ref[...] = _head_sum(r * kd_sum * rk_ref[...], e, et) * v


def rwkv_prepare(code, lora, op, tm=256):
    W = RW_H * RW_N
    heads = jnp.arange(W, dtype=jnp.int32) // RW_N
    e = (heads[:, None] == jnp.arange(128, dtype=jnp.int32)[None, :]).astype(BF16)
    et = jnp.transpose(e)
    gup = jnp.pad(op['g_up'], ((0, 256 - op['g_up'].shape[0]), (0, 0)))
    row = lambda a: a.reshape(1, W)
    full2 = lambda shape: pl.BlockSpec(shape, lambda i: (0,) * len(shape))
    outs = pl.pallas_call(
        _rw_prep_kernel,
        out_shape=(jax.ShapeDtypeStruct((2, N_ROWS, W), F32), jax.ShapeDtypeStruct((2, N_ROWS, W), F32),
                   jax.ShapeDtypeStruct((2, N_ROWS, W), F32), jax.ShapeDtypeStruct((N_ROWS, W), F32),
                   jax.ShapeDtypeStruct((N_ROWS, W), F32), jax.ShapeDtypeStruct((N_ROWS, W), F32)),
        grid=(N_ROWS // tm,),
        in_specs=[pl.BlockSpec((tm, W), lambda i: (i, 0)), pl.BlockSpec((tm, W), lambda i: (i, 1)),
                  pl.BlockSpec((tm, W), lambda i: (i, 2)), pl.BlockSpec((tm, 384), lambda i: (i, 0)),
                  full2((2, W)), full2((2, 64, W)), full2((2, W)), full2((2, 64, W)), full2((256, W)),
                  full2((1, W)), full2((1, W)), full2((1, W)), full2((W, 128)), full2((128, W))],
        out_specs=(pl.BlockSpec((2, tm, W), lambda i: (0, i, 0)), pl.BlockSpec((2, tm, W), lambda i: (0, i, 0)),
                   pl.BlockSpec((2, tm, W), lambda i: (0, i, 0)), pl.BlockSpec((tm, W), lambda i: (i, 0)),
                   pl.BlockSpec((tm, W), lambda i: (i, 0)), pl.BlockSpec((tm, W), lambda i: (i, 0))),
        compiler_params=_cp(("parallel",)), name="rwkv_prepare",
    )(code, code, code, lora, op['w0'], op['w_up'], op['a0'], op['a_up'], gup, row(op['k_k']), row(op['k_a']),
      row(op['r_k']), e, et)
    return outs


def _rw_scan_kernel(r_ref, v_ref, lw_ref, kd_ref, be_ref, kap_ref, y_ref, st_ref):
    d = pl.program_id(0)
    c = pl.program_id(2)
    C = RW_CHUNK
    N = RW_N

    @pl.when(c == 0)
    def _():
        st_ref[...] = jnp.zeros_like(st_ref)

    isb = d == 1
    sgn = 1 - 2 * d
    ii = lax.broadcasted_iota(jnp.int32, (C, C), 0)
    jj = lax.broadcasted_iota(jnp.int32, (C, C), 1)
    dif = sgn * (ii - jj)
    incl = dif >= 0
    strict = dif > 0
    tri = incl.astype(BF16)
    eye = (ii == jj).astype(F32)
    blk = [(ii >> s) == (jj >> s) for s in range(3, C.bit_length() - 1)]
    masks = [blk[0]] + [blk[l] & ~blk[l - 1] for l in range(1, len(blk))] + [~blk[-1]]
    lw = lw_ref[...]
    cum = _dot2r(tri, lw)
    ec = jnp.exp(cum)
    en = jnp.exp(-cum)
    ea = jnp.exp(cum - lw)
    last = jnp.where(isb, cum[0:1, :], cum[C - 1:C, :])
    el = jnp.exp(last - cum)
    kap = kap_ref[...]
    r = r_ref[...]
    v = v_ref[...]
    a_t = -kap * ea
    r_t = r * ec
    b_t = be_ref[...] * en
    k_t = kd_ref[...] * en
    b_l = be_ref[...] * el
    k_l = kd_ref[...] * el
    pc = jnp.exp(last)
    H = range(RW_H)
    sl = [slice(h * N, (h + 1) * N) for h in H]
    bd = lambda a, b: _dot(a.astype(BF16), b.astype(BF16))
    tn = lambda a, b: lax.dot_general(a, b, (((0,), (0,)), ((), ())), preferred_element_type=F32)
    sc = [_dot_nt(jnp.concatenate([a_t[:, sl[h]], r_t[:, sl[h]]], axis=0).astype(BF16),
                  jnp.concatenate([b_t[:, sl[h]], k_t[:, sl[h]]], axis=0).astype(BF16)) for h in H]
    n_ab = [jnp.where(strict, sc[h][0:C, 0:C], 0.0) for h in H]
    a_ak = [jnp.where(strict, sc[h][0:C, C:2 * C], 0.0).astype(BF16) for h in H]
    m_rb = [jnp.where(incl, sc[h][C:2 * C, 0:C], 0.0).astype(BF16) for h in H]
    m_rk = [jnp.where(incl, sc[h][C:2 * C, C:2 * C], 0.0).astype(BF16) for h in H]
    vh = [v[:, sl[h]].astype(BF16) for h in H]
    d0 = [jnp.where(masks[0], n_ab[h], 0.0) for h in H]
    d2 = [bd(d0[h], d0[h]) for h in H]
    d4 = [bd(d2[h], d2[h]) for h in H]
    t = [bd(eye + d0[h], eye + d2[h]) for h in H]
    t = [bd(t[h], eye + d4[h]) for h in H]
    for m in masks[1:]:
        et = [bd(jnp.where(m, n_ab[h], 0.0), t[h]) for h in H]
        t = [t[h] + bd(t[h], et[h]) for h in H]
    amv = [_dot(jnp.concatenate([a_ak[h], m_rk[h]], axis=0), vh[h]) for h in H]
    wub = [bd(t[h], jnp.concatenate([a_t[:, sl[h]], amv[h][0:C]], axis=1)).astype(BF16) for h in H]
    kv = [tn(k_l[:, sl[h]].astype(BF16), vh[h]) for h in H]
    qy = [_dot(m_rb[h], wub[h]) + jnp.concatenate([r_t[:, sl[h]], amv[h][C:2 * C]], axis=1) for h in H]
    pp = [tn(b_l[:, sl[h]].astype(BF16), wub[h]) + jnp.concatenate([eye[0:N, 0:N] * pc[:, sl[h]], kv[h]], axis=1)
          for h in H]
    h_old = [st_ref[h] for h in H]
    ys = [bd(qy[h][:, 0:N], h_old[h]) + qy[h][:, N:2 * N] for h in H]
    for h in H:
        st_ref[h] = _dot3(pp[h][:, 0:N], h_old[h]) + pp[h][:, N:2 * N]
    y_ref[...] = jnp.concatenate(ys, axis=1)


def _rw_rowblock(d, b, c):
    n_ctx = CTX // RW_CHUNK
    n_lat = SEQ // RW_CHUNK
    cc = jnp.where(d == 0, c, n_ctx - 1 - c)
    lc = jnp.where(d == 0, c - n_ctx, n_ctx + n_lat - 1 - c)
    return jnp.where(c < n_ctx, N_LAT // RW_CHUNK + b * n_ctx + cc, b * n_lat + lc)


def rwkv_scan(code, lw, kd, be, kap):
    W = RW_H * RW_N
    nch = (CTX + SEQ) // RW_CHUNK
    rb = lambda d, b, c: _rw_rowblock(d, b, c)
    return pl.pallas_call(
        _rw_scan_kernel,
        out_shape=jax.ShapeDtypeStruct((2, N_ROWS, W), F32),
        grid=(2, BATCH, nch),
        in_specs=[pl.BlockSpec((RW_CHUNK, W), lambda d, b, c: (rb(d, b, c), 0)),
                  pl.BlockSpec((RW_CHUNK, W), lambda d, b, c: (rb(d, b, c), 2)),
                  pl.BlockSpec((None, RW_CHUNK, W), lambda d, b, c: (d, rb(d, b, c), 0)),
                  pl.BlockSpec((None, RW_CHUNK, W), lambda d, b, c: (d, rb(d, b, c), 0)),
                  pl.BlockSpec((None, RW_CHUNK, W), lambda d, b, c: (d, rb(d, b, c), 0)),
                  pl.BlockSpec((RW_CHUNK, W), lambda d, b, c: (rb(d, b, c), 0))],
        out_specs=pl.BlockSpec((None, RW_CHUNK, W), lambda d, b, c: (d, rb(d, b, c), 0)),
        scratch_shapes=[pltpu.VMEM((RW_H, RW_N, RW_N), F32)],
        compiler_params=_cp(("parallel", "parallel", "arbitrary")), name="rwkv_scan",
    )(code, code, lw, kd, be, kap)


def _rw_out_kernel(yf_ref, yb_ref, bonus_ref, g_ref, lnw_ref, lnb_ref, e_ref, et_ref, o_ref):
    e, et = e_ref[...], et_ref[...]
    y = yf_ref[...] + yb_ref[...]
    mean = _head_sum(y, e, et) * (1.0 / RW_N)
    yc = y - mean
    var = _head_sum(yc * yc, e, et) * (1.0 / RW_N)
    yn = yc * lax.rsqrt(var + RW_GN_EPS) * lnw_ref[...] + lnb_ref[...]
    o_ref[...] = (yn + bonus_ref[...]) * g_ref[...]


def rwkv_output(y2, bonus, g, op, tm=512):
    W = RW_H * RW_N
    heads = jnp.arange(W, dtype=jnp.int32) // RW_N
    e = (heads[:, None] == jnp.arange(128, dtype=jnp.int32)[None, :]).astype(BF16)
    et = jnp.transpose(e)
    M = N_LAT
    return pl.pallas_call(
        _rw_out_kernel, out_shape=jax.ShapeDtypeStruct((M, W), F32), grid=(M // tm,),
        in_specs=[pl.BlockSpec((None, tm, W), lambda i: (0, i, 0)), pl.BlockSpec((None, tm, W), lambda i: (1, i, 0)),
                  pl.BlockSpec((tm, W), lambda i: (i, 0)), pl.BlockSpec((tm, W), lambda i: (i, 0)),
                  pl.BlockSpec((1, W), lambda i: (0, 0)), pl.BlockSpec((1, W), lambda i: (0, 0)),
                  pl.BlockSpec((W, 128), lambda i: (0, 0)), pl.BlockSpec((128, W), lambda i: (0, 0))],
        out_specs=pl.BlockSpec((tm, W), lambda i: (i, 0)),
        compiler_params=_cp(("parallel",)), name="rwkv_output",
    )(y2, y2, bonus, g, op['ln_w'].reshape(1, W), op['ln_b'].reshape(1, W), e, et)


AT_Q = RW_H * AT_HD
AT_KW = AT_KV * AT_HD
AT_TQ = 512
AT_TK = 768


def _rope_tables(tm):
    half = AT_HD // 2
    inv = 10000.0 ** (-jnp.arange(0, half, 2, dtype=F32) / half)
    pos = jnp.arange(SEQ, dtype=jnp.int32)
    row = (pos // GRID_W).astype(F32)[:, None] * inv
    col = (pos % GRID_W).astype(F32)[:, None] * inv
    cos_h = jnp.concatenate([jnp.cos(row), jnp.cos(row), jnp.cos(col), jnp.cos(col)], axis=1)
    sin_h = jnp.concatenate([-jnp.sin(row), jnp.sin(row), -jnp.sin(col), jnp.sin(col)], axis=1)
    cos_t = jnp.concatenate([jnp.tile(cos_h, (1, 2)), jnp.ones((tm, 128), F32)], axis=0)
    sin_t = jnp.concatenate([jnp.tile(sin_h, (1, 2)), jnp.zeros((tm, 128), F32)], axis=0)
    return cos_t, sin_t


def _rot_partner(x):
    q = AT_HD // 4
    w = x.shape[1]
    lane = lax.broadcasted_iota(jnp.int32, x.shape, 1)
    return jnp.where((lane % (2 * q)) < q, pltpu.roll(x, w - q, 1), pltpu.roll(x, q, 1))


def _at_prep_kernel(q_ref, k_ref, v_ref, cos_ref, sin_ref, qn_ref, kn_ref, e_ref, et_ref, qo_ref, ko_ref, vo_ref):
    e, et = e_ref[...], et_ref[...]
    cos2, sin2 = cos_ref[...], sin_ref[...]

    def norm_rope(x, gain, nrep):
        ms = _head_sum(x * x, e[:x.shape[1]], et[:, :x.shape[1]]) * (1.0 / AT_HD)
        xn = x * lax.rsqrt(ms + EPS) * gain
        cos = jnp.tile(cos2, (1, nrep))
        sin = jnp.tile(sin2, (1, nrep))
        return xn * cos + _rot_partner(xn) * sin

    qn = norm_rope(q_ref[...], qn_ref[...], AT_Q // 128) * (AT_HD ** -0.5 * math.log2(math.e))
    qo_ref[...] = jnp.transpose(qn).astype(BF16)
    ko_ref[...] = norm_rope(k_ref[...], kn_ref[...], AT_KW // 128).astype(BF16)
    vo_ref[...] = jnp.transpose(v_ref[...]).astype(BF16)


def attention_prepare(p, q_norm, k_norm, tm=256):
    cos_t, sin_t = _rope_tables(tm)
    heads = jnp.arange(AT_Q, dtype=jnp.int32) // AT_HD
    e = (heads[:, None] == jnp.arange(128, dtype=jnp.int32)[None, :]).astype(BF16)
    et = jnp.transpose(e)
    tab = lambda i: jnp.where(i * tm < N_LAT, ((i * tm) % SEQ) // tm, SEQ // tm)
    n_lat_t, n_seq_t, n_ctx_t = N_LAT // tm, SEQ // tm, CTX // tm
    kvb = lambda i: jnp.where(i < n_lat_t, (i // n_seq_t) * (n_seq_t + n_ctx_t) + n_ctx_t + i % n_seq_t,
                              ((i - n_lat_t) // n_ctx_t) * (n_seq_t + n_ctx_t) + (i - n_lat_t) % n_ctx_t)
    qcol = (3 * RW_H * RW_N) // AT_Q
    kcol = (3 * RW_H * RW_N + AT_Q) // AT_KW
    return pl.pallas_call(
        _at_prep_kernel,
        out_shape=(jax.ShapeDtypeStruct((AT_Q, N_ROWS), BF16), jax.ShapeDtypeStruct((N_ROWS, AT_KW), BF16),
                   jax.ShapeDtypeStruct((AT_KW, N_ROWS), BF16)),
        grid=(N_ROWS // tm,),
        in_specs=[pl.BlockSpec((tm, AT_Q), lambda i: (i, qcol)),
                  pl.BlockSpec((tm, AT_KW), lambda i: (i, kcol)),
                  pl.BlockSpec((tm, AT_KW), lambda i: (i, kcol + 1)),
                  pl.BlockSpec((tm, 128), lambda i: (tab(i), 0)),
                  pl.BlockSpec((tm, 128), lambda i: (tab(i), 0)),
                  pl.BlockSpec((1, AT_Q), lambda i: (0, 0)),
                  pl.BlockSpec((1, AT_KW), lambda i: (0, 0)),
                  pl.BlockSpec((AT_Q, 128), lambda i: (0, 0)),
                  pl.BlockSpec((128, AT_Q), lambda i: (0, 0))],
        out_specs=(pl.BlockSpec((AT_Q, tm), lambda i: (0, i)), pl.BlockSpec((tm, AT_KW), lambda i: (kvb(i), 0)),
                   pl.BlockSpec((AT_KW, tm), lambda i: (0, kvb(i)))),
        compiler_params=_cp(("parallel",)), name="attn_prepare",
    )(p, p, p, cos_t, sin_t, jnp.tile(q_norm, AT_Q // AT_HD).reshape(1, AT_Q),
      jnp.tile(k_norm, AT_KV).reshape(1, AT_KW), e, et)


AT_REBASE = 64.0
AT_SEED = 128


def _flash_t_kernel(qt_ref, k_ref, vt_ref, o_ref, m_sc, l_sc, acc_sc, p_sc):
    ki = pl.program_id(2)
    nq = AT_Q // AT_HD
    gq = nq // AT_KV

    @pl.when(ki == 0)
    def _():
        for g in range(AT_KV):
            k0 = k_ref[0:AT_SEED, g * AT_HD:(g + 1) * AT_HD]
            for h in range(g * gq, (g + 1) * gq):
                m_sc[h] = jnp.max(_dot(k0, qt_ref[h * AT_HD:(h + 1) * AT_HD, :]), axis=0, keepdims=True)
        l_sc[...] = jnp.zeros_like(l_sc)
        acc_sc[...] = jnp.zeros_like(acc_sc)

    def k_group(g):
        return k_ref[:, g * AT_HD:(g + 1) * AT_HD]

    def v_group(g):
        return jnp.concatenate([vt_ref[g * AT_HD:(g + 1) * AT_HD, :],
                                jnp.ones((16, vt_ref.shape[1]), BF16)], axis=0)

    def scores(g):
        kg = k_group(g)
        return [_dot(kg, qt_ref[h * AT_HD:(h + 1) * AT_HD, :]) for h in range(g * gq, (g + 1) * gq)]

    gap = None
    for g in range(AT_KV):
        st = scores(g)
        m_cur = [m_sc[h] for h in range(g * gq, (g + 1) * gq)]
        for i in range(gq):
            over_i = jnp.max(st[i], axis=0, keepdims=True) - m_cur[i]
            gap = over_i if gap is None else jnp.maximum(gap, over_i)
            p_sc[g * gq + i] = jnp.exp2(st[i] - m_cur[i]).astype(BF16)
    rebase = jnp.max(gap) > AT_REBASE

    @pl.when(jnp.logical_not(rebase))
    def _():
        for g in range(AT_KV):
            vtg1 = v_group(g)
            hs = range(g * gq, (g + 1) * gq)
            pv = [_dot(vtg1, p_sc[h]) for h in hs]
            for i, h in enumerate(hs):
                l_sc[h] = l_sc[h] + pv[i][AT_HD:AT_HD + 1, :]
                rows = pl.ds(h * AT_HD, AT_HD)
                acc_sc[rows, :] = acc_sc[rows, :] + pv[i][0:AT_HD, :]

    @pl.when(rebase)
    def _():
        for g in range(AT_KV):
            vtg1 = v_group(g)
            hs = range(g * gq, (g + 1) * gq)
            st = scores(g)
            m_old = [m_sc[h] for h in hs]
            m_new = [jnp.maximum(m_old[i], jnp.max(st[i], axis=0, keepdims=True)) for i in range(gq)]
            alpha = [jnp.exp2(m_old[i] - m_new[i]) for i in range(gq)]
            pt = [jnp.exp2(st[i] - m_new[i]).astype(BF16) for i in range(gq)]
            pv = [_dot(vtg1, pt[i]) for i in range(gq)]
            for i, h in enumerate(hs):
                l_sc[h] = alpha[i] * l_sc[h] + pv[i][AT_HD:AT_HD + 1, :]
                m_sc[h] = m_new[i]
                rows = pl.ds(h * AT_HD, AT_HD)
                acc_sc[rows, :] = alpha[i] * acc_sc[rows, :] + pv[i][0:AT_HD, :]

    @pl.when(ki == pl.num_programs(2) - 1)
    def _():
        inv = jnp.concatenate([jnp.broadcast_to(1.0 / l_sc[h], (AT_HD, l_sc.shape[2])) for h in range(nq)], axis=0)
        o_ref[...] = jnp.transpose(acc_sc[...] * inv)


def flash_attention_t(qt, k, vt):
    nq = AT_Q // AT_HD
    nk = (CTX + SEQ) // AT_TK
    kv_rb = lambda b, ki: b * nk + ki
    return pl.pallas_call(
        _flash_t_kernel,
        out_shape=jax.ShapeDtypeStruct((N_LAT, AT_Q), F32),
        grid=(BATCH, SEQ // AT_TQ, nk),
        in_specs=[pl.BlockSpec((AT_Q, AT_TQ), lambda b, qi, ki: (0, b * (SEQ // AT_TQ) + qi)),
                  pl.BlockSpec((AT_TK, AT_KW), lambda b, qi, ki: (kv_rb(b, ki), 0)),
                  pl.BlockSpec((AT_KW, AT_TK), lambda b, qi, ki: (0, kv_rb(b, ki)))],
        out_specs=pl.BlockSpec((AT_TQ, AT_Q), lambda b, qi, ki: (b * (SEQ // AT_TQ) + qi, 0)),
        scratch_shapes=[pltpu.VMEM((nq, 1, AT_TQ), F32), pltpu.VMEM((nq, 1, AT_TQ), F32),
                        pltpu.VMEM((AT_Q, AT_TQ), F32), pltpu.VMEM((nq, AT_TK, AT_TQ), BF16)],
        compiler_params=_cp(("parallel", "parallel", "arbitrary")), name="flash_attention")(qt, k, vt)


OD_PAD_N = 4992


def odd_mixer(x, mods, g_pre, g_post, op):
    W = RW_H * RW_N
    w = op['w_in']
    c3 = 3 * W
    code_n = c3 + 64 + 64 + 160
    w_perm = jnp.concatenate([w[:, :c3], w[:, code_n:], w[:, c3:code_n],
                              jnp.zeros((D, OD_PAD_N - w.shape[1]), F32)], axis=1).astype(BF16)
    p = norm_mod_matmul(x, g_pre, mods, w_perm, 0, 1, tn=1664, name="odd_in_proj")
    mu = op['mu']
    taps = lambda m: jnp.stack([0.5 * m, 1.0 - m, 0.5 * m], axis=1)
    code = dwconv3(p, 0, c3, taps(mu[:c3]), jnp.zeros((c3,), F32), False, "rwkv_shift")
    lo_col = c3 + AT_Q + 2 * AT_KW
    mu_lo = jnp.pad(mu[c3:], (0, 384 - (code_n - c3)))
    lora = dwconv3(p, lo_col, 384, taps(mu_lo), jnp.zeros((384,), F32), False, "rwkv_shift_lora", cb=128)
    lw, kd, be, kap, g, bonus = rwkv_prepare(code, lora, op)
    y2 = rwkv_scan(code, lw, kd, be, kap)
    o_l = rwkv_output(y2, bonus, g, op)
    q, k, v = attention_prepare(p, op['q_norm'], op['k_norm'])
    a_l = flash_attention_t(q, k, v)
    return outproj_residual(o_l, a_l, op['w_out'].astype(BF16), x, g_post, mods, 2, name="odd_out_proj")


MOE_T = 256
MOE_CAP = 64
MOE_EPS = 4


def _router2_kernel(x_ref, g_ref, mod_ref, rw_ref, rb_ref, s1_ref, s3_ref, s2_ref, t_ref, wt_ref, cnt_ref, sh_ref):
    x = x_ref[...]
    ms = jnp.mean(x * x, axis=-1, keepdims=True)
    t = x * lax.rsqrt(ms + EPS) * g_ref[...] * (1.0 + mod_ref[4:5, :]) + mod_ref[3:4, :]
    tb = t.astype(BF16)
    t_ref[...] = tb
    sh_ref[...] = _dot((_silu(_dot(tb, s1_ref[...])) * _dot(tb, s3_ref[...])).astype(BF16), s2_ref[...])
    th, tl = _split(t)
    wh, wl = _split(rw_ref[...])
    lg = _dot_nt(wh, th) + (_dot_nt(wh, tl) + _dot_nt(wl, th))
    sc = _sigmoid(lg)
    sel = sc + rb_ref[...]
    tm = sel.shape[1]
    gsz = N_EXP // N_GRP
    ninf = -jnp.inf
    sel3 = sel.reshape(N_GRP, gsz, tm)
    i3 = lax.broadcasted_iota(jnp.int32, sel3.shape, 1)
    m1 = jnp.max(sel3, axis=1, keepdims=True)
    first = jnp.min(jnp.where(sel3 == m1, i3, gsz), axis=1, keepdims=True)
    m2 = jnp.max(jnp.where(i3 == first, ninf, sel3), axis=1, keepdims=True)
    grp = (m1 + m2).reshape(N_GRP, tm)
    gi = lax.broadcasted_iota(jnp.int32, grp.shape, 0)
    gmask = jnp.zeros(grp.shape, F32)
    for _ in range(TOPK_GRP):
        m = jnp.max(grp, axis=0, keepdims=True)
        pick = jnp.min(jnp.where(grp == m, gi, N_GRP), axis=0, keepdims=True)
        hit = gi == pick
        gmask = jnp.where(hit, 1.0, gmask)
        grp = jnp.where(hit, ninf, grp)
    emask = jnp.broadcast_to(gmask.reshape(N_GRP, 1, tm), (N_GRP, gsz, tm)).reshape(N_EXP, tm)
    msel = jnp.where(emask > 0.5, sel, ninf)
    ei = lax.broadcasted_iota(jnp.int32, msel.shape, 0)
    chosen = jnp.zeros(msel.shape, F32)
    for _ in range(TOP_K):
        m = jnp.max(msel, axis=0, keepdims=True)
        pick = jnp.min(jnp.where(msel == m, ei, N_EXP), axis=0, keepdims=True)
        hit = ei == pick
        chosen = jnp.where(hit, 1.0, chosen)
        msel = jnp.where(hit, ninf, msel)
    w = chosen * sc
    wt = w / jnp.sum(w, axis=0, keepdims=True) * ROUTED_SCALE
    for s in range(tm // MOE_T):
        ws = wt[:, s * MOE_T:(s + 1) * MOE_T]
        wt_ref[s] = ws
        cnt_ref[s] = jnp.sum((ws > 0.0).astype(F32), axis=1, keepdims=True).astype(jnp.int32)


def moe_router2(x, M, g, mods, mp):
    tm = 2 * MOE_T
    ns = tm // MOE_T
    full = lambda shape: pl.BlockSpec(shape, lambda i: (0,) * len(shape))
    return pl.pallas_call(
        _router2_kernel,
        out_shape=(jax.ShapeDtypeStruct((M, D), BF16), jax.ShapeDtypeStruct((M // MOE_T, N_EXP, MOE_T), F32),
                   jax.ShapeDtypeStruct((M // MOE_T, N_EXP, 1), jnp.int32), jax.ShapeDtypeStruct((M, D), F32)),
        grid=(M // tm,),
        in_specs=[pl.BlockSpec((tm, D), lambda i: (i, 0)), full((1, D)),
                  pl.BlockSpec((None, 6, D), lambda i: (_seq_of_rowblock(i, tm), 0, 0)),
                  full((N_EXP, D)), full((N_EXP, 1)), full((D, EXP_FF)), full((D, EXP_FF)), full((EXP_FF, D))],
        out_specs=(pl.BlockSpec((tm, D), lambda i: (i, 0)), pl.BlockSpec((ns, N_EXP, MOE_T), lambda i: (i, 0, 0)),
                   pl.BlockSpec((ns, N_EXP, 1), lambda i: (i, 0, 0)), pl.BlockSpec((tm, D), lambda i: (i, 0))),
        compiler_params=_cp(("parallel",)), name="moe_router",
    )(x, g.reshape(1, D), mods, jnp.transpose(mp['router_w']), mp['router_bias'].reshape(N_EXP, 1),
      mp['s1'].astype(BF16), mp['s3'].astype(BF16), mp['s2'].astype(BF16))


def _moe3_kernel(cnt_ref, ovf_ref, t_ref, wt_ref, w1_ref, w3_ref, w2_ref, o_ref, rank_sc, *, nsub):
    i = pl.program_id(0)
    eb = pl.program_id(1)
    T, CAP, EPS = MOE_T, MOE_CAP, MOE_EPS

    @pl.when(eb == 0)
    def _():
        before = (lax.broadcasted_iota(jnp.int32, (T, T), 0) < lax.broadcasted_iota(jnp.int32, (T, T), 1))
        before = before.astype(BF16)
        for s in range(nsub):
            rank_sc[s] = _dot((wt_ref[s] > 0.0).astype(BF16), before)
        o_ref[...] = jnp.zeros_like(o_ref)

    slot = lax.broadcasted_iota(jnp.int32, (CAP, T), 0).astype(F32)

    def one_hot(s, e, first_slot):
        w_row = wt_ref[s, pl.ds(e, 1), :]
        r_row = rank_sc[s, pl.ds(e, 1), :]
        hit = ((r_row - first_slot) == slot) & (w_row > 0.0)
        w_slot = jnp.sum(jnp.where(hit, w_row, 0.0), axis=1, keepdims=True)
        return hit.astype(F32).astype(BF16), w_slot

    def swiglu(xg, j):
        h = _silu(_dot(xg, w1_ref[j])) * _dot(xg, w3_ref[j])
        return _dot(h.astype(BF16), w2_ref[j])

    hot = [[one_hot(s, eb * EPS + j, 0.0) for j in range(EPS)] for s in range(nsub)]
    pb = [jnp.concatenate([hot[s][j][0] for j in range(EPS)], axis=0) for s in range(nsub)]
    xg = [_dot(pb[s], t_ref[s * T:(s + 1) * T, :]).astype(BF16) for s in range(nsub)]
    y = [swiglu(jnp.concatenate([xg[s][j * CAP:(j + 1) * CAP] for s in range(nsub)], axis=0), j)
         for j in range(EPS)]
    for s in range(nsub):
        yw = jnp.concatenate([y[j][s * CAP:(s + 1) * CAP] * hot[s][j][1] for j in range(EPS)], axis=0)
        o_ref[s * T:(s + 1) * T, :] += lax.dot_general(pb[s], yw.astype(BF16), (((0,), (0,)), ((), ())),
                                                       preferred_element_type=F32)

    def pair(idx, carry):
        s = idx // EPS
        j = idx % EPS
        e = eb * EPS + j
        n_tok = cnt_ref[(i * nsub + s) * N_EXP + e]
        rows = pl.ds(pl.multiple_of(s * T, T), T)

        def chunk(ci, c2):
            p1, w_slot = one_hot(s, e, (ci * CAP).astype(F32))
            yw = (swiglu(_dot(p1, t_ref[rows, :]).astype(BF16), j) * w_slot).astype(BF16)
            o_ref[rows, :] += lax.dot_general(p1, yw, (((0,), (0,)), ((), ())), preferred_element_type=F32)
            return c2

        lax.fori_loop(1, (n_tok + CAP - 1) // CAP, chunk, 0)
        return carry

    @pl.when(ovf_ref[i * (N_EXP // EPS) + eb] > 0)
    def _():
        lax.fori_loop(0, nsub * EPS, pair, 0)


def _moe_out_kernel(r_ref, sh_ref, x_ref, g_ref, mod_ref, o_ref):
    f = r_ref[...] + sh_ref[...]
    ms = jnp.mean(f * f, axis=-1, keepdims=True)
    o_ref[...] = x_ref[...] + mod_ref[5:6, :] * (f * lax.rsqrt(ms + EPS) * g_ref[...])


def moe_layer3(x, M, g_pre, g_post, mods, mp, nsub):
    t, wt, cnt, sh = moe_router2(x, M, g_pre, mods, mp)
    T = MOE_T
    TS = nsub * T
    ovf = jnp.any(cnt.reshape(M // TS, nsub, N_EXP // MOE_EPS, MOE_EPS) > MOE_CAP, axis=(1, 3)).astype(jnp.int32)
    grid_spec = pltpu.PrefetchScalarGridSpec(
        num_scalar_prefetch=2, grid=(M // TS, N_EXP // MOE_EPS),
        in_specs=[pl.BlockSpec((TS, D), lambda i, e, c, o: (i, 0)),
                  pl.BlockSpec((nsub, N_EXP, T), lambda i, e, c, o: (i, 0, 0)),
                  pl.BlockSpec((MOE_EPS, D, EXP_FF), lambda i, e, c, o: (e, 0, 0)),
                  pl.BlockSpec((MOE_EPS, D, EXP_FF), lambda i, e, c, o: (e, 0, 0)),
                  pl.BlockSpec((MOE_EPS, EXP_FF, D), lambda i, e, c, o: (e, 0, 0))],
        out_specs=pl.BlockSpec((TS, D), lambda i, e, c, o: (i, 0)),
        scratch_shapes=[pltpu.VMEM((nsub, N_EXP, T), F32)])
    routed = pl.pallas_call(
        functools.partial(_moe3_kernel, nsub=nsub), out_shape=jax.ShapeDtypeStruct((M, D), F32),
        grid_spec=grid_spec, compiler_params=_cp(("parallel", "arbitrary")), name="moe_experts",
    )(cnt.reshape(-1), ovf.reshape(-1), t, wt, mp['w1'].astype(BF16), mp['w3'].astype(BF16),
      mp['w2'].astype(BF16))
    tm = 512
    return pl.pallas_call(
        _moe_out_kernel, out_shape=jax.ShapeDtypeStruct((M, D), F32), grid=(M // tm,),
        in_specs=[pl.BlockSpec((tm, D), lambda i: (i, 0)), pl.BlockSpec((tm, D), lambda i: (i, 0)),
                  pl.BlockSpec((tm, D), lambda i: (i, 0)), pl.BlockSpec((1, D), lambda i: (0, 0)),
                  pl.BlockSpec((None, 6, D), lambda i: (_seq_of_rowblock(i, tm), 0, 0))],
        out_specs=pl.BlockSpec((tm, D), lambda i: (i, 0)),
        compiler_params=_cp(("parallel",)), name="moe_output")(routed, sh, x, g_post.reshape(1, D), mods)


def kernel(x, c, ctx, c_ctx, mod_w, mod_b, norm_mix_pre, norm_mix_post, norm_ffn_pre, norm_ffn_post, router_w, router_bias, expert_w1, expert_w3, expert_w2, shared_w1, shared_w3, shared_w2, ev_w_in, ev_w_out, ssd_conv_w, ssd_conv_b, ssd_dt_bias, ssd_a_log, ssd_d, ssd_norm_w, hy_conv_w, hy_conv_b, hy_mlp_w0, hy_mlp_b0, hy_freq0, hy_mlp_w1, hy_mlp_b1, hy_freq1, hy_mlp_w2, hy_bias, od_w_in, od_w_out, rw_mu, rw_w0, rw_w_up, rw_a0, rw_a_up, rw_g_up, rw_k_k, rw_k_a, rw_r_k, rw_ln_w, rw_ln_b, at_q_norm, at_k_norm):
    xs = jnp.concatenate([x.reshape(N_LAT, D), ctx.reshape(BATCH * CTX, D)], axis=0)
    cvecs = jnp.zeros((8, D), F32).at[0:BATCH].set(c).at[BATCH].set(c_ctx)
    assert mod_w.shape[0] == 2, "one even (SSD | Hyena) layer followed by one odd (RWKV | attention) layer"

    def moe_params(i):
        return dict(router_w=router_w[i], router_bias=router_bias[i], w1=expert_w1[i], w3=expert_w3[i],
                    w2=expert_w2[i], s1=shared_w1[i], s3=shared_w3[i], s2=shared_w2[i])

    mods = modulation(cvecs, mod_w[0], mod_b[0])[:BATCH + 1].reshape(BATCH + 1, 6, D)
    ep = dict(w_in=ev_w_in[0], w_out=ev_w_out[0], ssd_conv_w=ssd_conv_w[0], ssd_conv_b=ssd_conv_b[0],
              ssd_dt_bias=ssd_dt_bias[0], ssd_a_log=ssd_a_log[0], ssd_d=ssd_d[0], ssd_norm_w=ssd_norm_w[0],
              hy_conv_w=hy_conv_w[0], hy_conv_b=hy_conv_b[0], hy_mlp_w0=hy_mlp_w0[0], hy_mlp_b0=hy_mlp_b0[0],
              hy_freq0=hy_freq0[0], hy_mlp_w1=hy_mlp_w1[0], hy_mlp_b1=hy_mlp_b1[0], hy_freq1=hy_freq1[0],
              hy_mlp_w2=hy_mlp_w2[0], hy_bias=hy_bias[0])
    xs = even_mixer(xs, mods, norm_mix_pre[0], norm_mix_post[0], ep)
    xs = moe_layer3(xs, N_ROWS, norm_ffn_pre[0], norm_ffn_post[0], mods, moe_params(0), 6)
    mods = modulation(cvecs, mod_w[1], mod_b[1])[:BATCH + 1].reshape(BATCH + 1, 6, D)
    op = dict(w_in=od_w_in[0], w_out=od_w_out[0], mu=rw_mu[0], w0=rw_w0[0], w_up=rw_w_up[0], a0=rw_a0[0],
              a_up=rw_a_up[0], g_up=rw_g_up[0], k_k=rw_k_k[0], k_a=rw_k_a[0], r_k=rw_r_k[0], ln_w=rw_ln_w[0],
              ln_b=rw_ln_b[0], q_norm=at_q_norm[0], k_norm=at_k_norm[0])
    xl = odd_mixer(xs, mods, norm_mix_pre[1], norm_mix_post[1], op)
    xl = moe_layer3(xl, N_LAT, norm_ffn_pre[1], norm_ffn_post[1], mods, moe_params(1), 8)
    return xl.reshape(BATCH, SEQ, D)
```

```python
import functools
import math

import jax
import jax.numpy as jnp
from jax import lax
from jax.experimental import pallas as pl
from jax.experimental.pallas import tpu as pltpu

F32 = jnp.float32
BF16 = jnp.bfloat16

D = 1024
BATCH = 2
SEQ = 8192
CTX = 256
N_LAT = BATCH * SEQ
N_ROWS = N_LAT + BATCH * CTX
EPS = 1e-6
GRID_W = 64

SSD_HEADS = 16
SSD_P = 64
SSD_G = 2
SSD_S = 128
SSD_Q = 128
HY_W = 1024
HY_EMB = 33
HY_HID = 64

RW_H = 16
RW_N = 64
RW_CHUNK = 128
RW_GN_EPS = 64e-5

AT_KV = 4
AT_HD = 64

N_EXP = 64
TOP_K = 8
N_GRP = 8
TOPK_GRP = 4
EXP_FF = 256
ROUTED_SCALE = 2.5

VMEM_LIMIT = 56 * 1024 * 1024


def _cp(sem, vmem=None):
    return pltpu.CompilerParams(dimension_semantics=sem, vmem_limit_bytes=vmem or VMEM_LIMIT)


def _dot(a, b):
    return jnp.dot(a, b, preferred_element_type=F32)


def _dot_nt(a, b):
    return lax.dot_general(a, b, (((1,), (1,)), ((), ())), preferred_element_type=F32)


def _split(x):
    hi = x.astype(BF16)
    lo = (x - hi.astype(F32)).astype(BF16)
    return hi, lo


def _dot3(a, b):
    ah, al = _split(a)
    bh, bl = _split(b)
    return _dot(ah, bh) + (_dot(ah, bl) + _dot(al, bh))


def _dot2l(a, b):
    ah, al = _split(a)
    return _dot(ah, b) + _dot(al, b)


def _dot2r(a, b):
    bh, bl = _split(b)
    return _dot(a, bh) + _dot(a, bl)


def _silu(x):
    return x * (1.0 / (1.0 + jnp.exp(-x)))


def _sigmoid(x):
    return 1.0 / (1.0 + jnp.exp(-x))


def _softplus(x):
    return jnp.maximum(x, 0.0) + jnp.log(1.0 + jnp.exp(-jnp.abs(x)))


def _seq_of_rowblock(i, tm):
    return jnp.minimum((i * tm) // SEQ, 2)


def _nmm_kernel(x_ref, g_ref, mod_ref, w_ref, o_ref, a_sc, *, shift_i, scale_i):
    @pl.when(pl.program_id(1) == 0)
    def _():
        x = x_ref[...]
        ms = jnp.mean(x * x, axis=-1, keepdims=True)
        y = x * lax.rsqrt(ms + EPS) * g_ref[...]
        h = y * (1.0 + mod_ref[scale_i:scale_i + 1, :]) + mod_ref[shift_i:shift_i + 1, :]
        a_sc[...] = h.astype(BF16)

    o_ref[...] = _dot(a_sc[...], w_ref[...])


def norm_mod_matmul(x, g, mods, w, shift_i, scale_i, tm=512, tn=None, name="nmm"):
    M = x.shape[0]
    N = w.shape[1]
    tn = tn or N
    return pl.pallas_call(
        functools.partial(_nmm_kernel, shift_i=shift_i, scale_i=scale_i),
        out_shape=jax.ShapeDtypeStruct((M, N), F32),
        grid=(M // tm, N // tn),
        in_specs=[pl.BlockSpec((tm, D), lambda i, j: (i, 0)),
                  pl.BlockSpec((1, D), lambda i, j: (0, 0)),
                  pl.BlockSpec((None, 6, D), lambda i, j: (_seq_of_rowblock(i, tm), 0, 0)),
                  pl.BlockSpec((D, tn), lambda i, j: (0, j))],
        out_specs=pl.BlockSpec((tm, tn), lambda i, j: (i, j)),
        scratch_shapes=[pltpu.VMEM((tm, D), BF16)],
        compiler_params=_cp(("parallel", "arbitrary")), name=name)(x, g.reshape(1, D), mods, w)


def _outproj_kernel(a1_ref, a2_ref, w_ref, x_ref, g_ref, mod_ref, o_ref, *, gate_i):
    y = _dot(a1_ref[...].astype(BF16), w_ref[0:D, :]) + _dot(a2_ref[...].astype(BF16), w_ref[D:2 * D, :])
    ms = jnp.mean(y * y, axis=-1, keepdims=True)
    o_ref[...] = x_ref[...] + mod_ref[gate_i:gate_i + 1, :] * (y * lax.rsqrt(ms + EPS) * g_ref[...])


def outproj_residual(a1, a2, w, x, g, mods, gate_i, tm=512, name="outproj"):
    M = a1.shape[0]
    return pl.pallas_call(
        functools.partial(_outproj_kernel, gate_i=gate_i),
        out_shape=jax.ShapeDtypeStruct((M, D), F32),
        grid=(M // tm,),
        in_specs=[pl.BlockSpec((tm, D), lambda i: (i, 0)),
                  pl.BlockSpec((tm, D), lambda i: (i, 0)),
                  pl.BlockSpec((2 * D, D), lambda i: (0, 0)),
                  pl.BlockSpec((tm, D), lambda i: (i, 0)),
                  pl.BlockSpec((1, D), lambda i: (0, 0)),
                  pl.BlockSpec((None, 6, D), lambda i: (_seq_of_rowblock(i, tm), 0, 0))],
        out_specs=pl.BlockSpec((tm, D), lambda i: (i, 0)),
        compiler_params=_cp(("parallel",)), name=name)(a1, a2, w, x, g.reshape(1, D), mods)


def _mod_kernel(c_ref, w_ref, b_ref, o_ref):
    o_ref[...] = _dot3(_silu(c_ref[...]), w_ref[...]) + b_ref[...]


def modulation(cvecs, w, b):
    N = w.shape[1]
    tn = 1024
    return pl.pallas_call(
        _mod_kernel, out_shape=jax.ShapeDtypeStruct((8, N), F32), grid=(N // tn,),
        in_specs=[pl.BlockSpec((8, D), lambda j: (0, 0)),
                  pl.BlockSpec((D, tn), lambda j: (0, j)),
                  pl.BlockSpec((1, tn), lambda j: (0, j))],
        out_specs=pl.BlockSpec((8, tn), lambda j: (0, j)),
        compiler_params=_cp(("parallel",)), name="modulation")(cvecs, w, b.reshape(1, N))


CONV_TM = 256


def _conv3_kernel(x_ref, prev_ref, next_ref, w_ref, b_ref, o_ref, *, act):
    tm = CONV_TM
    row0 = pl.program_id(0) * tm
    seq_len = jnp.where(row0 < N_LAT, SEQ, CTX)
    pos = jnp.where(row0 < N_LAT, row0 % SEQ, (row0 - N_LAT) % CTX)
    cur = x_ref[...]
    rows = lax.broadcasted_iota(jnp.int32, cur.shape, 0)
    prev_row = prev_ref[7:8, :] * (pos > 0).astype(F32)
    next_row = next_ref[0:1, :] * (pos + tm < seq_len).astype(F32)
    xm1 = jnp.where(rows == 0, prev_row, pltpu.roll(cur, 1, 0))
    xp1 = jnp.where(rows == tm - 1, next_row, pltpu.roll(cur, tm - 1, 0))
    y = xm1 * w_ref[0:1, :] + cur * w_ref[1:2, :] + xp1 * w_ref[2:3, :] + b_ref[...]
    o_ref[...] = _silu(y) if act else y


def dwconv3(p, col0, ncols, w, b, act, name, cb=1024):
    tm = CONV_TM
    cb = math.gcd(cb, math.gcd(col0, ncols)) if col0 else math.gcd(cb, ncols)
    r8 = tm // 8
    n8 = N_ROWS // 8
    c0 = col0 // cb
    return pl.pallas_call(
        functools.partial(_conv3_kernel, act=act),
        out_shape=jax.ShapeDtypeStruct((N_ROWS, ncols), F32),
        grid=(N_ROWS // tm, ncols // cb),
        in_specs=[pl.BlockSpec((tm, cb), lambda i, j: (i, c0 + j)),
                  pl.BlockSpec((8, cb), lambda i, j: (jnp.maximum(i * r8 - 1, 0), c0 + j)),
                  pl.BlockSpec((8, cb), lambda i, j: (jnp.minimum((i + 1) * r8, n8 - 1), c0 + j)),
                  pl.BlockSpec((3, cb), lambda i, j: (0, j)),
                  pl.BlockSpec((1, cb), lambda i, j: (0, j))],
        out_specs=pl.BlockSpec((tm, cb), lambda i, j: (i, j)),
        compiler_params=_cp(("parallel", "parallel")), name=name)(p, p, p, jnp.transpose(w), b.reshape(1, ncols))


def _ssd_kernel(xs_ref, bm_ref, cm_ref, dt_ref, dtT_ref, bias_ref, biasT_ref, alog_ref, alogT_ref,
                y_ref, st_ref):
    d = pl.program_id(0)
    c = pl.program_id(2)
    Q = SSD_Q
    HG = SSD_HEADS // SSD_G

    @pl.when(c == 0)
    def _():
        st_ref[...] = jnp.zeros_like(st_ref)

    isb = d == 1
    sgn = 1 - 2 * d
    ii = lax.broadcasted_iota(jnp.int32, (Q, Q), 0)
    jj = lax.broadcasted_iota(jnp.int32, (Q, Q), 1)
    tri = (jj <= ii).astype(BF16)
    triT = (ii <= jj).astype(BF16)
    mask = sgn * (ii - jj) >= 0
    xs = xs_ref[...]
    G = range(SSD_G)
    dt = [_softplus(dt_ref[g] + bias_ref[g]) for g in G]
    dtT = [_softplus(dtT_ref[g] + biasT_ref[g]) for g in G]
    a = [dt[g] * (-jnp.exp(alog_ref[g])) for g in G]
    aT = [dtT[g] * (-jnp.exp(alogT_ref[g])) for g in G]
    cs = [_dot2r(tri, a[g]) for g in G]
    csT = [_dot2l(aT[g], triT) for g in G]
    tot = [cs[g][Q - 1:Q, :] for g in G]
    p = [jnp.where(isb, a[g] - cs[g], cs[g]) for g in G]
    pT = [jnp.where(isb, aT[g] - csT[g], csT[g]) for g in G]
    dec_out = [jnp.exp(jnp.where(isb, tot[g], 0.0) + p[g]) for g in G]
    dec_state = [jnp.exp(jnp.where(isb, 0.0, tot[g]) - p[g]) for g in G]
    chunk_dec = [jnp.exp(tot[g]) for g in G]
    bm = [bm_ref[:, g * SSD_S:(g + 1) * SSD_S].astype(BF16) for g in G]
    cm = [cm_ref[:, g * SSD_S:(g + 1) * SSD_S].astype(BF16) for g in G]
    cb = [_dot_nt(cm[g], bm[g]) for g in G]
    nh = SSD_HEADS

    def spread(cols, width):
        v = jnp.concatenate(cols, axis=1)
        head = lax.broadcasted_iota(jnp.int32, (nh, nh * width), 1) // width
        e = (head == lax.broadcasted_iota(jnp.int32, (nh, nh * width), 0)).astype(BF16)
        h1 = v.astype(BF16)
        r1 = v - h1.astype(F32)
        h2 = r1.astype(BF16)
        h3 = (r1 - h2.astype(F32)).astype(BF16)
        return _dot(h1, e) + (_dot(h2, e) + _dot(h3, e))

    dt_x = spread(dt, SSD_P)
    dout_x = spread(dec_out, SSD_P)
    dst_x = spread(dec_state, SSD_P)
    p_x = spread(p, Q)
    xh_all = xs * dt_x
    xdec_all = (xh_all * dst_x).astype(BF16)
    xh_all = xh_all.astype(BF16)
    GH = [(g, h) for g in G for h in range(HG)]
    NH = range(len(GH))
    lm = [(cb[g] * jnp.exp(jnp.where(mask, p_x[:, n * Q:(n + 1) * Q] - pT[g][h:h + 1, :], -1e30))).astype(BF16)
          for n, (g, h) in enumerate(GH)]
    s_old = [st_ref[n] for n in NH]
    y_in = [_dot(lm[n], xh_all[:, n * SSD_P:(n + 1) * SSD_P]) for n in NH]
    y_st = [_dot(cm[g], s_old[n].astype(BF16)) for n, (g, h) in enumerate(GH)]
    upd = [lax.dot_general(bm[g], xdec_all[:, n * SSD_P:(n + 1) * SSD_P], (((0,), (0,)), ((), ())),
                           preferred_element_type=F32) for n, (g, h) in enumerate(GH)]
    for n, (g, h) in enumerate(GH):
        st_ref[n] = chunk_dec[g][:, h:h + 1] * s_old[n] + upd[n]
    y_ref[...] = jnp.concatenate(y_in, axis=1) + dout_x * jnp.concatenate(y_st, axis=1)


def _ssd_rowblock(d, b, c):
    n_ctx = CTX // SSD_Q
    n_lat = SEQ // SSD_Q
    cc = jnp.where(d == 0, c, n_ctx - 1 - c)
    lc = jnp.where(d == 0, c - n_ctx, n_ctx + n_lat - 1 - c)
    return jnp.where(c < n_ctx, N_LAT // SSD_Q + b * n_ctx + cc, b * n_lat + lc)


def ssd_scan(xbc, dt_raw, dt_bias, a_log):
    HG = SSD_HEADS // SSD_G
    W = SSD_HEADS * SSD_P
    dsel = dt_raw[:, :2 * SSD_HEADS].reshape(N_ROWS, 2, SSD_G, HG).transpose(1, 2, 0, 3)
    dselT = dsel.transpose(0, 1, 3, 2)
    bias = dt_bias.reshape(2, SSD_G, 1, HG)
    biasT = dt_bias.reshape(2, SSD_G, HG, 1)
    alog = a_log.reshape(2, SSD_G, 1, HG)
    alogT = a_log.reshape(2, SSD_G, HG, 1)
    nch = (CTX + SEQ) // SSD_Q
    rb = _ssd_rowblock
    GS = SSD_G * SSD_S
    par = lambda shape: pl.BlockSpec((None,) + shape, lambda d, b, c: (d, 0, 0, 0))
    return pl.pallas_call(
        _ssd_kernel,
        out_shape=jax.ShapeDtypeStruct((2, N_ROWS, W), F32),
        grid=(2, BATCH, nch),
        in_specs=[pl.BlockSpec((SSD_Q, W), lambda d, b, c: (rb(d, b, c), 0)),
                  pl.BlockSpec((SSD_Q, GS), lambda d, b, c: (rb(d, b, c), W // GS)),
                  pl.BlockSpec((SSD_Q, GS), lambda d, b, c: (rb(d, b, c), W // GS + 1)),
                  pl.BlockSpec((None, SSD_G, SSD_Q, HG), lambda d, b, c: (d, 0, rb(d, b, c), 0)),
                  pl.BlockSpec((None, SSD_G, HG, SSD_Q), lambda d, b, c: (d, 0, 0, rb(d, b, c))),
                  par((SSD_G, 1, HG)), par((SSD_G, HG, 1)), par((SSD_G, 1, HG)), par((SSD_G, HG, 1))],
        out_specs=pl.BlockSpec((None, SSD_Q, W), lambda d, b, c: (d, rb(d, b, c), 0)),
        scratch_shapes=[pltpu.VMEM((SSD_HEADS, SSD_S, SSD_P), F32)],
        compiler_params=_cp(("parallel", "parallel", "arbitrary")), name="ssd_scan",
    )(xbc, xbc, xbc, dsel, dselT, bias, biasT, alog, alogT)


def _ssd_out_kernel(yf_ref, yb_ref, xs_ref, z_ref, dskip_ref, nw_ref, o_ref):
    y = yf_ref[...] + yb_ref[...] + xs_ref[...] * dskip_ref[...]
    y = y * _silu(z_ref[...])
    gs = SSD_HEADS * SSD_P // SSD_G
    parts = []
    for g in range(SSD_G):
        yg = y[:, g * gs:(g + 1) * gs]
        parts.append(yg * lax.rsqrt(jnp.mean(yg * yg, axis=-1, keepdims=True) + EPS))
    o_ref[...] = jnp.concatenate(parts, axis=1) * nw_ref[...]


def ssd_output(y2, xbc, p, zcol, d_skip, norm_w, tm=512):
    W = SSD_HEADS * SSD_P
    dexp = jnp.repeat(d_skip, SSD_P).reshape(1, W)
    return pl.pallas_call(
        _ssd_out_kernel, out_shape=jax.ShapeDtypeStruct((N_ROWS, W), F32), grid=(N_ROWS // tm,),
        in_specs=[pl.BlockSpec((None, tm, W), lambda i: (0, i, 0)),
                  pl.BlockSpec((None, tm, W), lambda i: (1, i, 0)),
                  pl.BlockSpec((tm, W), lambda i: (i, 0)),
                  pl.BlockSpec((tm, W), lambda i: (i, zcol // W)),
                  pl.BlockSpec((1, W), lambda i: (0, 0)),
                  pl.BlockSpec((1, W), lambda i: (0, 0))],
        out_specs=pl.BlockSpec((tm, W), lambda i: (i, 0)),
        compiler_params=_cp(("parallel",)), name="ssd_output")(y2, y2, xbc, p, dexp, norm_w.reshape(1, W))


def _hyfilt_kernel(f_ref, w0_ref, b0_ref, fr0_ref, w1_ref, b1_ref, fr1_ref, w2_ref, dl_ref, h_ref, ss_ref, *,
                   n_tiles):
    f = f_ref[...]
    h = jnp.sin(fr0_ref[...] * (_dot3(f, w0_ref[...]) + b0_ref[...]))
    h = jnp.sin(fr1_ref[...] * (_dot3(h, w1_ref[...]) + b1_ref[...]))
    h = _dot3(h, w2_ref[...])
    h = h * jnp.exp(-f[:, 0:1] * dl_ref[...])
    side = pl.program_id(0) // n_tiles
    j = pl.program_id(0) % n_tiles

    @pl.when(j == 0)
    def _():
        ss_ref[...] = jnp.zeros_like(ss_ref)

    ss_ref[...] += jnp.sum(h * h, axis=0, keepdims=True)
    row = lax.broadcasted_iota(jnp.int32, (h.shape[0], 1), 0) + j * h.shape[0]
    h_ref[...] = jnp.where((side == 1) & (row == 0), 0.0, h)


def hyena_filter_taps(L, hp):
    pos = jnp.arange(L, dtype=F32)
    t = pos / (L - 1)
    bands = (HY_EMB - 1) // 2
    freqs = jnp.linspace(1e-4, bands - 1, bands, dtype=F32)
    ang = (2.0 * math.pi / L) * pos[:, None] * freqs[None, :]
    feats = jnp.concatenate([t[:, None], jnp.cos(ang), -jnp.sin(ang)], axis=-1)
    feats = jnp.pad(feats, ((0, 0), (0, 128 - HY_EMB)))
    feats = jnp.concatenate([feats, jnp.flip(feats, axis=0)], axis=0)
    w0 = jnp.pad(hp['hy_mlp_w0'], ((0, 128 - HY_EMB), (0, 0)))
    min_decay = math.log(1e-2) / 1.5
    max_decay = math.log(1e-2) / 0.3
    deltas = jnp.abs(jnp.linspace(min_decay, max_decay, HY_W, dtype=F32))
    dl = jnp.tile(deltas, 2).reshape(1, 2 * HY_W)
    w2 = hp['hy_mlp_w2'].reshape(HY_HID, 2, 2, HY_W).transpose(0, 2, 1, 3).reshape(HY_HID, 4 * HY_W)
    tl = min(L, 512)
    n_tiles = L // tl
    NS = 2 * HY_W
    full = lambda shape: pl.BlockSpec(shape, lambda i: (0, 0))
    return pl.pallas_call(
        functools.partial(_hyfilt_kernel, n_tiles=n_tiles),
        out_shape=(jax.ShapeDtypeStruct((2 * L, NS), F32), jax.ShapeDtypeStruct((1, 2 * NS), F32)),
        grid=(2 * n_tiles,),
        in_specs=[pl.BlockSpec((tl, 128), lambda i: (i, 0)), full((128, HY_HID)), full((1, HY_HID)),
                  full((1, HY_HID)), full((HY_HID, HY_HID)), full((1, HY_HID)), full((1, HY_HID)),
                  pl.BlockSpec((HY_HID, NS), lambda i: (0, i // n_tiles)), full((1, NS))],
        out_specs=(pl.BlockSpec((tl, NS), lambda i: (i, 0)), pl.BlockSpec((1, NS), lambda i: (0, i // n_tiles))),
        compiler_params=_cp(("arbitrary",)), name="hyena_filter",
    )(feats, w0, hp['hy_mlp_b0'].reshape(1, -1), hp['hy_freq0'].reshape(1, -1), hp['hy_mlp_w1'],
      hp['hy_mlp_b1'].reshape(1, -1), hp['hy_freq1'].reshape(1, -1), w2, dl)


def _cis(num, den):
    ang = (2.0 * math.pi / den) * (num % den).astype(F32)
    return jnp.cos(ang), -jnp.sin(ang)


def _fft_consts(NB, BS):
    N = NB * BS
    h = NB // 2
    k1 = jnp.arange(h, dtype=jnp.int32)
    j = jnp.arange(NB, dtype=jnp.int32)
    re, im = _cis(j[None, :] * (2 * k1[:, None] + 1), 2 * NB)
    f1 = jnp.concatenate([re, im], axis=0)
    neg = jnp.where(j >= h, -1.0, 1.0)[None, :]
    f1_data = f1[:, :h]
    f1_filt = f1 * neg
    f1_inv = (2.0 / N) * jnp.concatenate([re[:, :h].T, im[:, :h].T], axis=1)
    r = jnp.arange(BS, dtype=jnp.int32)
    k2 = jnp.arange(BS, dtype=jnp.int32)
    kk = 2 * k1[:, None, None] + 2 * NB * k2[None, :, None] + 1
    gre, gim = _cis(kk * r[None, None, :], 2 * N)
    gf = jnp.concatenate([jnp.concatenate([gre, -gim], axis=2), jnp.concatenate([gim, gre], axis=2)], axis=1)
    gret, gimt = gre.transpose(0, 2, 1), gim.transpose(0, 2, 1)
    gi = jnp.concatenate([jnp.concatenate([gret, gimt], axis=2), jnp.concatenate([-gimt, gret], axis=2)], axis=1)
    return (f1_data.astype(BF16), f1_filt.astype(BF16), f1_inv.astype(BF16), gf.astype(BF16), gi.astype(BF16))


FFT_PAD = 8


FFT_LW = 128


def _fft_fwd_kernel(ua_ref, ub_ref, f1_ref, g_ref, o_ref, t_sc, *, NB, BS, nj, kg):
    pitch = NB + FFT_PAD
    u_refs = (ua_ref, ub_ref)

    @pl.when(pl.program_id(2) == 0)
    def _():
        f1 = f1_ref[...]

        def body(r, carry):
            xr = jnp.concatenate([u[pl.ds(r, nj, stride=BS), :] for u in u_refs], axis=1).astype(BF16)
            res = _dot(f1, xr)
            for hh in range(2):
                t_sc[hh, pl.ds(pl.multiple_of(r * pitch, 8), NB), :] = res[:, hh * FFT_LW:(hh + 1) * FFT_LW]
            return carry

        lax.fori_loop(0, BS, body, 0, unroll=8)

    k0 = pl.program_id(2) * kg
    for i in range(kg):
        are = jnp.concatenate([t_sc[hh, pl.ds(k0 + i, BS, stride=pitch), :] for hh in range(2)], axis=1)
        aim = jnp.concatenate([t_sc[hh, pl.ds(k0 + i + NB // 2, BS, stride=pitch), :] for hh in range(2)], axis=1)
        a = jnp.concatenate([are, aim], axis=0).astype(BF16)
        o_ref[i] = _dot(g_ref[i], a)


def fft_fwd(u, col0, nbatch, nj, f1, gf, NB, BS, kg=8):
    h = NB // 2
    kg = min(kg, h)
    lw = FFT_LW
    ct = 2 * lw
    return pl.pallas_call(
        functools.partial(_fft_fwd_kernel, NB=NB, BS=BS, nj=nj, kg=kg),
        out_shape=jax.ShapeDtypeStruct((nbatch, h, 2 * BS, HY_W), F32),
        grid=(nbatch, HY_W // ct, h // kg),
        in_specs=[pl.BlockSpec((nj * BS, lw), lambda b, c, k: (b, col0 // lw + 2 * c), pipeline_mode=pl.Buffered(1)),
                  pl.BlockSpec((nj * BS, lw), lambda b, c, k: (b, col0 // lw + 2 * c + 1),
                               pipeline_mode=pl.Buffered(1)),
                  pl.BlockSpec((NB, nj), lambda b, c, k: (0, 0)),
                  pl.BlockSpec((kg, 2 * BS, 2 * BS), lambda b, c, k: (k, 0, 0))],
        out_specs=pl.BlockSpec((None, kg, 2 * BS, ct), lambda b, c, k: (b, k, 0, c)),
        scratch_shapes=[pltpu.VMEM((2, BS * (NB + FFT_PAD), lw), F32)],
        compiler_params=_cp(("parallel", "parallel", "arbitrary")), name="hyena_fft_fwd")(u, u, f1, gf)


def _cmul(u, h, half):
    ure, uim = u[:half], u[half:]
    hre, him = h[:half], h[half:]
    return jnp.concatenate([ure * hre - uim * him, ure * him + uim * hre], axis=0)


def _fft_inv_kernel(us_ref, hs_ref, gi_ref, f1i_ref, o_ref, t_sc, y_sc, *, NB, BS, kg):
    ks = pl.program_id(2)
    pitch = 2 * BS + FFT_PAD
    for i in range(kg):
        y = _cmul(us_ref[i], hs_ref[i], BS).astype(BF16)
        row = pl.multiple_of((ks * kg + i) * pitch, 8)
        res = _dot(gi_ref[i], y)
        for hh in range(2):
            t_sc[hh, pl.ds(row, 2 * BS), :] = res[:, hh * FFT_LW:(hh + 1) * FFT_LW]

    @pl.when(ks == pl.num_programs(2) - 1)
    def _():
        f1i = f1i_ref[...]

        def body(r, carry):
            bre = jnp.concatenate([t_sc[hh, pl.ds(r, NB // 2, stride=pitch), :] for hh in range(2)], axis=1)
            bim = jnp.concatenate([t_sc[hh, pl.ds(r + BS, NB // 2, stride=pitch), :] for hh in range(2)], axis=1)
            b = jnp.concatenate([bre, bim], axis=0).astype(BF16)
            res = _dot(f1i, b)
            for hh in range(2):
                y_sc[hh, pl.ds(r, NB // 2, stride=BS), :] = res[:, hh * FFT_LW:(hh + 1) * FFT_LW]
            return carry

        lax.fori_loop(0, BS, body, 0, unroll=8)
        o_ref[...] = jnp.concatenate([y_sc[0], y_sc[1]], axis=1)


def fft_inv(us, hs, gi, f1i, NB, BS, kg=8):
    nbatch, h = us.shape[0], NB // 2
    kg = min(kg, h)
    L = h * BS
    ct = 2 * FFT_LW
    return pl.pallas_call(
        functools.partial(_fft_inv_kernel, NB=NB, BS=BS, kg=kg),
        out_shape=jax.ShapeDtypeStruct((nbatch * L, HY_W), F32),
        grid=(nbatch, HY_W // ct, h // kg),
        in_specs=[pl.BlockSpec((None, kg, 2 * BS, ct), lambda b, c, k: (b, k, 0, c)),
                  pl.BlockSpec((None, kg, 2 * BS, ct), lambda b, c, k: (0, k, 0, c)),
                  pl.BlockSpec((kg, 2 * BS, 2 * BS), lambda b, c, k: (k, 0, 0)),
                  pl.BlockSpec((h, NB), lambda b, c, k: (0, 0))],
        out_specs=pl.BlockSpec((L, ct), lambda b, c, k: (b, c)),
        scratch_shapes=[pltpu.VMEM((2, h * (2 * BS + FFT_PAD), FFT_LW), F32), pltpu.VMEM((2, L, FFT_LW), F32)],
        compiler_params=_cp(("parallel", "parallel", "arbitrary")), name="hyena_fft_inv")(us, hs, gi, f1i)


def _dft_consts(L):
    N = 2 * L
    k = jnp.arange(L, dtype=jnp.int32)
    n = jnp.arange(N, dtype=jnp.int32)
    re, im = _cis(n[None, :] * (2 * k[:, None] + 1), 2 * N)
    f = jnp.concatenate([re, im], axis=0)
    neg = jnp.where(n >= L, -1.0, 1.0)[None, :]
    fi = (2.0 / N) * jnp.concatenate([re[:, :L].T, im[:, :L].T], axis=1)
    return f[:, :L].astype(BF16), (f * neg).astype(BF16), fi.astype(BF16)


def _cdft_kernel(f_ref, x_ref, o_ref):
    o_ref[...] = _dot(f_ref[...], x_ref[...].astype(BF16))


def dft_fwd(x, f, row0, col0, nbatch, ct=256):
    M, K = f.shape
    return pl.pallas_call(
        _cdft_kernel, out_shape=jax.ShapeDtypeStruct((nbatch, M, HY_W), F32),
        grid=(nbatch, HY_W // ct),
        in_specs=[pl.BlockSpec((M, K), lambda b, c: (0, 0)),
                  pl.BlockSpec((K, ct), lambda b, c: (row0 // K + b, col0 // ct + c))],
        out_specs=pl.BlockSpec((None, M, ct), lambda b, c: (b, 0, c)),
        compiler_params=_cp(("parallel", "parallel")), name="hyena_dft_fwd")(f, x)


def _cdft_inv_kernel(us_ref, hs_ref, fi_ref, o_ref):
    half = us_ref.shape[0] // 2
    o_ref[...] = _dot(fi_ref[...], _cmul(us_ref[...], hs_ref[...], half).astype(BF16))


def dft_inv(us, hs, fi, ct=256):
    nbatch, M2, _ = us.shape
    L = fi.shape[0]
    return pl.pallas_call(
        _cdft_inv_kernel, out_shape=jax.ShapeDtypeStruct((nbatch * L, HY_W), F32),
        grid=(nbatch, HY_W // ct),
        in_specs=[pl.BlockSpec((None, M2, ct), lambda b, c: (b, 0, c)),
                  pl.BlockSpec((None, M2, ct), lambda b, c: (0, 0, c)),
                  pl.BlockSpec((L, M2), lambda b, c: (0, 0))],
        out_specs=pl.BlockSpec((L, ct), lambda b, c: (b, c)),
        compiler_params=_cp(("parallel", "parallel")), name="hyena_dft_inv")(us, hs, fi)


def _hy_gate_kernel(g_ref, y_ref, u_ref, ss_ref, b_ref, o_ref):
    scale = lax.rsqrt(ss_ref[0:1, :] + ss_ref[1:2, :] + 1e-6)
    o_ref[...] = g_ref[...] * (y_ref[...] * scale + u_ref[...] * b_ref[...])


def _hy_gate2_kernel(g_ref, yl_ref, yc_ref, ul_ref, uc_ref, ssl_ref, ssc_ref, b_ref, o_ref, *, n_lat_t):
    is_lat = pl.program_id(0) < n_lat_t
    y = jnp.where(is_lat, yl_ref[...], yc_ref[...])
    uin = jnp.where(is_lat, ul_ref[...], uc_ref[...])
    ss = jnp.where(is_lat, ssl_ref[...], ssc_ref[...])
    scale = lax.rsqrt(ss[0:1, :] + ss[1:2, :] + 1e-6)
    o_ref[...] = g_ref[...] * (y * scale + uin * b_ref[...])


def hy_gate(gate, gcol, grow, y, uin, ucol, urow, ss, order, bias, tm=512):
    M = y.shape[0]
    return pl.pallas_call(
        _hy_gate_kernel, out_shape=jax.ShapeDtypeStruct((M, HY_W), F32), grid=(M // tm,),
        in_specs=[pl.BlockSpec((tm, HY_W), lambda i: (grow // tm + i, gcol // HY_W)),
                  pl.BlockSpec((tm, HY_W), lambda i: (i, 0)),
                  pl.BlockSpec((tm, HY_W), lambda i: (urow // tm + i, ucol // HY_W)),
                  pl.BlockSpec((None, 2, HY_W), lambda i: (order, 0, 0)),
                  pl.BlockSpec((None, 1, HY_W), lambda i: (order, 0, 0))],
        out_specs=pl.BlockSpec((tm, HY_W), lambda i: (i, 0)),
        compiler_params=_cp(("parallel",)), name="hyena_gate")(gate, y, uin, ss, bias)


def hyena(u, hp):
    C = HY_W
    bias = hp['hy_bias'].reshape(2, 1, C)
    NB = BS = int(round(math.sqrt(2 * SEQ)))
    f1d, f1f, f1i, gf, gi = _fft_consts(NB, BS)
    taps, ss = hyena_filter_taps(SEQ, hp)
    ss = ss.reshape(2, 2, C).transpose(1, 0, 2)
    conv_l = lambda zin, zcol, order: fft_inv(fft_fwd(zin, zcol, BATCH, NB // 2, f1d, gf, NB, BS),
                                              fft_fwd(taps, order * C, 1, NB, f1f, gf, NB, BS), gi, f1i, NB, BS)
    z1_lat = hy_gate(u, 0, 0, conv_l(u, 2 * C, 0), u, 2 * C, 0, ss, 0, bias)
    y2_lat = conv_l(z1_lat, 0, 1)
    fd, ff, fi = _dft_consts(CTX)
    taps_c, ss_c = hyena_filter_taps(CTX, hp)
    ss_c = ss_c.reshape(2, 2, C).transpose(1, 0, 2)
    conv_c = lambda zin, zrow, zcol, order: dft_inv(dft_fwd(zin, fd, zrow, zcol, BATCH),
                                                    dft_fwd(taps_c, ff, 0, order * C, 1), fi)
    z1_ctx = hy_gate(u, 0, N_LAT, conv_c(u, N_LAT, 2 * C, 0), u, 2 * C, N_LAT, ss_c, 0, bias)
    y2_ctx = conv_c(z1_ctx, 0, 0, 1)
    tm = 512
    n_lat_t = N_LAT // tm
    lat = lambda i: (jnp.minimum(i, n_lat_t - 1), 0)
    ctx = lambda i: (jnp.maximum(i - n_lat_t, 0), 0)
    return pl.pallas_call(
        functools.partial(_hy_gate2_kernel, n_lat_t=n_lat_t),
        out_shape=jax.ShapeDtypeStruct((N_ROWS, C), F32), grid=(N_ROWS // tm,),
        in_specs=[pl.BlockSpec((tm, C), lambda i: (i, 1)),
                  pl.BlockSpec((tm, C), lat), pl.BlockSpec((tm, C), ctx),
                  pl.BlockSpec((tm, C), lat), pl.BlockSpec((tm, C), ctx),
                  pl.BlockSpec((None, 2, C), lambda i: (1, 0, 0)), pl.BlockSpec((None, 2, C), lambda i: (1, 0, 0)),
                  pl.BlockSpec((None, 1, C), lambda i: (1, 0, 0))],
        out_specs=pl.BlockSpec((tm, C), lambda i: (i, 0)),
        compiler_params=_cp(("parallel",)), name="hyena_gate2",
    )(u, y2_lat, y2_ctx, z1_lat, z1_ctx, ss, ss_c, bias)


EV_SSD_IN = SSD_HEADS * SSD_P
EV_XBC = EV_SSD_IN + 2 * SSD_G * SSD_S
EV_PAD_N = 5760


def even_mixer(x, mods, g_pre, g_post, ep):
    o1 = EV_SSD_IN
    o2 = o1 + EV_XBC
    o3 = o2 + 2 * SSD_HEADS
    w = ep['w_in']
    n_in = w.shape[1]
    hw = 3 * HY_W
    w_perm = jnp.concatenate([w[:, o3:], w[:, :o2], w[:, o2:o3],
                              jnp.zeros((D, EV_PAD_N - n_in), F32)], axis=1).astype(BF16)
    p = norm_mod_matmul(x, g_pre, mods, w_perm, 0, 1, tn=1920, name="even_in_proj")
    xbc = dwconv3(p, hw + o1, EV_XBC, ep['ssd_conv_w'], ep['ssd_conv_b'], True, "ssd_conv")
    u = dwconv3(p, 0, hw, ep['hy_conv_w'], ep['hy_conv_b'], False, "hyena_conv")
    dt_raw = p[:, hw + o2:hw + o2 + 2 * SSD_HEADS]
    y2 = ssd_scan(xbc, dt_raw, ep['ssd_dt_bias'], ep['ssd_a_log'])
    s = ssd_output(y2, xbc, p, hw, ep['ssd_d'], ep['ssd_norm_w'])
    zh = hyena(u, ep)
    return outproj_residual(s, zh, ep['w_out'].astype(BF16), x, g_post, mods, 2, name="even_out_proj")


def _head_sum(x, e, et):
    return _dot2l(_dot2l(x, e), et)


def _rw_prep_kernel(r_ref, k_ref, v_ref, lo_ref, w0_ref, wup_ref, a0_ref, aup_ref, gup_ref, kk_ref, ka_ref,
                    rk_ref, e_ref, et_ref, lw_ref, kd_ref, be_ref, kap_ref, g_ref, bonus_ref):
    r, k, v = r_ref[...], k_ref[...], v_ref[...]
    lo = lo_ref[...]
    wc, ac, gc = lo[:, 0:64], lo[:, 64:128], lo[:, 128:384]
    e, et = e_ref[...], et_ref[...]
    kk = k * kk_ref[...]
    kap = kk * lax.rsqrt(_head_sum(kk * kk, e, et) + 1e-12)
    kap_ref[...] = kap
    g_ref[...] = _dot(_sigmoid(gc).astype(BF16), gup_ref[...].astype(BF16))
    kd_sum = jnp.zeros_like(k)
    for d in range(2):
        wlog = -_softplus(-(w0_ref[d:d + 1, :] + _dot3(jnp.tanh(wc), wup_ref[d]))) - 0.5
        lw_ref[d] = -jnp.exp(wlog)
        a = _sigmoid(a0_ref[d:d + 1, :] + _dot(ac.astype(BF16), aup_ref[d].astype(BF16)))
        kd = k * (1.0 + (a - 1.0) * ka_ref[...])
        kd_ref[d] = kd
        be_ref[d] = kap * a
        kd_sum = kd_sum + kd
    bonus_ref[...] = _head_sum(r * kd_sum * rk_ref[...], e, et) * v


def rwkv_prepare(code, lora, op, tm=256):
    W = RW_H * RW_N
    heads = jnp.arange(W, dtype=jnp.int32) // RW_N
    e = (heads[:, None] == jnp.arange(128, dtype=jnp.int32)[None, :]).astype(BF16)
    et = jnp.transpose(e)
    gup = jnp.pad(op['g_up'], ((0, 256 - op['g_up'].shape[0]), (0, 0)))
    row = lambda a: a.reshape(1, W)
    full2 = lambda shape: pl.BlockSpec(shape, lambda i: (0,) * len(shape))
    outs = pl.pallas_call(
        _rw_prep_kernel,
        out_shape=(jax.ShapeDtypeStruct((2, N_ROWS, W), F32), jax.ShapeDtypeStruct((2, N_ROWS, W), F32),
                   jax.ShapeDtypeStruct((2, N_ROWS, W), F32), jax.ShapeDtypeStruct((N_ROWS, W), F32),
                   jax.ShapeDtypeStruct((N_ROWS, W), F32), jax.ShapeDtypeStruct((N_ROWS, W), F32)),
        grid=(N_ROWS // tm,),
        in_specs=[pl.BlockSpec((tm, W), lambda i: (i, 0)), pl.BlockSpec((tm, W), lambda i: (i, 1)),
                  pl.BlockSpec((tm, W), lambda i: (i, 2)), pl.BlockSpec((tm, 384), lambda i: (i, 0)),
                  full2((2, W)), full2((2, 64, W)), full2((2, W)), full2((2, 64, W)), full2((256, W)),
                  full2((1, W)), full2((1, W)), full2((1, W)), full2((W, 128)), full2((128, W))],
        out_specs=(pl.BlockSpec((2, tm, W), lambda i: (0, i, 0)), pl.BlockSpec((2, tm, W), lambda i: (0, i, 0)),
                   pl.BlockSpec((2, tm, W), lambda i: (0, i, 0)), pl.BlockSpec((tm, W), lambda i: (i, 0)),
                   pl.BlockSpec((tm, W), lambda i: (i, 0)), pl.BlockSpec((tm, W), lambda i: (i, 0))),
        compiler_params=_cp(("parallel",)), name="rwkv_prepare",
    )(code, code, code, lora, op['w0'], op['w_up'], op['a0'], op['a_up'], gup, row(op['k_k']), row(op['k_a']),
      row(op['r_k']), e, et)
    return outs


def _rw_scan_kernel(r_ref, v_ref, lw_ref, kd_ref, be_ref, kap_ref, y_ref, st_ref):
    d = pl.program_id(0)
    c = pl.program_id(2)
    C = RW_CHUNK
    N = RW_N

    @pl.when(c == 0)
    def _():
        st_ref[...] = jnp.zeros_like(st_ref)

    isb = d == 1
    sgn = 1 - 2 * d
    ii = lax.broadcasted_iota(jnp.int32, (C, C), 0)
    jj = lax.broadcasted_iota(jnp.int32, (C, C), 1)
    dif = sgn * (ii - jj)
    incl = dif >= 0
    strict = dif > 0
    tri = incl.astype(BF16)
    eye = (ii == jj).astype(F32)
    blk = [(ii >> s) == (jj >> s) for s in range(3, C.bit_length() - 1)]
    masks = [blk[0]] + [blk[l] & ~blk[l - 1] for l in range(1, len(blk))] + [~blk[-1]]
    lw = lw_ref[...]
    cum = _dot2r(tri, lw)
    ec = jnp.exp(cum)
    en = jnp.exp(-cum)
    ea = jnp.exp(cum - lw)
    last = jnp.where(isb, cum[0:1, :], cum[C - 1:C, :])
    el = jnp.exp(last - cum)
    kap = kap_ref[...]
    r = r_ref[...]
    v = v_ref[...]
    a_t = -kap * ea
    r_t = r * ec
    b_t = be_ref[...] * en
    k_t = kd_ref[...] * en
    b_l = be_ref[...] * el
    k_l = kd_ref[...] * el
    pc = jnp.exp(last)
    H = range(RW_H)
    sl = [slice(h * N, (h + 1) * N) for h in H]
    bd = lambda a, b: _dot(a.astype(BF16), b.astype(BF16))
    tn = lambda a, b: lax.dot_general(a, b, (((0,), (0,)), ((), ())), preferred_element_type=F32)
    sc = [_dot_nt(jnp.concatenate([a_t[:, sl[h]], r_t[:, sl[h]]], axis=0).astype(BF16),
                  jnp.concatenate([b_t[:, sl[h]], k_t[:, sl[h]]], axis=0).astype(BF16)) for h in H]
    n_ab = [jnp.where(strict, sc[h][0:C, 0:C], 0.0) for h in H]
    a_ak = [jnp.where(strict, sc[h][0:C, C:2 * C], 0.0).astype(BF16) for h in H]
    m_rb = [jnp.where(incl, sc[h][C:2 * C, 0:C], 0.0).astype(BF16) for h in H]
    m_rk = [jnp.where(incl, sc[h][C:2 * C, C:2 * C], 0.0).astype(BF16) for h in H]
    vh = [v[:, sl[h]].astype(BF16) for h in H]
    d0 = [jnp.where(masks[0], n_ab[h], 0.0) for h in H]
    d2 = [bd(d0[h], d0[h]) for h in H]
    d4 = [bd(d2[h], d2[h]) for h in H]
    t = [bd(eye + d0[h], eye + d2[h]) for h in H]
    t = [bd(t[h], eye + d4[h]) for h in H]
    for m in masks[1:]:
        et = [bd(jnp.where(m, n_ab[h], 0.0), t[h]) for h in H]
        t = [t[h] + bd(t[h], et[h]) for h in H]
    amv = [_dot(jnp.concatenate([a_ak[h], m_rk[h]], axis=0), vh[h]) for h in H]
    wub = [bd(t[h], jnp.concatenate([a_t[:, sl[h]], amv[h][0:C]], axis=1)).astype(BF16) for h in H]
    kv = [tn(k_l[:, sl[h]].astype(BF16), vh[h]) for h in H]
    qy = [_dot(m_rb[h], wub[h]) + jnp.concatenate([r_t[:, sl[h]], amv[h][C:2 * C]], axis=1) for h in H]
    pp = [tn(b_l[:, sl[h]].astype(BF16), wub[h]) + jnp.concatenate([eye[0:N, 0:N] * pc[:, sl[h]], kv[h]], axis=1)
          for h in H]
    h_old = [st_ref[h] for h in H]
    ys = [bd(qy[h][:, 0:N], h_old[h]) + qy[h][:, N:2 * N] for h in H]
    for h in H:
        st_ref[h] = _dot3(pp[h][:, 0:N], h_old[h]) + pp[h][:, N:2 * N]
    y_ref[...] = jnp.concatenate(ys, axis=1)


def _rw_rowblock(d, b, c):
    n_ctx = CTX // RW_CHUNK
    n_lat = SEQ // RW_CHUNK
    cc = jnp.where(d == 0, c, n_ctx - 1 - c)
    lc = jnp.where(d == 0, c - n_ctx, n_ctx + n_lat - 1 - c)
    return jnp.where(c < n_ctx, N_LAT // RW_CHUNK + b * n_ctx + cc, b * n_lat + lc)


def rwkv_scan(code, lw, kd, be, kap):
    W = RW_H * RW_N
    nch = (CTX + SEQ) // RW_CHUNK
    rb = lambda d, b, c: _rw_rowblock(d, b, c)
    return pl.pallas_call(
        _rw_scan_kernel,
        out_shape=jax.ShapeDtypeStruct((2, N_ROWS, W), F32),
        grid=(2, BATCH, nch),
        in_specs=[pl.BlockSpec((RW_CHUNK, W), lambda d, b, c: (rb(d, b, c), 0)),
                  pl.BlockSpec((RW_CHUNK, W), lambda d, b, c: (rb(d, b, c), 2)),
                  pl.BlockSpec((None, RW_CHUNK, W), lambda d, b, c: (d, rb(d, b, c), 0)),
                  pl.BlockSpec((None, RW_CHUNK, W), lambda d, b, c: (d, rb(d, b, c), 0)),
                  pl.BlockSpec((None, RW_CHUNK, W), lambda d, b, c: (d, rb(d, b, c), 0)),
                  pl.BlockSpec((RW_CHUNK, W), lambda d, b, c: (rb(d, b, c), 0))],
        out_specs=pl.BlockSpec((None, RW_CHUNK, W), lambda d, b, c: (d, rb(d, b, c), 0)),
        scratch_shapes=[pltpu.VMEM((RW_H, RW_N, RW_N), F32)],
        compiler_params=_cp(("parallel", "parallel", "arbitrary")), name="rwkv_scan",
    )(code, code, lw, kd, be, kap)


def _rw_out_kernel(yf_ref, yb_ref, bonus_ref, g_ref, lnw_ref, lnb_ref, e_ref, et_ref, o_ref):
    e, et = e_ref[...], et_ref[...]
    y = yf_ref[...] + yb_ref[...]
    mean = _head_sum(y, e, et) * (1.0 / RW_N)
    yc = y - mean
    var = _head_sum(yc * yc, e, et) * (1.0 / RW_N)
    yn = yc * lax.rsqrt(var + RW_GN_EPS) * lnw_ref[...] + lnb_ref[...]
    o_ref[...] = (yn + bonus_ref[...]) * g_ref[...]


def rwkv_output(y2, bonus, g, op, tm=512):
    W = RW_H * RW_N
    heads = jnp.arange(W, dtype=jnp.int32) // RW_N
    e = (heads[:, None] == jnp.arange(128, dtype=jnp.int32)[None, :]).astype(BF16)
    et = jnp.transpose(e)
    M = N_LAT
    return pl.pallas_call(
        _rw_out_kernel, out_shape=jax.ShapeDtypeStruct((M, W), F32), grid=(M // tm,),
        in_specs=[pl.BlockSpec((None, tm, W), lambda i: (0, i, 0)), pl.BlockSpec((None, tm, W), lambda i: (1, i, 0)),
                  pl.BlockSpec((tm, W), lambda i: (i, 0)), pl.BlockSpec((tm, W), lambda i: (i, 0)),
                  pl.BlockSpec((1, W), lambda i: (0, 0)), pl.BlockSpec((1, W), lambda i: (0, 0)),
                  pl.BlockSpec((W, 128), lambda i: (0, 0)), pl.BlockSpec((128, W), lambda i: (0, 0))],
        out_specs=pl.BlockSpec((tm, W), lambda i: (i, 0)),
        compiler_params=_cp(("parallel",)), name="rwkv_output",
    )(y2, y2, bonus, g, op['ln_w'].reshape(1, W), op['ln_b'].reshape(1, W), e, et)


AT_Q = RW_H * AT_HD
AT_KW = AT_KV * AT_HD
AT_TQ = 512
AT_TK = 768


def _rope_tables(tm):
    half = AT_HD // 2
    inv = 10000.0 ** (-jnp.arange(0, half, 2, dtype=F32) / half)
    pos = jnp.arange(SEQ, dtype=jnp.int32)
    row = (pos // GRID_W).astype(F32)[:, None] * inv
    col = (pos % GRID_W).astype(F32)[:, None] * inv
    cos_h = jnp.concatenate([jnp.cos(row), jnp.cos(row), jnp.cos(col), jnp.cos(col)], axis=1)
    sin_h = jnp.concatenate([-jnp.sin(row), jnp.sin(row), -jnp.sin(col), jnp.sin(col)], axis=1)
    cos_t = jnp.concatenate([jnp.tile(cos_h, (1, 2)), jnp.ones((tm, 128), F32)], axis=0)
    sin_t = jnp.concatenate([jnp.tile(sin_h, (1, 2)), jnp.zeros((tm, 128), F32)], axis=0)
    return cos_t, sin_t


def _rot_partner(x):
    q = AT_HD // 4
    w = x.shape[1]
    lane = lax.broadcasted_iota(jnp.int32, x.shape, 1)
    return jnp.where((lane % (2 * q)) < q, pltpu.roll(x, w - q, 1), pltpu.roll(x, q, 1))


def _at_prep_kernel(q_ref, k_ref, v_ref, cos_ref, sin_ref, qn_ref, kn_ref, e_ref, et_ref, qo_ref, ko_ref, vo_ref):
    e, et = e_ref[...], et_ref[...]
    cos2, sin2 = cos_ref[...], sin_ref[...]

    def norm_rope(x, gain, nrep):
        ms = _head_sum(x * x, e[:x.shape[1]], et[:, :x.shape[1]]) * (1.0 / AT_HD)
        xn = x * lax.rsqrt(ms + EPS) * gain
        cos = jnp.tile(cos2, (1, nrep))
        sin = jnp.tile(sin2, (1, nrep))
        return xn * cos + _rot_partner(xn) * sin

    qn = norm_rope(q_ref[...], qn_ref[...], AT_Q // 128) * (AT_HD ** -0.5 * math.log2(math.e))
    qo_ref[...] = jnp.transpose(qn).astype(BF16)
    ko_ref[...] = norm_rope(k_ref[...], kn_ref[...], AT_KW // 128).astype(BF16)
    vo_ref[...] = jnp.transpose(v_ref[...]).astype(BF16)


def attention_prepare(p, q_norm, k_norm, tm=256):
    cos_t, sin_t = _rope_tables(tm)
    heads = jnp.arange(AT_Q, dtype=jnp.int32) // AT_HD
    e = (heads[:, None] == jnp.arange(128, dtype=jnp.int32)[None, :]).astype(BF16)
    et = jnp.transpose(e)
    tab = lambda i: jnp.where(i * tm < N_LAT, ((i * tm) % SEQ) // tm, SEQ // tm)
    n_lat_t, n_seq_t, n_ctx_t = N_LAT // tm, SEQ // tm, CTX // tm
    kvb = lambda i: jnp.where(i < n_lat_t, (i // n_seq_t) * (n_seq_t + n_ctx_t) + n_ctx_t + i % n_seq_t,
                              ((i - n_lat_t) // n_ctx_t) * (n_seq_t + n_ctx_t) + (i - n_lat_t) % n_ctx_t)
    qcol = (3 * RW_H * RW_N) // AT_Q
    kcol = (3 * RW_H * RW_N + AT_Q) // AT_KW
    return pl.pallas_call(
        _at_prep_kernel,
        out_shape=(jax.ShapeDtypeStruct((AT_Q, N_ROWS), BF16), jax.ShapeDtypeStruct((N_ROWS, AT_KW), BF16),
                   jax.ShapeDtypeStruct((AT_KW, N_ROWS), BF16)),
        grid=(N_ROWS // tm,),
        in_specs=[pl.BlockSpec((tm, AT_Q), lambda i: (i, qcol)),
                  pl.BlockSpec((tm, AT_KW), lambda i: (i, kcol)),
                  pl.BlockSpec((tm, AT_KW), lambda i: (i, kcol + 1)),
                  pl.BlockSpec((tm, 128), lambda i: (tab(i), 0)),
                  pl.BlockSpec((tm, 128), lambda i: (tab(i), 0)),
                  pl.BlockSpec((1, AT_Q), lambda i: (0, 0)),
                  pl.BlockSpec((1, AT_KW), lambda i: (0, 0)),
                  pl.BlockSpec((AT_Q, 128), lambda i: (0, 0)),
                  pl.BlockSpec((128, AT_Q), lambda i: (0, 0))],
        out_specs=(pl.BlockSpec((AT_Q, tm), lambda i: (0, i)), pl.BlockSpec((tm, AT_KW), lambda i: (kvb(i), 0)),
                   pl.BlockSpec((AT_KW, tm), lambda i: (0, kvb(i)))),
        compiler_params=_cp(("parallel",)), name="attn_prepare",
    )(p, p, p, cos_t, sin_t, jnp.tile(q_norm, AT_Q // AT_HD).reshape(1, AT_Q),
      jnp.tile(k_norm, AT_KV).reshape(1, AT_KW), e, et)


AT_REBASE = 64.0
AT_SEED = 128


def _flash_t_kernel(qt_ref, k_ref, vt_ref, o_ref, m_sc, l_sc, acc_sc, p_sc):
    ki = pl.program_id(2)
    nq = AT_Q // AT_HD
    gq = nq // AT_KV

    @pl.when(ki == 0)
    def _():
        for g in range(AT_KV):
            k0 = k_ref[0:AT_SEED, g * AT_HD:(g + 1) * AT_HD]
            for h in range(g * gq, (g + 1) * gq):
                m_sc[h] = jnp.max(_dot(k0, qt_ref[h * AT_HD:(h + 1) * AT_HD, :]), axis=0, keepdims=True)
        l_sc[...] = jnp.zeros_like(l_sc)
        acc_sc[...] = jnp.zeros_like(acc_sc)

    def k_group(g):
        return k_ref[:, g * AT_HD:(g + 1) * AT_HD]

    def v_group(g):
        return jnp.concatenate([vt_ref[g * AT_HD:(g + 1) * AT_HD, :],
                                jnp.ones((16, vt_ref.shape[1]), BF16)], axis=0)

    def scores(g):
        kg = k_group(g)
        return [_dot(kg, qt_ref[h * AT_HD:(h + 1) * AT_HD, :]) for h in range(g * gq, (g + 1) * gq)]

    gap = None
    for g in range(AT_KV):
        st = scores(g)
        m_cur = [m_sc[h] for h in range(g * gq, (g + 1) * gq)]
        for i in range(gq):
            over_i = jnp.max(st[i], axis=0, keepdims=True) - m_cur[i]
            gap = over_i if gap is None else jnp.maximum(gap, over_i)
            p_sc[g * gq + i] = jnp.exp2(st[i] - m_cur[i]).astype(BF16)
    rebase = jnp.max(gap) > AT_REBASE

    @pl.when(jnp.logical_not(rebase))
    def _():
        for g in range(AT_KV):
            vtg1 = v_group(g)
            hs = range(g * gq, (g + 1) * gq)
            pv = [_dot(vtg1, p_sc[h]) for h in hs]
            for i, h in enumerate(hs):
                l_sc[h] = l_sc[h] + pv[i][AT_HD:AT_HD + 1, :]
                rows = pl.ds(h * AT_HD, AT_HD)
                acc_sc[rows, :] = acc_sc[rows, :] + pv[i][0:AT_HD, :]

    @pl.when(rebase)
    def _():
        for g in range(AT_KV):
            vtg1 = v_group(g)
            hs = range(g * gq, (g + 1) * gq)
            st = scores(g)
            m_old = [m_sc[h] for h in hs]
            m_new = [jnp.maximum(m_old[i], jnp.max(st[i], axis=0, keepdims=True)) for i in range(gq)]
            alpha = [jnp.exp2(m_old[i] - m_new[i]) for i in range(gq)]
            pt = [jnp.exp2(st[i] - m_new[i]).astype(BF16) for i in range(gq)]
            pv = [_dot(vtg1, pt[i]) for i in range(gq)]
            for i, h in enumerate(hs):
                l_sc[h] = alpha[i] * l_sc[h] + pv[i][AT_HD:AT_HD + 1, :]
                m_sc[h] = m_new[i]
                rows = pl.ds(h * AT_HD, AT_HD)
                acc_sc[rows, :] = alpha[i] * acc_sc[rows, :] + pv[i][0:AT_HD, :]

    @pl.when(ki == pl.num_programs(2) - 1)
    def _():
        inv = jnp.concatenate([jnp.broadcast_to(1.0 / l_sc[h], (AT_HD, l_sc.shape[2])) for h in range(nq)], axis=0)
        o_ref[...] = jnp.transpose(acc_sc[...] * inv)


def flash_attention_t(qt, k, vt):
    nq = AT_Q // AT_HD
    nk = (CTX + SEQ) // AT_TK
    kv_rb = lambda b, ki: b * nk + ki
    return pl.pallas_call(
        _flash_t_kernel,
        out_shape=jax.ShapeDtypeStruct((N_LAT, AT_Q), F32),
        grid=(BATCH, SEQ // AT_TQ, nk),
        in_specs=[pl.BlockSpec((AT_Q, AT_TQ), lambda b, qi, ki: (0, b * (SEQ // AT_TQ) + qi)),
                  pl.BlockSpec((AT_TK, AT_KW), lambda b, qi, ki: (kv_rb(b, ki), 0)),
                  pl.BlockSpec((AT_KW, AT_TK), lambda b, qi, ki: (0, kv_rb(b, ki)))],
        out_specs=pl.BlockSpec((AT_TQ, AT_Q), lambda b, qi, ki: (b * (SEQ // AT_TQ) + qi, 0)),
        scratch_shapes=[pltpu.VMEM((nq, 1, AT_TQ), F32), pltpu.VMEM((nq, 1, AT_TQ), F32),
                        pltpu.VMEM((AT_Q, AT_TQ), F32), pltpu.VMEM((nq, AT_TK, AT_TQ), BF16)],
        compiler_params=_cp(("parallel", "parallel", "arbitrary")), name="flash_attention")(qt, k, vt)


OD_PAD_N = 4992


def odd_mixer(x, mods, g_pre, g_post, op):
    W = RW_H * RW_N
    w = op['w_in']
    c3 = 3 * W
    code_n = c3 + 64 + 64 + 160
    w_perm = jnp.concatenate([w[:, :c3], w[:, code_n:], w[:, c3:code_n],
                              jnp.zeros((D, OD_PAD_N - w.shape[1]), F32)], axis=1).astype(BF16)
    p = norm_mod_matmul(x, g_pre, mods, w_perm, 0, 1, tn=1664, name="odd_in_proj")
    mu = op['mu']
    taps = lambda m: jnp.stack([0.5 * m, 1.0 - m, 0.5 * m], axis=1)
    code = dwconv3(p, 0, c3, taps(mu[:c3]), jnp.zeros((c3,), F32), False, "rwkv_shift")
    lo_col = c3 + AT_Q + 2 * AT_KW
    mu_lo = jnp.pad(mu[c3:], (0, 384 - (code_n - c3)))
    lora = dwconv3(p, lo_col, 384, taps(mu_lo), jnp.zeros((384,), F32), False, "rwkv_shift_lora", cb=384)
    lw, kd, be, kap, g, bonus = rwkv_prepare(code, lora, op)
    y2 = rwkv_scan(code, lw, kd, be, kap)
    o_l = rwkv_output(y2, bonus, g, op)
    q, k, v = attention_prepare(p, op['q_norm'], op['k_norm'])
    a_l = flash_attention_t(q, k, v)
    return outproj_residual(o_l, a_l, op['w_out'].astype(BF16), x, g_post, mods, 2, name="odd_out_proj")


MOE_T = 256
MOE_CAP = 64
MOE_EPS = 4


def _router2_kernel(x_ref, g_ref, mod_ref, rw_ref, rb_ref, s1_ref, s3_ref, s2_ref, t_ref, wt_ref, cnt_ref, sh_ref):
    x = x_ref[...]
    ms = jnp.mean(x * x, axis=-1, keepdims=True)
    t = x * lax.rsqrt(ms + EPS) * g_ref[...] * (1.0 + mod_ref[4:5, :]) + mod_ref[3:4, :]
    tb = t.astype(BF16)
    t_ref[...] = tb
    sh_ref[...] = _dot((_silu(_dot(tb, s1_ref[...])) * _dot(tb, s3_ref[...])).astype(BF16), s2_ref[...])
    th, tl = _split(t)
    wh, wl = _split(rw_ref[...])
    lg = _dot_nt(wh, th) + (_dot_nt(wh, tl) + _dot_nt(wl, th))
    sc = _sigmoid(lg)
    sel = sc + rb_ref[...]
    tm = sel.shape[1]
    gsz = N_EXP // N_GRP
    ninf = -jnp.inf
    sel3 = sel.reshape(N_GRP, gsz, tm)
    i3 = lax.broadcasted_iota(jnp.int32, sel3.shape, 1)
    m1 = jnp.max(sel3, axis=1, keepdims=True)
    first = jnp.min(jnp.where(sel3 == m1, i3, gsz), axis=1, keepdims=True)
    m2 = jnp.max(jnp.where(i3 == first, ninf, sel3), axis=1, keepdims=True)
    grp = (m1 + m2).reshape(N_GRP, tm)
    gi = lax.broadcasted_iota(jnp.int32, grp.shape, 0)
    gmask = jnp.zeros(grp.shape, F32)
    for _ in range(TOPK_GRP):
        m = jnp.max(grp, axis=0, keepdims=True)
        pick = jnp.min(jnp.where(grp == m, gi, N_GRP), axis=0, keepdims=True)
        hit = gi == pick
        gmask = jnp.where(hit, 1.0, gmask)
        grp = jnp.where(hit, ninf, grp)
    emask = jnp.broadcast_to(gmask.reshape(N_GRP, 1, tm), (N_GRP, gsz, tm)).reshape(N_EXP, tm)
    msel = jnp.where(emask > 0.5, sel, ninf)
    ei = lax.broadcasted_iota(jnp.int32, msel.shape, 0)
    chosen = jnp.zeros(msel.shape, F32)
    for _ in range(TOP_K):
        m = jnp.max(msel, axis=0, keepdims=True)
        pick = jnp.min(jnp.where(msel == m, ei, N_EXP), axis=0, keepdims=True)
        hit = ei == pick
        chosen = jnp.where(hit, 1.0, chosen)
        msel = jnp.where(hit, ninf, msel)
    w = chosen * sc
    wt = w / jnp.sum(w, axis=0, keepdims=True) * ROUTED_SCALE
    for s in range(tm // MOE_T):
        ws = wt[:, s * MOE_T:(s + 1) * MOE_T]
        wt_ref[s] = ws
        cnt_ref[s] = jnp.sum((ws > 0.0).astype(F32), axis=1, keepdims=True).astype(jnp.int32)


def moe_router2(x, M, g, mods, mp):
    tm = 2 * MOE_T
    ns = tm // MOE_T
    full = lambda shape: pl.BlockSpec(shape, lambda i: (0,) * len(shape))
    return pl.pallas_call(
        _router2_kernel,
        out_shape=(jax.ShapeDtypeStruct((M, D), BF16), jax.ShapeDtypeStruct((M // MOE_T, N_EXP, MOE_T), F32),
                   jax.ShapeDtypeStruct((M // MOE_T, N_EXP, 1), jnp.int32), jax.ShapeDtypeStruct((M, D), F32)),
        grid=(M // tm,),
        in_specs=[pl.BlockSpec((tm, D), lambda i: (i, 0)), full((1, D)),
                  pl.BlockSpec((None, 6, D), lambda i: (_seq_of_rowblock(i, tm), 0, 0)),
                  full((N_EXP, D)), full((N_EXP, 1)), full((D, EXP_FF)), full((D, EXP_FF)), full((EXP_FF, D))],
        out_specs=(pl.BlockSpec((tm, D), lambda i: (i, 0)), pl.BlockSpec((ns, N_EXP, MOE_T), lambda i: (i, 0, 0)),
                   pl.BlockSpec((ns, N_EXP, 1), lambda i: (i, 0, 0)), pl.BlockSpec((tm, D), lambda i: (i, 0))),
        compiler_params=_cp(("parallel",)), name="moe_router",
    )(x, g.reshape(1, D), mods, jnp.transpose(mp['router_w']), mp['router_bias'].reshape(N_EXP, 1),
      mp['s1'].astype(BF16), mp['s3'].astype(BF16), mp['s2'].astype(BF16))


def _moe3_kernel(cnt_ref, ovf_ref, t_ref, wt_ref, w1_ref, w3_ref, w2_ref, o_ref, rank_sc, *, nsub):
    i = pl.program_id(0)
    eb = pl.program_id(1)
    T, CAP, EPS = MOE_T, MOE_CAP, MOE_EPS

    @pl.when(eb == 0)
    def _():
        before = (lax.broadcasted_iota(jnp.int32, (T, T), 0) < lax.broadcasted_iota(jnp.int32, (T, T), 1))
        before = before.astype(BF16)
        for s in range(nsub):
            rank_sc[s] = _dot((wt_ref[s] > 0.0).astype(BF16), before)
        o_ref[...] = jnp.zeros_like(o_ref)

    slot = lax.broadcasted_iota(jnp.int32, (CAP, T), 0).astype(F32)

    def one_hot(s, e, first_slot):
        w_row = wt_ref[s, pl.ds(e, 1), :]
        r_row = rank_sc[s, pl.ds(e, 1), :]
        hit = ((r_row - first_slot) == slot) & (w_row > 0.0)
        w_slot = jnp.sum(jnp.where(hit, w_row, 0.0), axis=1, keepdims=True)
        return hit.astype(F32).astype(BF16), w_slot

    def swiglu(xg, j):
        h = _silu(_dot(xg, w1_ref[j])) * _dot(xg, w3_ref[j])
        return _dot(h.astype(BF16), w2_ref[j])

    hot = [[one_hot(s, eb * EPS + j, 0.0) for j in range(EPS)] for s in range(nsub)]
    pb = [jnp.concatenate([hot[s][j][0] for j in range(EPS)], axis=0) for s in range(nsub)]
    xg = [_dot(pb[s], t_ref[s * T:(s + 1) * T, :]).astype(BF16) for s in range(nsub)]
    y = [swiglu(jnp.concatenate([xg[s][j * CAP:(j + 1) * CAP] for s in range(nsub)], axis=0), j)
         for j in range(EPS)]
    for s in range(nsub):
        yw = jnp.concatenate([y[j][s * CAP:(s + 1) * CAP] * hot[s][j][1] for j in range(EPS)], axis=0)
        o_ref[s * T:(s + 1) * T, :] += lax.dot_general(pb[s], yw.astype(BF16), (((0,), (0,)), ((), ())),
                                                       preferred_element_type=F32)

    def pair(idx, carry):
        s = idx // EPS
        j = idx % EPS
        e = eb * EPS + j
        n_tok = cnt_ref[(i * nsub + s) * N_EXP + e]
        rows = pl.ds(pl.multiple_of(s * T, T), T)

        def chunk(ci, c2):
            p1, w_slot = one_hot(s, e, (ci * CAP).astype(F32))
            yw = (swiglu(_dot(p1, t_ref[rows, :]).astype(BF16), j) * w_slot).astype(BF16)
            o_ref[rows, :] += lax.dot_general(p1, yw, (((0,), (0,)), ((), ())), preferred_element_type=F32)
            return c2

        lax.fori_loop(1, (n_tok + CAP - 1) // CAP, chunk, 0)
        return carry

    @pl.when(ovf_ref[i * (N_EXP // EPS) + eb] > 0)
    def _():
        lax.fori_loop(0, nsub * EPS, pair, 0)


def _moe_out_kernel(r_ref, sh_ref, x_ref, g_ref, mod_ref, o_ref):
    f = r_ref[...] + sh_ref[...]
    ms = jnp.mean(f * f, axis=-1, keepdims=True)
    o_ref[...] = x_ref[...] + mod_ref[5:6, :] * (f * lax.rsqrt(ms + EPS) * g_ref[...])


def moe_layer3(x, M, g_pre, g_post, mods, mp, nsub):
    t, wt, cnt, sh = moe_router2(x, M, g_pre, mods, mp)
    T = MOE_T
    TS = nsub * T
    ovf = jnp.any(cnt.reshape(M // TS, nsub, N_EXP // MOE_EPS, MOE_EPS) > MOE_CAP, axis=(1, 3)).astype(jnp.int32)
    grid_spec = pltpu.PrefetchScalarGridSpec(
        num_scalar_prefetch=2, grid=(M // TS, N_EXP // MOE_EPS),
        in_specs=[pl.BlockSpec((TS, D), lambda i, e, c, o: (i, 0)),
                  pl.BlockSpec((nsub, N_EXP, T), lambda i, e, c, o: (i, 0, 0)),
                  pl.BlockSpec((MOE_EPS, D, EXP_FF), lambda i, e, c, o: (e, 0, 0)),
                  pl.BlockSpec((MOE_EPS, D, EXP_FF), lambda i, e, c, o: (e, 0, 0)),
                  pl.BlockSpec((MOE_EPS, EXP_FF, D), lambda i, e, c, o: (e, 0, 0))],
        out_specs=pl.BlockSpec((TS, D), lambda i, e, c, o: (i, 0)),
        scratch_shapes=[pltpu.VMEM((nsub, N_EXP, T), F32)])
    routed = pl.pallas_call(
        functools.partial(_moe3_kernel, nsub=nsub), out_shape=jax.ShapeDtypeStruct((M, D), F32),
        grid_spec=grid_spec, compiler_params=_cp(("parallel", "arbitrary")), name="moe_experts",
    )(cnt.reshape(-1), ovf.reshape(-1), t, wt, mp['w1'].astype(BF16), mp['w3'].astype(BF16),
      mp['w2'].astype(BF16))
    tm = 512
    return pl.pallas_call(
        _moe_out_kernel, out_shape=jax.ShapeDtypeStruct((M, D), F32), grid=(M // tm,),
        in_specs=[pl.BlockSpec((tm, D), lambda i: (i, 0)), pl.BlockSpec((tm, D), lambda i: (i, 0)),
                  pl.BlockSpec((tm, D), lambda i: (i, 0)), pl.BlockSpec((1, D), lambda i: (0, 0)),
                  pl.BlockSpec((None, 6, D), lambda i: (_seq_of_rowblock(i, tm), 0, 0))],
        out_specs=pl.BlockSpec((tm, D), lambda i: (i, 0)),
        compiler_params=_cp(("parallel",)), name="moe_output")(routed, sh, x, g_post.reshape(1, D), mods)


def kernel(x, c, ctx, c_ctx, mod_w, mod_b, norm_mix_pre, norm_mix_post, norm_ffn_pre, norm_ffn_post, router_w, router_bias, expert_w1, expert_w3, expert_w2, shared_w1, shared_w3, shared_w2, ev_w_in, ev_w_out, ssd_conv_w, ssd_conv_b, ssd_dt_bias, ssd_a_log, ssd_d, ssd_norm_w, hy_conv_w, hy_conv_b, hy_mlp_w0, hy_mlp_b0, hy_freq0, hy_mlp_w1, hy_mlp_b1, hy_freq1, hy_mlp_w2, hy_bias, od_w_in, od_w_out, rw_mu, rw_w0, rw_w_up, rw_a0, rw_a_up, rw_g_up, rw_k_k, rw_k_a, rw_r_k, rw_ln_w, rw_ln_b, at_q_norm, at_k_norm):
    xs = jnp.concatenate([x.reshape(N_LAT, D), ctx.reshape(BATCH * CTX, D)], axis=0)
    cvecs = jnp.zeros((8, D), F32).at[0:BATCH].set(c).at[BATCH].set(c_ctx)
    assert mod_w.shape[0] == 2, "one even (SSD | Hyena) layer followed by one odd (RWKV | attention) layer"

    def moe_params(i):
        return dict(router_w=router_w[i], router_bias=router_bias[i], w1=expert_w1[i], w3=expert_w3[i],
                    w2=expert_w2[i], s1=shared_w1[i], s3=shared_w3[i], s2=shared_w2[i])

    mods = modulation(cvecs, mod_w[0], mod_b[0])[:BATCH + 1].reshape(BATCH + 1, 6, D)
    ep = dict(w_in=ev_w_in[0], w_out=ev_w_out[0], ssd_conv_w=ssd_conv_w[0], ssd_conv_b=ssd_conv_b[0],
              ssd_dt_bias=ssd_dt_bias[0], ssd_a_log=ssd_a_log[0], ssd_d=ssd_d[0], ssd_norm_w=ssd_norm_w[0],
              hy_conv_w=hy_conv_w[0], hy_conv_b=hy_conv_b[0], hy_mlp_w0=hy_mlp_w0[0], hy_mlp_b0=hy_mlp_b0[0],
              hy_freq0=hy_freq0[0], hy_mlp_w1=hy_mlp_w1[0], hy_mlp_b1=hy_mlp_b1[0], hy_freq1=hy_freq1[0],
              hy_mlp_w2=hy_mlp_w2[0], hy_bias=hy_bias[0])
    xs = even_mixer(xs, mods, norm_mix_pre[0], norm_mix_post[0], ep)
    xs = moe_layer3(xs, N_ROWS, norm_ffn_pre[0], norm_ffn_post[0], mods, moe_params(0), 6)
    mods = modulation(cvecs, mod_w[1], mod_b[1])[:BATCH + 1].reshape(BATCH + 1, 6, D)
    op = dict(w_in=od_w_in[0], w_out=od_w_out[0], mu=rw_mu[0], w0=rw_w0[0], w_up=rw_w_up[0], a0=rw_a0[0],
              a_up=rw_a_up[0], g_up=rw_g_up[0], k_k=rw_k_k[0], k_a=rw_k_a[0], r_k=rw_r_k[0], ln_w=rw_ln_w[0],
              ln_b=rw_ln_b[0], q_norm=at_q_norm[0], k_norm=at_k_norm[0])
    xl = odd_mixer(xs, mods, norm_mix_pre[1], norm_mix_post[1], op)
    xl = moe_layer3(xl, N_LAT, norm_ffn_pre[1], norm_ffn_post[1], mods, moe_params(1), 8)
    return xl.reshape(BATCH, SEQ, D)
```

```python
import functools
import math

import jax
import jax.numpy as jnp
from jax import lax
from jax.experimental import pallas as pl
from jax.experimental.pallas import tpu as pltpu

F32 = jnp.float32
BF16 = jnp.bfloat16

D = 1024
BATCH = 2
SEQ = 8192
CTX = 256
N_LAT = BATCH * SEQ
N_ROWS = N_LAT + BATCH * CTX
EPS = 1e-6
GRID_W = 64

SSD_HEADS = 16
SSD_P = 64
SSD_G = 2
SSD_S = 128
SSD_Q = 128
HY_W = 1024
HY_EMB = 33
HY_HID = 64

RW_H = 16
RW_N = 64
RW_CHUNK = 128
RW_GN_EPS = 64e-5

AT_KV = 4
AT_HD = 64

N_EXP = 64
TOP_K = 8
N_GRP = 8
TOPK_GRP = 4
EXP_FF = 256
ROUTED_SCALE = 2.5

VMEM_LIMIT = 56 * 1024 * 1024


def _cp(sem, vmem=None):
    return pltpu.CompilerParams(dimension_semantics=sem, vmem_limit_bytes=vmem or VMEM_LIMIT)


def _dot(a, b):
    return jnp.dot(a, b, preferred_element_type=F32)


def _dot_nt(a, b):
    return lax.dot_general(a, b, (((1,), (1,)), ((), ())), preferred_element_type=F32)


def _split(x):
    hi = x.astype(BF16)
    lo = (x - hi.astype(F32)).astype(BF16)
    return hi, lo


def _dot3(a, b):
    ah, al = _split(a)
    bh, bl = _split(b)
    return _dot(ah, bh) + (_dot(ah, bl) + _dot(al, bh))


def _dot2l(a, b):
    ah, al = _split(a)
    return _dot(ah, b) + _dot(al, b)


def _dot2r(a, b):
    bh, bl = _split(b)
    return _dot(a, bh) + _dot(a, bl)


def _silu(x):
    return x * (1.0 / (1.0 + jnp.exp(-x)))


def _sigmoid(x):
    return 1.0 / (1.0 + jnp.exp(-x))


def _softplus(x):
    return jnp.maximum(x, 0.0) + jnp.log(1.0 + jnp.exp(-jnp.abs(x)))


def _seq_of_rowblock(i, tm):
    return jnp.minimum((i * tm) // SEQ, 2)


def _nmm_kernel(x_ref, g_ref, mod_ref, w_ref, o_ref, a_sc, *, shift_i, scale_i):
    @pl.when(pl.program_id(1) == 0)
    def _():
        x = x_ref[...]
        ms = jnp.mean(x * x, axis=-1, keepdims=True)
        y = x * lax.rsqrt(ms + EPS) * g_ref[...]
        h = y * (1.0 + mod_ref[scale_i:scale_i + 1, :]) + mod_ref[shift_i:shift_i + 1, :]
        a_sc[...] = h.astype(BF16)

    o_ref[...] = _dot(a_sc[...], w_ref[...])


def norm_mod_matmul(x, g, mods, w, shift_i, scale_i, tm=512, tn=None, name="nmm"):
    M = x.shape[0]
    N = w.shape[1]
    tn = tn or N
    return pl.pallas_call(
        functools.partial(_nmm_kernel, shift_i=shift_i, scale_i=scale_i),
        out_shape=jax.ShapeDtypeStruct((M, N), F32),
        grid=(M // tm, N // tn),
        in_specs=[pl.BlockSpec((tm, D), lambda i, j: (i, 0)),
                  pl.BlockSpec((1, D), lambda i, j: (0, 0)),
                  pl.BlockSpec((None, 6, D), lambda i, j: (_seq_of_rowblock(i, tm), 0, 0)),
                  pl.BlockSpec((D, tn), lambda i, j: (0, j))],
        out_specs=pl.BlockSpec((tm, tn), lambda i, j: (i, j)),
        scratch_shapes=[pltpu.VMEM((tm, D), BF16)],
        compiler_params=_cp(("parallel", "arbitrary")), name=name)(x, g.reshape(1, D), mods, w)


def _outproj_kernel(a1_ref, a2_ref, w_ref, x_ref, g_ref, mod_ref, o_ref, *, gate_i):
    y = _dot(a1_ref[...].astype(BF16), w_ref[0:D, :]) + _dot(a2_ref[...].astype(BF16), w_ref[D:2 * D, :])
    ms = jnp.mean(y * y, axis=-1, keepdims=True)
    o_ref[...] = x_ref[...] + mod_ref[gate_i:gate_i + 1, :] * (y * lax.rsqrt(ms + EPS) * g_ref[...])


def outproj_residual(a1, a2, w, x, g, mods, gate_i, tm=512, name="outproj"):
    M = a1.shape[0]
    return pl.pallas_call(
        functools.partial(_outproj_kernel, gate_i=gate_i),
        out_shape=jax.ShapeDtypeStruct((M, D), F32),
        grid=(M // tm,),
        in_specs=[pl.BlockSpec((tm, D), lambda i: (i, 0)),
                  pl.BlockSpec((tm, D), lambda i: (i, 0)),
                  pl.BlockSpec((2 * D, D), lambda i: (0, 0)),
                  pl.BlockSpec((tm, D), lambda i: (i, 0)),
                  pl.BlockSpec((1, D), lambda i: (0, 0)),
                  pl.BlockSpec((None, 6, D), lambda i: (_seq_of_rowblock(i, tm), 0, 0))],
        out_specs=pl.BlockSpec((tm, D), lambda i: (i, 0)),
        compiler_params=_cp(("parallel",)), name=name)(a1, a2, w, x, g.reshape(1, D), mods)


def _mod_kernel(c_ref, w_ref, b_ref, o_ref):
    o_ref[...] = _dot3(_silu(c_ref[...]), w_ref[...]) + b_ref[...]


def modulation(cvecs, w, b):
    N = w.shape[1]
    tn = 1024
    return pl.pallas_call(
        _mod_kernel, out_shape=jax.ShapeDtypeStruct((8, N), F32), grid=(N // tn,),
        in_specs=[pl.BlockSpec((8, D), lambda j: (0, 0)),
                  pl.BlockSpec((D, tn), lambda j: (0, j)),
                  pl.BlockSpec((1, tn), lambda j: (0, j))],
        out_specs=pl.BlockSpec((8, tn), lambda j: (0, j)),
        compiler_params=_cp(("parallel",)), name="modulation")(cvecs, w, b.reshape(1, N))


CONV_TM = 256


def _conv3_kernel(x_ref, prev_ref, next_ref, w_ref, b_ref, o_ref, *, act):
    tm = CONV_TM
    row0 = pl.program_id(0) * tm
    seq_len = jnp.where(row0 < N_LAT, SEQ, CTX)
    pos = jnp.where(row0 < N_LAT, row0 % SEQ, (row0 - N_LAT) % CTX)
    cur = x_ref[...]
    rows = lax.broadcasted_iota(jnp.int32, cur.shape, 0)
    prev_row = prev_ref[7:8, :] * (pos > 0).astype(F32)
    next_row = next_ref[0:1, :] * (pos + tm < seq_len).astype(F32)
    xm1 = jnp.where(rows == 0, prev_row, pltpu.roll(cur, 1, 0))
    xp1 = jnp.where(rows == tm - 1, next_row, pltpu.roll(cur, tm - 1, 0))
    y = xm1 * w_ref[0:1, :] + cur * w_ref[1:2, :] + xp1 * w_ref[2:3, :] + b_ref[...]
    o_ref[...] = _silu(y) if act else y


def dwconv3(p, col0, ncols, w, b, act, name, cb=1024):
    tm = CONV_TM
    cb = math.gcd(cb, math.gcd(col0, ncols)) if col0 else math.gcd(cb, ncols)
    r8 = tm // 8
    n8 = N_ROWS // 8
    c0 = col0 // cb
    return pl.pallas_call(
        functools.partial(_conv3_kernel, act=act),
        out_shape=jax.ShapeDtypeStruct((N_ROWS, ncols), F32),
        grid=(N_ROWS // tm, ncols // cb),
        in_specs=[pl.BlockSpec((tm, cb), lambda i, j: (i, c0 + j)),
                  pl.BlockSpec((8, cb), lambda i, j: (jnp.maximum(i * r8 - 1, 0), c0 + j)),
                  pl.BlockSpec((8, cb), lambda i, j: (jnp.minimum((i + 1) * r8, n8 - 1), c0 + j)),
                  pl.BlockSpec((3, cb), lambda i, j: (0, j)),
                  pl.BlockSpec((1, cb), lambda i, j: (0, j))],
        out_specs=pl.BlockSpec((tm, cb), lambda i, j: (i, j)),
        compiler_params=_cp(("parallel", "parallel")), name=name)(p, p, p, jnp.transpose(w), b.reshape(1, ncols))


def _ssd_kernel(xs_ref, bm_ref, cm_ref, dt_ref, dtT_ref, bias_ref, biasT_ref, alog_ref, alogT_ref,
                y_ref, st_ref):
    d = pl.program_id(0)
    c = pl.program_id(2)
    Q = SSD_Q
    HG = SSD_HEADS // SSD_G

    @pl.when(c == 0)
    def _():
        st_ref[...] = jnp.zeros_like(st_ref)

    isb = d == 1
    sgn = 1 - 2 * d
    ii = lax.broadcasted_iota(jnp.int32, (Q, Q), 0)
    jj = lax.broadcasted_iota(jnp.int32, (Q, Q), 1)
    tri = (jj <= ii).astype(BF16)
    triT = (ii <= jj).astype(BF16)
    mask = sgn * (ii - jj) >= 0
    xs = xs_ref[...]
    G = range(SSD_G)
    dt = [_softplus(dt_ref[g] + bias_ref[g]) for g in G]
    dtT = [_softplus(dtT_ref[g] + biasT_ref[g]) for g in G]
    a = [dt[g] * (-jnp.exp(alog_ref[g])) for g in G]
    aT = [dtT[g] * (-jnp.exp(alogT_ref[g])) for g in G]
    cs = [_dot2r(tri, a[g]) for g in G]
    csT = [_dot2l(aT[g], triT) for g in G]
    tot = [cs[g][Q - 1:Q, :] for g in G]
    p = [jnp.where(isb, a[g] - cs[g], cs[g]) for g in G]
    pT = [jnp.where(isb, aT[g] - csT[g], csT[g]) for g in G]
    dec_out = [jnp.exp(jnp.where(isb, tot[g], 0.0) + p[g]) for g in G]
    dec_state = [jnp.exp(jnp.where(isb, 0.0, tot[g]) - p[g]) for g in G]
    chunk_dec = [jnp.exp(tot[g]) for g in G]
    bm = [bm_ref[:, g * SSD_S:(g + 1) * SSD_S].astype(BF16) for g in G]
    cm = [cm_ref[:, g * SSD_S:(g + 1) * SSD_S].astype(BF16) for g in G]
    cb = [_dot_nt(cm[g], bm[g]) for g in G]
    nh = SSD_HEADS

    def spread(cols, width):
        v = jnp.concatenate(cols, axis=1)
        head = lax.broadcasted_iota(jnp.int32, (nh, nh * width), 1) // width
        e = (head == lax.broadcasted_iota(jnp.int32, (nh, nh * width), 0)).astype(BF16)
        h1 = v.astype(BF16)
        r1 = v - h1.astype(F32)
        h2 = r1.astype(BF16)
        h3 = (r1 - h2.astype(F32)).astype(BF16)
        return _dot(h1, e) + (_dot(h2, e) + _dot(h3, e))

    dt_x = spread(dt, SSD_P)
    dout_x = spread(dec_out, SSD_P)
    dst_x = spread(dec_state, SSD_P)
    p_x = spread(p, Q)
    xh_all = xs * dt_x
    xdec_all = (xh_all * dst_x).astype(BF16)
    xh_all = xh_all.astype(BF16)
    GH = [(g, h) for g in G for h in range(HG)]
    NH = range(len(GH))
    lm = [(cb[g] * jnp.exp(jnp.where(mask, p_x[:, n * Q:(n + 1) * Q] - pT[g][h:h + 1, :], -1e30))).astype(BF16)
          for n, (g, h) in enumerate(GH)]
    s_old = [st_ref[n] for n in NH]
    y_in = [_dot(lm[n], xh_all[:, n * SSD_P:(n + 1) * SSD_P]) for n in NH]
    y_st = [_dot(cm[g], s_old[n].astype(BF16)) for n, (g, h) in enumerate(GH)]
    upd = [lax.dot_general(bm[g], xdec_all[:, n * SSD_P:(n + 1) * SSD_P], (((0,), (0,)), ((), ())),
                           preferred_element_type=F32) for n, (g, h) in enumerate(GH)]
    for n, (g, h) in enumerate(GH):
        st_ref[n] = chunk_dec[g][:, h:h + 1] * s_old[n] + upd[n]
    y_ref[...] = jnp.concatenate(y_in, axis=1) + dout_x * jnp.concatenate(y_st, axis=1)


def _ssd_rowblock(d, b, c):
    n_ctx = CTX // SSD_Q
    n_lat = SEQ // SSD_Q
    cc = jnp.where(d == 0, c, n_ctx - 1 - c)
    lc = jnp.where(d == 0, c - n_ctx, n_ctx + n_lat - 1 - c)
    return jnp.where(c < n_ctx, N_LAT // SSD_Q + b * n_ctx + cc, b * n_lat + lc)


def ssd_scan(xbc, dt_raw, dt_bias, a_log):
    HG = SSD_HEADS // SSD_G
    W = SSD_HEADS * SSD_P
    dsel = dt_raw[:, :2 * SSD_HEADS].reshape(N_ROWS, 2, SSD_G, HG).transpose(1, 2, 0, 3)
    dselT = dsel.transpose(0, 1, 3, 2)
    bias = dt_bias.reshape(2, SSD_G, 1, HG)
    biasT = dt_bias.reshape(2, SSD_G, HG, 1)
    alog = a_log.reshape(2, SSD_G, 1, HG)
    alogT = a_log.reshape(2, SSD_G, HG, 1)
    nch = (CTX + SEQ) // SSD_Q
    rb = _ssd_rowblock
    GS = SSD_G * SSD_S
    par = lambda shape: pl.BlockSpec((None,) + shape, lambda d, b, c: (d, 0, 0, 0))
    return pl.pallas_call(
        _ssd_kernel,
        out_shape=jax.ShapeDtypeStruct((2, N_ROWS, W), F32),
        grid=(2, BATCH, nch),
        in_specs=[pl.BlockSpec((SSD_Q, W), lambda d, b, c: (rb(d, b, c), 0)),
                  pl.BlockSpec((SSD_Q, GS), lambda d, b, c: (rb(d, b, c), W // GS)),
                  pl.BlockSpec((SSD_Q, GS), lambda d, b, c: (rb(d, b, c), W // GS + 1)),
                  pl.BlockSpec((None, SSD_G, SSD_Q, HG), lambda d, b, c: (d, 0, rb(d, b, c), 0)),
                  pl.BlockSpec((None, SSD_G, HG, SSD_Q), lambda d, b, c: (d, 0, 0, rb(d, b, c))),
                  par((SSD_G, 1, HG)), par((SSD_G, HG, 1)), par((SSD_G, 1, HG)), par((SSD_G, HG, 1))],
        out_specs=pl.BlockSpec((None, SSD_Q, W), lambda d, b, c: (d, rb(d, b, c), 0)),
        scratch_shapes=[pltpu.VMEM((SSD_HEADS, SSD_S, SSD_P), F32)],
        compiler_params=_cp(("parallel", "parallel", "arbitrary")), name="ssd_scan",
    )(xbc, xbc, xbc, dsel, dselT, bias, biasT, alog, alogT)


def _ssd_out_kernel(yf_ref, yb_ref, xs_ref, z_ref, dskip_ref, nw_ref, o_ref):
    y = yf_ref[...] + yb_ref[...] + xs_ref[...] * dskip_ref[...]
    y = y * _silu(z_ref[...])
    gs = SSD_HEADS * SSD_P // SSD_G
    parts = []
    for g in range(SSD_G):
        yg = y[:, g * gs:(g + 1) * gs]
        parts.append(yg * lax.rsqrt(jnp.mean(yg * yg, axis=-1, keepdims=True) + EPS))
    o_ref[...] = jnp.concatenate(parts, axis=1) * nw_ref[...]


def ssd_output(y2, xbc, p, zcol, d_skip, norm_w, tm=512):
    W = SSD_HEADS * SSD_P
    dexp = jnp.repeat(d_skip, SSD_P).reshape(1, W)
    return pl.pallas_call(
        _ssd_out_kernel, out_shape=jax.ShapeDtypeStruct((N_ROWS, W), F32), grid=(N_ROWS // tm,),
        in_specs=[pl.BlockSpec((None, tm, W), lambda i: (0, i, 0)),
                  pl.BlockSpec((None, tm, W), lambda i: (1, i, 0)),
                  pl.BlockSpec((tm, W), lambda i: (i, 0)),
                  pl.BlockSpec((tm, W), lambda i: (i, zcol // W)),
                  pl.BlockSpec((1, W), lambda i: (0, 0)),
                  pl.BlockSpec((1, W), lambda i: (0, 0))],
        out_specs=pl.BlockSpec((tm, W), lambda i: (i, 0)),
        compiler_params=_cp(("parallel",)), name="ssd_output")(y2, y2, xbc, p, dexp, norm_w.reshape(1, W))


def _hyfilt_kernel(f_ref, w0_ref, b0_ref, fr0_ref, w1_ref, b1_ref, fr1_ref, w2_ref, dl_ref, h_ref, ss_ref, *,
                   n_tiles):
    f = f_ref[...]
    h = jnp.sin(fr0_ref[...] * (_dot3(f, w0_ref[...]) + b0_ref[...]))
    h = jnp.sin(fr1_ref[...] * (_dot3(h, w1_ref[...]) + b1_ref[...]))
    h = _dot3(h, w2_ref[...])
    h = h * jnp.exp(-f[:, 0:1] * dl_ref[...])
    side = pl.program_id(0) // n_tiles
    j = pl.program_id(0) % n_tiles

    @pl.when(j == 0)
    def _():
        ss_ref[...] = jnp.zeros_like(ss_ref)

    ss_ref[...] += jnp.sum(h * h, axis=0, keepdims=True)
    row = lax.broadcasted_iota(jnp.int32, (h.shape[0], 1), 0) + j * h.shape[0]
    h_ref[...] = jnp.where((side == 1) & (row == 0), 0.0, h)


def hyena_filter_taps(L, hp):
    pos = jnp.arange(L, dtype=F32)
    t = pos / (L - 1)
    bands = (HY_EMB - 1) // 2
    freqs = jnp.linspace(1e-4, bands - 1, bands, dtype=F32)
    ang = (2.0 * math.pi / L) * pos[:, None] * freqs[None, :]
    feats = jnp.concatenate([t[:, None], jnp.cos(ang), -jnp.sin(ang)], axis=-1)
    feats = jnp.pad(feats, ((0, 0), (0, 128 - HY_EMB)))
    feats = jnp.concatenate([feats, jnp.flip(feats, axis=0)], axis=0)
    w0 = jnp.pad(hp['hy_mlp_w0'], ((0, 128 - HY_EMB), (0, 0)))
    min_decay = math.log(1e-2) / 1.5
    max_decay = math.log(1e-2) / 0.3
    deltas = jnp.abs(jnp.linspace(min_decay, max_decay, HY_W, dtype=F32))
    dl = jnp.tile(deltas, 2).reshape(1, 2 * HY_W)
    w2 = hp['hy_mlp_w2'].reshape(HY_HID, 2, 2, HY_W).transpose(0, 2, 1, 3).reshape(HY_HID, 4 * HY_W)
    tl = min(L, 512)
    n_tiles = L // tl
    NS = 2 * HY_W
    full = lambda shape: pl.BlockSpec(shape, lambda i: (0, 0))
    return pl.pallas_call(
        functools.partial(_hyfilt_kernel, n_tiles=n_tiles),
        out_shape=(jax.ShapeDtypeStruct((2 * L, NS), F32), jax.ShapeDtypeStruct((1, 2 * NS), F32)),
        grid=(2 * n_tiles,),
        in_specs=[pl.BlockSpec((tl, 128), lambda i: (i, 0)), full((128, HY_HID)), full((1, HY_HID)),
                  full((1, HY_HID)), full((HY_HID, HY_HID)), full((1, HY_HID)), full((1, HY_HID)),
                  pl.BlockSpec((HY_HID, NS), lambda i: (0, i // n_tiles)), full((1, NS))],
        out_specs=(pl.BlockSpec((tl, NS), lambda i: (i, 0)), pl.BlockSpec((1, NS), lambda i: (0, i // n_tiles))),
        compiler_params=_cp(("arbitrary",)), name="hyena_filter",
    )(feats, w0, hp['hy_mlp_b0'].reshape(1, -1), hp['hy_freq0'].reshape(1, -1), hp['hy_mlp_w1'],
      hp['hy_mlp_b1'].reshape(1, -1), hp['hy_freq1'].reshape(1, -1), w2, dl)


def _cis(num, den):
    ang = (2.0 * math.pi / den) * (num % den).astype(F32)
    return jnp.cos(ang), -jnp.sin(ang)


def _fft_consts(NB, BS):
    N = NB * BS
    h = NB // 2
    k1 = jnp.arange(h, dtype=jnp.int32)
    j = jnp.arange(NB, dtype=jnp.int32)
    re, im = _cis(j[None, :] * (2 * k1[:, None] + 1), 2 * NB)
    f1 = jnp.concatenate([re, im], axis=0)
    neg = jnp.where(j >= h, -1.0, 1.0)[None, :]
    f1_data = f1[:, :h]
    f1_filt = f1 * neg
    f1_inv = (2.0 / N) * jnp.concatenate([re[:, :h].T, im[:, :h].T], axis=1)
    r = jnp.arange(BS, dtype=jnp.int32)
    k2 = jnp.arange(BS, dtype=jnp.int32)
    kk = 2 * k1[:, None, None] + 2 * NB * k2[None, :, None] + 1
    gre, gim = _cis(kk * r[None, None, :], 2 * N)
    gf = jnp.concatenate([jnp.concatenate([gre, -gim], axis=2), jnp.concatenate([gim, gre], axis=2)], axis=1)
    gret, gimt = gre.transpose(0, 2, 1), gim.transpose(0, 2, 1)
    gi = jnp.concatenate([jnp.concatenate([gret, gimt], axis=2), jnp.concatenate([-gimt, gret], axis=2)], axis=1)
    return (f1_data.astype(BF16), f1_filt.astype(BF16), f1_inv.astype(BF16), gf.astype(BF16), gi.astype(BF16))


FFT_PAD = 8


FFT_LW = 128


def _fft_fwd_kernel(ua_ref, ub_ref, f1_ref, g_ref, o_ref, t_sc, *, NB, BS, nj, kg):
    pitch = NB + FFT_PAD
    u_refs = (ua_ref, ub_ref)

    @pl.when(pl.program_id(2) == 0)
    def _():
        f1 = f1_ref[...]

        def body(r, carry):
            xr = jnp.concatenate([u[pl.ds(r, nj, stride=BS), :] for u in u_refs], axis=1).astype(BF16)
            res = _dot(f1, xr)
            for hh in range(2):
                t_sc[hh, pl.ds(pl.multiple_of(r * pitch, 8), NB), :] = res[:, hh * FFT_LW:(hh + 1) * FFT_LW]
            return carry

        lax.fori_loop(0, BS, body, 0, unroll=8)

    k0 = pl.program_id(2) * kg
    for i in range(kg):
        are = jnp.concatenate([t_sc[hh, pl.ds(k0 + i, BS, stride=pitch), :] for hh in range(2)], axis=1)
        aim = jnp.concatenate([t_sc[hh, pl.ds(k0 + i + NB // 2, BS, stride=pitch), :] for hh in range(2)], axis=1)
        a = jnp.concatenate([are, aim], axis=0).astype(BF16)
        o_ref[i] = _dot(g_ref[i], a)


def fft_fwd(u, col0, nbatch, nj, f1, gf, NB, BS, kg=8):
    h = NB // 2
    kg = min(kg, h)
    lw = FFT_LW
    ct = 2 * lw
    return pl.pallas_call(
        functools.partial(_fft_fwd_kernel, NB=NB, BS=BS, nj=nj, kg=kg),
        out_shape=jax.ShapeDtypeStruct((nbatch, h, 2 * BS, HY_W), F32),
        grid=(nbatch, HY_W // ct, h // kg),
        in_specs=[pl.BlockSpec((nj * BS, lw), lambda b, c, k: (b, col0 // lw + 2 * c), pipeline_mode=pl.Buffered(1)),
                  pl.BlockSpec((nj * BS, lw), lambda b, c, k: (b, col0 // lw + 2 * c + 1),
                               pipeline_mode=pl.Buffered(1)),
                  pl.BlockSpec((NB, nj), lambda b, c, k: (0, 0)),
                  pl.BlockSpec((kg, 2 * BS, 2 * BS), lambda b, c, k: (k, 0, 0))],
        out_specs=pl.BlockSpec((None, kg, 2 * BS, ct), lambda b, c, k: (b, k, 0, c)),
        scratch_shapes=[pltpu.VMEM((2, BS * (NB + FFT_PAD), lw), F32)],
        compiler_params=_cp(("parallel", "parallel", "arbitrary")), name="hyena_fft_fwd")(u, u, f1, gf)


def _cmul(u, h, half):
    ure, uim = u[:half], u[half:]
    hre, him = h[:half], h[half:]
    return jnp.concatenate([ure * hre - uim * him, ure * him + uim * hre], axis=0)


def _fft_inv_kernel(us_ref, hs_ref, gi_ref, f1i_ref, o_ref, t_sc, y_sc, *, NB, BS, kg):
    ks = pl.program_id(2)
    pitch = 2 * BS + FFT_PAD
    for i in range(kg):
        y = _cmul(us_ref[i], hs_ref[i], BS).astype(BF16)
        row = pl.multiple_of((ks * kg + i) * pitch, 8)
        res = _dot(gi_ref[i], y)
        for hh in range(2):
            t_sc[hh, pl.ds(row, 2 * BS), :] = res[:, hh * FFT_LW:(hh + 1) * FFT_LW]

    @pl.when(ks == pl.num_programs(2) - 1)
    def _():
        f1i = f1i_ref[...]

        def body(r, carry):
            bre = jnp.concatenate([t_sc[hh, pl.ds(r, NB // 2, stride=pitch), :] for hh in range(2)], axis=1)
            bim = jnp.concatenate([t_sc[hh, pl.ds(r + BS, NB // 2, stride=pitch), :] for hh in range(2)], axis=1)
            b = jnp.concatenate([bre, bim], axis=0).astype(BF16)
            res = _dot(f1i, b)
            for hh in range(2):
                y_sc[hh, pl.ds(r, NB // 2, stride=BS), :] = res[:, hh * FFT_LW:(hh + 1) * FFT_LW]
            return carry

        lax.fori_loop(0, BS, body, 0, unroll=8)
        o_ref[...] = jnp.concatenate([y_sc[0], y_sc[1]], axis=1)


def fft_inv(us, hs, gi, f1i, NB, BS, kg=8):
    nbatch, h = us.shape[0], NB // 2
    kg = min(kg, h)
    L = h * BS
    ct = 2 * FFT_LW
    return pl.pallas_call(
        functools.partial(_fft_inv_kernel, NB=NB, BS=BS, kg=kg),
        out_shape=jax.ShapeDtypeStruct((nbatch * L, HY_W), F32),
        grid=(nbatch, HY_W // ct, h // kg),
        in_specs=[pl.BlockSpec((None, kg, 2 * BS, ct), lambda b, c, k: (b, k, 0, c)),
                  pl.BlockSpec((None, kg, 2 * BS, ct), lambda b, c, k: (0, k, 0, c)),
                  pl.BlockSpec((kg, 2 * BS, 2 * BS), lambda b, c, k: (k, 0, 0)),
                  pl.BlockSpec((h, NB), lambda b, c, k: (0, 0))],
        out_specs=pl.BlockSpec((L, ct), lambda b, c, k: (b, c)),
        scratch_shapes=[pltpu.VMEM((2, h * (2 * BS + FFT_PAD), FFT_LW), F32), pltpu.VMEM((2, L, FFT_LW), F32)],
        compiler_params=_cp(("parallel", "parallel", "arbitrary")), name="hyena_fft_inv")(us, hs, gi, f1i)


def _dft_consts(L):
    N = 2 * L
    k = jnp.arange(L, dtype=jnp.int32)
    n = jnp.arange(N, dtype=jnp.int32)
    re, im = _cis(n[None, :] * (2 * k[:, None] + 1), 2 * N)
    f = jnp.concatenate([re, im], axis=0)
    neg = jnp.where(n >= L, -1.0, 1.0)[None, :]
    fi = (2.0 / N) * jnp.concatenate([re[:, :L].T, im[:, :L].T], axis=1)
    return f[:, :L].astype(BF16), (f * neg).astype(BF16), fi.astype(BF16)


def _cdft_kernel(f_ref, x_ref, o_ref):
    o_ref[...] = _dot(f_ref[...], x_ref[...].astype(BF16))


def dft_fwd(x, f, row0, col0, nbatch, ct=256):
    M, K = f.shape
    return pl.pallas_call(
        _cdft_kernel, out_shape=jax.ShapeDtypeStruct((nbatch, M, HY_W), F32),
        grid=(nbatch, HY_W // ct),
        in_specs=[pl.BlockSpec((M, K), lambda b, c: (0, 0)),
                  pl.BlockSpec((K, ct), lambda b, c: (row0 // K + b, col0 // ct + c))],
        out_specs=pl.BlockSpec((None, M, ct), lambda b, c: (b, 0, c)),
        compiler_params=_cp(("parallel", "parallel")), name="hyena_dft_fwd")(f, x)


def _cdft_inv_kernel(us_ref, hs_ref, fi_ref, o_ref):
    half = us_ref.shape[0] // 2
    o_ref[...] = _dot(fi_ref[...], _cmul(us_ref[...], hs_ref[...], half).astype(BF16))


def dft_inv(us, hs, fi, ct=256):
    nbatch, M2, _ = us.shape
    L = fi.shape[0]
    return pl.pallas_call(
        _cdft_inv_kernel, out_shape=jax.ShapeDtypeStruct((nbatch * L, HY_W), F32),
        grid=(nbatch, HY_W // ct),
        in_specs=[pl.BlockSpec((None, M2, ct), lambda b, c: (b, 0, c)),
                  pl.BlockSpec((None, M2, ct), lambda b, c: (0, 0, c)),
                  pl.BlockSpec((L, M2), lambda b, c: (0, 0))],
        out_specs=pl.BlockSpec((L, ct), lambda b, c: (b, c)),
        compiler_params=_cp(("parallel", "parallel")), name="hyena_dft_inv")(us, hs, fi)


def _hy_gate_kernel(g_ref, y_ref, u_ref, ss_ref, b_ref, o_ref):
    scale = lax.rsqrt(ss_ref[0:1, :] + ss_ref[1:2, :] + 1e-6)
    o_ref[...] = g_ref[...] * (y_ref[...] * scale + u_ref[...] * b_ref[...])


def _hy_gate2_kernel(g_ref, yl_ref, yc_ref, ul_ref, uc_ref, ssl_ref, ssc_ref, b_ref, o_ref, *, n_lat_t):
    is_lat = pl.program_id(0) < n_lat_t
    y = jnp.where(is_lat, yl_ref[...], yc_ref[...])
    uin = jnp.where(is_lat, ul_ref[...], uc_ref[...])
    ss = jnp.where(is_lat, ssl_ref[...], ssc_ref[...])
    scale = lax.rsqrt(ss[0:1, :] + ss[1:2, :] + 1e-6)
    o_ref[...] = g_ref[...] * (y * scale + uin * b_ref[...])


def hy_gate(gate, gcol, grow, y, uin, ucol, urow, ss, order, bias, tm=512):
    M = y.shape[0]
    return pl.pallas_call(
        _hy_gate_kernel, out_shape=jax.ShapeDtypeStruct((M, HY_W), F32), grid=(M // tm,),
        in_specs=[pl.BlockSpec((tm, HY_W), lambda i: (grow // tm + i, gcol // HY_W)),
                  pl.BlockSpec((tm, HY_W), lambda i: (i, 0)),
                  pl.BlockSpec((tm, HY_W), lambda i: (urow // tm + i, ucol // HY_W)),
                  pl.BlockSpec((None, 2, HY_W), lambda i: (order, 0, 0)),
                  pl.BlockSpec((None, 1, HY_W), lambda i: (order, 0, 0))],
        out_specs=pl.BlockSpec((tm, HY_W), lambda i: (i, 0)),
        compiler_params=_cp(("parallel",)), name="hyena_gate")(gate, y, uin, ss, bias)


def hyena(u, hp):
    C = HY_W
    bias = hp['hy_bias'].reshape(2, 1, C)
    NB = BS = int(round(math.sqrt(2 * SEQ)))
    f1d, f1f, f1i, gf, gi = _fft_consts(NB, BS)
    taps, ss = hyena_filter_taps(SEQ, hp)
    ss = ss.reshape(2, 2, C).transpose(1, 0, 2)
    conv_l = lambda zin, zcol, order: fft_inv(fft_fwd(zin, zcol, BATCH, NB // 2, f1d, gf, NB, BS),
                                              fft_fwd(taps, order * C, 1, NB, f1f, gf, NB, BS), gi, f1i, NB, BS)
    z1_lat = hy_gate(u, 0, 0, conv_l(u, 2 * C, 0), u, 2 * C, 0, ss, 0, bias)
    y2_lat = conv_l(z1_lat, 0, 1)
    fd, ff, fi = _dft_consts(CTX)
    taps_c, ss_c = hyena_filter_taps(CTX, hp)
    ss_c = ss_c.reshape(2, 2, C).transpose(1, 0, 2)
    conv_c = lambda zin, zrow, zcol, order: dft_inv(dft_fwd(zin, fd, zrow, zcol, BATCH),
                                                    dft_fwd(taps_c, ff, 0, order * C, 1), fi)
    z1_ctx = hy_gate(u, 0, N_LAT, conv_c(u, N_LAT, 2 * C, 0), u, 2 * C, N_LAT, ss_c, 0, bias)
    y2_ctx = conv_c(z1_ctx, 0, 0, 1)
    tm = 512
    n_lat_t = N_LAT // tm
    lat = lambda i: (jnp.minimum(i, n_lat_t - 1), 0)
    ctx = lambda i: (jnp.maximum(i - n_lat_t, 0), 0)
    return pl.pallas_call(
        functools.partial(_hy_gate2_kernel, n_lat_t=n_lat_t),
        out_shape=jax.ShapeDtypeStruct((N_ROWS, C), F32), grid=(N_ROWS // tm,),
        in_specs=[pl.BlockSpec((tm, C), lambda i: (i, 1)),
                  pl.BlockSpec((tm, C), lat), pl.BlockSpec((tm, C), ctx),
                  pl.BlockSpec((tm, C), lat), pl.BlockSpec((tm, C), ctx),
                  pl.BlockSpec((None, 2, C), lambda i: (1, 0, 0)), pl.BlockSpec((None, 2, C), lambda i: (1, 0, 0)),
                  pl.BlockSpec((None, 1, C), lambda i: (1, 0, 0))],
        out_specs=pl.BlockSpec((tm, C), lambda i: (i, 0)),
        compiler_params=_cp(("parallel",)), name="hyena_gate2",
    )(u, y2_lat, y2_ctx, z1_lat, z1_ctx, ss, ss_c, bias)


EV_SSD_IN = SSD_HEADS * SSD_P
EV_XBC = EV_SSD_IN + 2 * SSD_G * SSD_S
EV_PAD_N = 5760


def even_mixer(x, mods, g_pre, g_post, ep):
    o1 = EV_SSD_IN
    o2 = o1 + EV_XBC
    o3 = o2 + 2 * SSD_HEADS
    w = ep['w_in']
    n_in = w.shape[1]
    hw = 3 * HY_W
    w_perm = jnp.concatenate([w[:, o3:], w[:, :o2], w[:, o2:o3],
                              jnp.zeros((D, EV_PAD_N - n_in), F32)], axis=1).astype(BF16)
    p = norm_mod_matmul(x, g_pre, mods, w_perm, 0, 1, tn=1920, name="even_in_proj")
    xbc = dwconv3(p, hw + o1, EV_XBC, ep['ssd_conv_w'], ep['ssd_conv_b'], True, "ssd_conv")
    u = dwconv3(p, 0, hw, ep['hy_conv_w'], ep['hy_conv_b'], False, "hyena_conv")
    dt_raw = p[:, hw + o2:hw + o2 + 2 * SSD_HEADS]
    y2 = ssd_scan(xbc, dt_raw, ep['ssd_dt_bias'], ep['ssd_a_log'])
    s = ssd_output(y2, xbc, p, hw, ep['ssd_d'], ep['ssd_norm_w'])
    zh = hyena(u, ep)
    return outproj_residual(s, zh, ep['w_out'].astype(BF16), x, g_post, mods, 2, name="even_out_proj")


def _head_sum(x, e, et):
    return _dot2l(_dot2l(x, e), et)


def _rw_prep_kernel(r_ref, k_ref, v_ref, lo_ref, w0_ref, wup_ref, a0_ref, aup_ref, gup_ref, kk_ref, ka_ref,
                    rk_ref, e_ref, et_ref, lw_ref, kd_ref, be_ref, kap_ref, g_ref, bonus_ref):
    r, k, v = r_ref[...], k_ref[...], v_ref[...]
    lo = lo_ref[...]
    wc, ac, gc = lo[:, 0:64], lo[:, 64:128], lo[:, 128:384]
    e, et = e_ref[...], et_ref[...]
    kk = k * kk_ref[...]
    kap = kk * lax.rsqrt(_head_sum(kk * kk, e, et) + 1e-12)
    kap_ref[...] = kap
    g_ref[...] = _dot(_sigmoid(gc).astype(BF16), gup_ref[...].astype(BF16))
    kd_sum = jnp.zeros_like(k)
    for d in range(2):
        wlog = -_softplus(-(w0_ref[d:d + 1, :] + _dot3(jnp.tanh(wc), wup_ref[d]))) - 0.5
        lw_ref[d] = -jnp.exp(wlog)
        a = _sigmoid(a0_ref[d:d + 1, :] + _dot(ac.astype(BF16), aup_ref[d].astype(BF16)))
        kd = k * (1.0 + (a - 1.0) * ka_ref[...])
        kd_ref[d] = kd
        be_ref[d] = kap * a
        kd_sum = kd_sum + kd
    bonus_ref[...] = _head_sum(r * kd_sum * rk_ref[...], e, et) * v


def rwkv_prepare(code, lora, op, tm=256):
    W = RW_H * RW_N
    heads = jnp.arange(W, dtype=jnp.int32) // RW_N
    e = (heads[:, None] == jnp.arange(128, dtype=jnp.int32)[None, :]).astype(BF16)
    et = jnp.transpose(e)
    gup = jnp.pad(op['g_up'], ((0, 256 - op['g_up'].shape[0]), (0, 0)))
    row = lambda a: a.reshape(1, W)
    full2 = lambda shape: pl.BlockSpec(shape, lambda i: (0,) * len(shape))
    outs = pl.pallas_call(
        _rw_prep_kernel,
        out_shape=(jax.ShapeDtypeStruct((2, N_ROWS, W), F32), jax.ShapeDtypeStruct((2, N_ROWS, W), F32),
                   jax.ShapeDtypeStruct((2, N_ROWS, W), F32), jax.ShapeDtypeStruct((N_ROWS, W), F32),
                   jax.ShapeDtypeStruct((N_ROWS, W), F32), jax.ShapeDtypeStruct((N_ROWS, W), F32)),
        grid=(N_ROWS // tm,),
        in_specs=[pl.BlockSpec((tm, W), lambda i: (i, 0)), pl.BlockSpec((tm, W), lambda i: (i, 1)),
                  pl.BlockSpec((tm, W), lambda i: (i, 2)), pl.BlockSpec((tm, 384), lambda i: (i, 0)),
                  full2((2, W)), full2((2, 64, W)), full2((2, W)), full2((2, 64, W)), full2((256, W)),
                  full2((1, W)), full2((1, W)), full2((1, W)), full2((W, 128)), full2((128, W))],
        out_specs=(pl.BlockSpec((2, tm, W), lambda i: (0, i, 0)), pl.BlockSpec((2, tm, W), lambda i: (0, i, 0)),
                   pl.BlockSpec((2, tm, W), lambda i: (0, i, 0)), pl.BlockSpec((tm, W), lambda i: (i, 0)),
                   pl.BlockSpec((tm, W), lambda i: (i, 0)), pl.BlockSpec((tm, W), lambda i: (i, 0))),
        compiler_params=_cp(("parallel",)), name="rwkv_prepare",
    )(code, code, code, lora, op['w0'], op['w_up'], op['a0'], op['a_up'], gup, row(op['k_k']), row(op['k_a']),
      row(op['r_k']), e, et)
    return outs


def _rw_scan_kernel(r_ref, v_ref, lw_ref, kd_ref, be_ref, kap_ref, y_ref, st_ref):
    d = pl.program_id(0)
    c = pl.program_id(2)
    C = RW_CHUNK
    N = RW_N

    @pl.when(c == 0)
    def _():
        st_ref[...] = jnp.zeros_like(st_ref)

    isb = d == 1
    sgn = 1 - 2 * d
    ii = lax.broadcasted_iota(jnp.int32, (C, C), 0)
    jj = lax.broadcasted_iota(jnp.int32, (C, C), 1)
    dif = sgn * (ii - jj)
    incl = dif >= 0
    strict = dif > 0
    tri = incl.astype(BF16)
    eye = (ii == jj).astype(F32)
    blk = [(ii >> s) == (jj >> s) for s in range(3, C.bit_length() - 1)]
    masks = [blk[0]] + [blk[l] & ~blk[l - 1] for l in range(1, len(blk))] + [~blk[-1]]
    lw = lw_ref[...]
    cum = _dot2r(tri, lw)
    ec = jnp.exp(cum)
    en = jnp.exp(-cum)
    ea = jnp.exp(cum - lw)
    last = jnp.where(isb, cum[0:1, :], cum[C - 1:C, :])
    el = jnp.exp(last - cum)
    kap = kap_ref[...]
    r = r_ref[...]
    v = v_ref[...]
    a_t = -kap * ea
    r_t = r * ec
    b_t = be_ref[...] * en
    k_t = kd_ref[...] * en
    b_l = be_ref[...] * el
    k_l = kd_ref[...] * el
    pc = jnp.exp(last)
    H = range(RW_H)
    sl = [slice(h * N, (h + 1) * N) for h in H]
    bd = lambda a, b: _dot(a.astype(BF16), b.astype(BF16))
    tn = lambda a, b: lax.dot_general(a, b, (((0,), (0,)), ((), ())), preferred_element_type=F32)
    sc = [_dot_nt(jnp.concatenate([a_t[:, sl[h]], r_t[:, sl[h]]], axis=0).astype(BF16),
                  jnp.concatenate([b_t[:, sl[h]], k_t[:, sl[h]]], axis=0).astype(BF16)) for h in H]
    n_ab = [jnp.where(strict, sc[h][0:C, 0:C], 0.0) for h in H]
    a_ak = [jnp.where(strict, sc[h][0:C, C:2 * C], 0.0).astype(BF16) for h in H]
    m_rb = [jnp.where(incl, sc[h][C:2 * C, 0:C], 0.0).astype(BF16) for h in H]
    m_rk = [jnp.where(incl, sc[h][C:2 * C, C:2 * C], 0.0).astype(BF16) for h in H]
    vh = [v[:, sl[h]].astype(BF16) for h in H]
    d0 = [jnp.where(masks[0], n_ab[h], 0.0) for h in H]
    d2 = [bd(d0[h], d0[h]) for h in H]
    d4 = [bd(d2[h], d2[h]) for h in H]
    t = [bd(eye + d0[h], eye + d2[h]) for h in H]
    t = [bd(t[h], eye + d4[h]) for h in H]
    for m in masks[1:]:
        et = [bd(jnp.where(m, n_ab[h], 0.0), t[h]) for h in H]
        t = [t[h] + bd(t[h], et[h]) for h in H]
    amv = [_dot(jnp.concatenate([a_ak[h], m_rk[h]], axis=0), vh[h]) for h in H]
    wub = [bd(t[h], jnp.concatenate([a_t[:, sl[h]], amv[h][0:C]], axis=1)).astype(BF16) for h in H]
    kv = [tn(k_l[:, sl[h]].astype(BF16), vh[h]) for h in H]
    qy = [_dot(m_rb[h], wub[h]) + jnp.concatenate([r_t[:, sl[h]], amv[h][C:2 * C]], axis=1) for h in H]
    pp = [tn(b_l[:, sl[h]].astype(BF16), wub[h]) + jnp.concatenate([eye[0:N, 0:N] * pc[:, sl[h]], kv[h]], axis=1)
          for h in H]
    h_old = [st_ref[h] for h in H]
    ys = [bd(qy[h][:, 0:N], h_old[h]) + qy[h][:, N:2 * N] for h in H]
    for h in H:
        st_ref[h] = _dot3(pp[h][:, 0:N], h_old[h]) + pp[h][:, N:2 * N]
    y_ref[...] = jnp.concatenate(ys, axis=1)


def _rw_rowblock(d, b, c):
    n_ctx = CTX // RW_CHUNK
    n_lat = SEQ // RW_CHUNK
    cc = jnp.where(d == 0, c, n_ctx - 1 - c)
    lc = jnp.where(d == 0, c - n_ctx, n_ctx + n_lat - 1 - c)
    return jnp.where(c < n_ctx, N_LAT // RW_CHUNK + b * n_ctx + cc, b * n_lat + lc)


def rwkv_scan(code, lw, kd, be, kap):
    W = RW_H * RW_N
    nch = (CTX + SEQ) // RW_CHUNK
    rb = lambda d, b, c: _rw_rowblock(d, b, c)
    return pl.pallas_call(
        _rw_scan_kernel,
        out_shape=jax.ShapeDtypeStruct((2, N_ROWS, W), F32),
        grid=(2, BATCH, nch),
        in_specs=[pl.BlockSpec((RW_CHUNK, W), lambda d, b, c: (rb(d, b, c), 0)),
                  pl.BlockSpec((RW_CHUNK, W), lambda d, b, c: (rb(d, b, c), 2)),
                  pl.BlockSpec((None, RW_CHUNK, W), lambda d, b, c: (d, rb(d, b, c), 0)),
                  pl.BlockSpec((None, RW_CHUNK, W), lambda d, b, c: (d, rb(d, b, c), 0)),
                  pl.BlockSpec((None, RW_CHUNK, W), lambda d, b, c: (d, rb(d, b, c), 0)),
                  pl.BlockSpec((RW_CHUNK, W), lambda d, b, c: (rb(d, b, c), 0))],
        out_specs=pl.BlockSpec((None, RW_CHUNK, W), lambda d, b, c: (d, rb(d, b, c), 0)),
        scratch_shapes=[pltpu.VMEM((RW_H, RW_N, RW_N), F32)],
        compiler_params=_cp(("parallel", "parallel", "arbitrary")), name="rwkv_scan",
    )(code, code, lw, kd, be, kap)


def _rw_out_kernel(yf_ref, yb_ref, bonus_ref, g_ref, lnw_ref, lnb_ref, e_ref, et_ref, o_ref):
    e, et = e_ref[...], et_ref[...]
    y = yf_ref[...] + yb_ref[...]
    mean = _head_sum(y, e, et) * (1.0 / RW_N)
    yc = y - mean
    var = _head_sum(yc * yc, e, et) * (1.0 / RW_N)
    yn = yc * lax.rsqrt(var + RW_GN_EPS) * lnw_ref[...] + lnb_ref[...]
    o_ref[...] = (yn + bonus_ref[...]) * g_ref[...]


def rwkv_output(y2, bonus, g, op, tm=512):
    W = RW_H * RW_N
    heads = jnp.arange(W, dtype=jnp.int32) // RW_N
    e = (heads[:, None] == jnp.arange(128, dtype=jnp.int32)[None, :]).astype(BF16)
    et = jnp.transpose(e)
    M = N_LAT
    return pl.pallas_call(
        _rw_out_kernel, out_shape=jax.ShapeDtypeStruct((M, W), F32), grid=(M // tm,),
        in_specs=[pl.BlockSpec((None, tm, W), lambda i: (0, i, 0)), pl.BlockSpec((None, tm, W), lambda i: (1, i, 0)),
                  pl.BlockSpec((tm, W), lambda i: (i, 0)), pl.BlockSpec((tm, W), lambda i: (i, 0)),
                  pl.BlockSpec((1, W), lambda i: (0, 0)), pl.BlockSpec((1, W), lambda i: (0, 0)),
                  pl.BlockSpec((W, 128), lambda i: (0, 0)), pl.BlockSpec((128, W), lambda i: (0, 0))],
        out_specs=pl.BlockSpec((tm, W), lambda i: (i, 0)),
        compiler_params=_cp(("parallel",)), name="rwkv_output",
    )(y2, y2, bonus, g, op['ln_w'].reshape(1, W), op['ln_b'].reshape(1, W), e, et)


AT_Q = RW_H * AT_HD
AT_KW = AT_KV * AT_HD
AT_TQ = 1024
AT_TK = 768


def _rope_tables(tm):
    half = AT_HD // 2
    inv = 10000.0 ** (-jnp.arange(0, half, 2, dtype=F32) / half)
    pos = jnp.arange(SEQ, dtype=jnp.int32)
    row = (pos // GRID_W).astype(F32)[:, None] * inv
    col = (pos % GRID_W).astype(F32)[:, None] * inv
    cos_h = jnp.concatenate([jnp.cos(row), jnp.cos(row), jnp.cos(col), jnp.cos(col)], axis=1)
    sin_h = jnp.concatenate([-jnp.sin(row), jnp.sin(row), -jnp.sin(col), jnp.sin(col)], axis=1)
    cos_t = jnp.concatenate([jnp.tile(cos_h, (1, 2)), jnp.ones((tm, 128), F32)], axis=0)
    sin_t = jnp.concatenate([jnp.tile(sin_h, (1, 2)), jnp.zeros((tm, 128), F32)], axis=0)
    return cos_t, sin_t


def _rot_partner(x):
    q = AT_HD // 4
    w = x.shape[1]
    lane = lax.broadcasted_iota(jnp.int32, x.shape, 1)
    return jnp.where((lane % (2 * q)) < q, pltpu.roll(x, w - q, 1), pltpu.roll(x, q, 1))


def _at_prep_kernel(q_ref, k_ref, v_ref, cos_ref, sin_ref, qn_ref, kn_ref, e_ref, et_ref, qo_ref, ko_ref, vo_ref):
    e, et = e_ref[...], et_ref[...]
    cos2, sin2 = cos_ref[...], sin_ref[...]

    def norm_rope(x, gain, nrep):
        ms = _head_sum(x * x, e[:x.shape[1]], et[:, :x.shape[1]]) * (1.0 / AT_HD)
        xn = x * lax.rsqrt(ms + EPS) * gain
        cos = jnp.tile(cos2, (1, nrep))
        sin = jnp.tile(sin2, (1, nrep))
        return xn * cos + _rot_partner(xn) * sin

    qn = norm_rope(q_ref[...], qn_ref[...], AT_Q // 128) * (AT_HD ** -0.5 * math.log2(math.e))
    qo_ref[...] = jnp.transpose(qn).astype(BF16)
    ko_ref[...] = norm_rope(k_ref[...], kn_ref[...], AT_KW // 128).astype(BF16)
    vo_ref[...] = jnp.transpose(v_ref[...]).astype(BF16)


def attention_prepare(p, q_norm, k_norm, tm=256):
    cos_t, sin_t = _rope_tables(tm)
    heads = jnp.arange(AT_Q, dtype=jnp.int32) // AT_HD
    e = (heads[:, None] == jnp.arange(128, dtype=jnp.int32)[None, :]).astype(BF16)
    et = jnp.transpose(e)
    tab = lambda i: jnp.where(i * tm < N_LAT, ((i * tm) % SEQ) // tm, SEQ // tm)
    n_lat_t, n_seq_t, n_ctx_t = N_LAT // tm, SEQ // tm, CTX // tm
    kvb = lambda i: jnp.where(i < n_lat_t, (i // n_seq_t) * (n_seq_t + n_ctx_t) + n_ctx_t + i % n_seq_t,
                              ((i - n_lat_t) // n_ctx_t) * (n_seq_t + n_ctx_t) + (i - n_lat_t) % n_ctx_t)
    qcol = (3 * RW_H * RW_N) // AT_Q
    kcol = (3 * RW_H * RW_N + AT_Q) // AT_KW
    return pl.pallas_call(
        _at_prep_kernel,
        out_shape=(jax.ShapeDtypeStruct((AT_Q, N_ROWS), BF16), jax.ShapeDtypeStruct((N_ROWS, AT_KW), BF16),
                   jax.ShapeDtypeStruct((AT_KW, N_ROWS), BF16)),
        grid=(N_ROWS // tm,),
        in_specs=[pl.BlockSpec((tm, AT_Q), lambda i: (i, qcol)),
                  pl.BlockSpec((tm, AT_KW), lambda i: (i, kcol)),
                  pl.BlockSpec((tm, AT_KW), lambda i: (i, kcol + 1)),
                  pl.BlockSpec((tm, 128), lambda i: (tab(i), 0)),
                  pl.BlockSpec((tm, 128), lambda i: (tab(i), 0)),
                  pl.BlockSpec((1, AT_Q), lambda i: (0, 0)),
                  pl.BlockSpec((1, AT_KW), lambda i: (0, 0)),
                  pl.BlockSpec((AT_Q, 128), lambda i: (0, 0)),
                  pl.BlockSpec((128, AT_Q), lambda i: (0, 0))],
        out_specs=(pl.BlockSpec((AT_Q, tm), lambda i: (0, i)), pl.BlockSpec((tm, AT_KW), lambda i: (kvb(i), 0)),
                   pl.BlockSpec((AT_KW, tm), lambda i: (0, kvb(i)))),
        compiler_params=_cp(("parallel",)), name="attn_prepare",
    )(p, p, p, cos_t, sin_t, jnp.tile(q_norm, AT_Q // AT_HD).reshape(1, AT_Q),
      jnp.tile(k_norm, AT_KV).reshape(1, AT_KW), e, et)


AT_REBASE = 64.0
AT_SEED = 128


def _flash_t_kernel(qt_ref, k_ref, vt_ref, o_ref, m_sc, l_sc, acc_sc, p_sc):
    ki = pl.program_id(2)
    nq = AT_Q // AT_HD
    gq = nq // AT_KV

    @pl.when(ki == 0)
    def _():
        for g in range(AT_KV):
            k0 = k_ref[0:AT_SEED, g * AT_HD:(g + 1) * AT_HD]
            for h in range(g * gq, (g + 1) * gq):
                m_sc[h] = jnp.max(_dot(k0, qt_ref[h * AT_HD:(h + 1) * AT_HD, :]), axis=0, keepdims=True)
        l_sc[...] = jnp.zeros_like(l_sc)
        acc_sc[...] = jnp.zeros_like(acc_sc)

    def k_group(g):
        return k_ref[:, g * AT_HD:(g + 1) * AT_HD]

    def v_group(g):
        return jnp.concatenate([vt_ref[g * AT_HD:(g + 1) * AT_HD, :],
                                jnp.ones((16, vt_ref.shape[1]), BF16)], axis=0)

    def scores(g):
        kg = k_group(g)
        return [_dot(kg, qt_ref[h * AT_HD:(h + 1) * AT_HD, :]) for h in range(g * gq, (g + 1) * gq)]

    gap = None
    for g in range(AT_KV):
        st = scores(g)
        m_cur = [m_sc[h] for h in range(g * gq, (g + 1) * gq)]
        for i in range(gq):
            over_i = jnp.max(st[i], axis=0, keepdims=True) - m_cur[i]
            gap = over_i if gap is None else jnp.maximum(gap, over_i)
            p_sc[g * gq + i] = jnp.exp2(st[i] - m_cur[i]).astype(BF16)
    rebase = jnp.max(gap) > AT_REBASE

    @pl.when(jnp.logical_not(rebase))
    def _():
        for g in range(AT_KV):
            vtg1 = v_group(g)
            hs = range(g * gq, (g + 1) * gq)
            pv = [_dot(vtg1, p_sc[h]) for h in hs]
            for i, h in enumerate(hs):
                l_sc[h] = l_sc[h] + pv[i][AT_HD:AT_HD + 1, :]
                rows = pl.ds(h * AT_HD, AT_HD)
                acc_sc[rows, :] = acc_sc[rows, :] + pv[i][0:AT_HD, :]

    @pl.when(rebase)
    def _():
        for g in range(AT_KV):
            vtg1 = v_group(g)
            hs = range(g * gq, (g + 1) * gq)
            st = scores(g)
            m_old = [m_sc[h] for h in hs]
            m_new = [jnp.maximum(m_old[i], jnp.max(st[i], axis=0, keepdims=True)) for i in range(gq)]
            alpha = [jnp.exp2(m_old[i] - m_new[i]) for i in range(gq)]
            pt = [jnp.exp2(st[i] - m_new[i]).astype(BF16) for i in range(gq)]
            pv = [_dot(vtg1, pt[i]) for i in range(gq)]
            for i, h in enumerate(hs):
                l_sc[h] = alpha[i] * l_sc[h] + pv[i][AT_HD:AT_HD + 1, :]
                m_sc[h] = m_new[i]
                rows = pl.ds(h * AT_HD, AT_HD)
                acc_sc[rows, :] = alpha[i] * acc_sc[rows, :] + pv[i][0:AT_HD, :]

    @pl.when(ki == pl.num_programs(2) - 1)
    def _():
        inv = jnp.concatenate([jnp.broadcast_to(1.0 / l_sc[h], (AT_HD, l_sc.shape[2])) for h in range(nq)], axis=0)
        o_ref[...] = jnp.transpose(acc_sc[...] * inv)


def flash_attention_t(qt, k, vt):
    nq = AT_Q // AT_HD
    assert (CTX + SEQ) % AT_TK == 0 and SEQ % AT_TQ == 0
    nk = (CTX + SEQ) // AT_TK
    kv_rb = lambda b, ki: b * nk + ki
    return pl.pallas_call(
        _flash_t_kernel,
        out_shape=jax.ShapeDtypeStruct((N_LAT, AT_Q), F32),
        grid=(BATCH, SEQ // AT_TQ, nk),
        in_specs=[pl.BlockSpec((AT_Q, AT_TQ), lambda b, qi, ki: (0, b * (SEQ // AT_TQ) + qi)),
                  pl.BlockSpec((AT_TK, AT_KW), lambda b, qi, ki: (kv_rb(b, ki), 0)),
                  pl.BlockSpec((AT_KW, AT_TK), lambda b, qi, ki: (0, kv_rb(b, ki)))],
        out_specs=pl.BlockSpec((AT_TQ, AT_Q), lambda b, qi, ki: (b * (SEQ // AT_TQ) + qi, 0)),
        scratch_shapes=[pltpu.VMEM((nq, 1, AT_TQ), F32), pltpu.VMEM((nq, 1, AT_TQ), F32),
                        pltpu.VMEM((AT_Q, AT_TQ), F32), pltpu.VMEM((nq, AT_TK, AT_TQ), BF16)],
        compiler_params=_cp(("parallel", "parallel", "arbitrary")), name="flash_attention")(qt, k, vt)


OD_PAD_N = 4992


def odd_mixer(x, mods, g_pre, g_post, op):
    W = RW_H * RW_N
    w = op['w_in']
    c3 = 3 * W
    code_n = c3 + 64 + 64 + 160
    w_perm = jnp.concatenate([w[:, :c3], w[:, code_n:], w[:, c3:code_n],
                              jnp.zeros((D, OD_PAD_N - w.shape[1]), F32)], axis=1).astype(BF16)
    p = norm_mod_matmul(x, g_pre, mods, w_perm, 0, 1, tn=1664, name="odd_in_proj")
    mu = op['mu']
    taps = lambda m: jnp.stack([0.5 * m, 1.0 - m, 0.5 * m], axis=1)
    code = dwconv3(p, 0, c3, taps(mu[:c3]), jnp.zeros((c3,), F32), False, "rwkv_shift")
    lo_col = c3 + AT_Q + 2 * AT_KW
    mu_lo = jnp.pad(mu[c3:], (0, 384 - (code_n - c3)))
    lora = dwconv3(p, lo_col, 384, taps(mu_lo), jnp.zeros((384,), F32), False, "rwkv_shift_lora", cb=384)
    lw, kd, be, kap, g, bonus = rwkv_prepare(code, lora, op)
    y2 = rwkv_scan(code, lw, kd, be, kap)
    o_l = rwkv_output(y2, bonus, g, op)
    q, k, v = attention_prepare(p, op['q_norm'], op['k_norm'])
    a_l = flash_attention_t(q, k, v)
    return outproj_residual(o_l, a_l, op['w_out'].astype(BF16), x, g_post, mods, 2, name="odd_out_proj")


MOE_T = 256
MOE_CAP = 64
MOE_EPS = 4


def _router2_kernel(x_ref, g_ref, mod_ref, rw_ref, rb_ref, s1_ref, s3_ref, s2_ref, t_ref, wt_ref, cnt_ref, sh_ref):
    x = x_ref[...]
    ms = jnp.mean(x * x, axis=-1, keepdims=True)
    t = x * lax.rsqrt(ms + EPS) * g_ref[...] * (1.0 + mod_ref[4:5, :]) + mod_ref[3:4, :]
    tb = t.astype(BF16)
    t_ref[...] = tb
    sh_ref[...] = _dot((_silu(_dot(tb, s1_ref[...])) * _dot(tb, s3_ref[...])).astype(BF16), s2_ref[...])
    th, tl = _split(t)
    wh, wl = _split(rw_ref[...])
    lg = _dot_nt(wh, th) + (_dot_nt(wh, tl) + _dot_nt(wl, th))
    sc = _sigmoid(lg)
    sel = sc + rb_ref[...]
    tm = sel.shape[1]
    gsz = N_EXP // N_GRP
    ninf = -jnp.inf
    sel3 = sel.reshape(N_GRP, gsz, tm)
    i3 = lax.broadcasted_iota(jnp.int32, sel3.shape, 1)
    m1 = jnp.max(sel3, axis=1, keepdims=True)
    first = jnp.min(jnp.where(sel3 == m1, i3, gsz), axis=1, keepdims=True)
    m2 = jnp.max(jnp.where(i3 == first, ninf, sel3), axis=1, keepdims=True)
    grp = (m1 + m2).reshape(N_GRP, tm)
    gi = lax.broadcasted_iota(jnp.int32, grp.shape, 0)
    gmask = jnp.zeros(grp.shape, F32)
    for _ in range(TOPK_GRP):
        m = jnp.max(grp, axis=0, keepdims=True)
        pick = jnp.min(jnp.where(grp == m, gi, N_GRP), axis=0, keepdims=True)
        hit = gi == pick
        gmask = jnp.where(hit, 1.0, gmask)
        grp = jnp.where(hit, ninf, grp)
    emask = jnp.broadcast_to(gmask.reshape(N_GRP, 1, tm), (N_GRP, gsz, tm)).reshape(N_EXP, tm)
    msel = jnp.where(emask > 0.5, sel, ninf)
    ei = lax.broadcasted_iota(jnp.int32, msel.shape, 0)
    chosen = jnp.zeros(msel.shape, F32)
    for _ in range(TOP_K):
        m = jnp.max(msel, axis=0, keepdims=True)
        pick = jnp.min(jnp.where(msel == m, ei, N_EXP), axis=0, keepdims=True)
        hit = ei == pick
        chosen = jnp.where(hit, 1.0, chosen)
        msel = jnp.where(hit, ninf, msel)
    w = chosen * sc
    wt = w / jnp.sum(w, axis=0, keepdims=True) * ROUTED_SCALE
    for s in range(tm // MOE_T):
        ws = wt[:, s * MOE_T:(s + 1) * MOE_T]
        wt_ref[s] = ws
        cnt_ref[s] = jnp.sum((ws > 0.0).astype(F32), axis=1, keepdims=True).astype(jnp.int32)


def moe_router2(x, M, g, mods, mp):
    tm = 2 * MOE_T
    ns = tm // MOE_T
    full = lambda shape: pl.BlockSpec(shape, lambda i: (0,) * len(shape))
    return pl.pallas_call(
        _router2_kernel,
        out_shape=(jax.ShapeDtypeStruct((M, D), BF16), jax.ShapeDtypeStruct((M // MOE_T, N_EXP, MOE_T), F32),
                   jax.ShapeDtypeStruct((M // MOE_T, N_EXP, 1), jnp.int32), jax.ShapeDtypeStruct((M, D), F32)),
        grid=(M // tm,),
        in_specs=[pl.BlockSpec((tm, D), lambda i: (i, 0)), full((1, D)),
                  pl.BlockSpec((None, 6, D), lambda i: (_seq_of_rowblock(i, tm), 0, 0)),
                  full((N_EXP, D)), full((N_EXP, 1)), full((D, EXP_FF)), full((D, EXP_FF)), full((EXP_FF, D))],
        out_specs=(pl.BlockSpec((tm, D), lambda i: (i, 0)), pl.BlockSpec((ns, N_EXP, MOE_T), lambda i: (i, 0, 0)),
                   pl.BlockSpec((ns, N_EXP, 1), lambda i: (i, 0, 0)), pl.BlockSpec((tm, D), lambda i: (i, 0))),
        compiler_params=_cp(("parallel",)), name="moe_router",
    )(x, g.reshape(1, D), mods, jnp.transpose(mp['router_w']), mp['router_bias'].reshape(N_EXP, 1),
      mp['s1'].astype(BF16), mp['s3'].astype(BF16), mp['s2'].astype(BF16))


def _moe3_kernel(cnt_ref, ovf_ref, t_ref, wt_ref, w1_ref, w3_ref, w2_ref, o_ref, rank_sc, *, nsub):
    i = pl.program_id(0)
    eb = pl.program_id(1)
    T, CAP, EPS = MOE_T, MOE_CAP, MOE_EPS

    @pl.when(eb == 0)
    def _():
        before = (lax.broadcasted_iota(jnp.int32, (T, T), 0) < lax.broadcasted_iota(jnp.int32, (T, T), 1))
        before = before.astype(BF16)
        for s in range(nsub):
            rank_sc[s] = _dot((wt_ref[s] > 0.0).astype(BF16), before)
        o_ref[...] = jnp.zeros_like(o_ref)

    slot = lax.broadcasted_iota(jnp.int32, (CAP, T), 0).astype(F32)

    def one_hot(s, e, first_slot):
        w_row = wt_ref[s, pl.ds(e, 1), :]
        r_row = rank_sc[s, pl.ds(e, 1), :]
        hit = ((r_row - first_slot) == slot) & (w_row > 0.0)
        w_slot = jnp.sum(jnp.where(hit, w_row, 0.0), axis=1, keepdims=True)
        return hit.astype(F32).astype(BF16), w_slot

    def swiglu(xg, j):
        h = _silu(_dot(xg, w1_ref[j])) * _dot(xg, w3_ref[j])
        return _dot(h.astype(BF16), w2_ref[j])

    hot = [[one_hot(s, eb * EPS + j, 0.0) for j in range(EPS)] for s in range(nsub)]
    pb = [jnp.concatenate([hot[s][j][0] for j in range(EPS)], axis=0) for s in range(nsub)]
    xg = [_dot(pb[s], t_ref[s * T:(s + 1) * T, :]).astype(BF16) for s in range(nsub)]
    y = [swiglu(jnp.concatenate([xg[s][j * CAP:(j + 1) * CAP] for s in range(nsub)], axis=0), j)
         for j in range(EPS)]
    for s in range(nsub):
        yw = jnp.concatenate([y[j][s * CAP:(s + 1) * CAP] * hot[s][j][1] for j in range(EPS)], axis=0)
        o_ref[s * T:(s + 1) * T, :] += lax.dot_general(pb[s], yw.astype(BF16), (((0,), (0,)), ((), ())),
                                                       preferred_element_type=F32)

    def pair(idx, carry):
        s = idx // EPS
        j = idx % EPS
        e = eb * EPS + j
        n_tok = cnt_ref[(i * nsub + s) * N_EXP + e]
        rows = pl.ds(pl.multiple_of(s * T, T), T)

        def chunk(ci, c2):
            p1, w_slot = one_hot(s, e, (ci * CAP).astype(F32))
            yw = (swiglu(_dot(p1, t_ref[rows, :]).astype(BF16), j) * w_slot).astype(BF16)
            o_ref[rows, :] += lax.dot_general(p1, yw, (((0,), (0,)), ((), ())), preferred_element_type=F32)
            return c2

        lax.fori_loop(1, (n_tok + CAP - 1) // CAP, chunk, 0)
        return carry

    @pl.when(ovf_ref[i * (N_EXP // EPS) + eb] > 0)
    def _():
        lax.fori_loop(0, nsub * EPS, pair, 0)


def _moe_out_kernel(r_ref, sh_ref, x_ref, g_ref, mod_ref, o_ref):
    f = r_ref[...] + sh_ref[...]
    ms = jnp.mean(f * f, axis=-1, keepdims=True)
    o_ref[...] = x_ref[...] + mod_ref[5:6, :] * (f * lax.rsqrt(ms + EPS) * g_ref[...])


def moe_layer3(x, M, g_pre, g_post, mods, mp, nsub):
    t, wt, cnt, sh = moe_router2(x, M, g_pre, mods, mp)
    T = MOE_T
    TS = nsub * T
    ovf = jnp.any(cnt.reshape(M // TS, nsub, N_EXP // MOE_EPS, MOE_EPS) > MOE_CAP, axis=(1, 3)).astype(jnp.int32)
    grid_spec = pltpu.PrefetchScalarGridSpec(
        num_scalar_prefetch=2, grid=(M // TS, N_EXP // MOE_EPS),
        in_specs=[pl.BlockSpec((TS, D), lambda i, e, c, o: (i, 0)),
                  pl.BlockSpec((nsub, N_EXP, T), lambda i, e, c, o: (i, 0, 0)),
                  pl.BlockSpec((MOE_EPS, D, EXP_FF), lambda i, e, c, o: (e, 0, 0)),
                  pl.BlockSpec((MOE_EPS, D, EXP_FF), lambda i, e, c, o: (e, 0, 0)),
                  pl.BlockSpec((MOE_EPS, EXP_FF, D), lambda i, e, c, o: (e, 0, 0))],
        out_specs=pl.BlockSpec((TS, D), lambda i, e, c, o: (i, 0)),
        scratch_shapes=[pltpu.VMEM((nsub, N_EXP, T), F32)])
    routed = pl.pallas_call(
        functools.partial(_moe3_kernel, nsub=nsub), out_shape=jax.ShapeDtypeStruct((M, D), F32),
        grid_spec=grid_spec, compiler_params=_cp(("parallel", "arbitrary")), name="moe_experts",
    )(cnt.reshape(-1), ovf.reshape(-1), t, wt, mp['w1'].astype(BF16), mp['w3'].astype(BF16),
      mp['w2'].astype(BF16))
    tm = 512
    return pl.pallas_call(
        _moe_out_kernel, out_shape=jax.ShapeDtypeStruct((M, D), F32), grid=(M // tm,),
        in_specs=[pl.BlockSpec((tm, D), lambda i: (i, 0)), pl.BlockSpec((tm, D), lambda i: (i, 0)),
                  pl.BlockSpec((tm, D), lambda i: (i, 0)), pl.BlockSpec((1, D), lambda i: (0, 0)),
                  pl.BlockSpec((None, 6, D), lambda i: (_seq_of_rowblock(i, tm), 0, 0))],
        out_specs=pl.BlockSpec((tm, D), lambda i: (i, 0)),
        compiler_params=_cp(("parallel",)), name="moe_output")(routed, sh, x, g_post.reshape(1, D), mods)


def kernel(x, c, ctx, c_ctx, mod_w, mod_b, norm_mix_pre, norm_mix_post, norm_ffn_pre, norm_ffn_post, router_w, router_bias, expert_w1, expert_w3, expert_w2, shared_w1, shared_w3, shared_w2, ev_w_in, ev_w_out, ssd_conv_w, ssd_conv_b, ssd_dt_bias, ssd_a_log, ssd_d, ssd_norm_w, hy_conv_w, hy_conv_b, hy_mlp_w0, hy_mlp_b0, hy_freq0, hy_mlp_w1, hy_mlp_b1, hy_freq1, hy_mlp_w2, hy_bias, od_w_in, od_w_out, rw_mu, rw_w0, rw_w_up, rw_a0, rw_a_up, rw_g_up, rw_k_k, rw_k_a, rw_r_k, rw_ln_w, rw_ln_b, at_q_norm, at_k_norm):
    xs = jnp.concatenate([x.reshape(N_LAT, D), ctx.reshape(BATCH * CTX, D)], axis=0)
    cvecs = jnp.zeros((8, D), F32).at[0:BATCH].set(c).at[BATCH].set(c_ctx)
    assert mod_w.shape[0] == 2, "one even (SSD | Hyena) layer followed by one odd (RWKV | attention) layer"

    def moe_params(i):
        return dict(router_w=router_w[i], router_bias=router_bias[i], w1=expert_w1[i], w3=expert_w3[i],
                    w2=expert_w2[i], s1=shared_w1[i], s3=shared_w3[i], s2=shared_w2[i])

    mods = modulation(cvecs, mod_w[0], mod_b[0])[:BATCH + 1].reshape(BATCH + 1, 6, D)
    ep = dict(w_in=ev_w_in[0], w_out=ev_w_out[0], ssd_conv_w=ssd_conv_w[0], ssd_conv_b=ssd_conv_b[0],
              ssd_dt_bias=ssd_dt_bias[0], ssd_a_log=ssd_a_log[0], ssd_d=ssd_d[0], ssd_norm_w=ssd_norm_w[0],
              hy_conv_w=hy_conv_w[0], hy_conv_b=hy_conv_b[0], hy_mlp_w0=hy_mlp_w0[0], hy_mlp_b0=hy_mlp_b0[0],
              hy_freq0=hy_freq0[0], hy_mlp_w1=hy_mlp_w1[0], hy_mlp_b1=hy_mlp_b1[0], hy_freq1=hy_freq1[0],
              hy_mlp_w2=hy_mlp_w2[0], hy_bias=hy_bias[0])
    xs = even_mixer(xs, mods, norm_mix_pre[0], norm_mix_post[0], ep)
    xs = moe_layer3(xs, N_ROWS, norm_ffn_pre[0], norm_ffn_post[0], mods, moe_params(0), 6)
    mods = modulation(cvecs, mod_w[1], mod_b[1])[:BATCH + 1].reshape(BATCH + 1, 6, D)
    op = dict(w_in=od_w_in[0], w_out=od_w_out[0], mu=rw_mu[0], w0=rw_w0[0], w_up=rw_w_up[0], a0=rw_a0[0],
              a_up=rw_a_up[0], g_up=rw_g_up[0], k_k=rw_k_k[0], k_a=rw_k_a[0], r_k=rw_r_k[0], ln_w=rw_ln_w[0],
              ln_b=rw_ln_b[0], q_norm=at_q_norm[0], k_norm=at_k_norm[0])
    xl = odd_mixer(xs, mods, norm_mix_pre[1], norm_mix_post[1], op)
    xl = moe_layer3(xl, N_LAT, norm_ffn_pre[1], norm_ffn_post[1], mods, moe_params(1), 8)
    return xl.reshape(BATCH, SEQ, D)
```
